```python
import math
import jax, jax.numpy as jnp
from jax import lax
import numpy as np

D_MODEL = 2048
BATCH = 8
SEQ = 4096
DEPTH = 2

N_A_LAYERS = DEPTH // 2
N_B_LAYERS = DEPTH - N_A_LAYERS
GLA_HEADS = 4
GLA_KEY_DIM = D_MODEL // 2
GLA_VAL_DIM = D_MODEL
GLA_DK = GLA_KEY_DIM // GLA_HEADS
GLA_DV = GLA_VAL_DIM // GLA_HEADS
GATE_RANK = 16
GATE_NORMALIZER = 16.0
GLA_CHUNK = 64
GLA_IN_DIM = 2 * GLA_KEY_DIM + 2 * GLA_VAL_DIM + GATE_RANK
ATT_HEADS = 16
HEAD_DIM = D_MODEL // ATT_HEADS
WINDOWS = (128, 512, 2048)
DILATIONS = (1, 4, 16)
N_BRANCH = 3
ATT_BLOCK = 128
D_FF = 5632
CONV_WIDTH = 3
EPS = 1e-6

kernel_name = "yoco_gla_dilated_swa_convglu"


def rmsnorm(x, g):
    x32 = x.astype(jnp.float32)
    y = x32 * lax.rsqrt(jnp.mean(x32 * x32, axis=-1, keepdims=True) + EPS)
    return (y * g.astype(jnp.float32)).astype(x.dtype)


def alibi_slopes(n):
    def pow2_slopes(m):
        start = 2.0 ** (-8.0 / m)
        return [start ** (i + 1) for i in range(m)]
    if math.log2(n).is_integer():
        s = pow2_slopes(n)
    else:
        c = 2 ** math.floor(math.log2(n))
        s = pow2_slopes(c) + pow2_slopes(2 * c)[0::2][: n - c]
    return jnp.asarray(np.array(s, dtype=np.float32))


def gla_mixer(h, w_in, w_a2, b_a2, head_norm, w_out):
    bsz, s_len, _ = h.shape
    n_chunks = s_len // GLA_CHUNK
    f32 = jnp.float32
    proj = h @ w_in
    q, k, v, r, a = jnp.split(
        proj, [GLA_KEY_DIM, 2 * GLA_KEY_DIM, 2 * GLA_KEY_DIM + GLA_VAL_DIM,
               2 * GLA_KEY_DIM + 2 * GLA_VAL_DIM], axis=-1)
    log_alpha = jax.nn.log_sigmoid((a @ w_a2 + b_a2).astype(f32)) / GATE_NORMALIZER

    def chunks(t, hd):
        return t.astype(f32).reshape(bsz, n_chunks, GLA_CHUNK, GLA_HEADS, hd).transpose(1, 0, 3, 2, 4)

    qc = chunks(q, GLA_DK) * (GLA_DK ** -0.5)
    kc = chunks(k, GLA_DK)
    vc = chunks(v, GLA_DV)
    cum = jnp.cumsum(chunks(log_alpha, GLA_DK), axis=3)
    last = cum[:, :, :, -1:, :]
    q_dec = qc * jnp.exp(cum)
    k_inv = kc * jnp.exp(-cum)
    k_to_end = kc * jnp.exp(last - cum)

    causal = jnp.tril(jnp.ones((GLA_CHUNK, GLA_CHUNK), dtype=bool))
    scores = jnp.where(causal, jnp.einsum('nbhtk,nbhsk->nbhts', q_dec, k_inv), 0.0)
    o_intra = jnp.einsum('nbhts,nbhsv->nbhtv', scores, vc)

    def step(state, xs):
        q_n, k_n, v_n, dec_n = xs
        o_n = jnp.einsum('bhtk,bhkv->bhtv', q_n, state)
        state = state * dec_n[..., None] + jnp.einsum('bhsk,bhsv->bhkv', k_n, v_n)
        return state, o_n

    state0 = jnp.zeros((bsz, GLA_HEADS, GLA_DK, GLA_DV), f32)
    _, o_inter = lax.scan(step, state0, (q_dec, k_to_end, vc, jnp.exp(last[:, :, :, 0, :])))
    o = (o_intra + o_inter).transpose(1, 0, 3, 2, 4).reshape(bsz, s_len, GLA_HEADS, GLA_DV)
    o = rmsnorm(o, head_norm)
    gate = jax.nn.silu(r.astype(f32)).reshape(bsz, s_len, GLA_HEADS, GLA_DV)
    o = (o * gate).reshape(bsz, s_len, GLA_VAL_DIM).astype(h.dtype)
    return o @ w_out


def to_dilated(t, d):
    bsz, s_len, nh, e = t.shape
    return t.reshape(bsz, s_len // d, d, nh, e).transpose(0, 2, 1, 3, 4)


def n_blocks(sub_len):
    return -(-sub_len // ATT_BLOCK)


def to_query_blocks(t, d):
    td = to_dilated(t, d)
    bsz, _, sub_len, nh, e = td.shape
    nb = n_blocks(sub_len)
    td = jnp.pad(td, ((0, 0), (0, 0), (0, nb * ATT_BLOCK - sub_len), (0, 0), (0, 0)))
    return td.reshape(bsz, d, nb, ATT_BLOCK, nh, e)


def to_key_blocks(t, d):
    td = to_dilated(t, d)
    bsz, _, sub_len, nh, e = td.shape
    nb = n_blocks(sub_len)
    td = jnp.pad(td, ((0, 0), (0, 0), (ATT_BLOCK, nb * ATT_BLOCK - sub_len), (0, 0), (0, 0)))
    return td.reshape(bsz, d, nb + 1, ATT_BLOCK, nh, e)


def from_blocks(t, d, s_len):
    bsz, _, nb, _, nh, e = t.shape
    t = t.reshape(bsz, d, nb * ATT_BLOCK, nh, e)[:, :, : s_len // d]
    return t.transpose(0, 2, 1, 3, 4).reshape(bsz, s_len, nh, e)


def shared_kv(h, kv_norm, w_kv):
    bsz, s_len, _ = h.shape
    kv = rmsnorm(h, kv_norm) @ w_kv
    k, v = jnp.split(kv, 2, axis=-1)
    k = k.reshape(bsz, s_len, ATT_HEADS, HEAD_DIM)
    v = v.reshape(bsz, s_len, ATT_HEADS, HEAD_DIM)
    return [(to_key_blocks(k, d), to_key_blocks(v, d)) for d in DILATIONS]


def dilated_branch(qb, kb, vb, d, keys_back, slopes):
    nb = qb.shape[2]
    s_prev = jnp.einsum('brnqhe,brnkhe->brnhqk', qb, kb[:, :, :-1])
    s_cur = jnp.einsum('brnqhe,brnkhe->brnhqk', qb, kb[:, :, 1:])
    s = jnp.concatenate([s_prev, s_cur], axis=-1).astype(jnp.float32) * (HEAD_DIM ** -0.5)
    qa = jnp.arange(ATT_BLOCK)
    kc = jnp.arange(2 * ATT_BLOCK)
    j = qa[:, None] - kc[None, :] + ATT_BLOCK
    key_sub = jnp.arange(nb)[:, None] * ATT_BLOCK - ATT_BLOCK + kc[None, :]
    valid = ((j >= 0) & (j <= keys_back))[None] & (key_sub >= 0)[:, None, :]
    alibi = -slopes[:, None, None] * (j * d).astype(jnp.float32)[None]
    s = jnp.where(valid[None, None, :, None], s + alibi[None, None, None], -jnp.inf)
    m = jnp.max(s, axis=-1, keepdims=True)
    p = jnp.exp(s - m)
    l = jnp.sum(p, axis=-1, keepdims=True)
    o = (jnp.einsum('brnhqk,brnkhe->brnqhe', p[..., :ATT_BLOCK], vb[:, :, :-1])
         + jnp.einsum('brnhqk,brnkhe->brnqhe', p[..., ATT_BLOCK:], vb[:, :, 1:]))
    o = o / l.transpose(0, 1, 2, 4, 3, 5)
    lse = (m + jnp.log(l)).transpose(0, 1, 2, 4, 3, 5)
    return o, lse


def dilated_mixer(h, kv_blocks, w_q, w_out):
    bsz, s_len, _ = h.shape
    q = (h @ w_q).reshape(bsz, s_len, N_BRANCH, ATT_HEADS, HEAD_DIM)
    slopes = alibi_slopes(ATT_HEADS)
    outs, lses = [], []
    for g in range(N_BRANCH):
        d = DILATIONS[g]
        kb, vb = kv_blocks[g]
        o, lse = dilated_branch(to_query_blocks(q[:, :, g], d), kb, vb, d, WINDOWS[g] // d, slopes)
        outs.append(from_blocks(o, d, s_len))
        lses.append(from_blocks(lse, d, s_len))
    w = jax.nn.softmax(jnp.stack(lses, axis=0), axis=0)
    o = jnp.sum(w * jnp.stack(outs, axis=0), axis=0)
    return o.reshape(bsz, s_len, ATT_HEADS * HEAD_DIM).astype(h.dtype) @ w_out


def conv_glu(h, w_up, conv_w, conv_b, w_down):
    u, g = jnp.split(h @ w_up, 2, axis=-1)
    gp = jnp.pad(g, ((0, 0), (CONV_WIDTH - 1, 0), (0, 0)))
    g = conv_w[0] * gp[:, :-2] + conv_w[1] * gp[:, 1:-1] + conv_w[2] * gp[:, 2:] + conv_b
    return (jax.nn.gelu(g, approximate=False) * u) @ w_down


def _fwd_setup_inputs(seed: int = 0) -> dict:
    key = jax.random.key(seed)
    ks = jax.random.split(key, 17)
    f32 = jnp.float32

    def nrm(k, shape, fan_in):
        return jax.random.normal(k, shape, f32) * (fan_in ** -0.5)

    def gain(k, shape):
        return 1.0 + 0.02 * jax.random.normal(k, shape, f32)

    return {
        "x": jax.random.normal(ks[0], (BATCH, SEQ, D_MODEL), f32),
        "attn_norm": gain(ks[1], (DEPTH, D_MODEL)),
        "gla_w_in": nrm(ks[2], (N_A_LAYERS, D_MODEL, GLA_IN_DIM), D_MODEL),
        "gla_w_a2": nrm(ks[3], (N_A_LAYERS, GATE_RANK, GLA_KEY_DIM), GATE_RANK),
        "gla_b_a2": 0.1 * jax.random.normal(ks[4], (N_A_LAYERS, GLA_KEY_DIM), f32),
        "gla_head_norm": gain(ks[5], (N_A_LAYERS, GLA_DV)),
        "gla_w_out": nrm(ks[6], (N_A_LAYERS, GLA_VAL_DIM, D_MODEL), GLA_VAL_DIM),
        "kv_norm": gain(ks[7], (D_MODEL,)),
        "w_kv": nrm(ks[8], (D_MODEL, 2 * ATT_HEADS * HEAD_DIM), D_MODEL),
        "dsa_w_q": nrm(ks[9], (N_B_LAYERS, D_MODEL, N_BRANCH * ATT_HEADS * HEAD_DIM), D_MODEL),
        "dsa_w_out": nrm(ks[10], (N_B_LAYERS, ATT_HEADS * HEAD_DIM, D_MODEL), ATT_HEADS * HEAD_DIM),
        "ffn_norm": gain(ks[11], (DEPTH, D_MODEL)),
        "ffn_w_up": nrm(ks[12], (DEPTH, D_MODEL, 2 * D_FF), D_MODEL),
        "ffn_conv_w": nrm(ks[13], (DEPTH, CONV_WIDTH, D_FF), CONV_WIDTH),
        "ffn_conv_b": 0.02 * jax.random.normal(ks[14], (DEPTH, D_FF), f32),
        "ffn_w_down": nrm(ks[15], (DEPTH, D_FF, D_MODEL), D_FF),
        "final_norm": gain(ks[16], (D_MODEL,)),
    }


def _fwd_reference(x, attn_norm, gla_w_in, gla_w_a2, gla_b_a2, gla_head_norm, gla_w_out,
              kv_norm, w_kv, dsa_w_q, dsa_w_out, ffn_norm, ffn_w_up, ffn_conv_w,
              ffn_conv_b, ffn_w_down, final_norm):
    h = x
    kv_blocks = None
    for i in range(DEPTH):
        if i < N_A_LAYERS:
            h = h + gla_mixer(rmsnorm(h, attn_norm[i]), gla_w_in[i], gla_w_a2[i], gla_b_a2[i],
                              gla_head_norm[i], gla_w_out[i])
        else:
            if i == N_A_LAYERS:
                kv_blocks = shared_kv(h, kv_norm, w_kv)
            j = i - N_A_LAYERS
            h = h + dilated_mixer(rmsnorm(h, attn_norm[i]), kv_blocks, dsa_w_q[j], dsa_w_out[j])
        h = h + conv_glu(rmsnorm(h, ffn_norm[i]), ffn_w_up[i], ffn_conv_w[i], ffn_conv_b[i],
                         ffn_w_down[i])
    return rmsnorm(h, final_norm)


import jax as _jax
import jax.numpy as _jnp

TWIN_FORMAT = 'train_step'
FWD_PARAMS = ['x', 'attn_norm', 'gla_w_in', 'gla_w_a2', 'gla_b_a2', 'gla_head_norm', 'gla_w_out', 'kv_norm', 'w_kv', 'dsa_w_q', 'dsa_w_out', 'ffn_norm', 'ffn_w_up', 'ffn_conv_w', 'ffn_conv_b', 'ffn_w_down', 'final_norm']
TWIN_WEIGHTS = ['attn_norm', 'gla_w_in', 'gla_w_a2', 'gla_b_a2', 'gla_head_norm', 'gla_w_out', 'kv_norm', 'w_kv', 'dsa_w_q', 'dsa_w_out', 'ffn_norm', 'ffn_w_up', 'ffn_conv_w', 'ffn_conv_b', 'ffn_w_down', 'final_norm']
TWIN_DIFF_INPUT = 'x'
TWIN_INPUTS = ['x', 'attn_norm', 'gla_w_in', 'gla_w_a2', 'gla_b_a2', 'gla_head_norm', 'gla_w_out', 'kv_norm', 'w_kv', 'dsa_w_q', 'dsa_w_out', 'ffn_norm', 'ffn_w_up', 'ffn_conv_w', 'ffn_conv_b', 'ffn_w_down', 'final_norm', 'loss_target', 'm_attn_norm', 'm_gla_w_in', 'm_gla_w_a2', 'm_gla_b_a2', 'm_gla_head_norm', 'm_gla_w_out', 'm_kv_norm', 'm_w_kv', 'm_dsa_w_q', 'm_dsa_w_out', 'm_ffn_norm', 'm_ffn_w_up', 'm_ffn_conv_w', 'm_ffn_conv_b', 'm_ffn_w_down', 'm_final_norm', 'v_attn_norm', 'v_gla_w_in', 'v_gla_w_a2', 'v_gla_b_a2', 'v_gla_head_norm', 'v_gla_w_out', 'v_kv_norm', 'v_w_kv', 'v_dsa_w_q', 'v_dsa_w_out', 'v_ffn_norm', 'v_ffn_w_up', 'v_ffn_conv_w', 'v_ffn_conv_b', 'v_ffn_w_down', 'v_final_norm']
TWIN_OUTPUTS = ['loss', 'grad_x', 'grad_attn_norm', 'grad_gla_w_in', 'grad_gla_w_a2', 'grad_gla_b_a2', 'grad_gla_head_norm', 'grad_gla_w_out', 'grad_kv_norm', 'grad_w_kv', 'grad_dsa_w_q', 'grad_dsa_w_out', 'grad_ffn_norm', 'grad_ffn_w_up', 'grad_ffn_conv_w', 'grad_ffn_conv_b', 'grad_ffn_w_down', 'grad_final_norm', 'delta_attn_norm', 'delta_gla_w_in', 'delta_gla_w_a2', 'delta_gla_b_a2', 'delta_gla_head_norm', 'delta_gla_w_out', 'delta_kv_norm', 'delta_w_kv', 'delta_dsa_w_q', 'delta_dsa_w_out', 'delta_ffn_norm', 'delta_ffn_w_up', 'delta_ffn_conv_w', 'delta_ffn_conv_b', 'delta_ffn_w_down', 'delta_final_norm', 'new_m_attn_norm', 'new_m_gla_w_in', 'new_m_gla_w_a2', 'new_m_gla_b_a2', 'new_m_gla_head_norm', 'new_m_gla_w_out', 'new_m_kv_norm', 'new_m_w_kv', 'new_m_dsa_w_q', 'new_m_dsa_w_out', 'new_m_ffn_norm', 'new_m_ffn_w_up', 'new_m_ffn_conv_w', 'new_m_ffn_conv_b', 'new_m_ffn_w_down', 'new_m_final_norm', 'new_v_attn_norm', 'new_v_gla_w_in', 'new_v_gla_w_a2', 'new_v_gla_b_a2', 'new_v_gla_head_norm', 'new_v_gla_w_out', 'new_v_kv_norm', 'new_v_w_kv', 'new_v_dsa_w_q', 'new_v_dsa_w_out', 'new_v_ffn_norm', 'new_v_ffn_w_up', 'new_v_ffn_conv_w', 'new_v_ffn_conv_b', 'new_v_ffn_w_down', 'new_v_final_norm']
TWIN_LEAF_KINDS = {'loss': 'loss', 'grad_x': 'grad_x', 'grad_attn_norm': 'grad_w', 'grad_gla_w_in': 'grad_w', 'grad_gla_w_a2': 'grad_w', 'grad_gla_b_a2': 'grad_w', 'grad_gla_head_norm': 'grad_w', 'grad_gla_w_out': 'grad_w', 'grad_kv_norm': 'grad_w', 'grad_w_kv': 'grad_w', 'grad_dsa_w_q': 'grad_w', 'grad_dsa_w_out': 'grad_w', 'grad_ffn_norm': 'grad_w', 'grad_ffn_w_up': 'grad_w', 'grad_ffn_conv_w': 'grad_w', 'grad_ffn_conv_b': 'grad_w', 'grad_ffn_w_down': 'grad_w', 'grad_final_norm': 'grad_w', 'delta_attn_norm': 'delta_w', 'delta_gla_w_in': 'delta_w', 'delta_gla_w_a2': 'delta_w', 'delta_gla_b_a2': 'delta_w', 'delta_gla_head_norm': 'delta_w', 'delta_gla_w_out': 'delta_w', 'delta_kv_norm': 'delta_w', 'delta_w_kv': 'delta_w', 'delta_dsa_w_q': 'delta_w', 'delta_dsa_w_out': 'delta_w', 'delta_ffn_norm': 'delta_w', 'delta_ffn_w_up': 'delta_w', 'delta_ffn_conv_w': 'delta_w', 'delta_ffn_conv_b': 'delta_w', 'delta_ffn_w_down': 'delta_w', 'delta_final_norm': 'delta_w', 'new_m_attn_norm': 'new_m', 'new_m_gla_w_in': 'new_m', 'new_m_gla_w_a2': 'new_m', 'new_m_gla_b_a2': 'new_m', 'new_m_gla_head_norm': 'new_m', 'new_m_gla_w_out': 'new_m', 'new_m_kv_norm': 'new_m', 'new_m_w_kv': 'new_m', 'new_m_dsa_w_q': 'new_m', 'new_m_dsa_w_out': 'new_m', 'new_m_ffn_norm': 'new_m', 'new_m_ffn_w_up': 'new_m', 'new_m_ffn_conv_w': 'new_m', 'new_m_ffn_conv_b': 'new_m', 'new_m_ffn_w_down': 'new_m', 'new_m_final_norm': 'new_m', 'new_v_attn_norm': 'new_v', 'new_v_gla_w_in': 'new_v', 'new_v_gla_w_a2': 'new_v', 'new_v_gla_b_a2': 'new_v', 'new_v_gla_head_norm': 'new_v', 'new_v_gla_w_out': 'new_v', 'new_v_kv_norm': 'new_v', 'new_v_w_kv': 'new_v', 'new_v_dsa_w_q': 'new_v', 'new_v_dsa_w_out': 'new_v', 'new_v_ffn_norm': 'new_v', 'new_v_ffn_w_up': 'new_v', 'new_v_ffn_conv_w': 'new_v', 'new_v_ffn_conv_b': 'new_v', 'new_v_ffn_w_down': 'new_v', 'new_v_final_norm': 'new_v'}


def _forward(args):
    return _fwd_reference(*[args[k] for k in FWD_PARAMS])


def _output_shape():
    def fwd():
        inp = _fwd_setup_inputs(0)
        return _fwd_reference(*[inp[k] for k in FWD_PARAMS])
    out = _jax.eval_shape(fwd)
    return out.shape, out.dtype

N_MICROBATCH = 1
ADAM_LR = 0.001
ADAM_B1 = 0.9
ADAM_B2 = 0.999
ADAM_EPS = 1e-08
ADAM_WD = 0.01
ADAM_STEP = 10
PER_EXAMPLE_BATCH_AXIS = {'x': 0, 'loss_target': 0}
SHARED_INPUTS = []
_WEIGHT_DTYPES = {'attn_norm': _jnp.float32, 'gla_w_in': _jnp.float32, 'gla_w_a2': _jnp.float32, 'gla_b_a2': _jnp.float32, 'gla_head_norm': _jnp.float32, 'gla_w_out': _jnp.float32, 'kv_norm': _jnp.float32, 'w_kv': _jnp.float32, 'dsa_w_q': _jnp.float32, 'dsa_w_out': _jnp.float32, 'ffn_norm': _jnp.float32, 'ffn_w_up': _jnp.float32, 'ffn_conv_w': _jnp.float32, 'ffn_conv_b': _jnp.float32, 'ffn_w_down': _jnp.float32, 'final_norm': _jnp.float32}
MOMENT_SCALE = {'attn_norm': 8.430758e-02, 'gla_w_in': 6.756085e-02, 'gla_w_a2': 9.503014e-03, 'gla_b_a2': 3.881103e-02, 'gla_head_norm': 1.177283e-01, 'gla_w_out': 5.713720e-02, 'kv_norm': 3.263685e-02, 'w_kv': 2.251578e-02, 'dsa_w_q': 8.743720e-03, 'dsa_w_out': 2.794643e-02, 'ffn_norm': 6.369890e-02, 'ffn_w_up': 2.701196e-02, 'ffn_conv_w': 2.718953e-02, 'ffn_conv_b': 2.621829e-02, 'ffn_w_down': 4.413774e-02, 'final_norm': 1.599446e+01}


def _to_microbatches(a, axis):
    t = _jnp.moveaxis(a, axis, 0)
    t = t.reshape((N_MICROBATCH, t.shape[0] // N_MICROBATCH) + t.shape[1:])
    return _jnp.moveaxis(t, 1, axis + 1)


def setup_inputs(seed: int = 0) -> dict:
    inp = _fwd_setup_inputs(seed)
    key = _jax.random.fold_in(_jax.random.key(seed), 7919)
    shape, _ = _output_shape()
    out = dict(inp)
    out["loss_target"] = _jax.random.normal(_jax.random.fold_in(key, 0), shape, _jnp.float32)
    for i, name in enumerate(TWIN_WEIGHTS):
        w = inp[name].astype(_jnp.float32)
        if MOMENT_SCALE is None:
            s = _jnp.sqrt(_jnp.mean(_jnp.square(w)) + 1e-30)
        else:
            s = MOMENT_SCALE[name]
        km, kv = _jax.random.split(_jax.random.fold_in(key, i + 1))
        out[name] = w
        out["m_" + name] = s * _jax.random.normal(km, w.shape, _jnp.float32)
        out["v_" + name] = (s * s) * _jax.random.uniform(kv, w.shape, _jnp.float32, 0.5, 1.5)
    if N_MICROBATCH > 1:
        for name, axis in PER_EXAMPLE_BATCH_AXIS.items():
            out[name] = _to_microbatches(out[name], axis)
    return {'x': out['x'], 'attn_norm': out['attn_norm'], 'gla_w_in': out['gla_w_in'], 'gla_w_a2': out['gla_w_a2'], 'gla_b_a2': out['gla_b_a2'], 'gla_head_norm': out['gla_head_norm'], 'gla_w_out': out['gla_w_out'], 'kv_norm': out['kv_norm'], 'w_kv': out['w_kv'], 'dsa_w_q': out['dsa_w_q'], 'dsa_w_out': out['dsa_w_out'], 'ffn_norm': out['ffn_norm'], 'ffn_w_up': out['ffn_w_up'], 'ffn_conv_w': out['ffn_conv_w'], 'ffn_conv_b': out['ffn_conv_b'], 'ffn_w_down': out['ffn_w_down'], 'final_norm': out['final_norm'], 'loss_target': out['loss_target'], 'm_attn_norm': out['m_attn_norm'], 'm_gla_w_in': out['m_gla_w_in'], 'm_gla_w_a2': out['m_gla_w_a2'], 'm_gla_b_a2': out['m_gla_b_a2'], 'm_gla_head_norm': out['m_gla_head_norm'], 'm_gla_w_out': out['m_gla_w_out'], 'm_kv_norm': out['m_kv_norm'], 'm_w_kv': out['m_w_kv'], 'm_dsa_w_q': out['m_dsa_w_q'], 'm_dsa_w_out': out['m_dsa_w_out'], 'm_ffn_norm': out['m_ffn_norm'], 'm_ffn_w_up': out['m_ffn_w_up'], 'm_ffn_conv_w': out['m_ffn_conv_w'], 'm_ffn_conv_b': out['m_ffn_conv_b'], 'm_ffn_w_down': out['m_ffn_w_down'], 'm_final_norm': out['m_final_norm'], 'v_attn_norm': out['v_attn_norm'], 'v_gla_w_in': out['v_gla_w_in'], 'v_gla_w_a2': out['v_gla_w_a2'], 'v_gla_b_a2': out['v_gla_b_a2'], 'v_gla_head_norm': out['v_gla_head_norm'], 'v_gla_w_out': out['v_gla_w_out'], 'v_kv_norm': out['v_kv_norm'], 'v_w_kv': out['v_w_kv'], 'v_dsa_w_q': out['v_dsa_w_q'], 'v_dsa_w_out': out['v_dsa_w_out'], 'v_ffn_norm': out['v_ffn_norm'], 'v_ffn_w_up': out['v_ffn_w_up'], 'v_ffn_conv_w': out['v_ffn_conv_w'], 'v_ffn_conv_b': out['v_ffn_conv_b'], 'v_ffn_w_down': out['v_ffn_w_down'], 'v_final_norm': out['v_final_norm']}


def _loss(weights, diff, rest, loss_target):
    with _jax.named_scope("forward"):
        args = {**rest, TWIN_DIFF_INPUT: diff, **{k: w.astype(_WEIGHT_DTYPES[k]) for k, w in weights.items()}}
        y = _forward(args)
    with _jax.named_scope("loss_head"):
        err = _jnp.square(y.astype(_jnp.float32) - loss_target)
        return 0.5 * _jnp.sum(_jnp.mean(err, axis=-1)) if err.ndim else 0.5 * err


def _adamw(w, g, m, v):
    m = ADAM_B1 * m + (1.0 - ADAM_B1) * g
    v = ADAM_B2 * v + (1.0 - ADAM_B2) * _jnp.square(g)
    m_hat = m / (1.0 - ADAM_B1 ** ADAM_STEP)
    v_hat = v / (1.0 - ADAM_B2 ** ADAM_STEP)
    delta = -ADAM_LR * (m_hat / (_jnp.sqrt(v_hat) + ADAM_EPS) + ADAM_WD * w)
    return delta, m, v


def reference(x, attn_norm, gla_w_in, gla_w_a2, gla_b_a2, gla_head_norm, gla_w_out, kv_norm, w_kv, dsa_w_q, dsa_w_out, ffn_norm, ffn_w_up, ffn_conv_w, ffn_conv_b, ffn_w_down, final_norm, loss_target, m_attn_norm, m_gla_w_in, m_gla_w_a2, m_gla_b_a2, m_gla_head_norm, m_gla_w_out, m_kv_norm, m_w_kv, m_dsa_w_q, m_dsa_w_out, m_ffn_norm, m_ffn_w_up, m_ffn_conv_w, m_ffn_conv_b, m_ffn_w_down, m_final_norm, v_attn_norm, v_gla_w_in, v_gla_w_a2, v_gla_b_a2, v_gla_head_norm, v_gla_w_out, v_kv_norm, v_w_kv, v_dsa_w_q, v_dsa_w_out, v_ffn_norm, v_ffn_w_up, v_ffn_conv_w, v_ffn_conv_b, v_ffn_w_down, v_final_norm):
    given = dict(x=x, attn_norm=attn_norm, gla_w_in=gla_w_in, gla_w_a2=gla_w_a2, gla_b_a2=gla_b_a2, gla_head_norm=gla_head_norm, gla_w_out=gla_w_out, kv_norm=kv_norm, w_kv=w_kv, dsa_w_q=dsa_w_q, dsa_w_out=dsa_w_out, ffn_norm=ffn_norm, ffn_w_up=ffn_w_up, ffn_conv_w=ffn_conv_w, ffn_conv_b=ffn_conv_b, ffn_w_down=ffn_w_down, final_norm=final_norm, loss_target=loss_target, m_attn_norm=m_attn_norm, m_gla_w_in=m_gla_w_in, m_gla_w_a2=m_gla_w_a2, m_gla_b_a2=m_gla_b_a2, m_gla_head_norm=m_gla_head_norm, m_gla_w_out=m_gla_w_out, m_kv_norm=m_kv_norm, m_w_kv=m_w_kv, m_dsa_w_q=m_dsa_w_q, m_dsa_w_out=m_dsa_w_out, m_ffn_norm=m_ffn_norm, m_ffn_w_up=m_ffn_w_up, m_ffn_conv_w=m_ffn_conv_w, m_ffn_conv_b=m_ffn_conv_b, m_ffn_w_down=m_ffn_w_down, m_final_norm=m_final_norm, v_attn_norm=v_attn_norm, v_gla_w_in=v_gla_w_in, v_gla_w_a2=v_gla_w_a2, v_gla_b_a2=v_gla_b_a2, v_gla_head_norm=v_gla_head_norm, v_gla_w_out=v_gla_w_out, v_kv_norm=v_kv_norm, v_w_kv=v_w_kv, v_dsa_w_q=v_dsa_w_q, v_dsa_w_out=v_dsa_w_out, v_ffn_norm=v_ffn_norm, v_ffn_w_up=v_ffn_w_up, v_ffn_conv_w=v_ffn_conv_w, v_ffn_conv_b=v_ffn_conv_b, v_ffn_w_down=v_ffn_w_down, v_final_norm=v_final_norm)
    weights = {n: given[n] for n in TWIN_WEIGHTS}
    shared = {n: given[n] for n in SHARED_INPUTS}
    per_example = {n: given[n] for n in ['x']}
    grad_fn = _jax.value_and_grad(_loss, argnums=(0, 1))

    def one_microbatch(ex, loss_target):
        ex = dict(ex)
        diff = ex.pop(TWIN_DIFF_INPUT)
        return grad_fn(weights, diff, {**shared, **ex}, loss_target)

    if N_MICROBATCH == 1:
        loss, (grad_w, grad_x) = one_microbatch(per_example, given["loss_target"])
    else:
        def body(carry, xs):
            loss_sum, grad_sum = carry
            l_k, (gw_k, gx_k) = one_microbatch(xs[0], xs[1])
            with _jax.named_scope("update"):
                return (loss_sum + l_k, _jax.tree.map(_jnp.add, grad_sum, gw_k)), gx_k

        init = (_jnp.zeros((), _jnp.float32), _jax.tree.map(_jnp.zeros_like, weights))
        (loss, grad_w), grad_x = _jax.lax.scan(body, init, (per_example, given["loss_target"]))
    with _jax.named_scope("update"):
        delta_w, new_m, new_v = {}, {}, {}
        for n in TWIN_WEIGHTS:
            delta_w[n], new_m[n], new_v[n] = _adamw(weights[n], grad_w[n], given["m_" + n], given["v_" + n])
    return (loss, grad_x, *[grad_w[n] for n in TWIN_WEIGHTS], *[delta_w[n] for n in TWIN_WEIGHTS],
            *[new_m[n] for n in TWIN_WEIGHTS], *[new_v[n] for n in TWIN_WEIGHTS])
```

```python
import functools

import jax
import jax.numpy as jnp
from jax import lax
from jax.experimental import pallas as pl
from jax.experimental.pallas import tpu as pltpu

F32 = jnp.float32
BF16 = jnp.bfloat16
MESH = pl.DeviceIdType.MESH
ANY = pl.BlockSpec(memory_space=pl.ANY)

N_DEV = 8
D_MODEL = 2048
GLA_HEADS = 4
GLA_KEY_DIM = 1024
GLA_VAL_DIM = 2048
GLA_DK = 256
GLA_DV = 512
GATE_RANK = 16
GATE_NORMALIZER = 16.0
GLA_CHUNK = 64
GLA_IN_DIM = 2 * GLA_KEY_DIM + 2 * GLA_VAL_DIM + GATE_RANK
GLA_IN_PAD = 6272
ATT_HEADS = 16
HEAD_DIM = 128
DILATIONS = (1, 4, 16)
ATT_BLOCK = 128
D_FF = 5632
EPS = 1e-6
ADAM_LR = 0.001
ADAM_B1 = 0.9
ADAM_B2 = 0.999
ADAM_EPS = 1e-08
ADAM_WD = 0.01
ADAM_STEP = 10
NEG = -1e30
LANE = 128
VMEM_LIMIT = 52 * 1024 * 1024
ALIBI_SLOPES = tuple(2.0 ** (-0.5 * (i + 1)) for i in range(ATT_HEADS))


def _params(*sem):
    return pltpu.CompilerParams(dimension_semantics=sem, vmem_limit_bytes=VMEM_LIMIT)


def _tile(n, cap):
    best = None
    for t in range(LANE, min(n, cap) + 1, LANE):
        if n % t == 0:
            best = t
    return best if best is not None else n


def _rows(r, c, budget=256 * 1024):
    best = None
    for t in range(16, r + 1, 16):
        if r % t == 0 and t * c <= budget:
            best = t
    return best if best is not None else r


def mm_nn(a, w, *, out_dtype, name, res=None, tm=512):
    m, k = a.shape
    j, k2, ns = w.shape
    assert k == k2 and m % tm == 0
    tn = _tile(ns, 1408)
    nsub = ns // tn
    tk = k if k <= 2048 else _tile(k, 1408)
    nk = k // tk
    has_res = res is not None

    def body(*refs):
        if has_res:
            a_ref, w_ref, r_ref, o_ref, acc = refs
        else:
            a_ref, w_ref, o_ref, acc = refs
        kk = pl.program_id(2)

        @pl.when(kk == 0)
        def _():
            acc[...] = jnp.zeros_like(acc)

        acc[...] += jnp.dot(a_ref[...].astype(BF16), w_ref[...], preferred_element_type=F32)

        @pl.when(kk == nk - 1)
        def _():
            r = acc[...]
            if has_res:
                r = r + r_ref[...]
            o_ref[...] = r.astype(out_dtype)

    in_specs = [
        pl.BlockSpec((tm, tk), lambda i, n, kk: (i, kk)),
        pl.BlockSpec((None, tk, tn), lambda i, n, kk: (n // nsub, kk, n % nsub)),
    ]
    args = [a, w]
    if has_res:
        in_specs.append(pl.BlockSpec((tm, tn), lambda i, n, kk: (i, n)))
        args.append(res)
    return pl.pallas_call(
        body,
        name=name,
        out_shape=jax.ShapeDtypeStruct((m, j * ns), out_dtype),
        grid=(m // tm, j * nsub, nk),
        in_specs=in_specs,
        out_specs=pl.BlockSpec((tm, tn), lambda i, n, kk: (i, n)),
        scratch_shapes=[pltpu.VMEM((tm, tn), F32)],
        compiler_params=_params("parallel", "parallel", "arbitrary"),
    )(*args)


def mm_nt(dy, w, *, out_dtype, name, tm=512):
    m, n = dy.shape
    j, k, ns = w.shape
    assert n == j * ns and m % tm == 0
    tn = _tile(ns, 2048)
    nsub = ns // tn
    tko = _tile(k, 1408)
    nn = j * nsub

    def body(a_ref, w_ref, o_ref, acc):
        nq = pl.program_id(2)

        @pl.when(nq == 0)
        def _():
            acc[...] = jnp.zeros_like(acc)

        acc[...] += lax.dot_general(a_ref[...].astype(BF16), w_ref[...], (((1,), (1,)), ((), ())),
                                    preferred_element_type=F32)

        @pl.when(nq == nn - 1)
        def _():
            o_ref[...] = acc[...].astype(out_dtype)

    return pl.pallas_call(
        body,
        name=name,
        out_shape=jax.ShapeDtypeStruct((m, k), out_dtype),
        grid=(m // tm, k // tko, nn),
        in_specs=[
            pl.BlockSpec((tm, tn), lambda i, ko, nq: (i, nq)),
            pl.BlockSpec((None, tko, tn), lambda i, ko, nq: (nq // nsub, ko, nq % nsub)),
        ],
        out_specs=pl.BlockSpec((tm, tko), lambda i, ko, nq: (i, ko)),
        scratch_shapes=[pltpu.VMEM((tm, tko), F32)],
        compiler_params=_params("parallel", "parallel", "arbitrary"),
    )(dy, w)


def mm_tn(x, dy, j, *, name, tm=1024):
    m, k = x.shape
    m2, n = dy.shape
    assert m == m2 and n % j == 0 and m % tm == 0
    ns = n // j
    tn = _tile(ns, 1408)
    nsub = ns // tn
    tk = _tile(k, 1408)
    nm = m // tm

    def body(x_ref, dy_ref, o_ref, acc):
        mi = pl.program_id(2)

        @pl.when(mi == 0)
        def _():
            acc[...] = jnp.zeros_like(acc)

        acc[...] += lax.dot_general(x_ref[...].astype(BF16), dy_ref[...].astype(BF16), (((0,), (0,)), ((), ())),
                                    preferred_element_type=F32)

        @pl.when(mi == nm - 1)
        def _():
            o_ref[...] = acc[...].astype(BF16)

    return pl.pallas_call(
        body,
        name=name,
        out_shape=jax.ShapeDtypeStruct((j, k, ns), BF16),
        grid=(k // tk, j * nsub, nm),
        in_specs=[
            pl.BlockSpec((tm, tk), lambda kq, nq, mi: (mi, kq)),
            pl.BlockSpec((tm, tn), lambda kq, nq, mi: (mi, nq)),
        ],
        out_specs=pl.BlockSpec((None, tk, tn), lambda kq, nq, mi: (nq // nsub, kq, nq % nsub)),
        scratch_shapes=[pltpu.VMEM((tk, tn), F32)],
        compiler_params=_params("parallel", "parallel", "arbitrary"),
    )(x, dy)


def rms_fwd(x, gains, *, name, ts=512):
    s, d = x.shape
    n = len(gains)

    def body(x_ref, *refs):
        xv = x_ref[...]
        xh = xv * lax.rsqrt(jnp.mean(xv * xv, axis=-1, keepdims=True) + EPS)
        for g_ref, o_ref in zip(refs[:n], refs[n:]):
            o_ref[...] = (xh * g_ref[...]).astype(BF16)

    row = pl.BlockSpec((ts, d), lambda i: (i, 0))
    vec = pl.BlockSpec((1, d), lambda i: (0, 0))
    return pl.pallas_call(
        body,
        name=name,
        out_shape=[jax.ShapeDtypeStruct((s, d), BF16)] * n,
        grid=(s // ts,),
        in_specs=[row] + [vec] * n,
        out_specs=[row] * n,
        compiler_params=_params("parallel"),
    )(x, *gains)


def rms_bwd(x, gains, dys, dres, *, name, ts=256):
    s, d = x.shape
    n = len(gains)

    def body(x_ref, r_ref, *refs):
        g_refs, dy_refs = refs[:n], refs[n:2 * n]
        dx_ref, dg_refs = refs[2 * n], refs[2 * n + 1:]
        i = pl.program_id(0)
        xv = x_ref[...]
        r = lax.rsqrt(jnp.mean(xv * xv, axis=-1, keepdims=True) + EPS)
        xh = xv * r
        acc = r_ref[...]
        for g_ref, dy_ref, dg_ref in zip(g_refs, dy_refs, dg_refs):
            dy = dy_ref[...].astype(F32)

            @pl.when(i == 0)
            def _():
                dg_ref[...] = jnp.zeros_like(dg_ref)

            dg_ref[...] += jnp.sum(dy * xh, axis=0, keepdims=True)
            dxh = dy * g_ref[...]
            acc = acc + r * (dxh - xh * jnp.mean(dxh * xh, axis=-1, keepdims=True))
        dx_ref[...] = acc

    row = pl.BlockSpec((ts, d), lambda i: (i, 0))
    vec = pl.BlockSpec((1, d), lambda i: (0, 0))
    outs = pl.pallas_call(
        body,
        name=name,
        out_shape=[jax.ShapeDtypeStruct((s, d), F32)] + [jax.ShapeDtypeStruct((1, d), F32)] * n,
        grid=(s // ts,),
        in_specs=[row, row] + [vec] * n + [row] * n,
        out_specs=[row] + [vec] * n,
        compiler_params=_params("arbitrary"),
    )(x, dres, *gains, *dys)
    return outs[0], outs[1:]


def loss_head(h, gain, target, *, ts=256):
    s, d = h.shape

    def body(h_ref, g_ref, t_ref, l_ref, dh_ref, dg_ref):
        i = pl.program_id(0)

        @pl.when(i == 0)
        def _():
            l_ref[...] = jnp.zeros_like(l_ref)
            dg_ref[...] = jnp.zeros_like(dg_ref)

        xv = h_ref[...]
        r = lax.rsqrt(jnp.mean(xv * xv, axis=-1, keepdims=True) + EPS)
        xh = xv * r
        g = g_ref[...]
        err = xh * g - t_ref[...]
        l_ref[...] += 0.5 * jnp.sum(jnp.mean(err * err, axis=-1, keepdims=True))
        dy = err * (1.0 / d)
        dg_ref[...] += jnp.sum(dy * xh, axis=0, keepdims=True)
        dxh = dy * g
        dh_ref[...] = r * (dxh - xh * jnp.mean(dxh * xh, axis=-1, keepdims=True))

    row = pl.BlockSpec((ts, d), lambda i: (i, 0))
    vec = pl.BlockSpec((1, d), lambda i: (0, 0))
    return pl.pallas_call(
        body,
        name="loss_head",
        out_shape=[jax.ShapeDtypeStruct((8, LANE), F32), jax.ShapeDtypeStruct((s, d), F32),
                   jax.ShapeDtypeStruct((1, d), F32)],
        grid=(s // ts,),
        in_specs=[row, vec, row],
        out_specs=[pl.BlockSpec((8, LANE), lambda i: (0, 0)), row, vec],
        compiler_params=_params("arbitrary"),
    )(h, gain, target)


A_BLOCK = (2 * GLA_KEY_DIM + 2 * GLA_VAL_DIM) // LANE


def gate_fwd(proj, w_a2p, b_a2, *, ts=512):
    s = proj.shape[0]

    def body(a_ref, w_ref, b_ref, o_ref):
        z = jnp.dot(a_ref[...].astype(BF16), w_ref[...], preferred_element_type=F32) + b_ref[...]
        o_ref[...] = (jnp.minimum(z, 0.0) - jnp.log(1.0 + jnp.exp(-jnp.abs(z)))) * (1.0 / GATE_NORMALIZER)

    return pl.pallas_call(
        body,
        name="gate_fwd",
        out_shape=jax.ShapeDtypeStruct((s, GLA_KEY_DIM), F32),
        grid=(s // ts,),
        in_specs=[pl.BlockSpec((ts, LANE), lambda i: (i, A_BLOCK)),
                  pl.BlockSpec((LANE, GLA_KEY_DIM), lambda i: (0, 0)),
                  pl.BlockSpec((1, GLA_KEY_DIM), lambda i: (0, 0))],
        out_specs=pl.BlockSpec((ts, GLA_KEY_DIM), lambda i: (i, 0)),
        compiler_params=_params("parallel"),
    )(proj, w_a2p, b_a2)


def gate_bwd(proj, w_a2p, b_a2, dla, *, ts=512):
    s = proj.shape[0]

    def body(a_ref, w_ref, b_ref, dla_ref, da_ref, dw_ref, db_ref):
        i = pl.program_id(0)

        @pl.when(i == 0)
        def _():
            dw_ref[...] = jnp.zeros_like(dw_ref)
            db_ref[...] = jnp.zeros_like(db_ref)

        a = a_ref[...].astype(BF16)
        w = w_ref[...]
        z = jnp.dot(a, w, preferred_element_type=F32) + b_ref[...]
        dz = dla_ref[...] * (1.0 / GATE_NORMALIZER) / (1.0 + jnp.exp(z))
        dzb = dz.astype(BF16)
        da_ref[...] = lax.dot_general(dzb, w, (((1,), (1,)), ((), ())), preferred_element_type=F32).astype(BF16)
        dw_ref[...] += lax.dot_general(a, dzb, (((0,), (0,)), ((), ())), preferred_element_type=F32)
        db_ref[...] += jnp.sum(dz, axis=0, keepdims=True)

    return pl.pallas_call(
        body,
        name="gate_bwd",
        out_shape=[jax.ShapeDtypeStruct((s, LANE), BF16), jax.ShapeDtypeStruct((LANE, GLA_KEY_DIM), F32),
                   jax.ShapeDtypeStruct((1, GLA_KEY_DIM), F32)],
        grid=(s // ts,),
        in_specs=[pl.BlockSpec((ts, LANE), lambda i: (i, A_BLOCK)),
                  pl.BlockSpec((LANE, GLA_KEY_DIM), lambda i: (0, 0)),
                  pl.BlockSpec((1, GLA_KEY_DIM), lambda i: (0, 0)),
                  pl.BlockSpec((ts, GLA_KEY_DIM), lambda i: (i, 0))],
        out_specs=[pl.BlockSpec((ts, LANE), lambda i: (i, 0)),
                   pl.BlockSpec((LANE, GLA_KEY_DIM), lambda i: (0, 0)),
                   pl.BlockSpec((1, GLA_KEY_DIM), lambda i: (0, 0))],
        compiler_params=_params("arbitrary"),
    )(proj, w_a2p, b_a2, dla)


def _chunk_terms(q_ref, k_ref, la_ref):
    c_len = GLA_CHUNK
    row = lax.broadcasted_iota(jnp.int32, (c_len, c_len), 0)
    col = lax.broadcasted_iota(jnp.int32, (c_len, c_len), 1)
    tri = row >= col
    la = la_ref[...]
    c = jnp.dot(tri.astype(F32), la, preferred_element_type=F32, precision=lax.Precision.HIGHEST)
    last = jnp.sum(la, axis=0, keepdims=True)
    k = k_ref[...]
    q_dec = q_ref[...] * (GLA_DK ** -0.5) * jnp.exp(c)
    k_inv = k * jnp.exp(-c)
    k_end = k * jnp.exp(last - c)
    return c, last, q_dec, k_inv, k_end, tri


def _dot(a, b, ca, cb):
    return lax.dot_general(a.astype(BF16), b.astype(BF16), (((ca,), (cb,)), ((), ())), preferred_element_type=F32)


def gla_fwd(proj, la):
    s = proj.shape[0]
    n_chunks = s // GLA_CHUNK
    kb = GLA_KEY_DIM // GLA_DK

    def body(q_ref, k_ref, v_ref, la_ref, o_ref, st_out, st):
        @pl.when(pl.program_id(1) == 0)
        def _():
            st[...] = jnp.zeros_like(st)

        _, last, q_dec, k_inv, k_end, tri = _chunk_terms(q_ref, k_ref, la_ref)
        v = v_ref[...]
        a = jnp.where(tri, _dot(q_dec, k_inv, 1, 1), 0.0)
        state = st[...]
        st_out[...] = state
        o_ref[...] = _dot(a, v, 1, 0) + _dot(q_dec, state, 1, 1)
        st[...] = state * jnp.exp(last) + _dot(v, k_end, 0, 0)

    return pl.pallas_call(
        body,
        name="gla_fwd",
        out_shape=[jax.ShapeDtypeStruct((s, GLA_VAL_DIM), F32),
                   jax.ShapeDtypeStruct((GLA_HEADS, n_chunks, GLA_DV, GLA_DK), F32)],
        grid=(GLA_HEADS, n_chunks),
        in_specs=[pl.BlockSpec((GLA_CHUNK, GLA_DK), lambda h, n: (n, h)),
                  pl.BlockSpec((GLA_CHUNK, GLA_DK), lambda h, n: (n, kb + h)),
                  pl.BlockSpec((GLA_CHUNK, GLA_DV), lambda h, n: (n, kb + h)),
                  pl.BlockSpec((GLA_CHUNK, GLA_DK), lambda h, n: (n, h))],
        out_specs=[pl.BlockSpec((GLA_CHUNK, GLA_DV), lambda h, n: (n, h)),
                   pl.BlockSpec((None, None, GLA_DV, GLA_DK), lambda h, n: (h, n, 0, 0))],
        scratch_shapes=[pltpu.VMEM((GLA_DV, GLA_DK), F32)],
        compiler_params=_params("parallel", "arbitrary"),
    )(proj, proj, proj, la)


def gla_bwd(proj, la, states, do):
    s = proj.shape[0]
    n_chunks = s // GLA_CHUNK
    kb = GLA_KEY_DIM // GLA_DK
    lastc = n_chunks - 1

    def body(q_ref, k_ref, v_ref, la_ref, do_ref, st_ref, dq_ref, dk_ref, dv_ref, dla_ref, dst):
        @pl.when(pl.program_id(1) == 0)
        def _():
            dst[...] = jnp.zeros_like(dst)

        c, last, q_dec, k_inv, k_end, tri = _chunk_terms(q_ref, k_ref, la_ref)
        v = v_ref[...]
        dout = do_ref[...]
        state = st_ref[...]
        dstate = dst[...]
        e_last = jnp.exp(last)
        a = jnp.where(tri, _dot(q_dec, k_inv, 1, 1), 0.0)
        da = jnp.where(tri, _dot(dout, v, 1, 1), 0.0)
        dv_ref[...] = (_dot(a, dout, 0, 0) + _dot(k_end, dstate, 1, 1)).astype(BF16)
        dq_dec = _dot(da, k_inv, 1, 0) + _dot(dout, state, 1, 0)
        dk_inv = _dot(da, q_dec, 0, 0)
        dk_end = _dot(v, dstate, 1, 0)
        dst[...] = dstate * e_last + _dot(dout, q_dec, 0, 0)
        e_c = jnp.exp(c)
        dq_ref[...] = (dq_dec * (GLA_DK ** -0.5) * e_c).astype(BF16)
        dk_ref[...] = (dk_inv * jnp.exp(-c) + dk_end * jnp.exp(last - c)).astype(BF16)
        ke_term = dk_end * k_end
        dc = dq_dec * q_dec - dk_inv * k_inv - ke_term
        dlast = jnp.sum(ke_term, axis=0, keepdims=True) + e_last * jnp.sum(dstate * state, axis=0, keepdims=True)
        upper = lax.broadcasted_iota(jnp.int32, tri.shape, 0) <= lax.broadcasted_iota(jnp.int32, tri.shape, 1)
        dla_ref[...] = jnp.dot(upper.astype(F32), dc, preferred_element_type=F32,
                               precision=lax.Precision.HIGHEST) + dlast

    return pl.pallas_call(
        body,
        name="gla_bwd",
        out_shape=[jax.ShapeDtypeStruct((s, GLA_KEY_DIM), BF16), jax.ShapeDtypeStruct((s, GLA_KEY_DIM), BF16),
                   jax.ShapeDtypeStruct((s, GLA_VAL_DIM), BF16), jax.ShapeDtypeStruct((s, GLA_KEY_DIM), F32)],
        grid=(GLA_HEADS, n_chunks),
        in_specs=[pl.BlockSpec((GLA_CHUNK, GLA_DK), lambda h, n: (lastc - n, h)),
                  pl.BlockSpec((GLA_CHUNK, GLA_DK), lambda h, n: (lastc - n, kb + h)),
                  pl.BlockSpec((GLA_CHUNK, GLA_DV), lambda h, n: (lastc - n, kb + h)),
                  pl.BlockSpec((GLA_CHUNK, GLA_DK), lambda h, n: (lastc - n, h)),
                  pl.BlockSpec((GLA_CHUNK, GLA_DV), lambda h, n: (lastc - n, h)),
                  pl.BlockSpec((None, None, GLA_DV, GLA_DK), lambda h, n: (h, lastc - n, 0, 0))],
        out_specs=[pl.BlockSpec((GLA_CHUNK, GLA_DK), lambda h, n: (lastc - n, h)),
                   pl.BlockSpec((GLA_CHUNK, GLA_DK), lambda h, n: (lastc - n, h)),
                   pl.BlockSpec((GLA_CHUNK, GLA_DV), lambda h, n: (lastc - n, h)),
                   pl.BlockSpec((GLA_CHUNK, GLA_DK), lambda h, n: (lastc - n, h))],
        scratch_shapes=[pltpu.VMEM((GLA_DV, GLA_DK), F32)],
        compiler_params=_params("parallel", "arbitrary"),
    )(proj, proj, proj, la, do, states)


R_BLOCK = (2 * GLA_KEY_DIM + GLA_VAL_DIM) // GLA_DV


def headnorm_fwd(o, proj, hn, *, ts=512):
    s = o.shape[0]

    def body(o_ref, r_ref, g_ref, out_ref):
        ov = o_ref[...]
        oh = ov * lax.rsqrt(jnp.mean(ov * ov, axis=-1, keepdims=True) + EPS)
        r = r_ref[...]
        out_ref[...] = (oh * g_ref[...] * (r * jax.nn.sigmoid(r))).astype(BF16)

    return pl.pallas_call(
        body,
        name="headnorm_fwd",
        out_shape=jax.ShapeDtypeStruct((s, GLA_VAL_DIM), BF16),
        grid=(s // ts, GLA_HEADS),
        in_specs=[pl.BlockSpec((ts, GLA_DV), lambda i, h: (i, h)),
                  pl.BlockSpec((ts, GLA_DV), lambda i, h: (i, R_BLOCK + h)),
                  pl.BlockSpec((1, GLA_DV), lambda i, h: (0, 0))],
        out_specs=pl.BlockSpec((ts, GLA_DV), lambda i, h: (i, h)),
        compiler_params=_params("parallel", "parallel"),
    )(o, proj, hn)


def headnorm_bwd(o, proj, hn, dog, *, ts=512):
    s = o.shape[0]

    def body(o_ref, r_ref, g_ref, dog_ref, do_ref, dr_ref, dg_ref):
        @pl.when((pl.program_id(0) == 0) & (pl.program_id(1) == 0))
        def _():
            dg_ref[...] = jnp.zeros_like(dg_ref)

        ov = o_ref[...]
        rr = lax.rsqrt(jnp.mean(ov * ov, axis=-1, keepdims=True) + EPS)
        oh = ov * rr
        g = g_ref[...]
        r = r_ref[...]
        sig = jax.nn.sigmoid(r)
        gate = r * sig
        dog_v = dog_ref[...]
        d_on = dog_v * gate
        dr_ref[...] = (dog_v * (oh * g) * (sig * (1.0 + r * (1.0 - sig)))).astype(BF16)
        dg_ref[...] += jnp.sum(d_on * oh, axis=0, keepdims=True)
        doh = d_on * g
        do_ref[...] = rr * (doh - oh * jnp.mean(doh * oh, axis=-1, keepdims=True))

    return pl.pallas_call(
        body,
        name="headnorm_bwd",
        out_shape=[jax.ShapeDtypeStruct((s, GLA_VAL_DIM), F32), jax.ShapeDtypeStruct((s, GLA_VAL_DIM), BF16),
                   jax.ShapeDtypeStruct((1, GLA_DV), F32)],
        grid=(s // ts, GLA_HEADS),
        in_specs=[pl.BlockSpec((ts, GLA_DV), lambda i, h: (i, h)),
                  pl.BlockSpec((ts, GLA_DV), lambda i, h: (i, R_BLOCK + h)),
                  pl.BlockSpec((1, GLA_DV), lambda i, h: (0, 0)),
                  pl.BlockSpec((ts, GLA_DV), lambda i, h: (i, h))],
        out_specs=[pl.BlockSpec((ts, GLA_DV), lambda i, h: (i, h)),
                   pl.BlockSpec((ts, GLA_DV), lambda i, h: (i, h)),
                   pl.BlockSpec((1, GLA_DV), lambda i, h: (0, 0))],
        compiler_params=_params("arbitrary", "arbitrary"),
    )(o, proj, hn, dog)


CONV_TC = 128
SQRT_HALF = 0.7071067811865476
INV_SQRT_2PI = 0.3989422804014327


def _conv_gate(g_ref, cw_ref, cb_ref):
    g0 = g_ref[...].astype(F32)
    t = lax.broadcasted_iota(jnp.int32, g0.shape, 0)
    g1 = jnp.where(t >= 1, pltpu.roll(g0, 1, 0), 0.0)
    g2 = jnp.where(t >= 2, pltpu.roll(g0, 2, 0), 0.0)
    gc = cw_ref[0:1, :] * g2 + cw_ref[1:2, :] * g1 + cw_ref[2:3, :] * g0 + cb_ref[...]
    return g0, g1, g2, gc, t


def convglu_fwd(up, conv_w, conv_b):
    s = up.shape[0]
    nc = D_FF // CONV_TC

    def body(u_ref, g_ref, cw_ref, cb_ref, o_ref):
        _, _, _, gc, _ = _conv_gate(g_ref, cw_ref, cb_ref)
        gelu = 0.5 * gc * (1.0 + lax.erf(gc * SQRT_HALF))
        o_ref[...] = (gelu * u_ref[...].astype(F32)).astype(BF16)

    return pl.pallas_call(
        body,
        name="convglu_fwd",
        out_shape=jax.ShapeDtypeStruct((s, D_FF), BF16),
        grid=(nc,),
        in_specs=[pl.BlockSpec((s, CONV_TC), lambda c: (0, c)),
                  pl.BlockSpec((s, CONV_TC), lambda c: (0, nc + c)),
                  pl.BlockSpec((3, CONV_TC), lambda c: (0, c)),
                  pl.BlockSpec((1, CONV_TC), lambda c: (0, c))],
        out_specs=pl.BlockSpec((s, CONV_TC), lambda c: (0, c)),
        compiler_params=_params("parallel"),
    )(up, up, conv_w, conv_b)


def convglu_bwd(up, conv_w, conv_b, dact):
    s = up.shape[0]
    nc = D_FF // CONV_TC

    def body(u_ref, g_ref, cw_ref, cb_ref, da_ref, du_ref, dg_ref, dcw_ref, dcb_ref):
        g0, g1, g2, gc, t = _conv_gate(g_ref, cw_ref, cb_ref)
        cdf = 0.5 * (1.0 + lax.erf(gc * SQRT_HALF))
        da = da_ref[...].astype(F32)
        du_ref[...] = (da * gc * cdf).astype(BF16)
        dgc = da * u_ref[...].astype(F32) * (cdf + gc * jnp.exp(-0.5 * gc * gc) * INV_SQRT_2PI)
        dcb_ref[...] = jnp.sum(dgc, axis=0, keepdims=True)
        dcw_ref[0:1, :] = jnp.sum(dgc * g2, axis=0, keepdims=True)
        dcw_ref[1:2, :] = jnp.sum(dgc * g1, axis=0, keepdims=True)
        dcw_ref[2:3, :] = jnp.sum(dgc * g0, axis=0, keepdims=True)
        n1 = jnp.where(t < s - 1, pltpu.roll(dgc, s - 1, 0), 0.0)
        n2 = jnp.where(t < s - 2, pltpu.roll(dgc, s - 2, 0), 0.0)
        dg_ref[...] = (cw_ref[2:3, :] * dgc + cw_ref[1:2, :] * n1 + cw_ref[0:1, :] * n2).astype(BF16)

    col = pl.BlockSpec((s, CONV_TC), lambda c: (0, c))
    return pl.pallas_call(
        body,
        name="convglu_bwd",
        out_shape=[jax.ShapeDtypeStruct((s, D_FF), BF16), jax.ShapeDtypeStruct((s, D_FF), BF16),
                   jax.ShapeDtypeStruct((3, D_FF), F32), jax.ShapeDtypeStruct((1, D_FF), F32)],
        grid=(nc,),
        in_specs=[col, pl.BlockSpec((s, CONV_TC), lambda c: (0, nc + c)),
                  pl.BlockSpec((3, CONV_TC), lambda c: (0, c)),
                  pl.BlockSpec((1, CONV_TC), lambda c: (0, c)), col],
        out_specs=[col, col, pl.BlockSpec((3, CONV_TC), lambda c: (0, c)),
                   pl.BlockSpec((1, CONV_TC), lambda c: (0, c))],
        compiler_params=_params("parallel"),
    )(up, up, conv_w, conv_b, dact)


def _branch_scalars(g, s):
    dil = jnp.where(g == 0, float(DILATIONS[0]), jnp.where(g == 1, float(DILATIONS[1]), float(DILATIONS[2])))
    nb = jnp.where(g == 0, s // DILATIONS[0] // ATT_BLOCK,
                   jnp.where(g == 1, s // DILATIONS[1] // ATT_BLOCK, s // DILATIONS[2] // ATT_BLOCK))
    return dil, nb


def _head(ref, h):
    return ref[:, h * HEAD_DIM:(h + 1) * HEAD_DIM]


def attn_fwd(qp, kp, vp):
    _, s, w = qp.shape
    nblk = s // ATT_BLOCK
    scale = HEAD_DIM ** -0.5

    def body(q_ref, kp_ref, kc_ref, vp_ref, vc_ref, o_ref, l_ref):
        g = pl.program_id(0)
        b = pl.program_id(1)
        dil, nb = _branch_scalars(g, s)
        prev_max = jnp.where(lax.rem(b, nb) != 0, 0, -ATT_BLOCK - 1)
        qa = lax.broadcasted_iota(jnp.int32, (ATT_BLOCK, ATT_BLOCK), 0)
        kc = lax.broadcasted_iota(jnp.int32, (ATT_BLOCK, ATT_BLOCK), 1)
        jd = qa - kc
        jdf = jd.astype(F32)
        ok_c = jd >= 0
        ok_p = jd <= prev_max
        for h in range(ATT_HEADS):
            sl = ALIBI_SLOPES[h] * dil
            q = _head(q_ref, h)
            s_c = jnp.where(ok_c, _dot(q, _head(kc_ref, h), 1, 1) * scale - sl * jdf, NEG)
            s_p = jnp.where(ok_p, _dot(q, _head(kp_ref, h), 1, 1) * scale - sl * (jdf + ATT_BLOCK), NEG)
            m = jnp.maximum(jnp.max(s_c, axis=-1, keepdims=True), jnp.max(s_p, axis=-1, keepdims=True))
            p_c = jnp.exp(s_c - m)
            p_p = jnp.exp(s_p - m)
            l = jnp.sum(p_c, axis=-1, keepdims=True) + jnp.sum(p_p, axis=-1, keepdims=True)
            acc = _dot(p_p, _head(vp_ref, h), 1, 0) + _dot(p_c, _head(vc_ref, h), 1, 0)
            o_ref[:, h * HEAD_DIM:(h + 1) * HEAD_DIM] = acc / l
            l_ref[:, h * HEAD_DIM:(h + 1) * HEAD_DIM] = jnp.broadcast_to(m + jnp.log(l), (ATT_BLOCK, HEAD_DIM))

    cur = pl.BlockSpec((None, ATT_BLOCK, w), lambda g, b: (g, b, 0))
    prev = pl.BlockSpec((None, ATT_BLOCK, w), lambda g, b: (g, jnp.maximum(b - 1, 0), 0))
    return pl.pallas_call(
        body,
        name="attn_fwd",
        out_shape=[jax.ShapeDtypeStruct((3, s, w), F32), jax.ShapeDtypeStruct((3, s, w), F32)],
        grid=(3, nblk),
        in_specs=[cur, prev, cur, prev, cur],
        out_specs=[cur, cur],
        compiler_params=_params("parallel", "parallel"),
    )(qp, kp, kp, vp, vp)


def attn_bwd(qp, kp, vp, dop, lsep, deltap):
    _, s, w = qp.shape
    nblk = s // ATT_BLOCK
    scale = HEAD_DIM ** -0.5

    def body(k_ref, v_ref, qc_ref, doc_ref, lc_ref, dc_ref, qn_ref, don_ref, ln_ref, dn_ref,
             dq_ref, dk_ref, dv_ref, carry):
        g = pl.program_id(0)
        b = pl.program_id(1)

        @pl.when(b == 0)
        def _():
            carry[...] = jnp.zeros_like(carry)

        dil, nb = _branch_scalars(g, s)
        next_max = jnp.where((b + 1 < nblk) & (lax.rem(b + 1, nb) != 0), 0, -ATT_BLOCK - 1)
        qa = lax.broadcasted_iota(jnp.int32, (ATT_BLOCK, ATT_BLOCK), 0)
        kc = lax.broadcasted_iota(jnp.int32, (ATT_BLOCK, ATT_BLOCK), 1)
        jd = qa - kc
        jdf = jd.astype(F32)
        ok_c = jd >= 0
        ok_n = jd <= next_max
        for h in range(ATT_HEADS):
            sl = ALIBI_SLOPES[h] * dil
            hs = slice(h * HEAD_DIM, (h + 1) * HEAD_DIM)
            k = k_ref[:, hs]
            v = v_ref[:, hs]
            q_c = qc_ref[:, hs]
            do_c = doc_ref[:, hs]
            s_c = jnp.where(ok_c, _dot(q_c, k, 1, 1) * scale - sl * jdf, NEG)
            p_c = jnp.exp(s_c - lc_ref[:, hs])
            ds_c = p_c * (_dot(do_c, v, 1, 1) - dc_ref[:, hs])
            q_n = qn_ref[:, hs]
            do_n = don_ref[:, hs]
            s_n = jnp.where(ok_n, _dot(q_n, k, 1, 1) * scale - sl * (jdf + ATT_BLOCK), NEG)
            p_n = jnp.exp(s_n - ln_ref[:, hs])
            ds_n = p_n * (_dot(do_n, v, 1, 1) - dn_ref[:, hs])
            dv_ref[:, hs] = _dot(p_c, do_c, 0, 0) + _dot(p_n, do_n, 0, 0)
            dk_ref[:, hs] = (_dot(ds_c, q_c, 0, 0) + _dot(ds_n, q_n, 0, 0)) * scale
            dq_ref[:, hs] = (carry[:, hs] + _dot(ds_c, k, 1, 0) * scale).astype(BF16)
            carry[:, hs] = _dot(ds_n, k, 1, 0) * scale

    cur = pl.BlockSpec((None, ATT_BLOCK, w), lambda g, b: (g, b, 0))
    nxt = pl.BlockSpec((None, ATT_BLOCK, w), lambda g, b: (g, jnp.minimum(b + 1, nblk - 1), 0))
    return pl.pallas_call(
        body,
        name="attn_bwd",
        out_shape=[jax.ShapeDtypeStruct((3, s, w), BF16), jax.ShapeDtypeStruct((3, s, w), F32),
                   jax.ShapeDtypeStruct((3, s, w), F32)],
        grid=(3, nblk),
        in_specs=[cur, cur, cur, cur, cur, cur, nxt, nxt, nxt, nxt],
        out_specs=[cur, cur, cur],
        scratch_shapes=[pltpu.VMEM((ATT_BLOCK, w), F32)],
        compiler_params=_params("parallel", "arbitrary"),
    )(kp, vp, qp, dop, lsep, deltap, qp, dop, lsep, deltap)


def attn_merge(o3, l3, *, ts=256):
    _, s, w = o3.shape

    def body(o_ref, l_ref, out_ref, lse_ref):
        l0, l1, l2 = l_ref[0], l_ref[1], l_ref[2]
        m = jnp.maximum(jnp.maximum(l0, l1), l2)
        e0, e1, e2 = jnp.exp(l0 - m), jnp.exp(l1 - m), jnp.exp(l2 - m)
        den = e0 + e1 + e2
        out_ref[...] = (e0 * o_ref[0] + e1 * o_ref[1] + e2 * o_ref[2]) / den
        lse_ref[...] = m + jnp.log(den)

    blk3 = pl.BlockSpec((3, ts, w), lambda i: (0, i, 0))
    blk = pl.BlockSpec((ts, w), lambda i: (i, 0))
    return pl.pallas_call(
        body,
        name="attn_merge",
        out_shape=[jax.ShapeDtypeStruct((s, w), F32), jax.ShapeDtypeStruct((s, w), F32)],
        grid=(s // ts,),
        in_specs=[blk3, blk3],
        out_specs=[blk, blk],
        compiler_params=_params("parallel"),
    )(o3, l3)


def attn_delta(do, o, *, ts=512):
    s, w = do.shape

    def body(do_ref, o_ref, d_ref, dob_ref):
        dob_ref[...] = do_ref[...].astype(BF16)
        for h in range(ATT_HEADS):
            hs = slice(h * HEAD_DIM, (h + 1) * HEAD_DIM)
            d = jnp.sum(do_ref[:, hs] * o_ref[:, hs], axis=-1, keepdims=True)
            d_ref[:, hs] = jnp.broadcast_to(d, (ts, HEAD_DIM))

    blk = pl.BlockSpec((ts, w), lambda i: (i, 0))
    return pl.pallas_call(
        body,
        name="attn_delta",
        out_shape=[jax.ShapeDtypeStruct((s, w), F32), jax.ShapeDtypeStruct((s, w), BF16)],
        grid=(s // ts,),
        in_specs=[blk, blk],
        out_specs=[blk, blk],
        compiler_params=_params("parallel"),
    )(do, o)


def kv_grad_sum(dk3, dv3, *, ts=256):
    _, s, w = dk3.shape

    def body(dk_ref, dv_ref, o_ref):
        o_ref[:, 0:w] = (dk_ref[0] + dk_ref[1] + dk_ref[2]).astype(BF16)
        o_ref[:, w:2 * w] = (dv_ref[0] + dv_ref[1] + dv_ref[2]).astype(BF16)

    blk3 = pl.BlockSpec((3, ts, w), lambda i: (0, i, 0))
    return pl.pallas_call(
        body,
        name="kv_grad_sum",
        out_shape=jax.ShapeDtypeStruct((s, 2 * w), BF16),
        grid=(s // ts,),
        in_specs=[blk3, blk3],
        out_specs=pl.BlockSpec((ts, 2 * w), lambda i: (i, 0)),
        compiler_params=_params("parallel"),
    )(dk3, dv3)


def _to_branch(t, d):
    s, w = t.shape
    return t.reshape(s // d, d, w).transpose(1, 0, 2).reshape(s, w)


def _from_branch(t, d):
    s, w = t.shape
    return t.reshape(d, s // d, w).transpose(1, 0, 2).reshape(s, w)


def _branches(t):
    return jnp.stack([_to_branch(t, d) for d in DILATIONS])


def _natural(t3):
    return jnp.stack([_from_branch(t3[g], d) for g, d in enumerate(DILATIONS)])


def _adam(w, g, m, v):
    m = ADAM_B1 * m + (1.0 - ADAM_B1) * g
    v = ADAM_B2 * v + (1.0 - ADAM_B2) * (g * g)
    m_hat = m / (1.0 - ADAM_B1 ** ADAM_STEP)
    v_hat = v / (1.0 - ADAM_B2 ** ADAM_STEP)
    delta = -ADAM_LR * (m_hat / (jnp.sqrt(v_hat) + ADAM_EPS) + ADAM_WD * w)
    return delta, m, v


def adam_sharded(recv, w, m, v, *, name):
    n_src, layers, r, c = recv.shape
    tr = _rows(r, c)

    def body(p_ref, w_ref, m_ref, v_ref, g_ref, d_ref, mo_ref, vo_ref):
        g = p_ref[0].astype(F32)
        for src in range(1, n_src):
            g = g + p_ref[src].astype(F32)
        delta, m_new, v_new = _adam(w_ref[...], g, m_ref[...], v_ref[...])
        g_ref[...] = g
        d_ref[...] = delta
        mo_ref[...] = m_new
        vo_ref[...] = v_new

    blk = pl.BlockSpec((None, tr, c), lambda l, i: (l, i, 0))
    out = jax.ShapeDtypeStruct((layers, r, c), F32)
    return pl.pallas_call(
        body,
        name=name,
        out_shape=[out] * 4,
        grid=(layers, r // tr),
        in_specs=[pl.BlockSpec((n_src, None, tr, c), lambda l, i: (0, l, i, 0)), blk, blk, blk],
        out_specs=[blk] * 4,
        compiler_params=_params("parallel", "parallel"),
    )(recv, w, m, v)


def sum_partials(parts):
    n_src, r, c = parts.shape

    def body(p_ref, o_ref):
        g = p_ref[0]
        for src in range(1, n_src):
            g = g + p_ref[src]
        o_ref[...] = g

    return pl.pallas_call(
        body,
        name="sum_small_grads",
        out_shape=jax.ShapeDtypeStruct((r, c), F32),
    )(parts)


def adam_packed(w, g, m, v):
    def body(w_ref, g_ref, m_ref, v_ref, d_ref, mo_ref, vo_ref):
        delta, m_new, v_new = _adam(w_ref[...], g_ref[...], m_ref[...], v_ref[...])
        d_ref[...] = delta
        mo_ref[...] = m_new
        vo_ref[...] = v_new

    out = jax.ShapeDtypeStruct(w.shape, F32)
    return pl.pallas_call(body, name="adam_small", out_shape=[out] * 3)(w, g, m, v)


def _flip(coord, bit):
    return 1 - coord if bit else coord


def _place():
    x, y, c = lax.axis_index("x"), lax.axis_index("y"), lax.axis_index("c")
    return x, y, c, 4 * x + 2 * y + c


def all_gather(srcs, *, name):
    n = len(srcs)

    def body(*refs):
        src, dst = refs[:n], refs[n:2 * n]
        send_sems, recv_sems, local_sems = refs[2 * n:]
        x, y, c, me = _place()
        sibling = (x, y, 1 - c)
        chips = [(1 - x, y), (x, 1 - y), (1 - x, 1 - y)]

        def index(px, py, pc):
            return 4 * px + 2 * py + pc

        def copy(p, k, block, to, from_src=False):
            slot = dst[p].at[index(*block)]
            return pltpu.make_async_remote_copy(
                src_ref=src[p] if from_src else slot, dst_ref=slot,
                send_sem=send_sems.at[p, k], recv_sem=recv_sems.at[p, k],
                device_id=to, device_id_type=MESH)

        mine = [pltpu.make_async_copy(src[p], dst[p].at[me], local_sems.at[p]) for p in range(n)]
        for cp in mine:
            cp.start()
        first = []
        for p in range(n):
            first.append(copy(p, 0, (x, y, c), sibling, from_src=True))
            for jj, chip in enumerate(chips):
                first.append(copy(p, 1 + jj, (x, y, c), (*chip, c), from_src=True))
        for cp in first:
            cp.start()
        passed = []
        for jj, chip in enumerate(chips):
            for p in range(n):
                copy(p, 1 + jj, (*chip, c), (x, y, c)).wait_recv()
                fwd = copy(p, 4 + jj, (*chip, c), sibling)
                fwd.start()
                passed.append(fwd)
        for p in range(n):
            copy(p, 0, sibling, (x, y, c)).wait_recv()
            for jj, chip in enumerate(chips):
                copy(p, 4 + jj, (*chip, 1 - c), (x, y, c)).wait_recv()
        for cp in first + passed:
            cp.wait_send()
        for cp in mine:
            cp.wait()

    return pl.pallas_call(
        body,
        name=name,
        out_shape=[jax.ShapeDtypeStruct((N_DEV,) + a.shape, a.dtype) for a in srcs],
        in_specs=[ANY] * n,
        out_specs=[ANY] * n,
        scratch_shapes=[pltpu.SemaphoreType.DMA((n, 7)), pltpu.SemaphoreType.DMA((n, 7)),
                        pltpu.SemaphoreType.DMA((n,))],
    )(*srcs)


def scatter_grads(srcs, slots, out_shapes, *, name):
    n = len(srcs)
    n_out = len(out_shapes)

    def body(*refs):
        src, dst = refs[:n], refs[n:n + n_out]
        send_sems, recv_sems, local_sems = refs[n + n_out:]
        x, y, c, me = _place()

        def slot(q, dev):
            p, layer = slots[q]
            return dst[p].at[dev, layer]

        mine = [pltpu.make_async_copy(src[q].at[me], slot(q, me), local_sems.at[q]) for q in range(n)]
        for cp in mine:
            cp.start()
        sends = []
        for k in range(1, N_DEV):
            px, py, pc = _flip(x, k >> 2), _flip(y, (k >> 1) & 1), _flip(c, k & 1)
            peer = 4 * px + 2 * py + pc
            for q in range(n):
                cp = pltpu.make_async_remote_copy(
                    src_ref=src[q].at[peer], dst_ref=slot(q, me),
                    send_sem=send_sems.at[q, k - 1], recv_sem=recv_sems.at[q, k - 1],
                    device_id=(px, py, pc), device_id_type=MESH)
                cp.start()
                sends.append(cp)
        for k in range(1, N_DEV):
            px, py, pc = _flip(x, k >> 2), _flip(y, (k >> 1) & 1), _flip(c, k & 1)
            peer = 4 * px + 2 * py + pc
            for q in range(n):
                pltpu.make_async_remote_copy(
                    src_ref=src[q].at[peer], dst_ref=slot(q, peer),
                    send_sem=send_sems.at[q, k - 1], recv_sem=recv_sems.at[q, k - 1],
                    device_id=(px, py, pc), device_id_type=MESH).wait_recv()
        for cp in sends:
            cp.wait_send()
        for cp in mine:
            cp.wait()

    return pl.pallas_call(
        body,
        name=name,
        out_shape=[jax.ShapeDtypeStruct(sh, BF16) for sh in out_shapes],
        in_specs=[ANY] * n,
        out_specs=[ANY] * n_out,
        scratch_shapes=[pltpu.SemaphoreType.DMA((n, 7)), pltpu.SemaphoreType.DMA((n, 7)),
                        pltpu.SemaphoreType.DMA((n,))],
    )(*srcs)


def _pack_rows(parts, rows):
    flat = jnp.concatenate([p.reshape(-1) for p in parts])
    return jnp.pad(flat, (0, rows * LANE - flat.shape[0])).reshape(rows, LANE)


def _unpack_rows(packed, shapes):
    flat = packed.reshape(-1)
    out, at = [], 0
    for sh in shapes:
        size = 1
        for dim in sh:
            size *= dim
        out.append(flat[at:at + size].reshape(sh))
        at += size
    return out


CONV_W_PAD = 768
SMALL_W_ROWS = 56


def _pack_small_weights(w_a2, b_a2, hn, conv_w):
    cw = jnp.pad(conv_w.reshape(6, -1), ((0, 0), (0, CONV_W_PAD - conv_w.shape[-1]))).reshape(-1, LANE)
    rows = jnp.concatenate([w_a2[0], b_a2, jnp.pad(hn, ((0, 0), (0, LANE - hn.shape[-1]))), cw], axis=0)
    return jnp.pad(rows, ((0, SMALL_W_ROWS - rows.shape[0]), (0, 0)))


def _unpack_small_weights(gathered):
    w_a2 = gathered[:, 0:GATE_RANK, :].transpose(1, 0, 2).reshape(GATE_RANK, GLA_KEY_DIM)
    b_a2 = gathered[:, GATE_RANK, :].reshape(1, GLA_KEY_DIM)
    hn = gathered[:, GATE_RANK + 1, :GLA_DV // N_DEV].reshape(1, GLA_DV)
    per = D_FF // N_DEV
    cw = gathered[:, GATE_RANK + 2:GATE_RANK + 2 + 6 * CONV_W_PAD // LANE, :].reshape(N_DEV, 6, CONV_W_PAD)[:, :, :per]
    cw = cw.reshape(N_DEV, 2, 3, per).transpose(1, 2, 0, 3).reshape(2, 3, D_FF)
    return w_a2, b_a2, hn, cw


def _ffn_fwd(h, norm_g, w_up, conv_w, conv_b, w_down, tag):
    (n,) = rms_fwd(h, [norm_g], name=f"ffn_norm_fwd{tag}")
    up = mm_nn(n, w_up, out_dtype=BF16, name=f"ffn_up{tag}")
    act = convglu_fwd(up, conv_w, conv_b)
    h_out = mm_nn(act, w_down, out_dtype=F32, res=h, name=f"ffn_down{tag}")
    return h_out, (n, up, act)


def _ffn_bwd(dh_out, h, saved, norm_g, w_up, conv_w, conv_b, w_down, tag):
    n, up, act = saved
    dact = mm_nt(dh_out, w_down, out_dtype=BF16, name=f"ffn_down_dx{tag}")
    dw_down = mm_tn(act, dh_out, 1, name=f"ffn_down_dw{tag}")
    du, dg, dconv_w, dconv_b = convglu_bwd(up, conv_w, conv_b, dact)
    dup = jnp.concatenate([du, dg], axis=1)
    dn = mm_nt(dup, w_up, out_dtype=F32, name=f"ffn_up_dx{tag}")
    dw_up = mm_tn(n, dup, N_DEV, name=f"ffn_up_dw{tag}")
    dh, (dnorm,) = rms_bwd(h, [norm_g], [dn], dh_out, name=f"ffn_norm_bwd{tag}")
    return dh, dnorm, dw_up, dw_down, dconv_w, dconv_b


def local_step(x, target, wts):
    s = x.shape[0]
    row = lambda v: v.reshape(1, -1)
    attn_norm, ffn_norm = wts["attn_norm"], wts["ffn_norm"]
    conv_w, conv_b = wts["ffn_conv_w"], wts["ffn_conv_b"]

    (n1,) = rms_fwd(x, [row(attn_norm[0])], name="attn_norm_fwd0")
    proj = mm_nn(n1, wts["gla_w_in"], out_dtype=F32, name="gla_in")
    la = gate_fwd(proj, wts["gla_w_a2"], wts["gla_b_a2"])
    o_gla, states = gla_fwd(proj, la)
    og = headnorm_fwd(o_gla, proj, wts["gla_head_norm"])
    h1 = mm_nn(og, wts["gla_w_out"], out_dtype=F32, res=x, name="gla_out")
    h2, ffn0 = _ffn_fwd(h1, row(ffn_norm[0]), wts["ffn_w_up0"], conv_w[0], row(conv_b[0]), wts["ffn_w_down0"], "0")

    kvn, n3 = rms_fwd(h2, [row(wts["kv_norm"]), row(attn_norm[1])], name="kv_attn_norm_fwd")
    kv = mm_nn(kvn, wts["w_kv"], out_dtype=BF16, name="kv_proj")
    q = mm_nn(n3, wts["dsa_w_q"], out_dtype=BF16, name="q_proj")
    width = ATT_HEADS * HEAD_DIM
    qp = jnp.stack([_to_branch(q[:, g * width:(g + 1) * width], d) for g, d in enumerate(DILATIONS)])
    kp = _branches(kv[:, :width])
    vp = _branches(kv[:, width:])
    o3, l3 = attn_fwd(qp, kp, vp)
    o_att, lse = attn_merge(_natural(o3), _natural(l3))
    h3 = mm_nn(o_att, wts["dsa_w_out"], out_dtype=F32, res=h2, name="dsa_out")
    h4, ffn1 = _ffn_fwd(h3, row(ffn_norm[1]), wts["ffn_w_up1"], conv_w[1], row(conv_b[1]), wts["ffn_w_down1"], "1")

    loss_tile, dh4, d_final = loss_head(h4, row(wts["final_norm"]), target)

    dh3, d_ffn1, dw_up1, dw_down1, dcw1, dcb1 = _ffn_bwd(
        dh4, h3, ffn1, row(ffn_norm[1]), wts["ffn_w_up1"], conv_w[1], row(conv_b[1]), wts["ffn_w_down1"], "1")
    do_att = mm_nt(dh3, wts["dsa_w_out"], out_dtype=F32, name="dsa_out_dx")
    dw_dsa_out = mm_tn(o_att, dh3, 1, name="dsa_out_dw")
    delta, do_b = attn_delta(do_att, o_att)
    dq3, dk3, dv3 = attn_bwd(qp, kp, vp, _branches(do_b), _branches(lse), _branches(delta))
    dq = jnp.concatenate([_from_branch(dq3[g], d) for g, d in enumerate(DILATIONS)], axis=1)
    dkv = kv_grad_sum(_natural(dk3), _natural(dv3))
    dn3 = mm_nt(dq, wts["dsa_w_q"], out_dtype=F32, name="q_proj_dx")
    dw_q = mm_tn(n3, dq, N_DEV, name="q_proj_dw")
    dkvn = mm_nt(dkv, wts["w_kv"], out_dtype=F32, name="kv_proj_dx")
    dw_kv = mm_tn(kvn, dkv, N_DEV, name="kv_proj_dw")
    dh2, (d_kvnorm, d_attn1) = rms_bwd(h2, [row(wts["kv_norm"]), row(attn_norm[1])], [dkvn, dn3], dh3,
                                       name="kv_attn_norm_bwd")
    dh1, d_ffn0, dw_up0, dw_down0, dcw0, dcb0 = _ffn_bwd(
        dh2, h1, ffn0, row(ffn_norm[0]), wts["ffn_w_up0"], conv_w[0], row(conv_b[0]), wts["ffn_w_down0"], "0")
    dog = mm_nt(dh1, wts["gla_w_out"], out_dtype=F32, name="gla_out_dx")
    dw_gla_out = mm_tn(og, dh1, 1, name="gla_out_dw")
    do_gla, dr, d_hn = headnorm_bwd(o_gla, proj, wts["gla_head_norm"], dog)
    dq_g, dk_g, dv_g, dla = gla_bwd(proj, la, states, do_gla)
    da, dw_a2p, db_a2 = gate_bwd(proj, wts["gla_w_a2"], wts["gla_b_a2"], dla)
    dproj = jnp.concatenate([dq_g, dk_g, dv_g, dr, da], axis=1)
    assert dproj.shape[1] == GLA_IN_PAD
    dn1 = mm_nt(dproj, wts["gla_w_in"], out_dtype=F32, name="gla_in_dx")
    dw_in = mm_tn(n1, dproj, 1, name="gla_in_dw")
    grad_x, (d_attn0,) = rms_bwd(x, [row(attn_norm[0])], [dn1], dh1, name="attn_norm_bwd0")

    big = dict(gla_w_in=dw_in, gla_w_out=dw_gla_out, w_kv=dw_kv, dsa_w_q=dw_q, dsa_w_out=dw_dsa_out,
               ffn_w_up0=dw_up0, ffn_w_up1=dw_up1, ffn_w_down0=dw_down0, ffn_w_down1=dw_down1)
    small = dict(
        attn_norm=jnp.concatenate([d_attn0, d_attn1], axis=0),
        ffn_norm=jnp.concatenate([d_ffn0, d_ffn1], axis=0),
        kv_norm=d_kvnorm.reshape(-1),
        final_norm=d_final.reshape(-1),
        ffn_conv_b=jnp.concatenate([dcb0, dcb1], axis=0),
        gla_w_a2=dw_a2p[:GATE_RANK],
        gla_b_a2=db_a2,
        gla_head_norm=d_hn,
        ffn_conv_w=jnp.stack([dcw0, dcw1]),
    )
    return loss_tile, grad_x, big, small


SMALL_ORDER = ("attn_norm", "ffn_norm", "kv_norm", "final_norm", "ffn_conv_b",
               "gla_w_a2", "gla_b_a2", "gla_head_norm", "ffn_conv_w")
SMALL_FULL = dict(attn_norm=(2, D_MODEL), ffn_norm=(2, D_MODEL), kv_norm=(D_MODEL,), final_norm=(D_MODEL,),
                  ffn_conv_b=(2, D_FF), gla_w_a2=(GATE_RANK, GLA_KEY_DIM), gla_b_a2=(1, GLA_KEY_DIM),
                  gla_head_norm=(1, GLA_DV), ffn_conv_w=(2, 3, D_FF))
SMALL_SHARDED = ("gla_w_a2", "gla_b_a2", "gla_head_norm", "ffn_conv_w")
SMALL_GRAD_ROWS = 592
SMALL_ADAM_ROWS = 240


def kernel(x, attn_norm, gla_w_in, gla_w_a2, gla_b_a2, gla_head_norm, gla_w_out, kv_norm, w_kv, dsa_w_q, dsa_w_out, ffn_norm, ffn_w_up, ffn_conv_w, ffn_conv_b, ffn_w_down, final_norm, loss_target, m_attn_norm, m_gla_w_in, m_gla_w_a2, m_gla_b_a2, m_gla_head_norm, m_gla_w_out, m_kv_norm, m_w_kv, m_dsa_w_q, m_dsa_w_out, m_ffn_norm, m_ffn_w_up, m_ffn_conv_w, m_ffn_conv_b, m_ffn_w_down, m_final_norm, v_attn_norm, v_gla_w_in, v_gla_w_a2, v_gla_b_a2, v_gla_head_norm, v_gla_w_out, v_kv_norm, v_w_kv, v_dsa_w_q, v_dsa_w_out, v_ffn_norm, v_ffn_w_up, v_ffn_conv_w, v_ffn_conv_b, v_ffn_w_down, v_final_norm):
    me = 4 * lax.axis_index("x") + 2 * lax.axis_index("y") + lax.axis_index("c")
    bf = lambda a: a.astype(BF16)

    srcs = [bf(gla_w_in[0]), bf(gla_w_out[0]), bf(w_kv), bf(dsa_w_q[0]), bf(dsa_w_out[0]),
            bf(ffn_w_up[0]), bf(ffn_w_up[1]), bf(ffn_w_down[0]), bf(ffn_w_down[1]),
            _pack_small_weights(gla_w_a2, gla_b_a2, gla_head_norm, ffn_conv_w)]
    g_in, g_gout, g_kv, g_q, g_dout, g_up0, g_up1, g_dn0, g_dn1, g_small = all_gather(srcs, name="gather_weights")
    w_a2_full, b_a2_full, hn_full, conv_w_full = _unpack_small_weights(g_small)
    w_in_full = jnp.pad(g_in.transpose(1, 0, 2).reshape(D_MODEL, GLA_IN_DIM), ((0, 0), (0, GLA_IN_PAD - GLA_IN_DIM)))
    wts = dict(
        attn_norm=attn_norm, ffn_norm=ffn_norm, kv_norm=kv_norm, final_norm=final_norm, ffn_conv_b=ffn_conv_b,
        gla_w_in=w_in_full[None],
        gla_w_a2=jnp.pad(bf(w_a2_full), ((0, LANE - GATE_RANK), (0, 0))),
        gla_b_a2=b_a2_full, gla_head_norm=hn_full, ffn_conv_w=conv_w_full,
        gla_w_out=g_gout.reshape(1, GLA_VAL_DIM, D_MODEL),
        w_kv=g_kv, dsa_w_q=g_q,
        dsa_w_out=g_dout.reshape(1, ATT_HEADS * HEAD_DIM, D_MODEL),
        ffn_w_up0=g_up0, ffn_w_up1=g_up1,
        ffn_w_down0=g_dn0.reshape(1, D_FF, D_MODEL), ffn_w_down1=g_dn1.reshape(1, D_FF, D_MODEL),
    )

    loss_tile, grad_x, big, small = local_step(x[0], loss_target[0], wts)
    loss = lax.psum(loss_tile[0, 0], ("x", "y", "c"))

    dw_in = big["gla_w_in"][0, :, :GLA_IN_DIM].reshape(D_MODEL, N_DEV, GLA_IN_DIM // N_DEV).transpose(1, 0, 2)
    by_rows = lambda a: a.reshape(N_DEV, a.shape[1] // N_DEV, a.shape[2])
    rs_srcs = [dw_in, by_rows(big["gla_w_out"]), big["w_kv"], big["dsa_w_q"], by_rows(big["dsa_w_out"]),
               big["ffn_w_up0"], big["ffn_w_up1"], by_rows(big["ffn_w_down0"]), by_rows(big["ffn_w_down1"])]
    slots = [(0, 0), (1, 0), (2, 0), (3, 0), (4, 0), (5, 0), (5, 1), (6, 0), (6, 1)]
    shard3 = lambda a: a.reshape((-1,) + a.shape[-2:])
    big_params = [(gla_w_in, m_gla_w_in, v_gla_w_in), (gla_w_out, m_gla_w_out, v_gla_w_out),
                  (w_kv, m_w_kv, v_w_kv), (dsa_w_q, m_dsa_w_q, v_dsa_w_q), (dsa_w_out, m_dsa_w_out, v_dsa_w_out),
                  (ffn_w_up, m_ffn_w_up, v_ffn_w_up), (ffn_w_down, m_ffn_w_down, v_ffn_w_down)]
    out_shapes = [(N_DEV,) + shard3(p[0]).shape for p in big_params]
    recvs = scatter_grads(rs_srcs, slots, out_shapes, name="scatter_grads")
    big_names = ("gla_w_in", "gla_w_out", "w_kv", "dsa_w_q", "dsa_w_out", "ffn_w_up", "ffn_w_down")
    res = {}
    for nm, recv, (w, m, v) in zip(big_names, recvs, big_params):
        outs = adam_sharded(recv, shard3(w), shard3(m), shard3(v), name=f"adam_{nm}")
        res[nm] = [o.reshape(w.shape) for o in outs]

    packed = _pack_rows([small[nm] for nm in SMALL_ORDER], SMALL_GRAD_ROWS)
    (parts,) = all_gather([packed], name="gather_small_grads")
    full = dict(zip(SMALL_ORDER, _unpack_rows(sum_partials(parts), [SMALL_FULL[nm] for nm in SMALL_ORDER])))
    local_w = dict(attn_norm=attn_norm, ffn_norm=ffn_norm, kv_norm=kv_norm, final_norm=final_norm,
                   ffn_conv_b=ffn_conv_b, gla_w_a2=gla_w_a2, gla_b_a2=gla_b_a2, gla_head_norm=gla_head_norm,
                   ffn_conv_w=ffn_conv_w)
    local_m = dict(attn_norm=m_attn_norm, ffn_norm=m_ffn_norm, kv_norm=m_kv_norm, final_norm=m_final_norm,
                   ffn_conv_b=m_ffn_conv_b, gla_w_a2=m_gla_w_a2, gla_b_a2=m_gla_b_a2, gla_head_norm=m_gla_head_norm,
                   ffn_conv_w=m_ffn_conv_w)
    local_v = dict(attn_norm=v_attn_norm, ffn_norm=v_ffn_norm, kv_norm=v_kv_norm, final_norm=v_final_norm,
                   ffn_conv_b=v_ffn_conv_b, gla_w_a2=v_gla_w_a2, gla_b_a2=v_gla_b_a2, gla_head_norm=v_gla_head_norm,
                   ffn_conv_w=v_ffn_conv_w)
    local_g = {}
    for nm in SMALL_ORDER:
        gfull = full[nm]
        if nm in SMALL_SHARDED:
            per = gfull.shape[-1] // N_DEV
            gfull = lax.dynamic_slice_in_dim(gfull, me * per, per, axis=gfull.ndim - 1)
        local_g[nm] = gfull.reshape(local_w[nm].shape)
    shapes = [local_w[nm].shape for nm in SMALL_ORDER]
    pk = lambda dd: _pack_rows([dd[nm] for nm in SMALL_ORDER], SMALL_ADAM_ROWS)
    d_p, m_p, v_p = adam_packed(pk(local_w), pk(local_g), pk(local_m), pk(local_v))
    for nm, dl, mn, vn in zip(SMALL_ORDER, _unpack_rows(d_p, shapes), _unpack_rows(m_p, shapes),
                              _unpack_rows(v_p, shapes)):
        res[nm] = [local_g[nm], dl, mn, vn]

    order = ("attn_norm", "gla_w_in", "gla_w_a2", "gla_b_a2", "gla_head_norm", "gla_w_out", "kv_norm", "w_kv",
             "dsa_w_q", "dsa_w_out", "ffn_norm", "ffn_w_up", "ffn_conv_w", "ffn_conv_b", "ffn_w_down", "final_norm")
    outs = [loss, grad_x[None]]
    for kind in range(4):
        outs.extend(res[nm][kind] for nm in order)
    return tuple(outs)
```

```python
import functools

import jax
import jax.numpy as jnp
from jax import lax
from jax.experimental import pallas as pl
from jax.experimental.pallas import tpu as pltpu

F32 = jnp.float32
BF16 = jnp.bfloat16
MESH = pl.DeviceIdType.MESH
ANY = pl.BlockSpec(memory_space=pl.ANY)

N_DEV = 8
D_MODEL = 2048
GLA_HEADS = 4
GLA_KEY_DIM = 1024
GLA_VAL_DIM = 2048
GLA_DK = 256
GLA_DV = 512
GATE_RANK = 16
GATE_NORMALIZER = 16.0
GLA_CHUNK = 64
GLA_IN_DIM = 2 * GLA_KEY_DIM + 2 * GLA_VAL_DIM + GATE_RANK
GLA_IN_PAD = 6272
ATT_HEADS = 16
HEAD_DIM = 128
DILATIONS = (1, 4, 16)
ATT_BLOCK = 128
D_FF = 5632
EPS = 1e-6
ADAM_LR = 0.001
ADAM_B1 = 0.9
ADAM_B2 = 0.999
ADAM_EPS = 1e-08
ADAM_WD = 0.01
ADAM_STEP = 10
NEG = -1e30
LANE = 128
VMEM_LIMIT = 52 * 1024 * 1024
ALIBI_SLOPES = tuple(2.0 ** (-0.5 * (i + 1)) for i in range(ATT_HEADS))


def _params(*sem):
    return pltpu.CompilerParams(dimension_semantics=sem, vmem_limit_bytes=VMEM_LIMIT)


def _tile(n, cap):
    best = None
    for t in range(LANE, min(n, cap) + 1, LANE):
        if n % t == 0:
            best = t
    return best if best is not None else n


def _rows(r, c, budget=256 * 1024):
    best = None
    for t in range(16, r + 1, 16):
        if r % t == 0 and t * c <= budget:
            best = t
    return best if best is not None else r


def _flip(coord, bit):
    return 1 - coord if bit else coord


def _place():
    x, y, c = lax.axis_index("x"), lax.axis_index("y"), lax.axis_index("c")
    return x, y, c, 4 * x + 2 * y + c


def _rows_of(ref, rows):
    return ref if rows is None else ref.at[pl.ds(rows[0], rows[1] - rows[0])]


class Jobs:
    def __init__(self):
        self.srcs = []
        self.bufs = []
        self.sems = []
        self.steps = []

    def _src(self, a):
        for i, b in enumerate(self.srcs):
            if b is a:
                return i
        self.srcs.append(a)
        return len(self.srcs) - 1

    def new(self, shape, dtype):
        self.bufs.append((None, jax.ShapeDtypeStruct(shape, dtype)))
        return len(self.bufs) - 1

    def thru(self, a):
        self.bufs.append((a, jax.ShapeDtypeStruct(a.shape, a.dtype)))
        return len(self.bufs) - 1

    def _sem(self, n):
        self.sems.append(pltpu.SemaphoreType.DMA((n,)))
        return len(self.sems) - 1

    def gather_ici(self, src, buf, rows=None):
        si, send, recv, loc = self._src(src), self._sem(4), self._sem(4), self._sem(1)

        def copies(srcs, bufs, sems):
            x, y, c, me = _place()
            peers = [(x, y, 1 - c), (1 - x, y, c), (x, 1 - y, c), (1 - x, 1 - y, c)]
            mine = _rows_of(srcs[si], rows)
            out = [pltpu.make_async_remote_copy(
                src_ref=mine, dst_ref=_rows_of(bufs[buf].at[me], rows), send_sem=sems[send].at[k],
                recv_sem=sems[recv].at[k], device_id=p, device_id_type=MESH) for k, p in enumerate(peers)]
            arrive = [pltpu.make_async_remote_copy(
                src_ref=mine, dst_ref=_rows_of(bufs[buf].at[4 * p[0] + 2 * p[1] + p[2]], rows),
                send_sem=sems[send].at[k], recv_sem=sems[recv].at[k], device_id=p, device_id_type=MESH)
                for k, p in enumerate(peers)]
            local = pltpu.make_async_copy(mine, _rows_of(bufs[buf].at[me], rows), sems[loc].at[0])
            return out, arrive, local

        def start(srcs, bufs, sems):
            out, _, local = copies(srcs, bufs, sems)
            local.start()
            for cp in out:
                cp.start()

        def finish(srcs, bufs, sems):
            out, arrive, local = copies(srcs, bufs, sems)
            for cp in arrive:
                cp.wait_recv()
            for cp in out:
                cp.wait_send()
            local.wait()

        self.steps.append((start, finish))

    def gather_d2d(self, buf, rows=None):
        send, recv = self._sem(3), self._sem(3)

        def copies(bufs, sems, core):
            x, y, c, _ = _place()
            cc = c if core == "mine" else 1 - c
            chips = [(1 - x, y), (x, 1 - y), (1 - x, 1 - y)]
            return [pltpu.make_async_remote_copy(
                src_ref=_rows_of(bufs[buf].at[4 * px + 2 * py + cc], rows),
                dst_ref=_rows_of(bufs[buf].at[4 * px + 2 * py + cc], rows),
                send_sem=sems[send].at[k], recv_sem=sems[recv].at[k],
                device_id=(x, y, 1 - c), device_id_type=MESH) for k, (px, py) in enumerate(chips)]

        def start(srcs, bufs, sems):
            for cp in copies(bufs, sems, "mine"):
                cp.start()

        def finish(srcs, bufs, sems):
            for cp in copies(bufs, sems, "sibling"):
                cp.wait_recv()
            for cp in copies(bufs, sems, "mine"):
                cp.wait_send()

        self.steps.append((start, finish))

    def scatter(self, src, buf, rows=None):
        si, send, recv, loc = self._src(src), self._sem(N_DEV - 1), self._sem(N_DEV - 1), self._sem(1)

        def copies(srcs, bufs, sems, slot_of):
            x, y, c, me = _place()
            out = []
            for k in range(1, N_DEV):
                px, py, pc = _flip(x, k >> 2), _flip(y, (k >> 1) & 1), _flip(c, k & 1)
                peer = 4 * px + 2 * py + pc
                out.append(pltpu.make_async_remote_copy(
                    src_ref=_rows_of(srcs[si].at[peer], rows),
                    dst_ref=_rows_of(bufs[buf].at[me if slot_of == "mine" else peer], rows),
                    send_sem=sems[send].at[k - 1], recv_sem=sems[recv].at[k - 1],
                    device_id=(px, py, pc), device_id_type=MESH))
            local = pltpu.make_async_copy(_rows_of(srcs[si].at[me], rows), _rows_of(bufs[buf].at[me], rows),
                                          sems[loc].at[0])
            return out, local

        def start(srcs, bufs, sems):
            out, local = copies(srcs, bufs, sems, "mine")
            local.start()
            for cp in out:
                cp.start()

        def finish(srcs, bufs, sems):
            arrive, _ = copies(srcs, bufs, sems, "peer")
            for cp in arrive:
                cp.wait_recv()
            out, local = copies(srcs, bufs, sems, "mine")
            for cp in out:
                cp.wait_send()
            local.wait()

        self.steps.append((start, finish))


def _call(body, *, name, grid, in_specs, out_specs, out_shape, args, sem, scratch_shapes=(), jobs=None):
    in_specs, out_specs, out_shape = list(in_specs), list(out_specs), list(out_shape)
    scratch_shapes = list(scratch_shapes)
    if jobs is None:
        res = pl.pallas_call(body, name=name, out_shape=out_shape, grid=grid, in_specs=in_specs,
                             out_specs=out_specs, scratch_shapes=scratch_shapes,
                             compiler_params=_params(*sem))(*args)
        return list(res), []
    thru = [a for a, _ in jobs.bufs if a is not None]
    n_in, n_src, n_thru = len(args), len(jobs.srcs), len(thru)
    n_out, n_buf, n_scr = len(out_shape), len(jobs.bufs), len(scratch_shapes)
    aliases, t = {}, 0
    for b, (a, _) in enumerate(jobs.bufs):
        if a is not None:
            aliases[n_in + n_src + t] = n_out + b
            t += 1

    def wrapped(*refs):
        at = 0
        ins = refs[at:at + n_in]; at += n_in
        srcs = refs[at:at + n_src]; at += n_src + n_thru
        outs = refs[at:at + n_out]; at += n_out
        bufs = refs[at:at + n_buf]; at += n_buf
        scr = refs[at:at + n_scr]; at += n_scr
        sems = refs[at:]
        first, last = None, None
        for axis, size in enumerate(grid):
            pid = pl.program_id(axis)
            f, l = pid == 0, pid == size - 1
            first = f if first is None else first & f
            last = l if last is None else last & l

        @pl.when(first)
        def _():
            for start, _ in jobs.steps:
                start(srcs, bufs, sems)

        body(*ins, *outs, *scr)

        @pl.when(last)
        def _():
            for _, finish in jobs.steps:
                finish(srcs, bufs, sems)

    res = pl.pallas_call(
        wrapped, name=name,
        out_shape=out_shape + [s for _, s in jobs.bufs],
        grid=grid,
        in_specs=in_specs + [ANY] * (n_src + n_thru),
        out_specs=out_specs + [ANY] * n_buf,
        scratch_shapes=scratch_shapes + jobs.sems,
        input_output_aliases=aliases,
        compiler_params=_params(*(["arbitrary"] * len(grid))),
    )(*args, *jobs.srcs, *thru)
    return res[:n_out], res[n_out:]


def mm_nn(a, w, *, out_dtype, name, res=None, tm=512, jobs=None):
    m, k = a.shape
    j, k2, ns = w.shape
    assert k == k2 and m % tm == 0
    tn = _tile(ns, 1408)
    nsub = ns // tn
    tk = k if k <= 2048 else _tile(k, 1408)
    nk = k // tk
    has_res = res is not None

    def body(*refs):
        if has_res:
            a_ref, w_ref, r_ref, o_ref, acc = refs
        else:
            a_ref, w_ref, o_ref, acc = refs
        kk = pl.program_id(2)

        @pl.when(kk == 0)
        def _():
            acc[...] = jnp.zeros_like(acc)

        acc[...] += jnp.dot(a_ref[...].astype(BF16), w_ref[...], preferred_element_type=F32)

        @pl.when(kk == nk - 1)
        def _():
            r = acc[...]
            if has_res:
                r = r + r_ref[...]
            o_ref[...] = r.astype(out_dtype)

    in_specs = [
        pl.BlockSpec((tm, tk), lambda i, n, kk: (i, kk)),
        pl.BlockSpec((None, tk, tn), lambda i, n, kk: (n // nsub, kk, n % nsub)),
    ]
    args = [a, w]
    if has_res:
        in_specs.append(pl.BlockSpec((tm, tn), lambda i, n, kk: (i, n)))
        args.append(res)
    (out,), bufs = _call(
        body, name=name, jobs=jobs,
        out_shape=[jax.ShapeDtypeStruct((m, j * ns), out_dtype)],
        grid=(m // tm, j * nsub, nk),
        in_specs=in_specs,
        out_specs=[pl.BlockSpec((tm, tn), lambda i, n, kk: (i, n))],
        scratch_shapes=[pltpu.VMEM((tm, tn), F32)],
        args=args, sem=("parallel", "parallel", "arbitrary"))
    return out if jobs is None else (out, bufs)


def mm_nt(dy, w, *, out_dtype, name, tm=512, jobs=None):
    m, n = dy.shape
    j, k, ns = w.shape
    assert n == j * ns and m % tm == 0
    tn = _tile(ns, 2048)
    nsub = ns // tn
    tko = _tile(k, 1408)
    nn = j * nsub

    def body(a_ref, w_ref, o_ref, acc):
        nq = pl.program_id(2)

        @pl.when(nq == 0)
        def _():
            acc[...] = jnp.zeros_like(acc)

        acc[...] += lax.dot_general(a_ref[...].astype(BF16), w_ref[...], (((1,), (1,)), ((), ())),
                                    preferred_element_type=F32)

        @pl.when(nq == nn - 1)
        def _():
            o_ref[...] = acc[...].astype(out_dtype)

    (out,), bufs = _call(
        body, name=name, jobs=jobs,
        out_shape=[jax.ShapeDtypeStruct((m, k), out_dtype)],
        grid=(m // tm, k // tko, nn),
        in_specs=[
            pl.BlockSpec((tm, tn), lambda i, ko, nq: (i, nq)),
            pl.BlockSpec((None, tko, tn), lambda i, ko, nq: (nq // nsub, ko, nq % nsub)),
        ],
        out_specs=[pl.BlockSpec((tm, tko), lambda i, ko, nq: (i, ko))],
        scratch_shapes=[pltpu.VMEM((tm, tko), F32)],
        args=[dy, w], sem=("parallel", "parallel", "arbitrary"))
    return out if jobs is None else (out, bufs)


def mm_tn(x, dy, j, *, name, tm=1024):
    m, k = x.shape
    m2, n = dy.shape
    assert m == m2 and n % j == 0 and m % tm == 0
    ns = n // j
    tn = _tile(ns, 1408)
    nsub = ns // tn
    tk = _tile(k, 1408)
    nm = m // tm

    def body(x_ref, dy_ref, o_ref, acc):
        mi = pl.program_id(2)

        @pl.when(mi == 0)
        def _():
            acc[...] = jnp.zeros_like(acc)

        acc[...] += lax.dot_general(x_ref[...].astype(BF16), dy_ref[...].astype(BF16), (((0,), (0,)), ((), ())),
                                    preferred_element_type=F32)

        @pl.when(mi == nm - 1)
        def _():
            o_ref[...] = acc[...].astype(BF16)

    return pl.pallas_call(
        body,
        name=name,
        out_shape=jax.ShapeDtypeStruct((j, k, ns), BF16),
        grid=(k // tk, j * nsub, nm),
        in_specs=[
            pl.BlockSpec((tm, tk), lambda kq, nq, mi: (mi, kq)),
            pl.BlockSpec((tm, tn), lambda kq, nq, mi: (mi, nq)),
        ],
        out_specs=pl.BlockSpec((None, tk, tn), lambda kq, nq, mi: (nq // nsub, kq, nq % nsub)),
        scratch_shapes=[pltpu.VMEM((tk, tn), F32)],
        compiler_params=_params("parallel", "parallel", "arbitrary"),
    )(x, dy)


def rms_fwd(x, gains, *, name, ts=512):
    s, d = x.shape
    n = len(gains)

    def body(x_ref, *refs):
        xv = x_ref[...]
        xh = xv * lax.rsqrt(jnp.mean(xv * xv, axis=-1, keepdims=True) + EPS)
        for g_ref, o_ref in zip(refs[:n], refs[n:]):
            o_ref[...] = (xh * g_ref[...]).astype(BF16)

    row = pl.BlockSpec((ts, d), lambda i: (i, 0))
    vec = pl.BlockSpec((1, d), lambda i: (0, 0))
    return pl.pallas_call(
        body,
        name=name,
        out_shape=[jax.ShapeDtypeStruct((s, d), BF16)] * n,
        grid=(s // ts,),
        in_specs=[row] + [vec] * n,
        out_specs=[row] * n,
        compiler_params=_params("parallel"),
    )(x, *gains)


def rms_bwd(x, gains, dys, dres, *, name, ts=256):
    s, d = x.shape
    n = len(gains)

    def body(x_ref, r_ref, *refs):
        g_refs, dy_refs = refs[:n], refs[n:2 * n]
        dx_ref, dg_refs = refs[2 * n], refs[2 * n + 1:]
        i = pl.program_id(0)
        xv = x_ref[...]
        r = lax.rsqrt(jnp.mean(xv * xv, axis=-1, keepdims=True) + EPS)
        xh = xv * r
        acc = r_ref[...]
        for g_ref, dy_ref, dg_ref in zip(g_refs, dy_refs, dg_refs):
            dy = dy_ref[...].astype(F32)

            @pl.when(i == 0)
            def _():
                dg_ref[...] = jnp.zeros_like(dg_ref)

            dg_ref[...] += jnp.sum(dy * xh, axis=0, keepdims=True)
            dxh = dy * g_ref[...]
            acc = acc + r * (dxh - xh * jnp.mean(dxh * xh, axis=-1, keepdims=True))
        dx_ref[...] = acc

    row = pl.BlockSpec((ts, d), lambda i: (i, 0))
    vec = pl.BlockSpec((1, d), lambda i: (0, 0))
    outs = pl.pallas_call(
        body,
        name=name,
        out_shape=[jax.ShapeDtypeStruct((s, d), F32)] + [jax.ShapeDtypeStruct((1, d), F32)] * n,
        grid=(s // ts,),
        in_specs=[row, row] + [vec] * n + [row] * n,
        out_specs=[row] + [vec] * n,
        compiler_params=_params("arbitrary"),
    )(x, dres, *gains, *dys)
    return outs[0], outs[1:]


def loss_head(h, gain, target, *, ts=256):
    s, d = h.shape

    def body(h_ref, g_ref, t_ref, l_ref, dh_ref, dg_ref):
        i = pl.program_id(0)

        @pl.when(i == 0)
        def _():
            l_ref[...] = jnp.zeros_like(l_ref)
            dg_ref[...] = jnp.zeros_like(dg_ref)

        xv = h_ref[...]
        r = lax.rsqrt(jnp.mean(xv * xv, axis=-1, keepdims=True) + EPS)
        xh = xv * r
        g = g_ref[...]
        err = xh * g - t_ref[...]
        l_ref[...] += 0.5 * jnp.sum(jnp.mean(err * err, axis=-1, keepdims=True))
        dy = err * (1.0 / d)
        dg_ref[...] += jnp.sum(dy * xh, axis=0, keepdims=True)
        dxh = dy * g
        dh_ref[...] = r * (dxh - xh * jnp.mean(dxh * xh, axis=-1, keepdims=True))

    row = pl.BlockSpec((ts, d), lambda i: (i, 0))
    vec = pl.BlockSpec((1, d), lambda i: (0, 0))
    return pl.pallas_call(
        body,
        name="loss_head",
        out_shape=[jax.ShapeDtypeStruct((8, LANE), F32), jax.ShapeDtypeStruct((s, d), F32),
                   jax.ShapeDtypeStruct((1, d), F32)],
        grid=(s // ts,),
        in_specs=[row, vec, row],
        out_specs=[pl.BlockSpec((8, LANE), lambda i: (0, 0)), row, vec],
        compiler_params=_params("arbitrary"),
    )(h, gain, target)


A_BLOCK = (2 * GLA_KEY_DIM + 2 * GLA_VAL_DIM) // LANE


def gate_fwd(proj, w_a2p, b_a2, *, ts=512):
    s = proj.shape[0]

    def body(a_ref, w_ref, b_ref, o_ref):
        z = jnp.dot(a_ref[...].astype(BF16), w_ref[...], preferred_element_type=F32) + b_ref[...]
        o_ref[...] = (jnp.minimum(z, 0.0) - jnp.log(1.0 + jnp.exp(-jnp.abs(z)))) * (1.0 / GATE_NORMALIZER)

    return pl.pallas_call(
        body,
        name="gate_fwd",
        out_shape=jax.ShapeDtypeStruct((s, GLA_KEY_DIM), F32),
        grid=(s // ts,),
        in_specs=[pl.BlockSpec((ts, LANE), lambda i: (i, A_BLOCK)),
                  pl.BlockSpec((LANE, GLA_KEY_DIM), lambda i: (0, 0)),
                  pl.BlockSpec((1, GLA_KEY_DIM), lambda i: (0, 0))],
        out_specs=pl.BlockSpec((ts, GLA_KEY_DIM), lambda i: (i, 0)),
        compiler_params=_params("parallel"),
    )(proj, w_a2p, b_a2)


def gate_bwd(proj, w_a2p, b_a2, dla, *, ts=512):
    s = proj.shape[0]

    def body(a_ref, w_ref, b_ref, dla_ref, da_ref, dw_ref, db_ref):
        i = pl.program_id(0)

        @pl.when(i == 0)
        def _():
            dw_ref[...] = jnp.zeros_like(dw_ref)
            db_ref[...] = jnp.zeros_like(db_ref)

        a = a_ref[...].astype(BF16)
        w = w_ref[...]
        z = jnp.dot(a, w, preferred_element_type=F32) + b_ref[...]
        dz = dla_ref[...] * (1.0 / GATE_NORMALIZER) / (1.0 + jnp.exp(z))
        dzb = dz.astype(BF16)
        da_ref[...] = lax.dot_general(dzb, w, (((1,), (1,)), ((), ())), preferred_element_type=F32).astype(BF16)
        dw_ref[...] += lax.dot_general(a, dzb, (((0,), (0,)), ((), ())), preferred_element_type=F32)
        db_ref[...] += jnp.sum(dz, axis=0, keepdims=True)

    return pl.pallas_call(
        body,
        name="gate_bwd",
        out_shape=[jax.ShapeDtypeStruct((s, LANE), BF16), jax.ShapeDtypeStruct((LANE, GLA_KEY_DIM), F32),
                   jax.ShapeDtypeStruct((1, GLA_KEY_DIM), F32)],
        grid=(s // ts,),
        in_specs=[pl.BlockSpec((ts, LANE), lambda i: (i, A_BLOCK)),
                  pl.BlockSpec((LANE, GLA_KEY_DIM), lambda i: (0, 0)),
                  pl.BlockSpec((1, GLA_KEY_DIM), lambda i: (0, 0)),
                  pl.BlockSpec((ts, GLA_KEY_DIM), lambda i: (i, 0))],
        out_specs=[pl.BlockSpec((ts, LANE), lambda i: (i, 0)),
                   pl.BlockSpec((LANE, GLA_KEY_DIM), lambda i: (0, 0)),
                   pl.BlockSpec((1, GLA_KEY_DIM), lambda i: (0, 0))],
        compiler_params=_params("arbitrary"),
    )(proj, w_a2p, b_a2, dla)


def _chunk_terms(q_ref, k_ref, la_ref):
    c_len = GLA_CHUNK
    row = lax.broadcasted_iota(jnp.int32, (c_len, c_len), 0)
    col = lax.broadcasted_iota(jnp.int32, (c_len, c_len), 1)
    tri = row >= col
    la = la_ref[...]
    c = jnp.dot(tri.astype(F32), la, preferred_element_type=F32, precision=lax.Precision.HIGHEST)
    last = jnp.sum(la, axis=0, keepdims=True)
    k = k_ref[...]
    q_dec = q_ref[...] * (GLA_DK ** -0.5) * jnp.exp(c)
    k_inv = k * jnp.exp(-c)
    k_end = k * jnp.exp(last - c)
    return c, last, q_dec, k_inv, k_end, tri


def _dot(a, b, ca, cb):
    return lax.dot_general(a.astype(BF16), b.astype(BF16), (((ca,), (cb,)), ((), ())), preferred_element_type=F32)


def gla_fwd(proj, la, jobs=None):
    s = proj.shape[0]
    n_chunks = s // GLA_CHUNK
    kb = GLA_KEY_DIM // GLA_DK

    def body(q_ref, k_ref, v_ref, la_ref, o_ref, st_out, st):
        @pl.when(pl.program_id(1) == 0)
        def _():
            st[...] = jnp.zeros_like(st)

        _, last, q_dec, k_inv, k_end, tri = _chunk_terms(q_ref, k_ref, la_ref)
        v = v_ref[...]
        a = jnp.where(tri, _dot(q_dec, k_inv, 1, 1), 0.0)
        state = st[...]
        st_out[...] = state
        o_ref[...] = _dot(a, v, 1, 0) + _dot(q_dec, state, 1, 1)
        st[...] = state * jnp.exp(last) + _dot(v, k_end, 0, 0)

    outs, bufs = _call(
        body, name="gla_fwd", jobs=jobs,
        out_shape=[jax.ShapeDtypeStruct((s, GLA_VAL_DIM), F32),
                   jax.ShapeDtypeStruct((GLA_HEADS, n_chunks, GLA_DV, GLA_DK), F32)],
        grid=(GLA_HEADS, n_chunks),
        in_specs=[pl.BlockSpec((GLA_CHUNK, GLA_DK), lambda h, n: (n, h)),
                  pl.BlockSpec((GLA_CHUNK, GLA_DK), lambda h, n: (n, kb + h)),
                  pl.BlockSpec((GLA_CHUNK, GLA_DV), lambda h, n: (n, kb + h)),
                  pl.BlockSpec((GLA_CHUNK, GLA_DK), lambda h, n: (n, h))],
        out_specs=[pl.BlockSpec((GLA_CHUNK, GLA_DV), lambda h, n: (n, h)),
                   pl.BlockSpec((None, None, GLA_DV, GLA_DK), lambda h, n: (h, n, 0, 0))],
        scratch_shapes=[pltpu.VMEM((GLA_DV, GLA_DK), F32)],
        args=[proj, proj, proj, la], sem=("parallel", "arbitrary"))
    return outs if jobs is None else (outs, bufs)


def gla_bwd(proj, la, states, do, jobs=None):
    s = proj.shape[0]
    n_chunks = s // GLA_CHUNK
    kb = GLA_KEY_DIM // GLA_DK
    lastc = n_chunks - 1

    def body(q_ref, k_ref, v_ref, la_ref, do_ref, st_ref, dq_ref, dk_ref, dv_ref, dla_ref, dst):
        @pl.when(pl.program_id(1) == 0)
        def _():
            dst[...] = jnp.zeros_like(dst)

        c, last, q_dec, k_inv, k_end, tri = _chunk_terms(q_ref, k_ref, la_ref)
        v = v_ref[...]
        dout = do_ref[...]
        state = st_ref[...]
        dstate = dst[...]
        e_last = jnp.exp(last)
        a = jnp.where(tri, _dot(q_dec, k_inv, 1, 1), 0.0)
        da = jnp.where(tri, _dot(dout, v, 1, 1), 0.0)
        dv_ref[...] = (_dot(a, dout, 0, 0) + _dot(k_end, dstate, 1, 1)).astype(BF16)
        dq_dec = _dot(da, k_inv, 1, 0) + _dot(dout, state, 1, 0)
        dk_inv = _dot(da, q_dec, 0, 0)
        dk_end = _dot(v, dstate, 1, 0)
        dst[...] = dstate * e_last + _dot(dout, q_dec, 0, 0)
        e_c = jnp.exp(c)
        dq_ref[...] = (dq_dec * (GLA_DK ** -0.5) * e_c).astype(BF16)
        dk_ref[...] = (dk_inv * jnp.exp(-c) + dk_end * jnp.exp(last - c)).astype(BF16)
        ke_term = dk_end * k_end
        dc = dq_dec * q_dec - dk_inv * k_inv - ke_term
        dlast = jnp.sum(ke_term, axis=0, keepdims=True) + e_last * jnp.sum(dstate * state, axis=0, keepdims=True)
        upper = lax.broadcasted_iota(jnp.int32, tri.shape, 0) <= lax.broadcasted_iota(jnp.int32, tri.shape, 1)
        dla_ref[...] = jnp.dot(upper.astype(F32), dc, preferred_element_type=F32,
                               precision=lax.Precision.HIGHEST) + dlast

    outs, bufs = _call(
        body, name="gla_bwd", jobs=jobs,
        out_shape=[jax.ShapeDtypeStruct((s, GLA_KEY_DIM), BF16), jax.ShapeDtypeStruct((s, GLA_KEY_DIM), BF16),
                   jax.ShapeDtypeStruct((s, GLA_VAL_DIM), BF16), jax.ShapeDtypeStruct((s, GLA_KEY_DIM), F32)],
        grid=(GLA_HEADS, n_chunks),
        in_specs=[pl.BlockSpec((GLA_CHUNK, GLA_DK), lambda h, n: (lastc - n, h)),
                  pl.BlockSpec((GLA_CHUNK, GLA_DK), lambda h, n: (lastc - n, kb + h)),
                  pl.BlockSpec((GLA_CHUNK, GLA_DV), lambda h, n: (lastc - n, kb + h)),
                  pl.BlockSpec((GLA_CHUNK, GLA_DK), lambda h, n: (lastc - n, h)),
                  pl.BlockSpec((GLA_CHUNK, GLA_DV), lambda h, n: (lastc - n, h)),
                  pl.BlockSpec((None, None, GLA_DV, GLA_DK), lambda h, n: (h, lastc - n, 0, 0))],
        out_specs=[pl.BlockSpec((GLA_CHUNK, GLA_DK), lambda h, n: (lastc - n, h)),
                   pl.BlockSpec((GLA_CHUNK, GLA_DK), lambda h, n: (lastc - n, h)),
                   pl.BlockSpec((GLA_CHUNK, GLA_DV), lambda h, n: (lastc - n, h)),
                   pl.BlockSpec((GLA_CHUNK, GLA_DK), lambda h, n: (lastc - n, h))],
        scratch_shapes=[pltpu.VMEM((GLA_DV, GLA_DK), F32)],
        args=[proj, proj, proj, la, do, states], sem=("parallel", "arbitrary"))
    return outs if jobs is None else (outs, bufs)


R_BLOCK = (2 * GLA_KEY_DIM + GLA_VAL_DIM) // GLA_DV


def headnorm_fwd(o, proj, hn, *, ts=512):
    s = o.shape[0]

    def body(o_ref, r_ref, g_ref, out_ref):
        ov = o_ref[...]
        oh = ov * lax.rsqrt(jnp.mean(ov * ov, axis=-1, keepdims=True) + EPS)
        r = r_ref[...]
        out_ref[...] = (oh * g_ref[...] * (r * jax.nn.sigmoid(r))).astype(BF16)

    return pl.pallas_call(
        body,
        name="headnorm_fwd",
        out_shape=jax.ShapeDtypeStruct((s, GLA_VAL_DIM), BF16),
        grid=(s // ts, GLA_HEADS),
        in_specs=[pl.BlockSpec((ts, GLA_DV), lambda i, h: (i, h)),
                  pl.BlockSpec((ts, GLA_DV), lambda i, h: (i, R_BLOCK + h)),
                  pl.BlockSpec((1, GLA_DV), lambda i, h: (0, 0))],
        out_specs=pl.BlockSpec((ts, GLA_DV), lambda i, h: (i, h)),
        compiler_params=_params("parallel", "parallel"),
    )(o, proj, hn)


def headnorm_bwd(o, proj, hn, dog, *, ts=512):
    s = o.shape[0]

    def body(o_ref, r_ref, g_ref, dog_ref, do_ref, dr_ref, dg_ref):
        @pl.when((pl.program_id(0) == 0) & (pl.program_id(1) == 0))
        def _():
            dg_ref[...] = jnp.zeros_like(dg_ref)

        ov = o_ref[...]
        rr = lax.rsqrt(jnp.mean(ov * ov, axis=-1, keepdims=True) + EPS)
        oh = ov * rr
        g = g_ref[...]
        r = r_ref[...]
        sig = jax.nn.sigmoid(r)
        gate = r * sig
        dog_v = dog_ref[...]
        d_on = dog_v * gate
        dr_ref[...] = (dog_v * (oh * g) * (sig * (1.0 + r * (1.0 - sig)))).astype(BF16)
        dg_ref[...] += jnp.sum(d_on * oh, axis=0, keepdims=True)
        doh = d_on * g
        do_ref[...] = rr * (doh - oh * jnp.mean(doh * oh, axis=-1, keepdims=True))

    return pl.pallas_call(
        body,
        name="headnorm_bwd",
        out_shape=[jax.ShapeDtypeStruct((s, GLA_VAL_DIM), F32), jax.ShapeDtypeStruct((s, GLA_VAL_DIM), BF16),
                   jax.ShapeDtypeStruct((1, GLA_DV), F32)],
        grid=(s // ts, GLA_HEADS),
        in_specs=[pl.BlockSpec((ts, GLA_DV), lambda i, h: (i, h)),
                  pl.BlockSpec((ts, GLA_DV), lambda i, h: (i, R_BLOCK + h)),
                  pl.BlockSpec((1, GLA_DV), lambda i, h: (0, 0)),
                  pl.BlockSpec((ts, GLA_DV), lambda i, h: (i, h))],
        out_specs=[pl.BlockSpec((ts, GLA_DV), lambda i, h: (i, h)),
                   pl.BlockSpec((ts, GLA_DV), lambda i, h: (i, h)),
                   pl.BlockSpec((1, GLA_DV), lambda i, h: (0, 0))],
        compiler_params=_params("arbitrary", "arbitrary"),
    )(o, proj, hn, dog)


CONV_TC = 128
SQRT_HALF = 0.7071067811865476
INV_SQRT_2PI = 0.3989422804014327


def _conv_gate(g_ref, cw_ref, cb_ref):
    g0 = g_ref[...].astype(F32)
    t = lax.broadcasted_iota(jnp.int32, g0.shape, 0)
    g1 = jnp.where(t >= 1, pltpu.roll(g0, 1, 0), 0.0)
    g2 = jnp.where(t >= 2, pltpu.roll(g0, 2, 0), 0.0)
    gc = cw_ref[0:1, :] * g2 + cw_ref[1:2, :] * g1 + cw_ref[2:3, :] * g0 + cb_ref[...]
    return g0, g1, g2, gc, t


def convglu_fwd(up, conv_w, conv_b):
    s = up.shape[0]
    nc = D_FF // CONV_TC

    def body(u_ref, g_ref, cw_ref, cb_ref, o_ref):
        _, _, _, gc, _ = _conv_gate(g_ref, cw_ref, cb_ref)
        gelu = 0.5 * gc * (1.0 + lax.erf(gc * SQRT_HALF))
        o_ref[...] = (gelu * u_ref[...].astype(F32)).astype(BF16)

    return pl.pallas_call(
        body,
        name="convglu_fwd",
        out_shape=jax.ShapeDtypeStruct((s, D_FF), BF16),
        grid=(nc,),
        in_specs=[pl.BlockSpec((s, CONV_TC), lambda c: (0, c)),
                  pl.BlockSpec((s, CONV_TC), lambda c: (0, nc + c)),
                  pl.BlockSpec((3, CONV_TC), lambda c: (0, c)),
                  pl.BlockSpec((1, CONV_TC), lambda c: (0, c))],
        out_specs=pl.BlockSpec((s, CONV_TC), lambda c: (0, c)),
        compiler_params=_params("parallel"),
    )(up, up, conv_w, conv_b)


def convglu_bwd(up, conv_w, conv_b, dact, *, name, jobs=None):
    s = up.shape[0]
    nc = D_FF // CONV_TC

    def body(u_ref, g_ref, cw_ref, cb_ref, da_ref, du_ref, dg_ref, dcw_ref, dcb_ref):
        g0, g1, g2, gc, t = _conv_gate(g_ref, cw_ref, cb_ref)
        cdf = 0.5 * (1.0 + lax.erf(gc * SQRT_HALF))
        da = da_ref[...].astype(F32)
        du_ref[...] = (da * gc * cdf).astype(BF16)
        dgc = da * u_ref[...].astype(F32) * (cdf + gc * jnp.exp(-0.5 * gc * gc) * INV_SQRT_2PI)
        dcb_ref[...] = jnp.sum(dgc, axis=0, keepdims=True)
        dcw_ref[0:1, :] = jnp.sum(dgc * g2, axis=0, keepdims=True)
        dcw_ref[1:2, :] = jnp.sum(dgc * g1, axis=0, keepdims=True)
        dcw_ref[2:3, :] = jnp.sum(dgc * g0, axis=0, keepdims=True)
        n1 = jnp.where(t < s - 1, pltpu.roll(dgc, s - 1, 0), 0.0)
        n2 = jnp.where(t < s - 2, pltpu.roll(dgc, s - 2, 0), 0.0)
        dg_ref[...] = (cw_ref[2:3, :] * dgc + cw_ref[1:2, :] * n1 + cw_ref[0:1, :] * n2).astype(BF16)

    col = pl.BlockSpec((s, CONV_TC), lambda c: (0, c))
    outs, bufs = _call(
        body, name=name, jobs=jobs,
        out_shape=[jax.ShapeDtypeStruct((s, D_FF), BF16), jax.ShapeDtypeStruct((s, D_FF), BF16),
                   jax.ShapeDtypeStruct((3, D_FF), F32), jax.ShapeDtypeStruct((1, D_FF), F32)],
        grid=(nc,),
        in_specs=[col, pl.BlockSpec((s, CONV_TC), lambda c: (0, nc + c)),
                  pl.BlockSpec((3, CONV_TC), lambda c: (0, c)),
                  pl.BlockSpec((1, CONV_TC), lambda c: (0, c)), col],
        out_specs=[col, col, pl.BlockSpec((3, CONV_TC), lambda c: (0, c)),
                   pl.BlockSpec((1, CONV_TC), lambda c: (0, c))],
        args=[up, up, conv_w, conv_b, dact], sem=("parallel",))
    return outs if jobs is None else (outs, bufs)


def _branch_scalars(g, s):
    dil = jnp.where(g == 0, float(DILATIONS[0]), jnp.where(g == 1, float(DILATIONS[1]), float(DILATIONS[2])))
    nb = jnp.where(g == 0, s // DILATIONS[0] // ATT_BLOCK,
                   jnp.where(g == 1, s // DILATIONS[1] // ATT_BLOCK, s // DILATIONS[2] // ATT_BLOCK))
    return dil, nb


def _head(ref, h):
    return ref[:, h * HEAD_DIM:(h + 1) * HEAD_DIM]


def attn_fwd(qp, kp, vp, jobs=None):
    _, s, w = qp.shape
    nblk = s // ATT_BLOCK
    scale = HEAD_DIM ** -0.5

    def body(q_ref, kp_ref, kc_ref, vp_ref, vc_ref, o_ref, l_ref):
        g = pl.program_id(0)
        b = pl.program_id(1)
        dil, nb = _branch_scalars(g, s)
        prev_max = jnp.where(lax.rem(b, nb) != 0, 0, -ATT_BLOCK - 1)
        qa = lax.broadcasted_iota(jnp.int32, (ATT_BLOCK, ATT_BLOCK), 0)
        kc = lax.broadcasted_iota(jnp.int32, (ATT_BLOCK, ATT_BLOCK), 1)
        jd = qa - kc
        jdf = jd.astype(F32)
        ok_c = jd >= 0
        ok_p = jd <= prev_max
        for h in range(ATT_HEADS):
            sl = ALIBI_SLOPES[h] * dil
            q = _head(q_ref, h)
            s_c = jnp.where(ok_c, _dot(q, _head(kc_ref, h), 1, 1) * scale - sl * jdf, NEG)
            s_p = jnp.where(ok_p, _dot(q, _head(kp_ref, h), 1, 1) * scale - sl * (jdf + ATT_BLOCK), NEG)
            m = jnp.maximum(jnp.max(s_c, axis=-1, keepdims=True), jnp.max(s_p, axis=-1, keepdims=True))
            p_c = jnp.exp(s_c - m)
            p_p = jnp.exp(s_p - m)
            l = jnp.sum(p_c, axis=-1, keepdims=True) + jnp.sum(p_p, axis=-1, keepdims=True)
            acc = _dot(p_p, _head(vp_ref, h), 1, 0) + _dot(p_c, _head(vc_ref, h), 1, 0)
            o_ref[:, h * HEAD_DIM:(h + 1) * HEAD_DIM] = acc / l
            l_ref[:, h * HEAD_DIM:(h + 1) * HEAD_DIM] = jnp.broadcast_to(m + jnp.log(l), (ATT_BLOCK, HEAD_DIM))

    cur = pl.BlockSpec((None, ATT_BLOCK, w), lambda g, b: (g, b, 0))
    prev = pl.BlockSpec((None, ATT_BLOCK, w), lambda g, b: (g, jnp.maximum(b - 1, 0), 0))
    outs, bufs = _call(
        body, name="attn_fwd", jobs=jobs,
        out_shape=[jax.ShapeDtypeStruct((3, s, w), F32), jax.ShapeDtypeStruct((3, s, w), F32)],
        grid=(3, nblk),
        in_specs=[cur, prev, cur, prev, cur],
        out_specs=[cur, cur],
        args=[qp, kp, kp, vp, vp], sem=("parallel", "parallel"))
    return outs if jobs is None else (outs, bufs)


def attn_bwd(qp, kp, vp, dop, lsep, deltap, jobs=None):
    _, s, w = qp.shape
    nblk = s // ATT_BLOCK
    scale = HEAD_DIM ** -0.5

    def body(k_ref, v_ref, qc_ref, doc_ref, lc_ref, dc_ref, qn_ref, don_ref, ln_ref, dn_ref,
             dq_ref, dk_ref, dv_ref, carry):
        g = pl.program_id(0)
        b = pl.program_id(1)

        @pl.when(b == 0)
        def _():
            carry[...] = jnp.zeros_like(carry)

        dil, nb = _branch_scalars(g, s)
        next_max = jnp.where((b + 1 < nblk) & (lax.rem(b + 1, nb) != 0), 0, -ATT_BLOCK - 1)
        qa = lax.broadcasted_iota(jnp.int32, (ATT_BLOCK, ATT_BLOCK), 0)
        kc = lax.broadcasted_iota(jnp.int32, (ATT_BLOCK, ATT_BLOCK), 1)
        jd = qa - kc
        jdf = jd.astype(F32)
        ok_c = jd >= 0
        ok_n = jd <= next_max
        for h in range(ATT_HEADS):
            sl = ALIBI_SLOPES[h] * dil
            hs = slice(h * HEAD_DIM, (h + 1) * HEAD_DIM)
            k = k_ref[:, hs]
            v = v_ref[:, hs]
            q_c = qc_ref[:, hs]
            do_c = doc_ref[:, hs]
            s_c = jnp.where(ok_c, _dot(q_c, k, 1, 1) * scale - sl * jdf, NEG)
            p_c = jnp.exp(s_c - lc_ref[:, hs])
            ds_c = p_c * (_dot(do_c, v, 1, 1) - dc_ref[:, hs])
            q_n = qn_ref[:, hs]
            do_n = don_ref[:, hs]
            s_n = jnp.where(ok_n, _dot(q_n, k, 1, 1) * scale - sl * (jdf + ATT_BLOCK), NEG)
            p_n = jnp.exp(s_n - ln_ref[:, hs])
            ds_n = p_n * (_dot(do_n, v, 1, 1) - dn_ref[:, hs])
            dv_ref[:, hs] = _dot(p_c, do_c, 0, 0) + _dot(p_n, do_n, 0, 0)
            dk_ref[:, hs] = (_dot(ds_c, q_c, 0, 0) + _dot(ds_n, q_n, 0, 0)) * scale
            dq_ref[:, hs] = (carry[:, hs] + _dot(ds_c, k, 1, 0) * scale).astype(BF16)
            carry[:, hs] = _dot(ds_n, k, 1, 0) * scale

    cur = pl.BlockSpec((None, ATT_BLOCK, w), lambda g, b: (g, b, 0))
    nxt = pl.BlockSpec((None, ATT_BLOCK, w), lambda g, b: (g, jnp.minimum(b + 1, nblk - 1), 0))
    outs, bufs = _call(
        body, name="attn_bwd", jobs=jobs,
        out_shape=[jax.ShapeDtypeStruct((3, s, w), BF16), jax.ShapeDtypeStruct((3, s, w), F32),
                   jax.ShapeDtypeStruct((3, s, w), F32)],
        grid=(3, nblk),
        in_specs=[cur, cur, cur, cur, cur, cur, nxt, nxt, nxt, nxt],
        out_specs=[cur, cur, cur],
        scratch_shapes=[pltpu.VMEM((ATT_BLOCK, w), F32)],
        args=[kp, vp, qp, dop, lsep, deltap, qp, dop, lsep, deltap], sem=("parallel", "arbitrary"))
    return outs if jobs is None else (outs, bufs)


def attn_merge(o3, l3, *, ts=256):
    _, s, w = o3.shape

    def body(o_ref, l_ref, out_ref, lse_ref):
        l0, l1, l2 = l_ref[0], l_ref[1], l_ref[2]
        m = jnp.maximum(jnp.maximum(l0, l1), l2)
        e0, e1, e2 = jnp.exp(l0 - m), jnp.exp(l1 - m), jnp.exp(l2 - m)
        den = e0 + e1 + e2
        out_ref[...] = (e0 * o_ref[0] + e1 * o_ref[1] + e2 * o_ref[2]) / den
        lse_ref[...] = m + jnp.log(den)

    blk3 = pl.BlockSpec((3, ts, w), lambda i: (0, i, 0))
    blk = pl.BlockSpec((ts, w), lambda i: (i, 0))
    return pl.pallas_call(
        body,
        name="attn_merge",
        out_shape=[jax.ShapeDtypeStruct((s, w), F32), jax.ShapeDtypeStruct((s, w), F32)],
        grid=(s // ts,),
        in_specs=[blk3, blk3],
        out_specs=[blk, blk],
        compiler_params=_params("parallel"),
    )(o3, l3)


def attn_delta(do, o, *, ts=512):
    s, w = do.shape

    def body(do_ref, o_ref, d_ref, dob_ref):
        dob_ref[...] = do_ref[...].astype(BF16)
        for h in range(ATT_HEADS):
            hs = slice(h * HEAD_DIM, (h + 1) * HEAD_DIM)
            d = jnp.sum(do_ref[:, hs] * o_ref[:, hs], axis=-1, keepdims=True)
            d_ref[:, hs] = jnp.broadcast_to(d, (ts, HEAD_DIM))

    blk = pl.BlockSpec((ts, w), lambda i: (i, 0))
    return pl.pallas_call(
        body,
        name="attn_delta",
        out_shape=[jax.ShapeDtypeStruct((s, w), F32), jax.ShapeDtypeStruct((s, w), BF16)],
        grid=(s // ts,),
        in_specs=[blk, blk],
        out_specs=[blk, blk],
        compiler_params=_params("parallel"),
    )(do, o)


def kv_grad_sum(dk3, dv3, *, ts=256):
    _, s, w = dk3.shape

    def body(dk_ref, dv_ref, o_ref):
        o_ref[:, 0:w] = (dk_ref[0] + dk_ref[1] + dk_ref[2]).astype(BF16)
        o_ref[:, w:2 * w] = (dv_ref[0] + dv_ref[1] + dv_ref[2]).astype(BF16)

    blk3 = pl.BlockSpec((3, ts, w), lambda i: (0, i, 0))
    return pl.pallas_call(
        body,
        name="kv_grad_sum",
        out_shape=jax.ShapeDtypeStruct((s, 2 * w), BF16),
        grid=(s // ts,),
        in_specs=[blk3, blk3],
        out_specs=pl.BlockSpec((ts, 2 * w), lambda i: (i, 0)),
        compiler_params=_params("parallel"),
    )(dk3, dv3)


def _to_branch(t, d):
    s, w = t.shape
    return t.reshape(s // d, d, w).transpose(1, 0, 2).reshape(s, w)


def _from_branch(t, d):
    s, w = t.shape
    return t.reshape(d, s // d, w).transpose(1, 0, 2).reshape(s, w)


def _branches(t):
    return jnp.stack([_to_branch(t, d) for d in DILATIONS])


def _natural(t3):
    return jnp.stack([_from_branch(t3[g], d) for g, d in enumerate(DILATIONS)])


def _adam(w, g, m, v):
    m = ADAM_B1 * m + (1.0 - ADAM_B1) * g
    v = ADAM_B2 * v + (1.0 - ADAM_B2) * (g * g)
    m_hat = m / (1.0 - ADAM_B1 ** ADAM_STEP)
    v_hat = v / (1.0 - ADAM_B2 ** ADAM_STEP)
    delta = -ADAM_LR * (m_hat / (jnp.sqrt(v_hat) + ADAM_EPS) + ADAM_WD * w)
    return delta, m, v


def adam_sharded(recvs, w, m, v, *, name):
    layers = len(recvs)
    n_src, r, c = recvs[0].shape
    tr = _rows(r, c)

    def body(*refs):
        p_refs = refs[:layers]
        w_ref, m_ref, v_ref, g_ref, d_ref, mo_ref, vo_ref = refs[layers:]
        for layer, p_ref in enumerate(p_refs):
            @pl.when(pl.program_id(0) == layer)
            def _():
                g = p_ref[0].astype(F32)
                for src in range(1, n_src):
                    g = g + p_ref[src].astype(F32)
                delta, m_new, v_new = _adam(w_ref[...], g, m_ref[...], v_ref[...])
                g_ref[...] = g
                d_ref[...] = delta
                mo_ref[...] = m_new
                vo_ref[...] = v_new

    blk = pl.BlockSpec((None, tr, c), lambda l, i: (l, i, 0))
    out = jax.ShapeDtypeStruct((layers, r, c), F32)
    part = [pl.BlockSpec((n_src, tr, c), functools.partial(lambda l, i, layer: (0, jnp.where(l == layer, i, 0), 0),
                                                            layer=layer)) for layer in range(layers)]
    return pl.pallas_call(
        body,
        name=name,
        out_shape=[out] * 4,
        grid=(layers, r // tr),
        in_specs=part + [blk, blk, blk],
        out_specs=[blk] * 4,
        compiler_params=_params("parallel", "parallel"),
    )(*recvs, w, m, v)


def sum_partials(parts):
    n_src, r, c = parts.shape

    def body(p_ref, o_ref):
        g = p_ref[0]
        for src in range(1, n_src):
            g = g + p_ref[src]
        o_ref[...] = g

    return pl.pallas_call(
        body,
        name="sum_small_grads",
        out_shape=jax.ShapeDtypeStruct((r, c), F32),
    )(parts)


def adam_packed(w, g, m, v):
    def body(w_ref, g_ref, m_ref, v_ref, d_ref, mo_ref, vo_ref):
        delta, m_new, v_new = _adam(w_ref[...], g_ref[...], m_ref[...], v_ref[...])
        d_ref[...] = delta
        mo_ref[...] = m_new
        vo_ref[...] = v_new

    out = jax.ShapeDtypeStruct(w.shape, F32)
    return pl.pallas_call(body, name="adam_small", out_shape=[out] * 3)(w, g, m, v)


def all_gather(srcs, *, name):
    n = len(srcs)

    def body(*refs):
        src, dst = refs[:n], refs[n:2 * n]
        send_sems, recv_sems, local_sems = refs[2 * n:]
        x, y, c, me = _place()
        sibling = (x, y, 1 - c)
        chips = [(1 - x, y), (x, 1 - y), (1 - x, 1 - y)]

        def index(px, py, pc):
            return 4 * px + 2 * py + pc

        def copy(p, k, block, to, from_src=False):
            slot = dst[p].at[index(*block)]
            return pltpu.make_async_remote_copy(
                src_ref=src[p] if from_src else slot, dst_ref=slot,
                send_sem=send_sems.at[p, k], recv_sem=recv_sems.at[p, k],
                device_id=to, device_id_type=MESH)

        mine = [pltpu.make_async_copy(src[p], dst[p].at[me], local_sems.at[p]) for p in range(n)]
        for cp in mine:
            cp.start()
        first = []
        for p in range(n):
            first.append(copy(p, 0, (x, y, c), sibling, from_src=True))
            for jj, chip in enumerate(chips):
                first.append(copy(p, 1 + jj, (x, y, c), (*chip, c), from_src=True))
        for cp in first:
            cp.start()
        passed = []
        for jj, chip in enumerate(chips):
            for p in range(n):
                copy(p, 1 + jj, (*chip, c), (x, y, c)).wait_recv()
                fwd = copy(p, 4 + jj, (*chip, c), sibling)
                fwd.start()
                passed.append(fwd)
        for p in range(n):
            copy(p, 0, sibling, (x, y, c)).wait_recv()
            for jj, chip in enumerate(chips):
                copy(p, 4 + jj, (*chip, 1 - c), (x, y, c)).wait_recv()
        for cp in first + passed:
            cp.wait_send()
        for cp in mine:
            cp.wait()

    return pl.pallas_call(
        body,
        name=name,
        out_shape=[jax.ShapeDtypeStruct((N_DEV,) + a.shape, a.dtype) for a in srcs],
        in_specs=[ANY] * n,
        out_specs=[ANY] * n,
        scratch_shapes=[pltpu.SemaphoreType.DMA((n, 7)), pltpu.SemaphoreType.DMA((n, 7)),
                        pltpu.SemaphoreType.DMA((n,))],
    )(*srcs)


def exchange_only(*, name, jobs):
    def body(o_ref):
        o_ref[...] = jnp.zeros_like(o_ref)

    _, bufs = _call(body, name=name, jobs=jobs, out_shape=[jax.ShapeDtypeStruct((8, LANE), F32)], grid=(1,),
                    in_specs=[], out_specs=[pl.BlockSpec((8, LANE), lambda i: (0, 0))], args=[], sem=("arbitrary",))
    return None, bufs


def _pack_rows(parts, rows):
    flat = jnp.concatenate([p.reshape(-1) for p in parts])
    return jnp.pad(flat, (0, rows * LANE - flat.shape[0])).reshape(rows, LANE)


def _unpack_rows(packed, shapes):
    flat = packed.reshape(-1)
    out, at = [], 0
    for sh in shapes:
        size = 1
        for dim in sh:
            size *= dim
        out.append(flat[at:at + size].reshape(sh))
        at += size
    return out


CONV_W_PAD = 768
SMALL_W_ROWS = 56


def _pack_small_weights(w_a2, b_a2, hn, conv_w):
    cw = jnp.pad(conv_w.reshape(6, -1), ((0, 0), (0, CONV_W_PAD - conv_w.shape[-1]))).reshape(-1, LANE)
    rows = jnp.concatenate([w_a2[0], b_a2, jnp.pad(hn, ((0, 0), (0, LANE - hn.shape[-1]))), cw], axis=0)
    return jnp.pad(rows, ((0, SMALL_W_ROWS - rows.shape[0]), (0, 0)))


def _unpack_small_weights(gathered):
    w_a2 = gathered[:, 0:GATE_RANK, :].transpose(1, 0, 2).reshape(GATE_RANK, GLA_KEY_DIM)
    b_a2 = gathered[:, GATE_RANK, :].reshape(1, GLA_KEY_DIM)
    hn = gathered[:, GATE_RANK + 1, :GLA_DV // N_DEV].reshape(1, GLA_DV)
    per = D_FF // N_DEV
    cw = gathered[:, GATE_RANK + 2:GATE_RANK + 2 + 6 * CONV_W_PAD // LANE, :].reshape(N_DEV, 6, CONV_W_PAD)[:, :, :per]
    cw = cw.reshape(N_DEV, 2, 3, per).transpose(1, 2, 0, 3).reshape(2, 3, D_FF)
    return w_a2, b_a2, hn, cw


SCHEDULE = {
    "gla_in": [("g1", "gout", None), ("g1", "up0", (0, 1024))],
    "gla_fwd": [("g2", "gout", None), ("g2", "up0", (0, 1024)), ("g1", "up0", (1024, 2048)), ("g1", "dn0", (0, 352))],
    "gla_out": [("g2", "up0", (1024, 2048)), ("g2", "dn0", (0, 352)), ("g1", "dn0", (352, 704))],
    "ffn_up0": [("g2", "dn0", (352, 704)), ("g1", "kv", None), ("g1", "q", None)],
    "ffn_down0": [("g2", "kv", None), ("g2", "q", None), ("g1", "dout", None), ("g1", "up1", (0, 704))],
    "kv_proj": [("g2", "dout", None), ("g2", "up1", (0, 704)), ("g1", "up1", (704, 1408))],
    "q_proj": [("g2", "up1", (704, 1408)), ("g1", "up1", (1408, 2048))],
    "attn_fwd": [("g2", "up1", (1408, 2048)), ("g1", "dn1", None)],
    "dsa_out": [("g2", "dn1", None)],
    "ffn_down_dx1": [("sc", "dn1", (0, 352))],
    "convglu_bwd1": [("sc", "dn1", (352, 704))],
    "ffn_up_dx1": [("sc", "up1", (0, 1024))],
    "attn_bwd": [("sc", "up1", (1024, 2048)), ("sc", "dout", None)],
    "q_proj_dx": [("sc", "q", (0, 1408))],
    "kv_proj_dx": [("sc", "q", (1408, 2048)), ("sc", "kv", (0, 1024))],
    "ffn_down_dx0": [("sc", "kv", (1024, 2048)), ("sc", "dn0", (0, 176))],
    "convglu_bwd0": [("sc", "dn0", (176, 528))],
    "ffn_up_dx0": [("sc", "dn0", (528, 704)), ("sc", "up0", (0, 896))],
    "gla_bwd": [("sc", "up0", (896, 1792)), ("sc", "gout", None)],
    "gla_in_dx": [("sc", "up0", (1792, 2048)), ("sc", "in", (0, 1024))],
    "grads_tail": [("sc", "in", (1024, 2048))],
}
ROW_SHARDED = ("gout", "dout", "dn0", "dn1")


class Plan:
    def __init__(self, weights, srcs=None):
        self.w = dict(weights)
        self.srcs = srcs
        self.grads = {}
        self.recv = {}
        self._names = None

    def weight(self, name):
        buf = self.w[name]
        if name in ROW_SHARDED:
            return buf.reshape(1, buf.shape[0] * buf.shape[1], buf.shape[2])
        return buf

    def jobs(self, call):
        ops = SCHEDULE.get(call)
        if self.srcs is None or not ops:
            return None
        jobs, handles = Jobs(), {}
        for op, name, rows in ops:
            store = self.recv if op == "sc" else self.w
            if name not in handles:
                if name in store:
                    handles[name] = jobs.thru(store[name])
                elif op == "sc":
                    handles[name] = jobs.new(self.grads[name].shape, BF16)
                else:
                    handles[name] = jobs.new((N_DEV,) + self.srcs[name].shape, BF16)
            if op == "g1":
                jobs.gather_ici(self.srcs[name], handles[name], rows)
            elif op == "g2":
                jobs.gather_d2d(handles[name], rows)
            else:
                jobs.scatter(self.grads[name], handles[name], rows)
        self._names = [(name, self.recv if ops[0][0] == "sc" else self.w) for name in handles]
        assert len({op == "sc" for op, _, _ in ops}) == 1
        return jobs

    def run(self, call, fn, *args, **kwargs):
        jobs = self.jobs(call)
        if jobs is None:
            return fn(*args, **kwargs)
        out, bufs = fn(*args, jobs=jobs, **kwargs)
        for (name, store), buf in zip(self._names, bufs):
            store[name] = buf
        return out


def _ffn_fwd(plan, h, norm_g, conv_w, conv_b, tag):
    (n,) = rms_fwd(h, [norm_g], name=f"ffn_norm_fwd{tag}")
    up = plan.run(f"ffn_up{tag}", mm_nn, n, plan.weight(f"up{tag}"), out_dtype=BF16, name=f"ffn_up{tag}")
    act = convglu_fwd(up, conv_w, conv_b)
    h_out = plan.run(f"ffn_down{tag}", mm_nn, act, plan.weight(f"dn{tag}"), out_dtype=F32, res=h,
                     name=f"ffn_down{tag}")
    return h_out, (n, up, act)


def _by_rows(dw):
    return dw.reshape(N_DEV, dw.shape[1] // N_DEV, dw.shape[2])


def _ffn_bwd(plan, dh_out, h, saved, norm_g, conv_w, conv_b, tag):
    n, up, act = saved
    plan.grads[f"dn{tag}"] = _by_rows(mm_tn(act, dh_out, 1, name=f"ffn_down_dw{tag}"))
    dact = plan.run(f"ffn_down_dx{tag}", mm_nt, dh_out, plan.weight(f"dn{tag}"), out_dtype=BF16,
                    name=f"ffn_down_dx{tag}")
    du, dg, dconv_w, dconv_b = plan.run(f"convglu_bwd{tag}", convglu_bwd, up, conv_w, conv_b, dact,
                                        name=f"convglu_bwd{tag}")
    dup = jnp.concatenate([du, dg], axis=1)
    plan.grads[f"up{tag}"] = mm_tn(n, dup, N_DEV, name=f"ffn_up_dw{tag}")
    dn = plan.run(f"ffn_up_dx{tag}", mm_nt, dup, plan.weight(f"up{tag}"), out_dtype=F32, name=f"ffn_up_dx{tag}")
    dh, (dnorm,) = rms_bwd(h, [norm_g], [dn], dh_out, name=f"ffn_norm_bwd{tag}")
    return dh, dnorm, dconv_w, dconv_b


def local_step(x, target, wts, plan):
    row = lambda v: v.reshape(1, -1)
    attn_norm, ffn_norm = wts["attn_norm"], wts["ffn_norm"]
    conv_w, conv_b = wts["ffn_conv_w"], wts["ffn_conv_b"]

    (n1,) = rms_fwd(x, [row(attn_norm[0])], name="attn_norm_fwd0")
    proj = plan.run("gla_in", mm_nn, n1, wts["gla_w_in"], out_dtype=F32, name="gla_in")
    la = gate_fwd(proj, wts["gla_w_a2"], wts["gla_b_a2"])
    o_gla, states = plan.run("gla_fwd", gla_fwd, proj, la)
    og = headnorm_fwd(o_gla, proj, wts["gla_head_norm"])
    h1 = plan.run("gla_out", mm_nn, og, plan.weight("gout"), out_dtype=F32, res=x, name="gla_out")
    h2, ffn0 = _ffn_fwd(plan, h1, row(ffn_norm[0]), conv_w[0], row(conv_b[0]), "0")

    kvn, n3 = rms_fwd(h2, [row(wts["kv_norm"]), row(attn_norm[1])], name="kv_attn_norm_fwd")
    kv = plan.run("kv_proj", mm_nn, kvn, plan.weight("kv"), out_dtype=BF16, name="kv_proj")
    q = plan.run("q_proj", mm_nn, n3, plan.weight("q"), out_dtype=BF16, name="q_proj")
    width = ATT_HEADS * HEAD_DIM
    qp = jnp.stack([_to_branch(q[:, g * width:(g + 1) * width], d) for g, d in enumerate(DILATIONS)])
    kp = _branches(kv[:, :width])
    vp = _branches(kv[:, width:])
    o3, l3 = plan.run("attn_fwd", attn_fwd, qp, kp, vp)
    o_att, lse = attn_merge(_natural(o3), _natural(l3))
    h3 = plan.run("dsa_out", mm_nn, o_att, plan.weight("dout"), out_dtype=F32, res=h2, name="dsa_out")
    h4, ffn1 = _ffn_fwd(plan, h3, row(ffn_norm[1]), conv_w[1], row(conv_b[1]), "1")

    loss_tile, dh4, d_final = loss_head(h4, row(wts["final_norm"]), target)

    dh3, d_ffn1, dcw1, dcb1 = _ffn_bwd(plan, dh4, h3, ffn1, row(ffn_norm[1]), conv_w[1], row(conv_b[1]), "1")
    plan.grads["dout"] = _by_rows(mm_tn(o_att, dh3, 1, name="dsa_out_dw"))
    do_att = mm_nt(dh3, plan.weight("dout"), out_dtype=F32, name="dsa_out_dx")
    delta, do_b = attn_delta(do_att, o_att)
    dq3, dk3, dv3 = plan.run("attn_bwd", attn_bwd, qp, kp, vp, _branches(do_b), _branches(lse), _branches(delta))
    dq = jnp.concatenate([_from_branch(dq3[g], d) for g, d in enumerate(DILATIONS)], axis=1)
    dkv = kv_grad_sum(_natural(dk3), _natural(dv3))
    plan.grads["q"] = mm_tn(n3, dq, N_DEV, name="q_proj_dw")
    dn3 = plan.run("q_proj_dx", mm_nt, dq, plan.weight("q"), out_dtype=F32, name="q_proj_dx")
    plan.grads["kv"] = mm_tn(kvn, dkv, N_DEV, name="kv_proj_dw")
    dkvn = plan.run("kv_proj_dx", mm_nt, dkv, plan.weight("kv"), out_dtype=F32, name="kv_proj_dx")
    dh2, (d_kvnorm, d_attn1) = rms_bwd(h2, [row(wts["kv_norm"]), row(attn_norm[1])], [dkvn, dn3], dh3,
                                       name="kv_attn_norm_bwd")
    dh1, d_ffn0, dcw0, dcb0 = _ffn_bwd(plan, dh2, h1, ffn0, row(ffn_norm[0]), conv_w[0], row(conv_b[0]), "0")
    plan.grads["gout"] = _by_rows(mm_tn(og, dh1, 1, name="gla_out_dw"))
    dog = mm_nt(dh1, plan.weight("gout"), out_dtype=F32, name="gla_out_dx")
    do_gla, dr, d_hn = headnorm_bwd(o_gla, proj, wts["gla_head_norm"], dog)
    dq_g, dk_g, dv_g, dla = plan.run("gla_bwd", gla_bwd, proj, la, states, do_gla)
    da, dw_a2p, db_a2 = gate_bwd(proj, wts["gla_w_a2"], wts["gla_b_a2"], dla)
    dproj = jnp.concatenate([dq_g, dk_g, dv_g, dr, da], axis=1)
    assert dproj.shape[1] == GLA_IN_PAD
    dw_in = mm_tn(n1, dproj, 1, name="gla_in_dw")
    plan.grads["in"] = dw_in[0, :, :GLA_IN_DIM].reshape(D_MODEL, N_DEV, GLA_IN_DIM // N_DEV).transpose(1, 0, 2)
    dn1 = plan.run("gla_in_dx", mm_nt, dproj, wts["gla_w_in"], out_dtype=F32, name="gla_in_dx")
    grad_x, (d_attn0,) = rms_bwd(x, [row(attn_norm[0])], [dn1], dh1, name="attn_norm_bwd0")

    small = dict(
        attn_norm=jnp.concatenate([d_attn0, d_attn1], axis=0),
        ffn_norm=jnp.concatenate([d_ffn0, d_ffn1], axis=0),
        kv_norm=d_kvnorm.reshape(-1),
        final_norm=d_final.reshape(-1),
        ffn_conv_b=jnp.concatenate([dcb0, dcb1], axis=0),
        gla_w_a2=dw_a2p[:GATE_RANK],
        gla_b_a2=db_a2,
        gla_head_norm=d_hn,
        ffn_conv_w=jnp.stack([dcw0, dcw1]),
    )
    return loss_tile, grad_x, small


SMALL_ORDER = ("attn_norm", "ffn_norm", "kv_norm", "final_norm", "ffn_conv_b",
               "gla_w_a2", "gla_b_a2", "gla_head_norm", "ffn_conv_w")
SMALL_FULL = dict(attn_norm=(2, D_MODEL), ffn_norm=(2, D_MODEL), kv_norm=(D_MODEL,), final_norm=(D_MODEL,),
                  ffn_conv_b=(2, D_FF), gla_w_a2=(GATE_RANK, GLA_KEY_DIM), gla_b_a2=(1, GLA_KEY_DIM),
                  gla_head_norm=(1, GLA_DV), ffn_conv_w=(2, 3, D_FF))
SMALL_SHARDED = ("gla_w_a2", "gla_b_a2", "gla_head_norm", "ffn_conv_w")
SMALL_GRAD_ROWS = 592
SMALL_ADAM_ROWS = 240


def kernel(x, attn_norm, gla_w_in, gla_w_a2, gla_b_a2, gla_head_norm, gla_w_out, kv_norm, w_kv, dsa_w_q, dsa_w_out, ffn_norm, ffn_w_up, ffn_conv_w, ffn_conv_b, ffn_w_down, final_norm, loss_target, m_attn_norm, m_gla_w_in, m_gla_w_a2, m_gla_b_a2, m_gla_head_norm, m_gla_w_out, m_kv_norm, m_w_kv, m_dsa_w_q, m_dsa_w_out, m_ffn_norm, m_ffn_w_up, m_ffn_conv_w, m_ffn_conv_b, m_ffn_w_down, m_final_norm, v_attn_norm, v_gla_w_in, v_gla_w_a2, v_gla_b_a2, v_gla_head_norm, v_gla_w_out, v_kv_norm, v_w_kv, v_dsa_w_q, v_dsa_w_out, v_ffn_norm, v_ffn_w_up, v_ffn_conv_w, v_ffn_conv_b, v_ffn_w_down, v_final_norm):
    me = 4 * lax.axis_index("x") + 2 * lax.axis_index("y") + lax.axis_index("c")
    bf = lambda a: a.astype(BF16)

    g_in, g_small = all_gather([bf(gla_w_in[0]), _pack_small_weights(gla_w_a2, gla_b_a2, gla_head_norm, ffn_conv_w)],
                               name="gather_first")
    w_a2_full, b_a2_full, hn_full, conv_w_full = _unpack_small_weights(g_small)
    w_in_full = jnp.pad(g_in.transpose(1, 0, 2).reshape(D_MODEL, GLA_IN_DIM), ((0, 0), (0, GLA_IN_PAD - GLA_IN_DIM)))
    wts = dict(
        attn_norm=attn_norm, ffn_norm=ffn_norm, kv_norm=kv_norm, final_norm=final_norm, ffn_conv_b=ffn_conv_b,
        gla_w_in=w_in_full[None],
        gla_w_a2=jnp.pad(bf(w_a2_full), ((0, LANE - GATE_RANK), (0, 0))),
        gla_b_a2=b_a2_full, gla_head_norm=hn_full, ffn_conv_w=conv_w_full,
    )
    plan = Plan({}, srcs=dict(gout=bf(gla_w_out[0]), kv=bf(w_kv), q=bf(dsa_w_q[0]), dout=bf(dsa_w_out[0]),
                              up0=bf(ffn_w_up[0]), up1=bf(ffn_w_up[1]), dn0=bf(ffn_w_down[0]), dn1=bf(ffn_w_down[1])))

    loss_tile, grad_x, small = local_step(x[0], loss_target[0], wts, plan)
    loss = lax.psum(loss_tile[0, 0], ("x", "y", "c"))

    plan.run("grads_tail", exchange_only, name="grads_tail")
    shard3 = lambda a: a.reshape((-1,) + a.shape[-2:])
    big_params = dict(gla_w_in=(("in",), gla_w_in, m_gla_w_in, v_gla_w_in),
                      gla_w_out=(("gout",), gla_w_out, m_gla_w_out, v_gla_w_out),
                      w_kv=(("kv",), w_kv, m_w_kv, v_w_kv),
                      dsa_w_q=(("q",), dsa_w_q, m_dsa_w_q, v_dsa_w_q),
                      dsa_w_out=(("dout",), dsa_w_out, m_dsa_w_out, v_dsa_w_out),
                      ffn_w_up=(("up0", "up1"), ffn_w_up, m_ffn_w_up, v_ffn_w_up),
                      ffn_w_down=(("dn0", "dn1"), ffn_w_down, m_ffn_w_down, v_ffn_w_down))
    res = {}
    for nm, (parts, w, m, v) in big_params.items():
        outs = adam_sharded([plan.recv[p] for p in parts], shard3(w), shard3(m), shard3(v), name=f"adam_{nm}")
        res[nm] = [o.reshape(w.shape) for o in outs]

    packed = _pack_rows([small[nm] for nm in SMALL_ORDER], SMALL_GRAD_ROWS)
    (parts,) = all_gather([packed], name="gather_small_grads")
    full = dict(zip(SMALL_ORDER, _unpack_rows(sum_partials(parts), [SMALL_FULL[nm] for nm in SMALL_ORDER])))
    local_w = dict(attn_norm=attn_norm, ffn_norm=ffn_norm, kv_norm=kv_norm, final_norm=final_norm,
                   ffn_conv_b=ffn_conv_b, gla_w_a2=gla_w_a2, gla_b_a2=gla_b_a2, gla_head_norm=gla_head_norm,
                   ffn_conv_w=ffn_conv_w)
    local_m = dict(attn_norm=m_attn_norm, ffn_norm=m_ffn_norm, kv_norm=m_kv_norm, final_norm=m_final_norm,
                   ffn_conv_b=m_ffn_conv_b, gla_w_a2=m_gla_w_a2, gla_b_a2=m_gla_b_a2, gla_head_norm=m_gla_head_norm,
                   ffn_conv_w=m_ffn_conv_w)
    local_v = dict(attn_norm=v_attn_norm, ffn_norm=v_ffn_norm, kv_norm=v_kv_norm, final_norm=v_final_norm,
                   ffn_conv_b=v_ffn_conv_b, gla_w_a2=v_gla_w_a2, gla_b_a2=v_gla_b_a2, gla_head_norm=v_gla_head_norm,
                   ffn_conv_w=v_ffn_conv_w)
    local_g = {}
    for nm in SMALL_ORDER:
        gfull = full[nm]
        if nm in SMALL_SHARDED:
            per = gfull.shape[-1] // N_DEV
            gfull = lax.dynamic_slice_in_dim(gfull, me * per, per, axis=gfull.ndim - 1)
        local_g[nm] = gfull.reshape(local_w[nm].shape)
    shapes = [local_w[nm].shape for nm in SMALL_ORDER]
    pk = lambda dd: _pack_rows([dd[nm] for nm in SMALL_ORDER], SMALL_ADAM_ROWS)
    d_p, m_p, v_p = adam_packed(pk(local_w), pk(local_g), pk(local_m), pk(local_v))
    for nm, dl, mn, vn in zip(SMALL_ORDER, _unpack_rows(d_p, shapes), _unpack_rows(m_p, shapes),
                              _unpack_rows(v_p, shapes)):
        res[nm] = [local_g[nm], dl, mn, vn]

    order = ("attn_norm", "gla_w_in", "gla_w_a2", "gla_b_a2", "gla_head_norm", "gla_w_out", "kv_norm", "w_kv",
             "dsa_w_q", "dsa_w_out", "ffn_norm", "ffn_w_up", "ffn_conv_w", "ffn_conv_b", "ffn_w_down", "final_norm")
    outs = [loss, grad_x[None]]
    for kind in range(4):
        outs.extend(res[nm][kind] for nm in order)
    return tuple(outs)
```

```python
import functools

import jax
import jax.numpy as jnp
from jax import lax
from jax.experimental import pallas as pl
from jax.experimental.pallas import tpu as pltpu

F32 = jnp.float32
BF16 = jnp.bfloat16
MESH = pl.DeviceIdType.MESH
ANY = pl.BlockSpec(memory_space=pl.ANY)

N_DEV = 8
D_MODEL = 2048
GLA_HEADS = 4
GLA_KEY_DIM = 1024
GLA_VAL_DIM = 2048
GLA_DK = 256
GLA_DV = 512
GATE_RANK = 16
GATE_NORMALIZER = 16.0
GLA_CHUNK = 64
GLA_IN_DIM = 2 * GLA_KEY_DIM + 2 * GLA_VAL_DIM + GATE_RANK
GLA_IN_PAD = 6272
ATT_HEADS = 16
HEAD_DIM = 128
DILATIONS = (1, 4, 16)
ATT_BLOCK = 128
D_FF = 5632
EPS = 1e-6
ADAM_LR = 0.001
ADAM_B1 = 0.9
ADAM_B2 = 0.999
ADAM_EPS = 1e-08
ADAM_WD = 0.01
ADAM_STEP = 10
NEG = -1e30
LANE = 128
VMEM_LIMIT = 52 * 1024 * 1024
ALIBI_SLOPES = tuple(2.0 ** (-0.5 * (i + 1)) for i in range(ATT_HEADS))


def _params(*sem):
    return pltpu.CompilerParams(dimension_semantics=sem, vmem_limit_bytes=VMEM_LIMIT)


def _tile(n, cap):
    best = None
    for t in range(LANE, min(n, cap) + 1, LANE):
        if n % t == 0:
            best = t
    return best if best is not None else n


def _rows(r, c, budget=256 * 1024):
    best = None
    for t in range(16, r + 1, 16):
        if r % t == 0 and t * c <= budget:
            best = t
    return best if best is not None else r


def _flip(coord, bit):
    return 1 - coord if bit else coord


def _place():
    x, y, c = lax.axis_index("x"), lax.axis_index("y"), lax.axis_index("c")
    return x, y, c, 4 * x + 2 * y + c


def _rows_of(ref, rows):
    return ref if rows is None else ref.at[pl.ds(rows[0], rows[1] - rows[0])]


class Jobs:
    def __init__(self):
        self.srcs = []
        self.bufs = []
        self.sems = []
        self.steps = []

    def _src(self, a):
        for i, b in enumerate(self.srcs):
            if b is a:
                return i
        self.srcs.append(a)
        return len(self.srcs) - 1

    def new(self, shape, dtype):
        self.bufs.append((None, jax.ShapeDtypeStruct(shape, dtype)))
        return len(self.bufs) - 1

    def thru(self, a):
        self.bufs.append((a, jax.ShapeDtypeStruct(a.shape, a.dtype)))
        return len(self.bufs) - 1

    def _sem(self, n):
        self.sems.append(pltpu.SemaphoreType.DMA((n,)))
        return len(self.sems) - 1

    def gather_ici(self, src, buf, rows=None):
        si, send, recv, loc = self._src(src), self._sem(4), self._sem(4), self._sem(1)

        def remote(srcs, bufs, sems, slot_of):
            x, y, c, me = _place()
            peers = [(x, y, 1 - c), (1 - x, y, c), (x, 1 - y, c), (1 - x, 1 - y, c)]
            return [pltpu.make_async_remote_copy(
                src_ref=_rows_of(srcs[si], rows),
                dst_ref=_rows_of(bufs[buf].at[me if slot_of == "mine" else 4 * p[0] + 2 * p[1] + p[2]], rows),
                send_sem=sems[send].at[k], recv_sem=sems[recv].at[k], device_id=p, device_id_type=MESH)
                for k, p in enumerate(peers)]

        def local(srcs, bufs, sems):
            return pltpu.make_async_copy(_rows_of(srcs[si], rows), _rows_of(bufs[buf].at[_place()[3]], rows),
                                         sems[loc].at[0])

        def start(srcs, bufs, sems):
            local(srcs, bufs, sems).start()
            for cp in remote(srcs, bufs, sems, "mine"):
                cp.start()

        def finish(srcs, bufs, sems):
            for cp in remote(srcs, bufs, sems, "peer"):
                cp.wait_recv()
            for cp in remote(srcs, bufs, sems, "mine"):
                cp.wait_send()
            local(srcs, bufs, sems).wait()

        self.steps.append((start, finish))

    def gather_d2d(self, buf, rows=None):
        send, recv = self._sem(3), self._sem(3)

        def copies(bufs, sems, core):
            x, y, c, _ = _place()
            cc = c if core == "mine" else 1 - c
            chips = [(1 - x, y), (x, 1 - y), (1 - x, 1 - y)]
            return [pltpu.make_async_remote_copy(
                src_ref=_rows_of(bufs[buf].at[4 * px + 2 * py + cc], rows),
                dst_ref=_rows_of(bufs[buf].at[4 * px + 2 * py + cc], rows),
                send_sem=sems[send].at[k], recv_sem=sems[recv].at[k],
                device_id=(x, y, 1 - c), device_id_type=MESH) for k, (px, py) in enumerate(chips)]

        def start(srcs, bufs, sems):
            for cp in copies(bufs, sems, "mine"):
                cp.start()

        def finish(srcs, bufs, sems):
            for cp in copies(bufs, sems, "sibling"):
                cp.wait_recv()
            for cp in copies(bufs, sems, "mine"):
                cp.wait_send()

        self.steps.append((start, finish))

    def scatter(self, src, buf, rows=None):
        si, send, recv, loc = self._src(src), self._sem(N_DEV - 1), self._sem(N_DEV - 1), self._sem(1)

        def remote(srcs, bufs, sems, slot_of):
            x, y, c, me = _place()
            out = []
            for k in range(1, N_DEV):
                px, py, pc = _flip(x, k >> 2), _flip(y, (k >> 1) & 1), _flip(c, k & 1)
                peer = 4 * px + 2 * py + pc
                out.append(pltpu.make_async_remote_copy(
                    src_ref=_rows_of(srcs[si].at[peer], rows),
                    dst_ref=_rows_of(bufs[buf].at[me if slot_of == "mine" else peer], rows),
                    send_sem=sems[send].at[k - 1], recv_sem=sems[recv].at[k - 1],
                    device_id=(px, py, pc), device_id_type=MESH))
            return out

        def local(srcs, bufs, sems):
            me = _place()[3]
            return pltpu.make_async_copy(_rows_of(srcs[si].at[me], rows), _rows_of(bufs[buf].at[me], rows),
                                         sems[loc].at[0])

        def start(srcs, bufs, sems):
            local(srcs, bufs, sems).start()
            for cp in remote(srcs, bufs, sems, "mine"):
                cp.start()

        def finish(srcs, bufs, sems):
            for cp in remote(srcs, bufs, sems, "peer"):
                cp.wait_recv()
            for cp in remote(srcs, bufs, sems, "mine"):
                cp.wait_send()
            local(srcs, bufs, sems).wait()

        self.steps.append((start, finish))


def _call(body, *, name, grid, in_specs, out_specs, out_shape, args, sem, scratch_shapes=(), jobs=None):
    in_specs, out_specs, out_shape = list(in_specs), list(out_specs), list(out_shape)
    scratch_shapes = list(scratch_shapes)
    if jobs is None:
        res = pl.pallas_call(body, name=name, out_shape=out_shape, grid=grid, in_specs=in_specs,
                             out_specs=out_specs, scratch_shapes=scratch_shapes,
                             compiler_params=_params(*sem))(*args)
        return list(res), []
    thru = [a for a, _ in jobs.bufs if a is not None]
    n_in, n_src, n_thru = len(args), len(jobs.srcs), len(thru)
    n_out, n_buf, n_scr = len(out_shape), len(jobs.bufs), len(scratch_shapes)
    aliases, t = {}, 0
    for b, (a, _) in enumerate(jobs.bufs):
        if a is not None:
            aliases[n_in + n_src + t] = n_out + b
            t += 1

    def wrapped(*refs):
        at = 0
        ins = refs[at:at + n_in]; at += n_in
        srcs = refs[at:at + n_src]; at += n_src + n_thru
        outs = refs[at:at + n_out]; at += n_out
        bufs = refs[at:at + n_buf]; at += n_buf
        scr = refs[at:at + n_scr]; at += n_scr
        sems = refs[at:]
        first, last = None, None
        for axis, size in enumerate(grid):
            pid = pl.program_id(axis)
            f, l = pid == 0, pid == size - 1
            first = f if first is None else first & f
            last = l if last is None else last & l

        @pl.when(first)
        def _():
            for start, _ in jobs.steps:
                start(srcs, bufs, sems)

        body(*ins, *outs, *scr)

        @pl.when(last)
        def _():
            for _, finish in jobs.steps:
                finish(srcs, bufs, sems)

    res = pl.pallas_call(
        wrapped, name=name,
        out_shape=out_shape + [s for _, s in jobs.bufs],
        grid=grid,
        in_specs=in_specs + [ANY] * (n_src + n_thru),
        out_specs=out_specs + [ANY] * n_buf,
        scratch_shapes=scratch_shapes + jobs.sems,
        input_output_aliases=aliases,
        compiler_params=_params(*(["arbitrary"] * len(grid))),
    )(*args, *jobs.srcs, *thru)
    return res[:n_out], res[n_out:]


def mm_nn(a, w, *, out_dtype, name, res=None, tm=None, jobs=None):
    m, k = a.shape
    tm = tm or (1024 if a.dtype == BF16 else 512)
    j, k2, ns = w.shape
    assert k == k2 and m % tm == 0
    tn = _tile(ns, 1408)
    nsub = ns // tn
    tk = k if k <= 2048 else _tile(k, 1408)
    nk = k // tk
    has_res = res is not None

    def body(*refs):
        if has_res:
            a_ref, w_ref, r_ref, o_ref, acc = refs
        else:
            a_ref, w_ref, o_ref, acc = refs
        kk = pl.program_id(2)

        @pl.when(kk == 0)
        def _():
            acc[...] = jnp.zeros_like(acc)

        acc[...] += jnp.dot(a_ref[...].astype(BF16), w_ref[...], preferred_element_type=F32)

        @pl.when(kk == nk - 1)
        def _():
            r = acc[...]
            if has_res:
                r = r + r_ref[...]
            o_ref[...] = r.astype(out_dtype)

    in_specs = [
        pl.BlockSpec((tm, tk), lambda i, n, kk: (i, kk)),
        pl.BlockSpec((None, tk, tn), lambda i, n, kk: (n // nsub, kk, n % nsub)),
    ]
    args = [a, w]
    if has_res:
        in_specs.append(pl.BlockSpec((tm, tn), lambda i, n, kk: (i, n)))
        args.append(res)
    (out,), bufs = _call(
        body, name=name, jobs=jobs,
        out_shape=[jax.ShapeDtypeStruct((m, j * ns), out_dtype)],
        grid=(m // tm, j * nsub, nk),
        in_specs=in_specs,
        out_specs=[pl.BlockSpec((tm, tn), lambda i, n, kk: (i, n))],
        scratch_shapes=[pltpu.VMEM((tm, tn), F32)],
        args=args, sem=("parallel", "parallel", "arbitrary"))
    return out if jobs is None else (out, bufs)


def mm_nt(dy, w, *, out_dtype, name, tm=1024, jobs=None):
    parts, m, n = (1,) + dy.shape if dy.ndim == 2 else dy.shape
    n *= parts
    j, k, ns = w.shape
    assert n == j * ns and m % tm == 0
    tn = _tile(ns, 2048)
    nsub = ns // tn
    tko = _tile(k, 1408)
    nn = j * nsub
    per_part = nn // parts
    if dy.ndim == 2:
        dy_spec = pl.BlockSpec((tm, tn), lambda i, ko, nq: (i, nq))
    else:
        dy_spec = pl.BlockSpec((None, tm, tn), lambda i, ko, nq: (nq // per_part, i, nq % per_part))

    def body(a_ref, w_ref, o_ref, acc):
        nq = pl.program_id(2)

        @pl.when(nq == 0)
        def _():
            acc[...] = jnp.zeros_like(acc)

        acc[...] += lax.dot_general(a_ref[...].astype(BF16), w_ref[...], (((1,), (1,)), ((), ())),
                                    preferred_element_type=F32)

        @pl.when(nq == nn - 1)
        def _():
            o_ref[...] = acc[...].astype(out_dtype)

    (out,), bufs = _call(
        body, name=name, jobs=jobs,
        out_shape=[jax.ShapeDtypeStruct((m, k), out_dtype)],
        grid=(m // tm, k // tko, nn),
        in_specs=[
            dy_spec,
            pl.BlockSpec((None, tko, tn), lambda i, ko, nq: (nq // nsub, ko, nq % nsub)),
        ],
        out_specs=[pl.BlockSpec((tm, tko), lambda i, ko, nq: (i, ko))],
        scratch_shapes=[pltpu.VMEM((tm, tko), F32)],
        args=[dy, w], sem=("parallel", "parallel", "arbitrary"))
    return out if jobs is None else (out, bufs)


def mm_tn(x, dy, j, *, name, tm=1024):
    m, k = x.shape
    parts, m2, n = (1,) + dy.shape if dy.ndim == 2 else dy.shape
    n *= parts
    assert m == m2 and n % j == 0 and m % tm == 0
    ns = n // j
    tn = _tile(ns, 1408)
    nsub = ns // tn
    tk = _tile(k, 1408)
    nm = m // tm
    per_part = j * nsub // parts
    if dy.ndim == 2:
        dy_spec = pl.BlockSpec((tm, tn), lambda kq, nq, mi: (mi, nq))
    else:
        dy_spec = pl.BlockSpec((None, tm, tn), lambda kq, nq, mi: (nq // per_part, mi, nq % per_part))

    def body(x_ref, dy_ref, o_ref, acc):
        mi = pl.program_id(2)

        @pl.when(mi == 0)
        def _():
            acc[...] = jnp.zeros_like(acc)

        acc[...] += lax.dot_general(x_ref[...].astype(BF16), dy_ref[...].astype(BF16), (((0,), (0,)), ((), ())),
                                    preferred_element_type=F32)

        @pl.when(mi == nm - 1)
        def _():
            o_ref[...] = acc[...].astype(BF16)

    return pl.pallas_call(
        body,
        name=name,
        out_shape=jax.ShapeDtypeStruct((j, k, ns), BF16),
        grid=(k // tk, j * nsub, nm),
        in_specs=[
            pl.BlockSpec((tm, tk), lambda kq, nq, mi: (mi, kq)),
            dy_spec,
        ],
        out_specs=pl.BlockSpec((None, tk, tn), lambda kq, nq, mi: (nq // nsub, kq, nq % nsub)),
        scratch_shapes=[pltpu.VMEM((tk, tn), F32)],
        compiler_params=_params("parallel", "parallel", "arbitrary"),
    )(x, dy)


def rms_fwd(x, gains, *, name, ts=512):
    s, d = x.shape
    n = len(gains)

    def body(x_ref, *refs):
        xv = x_ref[...]
        xh = xv * lax.rsqrt(jnp.mean(xv * xv, axis=-1, keepdims=True) + EPS)
        for g_ref, o_ref in zip(refs[:n], refs[n:]):
            o_ref[...] = (xh * g_ref[...]).astype(BF16)

    row = pl.BlockSpec((ts, d), lambda i: (i, 0))
    vec = pl.BlockSpec((1, d), lambda i: (0, 0))
    return pl.pallas_call(
        body,
        name=name,
        out_shape=[jax.ShapeDtypeStruct((s, d), BF16)] * n,
        grid=(s // ts,),
        in_specs=[row] + [vec] * n,
        out_specs=[row] * n,
        compiler_params=_params("parallel"),
    )(x, *gains)


def rms_bwd(x, gains, dys, dres, *, name, ts=256):
    s, d = x.shape
    n = len(gains)

    def body(x_ref, r_ref, *refs):
        g_refs, dy_refs = refs[:n], refs[n:2 * n]
        dx_ref, dg_refs = refs[2 * n], refs[2 * n + 1:]
        i = pl.program_id(0)
        xv = x_ref[...]
        r = lax.rsqrt(jnp.mean(xv * xv, axis=-1, keepdims=True) + EPS)
        xh = xv * r
        acc = r_ref[...]
        for g_ref, dy_ref, dg_ref in zip(g_refs, dy_refs, dg_refs):
            dy = dy_ref[...].astype(F32)

            @pl.when(i == 0)
            def _():
                dg_ref[...] = jnp.zeros_like(dg_ref)

            dg_ref[...] += jnp.sum(dy * xh, axis=0, keepdims=True)
            dxh = dy * g_ref[...]
            acc = acc + r * (dxh - xh * jnp.mean(dxh * xh, axis=-1, keepdims=True))
        dx_ref[...] = acc

    row = pl.BlockSpec((ts, d), lambda i: (i, 0))
    vec = pl.BlockSpec((1, d), lambda i: (0, 0))
    outs = pl.pallas_call(
        body,
        name=name,
        out_shape=[jax.ShapeDtypeStruct((s, d), F32)] + [jax.ShapeDtypeStruct((1, d), F32)] * n,
        grid=(s // ts,),
        in_specs=[row, row] + [vec] * n + [row] * n,
        out_specs=[row] + [vec] * n,
        compiler_params=_params("arbitrary"),
    )(x, dres, *gains, *dys)
    return outs[0], outs[1:]


def loss_head(h, gain, target, *, ts=256):
    s, d = h.shape

    def body(h_ref, g_ref, t_ref, l_ref, dh_ref, dg_ref):
        i = pl.program_id(0)

        @pl.when(i == 0)
        def _():
            l_ref[...] = jnp.zeros_like(l_ref)
            dg_ref[...] = jnp.zeros_like(dg_ref)

        xv = h_ref[...]
        r = lax.rsqrt(jnp.mean(xv * xv, axis=-1, keepdims=True) + EPS)
        xh = xv * r
        g = g_ref[...]
        err = xh * g - t_ref[...]
        l_ref[...] += 0.5 * jnp.sum(jnp.mean(err * err, axis=-1, keepdims=True))
        dy = err * (1.0 / d)
        dg_ref[...] += jnp.sum(dy * xh, axis=0, keepdims=True)
        dxh = dy * g
        dh_ref[...] = r * (dxh - xh * jnp.mean(dxh * xh, axis=-1, keepdims=True))

    row = pl.BlockSpec((ts, d), lambda i: (i, 0))
    vec = pl.BlockSpec((1, d), lambda i: (0, 0))
    return pl.pallas_call(
        body,
        name="loss_head",
        out_shape=[jax.ShapeDtypeStruct((8, LANE), F32), jax.ShapeDtypeStruct((s, d), F32),
                   jax.ShapeDtypeStruct((1, d), F32)],
        grid=(s // ts,),
        in_specs=[row, vec, row],
        out_specs=[pl.BlockSpec((8, LANE), lambda i: (0, 0)), row, vec],
        compiler_params=_params("arbitrary"),
    )(h, gain, target)


A_BLOCK = (2 * GLA_KEY_DIM + 2 * GLA_VAL_DIM) // LANE


def gate_fwd(proj, w_a2p, b_a2, *, ts=512):
    s = proj.shape[0]

    def body(a_ref, w_ref, b_ref, o_ref):
        z = jnp.dot(a_ref[...].astype(BF16), w_ref[...], preferred_element_type=F32) + b_ref[...]
        o_ref[...] = (jnp.minimum(z, 0.0) - jnp.log(1.0 + jnp.exp(-jnp.abs(z)))) * (1.0 / GATE_NORMALIZER)

    return pl.pallas_call(
        body,
        name="gate_fwd",
        out_shape=jax.ShapeDtypeStruct((s, GLA_KEY_DIM), F32),
        grid=(s // ts,),
        in_specs=[pl.BlockSpec((ts, LANE), lambda i: (i, A_BLOCK)),
                  pl.BlockSpec((LANE, GLA_KEY_DIM), lambda i: (0, 0)),
                  pl.BlockSpec((1, GLA_KEY_DIM), lambda i: (0, 0))],
        out_specs=pl.BlockSpec((ts, GLA_KEY_DIM), lambda i: (i, 0)),
        compiler_params=_params("parallel"),
    )(proj, w_a2p, b_a2)


def gate_bwd(proj, w_a2p, b_a2, dla, *, ts=512):
    s = proj.shape[0]

    def body(a_ref, w_ref, b_ref, dla_ref, da_ref, dw_ref, db_ref):
        i = pl.program_id(0)

        @pl.when(i == 0)
        def _():
            dw_ref[...] = jnp.zeros_like(dw_ref)
            db_ref[...] = jnp.zeros_like(db_ref)

        a = a_ref[...].astype(BF16)
        w = w_ref[...]
        z = jnp.dot(a, w, preferred_element_type=F32) + b_ref[...]
        dz = dla_ref[...] * (1.0 / GATE_NORMALIZER) / (1.0 + jnp.exp(z))
        dzb = dz.astype(BF16)
        da_ref[...] = lax.dot_general(dzb, w, (((1,), (1,)), ((), ())), preferred_element_type=F32).astype(BF16)
        dw_ref[...] += lax.dot_general(a, dzb, (((0,), (0,)), ((), ())), preferred_element_type=F32)
        db_ref[...] += jnp.sum(dz, axis=0, keepdims=True)

    return pl.pallas_call(
        body,
        name="gate_bwd",
        out_shape=[jax.ShapeDtypeStruct((s, LANE), BF16), jax.ShapeDtypeStruct((LANE, GLA_KEY_DIM), F32),
                   jax.ShapeDtypeStruct((1, GLA_KEY_DIM), F32)],
        grid=(s // ts,),
        in_specs=[pl.BlockSpec((ts, LANE), lambda i: (i, A_BLOCK)),
                  pl.BlockSpec((LANE, GLA_KEY_DIM), lambda i: (0, 0)),
                  pl.BlockSpec((1, GLA_KEY_DIM), lambda i: (0, 0)),
                  pl.BlockSpec((ts, GLA_KEY_DIM), lambda i: (i, 0))],
        out_specs=[pl.BlockSpec((ts, LANE), lambda i: (i, 0)),
                   pl.BlockSpec((LANE, GLA_KEY_DIM), lambda i: (0, 0)),
                   pl.BlockSpec((1, GLA_KEY_DIM), lambda i: (0, 0))],
        compiler_params=_params("arbitrary"),
    )(proj, w_a2p, b_a2, dla)


def _chunk_terms(q_ref, k_ref, la_ref):
    c_len = GLA_CHUNK
    row = lax.broadcasted_iota(jnp.int32, (c_len, c_len), 0)
    col = lax.broadcasted_iota(jnp.int32, (c_len, c_len), 1)
    tri = row >= col
    la = la_ref[...]
    c = jnp.dot(tri.astype(F32), la, preferred_element_type=F32, precision=lax.Precision.HIGHEST)
    last = jnp.sum(la, axis=0, keepdims=True)
    k = k_ref[...]
    q_dec = q_ref[...] * (GLA_DK ** -0.5) * jnp.exp(c)
    k_inv = k * jnp.exp(-c)
    k_end = k * jnp.exp(last - c)
    return c, last, q_dec, k_inv, k_end, tri


def _dot(a, b, ca, cb):
    return lax.dot_general(a.astype(BF16), b.astype(BF16), (((ca,), (cb,)), ((), ())), preferred_element_type=F32)


def gla_fwd(proj, la, jobs=None):
    s = proj.shape[0]
    n_chunks = s // GLA_CHUNK
    kb = GLA_KEY_DIM // GLA_DK

    def body(q_ref, k_ref, v_ref, la_ref, o_ref, st_out, st):
        @pl.when(pl.program_id(1) == 0)
        def _():
            st[...] = jnp.zeros_like(st)

        _, last, q_dec, k_inv, k_end, tri = _chunk_terms(q_ref, k_ref, la_ref)
        v = v_ref[...]
        a = jnp.where(tri, _dot(q_dec, k_inv, 1, 1), 0.0)
        state = st[...]
        st_out[...] = state
        o_ref[...] = _dot(a, v, 1, 0) + _dot(q_dec, state, 1, 1)
        st[...] = state * jnp.exp(last) + _dot(v, k_end, 0, 0)

    outs, bufs = _call(
        body, name="gla_fwd", jobs=jobs,
        out_shape=[jax.ShapeDtypeStruct((s, GLA_VAL_DIM), F32),
                   jax.ShapeDtypeStruct((GLA_HEADS, n_chunks, GLA_DV, GLA_DK), F32)],
        grid=(GLA_HEADS, n_chunks),
        in_specs=[pl.BlockSpec((GLA_CHUNK, GLA_DK), lambda h, n: (n, h)),
                  pl.BlockSpec((GLA_CHUNK, GLA_DK), lambda h, n: (n, kb + h)),
                  pl.BlockSpec((GLA_CHUNK, GLA_DV), lambda h, n: (n, kb + h)),
                  pl.BlockSpec((GLA_CHUNK, GLA_DK), lambda h, n: (n, h))],
        out_specs=[pl.BlockSpec((GLA_CHUNK, GLA_DV), lambda h, n: (n, h)),
                   pl.BlockSpec((None, None, GLA_DV, GLA_DK), lambda h, n: (h, n, 0, 0))],
        scratch_shapes=[pltpu.VMEM((GLA_DV, GLA_DK), F32)],
        args=[proj, proj, proj, la], sem=("parallel", "arbitrary"))
    return outs if jobs is None else (outs, bufs)


def gla_bwd(proj, la, states, do, jobs=None):
    s = proj.shape[0]
    n_chunks = s // GLA_CHUNK
    kb = GLA_KEY_DIM // GLA_DK
    lastc = n_chunks - 1

    def body(q_ref, k_ref, v_ref, la_ref, do_ref, st_ref, dq_ref, dk_ref, dv_ref, dla_ref, dst):
        @pl.when(pl.program_id(1) == 0)
        def _():
            dst[...] = jnp.zeros_like(dst)

        c, last, q_dec, k_inv, k_end, tri = _chunk_terms(q_ref, k_ref, la_ref)
        v = v_ref[...]
        dout = do_ref[...]
        state = st_ref[...]
        dstate = dst[...]
        e_last = jnp.exp(last)
        a = jnp.where(tri, _dot(q_dec, k_inv, 1, 1), 0.0)
        da = jnp.where(tri, _dot(dout, v, 1, 1), 0.0)
        dv_ref[...] = (_dot(a, dout, 0, 0) + _dot(k_end, dstate, 1, 1)).astype(BF16)
        dq_dec = _dot(da, k_inv, 1, 0) + _dot(dout, state, 1, 0)
        dk_inv = _dot(da, q_dec, 0, 0)
        dk_end = _dot(v, dstate, 1, 0)
        dst[...] = dstate * e_last + _dot(dout, q_dec, 0, 0)
        e_c = jnp.exp(c)
        dq_ref[...] = (dq_dec * (GLA_DK ** -0.5) * e_c).astype(BF16)
        dk_ref[...] = (dk_inv * jnp.exp(-c) + dk_end * jnp.exp(last - c)).astype(BF16)
        ke_term = dk_end * k_end
        dc = dq_dec * q_dec - dk_inv * k_inv - ke_term
        dlast = jnp.sum(ke_term, axis=0, keepdims=True) + e_last * jnp.sum(dstate * state, axis=0, keepdims=True)
        upper = lax.broadcasted_iota(jnp.int32, tri.shape, 0) <= lax.broadcasted_iota(jnp.int32, tri.shape, 1)
        dla_ref[...] = jnp.dot(upper.astype(F32), dc, preferred_element_type=F32,
                               precision=lax.Precision.HIGHEST) + dlast

    outs, bufs = _call(
        body, name="gla_bwd", jobs=jobs,
        out_shape=[jax.ShapeDtypeStruct((s, GLA_KEY_DIM), BF16), jax.ShapeDtypeStruct((s, GLA_KEY_DIM), BF16),
                   jax.ShapeDtypeStruct((s, GLA_VAL_DIM), BF16), jax.ShapeDtypeStruct((s, GLA_KEY_DIM), F32)],
        grid=(GLA_HEADS, n_chunks),
        in_specs=[pl.BlockSpec((GLA_CHUNK, GLA_DK), lambda h, n: (lastc - n, h)),
                  pl.BlockSpec((GLA_CHUNK, GLA_DK), lambda h, n: (lastc - n, kb + h)),
                  pl.BlockSpec((GLA_CHUNK, GLA_DV), lambda h, n: (lastc - n, kb + h)),
                  pl.BlockSpec((GLA_CHUNK, GLA_DK), lambda h, n: (lastc - n, h)),
                  pl.BlockSpec((GLA_CHUNK, GLA_DV), lambda h, n: (lastc - n, h)),
                  pl.BlockSpec((None, None, GLA_DV, GLA_DK), lambda h, n: (h, lastc - n, 0, 0))],
        out_specs=[pl.BlockSpec((GLA_CHUNK, GLA_DK), lambda h, n: (lastc - n, h)),
                   pl.BlockSpec((GLA_CHUNK, GLA_DK), lambda h, n: (lastc - n, h)),
                   pl.BlockSpec((GLA_CHUNK, GLA_DV), lambda h, n: (lastc - n, h)),
                   pl.BlockSpec((GLA_CHUNK, GLA_DK), lambda h, n: (lastc - n, h))],
        scratch_shapes=[pltpu.VMEM((GLA_DV, GLA_DK), F32)],
        args=[proj, proj, proj, la, do, states], sem=("parallel", "arbitrary"))
    return outs if jobs is None else (outs, bufs)


R_BLOCK = (2 * GLA_KEY_DIM + GLA_VAL_DIM) // GLA_DV


def headnorm_fwd(o, proj, hn, *, ts=512):
    s = o.shape[0]

    def body(o_ref, r_ref, g_ref, out_ref):
        ov = o_ref[...]
        oh = ov * lax.rsqrt(jnp.mean(ov * ov, axis=-1, keepdims=True) + EPS)
        r = r_ref[...]
        out_ref[...] = (oh * g_ref[...] * (r * jax.nn.sigmoid(r))).astype(BF16)

    return pl.pallas_call(
        body,
        name="headnorm_fwd",
        out_shape=jax.ShapeDtypeStruct((s, GLA_VAL_DIM), BF16),
        grid=(s // ts, GLA_HEADS),
        in_specs=[pl.BlockSpec((ts, GLA_DV), lambda i, h: (i, h)),
                  pl.BlockSpec((ts, GLA_DV), lambda i, h: (i, R_BLOCK + h)),
                  pl.BlockSpec((1, GLA_DV), lambda i, h: (0, 0))],
        out_specs=pl.BlockSpec((ts, GLA_DV), lambda i, h: (i, h)),
        compiler_params=_params("parallel", "parallel"),
    )(o, proj, hn)


def headnorm_bwd(o, proj, hn, dog, *, ts=512):
    s = o.shape[0]

    def body(o_ref, r_ref, g_ref, dog_ref, do_ref, dr_ref, dg_ref):
        @pl.when((pl.program_id(0) == 0) & (pl.program_id(1) == 0))
        def _():
            dg_ref[...] = jnp.zeros_like(dg_ref)

        ov = o_ref[...]
        rr = lax.rsqrt(jnp.mean(ov * ov, axis=-1, keepdims=True) + EPS)
        oh = ov * rr
        g = g_ref[...]
        r = r_ref[...]
        sig = jax.nn.sigmoid(r)
        gate = r * sig
        dog_v = dog_ref[...]
        d_on = dog_v * gate
        dr_ref[...] = (dog_v * (oh * g) * (sig * (1.0 + r * (1.0 - sig)))).astype(BF16)
        dg_ref[...] += jnp.sum(d_on * oh, axis=0, keepdims=True)
        doh = d_on * g
        do_ref[...] = rr * (doh - oh * jnp.mean(doh * oh, axis=-1, keepdims=True))

    return pl.pallas_call(
        body,
        name="headnorm_bwd",
        out_shape=[jax.ShapeDtypeStruct((s, GLA_VAL_DIM), F32), jax.ShapeDtypeStruct((s, GLA_VAL_DIM), BF16),
                   jax.ShapeDtypeStruct((1, GLA_DV), F32)],
        grid=(s // ts, GLA_HEADS),
        in_specs=[pl.BlockSpec((ts, GLA_DV), lambda i, h: (i, h)),
                  pl.BlockSpec((ts, GLA_DV), lambda i, h: (i, R_BLOCK + h)),
                  pl.BlockSpec((1, GLA_DV), lambda i, h: (0, 0)),
                  pl.BlockSpec((ts, GLA_DV), lambda i, h: (i, h))],
        out_specs=[pl.BlockSpec((ts, GLA_DV), lambda i, h: (i, h)),
                   pl.BlockSpec((ts, GLA_DV), lambda i, h: (i, h)),
                   pl.BlockSpec((1, GLA_DV), lambda i, h: (0, 0))],
        compiler_params=_params("arbitrary", "arbitrary"),
    )(o, proj, hn, dog)


CONV_TC = 128
SQRT_HALF = 0.7071067811865476
INV_SQRT_2PI = 0.3989422804014327


def _conv_gate(g_ref, cw_ref, cb_ref):
    g0 = g_ref[...].astype(F32)
    t = lax.broadcasted_iota(jnp.int32, g0.shape, 0)
    g1 = jnp.where(t >= 1, pltpu.roll(g0, 1, 0), 0.0)
    g2 = jnp.where(t >= 2, pltpu.roll(g0, 2, 0), 0.0)
    gc = cw_ref[0:1, :] * g2 + cw_ref[1:2, :] * g1 + cw_ref[2:3, :] * g0 + cb_ref[...]
    return g0, g1, g2, gc, t


def convglu_fwd(up, conv_w, conv_b):
    s = up.shape[0]
    nc = D_FF // CONV_TC

    def body(u_ref, g_ref, cw_ref, cb_ref, o_ref):
        _, _, _, gc, _ = _conv_gate(g_ref, cw_ref, cb_ref)
        gelu = 0.5 * gc * (1.0 + lax.erf(gc * SQRT_HALF))
        o_ref[...] = (gelu * u_ref[...].astype(F32)).astype(BF16)

    return pl.pallas_call(
        body,
        name="convglu_fwd",
        out_shape=jax.ShapeDtypeStruct((s, D_FF), BF16),
        grid=(nc,),
        in_specs=[pl.BlockSpec((s, CONV_TC), lambda c: (0, c)),
                  pl.BlockSpec((s, CONV_TC), lambda c: (0, nc + c)),
                  pl.BlockSpec((3, CONV_TC), lambda c: (0, c)),
                  pl.BlockSpec((1, CONV_TC), lambda c: (0, c))],
        out_specs=pl.BlockSpec((s, CONV_TC), lambda c: (0, c)),
        compiler_params=_params("parallel"),
    )(up, up, conv_w, conv_b)


def convglu_bwd(up, conv_w, conv_b, dact, *, name, jobs=None):
    s = up.shape[0]
    nc = D_FF // CONV_TC

    def body(u_ref, g_ref, cw_ref, cb_ref, da_ref, dup_ref, dcw_ref, dcb_ref):
        du_ref, dg_ref = dup_ref.at[0], dup_ref.at[1]
        g0, g1, g2, gc, t = _conv_gate(g_ref, cw_ref, cb_ref)
        cdf = 0.5 * (1.0 + lax.erf(gc * SQRT_HALF))
        da = da_ref[...].astype(F32)
        du_ref[...] = (da * gc * cdf).astype(BF16)
        dgc = da * u_ref[...].astype(F32) * (cdf + gc * jnp.exp(-0.5 * gc * gc) * INV_SQRT_2PI)
        dcb_ref[...] = jnp.sum(dgc, axis=0, keepdims=True)
        dcw_ref[0:1, :] = jnp.sum(dgc * g2, axis=0, keepdims=True)
        dcw_ref[1:2, :] = jnp.sum(dgc * g1, axis=0, keepdims=True)
        dcw_ref[2:3, :] = jnp.sum(dgc * g0, axis=0, keepdims=True)
        n1 = jnp.where(t < s - 1, pltpu.roll(dgc, s - 1, 0), 0.0)
        n2 = jnp.where(t < s - 2, pltpu.roll(dgc, s - 2, 0), 0.0)
        dg_ref[...] = (cw_ref[2:3, :] * dgc + cw_ref[1:2, :] * n1 + cw_ref[0:1, :] * n2).astype(BF16)

    col = pl.BlockSpec((s, CONV_TC), lambda c: (0, c))
    outs, bufs = _call(
        body, name=name, jobs=jobs,
        out_shape=[jax.ShapeDtypeStruct((2, s, D_FF), BF16),
                   jax.ShapeDtypeStruct((3, D_FF), F32), jax.ShapeDtypeStruct((1, D_FF), F32)],
        grid=(nc,),
        in_specs=[col, pl.BlockSpec((s, CONV_TC), lambda c: (0, nc + c)),
                  pl.BlockSpec((3, CONV_TC), lambda c: (0, c)),
                  pl.BlockSpec((1, CONV_TC), lambda c: (0, c)), col],
        out_specs=[pl.BlockSpec((2, s, CONV_TC), lambda c: (0, 0, c)), pl.BlockSpec((3, CONV_TC), lambda c: (0, c)),
                   pl.BlockSpec((1, CONV_TC), lambda c: (0, c))],
        args=[up, up, conv_w, conv_b, dact], sem=("parallel",))
    return outs if jobs is None else (outs, bufs)


REGION = ATT_BLOCK * DILATIONS[-1]
SLOPE_TILE = (8, LANE)


def _slope_table():
    return jnp.broadcast_to(jnp.asarray(ALIBI_SLOPES, F32)[:, None, None], (ATT_HEADS,) + SLOPE_TILE)


def _sub(r, i, d):
    start = r + d * ATT_BLOCK * i
    return pl.ds(start, ATT_BLOCK) if d == 1 else pl.ds(start, ATT_BLOCK, stride=d)


def _att_bias(slope, d, first_key):
    qa = lax.broadcasted_iota(jnp.int32, (ATT_BLOCK, 2 * ATT_BLOCK), 0)
    cc = lax.broadcasted_iota(jnp.int32, (ATT_BLOCK, 2 * ATT_BLOCK), 1)
    dist = qa - cc + ATT_BLOCK
    ok = (dist >= 0) & (dist <= ATT_BLOCK) & (cc >= first_key)
    return jnp.where(ok, (slope * (-float(d))) * dist.astype(F32), NEG)


def _keys(kc_ref, kp_ref, r, i, d, nsub):
    prev = kp_ref[_sub(r, nsub - 1, d), :] if i == 0 else kc_ref[_sub(r, i - 1, d), :]
    return jnp.concatenate([prev, kc_ref[_sub(r, i, d), :]], axis=0)


def _per_residue(d, body):
    if d == 1:
        body(0)
    else:
        def step(r, carry):
            body(r)
            return carry
        lax.fori_loop(0, d, step, 0)


def attn_fwd(q, kv, jobs=None):
    s = q.shape[0]
    nreg = s // REGION
    scale = HEAD_DIM ** -0.5

    def body(sl_ref, q_ref, kc_ref, kp_ref, vc_ref, vp_ref, o_ref, lse_ref, ob, lb):
        n = pl.program_id(0)
        g = pl.program_id(2)
        slope = sl_ref[0:1, 0:1]
        first_key = jnp.where(n > 0, 0, ATT_BLOCK)

        def branch(gi, d):
            nsub = REGION // (ATT_BLOCK * d)
            bias = _att_bias(slope, d, 0)
            bias0 = _att_bias(slope, d, first_key)

            def residue(r):
                for i in range(nsub):
                    rows = _sub(r, i, d)
                    kcat = _keys(kc_ref, kp_ref, r, i, d, nsub)
                    vcat = _keys(vc_ref, vp_ref, r, i, d, nsub)
                    sc = _dot(q_ref[rows, :], kcat, 1, 1) * scale + (bias0 if i == 0 else bias)
                    m = jnp.max(sc, axis=-1, keepdims=True)
                    p = jnp.exp(sc - m)
                    l = jnp.sum(p, axis=-1, keepdims=True)
                    ob.at[gi][rows, :] = _dot(p, vcat, 1, 0) / l
                    lb.at[gi][rows, :] = jnp.broadcast_to(m + jnp.log(l), (ATT_BLOCK, HEAD_DIM))

            _per_residue(d, residue)

        for gi, d in enumerate(DILATIONS):
            @pl.when(g == gi)
            def _():
                branch(gi, d)

        @pl.when(g == len(DILATIONS) - 1)
        def _():
            def merge(c, carry):
                rows = pl.ds(pl.multiple_of(c * ATT_BLOCK, ATT_BLOCK), ATT_BLOCK)
                l0, l1, l2 = lb[0, rows, :], lb[1, rows, :], lb[2, rows, :]
                m = jnp.maximum(jnp.maximum(l0, l1), l2)
                e0, e1, e2 = jnp.exp(l0 - m), jnp.exp(l1 - m), jnp.exp(l2 - m)
                den = e0 + e1 + e2
                o_ref[rows, :] = (e0 * ob[0, rows, :] + e1 * ob[1, rows, :] + e2 * ob[2, rows, :]) / den
                lse_ref[rows, :] = m + jnp.log(den)
                return carry
            lax.fori_loop(0, REGION // ATT_BLOCK, merge, 0)

    def blk(col, prev=False):
        if prev:
            return pl.BlockSpec((REGION, HEAD_DIM), lambda n, h, g: (jnp.maximum(n - 1, 0), col(h, g)))
        return pl.BlockSpec((REGION, HEAD_DIM), lambda n, h, g: (n, col(h, g)))

    k_col = lambda h, g: h
    v_col = lambda h, g: ATT_HEADS + h
    outs, bufs = _call(
        body, name="attn_fwd", jobs=jobs,
        out_shape=[jax.ShapeDtypeStruct((s, ATT_HEADS * HEAD_DIM), F32)] * 2,
        grid=(nreg, ATT_HEADS, len(DILATIONS)),
        in_specs=[pl.BlockSpec((None,) + SLOPE_TILE, lambda n, h, g: (h, 0, 0)),
                  blk(lambda h, g: g * ATT_HEADS + h), blk(k_col), blk(k_col, True), blk(v_col), blk(v_col, True)],
        out_specs=[blk(k_col), blk(k_col)],
        scratch_shapes=[pltpu.VMEM((len(DILATIONS), REGION, HEAD_DIM), F32)] * 2,
        args=[_slope_table(), q, kv, kv, kv, kv], sem=("parallel", "parallel", "arbitrary"))
    return outs if jobs is None else (outs, bufs)


def attn_bwd(q, kv, o, lse, do, jobs=None):
    s = q.shape[0]
    nreg = s // REGION
    scale = HEAD_DIM ** -0.5

    def body(sl_ref, q_ref, kc_ref, kp_ref, vc_ref, vp_ref, o_ref, lse_ref, do_ref,
             qn_ref, on_ref, lsen_ref, don_ref, dq_ref, dkv_ref, dlt, dltn):
        n = pl.program_id(0)
        g = pl.program_id(2)
        slope = sl_ref[0:1, 0:1]
        first_key = jnp.where(n > 0, 0, ATT_BLOCK)
        has_next = n + 1 < nreg

        @pl.when(g == 0)
        def _():
            dkv_ref[...] = jnp.zeros_like(dkv_ref)

            def deltas(c, carry):
                rows = pl.ds(pl.multiple_of(c * ATT_BLOCK, ATT_BLOCK), ATT_BLOCK)
                dlt[rows, :] = jnp.sum(do_ref[rows, :] * o_ref[rows, :], axis=-1, keepdims=True)
                dltn[rows, :] = jnp.sum(don_ref[rows, :] * on_ref[rows, :], axis=-1, keepdims=True)
                return carry
            lax.fori_loop(0, REGION // ATT_BLOCK, deltas, 0)

        def branch(d):
            nsub = REGION // (ATT_BLOCK * d)
            bias = _att_bias(slope, d, 0)
            bias0 = _att_bias(slope, d, first_key)

            def residue(r):
                for i in range(nsub):
                    rows = _sub(r, i, d)
                    kcat = _keys(kc_ref, kp_ref, r, i, d, nsub)
                    vcat = _keys(vc_ref, vp_ref, r, i, d, nsub)
                    qb = q_ref[rows, :]
                    dob = do_ref[rows, :]
                    sc = _dot(qb, kcat, 1, 1) * scale + (bias0 if i == 0 else bias)
                    p = jnp.exp(sc - lse_ref[rows, :][:, 0:1])
                    ds = p * (_dot(dob, vcat, 1, 1) - dlt[rows, :])
                    dq_ref[rows, :] = _dot(ds, kcat, 1, 0) * scale
                    dk = _dot(ds, qb, 0, 0) * scale
                    dv = _dot(p, dob, 0, 0)
                    dkv_ref.at[0][rows, :] += dk[ATT_BLOCK:]
                    dkv_ref.at[1][rows, :] += dv[ATT_BLOCK:]
                    if i > 0:
                        prev = _sub(r, i - 1, d)
                        dkv_ref.at[0][prev, :] += dk[:ATT_BLOCK]
                        dkv_ref.at[1][prev, :] += dv[:ATT_BLOCK]

            _per_residue(d, residue)

            @pl.when(has_next)
            def _():
                bias_prev = bias[:, :ATT_BLOCK]

                def residue_next(r):
                    last = _sub(r, nsub - 1, d)
                    first = _sub(r, 0, d)
                    qb = qn_ref[first, :]
                    dob = don_ref[first, :]
                    sc = _dot(qb, kc_ref[last, :], 1, 1) * scale + bias_prev
                    p = jnp.exp(sc - lsen_ref[first, :][:, 0:1])
                    ds = p * (_dot(dob, vc_ref[last, :], 1, 1) - dltn[first, :])
                    dkv_ref.at[0][last, :] += _dot(ds, qb, 0, 0) * scale
                    dkv_ref.at[1][last, :] += _dot(p, dob, 0, 0)

                _per_residue(d, residue_next)

        for gi, d in enumerate(DILATIONS):
            @pl.when(g == gi)
            def _():
                branch(d)

    last_reg = nreg - 1

    def blk(col, shift=0):
        if shift < 0:
            return pl.BlockSpec((REGION, HEAD_DIM), lambda n, h, g: (jnp.maximum(n - 1, 0), col(h, g)))
        if shift > 0:
            return pl.BlockSpec((REGION, HEAD_DIM), lambda n, h, g: (jnp.minimum(n + 1, last_reg), col(h, g)))
        return pl.BlockSpec((REGION, HEAD_DIM), lambda n, h, g: (n, col(h, g)))

    q_col = lambda h, g: g * ATT_HEADS + h
    k_col = lambda h, g: h
    v_col = lambda h, g: ATT_HEADS + h
    outs, bufs = _call(
        body, name="attn_bwd", jobs=jobs,
        out_shape=[jax.ShapeDtypeStruct(q.shape, F32), jax.ShapeDtypeStruct((2, s, ATT_HEADS * HEAD_DIM), F32)],
        grid=(nreg, ATT_HEADS, len(DILATIONS)),
        in_specs=[pl.BlockSpec((None,) + SLOPE_TILE, lambda n, h, g: (h, 0, 0)),
                  blk(q_col), blk(k_col), blk(k_col, -1), blk(v_col), blk(v_col, -1),
                  blk(k_col), blk(k_col), blk(k_col),
                  blk(q_col, 1), blk(k_col, 1), blk(k_col, 1), blk(k_col, 1)],
        out_specs=[blk(q_col), pl.BlockSpec((2, REGION, HEAD_DIM), lambda n, h, g: (0, n, h))],
        scratch_shapes=[pltpu.VMEM((REGION, 1), F32)] * 2,
        args=[_slope_table(), q, kv, kv, kv, kv, o, lse, do, q, o, lse, do],
        sem=("parallel", "parallel", "arbitrary"))
    return outs if jobs is None else (outs, bufs)


def _adam(w, g, m, v):
    m = ADAM_B1 * m + (1.0 - ADAM_B1) * g
    v = ADAM_B2 * v + (1.0 - ADAM_B2) * (g * g)
    m_hat = m / (1.0 - ADAM_B1 ** ADAM_STEP)
    v_hat = v / (1.0 - ADAM_B2 ** ADAM_STEP)
    delta = -ADAM_LR * (m_hat / (jnp.sqrt(v_hat) + ADAM_EPS) + ADAM_WD * w)
    return delta, m, v


def adam_sharded(recvs, w, m, v, *, name):
    layers = len(recvs)
    n_src, r, c = recvs[0].shape
    tr = _rows(r, c)

    def body(*refs):
        p_refs = refs[:layers]
        w_ref, m_ref, v_ref, g_ref, d_ref, mo_ref, vo_ref = refs[layers:]
        for layer, p_ref in enumerate(p_refs):
            @pl.when(pl.program_id(0) == layer)
            def _():
                g = p_ref[0].astype(F32)
                for src in range(1, n_src):
                    g = g + p_ref[src].astype(F32)
                delta, m_new, v_new = _adam(w_ref[...], g, m_ref[...], v_ref[...])
                g_ref[...] = g
                d_ref[...] = delta
                mo_ref[...] = m_new
                vo_ref[...] = v_new

    blk = pl.BlockSpec((None, tr, c), lambda l, i: (l, i, 0))
    out = jax.ShapeDtypeStruct((layers, r, c), F32)
    part = [pl.BlockSpec((n_src, tr, c), functools.partial(lambda l, i, layer: (0, jnp.where(l == layer, i, 0), 0),
                                                            layer=layer)) for layer in range(layers)]
    return pl.pallas_call(
        body,
        name=name,
        out_shape=[out] * 4,
        grid=(layers, r // tr),
        in_specs=part + [blk, blk, blk],
        out_specs=[blk] * 4,
        compiler_params=_params("parallel", "parallel"),
    )(*recvs, w, m, v)


def sum_partials(parts):
    n_src, r, c = parts.shape

    def body(p_ref, o_ref):
        g = p_ref[0]
        for src in range(1, n_src):
            g = g + p_ref[src]
        o_ref[...] = g

    return pl.pallas_call(
        body,
        name="sum_small_grads",
        out_shape=jax.ShapeDtypeStruct((r, c), F32),
    )(parts)


def adam_packed(w, g, m, v):
    def body(w_ref, g_ref, m_ref, v_ref, d_ref, mo_ref, vo_ref):
        delta, m_new, v_new = _adam(w_ref[...], g_ref[...], m_ref[...], v_ref[...])
        d_ref[...] = delta
        mo_ref[...] = m_new
        vo_ref[...] = v_new

    out = jax.ShapeDtypeStruct(w.shape, F32)
    return pl.pallas_call(body, name="adam_small", out_shape=[out] * 3)(w, g, m, v)


def all_gather(srcs, *, name):
    n = len(srcs)

    def body(*refs):
        src, dst = refs[:n], refs[n:2 * n]
        send_sems, recv_sems, local_sems = refs[2 * n:]
        x, y, c, me = _place()
        sibling = (x, y, 1 - c)
        chips = [(1 - x, y), (x, 1 - y), (1 - x, 1 - y)]

        def index(px, py, pc):
            return 4 * px + 2 * py + pc

        def copy(p, k, block, to, from_src=False):
            slot = dst[p].at[index(*block)]
            return pltpu.make_async_remote_copy(
                src_ref=src[p] if from_src else slot, dst_ref=slot,
                send_sem=send_sems.at[p, k], recv_sem=recv_sems.at[p, k],
                device_id=to, device_id_type=MESH)

        mine = [pltpu.make_async_copy(src[p], dst[p].at[me], local_sems.at[p]) for p in range(n)]
        for cp in mine:
            cp.start()
        first = []
        for p in range(n):
            first.append(copy(p, 0, (x, y, c), sibling, from_src=True))
            for jj, chip in enumerate(chips):
                first.append(copy(p, 1 + jj, (x, y, c), (*chip, c), from_src=True))
        for cp in first:
            cp.start()
        passed = []
        for jj, chip in enumerate(chips):
            for p in range(n):
                copy(p, 1 + jj, (*chip, c), (x, y, c)).wait_recv()
                fwd = copy(p, 4 + jj, (*chip, c), sibling)
                fwd.start()
                passed.append(fwd)
        for p in range(n):
            copy(p, 0, sibling, (x, y, c)).wait_recv()
            for jj, chip in enumerate(chips):
                copy(p, 4 + jj, (*chip, 1 - c), (x, y, c)).wait_recv()
        for cp in first + passed:
            cp.wait_send()
        for cp in mine:
            cp.wait()

    return pl.pallas_call(
        body,
        name=name,
        out_shape=[jax.ShapeDtypeStruct((N_DEV,) + a.shape, a.dtype) for a in srcs],
        in_specs=[ANY] * n,
        out_specs=[ANY] * n,
        scratch_shapes=[pltpu.SemaphoreType.DMA((n, 7)), pltpu.SemaphoreType.DMA((n, 7)),
                        pltpu.SemaphoreType.DMA((n,))],
    )(*srcs)


def exchange_only(*, name, jobs):
    def body(o_ref):
        o_ref[...] = jnp.zeros_like(o_ref)

    _, bufs = _call(body, name=name, jobs=jobs, out_shape=[jax.ShapeDtypeStruct((8, LANE), F32)], grid=(1,),
                    in_specs=[], out_specs=[pl.BlockSpec((8, LANE), lambda i: (0, 0))], args=[], sem=("arbitrary",))
    return None, bufs


def _pack_rows(parts, rows):
    flat = jnp.concatenate([p.reshape(-1) for p in parts])
    return jnp.pad(flat, (0, rows * LANE - flat.shape[0])).reshape(rows, LANE)


def _unpack_rows(packed, shapes):
    flat = packed.reshape(-1)
    out, at = [], 0
    for sh in shapes:
        size = 1
        for dim in sh:
            size *= dim
        out.append(flat[at:at + size].reshape(sh))
        at += size
    return out


CONV_W_PAD = 768
SMALL_W_ROWS = 56


def _pack_small_weights(w_a2, b_a2, hn, conv_w):
    cw = jnp.pad(conv_w.reshape(6, -1), ((0, 0), (0, CONV_W_PAD - conv_w.shape[-1]))).reshape(-1, LANE)
    rows = jnp.concatenate([w_a2[0], b_a2, jnp.pad(hn, ((0, 0), (0, LANE - hn.shape[-1]))), cw], axis=0)
    return jnp.pad(rows, ((0, SMALL_W_ROWS - rows.shape[0]), (0, 0)))


def _unpack_small_weights(gathered):
    w_a2 = gathered[:, 0:GATE_RANK, :].transpose(1, 0, 2).reshape(GATE_RANK, GLA_KEY_DIM)
    b_a2 = gathered[:, GATE_RANK, :].reshape(1, GLA_KEY_DIM)
    hn = gathered[:, GATE_RANK + 1, :GLA_DV // N_DEV].reshape(1, GLA_DV)
    per = D_FF // N_DEV
    cw = gathered[:, GATE_RANK + 2:GATE_RANK + 2 + 6 * CONV_W_PAD // LANE, :].reshape(N_DEV, 6, CONV_W_PAD)[:, :, :per]
    cw = cw.reshape(N_DEV, 2, 3, per).transpose(1, 2, 0, 3).reshape(2, 3, D_FF)
    return w_a2, b_a2, hn, cw


SCHEDULE = {
    "gla_in": [("g1", "gout", None), ("g1", "up0", (0, 1024))],
    "gla_fwd": [("g2", "gout", None), ("g2", "up0", (0, 1024)), ("g1", "up0", (1024, 2048)), ("g1", "dn0", (0, 352))],
    "gla_out": [("g2", "up0", (1024, 2048)), ("g2", "dn0", (0, 352)), ("g1", "dn0", (352, 704))],
    "ffn_up0": [("g2", "dn0", (352, 704)), ("g1", "kv", None), ("g1", "q", None)],
    "ffn_down0": [("g2", "kv", None), ("g2", "q", None), ("g1", "dout", None), ("g1", "up1", (0, 704))],
    "kv_proj": [("g2", "dout", None), ("g2", "up1", (0, 704)), ("g1", "up1", (704, 1408))],
    "q_proj": [("g2", "up1", (704, 1408)), ("g1", "up1", (1408, 2048))],
    "attn_fwd": [("g2", "up1", (1408, 2048)), ("g1", "dn1", None)],
    "dsa_out": [("g2", "dn1", None)],
    "ffn_down_dx1": [("sc", "dn1", (0, 352))],
    "convglu_bwd1": [("sc", "dn1", (352, 704))],
    "ffn_up_dx1": [("sc", "up1", (0, 1024))],
    "attn_bwd": [("sc", "up1", (1024, 2048)), ("sc", "dout", None)],
    "q_proj_dx": [("sc", "q", (0, 1408))],
    "kv_proj_dx": [("sc", "q", (1408, 2048)), ("sc", "kv", (0, 1024))],
    "ffn_down_dx0": [("sc", "kv", (1024, 2048)), ("sc", "dn0", (0, 176))],
    "convglu_bwd0": [("sc", "dn0", (176, 528))],
    "ffn_up_dx0": [("sc", "dn0", (528, 704)), ("sc", "up0", (0, 896))],
    "gla_bwd": [("sc", "up0", (896, 1792)), ("sc", "gout", None)],
    "gla_in_dx": [("sc", "up0", (1792, 2048)), ("sc", "in", (0, 1024))],
    "grads_tail": [("sc", "in", (1024, 2048))],
}
ROW_SHARDED = ("gout", "dout", "dn0", "dn1")


class Plan:
    def __init__(self, weights, srcs=None):
        self.w = dict(weights)
        self.srcs = srcs
        self.grads = {}
        self.recv = {}
        self._names = None

    def weight(self, name):
        buf = self.w[name]
        if name in ROW_SHARDED:
            return buf.reshape(1, buf.shape[0] * buf.shape[1], buf.shape[2])
        return buf

    def jobs(self, call):
        ops = SCHEDULE.get(call)
        if self.srcs is None or not ops:
            return None
        jobs, handles = Jobs(), {}
        for op, name, rows in ops:
            store = self.recv if op == "sc" else self.w
            if name not in handles:
                if name in store:
                    handles[name] = jobs.thru(store[name])
                elif op == "sc":
                    handles[name] = jobs.new(self.grads[name].shape, BF16)
                else:
                    handles[name] = jobs.new((N_DEV,) + self.srcs[name].shape, BF16)
            if op == "g1":
                jobs.gather_ici(self.srcs[name], handles[name], rows)
            elif op == "g2":
                jobs.gather_d2d(handles[name], rows)
            else:
                jobs.scatter(self.grads[name], handles[name], rows)
        self._names = [(name, self.recv if ops[0][0] == "sc" else self.w) for name in handles]
        assert len({op == "sc" for op, _, _ in ops}) == 1
        return jobs

    def run(self, call, fn, *args, **kwargs):
        jobs = self.jobs(call)
        if jobs is None:
            return fn(*args, **kwargs)
        out, bufs = fn(*args, jobs=jobs, **kwargs)
        for (name, store), buf in zip(self._names, bufs):
            store[name] = buf
        return out


def _ffn_fwd(plan, h, norm_g, conv_w, conv_b, tag):
    (n,) = rms_fwd(h, [norm_g], name=f"ffn_norm_fwd{tag}")
    up = plan.run(f"ffn_up{tag}", mm_nn, n, plan.weight(f"up{tag}"), out_dtype=BF16, name=f"ffn_up{tag}")
    act = convglu_fwd(up, conv_w, conv_b)
    h_out = plan.run(f"ffn_down{tag}", mm_nn, act, plan.weight(f"dn{tag}"), out_dtype=F32, res=h,
                     name=f"ffn_down{tag}")
    return h_out, (n, up, act)


def _by_rows(dw):
    return dw.reshape(N_DEV, dw.shape[1] // N_DEV, dw.shape[2])


def _ffn_bwd(plan, dh_out, h, saved, norm_g, conv_w, conv_b, tag):
    n, up, act = saved
    plan.grads[f"dn{tag}"] = _by_rows(mm_tn(act, dh_out, 1, name=f"ffn_down_dw{tag}"))
    dact = plan.run(f"ffn_down_dx{tag}", mm_nt, dh_out, plan.weight(f"dn{tag}"), out_dtype=BF16,
                    name=f"ffn_down_dx{tag}")
    dup, dconv_w, dconv_b = plan.run(f"convglu_bwd{tag}", convglu_bwd, up, conv_w, conv_b, dact,
                                     name=f"convglu_bwd{tag}")
    plan.grads[f"up{tag}"] = mm_tn(n, dup, N_DEV, name=f"ffn_up_dw{tag}")
    dn = plan.run(f"ffn_up_dx{tag}", mm_nt, dup, plan.weight(f"up{tag}"), out_dtype=F32, name=f"ffn_up_dx{tag}")
    dh, (dnorm,) = rms_bwd(h, [norm_g], [dn], dh_out, name=f"ffn_norm_bwd{tag}")
    return dh, dnorm, dconv_w, dconv_b


def local_step(x, target, wts, plan):
    row = lambda v: v.reshape(1, -1)
    attn_norm, ffn_norm = wts["attn_norm"], wts["ffn_norm"]
    conv_w, conv_b = wts["ffn_conv_w"], wts["ffn_conv_b"]

    (n1,) = rms_fwd(x, [row(attn_norm[0])], name="attn_norm_fwd0")
    proj = plan.run("gla_in", mm_nn, n1, wts["gla_w_in"], out_dtype=F32, name="gla_in")
    la = gate_fwd(proj, wts["gla_w_a2"], wts["gla_b_a2"])
    o_gla, states = plan.run("gla_fwd", gla_fwd, proj, la)
    og = headnorm_fwd(o_gla, proj, wts["gla_head_norm"])
    h1 = plan.run("gla_out", mm_nn, og, plan.weight("gout"), out_dtype=F32, res=x, name="gla_out")
    h2, ffn0 = _ffn_fwd(plan, h1, row(ffn_norm[0]), conv_w[0], row(conv_b[0]), "0")

    kvn, n3 = rms_fwd(h2, [row(wts["kv_norm"]), row(attn_norm[1])], name="kv_attn_norm_fwd")
    kv = plan.run("kv_proj", mm_nn, kvn, plan.weight("kv"), out_dtype=F32, name="kv_proj")
    q = plan.run("q_proj", mm_nn, n3, plan.weight("q"), out_dtype=F32, name="q_proj")
    o_att, lse = plan.run("attn_fwd", attn_fwd, q, kv)
    h3 = plan.run("dsa_out", mm_nn, o_att, plan.weight("dout"), out_dtype=F32, res=h2, name="dsa_out")
    h4, ffn1 = _ffn_fwd(plan, h3, row(ffn_norm[1]), conv_w[1], row(conv_b[1]), "1")

    loss_tile, dh4, d_final = loss_head(h4, row(wts["final_norm"]), target)

    dh3, d_ffn1, dcw1, dcb1 = _ffn_bwd(plan, dh4, h3, ffn1, row(ffn_norm[1]), conv_w[1], row(conv_b[1]), "1")
    plan.grads["dout"] = _by_rows(mm_tn(o_att, dh3, 1, name="dsa_out_dw"))
    do_att = mm_nt(dh3, plan.weight("dout"), out_dtype=F32, name="dsa_out_dx")
    dq, dkv = plan.run("attn_bwd", attn_bwd, q, kv, o_att, lse, do_att)
    plan.grads["q"] = mm_tn(n3, dq, N_DEV, name="q_proj_dw")
    dn3 = plan.run("q_proj_dx", mm_nt, dq, plan.weight("q"), out_dtype=F32, name="q_proj_dx")
    plan.grads["kv"] = mm_tn(kvn, dkv, N_DEV, name="kv_proj_dw")
    dkvn = plan.run("kv_proj_dx", mm_nt, dkv, plan.weight("kv"), out_dtype=F32, name="kv_proj_dx")
    dh2, (d_kvnorm, d_attn1) = rms_bwd(h2, [row(wts["kv_norm"]), row(attn_norm[1])], [dkvn, dn3], dh3,
                                       name="kv_attn_norm_bwd")
    dh1, d_ffn0, dcw0, dcb0 = _ffn_bwd(plan, dh2, h1, ffn0, row(ffn_norm[0]), conv_w[0], row(conv_b[0]), "0")
    plan.grads["gout"] = _by_rows(mm_tn(og, dh1, 1, name="gla_out_dw"))
    dog = mm_nt(dh1, plan.weight("gout"), out_dtype=F32, name="gla_out_dx")
    do_gla, dr, d_hn = headnorm_bwd(o_gla, proj, wts["gla_head_norm"], dog)
    dq_g, dk_g, dv_g, dla = plan.run("gla_bwd", gla_bwd, proj, la, states, do_gla)
    da, dw_a2p, db_a2 = gate_bwd(proj, wts["gla_w_a2"], wts["gla_b_a2"], dla)
    dproj = jnp.concatenate([dq_g, dk_g, dv_g, dr, da], axis=1)
    assert dproj.shape[1] == GLA_IN_PAD
    dw_in = mm_tn(n1, dproj, 1, name="gla_in_dw")
    plan.grads["in"] = dw_in[0, :, :GLA_IN_DIM].reshape(D_MODEL, N_DEV, GLA_IN_DIM // N_DEV).transpose(1, 0, 2)
    dn1 = plan.run("gla_in_dx", mm_nt, dproj, wts["gla_w_in"], out_dtype=F32, name="gla_in_dx")
    grad_x, (d_attn0,) = rms_bwd(x, [row(attn_norm[0])], [dn1], dh1, name="attn_norm_bwd0")

    small = dict(
        attn_norm=jnp.concatenate([d_attn0, d_attn1], axis=0),
        ffn_norm=jnp.concatenate([d_ffn0, d_ffn1], axis=0),
        kv_norm=d_kvnorm.reshape(-1),
        final_norm=d_final.reshape(-1),
        ffn_conv_b=jnp.concatenate([dcb0, dcb1], axis=0),
        gla_w_a2=dw_a2p[:GATE_RANK],
        gla_b_a2=db_a2,
        gla_head_norm=d_hn,
        ffn_conv_w=jnp.stack([dcw0, dcw1]),
    )
    return loss_tile, grad_x, small


SMALL_ORDER = ("attn_norm", "ffn_norm", "kv_norm", "final_norm", "ffn_conv_b",
               "gla_w_a2", "gla_b_a2", "gla_head_norm", "ffn_conv_w")
SMALL_FULL = dict(attn_norm=(2, D_MODEL), ffn_norm=(2, D_MODEL), kv_norm=(D_MODEL,), final_norm=(D_MODEL,),
                  ffn_conv_b=(2, D_FF), gla_w_a2=(GATE_RANK, GLA_KEY_DIM), gla_b_a2=(1, GLA_KEY_DIM),
                  gla_head_norm=(1, GLA_DV), ffn_conv_w=(2, 3, D_FF))
SMALL_SHARDED = ("gla_w_a2", "gla_b_a2", "gla_head_norm", "ffn_conv_w")
SMALL_GRAD_ROWS = 592
SMALL_ADAM_ROWS = 240


def kernel(x, attn_norm, gla_w_in, gla_w_a2, gla_b_a2, gla_head_norm, gla_w_out, kv_norm, w_kv, dsa_w_q, dsa_w_out, ffn_norm, ffn_w_up, ffn_conv_w, ffn_conv_b, ffn_w_down, final_norm, loss_target, m_attn_norm, m_gla_w_in, m_gla_w_a2, m_gla_b_a2, m_gla_head_norm, m_gla_w_out, m_kv_norm, m_w_kv, m_dsa_w_q, m_dsa_w_out, m_ffn_norm, m_ffn_w_up, m_ffn_conv_w, m_ffn_conv_b, m_ffn_w_down, m_final_norm, v_attn_norm, v_gla_w_in, v_gla_w_a2, v_gla_b_a2, v_gla_head_norm, v_gla_w_out, v_kv_norm, v_w_kv, v_dsa_w_q, v_dsa_w_out, v_ffn_norm, v_ffn_w_up, v_ffn_conv_w, v_ffn_conv_b, v_ffn_w_down, v_final_norm):
    me = 4 * lax.axis_index("x") + 2 * lax.axis_index("y") + lax.axis_index("c")
    bf = lambda a: a.astype(BF16)

    g_in, g_small = all_gather([bf(gla_w_in[0]), _pack_small_weights(gla_w_a2, gla_b_a2, gla_head_norm, ffn_conv_w)],
                               name="gather_first")
    w_a2_full, b_a2_full, hn_full, conv_w_full = _unpack_small_weights(g_small)
    w_in_full = jnp.pad(g_in.transpose(1, 0, 2).reshape(D_MODEL, GLA_IN_DIM), ((0, 0), (0, GLA_IN_PAD - GLA_IN_DIM)))
    wts = dict(
        attn_norm=attn_norm, ffn_norm=ffn_norm, kv_norm=kv_norm, final_norm=final_norm, ffn_conv_b=ffn_conv_b,
        gla_w_in=w_in_full[None],
        gla_w_a2=jnp.pad(bf(w_a2_full), ((0, LANE - GATE_RANK), (0, 0))),
        gla_b_a2=b_a2_full, gla_head_norm=hn_full, ffn_conv_w=conv_w_full,
    )
    plan = Plan({}, srcs=dict(gout=bf(gla_w_out[0]), kv=bf(w_kv), q=bf(dsa_w_q[0]), dout=bf(dsa_w_out[0]),
                              up0=bf(ffn_w_up[0]), up1=bf(ffn_w_up[1]), dn0=bf(ffn_w_down[0]), dn1=bf(ffn_w_down[1])))

    loss_tile, grad_x, small = local_step(x[0], loss_target[0], wts, plan)
    loss = lax.psum(loss_tile[0, 0], ("x", "y", "c"))

    plan.run("grads_tail", exchange_only, name="grads_tail")
    shard3 = lambda a: a.reshape((-1,) + a.shape[-2:])
    big_params = dict(gla_w_in=(("in",), gla_w_in, m_gla_w_in, v_gla_w_in),
                      gla_w_out=(("gout",), gla_w_out, m_gla_w_out, v_gla_w_out),
                      w_kv=(("kv",), w_kv, m_w_kv, v_w_kv),
                      dsa_w_q=(("q",), dsa_w_q, m_dsa_w_q, v_dsa_w_q),
                      dsa_w_out=(("dout",), dsa_w_out, m_dsa_w_out, v_dsa_w_out),
                      ffn_w_up=(("up0", "up1"), ffn_w_up, m_ffn_w_up, v_ffn_w_up),
                      ffn_w_down=(("dn0", "dn1"), ffn_w_down, m_ffn_w_down, v_ffn_w_down))
    res = {}
    for nm, (parts, w, m, v) in big_params.items():
        outs = adam_sharded([plan.recv[p] for p in parts], shard3(w), shard3(m), shard3(v), name=f"adam_{nm}")
        res[nm] = [o.reshape(w.shape) for o in outs]

    packed = _pack_rows([small[nm] for nm in SMALL_ORDER], SMALL_GRAD_ROWS)
    (parts,) = all_gather([packed], name="gather_small_grads")
    full = dict(zip(SMALL_ORDER, _unpack_rows(sum_partials(parts), [SMALL_FULL[nm] for nm in SMALL_ORDER])))
    local_w = dict(attn_norm=attn_norm, ffn_norm=ffn_norm, kv_norm=kv_norm, final_norm=final_norm,
                   ffn_conv_b=ffn_conv_b, gla_w_a2=gla_w_a2, gla_b_a2=gla_b_a2, gla_head_norm=gla_head_norm,
                   ffn_conv_w=ffn_conv_w)
    local_m = dict(attn_norm=m_attn_norm, ffn_norm=m_ffn_norm, kv_norm=m_kv_norm, final_norm=m_final_norm,
                   ffn_conv_b=m_ffn_conv_b, gla_w_a2=m_gla_w_a2, gla_b_a2=m_gla_b_a2, gla_head_norm=m_gla_head_norm,
                   ffn_conv_w=m_ffn_conv_w)
    local_v = dict(attn_norm=v_attn_norm, ffn_norm=v_ffn_norm, kv_norm=v_kv_norm, final_norm=v_final_norm,
                   ffn_conv_b=v_ffn_conv_b, gla_w_a2=v_gla_w_a2, gla_b_a2=v_gla_b_a2, gla_head_norm=v_gla_head_norm,
                   ffn_conv_w=v_ffn_conv_w)
    local_g = {}
    for nm in SMALL_ORDER:
        gfull = full[nm]
        if nm in SMALL_SHARDED:
            per = gfull.shape[-1] // N_DEV
            gfull = lax.dynamic_slice_in_dim(gfull, me * per, per, axis=gfull.ndim - 1)
        local_g[nm] = gfull.reshape(local_w[nm].shape)
    shapes = [local_w[nm].shape for nm in SMALL_ORDER]
    pk = lambda dd: _pack_rows([dd[nm] for nm in SMALL_ORDER], SMALL_ADAM_ROWS)
    d_p, m_p, v_p = adam_packed(pk(local_w), pk(local_g), pk(local_m), pk(local_v))
    for nm, dl, mn, vn in zip(SMALL_ORDER, _unpack_rows(d_p, shapes), _unpack_rows(m_p, shapes),
                              _unpack_rows(v_p, shapes)):
        res[nm] = [local_g[nm], dl, mn, vn]

    order = ("attn_norm", "gla_w_in", "gla_w_a2", "gla_b_a2", "gla_head_norm", "gla_w_out", "kv_norm", "w_kv",
             "dsa_w_q", "dsa_w_out", "ffn_norm", "ffn_w_up", "ffn_conv_w", "ffn_conv_b", "ffn_w_down", "final_norm")
    outs = [loss, grad_x[None]]
    for kind in range(4):
        outs.extend(res[nm][kind] for nm in order)
    return tuple(outs)
```

```python
import functools

import jax
import jax.numpy as jnp
from jax import lax
from jax.experimental import pallas as pl
from jax.experimental.pallas import tpu as pltpu

F32 = jnp.float32
BF16 = jnp.bfloat16
MESH = pl.DeviceIdType.MESH
ANY = pl.BlockSpec(memory_space=pl.ANY)

N_DEV = 8
D_MODEL = 2048
GLA_HEADS = 4
GLA_KEY_DIM = 1024
GLA_VAL_DIM = 2048
GLA_DK = 256
GLA_DV = 512
GATE_RANK = 16
GATE_NORMALIZER = 16.0
GLA_CHUNK = 64
GLA_IN_DIM = 2 * GLA_KEY_DIM + 2 * GLA_VAL_DIM + GATE_RANK
GLA_IN_PAD = 6272
ATT_HEADS = 16
HEAD_DIM = 128
DILATIONS = (1, 4, 16)
ATT_BLOCK = 128
D_FF = 5632
EPS = 1e-6
ADAM_LR = 0.001
ADAM_B1 = 0.9
ADAM_B2 = 0.999
ADAM_EPS = 1e-08
ADAM_WD = 0.01
ADAM_STEP = 10
NEG = -1e30
LANE = 128
VMEM_LIMIT = 52 * 1024 * 1024
ALIBI_SLOPES = tuple(2.0 ** (-0.5 * (i + 1)) for i in range(ATT_HEADS))


def _params(*sem):
    return pltpu.CompilerParams(dimension_semantics=sem, vmem_limit_bytes=VMEM_LIMIT)


def _tile(n, cap):
    best = None
    for t in range(LANE, min(n, cap) + 1, LANE):
        if n % t == 0:
            best = t
    return best if best is not None else n


def _shard_group(j, ns, cap):
    best = 1
    for g in range(1, j + 1):
        if j % g == 0 and g * ns <= cap:
            best = g
    return best


def _rows(r, c, budget=256 * 1024):
    best = None
    for t in range(16, r + 1, 16):
        if r % t == 0 and t * c <= budget:
            best = t
    return best if best is not None else r


def _flip(coord, bit):
    return 1 - coord if bit else coord


def _place():
    x, y, c = lax.axis_index("x"), lax.axis_index("y"), lax.axis_index("c")
    return x, y, c, 4 * x + 2 * y + c


def _rows_of(ref, rows):
    return ref if rows is None else ref.at[pl.ds(rows[0], rows[1] - rows[0])]


class Jobs:
    def __init__(self):
        self.srcs = []
        self.bufs = []
        self.sems = []
        self.steps = []

    def _src(self, a):
        for i, b in enumerate(self.srcs):
            if b is a:
                return i
        self.srcs.append(a)
        return len(self.srcs) - 1

    def new(self, shape, dtype):
        self.bufs.append((None, jax.ShapeDtypeStruct(shape, dtype)))
        return len(self.bufs) - 1

    def thru(self, a):
        self.bufs.append((a, jax.ShapeDtypeStruct(a.shape, a.dtype)))
        return len(self.bufs) - 1

    def _sem(self, n):
        self.sems.append(pltpu.SemaphoreType.DMA((n,)))
        return len(self.sems) - 1

    def gather_ici(self, src, buf, rows=None):
        si, send, recv, loc = self._src(src), self._sem(4), self._sem(4), self._sem(1)

        def remote(srcs, bufs, sems, slot_of):
            x, y, c, me = _place()
            peers = [(x, y, 1 - c), (1 - x, y, c), (x, 1 - y, c), (1 - x, 1 - y, c)]
            return [pltpu.make_async_remote_copy(
                src_ref=_rows_of(srcs[si], rows),
                dst_ref=_rows_of(bufs[buf].at[me if slot_of == "mine" else 4 * p[0] + 2 * p[1] + p[2]], rows),
                send_sem=sems[send].at[k], recv_sem=sems[recv].at[k], device_id=p, device_id_type=MESH)
                for k, p in enumerate(peers)]

        def local(srcs, bufs, sems):
            return pltpu.make_async_copy(_rows_of(srcs[si], rows), _rows_of(bufs[buf].at[_place()[3]], rows),
                                         sems[loc].at[0])

        def start(srcs, bufs, sems):
            local(srcs, bufs, sems).start()
            for cp in remote(srcs, bufs, sems, "mine"):
                cp.start()

        def finish(srcs, bufs, sems):
            for cp in remote(srcs, bufs, sems, "peer"):
                cp.wait_recv()
            for cp in remote(srcs, bufs, sems, "mine"):
                cp.wait_send()
            local(srcs, bufs, sems).wait()

        self.steps.append((start, finish))

    def gather_d2d(self, buf, rows=None):
        send, recv = self._sem(3), self._sem(3)

        def copies(bufs, sems, core):
            x, y, c, _ = _place()
            cc = c if core == "mine" else 1 - c
            chips = [(1 - x, y), (x, 1 - y), (1 - x, 1 - y)]
            return [pltpu.make_async_remote_copy(
                src_ref=_rows_of(bufs[buf].at[4 * px + 2 * py + cc], rows),
                dst_ref=_rows_of(bufs[buf].at[4 * px + 2 * py + cc], rows),
                send_sem=sems[send].at[k], recv_sem=sems[recv].at[k],
                device_id=(x, y, 1 - c), device_id_type=MESH) for k, (px, py) in enumerate(chips)]

        def start(srcs, bufs, sems):
            for cp in copies(bufs, sems, "mine"):
                cp.start()

        def finish(srcs, bufs, sems):
            for cp in copies(bufs, sems, "sibling"):
                cp.wait_recv()
            for cp in copies(bufs, sems, "mine"):
                cp.wait_send()

        self.steps.append((start, finish))

    def scatter(self, src, buf, rows=None):
        si, send, recv, loc = self._src(src), self._sem(N_DEV - 1), self._sem(N_DEV - 1), self._sem(1)

        def remote(srcs, bufs, sems, slot_of):
            x, y, c, me = _place()
            out = []
            for k in range(1, N_DEV):
                px, py, pc = _flip(x, k >> 2), _flip(y, (k >> 1) & 1), _flip(c, k & 1)
                peer = 4 * px + 2 * py + pc
                out.append(pltpu.make_async_remote_copy(
                    src_ref=_rows_of(srcs[si].at[peer], rows),
                    dst_ref=_rows_of(bufs[buf].at[me if slot_of == "mine" else peer], rows),
                    send_sem=sems[send].at[k - 1], recv_sem=sems[recv].at[k - 1],
                    device_id=(px, py, pc), device_id_type=MESH))
            return out

        def local(srcs, bufs, sems):
            me = _place()[3]
            return pltpu.make_async_copy(_rows_of(srcs[si].at[me], rows), _rows_of(bufs[buf].at[me], rows),
                                         sems[loc].at[0])

        def start(srcs, bufs, sems):
            local(srcs, bufs, sems).start()
            for cp in remote(srcs, bufs, sems, "mine"):
                cp.start()

        def finish(srcs, bufs, sems):
            for cp in remote(srcs, bufs, sems, "peer"):
                cp.wait_recv()
            for cp in remote(srcs, bufs, sems, "mine"):
                cp.wait_send()
            local(srcs, bufs, sems).wait()

        self.steps.append((start, finish))


def _call(body, *, name, grid, in_specs, out_specs, out_shape, args, sem, scratch_shapes=(), jobs=None):
    in_specs, out_specs, out_shape = list(in_specs), list(out_specs), list(out_shape)
    scratch_shapes = list(scratch_shapes)
    if jobs is None:
        res = pl.pallas_call(body, name=name, out_shape=out_shape, grid=grid, in_specs=in_specs,
                             out_specs=out_specs, scratch_shapes=scratch_shapes,
                             compiler_params=_params(*sem))(*args)
        return list(res), []
    thru = [a for a, _ in jobs.bufs if a is not None]
    n_in, n_src, n_thru = len(args), len(jobs.srcs), len(thru)
    n_out, n_buf, n_scr = len(out_shape), len(jobs.bufs), len(scratch_shapes)
    aliases, t = {}, 0
    for b, (a, _) in enumerate(jobs.bufs):
        if a is not None:
            aliases[n_in + n_src + t] = n_out + b
            t += 1

    def wrapped(*refs):
        at = 0
        ins = refs[at:at + n_in]; at += n_in
        srcs = refs[at:at + n_src]; at += n_src + n_thru
        outs = refs[at:at + n_out]; at += n_out
        bufs = refs[at:at + n_buf]; at += n_buf
        scr = refs[at:at + n_scr]; at += n_scr
        sems = refs[at:]
        first, last = None, None
        for axis, size in enumerate(grid):
            pid = pl.program_id(axis)
            f, l = pid == 0, pid == size - 1
            first = f if first is None else first & f
            last = l if last is None else last & l

        @pl.when(first)
        def _():
            for start, _ in jobs.steps:
                start(srcs, bufs, sems)

        body(*ins, *outs, *scr)

        @pl.when(last)
        def _():
            for _, finish in jobs.steps:
                finish(srcs, bufs, sems)

    res = pl.pallas_call(
        wrapped, name=name,
        out_shape=out_shape + [s for _, s in jobs.bufs],
        grid=grid,
        in_specs=in_specs + [ANY] * (n_src + n_thru),
        out_specs=out_specs + [ANY] * n_buf,
        scratch_shapes=scratch_shapes + jobs.sems,
        input_output_aliases=aliases,
        compiler_params=_params(*(["arbitrary"] * len(grid))),
    )(*args, *jobs.srcs, *thru)
    return res[:n_out], res[n_out:]


def mm_nn(a, w, *, out_dtype, name, res=None, tm=None, jobs=None):
    m, k = a.shape
    tm = tm or (1024 if a.dtype == BF16 else 512)
    j, k2, ns = w.shape
    assert k == k2 and m % tm == 0
    tn = _tile(ns, 1408)
    nsub = ns // tn
    tk = k if k <= 2048 else _tile(k, 1408)
    nk = k // tk
    has_res = res is not None

    def body(*refs):
        if has_res:
            a_ref, w_ref, r_ref, o_ref, acc = refs
        else:
            a_ref, w_ref, o_ref, acc = refs
        kk = pl.program_id(2)

        @pl.when(kk == 0)
        def _():
            acc[...] = jnp.zeros_like(acc)

        acc[...] += jnp.dot(a_ref[...].astype(BF16), w_ref[...], preferred_element_type=F32)

        @pl.when(kk == nk - 1)
        def _():
            r = acc[...]
            if has_res:
                r = r + r_ref[...]
            o_ref[...] = r.astype(out_dtype)

    in_specs = [
        pl.BlockSpec((tm, tk), lambda i, n, kk: (i, kk)),
        pl.BlockSpec((None, tk, tn), lambda i, n, kk: (n // nsub, kk, n % nsub)),
    ]
    args = [a, w]
    if has_res:
        in_specs.append(pl.BlockSpec((tm, tn), lambda i, n, kk: (i, n)))
        args.append(res)
    (out,), bufs = _call(
        body, name=name, jobs=jobs,
        out_shape=[jax.ShapeDtypeStruct((m, j * ns), out_dtype)],
        grid=(m // tm, j * nsub, nk),
        in_specs=in_specs,
        out_specs=[pl.BlockSpec((tm, tn), lambda i, n, kk: (i, n))],
        scratch_shapes=[pltpu.VMEM((tm, tn), F32)],
        args=args, sem=("parallel", "parallel", "arbitrary"))
    return out if jobs is None else (out, bufs)


def mm_nt(dy, w, *, out_dtype, name, tm=1024, jobs=None):
    parts, m, n = (1,) + dy.shape if dy.ndim == 2 else dy.shape
    n *= parts
    j, k, ns = w.shape
    assert n == j * ns and m % tm == 0
    tn = _tile(ns, 2048)
    nsub = ns // tn
    jb = _shard_group(j // parts, ns, 2048) if nsub == 1 else 1
    tko = _tile(k, 1408)
    nn = j * nsub // jb
    per_part = nn // parts
    if dy.ndim == 2:
        dy_spec = pl.BlockSpec((tm, jb * tn), lambda i, ko, nq: (i, nq))
    else:
        dy_spec = pl.BlockSpec((None, tm, jb * tn), lambda i, ko, nq: (nq // per_part, i, nq % per_part))
    if jb == 1:
        w_spec = pl.BlockSpec((None, tko, tn), lambda i, ko, nq: (nq // nsub, ko, nq % nsub))
    else:
        w_spec = pl.BlockSpec((jb, tko, ns), lambda i, ko, nq: (nq, ko, 0))

    def body(a_ref, w_ref, o_ref, acc):
        nq = pl.program_id(2)

        @pl.when(nq == 0)
        def _():
            acc[...] = jnp.zeros_like(acc)

        if jb == 1:
            acc[...] += lax.dot_general(a_ref[...].astype(BF16), w_ref[...], (((1,), (1,)), ((), ())),
                                        preferred_element_type=F32)
        else:
            part = acc[...]
            for jj in range(jb):
                part = part + lax.dot_general(a_ref[:, jj * ns:(jj + 1) * ns].astype(BF16), w_ref[jj],
                                              (((1,), (1,)), ((), ())), preferred_element_type=F32)
            acc[...] = part

        @pl.when(nq == nn - 1)
        def _():
            o_ref[...] = acc[...].astype(out_dtype)

    (out,), bufs = _call(
        body, name=name, jobs=jobs,
        out_shape=[jax.ShapeDtypeStruct((m, k), out_dtype)],
        grid=(m // tm, k // tko, nn),
        in_specs=[dy_spec, w_spec],
        out_specs=[pl.BlockSpec((tm, tko), lambda i, ko, nq: (i, ko))],
        scratch_shapes=[pltpu.VMEM((tm, tko), F32)],
        args=[dy, w], sem=("parallel", "parallel", "arbitrary"))
    return out if jobs is None else (out, bufs)


def mm_tn(x, dy, j, *, name, tm=1024):
    m, k = x.shape
    parts, m2, n = (1,) + dy.shape if dy.ndim == 2 else dy.shape
    n *= parts
    assert m == m2 and n % j == 0 and m % tm == 0
    ns = n // j
    tn = _tile(ns, 1408)
    nsub = ns // tn
    jb = _shard_group(j // parts, ns, 1536) if nsub == 1 else 1
    tk = _tile(k, 1408)
    nm = m // tm
    n_steps = j * nsub // jb
    per_part = n_steps // parts
    if dy.ndim == 2:
        dy_spec = pl.BlockSpec((tm, jb * tn), lambda kq, nq, mi: (mi, nq))
    else:
        dy_spec = pl.BlockSpec((None, tm, jb * tn), lambda kq, nq, mi: (nq // per_part, mi, nq % per_part))
    if jb == 1:
        out_spec = pl.BlockSpec((None, tk, tn), lambda kq, nq, mi: (nq // nsub, kq, nq % nsub))
        acc_shape = (tk, tn)
    else:
        out_spec = pl.BlockSpec((jb, tk, ns), lambda kq, nq, mi: (nq, kq, 0))
        acc_shape = (jb, tk, ns)

    def body(x_ref, dy_ref, o_ref, acc):
        mi = pl.program_id(2)

        @pl.when(mi == 0)
        def _():
            acc[...] = jnp.zeros_like(acc)

        xb = x_ref[...].astype(BF16)
        if jb == 1:
            acc[...] += lax.dot_general(xb, dy_ref[...].astype(BF16), (((0,), (0,)), ((), ())),
                                        preferred_element_type=F32)
        else:
            for jj in range(jb):
                acc[jj] += lax.dot_general(xb, dy_ref[:, jj * ns:(jj + 1) * ns].astype(BF16),
                                           (((0,), (0,)), ((), ())), preferred_element_type=F32)

        @pl.when(mi == nm - 1)
        def _():
            o_ref[...] = acc[...].astype(BF16)

    return pl.pallas_call(
        body,
        name=name,
        out_shape=jax.ShapeDtypeStruct((j, k, ns), BF16),
        grid=(k // tk, n_steps, nm),
        in_specs=[
            pl.BlockSpec((tm, tk), lambda kq, nq, mi: (mi, kq)),
            dy_spec,
        ],
        out_specs=out_spec,
        scratch_shapes=[pltpu.VMEM(acc_shape, F32)],
        compiler_params=_params("parallel", "parallel", "arbitrary"),
    )(x, dy)


def rms_fwd(x, gains, *, name, ts=512):
    s, d = x.shape
    n = len(gains)

    def body(x_ref, *refs):
        xv = x_ref[...]
        xh = xv * lax.rsqrt(jnp.mean(xv * xv, axis=-1, keepdims=True) + EPS)
        for g_ref, o_ref in zip(refs[:n], refs[n:]):
            o_ref[...] = (xh * g_ref[...]).astype(BF16)

    row = pl.BlockSpec((ts, d), lambda i: (i, 0))
    vec = pl.BlockSpec((1, d), lambda i: (0, 0))
    return pl.pallas_call(
        body,
        name=name,
        out_shape=[jax.ShapeDtypeStruct((s, d), BF16)] * n,
        grid=(s // ts,),
        in_specs=[row] + [vec] * n,
        out_specs=[row] * n,
        compiler_params=_params("parallel"),
    )(x, *gains)


def rms_bwd(x, gains, dys, dres, *, name, ts=256):
    s, d = x.shape
    n = len(gains)

    def body(x_ref, r_ref, *refs):
        g_refs, dy_refs = refs[:n], refs[n:2 * n]
        dx_ref, dg_refs = refs[2 * n], refs[2 * n + 1:]
        i = pl.program_id(0)
        xv = x_ref[...]
        r = lax.rsqrt(jnp.mean(xv * xv, axis=-1, keepdims=True) + EPS)
        xh = xv * r
        acc = r_ref[...]
        for g_ref, dy_ref, dg_ref in zip(g_refs, dy_refs, dg_refs):
            dy = dy_ref[...].astype(F32)

            @pl.when(i == 0)
            def _():
                dg_ref[...] = jnp.zeros_like(dg_ref)

            dg_ref[...] += jnp.sum(dy * xh, axis=0, keepdims=True)
            dxh = dy * g_ref[...]
            acc = acc + r * (dxh - xh * jnp.mean(dxh * xh, axis=-1, keepdims=True))
        dx_ref[...] = acc

    row = pl.BlockSpec((ts, d), lambda i: (i, 0))
    vec = pl.BlockSpec((1, d), lambda i: (0, 0))
    outs = pl.pallas_call(
        body,
        name=name,
        out_shape=[jax.ShapeDtypeStruct((s, d), F32)] + [jax.ShapeDtypeStruct((1, d), F32)] * n,
        grid=(s // ts,),
        in_specs=[row, row] + [vec] * n + [row] * n,
        out_specs=[row] + [vec] * n,
        compiler_params=_params("arbitrary"),
    )(x, dres, *gains, *dys)
    return outs[0], outs[1:]


def loss_head(h, gain, target, *, ts=256):
    s, d = h.shape

    def body(h_ref, g_ref, t_ref, l_ref, dh_ref, dg_ref):
        i = pl.program_id(0)

        @pl.when(i == 0)
        def _():
            l_ref[...] = jnp.zeros_like(l_ref)
            dg_ref[...] = jnp.zeros_like(dg_ref)

        xv = h_ref[...]
        r = lax.rsqrt(jnp.mean(xv * xv, axis=-1, keepdims=True) + EPS)
        xh = xv * r
        g = g_ref[...]
        err = xh * g - t_ref[...]
        l_ref[...] += 0.5 * jnp.sum(jnp.mean(err * err, axis=-1, keepdims=True))
        dy = err * (1.0 / d)
        dg_ref[...] += jnp.sum(dy * xh, axis=0, keepdims=True)
        dxh = dy * g
        dh_ref[...] = r * (dxh - xh * jnp.mean(dxh * xh, axis=-1, keepdims=True))

    row = pl.BlockSpec((ts, d), lambda i: (i, 0))
    vec = pl.BlockSpec((1, d), lambda i: (0, 0))
    return pl.pallas_call(
        body,
        name="loss_head",
        out_shape=[jax.ShapeDtypeStruct((8, LANE), F32), jax.ShapeDtypeStruct((s, d), F32),
                   jax.ShapeDtypeStruct((1, d), F32)],
        grid=(s // ts,),
        in_specs=[row, vec, row],
        out_specs=[pl.BlockSpec((8, LANE), lambda i: (0, 0)), row, vec],
        compiler_params=_params("arbitrary"),
    )(h, gain, target)


A_BLOCK = (2 * GLA_KEY_DIM + 2 * GLA_VAL_DIM) // LANE


def gate_fwd(proj, w_a2p, b_a2, *, ts=512):
    s = proj.shape[0]

    def body(a_ref, w_ref, b_ref, o_ref):
        z = jnp.dot(a_ref[...].astype(BF16), w_ref[...], preferred_element_type=F32) + b_ref[...]
        o_ref[...] = (jnp.minimum(z, 0.0) - jnp.log(1.0 + jnp.exp(-jnp.abs(z)))) * (1.0 / GATE_NORMALIZER)

    return pl.pallas_call(
        body,
        name="gate_fwd",
        out_shape=jax.ShapeDtypeStruct((s, GLA_KEY_DIM), F32),
        grid=(s // ts,),
        in_specs=[pl.BlockSpec((ts, LANE), lambda i: (i, A_BLOCK)),
                  pl.BlockSpec((LANE, GLA_KEY_DIM), lambda i: (0, 0)),
                  pl.BlockSpec((1, GLA_KEY_DIM), lambda i: (0, 0))],
        out_specs=pl.BlockSpec((ts, GLA_KEY_DIM), lambda i: (i, 0)),
        compiler_params=_params("parallel"),
    )(proj, w_a2p, b_a2)


def gate_bwd(proj, w_a2p, b_a2, dla, *, ts=512):
    s = proj.shape[0]

    def body(a_ref, w_ref, b_ref, dla_ref, da_ref, dw_ref, db_ref):
        i = pl.program_id(0)

        @pl.when(i == 0)
        def _():
            dw_ref[...] = jnp.zeros_like(dw_ref)
            db_ref[...] = jnp.zeros_like(db_ref)

        a = a_ref[...].astype(BF16)
        w = w_ref[...]
        z = jnp.dot(a, w, preferred_element_type=F32) + b_ref[...]
        dz = dla_ref[...] * (1.0 / GATE_NORMALIZER) / (1.0 + jnp.exp(z))
        dzb = dz.astype(BF16)
        da_ref[...] = lax.dot_general(dzb, w, (((1,), (1,)), ((), ())), preferred_element_type=F32).astype(BF16)
        dw_ref[...] += lax.dot_general(a, dzb, (((0,), (0,)), ((), ())), preferred_element_type=F32)
        db_ref[...] += jnp.sum(dz, axis=0, keepdims=True)

    return pl.pallas_call(
        body,
        name="gate_bwd",
        out_shape=[jax.ShapeDtypeStruct((s, LANE), BF16), jax.ShapeDtypeStruct((LANE, GLA_KEY_DIM), F32),
                   jax.ShapeDtypeStruct((1, GLA_KEY_DIM), F32)],
        grid=(s // ts,),
        in_specs=[pl.BlockSpec((ts, LANE), lambda i: (i, A_BLOCK)),
                  pl.BlockSpec((LANE, GLA_KEY_DIM), lambda i: (0, 0)),
                  pl.BlockSpec((1, GLA_KEY_DIM), lambda i: (0, 0)),
                  pl.BlockSpec((ts, GLA_KEY_DIM), lambda i: (i, 0))],
        out_specs=[pl.BlockSpec((ts, LANE), lambda i: (i, 0)),
                   pl.BlockSpec((LANE, GLA_KEY_DIM), lambda i: (0, 0)),
                   pl.BlockSpec((1, GLA_KEY_DIM), lambda i: (0, 0))],
        compiler_params=_params("arbitrary"),
    )(proj, w_a2p, b_a2, dla)


def _chunk_terms(q, k, la):
    c_len = GLA_CHUNK
    row = lax.broadcasted_iota(jnp.int32, (c_len, c_len), 0)
    col = lax.broadcasted_iota(jnp.int32, (c_len, c_len), 1)
    tri = row >= col
    c = jnp.dot(tri.astype(F32), la, preferred_element_type=F32, precision=lax.Precision.HIGHEST)
    last = jnp.sum(la, axis=0, keepdims=True)
    q_dec = q * (GLA_DK ** -0.5) * jnp.exp(c)
    k_inv = k * jnp.exp(-c)
    k_end = k * jnp.exp(last - c)
    return c, last, q_dec, k_inv, k_end, tri


def _dot(a, b, ca, cb):
    return lax.dot_general(a.astype(BF16), b.astype(BF16), (((ca,), (cb,)), ((), ())), preferred_element_type=F32)


def gla_fwd(proj, la, jobs=None):
    s = proj.shape[0]
    n_chunks = s // GLA_CHUNK

    def body(q_ref, k_ref, v_ref, la_ref, o_ref, st_out, st):
        @pl.when(pl.program_id(0) == 0)
        def _():
            st[...] = jnp.zeros_like(st)

        for h in range(GLA_HEADS):
            hk = slice(h * GLA_DK, (h + 1) * GLA_DK)
            hv = slice(h * GLA_DV, (h + 1) * GLA_DV)
            _, last, q_dec, k_inv, k_end, tri = _chunk_terms(q_ref[:, hk], k_ref[:, hk], la_ref[:, hk])
            v = v_ref[:, hv]
            a = jnp.where(tri, _dot(q_dec, k_inv, 1, 1), 0.0)
            state = st[h]
            st_out[h] = state
            o_ref[:, hv] = _dot(a, v, 1, 0) + _dot(q_dec, state, 1, 1)
            st[h] = state * jnp.exp(last) + _dot(v, k_end, 0, 0)

    key = lambda col: pl.BlockSpec((GLA_CHUNK, GLA_KEY_DIM), lambda n: (n, col))
    outs, bufs = _call(
        body, name="gla_fwd", jobs=jobs,
        out_shape=[jax.ShapeDtypeStruct((s, GLA_VAL_DIM), F32),
                   jax.ShapeDtypeStruct((GLA_HEADS, n_chunks, GLA_DV, GLA_DK), F32)],
        grid=(n_chunks,),
        in_specs=[key(0), key(1), pl.BlockSpec((GLA_CHUNK, GLA_VAL_DIM), lambda n: (n, 1)), key(0)],
        out_specs=[pl.BlockSpec((GLA_CHUNK, GLA_VAL_DIM), lambda n: (n, 0)),
                   pl.BlockSpec((GLA_HEADS, None, GLA_DV, GLA_DK), lambda n: (0, n, 0, 0))],
        scratch_shapes=[pltpu.VMEM((GLA_HEADS, GLA_DV, GLA_DK), F32)],
        args=[proj, proj, proj, la], sem=("arbitrary",))
    return outs if jobs is None else (outs, bufs)


def gla_bwd(proj, la, states, do, jobs=None):
    s = proj.shape[0]
    n_chunks = s // GLA_CHUNK
    lastc = n_chunks - 1

    def body(q_ref, k_ref, v_ref, la_ref, do_ref, st_ref, dq_ref, dk_ref, dv_ref, dla_ref, dst):
        @pl.when(pl.program_id(0) == 0)
        def _():
            dst[...] = jnp.zeros_like(dst)

        upper = (lax.broadcasted_iota(jnp.int32, (GLA_CHUNK, GLA_CHUNK), 0)
                 <= lax.broadcasted_iota(jnp.int32, (GLA_CHUNK, GLA_CHUNK), 1)).astype(F32)
        for h in range(GLA_HEADS):
            hk = slice(h * GLA_DK, (h + 1) * GLA_DK)
            hv = slice(h * GLA_DV, (h + 1) * GLA_DV)
            c, last, q_dec, k_inv, k_end, tri = _chunk_terms(q_ref[:, hk], k_ref[:, hk], la_ref[:, hk])
            v = v_ref[:, hv]
            dout = do_ref[:, hv]
            state = st_ref[h]
            dstate = dst[h]
            e_last = jnp.exp(last)
            a = jnp.where(tri, _dot(q_dec, k_inv, 1, 1), 0.0)
            da = jnp.where(tri, _dot(dout, v, 1, 1), 0.0)
            dv_ref[:, hv] = (_dot(a, dout, 0, 0) + _dot(k_end, dstate, 1, 1)).astype(BF16)
            dq_dec = _dot(da, k_inv, 1, 0) + _dot(dout, state, 1, 0)
            dk_inv = _dot(da, q_dec, 0, 0)
            dk_end = _dot(v, dstate, 1, 0)
            dst[h] = dstate * e_last + _dot(dout, q_dec, 0, 0)
            dq_ref[:, hk] = (dq_dec * (GLA_DK ** -0.5) * jnp.exp(c)).astype(BF16)
            dk_ref[:, hk] = (dk_inv * jnp.exp(-c) + dk_end * jnp.exp(last - c)).astype(BF16)
            ke_term = dk_end * k_end
            dc = dq_dec * q_dec - dk_inv * k_inv - ke_term
            dlast = (jnp.sum(ke_term, axis=0, keepdims=True)
                     + e_last * jnp.sum(dstate * state, axis=0, keepdims=True))
            dla_ref[:, hk] = jnp.dot(upper, dc, preferred_element_type=F32,
                                     precision=lax.Precision.HIGHEST) + dlast

    key = lambda col: pl.BlockSpec((GLA_CHUNK, GLA_KEY_DIM), lambda n: (lastc - n, col))
    val = lambda col: pl.BlockSpec((GLA_CHUNK, GLA_VAL_DIM), lambda n: (lastc - n, col))
    outs, bufs = _call(
        body, name="gla_bwd", jobs=jobs,
        out_shape=[jax.ShapeDtypeStruct((s, GLA_KEY_DIM), BF16), jax.ShapeDtypeStruct((s, GLA_KEY_DIM), BF16),
                   jax.ShapeDtypeStruct((s, GLA_VAL_DIM), BF16), jax.ShapeDtypeStruct((s, GLA_KEY_DIM), F32)],
        grid=(n_chunks,),
        in_specs=[key(0), key(1), val(1), key(0), val(0),
                  pl.BlockSpec((GLA_HEADS, None, GLA_DV, GLA_DK), lambda n: (0, lastc - n, 0, 0))],
        out_specs=[key(0), key(0), val(0), key(0)],
        scratch_shapes=[pltpu.VMEM((GLA_HEADS, GLA_DV, GLA_DK), F32)],
        args=[proj, proj, proj, la, do, states], sem=("arbitrary",))
    return outs if jobs is None else (outs, bufs)


R_BLOCK = (2 * GLA_KEY_DIM + GLA_VAL_DIM) // GLA_DV


def headnorm_fwd(o, proj, hn, *, ts=512):
    s = o.shape[0]

    def body(o_ref, r_ref, g_ref, out_ref):
        ov = o_ref[...]
        oh = ov * lax.rsqrt(jnp.mean(ov * ov, axis=-1, keepdims=True) + EPS)
        r = r_ref[...]
        out_ref[...] = (oh * g_ref[...] * (r * jax.nn.sigmoid(r))).astype(BF16)

    return pl.pallas_call(
        body,
        name="headnorm_fwd",
        out_shape=jax.ShapeDtypeStruct((s, GLA_VAL_DIM), BF16),
        grid=(s // ts, GLA_HEADS),
        in_specs=[pl.BlockSpec((ts, GLA_DV), lambda i, h: (i, h)),
                  pl.BlockSpec((ts, GLA_DV), lambda i, h: (i, R_BLOCK + h)),
                  pl.BlockSpec((1, GLA_DV), lambda i, h: (0, 0))],
        out_specs=pl.BlockSpec((ts, GLA_DV), lambda i, h: (i, h)),
        compiler_params=_params("parallel", "parallel"),
    )(o, proj, hn)


def headnorm_bwd(o, proj, hn, dog, *, ts=512):
    s = o.shape[0]

    def body(o_ref, r_ref, g_ref, dog_ref, do_ref, dr_ref, dg_ref):
        @pl.when((pl.program_id(0) == 0) & (pl.program_id(1) == 0))
        def _():
            dg_ref[...] = jnp.zeros_like(dg_ref)

        ov = o_ref[...]
        rr = lax.rsqrt(jnp.mean(ov * ov, axis=-1, keepdims=True) + EPS)
        oh = ov * rr
        g = g_ref[...]
        r = r_ref[...]
        sig = jax.nn.sigmoid(r)
        gate = r * sig
        dog_v = dog_ref[...]
        d_on = dog_v * gate
        dr_ref[...] = (dog_v * (oh * g) * (sig * (1.0 + r * (1.0 - sig)))).astype(BF16)
        dg_ref[...] += jnp.sum(d_on * oh, axis=0, keepdims=True)
        doh = d_on * g
        do_ref[...] = rr * (doh - oh * jnp.mean(doh * oh, axis=-1, keepdims=True))

    return pl.pallas_call(
        body,
        name="headnorm_bwd",
        out_shape=[jax.ShapeDtypeStruct((s, GLA_VAL_DIM), F32), jax.ShapeDtypeStruct((s, GLA_VAL_DIM), BF16),
                   jax.ShapeDtypeStruct((1, GLA_DV), F32)],
        grid=(s // ts, GLA_HEADS),
        in_specs=[pl.BlockSpec((ts, GLA_DV), lambda i, h: (i, h)),
                  pl.BlockSpec((ts, GLA_DV), lambda i, h: (i, R_BLOCK + h)),
                  pl.BlockSpec((1, GLA_DV), lambda i, h: (0, 0)),
                  pl.BlockSpec((ts, GLA_DV), lambda i, h: (i, h))],
        out_specs=[pl.BlockSpec((ts, GLA_DV), lambda i, h: (i, h)),
                   pl.BlockSpec((ts, GLA_DV), lambda i, h: (i, h)),
                   pl.BlockSpec((1, GLA_DV), lambda i, h: (0, 0))],
        compiler_params=_params("arbitrary", "arbitrary"),
    )(o, proj, hn, dog)


CONV_TC = 128
SQRT_HALF = 0.7071067811865476
INV_SQRT_2PI = 0.3989422804014327


def _conv_gate(g_ref, cw_ref, cb_ref):
    g0 = g_ref[...].astype(F32)
    t = lax.broadcasted_iota(jnp.int32, g0.shape, 0)
    g1 = jnp.where(t >= 1, pltpu.roll(g0, 1, 0), 0.0)
    g2 = jnp.where(t >= 2, pltpu.roll(g0, 2, 0), 0.0)
    gc = cw_ref[0:1, :] * g2 + cw_ref[1:2, :] * g1 + cw_ref[2:3, :] * g0 + cb_ref[...]
    return g0, g1, g2, gc, t


def convglu_fwd(up, conv_w, conv_b):
    s = up.shape[0]
    nc = D_FF // CONV_TC

    def body(u_ref, g_ref, cw_ref, cb_ref, o_ref):
        _, _, _, gc, _ = _conv_gate(g_ref, cw_ref, cb_ref)
        gelu = 0.5 * gc * (1.0 + lax.erf(gc * SQRT_HALF))
        o_ref[...] = (gelu * u_ref[...].astype(F32)).astype(BF16)

    return pl.pallas_call(
        body,
        name="convglu_fwd",
        out_shape=jax.ShapeDtypeStruct((s, D_FF), BF16),
        grid=(nc,),
        in_specs=[pl.BlockSpec((s, CONV_TC), lambda c: (0, c)),
                  pl.BlockSpec((s, CONV_TC), lambda c: (0, nc + c)),
                  pl.BlockSpec((3, CONV_TC), lambda c: (0, c)),
                  pl.BlockSpec((1, CONV_TC), lambda c: (0, c))],
        out_specs=pl.BlockSpec((s, CONV_TC), lambda c: (0, c)),
        compiler_params=_params("parallel"),
    )(up, up, conv_w, conv_b)


def convglu_bwd(up, conv_w, conv_b, dact, *, name, jobs=None):
    s = up.shape[0]
    nc = D_FF // CONV_TC

    def body(u_ref, g_ref, cw_ref, cb_ref, da_ref, dup_ref, dcw_ref, dcb_ref):
        du_ref, dg_ref = dup_ref.at[0], dup_ref.at[1]
        g0, g1, g2, gc, t = _conv_gate(g_ref, cw_ref, cb_ref)
        cdf = 0.5 * (1.0 + lax.erf(gc * SQRT_HALF))
        da = da_ref[...].astype(F32)
        du_ref[...] = (da * gc * cdf).astype(BF16)
        dgc = da * u_ref[...].astype(F32) * (cdf + gc * jnp.exp(-0.5 * gc * gc) * INV_SQRT_2PI)
        dcb_ref[...] = jnp.sum(dgc, axis=0, keepdims=True)
        dcw_ref[0:1, :] = jnp.sum(dgc * g2, axis=0, keepdims=True)
        dcw_ref[1:2, :] = jnp.sum(dgc * g1, axis=0, keepdims=True)
        dcw_ref[2:3, :] = jnp.sum(dgc * g0, axis=0, keepdims=True)
        n1 = jnp.where(t < s - 1, pltpu.roll(dgc, s - 1, 0), 0.0)
        n2 = jnp.where(t < s - 2, pltpu.roll(dgc, s - 2, 0), 0.0)
        dg_ref[...] = (cw_ref[2:3, :] * dgc + cw_ref[1:2, :] * n1 + cw_ref[0:1, :] * n2).astype(BF16)

    col = pl.BlockSpec((s, CONV_TC), lambda c: (0, c))
    outs, bufs = _call(
        body, name=name, jobs=jobs,
        out_shape=[jax.ShapeDtypeStruct((2, s, D_FF), BF16),
                   jax.ShapeDtypeStruct((3, D_FF), F32), jax.ShapeDtypeStruct((1, D_FF), F32)],
        grid=(nc,),
        in_specs=[col, pl.BlockSpec((s, CONV_TC), lambda c: (0, nc + c)),
                  pl.BlockSpec((3, CONV_TC), lambda c: (0, c)),
                  pl.BlockSpec((1, CONV_TC), lambda c: (0, c)), col],
        out_specs=[pl.BlockSpec((2, s, CONV_TC), lambda c: (0, 0, c)), pl.BlockSpec((3, CONV_TC), lambda c: (0, c)),
                   pl.BlockSpec((1, CONV_TC), lambda c: (0, c))],
        args=[up, up, conv_w, conv_b, dact], sem=("parallel",))
    return outs if jobs is None else (outs, bufs)


REGION = ATT_BLOCK * DILATIONS[-1]
SLOPE_TILE = (8, LANE)


def _slope_table():
    return jnp.broadcast_to(jnp.asarray(ALIBI_SLOPES, F32)[:, None, None], (ATT_HEADS,) + SLOPE_TILE)


def _sub(r, i, d):
    start = r + d * ATT_BLOCK * i
    return pl.ds(start, ATT_BLOCK) if d == 1 else pl.ds(start, ATT_BLOCK, stride=d)


def _att_bias(slope, d, first_key):
    qa = lax.broadcasted_iota(jnp.int32, (ATT_BLOCK, 2 * ATT_BLOCK), 0)
    cc = lax.broadcasted_iota(jnp.int32, (ATT_BLOCK, 2 * ATT_BLOCK), 1)
    dist = qa - cc + ATT_BLOCK
    ok = (dist >= 0) & (dist <= ATT_BLOCK) & (cc >= first_key)
    return jnp.where(ok, (slope * (-float(d))) * dist.astype(F32), NEG)


def _keys(kc_ref, kp_ref, r, i, d, nsub):
    prev = kp_ref[_sub(r, nsub - 1, d), :] if i == 0 else kc_ref[_sub(r, i - 1, d), :]
    return jnp.concatenate([prev, kc_ref[_sub(r, i, d), :]], axis=0)


def _per_residue(d, body):
    for r in range(d):
        body(r)


def attn_fwd(q, kv, jobs=None):
    s = q.shape[0]
    nreg = s // REGION
    scale = HEAD_DIM ** -0.5

    def body(sl_ref, q_ref, kc_ref, kp_ref, vc_ref, vp_ref, o_ref, lse_ref, ob, lb):
        n = pl.program_id(0)
        g = pl.program_id(2)
        slope = sl_ref[0:1, 0:1]
        first_key = jnp.where(n > 0, 0, ATT_BLOCK)

        def branch(gi, d):
            nsub = REGION // (ATT_BLOCK * d)
            bias = _att_bias(slope, d, 0)
            bias0 = _att_bias(slope, d, first_key)

            def residue(r):
                for i in range(nsub):
                    rows = _sub(r, i, d)
                    kcat = _keys(kc_ref, kp_ref, r, i, d, nsub)
                    vcat = _keys(vc_ref, vp_ref, r, i, d, nsub)
                    sc = _dot(q_ref[rows, :], kcat, 1, 1) * scale + (bias0 if i == 0 else bias)
                    m = jnp.max(sc, axis=-1, keepdims=True)
                    p = jnp.exp(sc - m)
                    l = jnp.sum(p, axis=-1, keepdims=True)
                    ob.at[gi][rows, :] = _dot(p, vcat, 1, 0) / l
                    lb.at[gi][rows, :] = jnp.broadcast_to(m + jnp.log(l), (ATT_BLOCK, HEAD_DIM))

            _per_residue(d, residue)

        for gi, d in enumerate(DILATIONS):
            @pl.when(g == gi)
            def _():
                branch(gi, d)

        @pl.when(g == len(DILATIONS) - 1)
        def _():
            def merge(c, carry):
                rows = pl.ds(pl.multiple_of(c * ATT_BLOCK, ATT_BLOCK), ATT_BLOCK)
                l0, l1, l2 = lb[0, rows, :], lb[1, rows, :], lb[2, rows, :]
                m = jnp.maximum(jnp.maximum(l0, l1), l2)
                e0, e1, e2 = jnp.exp(l0 - m), jnp.exp(l1 - m), jnp.exp(l2 - m)
                den = e0 + e1 + e2
                o_ref[rows, :] = (e0 * ob[0, rows, :] + e1 * ob[1, rows, :] + e2 * ob[2, rows, :]) / den
                lse_ref[rows, :] = m + jnp.log(den)
                return carry
            lax.fori_loop(0, REGION // ATT_BLOCK, merge, 0)

    def blk(col, prev=False):
        if prev:
            return pl.BlockSpec((REGION, HEAD_DIM), lambda n, h, g: (jnp.maximum(n - 1, 0), col(h, g)))
        return pl.BlockSpec((REGION, HEAD_DIM), lambda n, h, g: (n, col(h, g)))

    k_col = lambda h, g: h
    v_col = lambda h, g: ATT_HEADS + h
    outs, bufs = _call(
        body, name="attn_fwd", jobs=jobs,
        out_shape=[jax.ShapeDtypeStruct((s, ATT_HEADS * HEAD_DIM), F32)] * 2,
        grid=(nreg, ATT_HEADS, len(DILATIONS)),
        in_specs=[pl.BlockSpec((None,) + SLOPE_TILE, lambda n, h, g: (h, 0, 0)),
                  blk(lambda h, g: g * ATT_HEADS + h), blk(k_col), blk(k_col, True), blk(v_col), blk(v_col, True)],
        out_specs=[blk(k_col), blk(k_col)],
        scratch_shapes=[pltpu.VMEM((len(DILATIONS), REGION, HEAD_DIM), F32)] * 2,
        args=[_slope_table(), q, kv, kv, kv, kv], sem=("parallel", "parallel", "arbitrary"))
    return outs if jobs is None else (outs, bufs)


def attn_bwd(q, kv, o, lse, do, jobs=None):
    s = q.shape[0]
    nreg = s // REGION
    scale = HEAD_DIM ** -0.5

    def body(sl_ref, q_ref, kc_ref, kp_ref, vc_ref, vp_ref, o_ref, lse_ref, do_ref,
             qn_ref, on_ref, lsen_ref, don_ref, dq_ref, dkv_ref, dlt, dltn):
        n = pl.program_id(0)
        g = pl.program_id(2)
        slope = sl_ref[0:1, 0:1]
        first_key = jnp.where(n > 0, 0, ATT_BLOCK)
        has_next = n + 1 < nreg

        @pl.when(g == 0)
        def _():
            dkv_ref[...] = jnp.zeros_like(dkv_ref)

            def deltas(c, carry):
                rows = pl.ds(pl.multiple_of(c * ATT_BLOCK, ATT_BLOCK), ATT_BLOCK)
                dlt[rows, :] = jnp.sum(do_ref[rows, :] * o_ref[rows, :], axis=-1, keepdims=True)
                dltn[rows, :] = jnp.sum(don_ref[rows, :] * on_ref[rows, :], axis=-1, keepdims=True)
                return carry
            lax.fori_loop(0, REGION // ATT_BLOCK, deltas, 0)

        def branch(d):
            nsub = REGION // (ATT_BLOCK * d)
            bias = _att_bias(slope, d, 0)
            bias0 = _att_bias(slope, d, first_key)

            def residue(r):
                for i in range(nsub):
                    rows = _sub(r, i, d)
                    kcat = _keys(kc_ref, kp_ref, r, i, d, nsub)
                    vcat = _keys(vc_ref, vp_ref, r, i, d, nsub)
                    qb = q_ref[rows, :]
                    dob = do_ref[rows, :]
                    sc = _dot(qb, kcat, 1, 1) * scale + (bias0 if i == 0 else bias)
                    p = jnp.exp(sc - lse_ref[rows, :][:, 0:1])
                    ds = p * (_dot(dob, vcat, 1, 1) - dlt[rows, :])
                    dq_ref[rows, :] = _dot(ds, kcat, 1, 0) * scale
                    dk = _dot(ds, qb, 0, 0) * scale
                    dv = _dot(p, dob, 0, 0)
                    dkv_ref.at[0][rows, :] += dk[ATT_BLOCK:]
                    dkv_ref.at[1][rows, :] += dv[ATT_BLOCK:]
                    if i > 0:
                        prev = _sub(r, i - 1, d)
                        dkv_ref.at[0][prev, :] += dk[:ATT_BLOCK]
                        dkv_ref.at[1][prev, :] += dv[:ATT_BLOCK]

            _per_residue(d, residue)

            @pl.when(has_next)
            def _():
                bias_prev = bias[:, :ATT_BLOCK]

                def residue_next(r):
                    last = _sub(r, nsub - 1, d)
                    first = _sub(r, 0, d)
                    qb = qn_ref[first, :]
                    dob = don_ref[first, :]
                    sc = _dot(qb, kc_ref[last, :], 1, 1) * scale + bias_prev
                    p = jnp.exp(sc - lsen_ref[first, :][:, 0:1])
                    ds = p * (_dot(dob, vc_ref[last, :], 1, 1) - dltn[first, :])
                    dkv_ref.at[0][last, :] += _dot(ds, qb, 0, 0) * scale
                    dkv_ref.at[1][last, :] += _dot(p, dob, 0, 0)

                _per_residue(d, residue_next)

        for gi, d in enumerate(DILATIONS):
            @pl.when(g == gi)
            def _():
                branch(d)

    last_reg = nreg - 1

    def blk(col, shift=0):
        if shift < 0:
            return pl.BlockSpec((REGION, HEAD_DIM), lambda n, h, g: (jnp.maximum(n - 1, 0), col(h, g)))
        if shift > 0:
            return pl.BlockSpec((REGION, HEAD_DIM), lambda n, h, g: (jnp.minimum(n + 1, last_reg), col(h, g)))
        return pl.BlockSpec((REGION, HEAD_DIM), lambda n, h, g: (n, col(h, g)))

    q_col = lambda h, g: g * ATT_HEADS + h
    k_col = lambda h, g: h
    v_col = lambda h, g: ATT_HEADS + h
    outs, bufs = _call(
        body, name="attn_bwd", jobs=jobs,
        out_shape=[jax.ShapeDtypeStruct(q.shape, F32), jax.ShapeDtypeStruct((2, s, ATT_HEADS * HEAD_DIM), F32)],
        grid=(nreg, ATT_HEADS, len(DILATIONS)),
        in_specs=[pl.BlockSpec((None,) + SLOPE_TILE, lambda n, h, g: (h, 0, 0)),
                  blk(q_col), blk(k_col), blk(k_col, -1), blk(v_col), blk(v_col, -1),
                  blk(k_col), blk(k_col), blk(k_col),
                  blk(q_col, 1), blk(k_col, 1), blk(k_col, 1), blk(k_col, 1)],
        out_specs=[blk(q_col), pl.BlockSpec((2, REGION, HEAD_DIM), lambda n, h, g: (0, n, h))],
        scratch_shapes=[pltpu.VMEM((REGION, 1), F32)] * 2,
        args=[_slope_table(), q, kv, kv, kv, kv, o, lse, do, q, o, lse, do],
        sem=("parallel", "parallel", "arbitrary"))
    return outs if jobs is None else (outs, bufs)


def _adam(w, g, m, v):
    m = ADAM_B1 * m + (1.0 - ADAM_B1) * g
    v = ADAM_B2 * v + (1.0 - ADAM_B2) * (g * g)
    m_hat = m / (1.0 - ADAM_B1 ** ADAM_STEP)
    v_hat = v / (1.0 - ADAM_B2 ** ADAM_STEP)
    delta = -ADAM_LR * (m_hat / (jnp.sqrt(v_hat) + ADAM_EPS) + ADAM_WD * w)
    return delta, m, v


def adam_sharded(recvs, w, m, v, *, name):
    layers = len(recvs)
    n_src, r, c = recvs[0].shape
    tr = _rows(r, c)

    def body(*refs):
        p_refs = refs[:layers]
        w_ref, m_ref, v_ref, g_ref, d_ref, mo_ref, vo_ref = refs[layers:]
        for layer, p_ref in enumerate(p_refs):
            @pl.when(pl.program_id(0) == layer)
            def _():
                g = p_ref[0].astype(F32)
                for src in range(1, n_src):
                    g = g + p_ref[src].astype(F32)
                delta, m_new, v_new = _adam(w_ref[...], g, m_ref[...], v_ref[...])
                g_ref[...] = g
                d_ref[...] = delta
                mo_ref[...] = m_new
                vo_ref[...] = v_new

    blk = pl.BlockSpec((None, tr, c), lambda l, i: (l, i, 0))
    out = jax.ShapeDtypeStruct((layers, r, c), F32)
    part = [pl.BlockSpec((n_src, tr, c), functools.partial(lambda l, i, layer: (0, jnp.where(l == layer, i, 0), 0),
                                                            layer=layer)) for layer in range(layers)]
    return pl.pallas_call(
        body,
        name=name,
        out_shape=[out] * 4,
        grid=(layers, r // tr),
        in_specs=part + [blk, blk, blk],
        out_specs=[blk] * 4,
        compiler_params=_params("parallel", "parallel"),
    )(*recvs, w, m, v)


def sum_partials(parts):
    n_src, r, c = parts.shape

    def body(p_ref, o_ref):
        g = p_ref[0]
        for src in range(1, n_src):
            g = g + p_ref[src]
        o_ref[...] = g

    return pl.pallas_call(
        body,
        name="sum_small_grads",
        out_shape=jax.ShapeDtypeStruct((r, c), F32),
    )(parts)


def adam_packed(w, g, m, v):
    def body(w_ref, g_ref, m_ref, v_ref, d_ref, mo_ref, vo_ref):
        delta, m_new, v_new = _adam(w_ref[...], g_ref[...], m_ref[...], v_ref[...])
        d_ref[...] = delta
        mo_ref[...] = m_new
        vo_ref[...] = v_new

    out = jax.ShapeDtypeStruct(w.shape, F32)
    return pl.pallas_call(body, name="adam_small", out_shape=[out] * 3)(w, g, m, v)


def all_gather(srcs, *, name):
    n = len(srcs)

    def body(*refs):
        src, dst = refs[:n], refs[n:2 * n]
        send_sems, recv_sems, local_sems = refs[2 * n:]
        x, y, c, me = _place()
        sibling = (x, y, 1 - c)
        chips = [(1 - x, y), (x, 1 - y), (1 - x, 1 - y)]

        def index(px, py, pc):
            return 4 * px + 2 * py + pc

        def copy(p, k, block, to, from_src=False):
            slot = dst[p].at[index(*block)]
            return pltpu.make_async_remote_copy(
                src_ref=src[p] if from_src else slot, dst_ref=slot,
                send_sem=send_sems.at[p, k], recv_sem=recv_sems.at[p, k],
                device_id=to, device_id_type=MESH)

        mine = [pltpu.make_async_copy(src[p], dst[p].at[me], local_sems.at[p]) for p in range(n)]
        for cp in mine:
            cp.start()
        first = []
        for p in range(n):
            first.append(copy(p, 0, (x, y, c), sibling, from_src=True))
            for jj, chip in enumerate(chips):
                first.append(copy(p, 1 + jj, (x, y, c), (*chip, c), from_src=True))
        for cp in first:
            cp.start()
        passed = []
        for jj, chip in enumerate(chips):
            for p in range(n):
                copy(p, 1 + jj, (*chip, c), (x, y, c)).wait_recv()
                fwd = copy(p, 4 + jj, (*chip, c), sibling)
                fwd.start()
                passed.append(fwd)
        for p in range(n):
            copy(p, 0, sibling, (x, y, c)).wait_recv()
            for jj, chip in enumerate(chips):
                copy(p, 4 + jj, (*chip, 1 - c), (x, y, c)).wait_recv()
        for cp in first + passed:
            cp.wait_send()
        for cp in mine:
            cp.wait()

    return pl.pallas_call(
        body,
        name=name,
        out_shape=[jax.ShapeDtypeStruct((N_DEV,) + a.shape, a.dtype) for a in srcs],
        in_specs=[ANY] * n,
        out_specs=[ANY] * n,
        scratch_shapes=[pltpu.SemaphoreType.DMA((n, 7)), pltpu.SemaphoreType.DMA((n, 7)),
                        pltpu.SemaphoreType.DMA((n,))],
    )(*srcs)


def exchange_only(*, name, jobs):
    def body(o_ref):
        o_ref[...] = jnp.zeros_like(o_ref)

    _, bufs = _call(body, name=name, jobs=jobs, out_shape=[jax.ShapeDtypeStruct((8, LANE), F32)], grid=(1,),
                    in_specs=[], out_specs=[pl.BlockSpec((8, LANE), lambda i: (0, 0))], args=[], sem=("arbitrary",))
    return None, bufs


def _pack_rows(parts, rows):
    flat = jnp.concatenate([p.reshape(-1) for p in parts])
    return jnp.pad(flat, (0, rows * LANE - flat.shape[0])).reshape(rows, LANE)


def _unpack_rows(packed, shapes):
    flat = packed.reshape(-1)
    out, at = [], 0
    for sh in shapes:
        size = 1
        for dim in sh:
            size *= dim
        out.append(flat[at:at + size].reshape(sh))
        at += size
    return out


CONV_W_PAD = 768
SMALL_W_ROWS = 56


def _pack_small_weights(w_a2, b_a2, hn, conv_w):
    cw = jnp.pad(conv_w.reshape(6, -1), ((0, 0), (0, CONV_W_PAD - conv_w.shape[-1]))).reshape(-1, LANE)
    rows = jnp.concatenate([w_a2[0], b_a2, jnp.pad(hn, ((0, 0), (0, LANE - hn.shape[-1]))), cw], axis=0)
    return jnp.pad(rows, ((0, SMALL_W_ROWS - rows.shape[0]), (0, 0)))


def _unpack_small_weights(gathered):
    w_a2 = gathered[:, 0:GATE_RANK, :].transpose(1, 0, 2).reshape(GATE_RANK, GLA_KEY_DIM)
    b_a2 = gathered[:, GATE_RANK, :].reshape(1, GLA_KEY_DIM)
    hn = gathered[:, GATE_RANK + 1, :GLA_DV // N_DEV].reshape(1, GLA_DV)
    per = D_FF // N_DEV
    cw = gathered[:, GATE_RANK + 2:GATE_RANK + 2 + 6 * CONV_W_PAD // LANE, :].reshape(N_DEV, 6, CONV_W_PAD)[:, :, :per]
    cw = cw.reshape(N_DEV, 2, 3, per).transpose(1, 2, 0, 3).reshape(2, 3, D_FF)
    return w_a2, b_a2, hn, cw


SCHEDULE = {
    "gla_in": [("g1", "gout", None), ("g1", "up0", (0, 1024))],
    "gla_fwd": [("g2", "gout", None), ("g2", "up0", (0, 1024)), ("g1", "up0", (1024, 2048)), ("g1", "dn0", (0, 352))],
    "gla_out": [("g2", "up0", (1024, 2048)), ("g2", "dn0", (0, 352)), ("g1", "dn0", (352, 704))],
    "ffn_up0": [("g2", "dn0", (352, 704)), ("g1", "kv", None), ("g1", "q", None)],
    "ffn_down0": [("g2", "kv", None), ("g2", "q", None), ("g1", "dout", None), ("g1", "up1", (0, 704))],
    "kv_proj": [("g2", "dout", None), ("g2", "up1", (0, 704)), ("g1", "up1", (704, 1408))],
    "q_proj": [("g2", "up1", (704, 1408)), ("g1", "up1", (1408, 2048))],
    "attn_fwd": [("g2", "up1", (1408, 2048)), ("g1", "dn1", None)],
    "dsa_out": [("g2", "dn1", None)],
    "ffn_down_dx1": [("sc", "dn1", (0, 352))],
    "convglu_bwd1": [("sc", "dn1", (352, 704))],
    "ffn_up_dx1": [("sc", "up1", (0, 1024))],
    "attn_bwd": [("sc", "up1", (1024, 2048)), ("sc", "dout", None)],
    "q_proj_dx": [("sc", "q", (0, 1408))],
    "kv_proj_dx": [("sc", "q", (1408, 2048)), ("sc", "kv", (0, 1024))],
    "ffn_down_dx0": [("sc", "kv", (1024, 2048)), ("sc", "dn0", (0, 176))],
    "convglu_bwd0": [("sc", "dn0", (176, 528))],
    "ffn_up_dx0": [("sc", "dn0", (528, 704)), ("sc", "up0", (0, 896))],
    "gla_bwd": [("sc", "up0", (896, 1792)), ("sc", "gout", None)],
    "gla_in_dx": [("sc", "up0", (1792, 2048)), ("sc", "in", (0, 1024))],
    "grads_tail": [("sc", "in", (1024, 2048))],
}
ROW_SHARDED = ("gout", "dout", "dn0", "dn1")


class Plan:
    def __init__(self, weights, srcs=None):
        self.w = dict(weights)
        self.srcs = srcs
        self.grads = {}
        self.recv = {}
        self._names = None

    def weight(self, name):
        buf = self.w[name]
        if name in ROW_SHARDED:
            return buf.reshape(1, buf.shape[0] * buf.shape[1], buf.shape[2])
        return buf

    def jobs(self, call):
        ops = SCHEDULE.get(call)
        if self.srcs is None or not ops:
            return None
        jobs, handles = Jobs(), {}
        for op, name, rows in ops:
            store = self.recv if op == "sc" else self.w
            if name not in handles:
                if name in store:
                    handles[name] = jobs.thru(store[name])
                elif op == "sc":
                    handles[name] = jobs.new(self.grads[name].shape, BF16)
                else:
                    handles[name] = jobs.new((N_DEV,) + self.srcs[name].shape, BF16)
            if op == "g1":
                jobs.gather_ici(self.srcs[name], handles[name], rows)
            elif op == "g2":
                jobs.gather_d2d(handles[name], rows)
            else:
                jobs.scatter(self.grads[name], handles[name], rows)
        self._names = [(name, self.recv if ops[0][0] == "sc" else self.w) for name in handles]
        assert len({op == "sc" for op, _, _ in ops}) == 1
        return jobs

    def run(self, call, fn, *args, **kwargs):
        jobs = self.jobs(call)
        if jobs is None:
            return fn(*args, **kwargs)
        out, bufs = fn(*args, jobs=jobs, **kwargs)
        for (name, store), buf in zip(self._names, bufs):
            store[name] = buf
        return out


def _ffn_fwd(plan, h, norm_g, conv_w, conv_b, tag):
    (n,) = rms_fwd(h, [norm_g], name=f"ffn_norm_fwd{tag}")
    up = plan.run(f"ffn_up{tag}", mm_nn, n, plan.weight(f"up{tag}"), out_dtype=BF16, name=f"ffn_up{tag}")
    act = convglu_fwd(up, conv_w, conv_b)
    h_out = plan.run(f"ffn_down{tag}", mm_nn, act, plan.weight(f"dn{tag}"), out_dtype=F32, res=h,
                     name=f"ffn_down{tag}")
    return h_out, (n, up, act)


def _by_rows(dw):
    return dw.reshape(N_DEV, dw.shape[1] // N_DEV, dw.shape[2])


def _ffn_bwd(plan, dh_out, h, saved, norm_g, conv_w, conv_b, tag):
    n, up, act = saved
    plan.grads[f"dn{tag}"] = _by_rows(mm_tn(act, dh_out, 1, name=f"ffn_down_dw{tag}"))
    dact = plan.run(f"ffn_down_dx{tag}", mm_nt, dh_out, plan.weight(f"dn{tag}"), out_dtype=BF16,
                    name=f"ffn_down_dx{tag}")
    dup, dconv_w, dconv_b = plan.run(f"convglu_bwd{tag}", convglu_bwd, up, conv_w, conv_b, dact,
                                     name=f"convglu_bwd{tag}")
    plan.grads[f"up{tag}"] = mm_tn(n, dup, N_DEV, name=f"ffn_up_dw{tag}")
    dn = plan.run(f"ffn_up_dx{tag}", mm_nt, dup, plan.weight(f"up{tag}"), out_dtype=F32, name=f"ffn_up_dx{tag}")
    dh, (dnorm,) = rms_bwd(h, [norm_g], [dn], dh_out, name=f"ffn_norm_bwd{tag}")
    return dh, dnorm, dconv_w, dconv_b


def local_step(x, target, wts, plan):
    row = lambda v: v.reshape(1, -1)
    attn_norm, ffn_norm = wts["attn_norm"], wts["ffn_norm"]
    conv_w, conv_b = wts["ffn_conv_w"], wts["ffn_conv_b"]

    (n1,) = rms_fwd(x, [row(attn_norm[0])], name="attn_norm_fwd0")
    proj = plan.run("gla_in", mm_nn, n1, wts["gla_w_in"], out_dtype=F32, name="gla_in")
    la = gate_fwd(proj, wts["gla_w_a2"], wts["gla_b_a2"])
    o_gla, states = plan.run("gla_fwd", gla_fwd, proj, la)
    og = headnorm_fwd(o_gla, proj, wts["gla_head_norm"])
    h1 = plan.run("gla_out", mm_nn, og, plan.weight("gout"), out_dtype=F32, res=x, name="gla_out")
    h2, ffn0 = _ffn_fwd(plan, h1, row(ffn_norm[0]), conv_w[0], row(conv_b[0]), "0")

    kvn, n3 = rms_fwd(h2, [row(wts["kv_norm"]), row(attn_norm[1])], name="kv_attn_norm_fwd")
    kv = plan.run("kv_proj", mm_nn, kvn, plan.weight("kv"), out_dtype=F32, name="kv_proj")
    q = plan.run("q_proj", mm_nn, n3, plan.weight("q"), out_dtype=F32, name="q_proj")
    o_att, lse = plan.run("attn_fwd", attn_fwd, q, kv)
    h3 = plan.run("dsa_out", mm_nn, o_att, plan.weight("dout"), out_dtype=F32, res=h2, name="dsa_out")
    h4, ffn1 = _ffn_fwd(plan, h3, row(ffn_norm[1]), conv_w[1], row(conv_b[1]), "1")

    loss_tile, dh4, d_final = loss_head(h4, row(wts["final_norm"]), target)

    dh3, d_ffn1, dcw1, dcb1 = _ffn_bwd(plan, dh4, h3, ffn1, row(ffn_norm[1]), conv_w[1], row(conv_b[1]), "1")
    plan.grads["dout"] = _by_rows(mm_tn(o_att, dh3, 1, name="dsa_out_dw"))
    do_att = mm_nt(dh3, plan.weight("dout"), out_dtype=F32, name="dsa_out_dx")
    dq, dkv = plan.run("attn_bwd", attn_bwd, q, kv, o_att, lse, do_att)
    plan.grads["q"] = mm_tn(n3, dq, N_DEV, name="q_proj_dw")
    dn3 = plan.run("q_proj_dx", mm_nt, dq, plan.weight("q"), out_dtype=F32, name="q_proj_dx")
    plan.grads["kv"] = mm_tn(kvn, dkv, N_DEV, name="kv_proj_dw")
    dkvn = plan.run("kv_proj_dx", mm_nt, dkv, plan.weight("kv"), out_dtype=F32, name="kv_proj_dx")
    dh2, (d_kvnorm, d_attn1) = rms_bwd(h2, [row(wts["kv_norm"]), row(attn_norm[1])], [dkvn, dn3], dh3,
                                       name="kv_attn_norm_bwd")
    dh1, d_ffn0, dcw0, dcb0 = _ffn_bwd(plan, dh2, h1, ffn0, row(ffn_norm[0]), conv_w[0], row(conv_b[0]), "0")
    plan.grads["gout"] = _by_rows(mm_tn(og, dh1, 1, name="gla_out_dw"))
    dog = mm_nt(dh1, plan.weight("gout"), out_dtype=F32, name="gla_out_dx")
    do_gla, dr, d_hn = headnorm_bwd(o_gla, proj, wts["gla_head_norm"], dog)
    dq_g, dk_g, dv_g, dla = plan.run("gla_bwd", gla_bwd, proj, la, states, do_gla)
    da, dw_a2p, db_a2 = gate_bwd(proj, wts["gla_w_a2"], wts["gla_b_a2"], dla)
    dproj = jnp.concatenate([dq_g, dk_g, dv_g, dr, da], axis=1)
    assert dproj.shape[1] == GLA_IN_PAD
    dw_in = mm_tn(n1, dproj, 1, name="gla_in_dw")
    plan.grads["in"] = dw_in[0, :, :GLA_IN_DIM].reshape(D_MODEL, N_DEV, GLA_IN_DIM // N_DEV).transpose(1, 0, 2)
    dn1 = plan.run("gla_in_dx", mm_nt, dproj, wts["gla_w_in"], out_dtype=F32, name="gla_in_dx")
    grad_x, (d_attn0,) = rms_bwd(x, [row(attn_norm[0])], [dn1], dh1, name="attn_norm_bwd0")

    small = dict(
        attn_norm=jnp.concatenate([d_attn0, d_attn1], axis=0),
        ffn_norm=jnp.concatenate([d_ffn0, d_ffn1], axis=0),
        kv_norm=d_kvnorm.reshape(-1),
        final_norm=d_final.reshape(-1),
        ffn_conv_b=jnp.concatenate([dcb0, dcb1], axis=0),
        gla_w_a2=dw_a2p[:GATE_RANK],
        gla_b_a2=db_a2,
        gla_head_norm=d_hn,
        ffn_conv_w=jnp.stack([dcw0, dcw1]),
    )
    return loss_tile, grad_x, small


SMALL_ORDER = ("attn_norm", "ffn_norm", "kv_norm", "final_norm", "ffn_conv_b",
               "gla_w_a2", "gla_b_a2", "gla_head_norm", "ffn_conv_w")
SMALL_FULL = dict(attn_norm=(2, D_MODEL), ffn_norm=(2, D_MODEL), kv_norm=(D_MODEL,), final_norm=(D_MODEL,),
                  ffn_conv_b=(2, D_FF), gla_w_a2=(GATE_RANK, GLA_KEY_DIM), gla_b_a2=(1, GLA_KEY_DIM),
                  gla_head_norm=(1, GLA_DV), ffn_conv_w=(2, 3, D_FF))
SMALL_SHARDED = ("gla_w_a2", "gla_b_a2", "gla_head_norm", "ffn_conv_w")
SMALL_GRAD_ROWS = 592
SMALL_ADAM_ROWS = 240


def kernel(x, attn_norm, gla_w_in, gla_w_a2, gla_b_a2, gla_head_norm, gla_w_out, kv_norm, w_kv, dsa_w_q, dsa_w_out, ffn_norm, ffn_w_up, ffn_conv_w, ffn_conv_b, ffn_w_down, final_norm, loss_target, m_attn_norm, m_gla_w_in, m_gla_w_a2, m_gla_b_a2, m_gla_head_norm, m_gla_w_out, m_kv_norm, m_w_kv, m_dsa_w_q, m_dsa_w_out, m_ffn_norm, m_ffn_w_up, m_ffn_conv_w, m_ffn_conv_b, m_ffn_w_down, m_final_norm, v_attn_norm, v_gla_w_in, v_gla_w_a2, v_gla_b_a2, v_gla_head_norm, v_gla_w_out, v_kv_norm, v_w_kv, v_dsa_w_q, v_dsa_w_out, v_ffn_norm, v_ffn_w_up, v_ffn_conv_w, v_ffn_conv_b, v_ffn_w_down, v_final_norm):
    me = 4 * lax.axis_index("x") + 2 * lax.axis_index("y") + lax.axis_index("c")
    bf = lambda a: a.astype(BF16)

    g_in, g_small = all_gather([bf(gla_w_in[0]), _pack_small_weights(gla_w_a2, gla_b_a2, gla_head_norm, ffn_conv_w)],
                               name="gather_first")
    w_a2_full, b_a2_full, hn_full, conv_w_full = _unpack_small_weights(g_small)
    w_in_full = jnp.pad(g_in.transpose(1, 0, 2).reshape(D_MODEL, GLA_IN_DIM), ((0, 0), (0, GLA_IN_PAD - GLA_IN_DIM)))
    wts = dict(
        attn_norm=attn_norm, ffn_norm=ffn_norm, kv_norm=kv_norm, final_norm=final_norm, ffn_conv_b=ffn_conv_b,
        gla_w_in=w_in_full[None],
        gla_w_a2=jnp.pad(bf(w_a2_full), ((0, LANE - GATE_RANK), (0, 0))),
        gla_b_a2=b_a2_full, gla_head_norm=hn_full, ffn_conv_w=conv_w_full,
    )
    plan = Plan({}, srcs=dict(gout=bf(gla_w_out[0]), kv=bf(w_kv), q=bf(dsa_w_q[0]), dout=bf(dsa_w_out[0]),
                              up0=bf(ffn_w_up[0]), up1=bf(ffn_w_up[1]), dn0=bf(ffn_w_down[0]), dn1=bf(ffn_w_down[1])))

    loss_tile, grad_x, small = local_step(x[0], loss_target[0], wts, plan)
    loss = lax.psum(loss_tile[0, 0], ("x", "y", "c"))

    plan.run("grads_tail", exchange_only, name="grads_tail")
    shard3 = lambda a: a.reshape((-1,) + a.shape[-2:])
    big_params = dict(gla_w_in=(("in",), gla_w_in, m_gla_w_in, v_gla_w_in),
                      gla_w_out=(("gout",), gla_w_out, m_gla_w_out, v_gla_w_out),
                      w_kv=(("kv",), w_kv, m_w_kv, v_w_kv),
                      dsa_w_q=(("q",), dsa_w_q, m_dsa_w_q, v_dsa_w_q),
                      dsa_w_out=(("dout",), dsa_w_out, m_dsa_w_out, v_dsa_w_out),
                      ffn_w_up=(("up0", "up1"), ffn_w_up, m_ffn_w_up, v_ffn_w_up),
                      ffn_w_down=(("dn0", "dn1"), ffn_w_down, m_ffn_w_down, v_ffn_w_down))
    res = {}
    for nm, (parts, w, m, v) in big_params.items():
        outs = adam_sharded([plan.recv[p] for p in parts], shard3(w), shard3(m), shard3(v), name=f"adam_{nm}")
        res[nm] = [o.reshape(w.shape) for o in outs]

    packed = _pack_rows([small[nm] for nm in SMALL_ORDER], SMALL_GRAD_ROWS)
    (parts,) = all_gather([packed], name="gather_small_grads")
    full = dict(zip(SMALL_ORDER, _unpack_rows(sum_partials(parts), [SMALL_FULL[nm] for nm in SMALL_ORDER])))
    local_w = dict(attn_norm=attn_norm, ffn_norm=ffn_norm, kv_norm=kv_norm, final_norm=final_norm,
                   ffn_conv_b=ffn_conv_b, gla_w_a2=gla_w_a2, gla_b_a2=gla_b_a2, gla_head_norm=gla_head_norm,
                   ffn_conv_w=ffn_conv_w)
    local_m = dict(attn_norm=m_attn_norm, ffn_norm=m_ffn_norm, kv_norm=m_kv_norm, final_norm=m_final_norm,
                   ffn_conv_b=m_ffn_conv_b, gla_w_a2=m_gla_w_a2, gla_b_a2=m_gla_b_a2, gla_head_norm=m_gla_head_norm,
                   ffn_conv_w=m_ffn_conv_w)
    local_v = dict(attn_norm=v_attn_norm, ffn_norm=v_ffn_norm, kv_norm=v_kv_norm, final_norm=v_final_norm,
                   ffn_conv_b=v_ffn_conv_b, gla_w_a2=v_gla_w_a2, gla_b_a2=v_gla_b_a2, gla_head_norm=v_gla_head_norm,
                   ffn_conv_w=v_ffn_conv_w)
    local_g = {}
    for nm in SMALL_ORDER:
        gfull = full[nm]
        if nm in SMALL_SHARDED:
            per = gfull.shape[-1] // N_DEV
            gfull = lax.dynamic_slice_in_dim(gfull, me * per, per, axis=gfull.ndim - 1)
        local_g[nm] = gfull.reshape(local_w[nm].shape)
    shapes = [local_w[nm].shape for nm in SMALL_ORDER]
    pk = lambda dd: _pack_rows([dd[nm] for nm in SMALL_ORDER], SMALL_ADAM_ROWS)
    d_p, m_p, v_p = adam_packed(pk(local_w), pk(local_g), pk(local_m), pk(local_v))
    for nm, dl, mn, vn in zip(SMALL_ORDER, _unpack_rows(d_p, shapes), _unpack_rows(m_p, shapes),
                              _unpack_rows(v_p, shapes)):
        res[nm] = [local_g[nm], dl, mn, vn]

    order = ("attn_norm", "gla_w_in", "gla_w_a2", "gla_b_a2", "gla_head_norm", "gla_w_out", "kv_norm", "w_kv",
             "dsa_w_q", "dsa_w_out", "ffn_norm", "ffn_w_up", "ffn_conv_w", "ffn_conv_b", "ffn_w_down", "final_norm")
    outs = [loss, grad_x[None]]
    for kind in range(4):
        outs.extend(res[nm][kind] for nm in order)
    return tuple(outs)
```

```python
import functools

import jax
import jax.numpy as jnp
from jax import lax
from jax.experimental import pallas as pl
from jax.experimental.pallas import tpu as pltpu

F32 = jnp.float32
BF16 = jnp.bfloat16
MESH = pl.DeviceIdType.MESH
ANY = pl.BlockSpec(memory_space=pl.ANY)

N_DEV = 8
D_MODEL = 2048
GLA_HEADS = 4
GLA_KEY_DIM = 1024
GLA_VAL_DIM = 2048
GLA_DK = 256
GLA_DV = 512
GATE_RANK = 16
GATE_NORMALIZER = 16.0
GLA_CHUNK = 64
GLA_IN_DIM = 2 * GLA_KEY_DIM + 2 * GLA_VAL_DIM + GATE_RANK
GLA_IN_PAD = 6272
ATT_HEADS = 16
HEAD_DIM = 128
DILATIONS = (1, 4, 16)
ATT_BLOCK = 128
D_FF = 5632
EPS = 1e-6
ADAM_LR = 0.001
ADAM_B1 = 0.9
ADAM_B2 = 0.999
ADAM_EPS = 1e-08
ADAM_WD = 0.01
ADAM_STEP = 10
NEG = -1e30
LANE = 128
VMEM_LIMIT = 52 * 1024 * 1024
ALIBI_SLOPES = tuple(2.0 ** (-0.5 * (i + 1)) for i in range(ATT_HEADS))


def _params(*sem):
    return pltpu.CompilerParams(dimension_semantics=sem, vmem_limit_bytes=VMEM_LIMIT)


def _tile(n, cap):
    best = None
    for t in range(LANE, min(n, cap) + 1, LANE):
        if n % t == 0:
            best = t
    return best if best is not None else n


def _shard_group(j, ns, cap):
    best = 1
    for g in range(1, j + 1):
        if j % g == 0 and g * ns <= cap:
            best = g
    return best


def _rows(r, c, budget=256 * 1024):
    best = None
    for t in range(16, r + 1, 16):
        if r % t == 0 and t * c <= budget:
            best = t
    return best if best is not None else r


def _flip(coord, bit):
    return 1 - coord if bit else coord


def _place():
    x, y, c = lax.axis_index("x"), lax.axis_index("y"), lax.axis_index("c")
    return x, y, c, 4 * x + 2 * y + c


def _rows_of(ref, rows):
    return ref if rows is None else ref.at[pl.ds(rows[0], rows[1] - rows[0])]


class Jobs:
    def __init__(self):
        self.srcs = []
        self.bufs = []
        self.sems = []
        self.steps = []

    def _src(self, a):
        for i, b in enumerate(self.srcs):
            if b is a:
                return i
        self.srcs.append(a)
        return len(self.srcs) - 1

    def new(self, shape, dtype):
        self.bufs.append((None, jax.ShapeDtypeStruct(shape, dtype)))
        return len(self.bufs) - 1

    def thru(self, a):
        self.bufs.append((a, jax.ShapeDtypeStruct(a.shape, a.dtype)))
        return len(self.bufs) - 1

    def _sem(self, n):
        self.sems.append(pltpu.SemaphoreType.DMA((n,)))
        return len(self.sems) - 1

    def gather_ici(self, src, buf, rows=None):
        si, send, recv, loc = self._src(src), self._sem(4), self._sem(4), self._sem(1)

        def remote(srcs, bufs, sems, slot_of):
            x, y, c, me = _place()
            peers = [(x, y, 1 - c), (1 - x, y, c), (x, 1 - y, c), (1 - x, 1 - y, c)]
            return [pltpu.make_async_remote_copy(
                src_ref=_rows_of(srcs[si], rows),
                dst_ref=_rows_of(bufs[buf].at[me if slot_of == "mine" else 4 * p[0] + 2 * p[1] + p[2]], rows),
                send_sem=sems[send].at[k], recv_sem=sems[recv].at[k], device_id=p, device_id_type=MESH)
                for k, p in enumerate(peers)]

        def local(srcs, bufs, sems):
            return pltpu.make_async_copy(_rows_of(srcs[si], rows), _rows_of(bufs[buf].at[_place()[3]], rows),
                                         sems[loc].at[0])

        def start(srcs, bufs, sems):
            local(srcs, bufs, sems).start()
            for cp in remote(srcs, bufs, sems, "mine"):
                cp.start()

        def finish(srcs, bufs, sems):
            for cp in remote(srcs, bufs, sems, "peer"):
                cp.wait_recv()
            for cp in remote(srcs, bufs, sems, "mine"):
                cp.wait_send()
            local(srcs, bufs, sems).wait()

        self.steps.append((start, finish))

    def gather_d2d(self, buf, rows=None):
        send, recv = self._sem(3), self._sem(3)

        def copies(bufs, sems, core):
            x, y, c, _ = _place()
            cc = c if core == "mine" else 1 - c
            chips = [(1 - x, y), (x, 1 - y), (1 - x, 1 - y)]
            return [pltpu.make_async_remote_copy(
                src_ref=_rows_of(bufs[buf].at[4 * px + 2 * py + cc], rows),
                dst_ref=_rows_of(bufs[buf].at[4 * px + 2 * py + cc], rows),
                send_sem=sems[send].at[k], recv_sem=sems[recv].at[k],
                device_id=(x, y, 1 - c), device_id_type=MESH) for k, (px, py) in enumerate(chips)]

        def start(srcs, bufs, sems):
            for cp in copies(bufs, sems, "mine"):
                cp.start()

        def finish(srcs, bufs, sems):
            for cp in copies(bufs, sems, "sibling"):
                cp.wait_recv()
            for cp in copies(bufs, sems, "mine"):
                cp.wait_send()

        self.steps.append((start, finish))

    def scatter(self, src, buf, rows=None):
        si, send, recv, loc = self._src(src), self._sem(N_DEV - 1), self._sem(N_DEV - 1), self._sem(1)

        def remote(srcs, bufs, sems, slot_of):
            x, y, c, me = _place()
            out = []
            for k in range(1, N_DEV):
                px, py, pc = _flip(x, k >> 2), _flip(y, (k >> 1) & 1), _flip(c, k & 1)
                peer = 4 * px + 2 * py + pc
                out.append(pltpu.make_async_remote_copy(
                    src_ref=_rows_of(srcs[si].at[peer], rows),
                    dst_ref=_rows_of(bufs[buf].at[me if slot_of == "mine" else peer], rows),
                    send_sem=sems[send].at[k - 1], recv_sem=sems[recv].at[k - 1],
                    device_id=(px, py, pc), device_id_type=MESH))
            return out

        def local(srcs, bufs, sems):
            me = _place()[3]
            return pltpu.make_async_copy(_rows_of(srcs[si].at[me], rows), _rows_of(bufs[buf].at[me], rows),
                                         sems[loc].at[0])

        def start(srcs, bufs, sems):
            local(srcs, bufs, sems).start()
            for cp in remote(srcs, bufs, sems, "mine"):
                cp.start()

        def finish(srcs, bufs, sems):
            for cp in remote(srcs, bufs, sems, "peer"):
                cp.wait_recv()
            for cp in remote(srcs, bufs, sems, "mine"):
                cp.wait_send()
            local(srcs, bufs, sems).wait()

        self.steps.append((start, finish))


def _call(body, *, name, grid, in_specs, out_specs, out_shape, args, sem, scratch_shapes=(), jobs=None):
    in_specs, out_specs, out_shape = list(in_specs), list(out_specs), list(out_shape)
    scratch_shapes = list(scratch_shapes)
    if jobs is None:
        res = pl.pallas_call(body, name=name, out_shape=out_shape, grid=grid, in_specs=in_specs,
                             out_specs=out_specs, scratch_shapes=scratch_shapes,
                             compiler_params=_params(*sem))(*args)
        return list(res), []
    thru = [a for a, _ in jobs.bufs if a is not None]
    n_in, n_src, n_thru = len(args), len(jobs.srcs), len(thru)
    n_out, n_buf, n_scr = len(out_shape), len(jobs.bufs), len(scratch_shapes)
    aliases, t = {}, 0
    for b, (a, _) in enumerate(jobs.bufs):
        if a is not None:
            aliases[n_in + n_src + t] = n_out + b
            t += 1

    def wrapped(*refs):
        at = 0
        ins = refs[at:at + n_in]; at += n_in
        srcs = refs[at:at + n_src]; at += n_src + n_thru
        outs = refs[at:at + n_out]; at += n_out
        bufs = refs[at:at + n_buf]; at += n_buf
        scr = refs[at:at + n_scr]; at += n_scr
        sems = refs[at:]
        first, last = None, None
        for axis, size in enumerate(grid):
            pid = pl.program_id(axis)
            f, l = pid == 0, pid == size - 1
            first = f if first is None else first & f
            last = l if last is None else last & l

        @pl.when(first)
        def _():
            for start, _ in jobs.steps:
                start(srcs, bufs, sems)

        body(*ins, *outs, *scr)

        @pl.when(last)
        def _():
            for _, finish in jobs.steps:
                finish(srcs, bufs, sems)

    res = pl.pallas_call(
        wrapped, name=name,
        out_shape=out_shape + [s for _, s in jobs.bufs],
        grid=grid,
        in_specs=in_specs + [ANY] * (n_src + n_thru),
        out_specs=out_specs + [ANY] * n_buf,
        scratch_shapes=scratch_shapes + jobs.sems,
        input_output_aliases=aliases,
        compiler_params=_params(*(["arbitrary"] * len(grid))),
    )(*args, *jobs.srcs, *thru)
    return res[:n_out], res[n_out:]


def mm_nn(a, w, *, out_dtype, name, res=None, tm=None, jobs=None):
    m, k = a.shape
    tm = tm or (1024 if a.dtype == BF16 else 512)
    j, k2, ns = w.shape
    assert k == k2 and m % tm == 0
    tn = _tile(ns, 1408)
    nsub = ns // tn
    tk = k if k <= 2048 else _tile(k, 1408)
    nk = k // tk
    has_res = res is not None

    def body(*refs):
        if has_res:
            a_ref, w_ref, r_ref, o_ref, acc = refs
        else:
            a_ref, w_ref, o_ref, acc = refs
        kk = pl.program_id(2)

        @pl.when(kk == 0)
        def _():
            acc[...] = jnp.zeros_like(acc)

        acc[...] += jnp.dot(a_ref[...].astype(BF16), w_ref[...], preferred_element_type=F32)

        @pl.when(kk == nk - 1)
        def _():
            r = acc[...]
            if has_res:
                r = r + r_ref[...]
            o_ref[...] = r.astype(out_dtype)

    in_specs = [
        pl.BlockSpec((tm, tk), lambda i, n, kk: (i, kk)),
        pl.BlockSpec((None, tk, tn), lambda i, n, kk: (n // nsub, kk, n % nsub)),
    ]
    args = [a, w]
    if has_res:
        in_specs.append(pl.BlockSpec((tm, tn), lambda i, n, kk: (i, n)))
        args.append(res)
    (out,), bufs = _call(
        body, name=name, jobs=jobs,
        out_shape=[jax.ShapeDtypeStruct((m, j * ns), out_dtype)],
        grid=(m // tm, j * nsub, nk),
        in_specs=in_specs,
        out_specs=[pl.BlockSpec((tm, tn), lambda i, n, kk: (i, n))],
        scratch_shapes=[pltpu.VMEM((tm, tn), F32)],
        args=args, sem=("parallel", "parallel", "arbitrary"))
    return out if jobs is None else (out, bufs)


def mm_nt(dy, w, *, out_dtype, name, tm=1024, jobs=None):
    parts, m, n = (1,) + dy.shape if dy.ndim == 2 else dy.shape
    n *= parts
    j, k, ns = w.shape
    assert n == j * ns and m % tm == 0
    tn = _tile(ns, 2048)
    nsub = ns // tn
    jb = _shard_group(j // parts, ns, 2048) if nsub == 1 else 1
    tko = _tile(k, 1408)
    nn = j * nsub // jb
    per_part = nn // parts
    if dy.ndim == 2:
        dy_spec = pl.BlockSpec((tm, jb * tn), lambda i, ko, nq: (i, nq))
    else:
        dy_spec = pl.BlockSpec((None, tm, jb * tn), lambda i, ko, nq: (nq // per_part, i, nq % per_part))
    if jb == 1:
        w_spec = pl.BlockSpec((None, tko, tn), lambda i, ko, nq: (nq // nsub, ko, nq % nsub))
    else:
        w_spec = pl.BlockSpec((jb, tko, ns), lambda i, ko, nq: (nq, ko, 0))

    def body(a_ref, w_ref, o_ref, acc):
        nq = pl.program_id(2)

        @pl.when(nq == 0)
        def _():
            acc[...] = jnp.zeros_like(acc)

        if jb == 1:
            acc[...] += lax.dot_general(a_ref[...].astype(BF16), w_ref[...], (((1,), (1,)), ((), ())),
                                        preferred_element_type=F32)
        else:
            part = acc[...]
            for jj in range(jb):
                part = part + lax.dot_general(a_ref[:, jj * ns:(jj + 1) * ns].astype(BF16), w_ref[jj],
                                              (((1,), (1,)), ((), ())), preferred_element_type=F32)
            acc[...] = part

        @pl.when(nq == nn - 1)
        def _():
            o_ref[...] = acc[...].astype(out_dtype)

    (out,), bufs = _call(
        body, name=name, jobs=jobs,
        out_shape=[jax.ShapeDtypeStruct((m, k), out_dtype)],
        grid=(m // tm, k // tko, nn),
        in_specs=[dy_spec, w_spec],
        out_specs=[pl.BlockSpec((tm, tko), lambda i, ko, nq: (i, ko))],
        scratch_shapes=[pltpu.VMEM((tm, tko), F32)],
        args=[dy, w], sem=("parallel", "parallel", "arbitrary"))
    return out if jobs is None else (out, bufs)


def mm_tn(x, dy, j, *, name, tm=1024, jobs=None):
    m, k = x.shape
    parts, m2, n = (1,) + dy.shape if dy.ndim == 2 else dy.shape
    n *= parts
    assert m == m2 and n % j == 0 and m % tm == 0
    ns = n // j
    tn = _tile(ns, 1408)
    nsub = ns // tn
    jb = _shard_group(j // parts, ns, 1536) if nsub == 1 else 1
    tk = _tile(k, 1408)
    nm = m // tm
    n_steps = j * nsub // jb
    per_part = n_steps // parts
    if dy.ndim == 2:
        dy_spec = pl.BlockSpec((tm, jb * tn), lambda kq, nq, mi: (mi, nq))
    else:
        dy_spec = pl.BlockSpec((None, tm, jb * tn), lambda kq, nq, mi: (nq // per_part, mi, nq % per_part))
    if jb == 1:
        out_spec = pl.BlockSpec((None, tk, tn), lambda kq, nq, mi: (nq // nsub, kq, nq % nsub))
        acc_shape = (tk, tn)
    else:
        out_spec = pl.BlockSpec((jb, tk, ns), lambda kq, nq, mi: (nq, kq, 0))
        acc_shape = (jb, tk, ns)

    def body(x_ref, dy_ref, o_ref, acc):
        mi = pl.program_id(2)

        @pl.when(mi == 0)
        def _():
            acc[...] = jnp.zeros_like(acc)

        xb = x_ref[...].astype(BF16)
        if jb == 1:
            acc[...] += lax.dot_general(xb, dy_ref[...].astype(BF16), (((0,), (0,)), ((), ())),
                                        preferred_element_type=F32)
        else:
            for jj in range(jb):
                acc[jj] += lax.dot_general(xb, dy_ref[:, jj * ns:(jj + 1) * ns].astype(BF16),
                                           (((0,), (0,)), ((), ())), preferred_element_type=F32)

        @pl.when(mi == nm - 1)
        def _():
            o_ref[...] = acc[...].astype(BF16)

    (out,), bufs = _call(
        body, name=name, jobs=jobs,
        out_shape=[jax.ShapeDtypeStruct((j, k, ns), BF16)],
        grid=(k // tk, n_steps, nm),
        in_specs=[
            pl.BlockSpec((tm, tk), lambda kq, nq, mi: (mi, kq)),
            dy_spec,
        ],
        out_specs=[out_spec],
        scratch_shapes=[pltpu.VMEM(acc_shape, F32)],
        args=[x, dy], sem=("parallel", "parallel", "arbitrary"))
    return out if jobs is None else (out, bufs)


def rms_fwd(x, gains, *, name, ts=512):
    s, d = x.shape
    n = len(gains)

    def body(x_ref, *refs):
        xv = x_ref[...]
        xh = xv * lax.rsqrt(jnp.mean(xv * xv, axis=-1, keepdims=True) + EPS)
        for g_ref, o_ref in zip(refs[:n], refs[n:]):
            o_ref[...] = (xh * g_ref[...]).astype(BF16)

    row = pl.BlockSpec((ts, d), lambda i: (i, 0))
    vec = pl.BlockSpec((1, d), lambda i: (0, 0))
    return pl.pallas_call(
        body,
        name=name,
        out_shape=[jax.ShapeDtypeStruct((s, d), BF16)] * n,
        grid=(s // ts,),
        in_specs=[row] + [vec] * n,
        out_specs=[row] * n,
        compiler_params=_params("parallel"),
    )(x, *gains)


def rms_bwd(x, gains, dys, dres, *, name, ts=256):
    s, d = x.shape
    n = len(gains)

    def body(x_ref, r_ref, *refs):
        g_refs, dy_refs = refs[:n], refs[n:2 * n]
        dx_ref, dg_refs = refs[2 * n], refs[2 * n + 1:]
        i = pl.program_id(0)
        xv = x_ref[...]
        r = lax.rsqrt(jnp.mean(xv * xv, axis=-1, keepdims=True) + EPS)
        xh = xv * r
        acc = r_ref[...]
        for g_ref, dy_ref, dg_ref in zip(g_refs, dy_refs, dg_refs):
            dy = dy_ref[...].astype(F32)

            @pl.when(i == 0)
            def _():
                dg_ref[...] = jnp.zeros_like(dg_ref)

            dg_ref[...] += jnp.sum(dy * xh, axis=0, keepdims=True)
            dxh = dy * g_ref[...]
            acc = acc + r * (dxh - xh * jnp.mean(dxh * xh, axis=-1, keepdims=True))
        dx_ref[...] = acc

    row = pl.BlockSpec((ts, d), lambda i: (i, 0))
    vec = pl.BlockSpec((1, d), lambda i: (0, 0))
    outs = pl.pallas_call(
        body,
        name=name,
        out_shape=[jax.ShapeDtypeStruct((s, d), F32)] + [jax.ShapeDtypeStruct((1, d), F32)] * n,
        grid=(s // ts,),
        in_specs=[row, row] + [vec] * n + [row] * n,
        out_specs=[row] + [vec] * n,
        compiler_params=_params("arbitrary"),
    )(x, dres, *gains, *dys)
    return outs[0], outs[1:]


def loss_head(h, gain, target, *, ts=256):
    s, d = h.shape

    def body(h_ref, g_ref, t_ref, l_ref, dh_ref, dg_ref):
        i = pl.program_id(0)

        @pl.when(i == 0)
        def _():
            l_ref[...] = jnp.zeros_like(l_ref)
            dg_ref[...] = jnp.zeros_like(dg_ref)

        xv = h_ref[...]
        r = lax.rsqrt(jnp.mean(xv * xv, axis=-1, keepdims=True) + EPS)
        xh = xv * r
        g = g_ref[...]
        err = xh * g - t_ref[...]
        l_ref[...] += 0.5 * jnp.sum(jnp.mean(err * err, axis=-1, keepdims=True))
        dy = err * (1.0 / d)
        dg_ref[...] += jnp.sum(dy * xh, axis=0, keepdims=True)
        dxh = dy * g
        dh_ref[...] = r * (dxh - xh * jnp.mean(dxh * xh, axis=-1, keepdims=True))

    row = pl.BlockSpec((ts, d), lambda i: (i, 0))
    vec = pl.BlockSpec((1, d), lambda i: (0, 0))
    return pl.pallas_call(
        body,
        name="loss_head",
        out_shape=[jax.ShapeDtypeStruct((8, LANE), F32), jax.ShapeDtypeStruct((s, d), F32),
                   jax.ShapeDtypeStruct((1, d), F32)],
        grid=(s // ts,),
        in_specs=[row, vec, row],
        out_specs=[pl.BlockSpec((8, LANE), lambda i: (0, 0)), row, vec],
        compiler_params=_params("arbitrary"),
    )(h, gain, target)


A_BLOCK = (2 * GLA_KEY_DIM + 2 * GLA_VAL_DIM) // LANE


def gate_fwd(proj, w_a2p, b_a2, *, ts=512):
    s = proj.shape[0]

    def body(a_ref, w_ref, b_ref, o_ref):
        z = jnp.dot(a_ref[...].astype(BF16), w_ref[...], preferred_element_type=F32) + b_ref[...]
        o_ref[...] = (jnp.minimum(z, 0.0) - jnp.log(1.0 + jnp.exp(-jnp.abs(z)))) * (1.0 / GATE_NORMALIZER)

    return pl.pallas_call(
        body,
        name="gate_fwd",
        out_shape=jax.ShapeDtypeStruct((s, GLA_KEY_DIM), F32),
        grid=(s // ts,),
        in_specs=[pl.BlockSpec((ts, LANE), lambda i: (i, A_BLOCK)),
                  pl.BlockSpec((LANE, GLA_KEY_DIM), lambda i: (0, 0)),
                  pl.BlockSpec((1, GLA_KEY_DIM), lambda i: (0, 0))],
        out_specs=pl.BlockSpec((ts, GLA_KEY_DIM), lambda i: (i, 0)),
        compiler_params=_params("parallel"),
    )(proj, w_a2p, b_a2)


def gate_bwd(proj, w_a2p, b_a2, dla, *, ts=512):
    s = proj.shape[0]

    def body(a_ref, w_ref, b_ref, dla_ref, da_ref, dw_ref, db_ref):
        i = pl.program_id(0)

        @pl.when(i == 0)
        def _():
            dw_ref[...] = jnp.zeros_like(dw_ref)
            db_ref[...] = jnp.zeros_like(db_ref)

        a = a_ref[...].astype(BF16)
        w = w_ref[...]
        z = jnp.dot(a, w, preferred_element_type=F32) + b_ref[...]
        dz = dla_ref[...] * (1.0 / GATE_NORMALIZER) / (1.0 + jnp.exp(z))
        dzb = dz.astype(BF16)
        da_ref[...] = lax.dot_general(dzb, w, (((1,), (1,)), ((), ())), preferred_element_type=F32).astype(BF16)
        dw_ref[...] += lax.dot_general(a, dzb, (((0,), (0,)), ((), ())), preferred_element_type=F32)
        db_ref[...] += jnp.sum(dz, axis=0, keepdims=True)

    return pl.pallas_call(
        body,
        name="gate_bwd",
        out_shape=[jax.ShapeDtypeStruct((s, LANE), BF16), jax.ShapeDtypeStruct((LANE, GLA_KEY_DIM), F32),
                   jax.ShapeDtypeStruct((1, GLA_KEY_DIM), F32)],
        grid=(s // ts,),
        in_specs=[pl.BlockSpec((ts, LANE), lambda i: (i, A_BLOCK)),
                  pl.BlockSpec((LANE, GLA_KEY_DIM), lambda i: (0, 0)),
                  pl.BlockSpec((1, GLA_KEY_DIM), lambda i: (0, 0)),
                  pl.BlockSpec((ts, GLA_KEY_DIM), lambda i: (i, 0))],
        out_specs=[pl.BlockSpec((ts, LANE), lambda i: (i, 0)),
                   pl.BlockSpec((LANE, GLA_KEY_DIM), lambda i: (0, 0)),
                   pl.BlockSpec((1, GLA_KEY_DIM), lambda i: (0, 0))],
        compiler_params=_params("arbitrary"),
    )(proj, w_a2p, b_a2, dla)


def _chunk_terms(q, k, la):
    c_len = GLA_CHUNK
    row = lax.broadcasted_iota(jnp.int32, (c_len, c_len), 0)
    col = lax.broadcasted_iota(jnp.int32, (c_len, c_len), 1)
    tri = row >= col
    c = jnp.dot(tri.astype(F32), la, preferred_element_type=F32, precision=lax.Precision.HIGHEST)
    last = jnp.sum(la, axis=0, keepdims=True)
    q_dec = q * (GLA_DK ** -0.5) * jnp.exp(c)
    k_inv = k * jnp.exp(-c)
    k_end = k * jnp.exp(last - c)
    return c, last, q_dec, k_inv, k_end, tri


def _dot(a, b, ca, cb):
    return lax.dot_general(a.astype(BF16), b.astype(BF16), (((ca,), (cb,)), ((), ())), preferred_element_type=F32)


def gla_fwd(proj, la, jobs=None):
    s = proj.shape[0]
    n_chunks = s // GLA_CHUNK

    def body(q_ref, k_ref, v_ref, la_ref, o_ref, st_out, st):
        @pl.when(pl.program_id(0) == 0)
        def _():
            st[...] = jnp.zeros_like(st)

        for h in range(GLA_HEADS):
            hk = slice(h * GLA_DK, (h + 1) * GLA_DK)
            hv = slice(h * GLA_DV, (h + 1) * GLA_DV)
            _, last, q_dec, k_inv, k_end, tri = _chunk_terms(q_ref[:, hk], k_ref[:, hk], la_ref[:, hk])
            v = v_ref[:, hv]
            a = jnp.where(tri, _dot(q_dec, k_inv, 1, 1), 0.0)
            state = st[h]
            st_out[h] = state
            o_ref[:, hv] = _dot(a, v, 1, 0) + _dot(q_dec, state, 1, 1)
            st[h] = state * jnp.exp(last) + _dot(v, k_end, 0, 0)

    key = lambda col: pl.BlockSpec((GLA_CHUNK, GLA_KEY_DIM), lambda n: (n, col))
    outs, bufs = _call(
        body, name="gla_fwd", jobs=jobs,
        out_shape=[jax.ShapeDtypeStruct((s, GLA_VAL_DIM), F32),
                   jax.ShapeDtypeStruct((GLA_HEADS, n_chunks, GLA_DV, GLA_DK), F32)],
        grid=(n_chunks,),
        in_specs=[key(0), key(1), pl.BlockSpec((GLA_CHUNK, GLA_VAL_DIM), lambda n: (n, 1)), key(0)],
        out_specs=[pl.BlockSpec((GLA_CHUNK, GLA_VAL_DIM), lambda n: (n, 0)),
                   pl.BlockSpec((GLA_HEADS, None, GLA_DV, GLA_DK), lambda n: (0, n, 0, 0))],
        scratch_shapes=[pltpu.VMEM((GLA_HEADS, GLA_DV, GLA_DK), F32)],
        args=[proj, proj, proj, la], sem=("arbitrary",))
    return outs if jobs is None else (outs, bufs)


def gla_bwd(proj, la, states, do, jobs=None):
    s = proj.shape[0]
    n_chunks = s // GLA_CHUNK
    lastc = n_chunks - 1

    def body(q_ref, k_ref, v_ref, la_ref, do_ref, st_ref, dq_ref, dk_ref, dv_ref, dla_ref, dst):
        @pl.when(pl.program_id(0) == 0)
        def _():
            dst[...] = jnp.zeros_like(dst)

        upper = (lax.broadcasted_iota(jnp.int32, (GLA_CHUNK, GLA_CHUNK), 0)
                 <= lax.broadcasted_iota(jnp.int32, (GLA_CHUNK, GLA_CHUNK), 1)).astype(F32)
        for h in range(GLA_HEADS):
            hk = slice(h * GLA_DK, (h + 1) * GLA_DK)
            hv = slice(h * GLA_DV, (h + 1) * GLA_DV)
            c, last, q_dec, k_inv, k_end, tri = _chunk_terms(q_ref[:, hk], k_ref[:, hk], la_ref[:, hk])
            v = v_ref[:, hv]
            dout = do_ref[:, hv]
            state = st_ref[h]
            dstate = dst[h]
            e_last = jnp.exp(last)
            a = jnp.where(tri, _dot(q_dec, k_inv, 1, 1), 0.0)
            da = jnp.where(tri, _dot(dout, v, 1, 1), 0.0)
            dv_ref[:, hv] = (_dot(a, dout, 0, 0) + _dot(k_end, dstate, 1, 1)).astype(BF16)
            dq_dec = _dot(da, k_inv, 1, 0) + _dot(dout, state, 1, 0)
            dk_inv = _dot(da, q_dec, 0, 0)
            dk_end = _dot(v, dstate, 1, 0)
            dst[h] = dstate * e_last + _dot(dout, q_dec, 0, 0)
            dq_ref[:, hk] = (dq_dec * (GLA_DK ** -0.5) * jnp.exp(c)).astype(BF16)
            dk_ref[:, hk] = (dk_inv * jnp.exp(-c) + dk_end * jnp.exp(last - c)).astype(BF16)
            ke_term = dk_end * k_end
            dc = dq_dec * q_dec - dk_inv * k_inv - ke_term
            dlast = (jnp.sum(ke_term, axis=0, keepdims=True)
                     + e_last * jnp.sum(dstate * state, axis=0, keepdims=True))
            dla_ref[:, hk] = jnp.dot(upper, dc, preferred_element_type=F32,
                                     precision=lax.Precision.HIGHEST) + dlast

    key = lambda col: pl.BlockSpec((GLA_CHUNK, GLA_KEY_DIM), lambda n: (lastc - n, col))
    val = lambda col: pl.BlockSpec((GLA_CHUNK, GLA_VAL_DIM), lambda n: (lastc - n, col))
    outs, bufs = _call(
        body, name="gla_bwd", jobs=jobs,
        out_shape=[jax.ShapeDtypeStruct((s, GLA_KEY_DIM), BF16), jax.ShapeDtypeStruct((s, GLA_KEY_DIM), BF16),
                   jax.ShapeDtypeStruct((s, GLA_VAL_DIM), BF16), jax.ShapeDtypeStruct((s, GLA_KEY_DIM), F32)],
        grid=(n_chunks,),
        in_specs=[key(0), key(1), val(1), key(0), val(0),
                  pl.BlockSpec((GLA_HEADS, None, GLA_DV, GLA_DK), lambda n: (0, lastc - n, 0, 0))],
        out_specs=[key(0), key(0), val(0), key(0)],
        scratch_shapes=[pltpu.VMEM((GLA_HEADS, GLA_DV, GLA_DK), F32)],
        args=[proj, proj, proj, la, do, states], sem=("arbitrary",))
    return outs if jobs is None else (outs, bufs)


R_BLOCK = (2 * GLA_KEY_DIM + GLA_VAL_DIM) // GLA_DV


def headnorm_fwd(o, proj, hn, *, ts=512):
    s = o.shape[0]

    def body(o_ref, r_ref, g_ref, out_ref):
        ov = o_ref[...]
        oh = ov * lax.rsqrt(jnp.mean(ov * ov, axis=-1, keepdims=True) + EPS)
        r = r_ref[...]
        out_ref[...] = (oh * g_ref[...] * (r * jax.nn.sigmoid(r))).astype(BF16)

    return pl.pallas_call(
        body,
        name="headnorm_fwd",
        out_shape=jax.ShapeDtypeStruct((s, GLA_VAL_DIM), BF16),
        grid=(s // ts, GLA_HEADS),
        in_specs=[pl.BlockSpec((ts, GLA_DV), lambda i, h: (i, h)),
                  pl.BlockSpec((ts, GLA_DV), lambda i, h: (i, R_BLOCK + h)),
                  pl.BlockSpec((1, GLA_DV), lambda i, h: (0, 0))],
        out_specs=pl.BlockSpec((ts, GLA_DV), lambda i, h: (i, h)),
        compiler_params=_params("parallel", "parallel"),
    )(o, proj, hn)


def headnorm_bwd(o, proj, hn, dog, *, ts=512):
    s = o.shape[0]

    def body(o_ref, r_ref, g_ref, dog_ref, do_ref, dr_ref, dg_ref):
        @pl.when((pl.program_id(0) == 0) & (pl.program_id(1) == 0))
        def _():
            dg_ref[...] = jnp.zeros_like(dg_ref)

        ov = o_ref[...]
        rr = lax.rsqrt(jnp.mean(ov * ov, axis=-1, keepdims=True) + EPS)
        oh = ov * rr
        g = g_ref[...]
        r = r_ref[...]
        sig = jax.nn.sigmoid(r)
        gate = r * sig
        dog_v = dog_ref[...]
        d_on = dog_v * gate
        dr_ref[...] = (dog_v * (oh * g) * (sig * (1.0 + r * (1.0 - sig)))).astype(BF16)
        dg_ref[...] += jnp.sum(d_on * oh, axis=0, keepdims=True)
        doh = d_on * g
        do_ref[...] = rr * (doh - oh * jnp.mean(doh * oh, axis=-1, keepdims=True))

    return pl.pallas_call(
        body,
        name="headnorm_bwd",
        out_shape=[jax.ShapeDtypeStruct((s, GLA_VAL_DIM), F32), jax.ShapeDtypeStruct((s, GLA_VAL_DIM), BF16),
                   jax.ShapeDtypeStruct((1, GLA_DV), F32)],
        grid=(s // ts, GLA_HEADS),
        in_specs=[pl.BlockSpec((ts, GLA_DV), lambda i, h: (i, h)),
                  pl.BlockSpec((ts, GLA_DV), lambda i, h: (i, R_BLOCK + h)),
                  pl.BlockSpec((1, GLA_DV), lambda i, h: (0, 0)),
                  pl.BlockSpec((ts, GLA_DV), lambda i, h: (i, h))],
        out_specs=[pl.BlockSpec((ts, GLA_DV), lambda i, h: (i, h)),
                   pl.BlockSpec((ts, GLA_DV), lambda i, h: (i, h)),
                   pl.BlockSpec((1, GLA_DV), lambda i, h: (0, 0))],
        compiler_params=_params("arbitrary", "arbitrary"),
    )(o, proj, hn, dog)


CONV_TC = 128
SQRT_HALF = 0.7071067811865476
INV_SQRT_2PI = 0.3989422804014327


def _conv_gate(g_ref, cw_ref, cb_ref):
    g0 = g_ref[...].astype(F32)
    t = lax.broadcasted_iota(jnp.int32, g0.shape, 0)
    g1 = jnp.where(t >= 1, pltpu.roll(g0, 1, 0), 0.0)
    g2 = jnp.where(t >= 2, pltpu.roll(g0, 2, 0), 0.0)
    gc = cw_ref[0:1, :] * g2 + cw_ref[1:2, :] * g1 + cw_ref[2:3, :] * g0 + cb_ref[...]
    return g0, g1, g2, gc, t


def convglu_fwd(up, conv_w, conv_b, *, name, jobs=None):
    s = up.shape[0]
    nc = D_FF // CONV_TC

    def body(u_ref, g_ref, cw_ref, cb_ref, o_ref):
        _, _, _, gc, _ = _conv_gate(g_ref, cw_ref, cb_ref)
        gelu = 0.5 * gc * (1.0 + lax.erf(gc * SQRT_HALF))
        o_ref[...] = (gelu * u_ref[...].astype(F32)).astype(BF16)

    (out,), bufs = _call(
        body, name=name, jobs=jobs,
        out_shape=[jax.ShapeDtypeStruct((s, D_FF), BF16)],
        grid=(nc,),
        in_specs=[pl.BlockSpec((s, CONV_TC), lambda c: (0, c)),
                  pl.BlockSpec((s, CONV_TC), lambda c: (0, nc + c)),
                  pl.BlockSpec((3, CONV_TC), lambda c: (0, c)),
                  pl.BlockSpec((1, CONV_TC), lambda c: (0, c))],
        out_specs=[pl.BlockSpec((s, CONV_TC), lambda c: (0, c))],
        args=[up, up, conv_w, conv_b], sem=("parallel",))
    return out if jobs is None else (out, bufs)


def convglu_bwd(up, conv_w, conv_b, dact, *, name, jobs=None):
    s = up.shape[0]
    nc = D_FF // CONV_TC

    def body(u_ref, g_ref, cw_ref, cb_ref, da_ref, dup_ref, dcw_ref, dcb_ref):
        du_ref, dg_ref = dup_ref.at[0], dup_ref.at[1]
        g0, g1, g2, gc, t = _conv_gate(g_ref, cw_ref, cb_ref)
        cdf = 0.5 * (1.0 + lax.erf(gc * SQRT_HALF))
        da = da_ref[...].astype(F32)
        du_ref[...] = (da * gc * cdf).astype(BF16)
        dgc = da * u_ref[...].astype(F32) * (cdf + gc * jnp.exp(-0.5 * gc * gc) * INV_SQRT_2PI)
        dcb_ref[...] = jnp.sum(dgc, axis=0, keepdims=True)
        dcw_ref[0:1, :] = jnp.sum(dgc * g2, axis=0, keepdims=True)
        dcw_ref[1:2, :] = jnp.sum(dgc * g1, axis=0, keepdims=True)
        dcw_ref[2:3, :] = jnp.sum(dgc * g0, axis=0, keepdims=True)
        n1 = jnp.where(t < s - 1, pltpu.roll(dgc, s - 1, 0), 0.0)
        n2 = jnp.where(t < s - 2, pltpu.roll(dgc, s - 2, 0), 0.0)
        dg_ref[...] = (cw_ref[2:3, :] * dgc + cw_ref[1:2, :] * n1 + cw_ref[0:1, :] * n2).astype(BF16)

    col = pl.BlockSpec((s, CONV_TC), lambda c: (0, c))
    outs, bufs = _call(
        body, name=name, jobs=jobs,
        out_shape=[jax.ShapeDtypeStruct((2, s, D_FF), BF16),
                   jax.ShapeDtypeStruct((3, D_FF), F32), jax.ShapeDtypeStruct((1, D_FF), F32)],
        grid=(nc,),
        in_specs=[col, pl.BlockSpec((s, CONV_TC), lambda c: (0, nc + c)),
                  pl.BlockSpec((3, CONV_TC), lambda c: (0, c)),
                  pl.BlockSpec((1, CONV_TC), lambda c: (0, c)), col],
        out_specs=[pl.BlockSpec((2, s, CONV_TC), lambda c: (0, 0, c)), pl.BlockSpec((3, CONV_TC), lambda c: (0, c)),
                   pl.BlockSpec((1, CONV_TC), lambda c: (0, c))],
        args=[up, up, conv_w, conv_b, dact], sem=("parallel",))
    return outs if jobs is None else (outs, bufs)


REGION = ATT_BLOCK * DILATIONS[-1]
SLOPE_TILE = (8, LANE)


def _slope_table():
    return jnp.broadcast_to(jnp.asarray(ALIBI_SLOPES, F32)[:, None, None], (ATT_HEADS,) + SLOPE_TILE)


def _sub(r, i, d):
    start = r + d * ATT_BLOCK * i
    return pl.ds(start, ATT_BLOCK) if d == 1 else pl.ds(start, ATT_BLOCK, stride=d)


def _att_bias(slope, d, first_key):
    qa = lax.broadcasted_iota(jnp.int32, (ATT_BLOCK, 2 * ATT_BLOCK), 0)
    cc = lax.broadcasted_iota(jnp.int32, (ATT_BLOCK, 2 * ATT_BLOCK), 1)
    dist = qa - cc + ATT_BLOCK
    ok = (dist >= 0) & (dist <= ATT_BLOCK) & (cc >= first_key)
    return jnp.where(ok, (slope * (-float(d))) * dist.astype(F32), NEG)


def _keys(kc_ref, kp_ref, r, i, d, nsub):
    prev = kp_ref[_sub(r, nsub - 1, d), :] if i == 0 else kc_ref[_sub(r, i - 1, d), :]
    return jnp.concatenate([prev, kc_ref[_sub(r, i, d), :]], axis=0)


def _per_residue(d, body):
    for r in range(d):
        body(r)


def attn_fwd(q, kv, jobs=None):
    s = q.shape[0]
    nreg = s // REGION
    scale = HEAD_DIM ** -0.5

    def body(sl_ref, q_ref, kc_ref, kp_ref, vc_ref, vp_ref, o_ref, lse_ref, ob, lb):
        n = pl.program_id(0)
        g = pl.program_id(2)
        slope = sl_ref[0:1, 0:1]
        first_key = jnp.where(n > 0, 0, ATT_BLOCK)

        def branch(gi, d):
            nsub = REGION // (ATT_BLOCK * d)
            bias = _att_bias(slope, d, 0)
            bias0 = _att_bias(slope, d, first_key)

            def residue(r):
                for i in range(nsub):
                    rows = _sub(r, i, d)
                    kcat = _keys(kc_ref, kp_ref, r, i, d, nsub)
                    vcat = _keys(vc_ref, vp_ref, r, i, d, nsub)
                    sc = _dot(q_ref[rows, :], kcat, 1, 1) * scale + (bias0 if i == 0 else bias)
                    m = jnp.max(sc, axis=-1, keepdims=True)
                    p = jnp.exp(sc - m)
                    l = jnp.sum(p, axis=-1, keepdims=True)
                    ob.at[gi][rows, :] = _dot(p, vcat, 1, 0) / l
                    lb.at[gi][rows, :] = jnp.broadcast_to(m + jnp.log(l), (ATT_BLOCK, HEAD_DIM))

            _per_residue(d, residue)

        for gi, d in enumerate(DILATIONS):
            @pl.when(g == gi)
            def _():
                branch(gi, d)

        @pl.when(g == len(DILATIONS) - 1)
        def _():
            def merge(c, carry):
                rows = pl.ds(pl.multiple_of(c * ATT_BLOCK, ATT_BLOCK), ATT_BLOCK)
                l0, l1, l2 = lb[0, rows, :], lb[1, rows, :], lb[2, rows, :]
                m = jnp.maximum(jnp.maximum(l0, l1), l2)
                e0, e1, e2 = jnp.exp(l0 - m), jnp.exp(l1 - m), jnp.exp(l2 - m)
                den = e0 + e1 + e2
                o_ref[rows, :] = (e0 * ob[0, rows, :] + e1 * ob[1, rows, :] + e2 * ob[2, rows, :]) / den
                lse_ref[rows, :] = m + jnp.log(den)
                return carry
            lax.fori_loop(0, REGION // ATT_BLOCK, merge, 0)

    def blk(col, prev=False):
        if prev:
            return pl.BlockSpec((REGION, HEAD_DIM), lambda n, h, g: (jnp.maximum(n - 1, 0), col(h, g)))
        return pl.BlockSpec((REGION, HEAD_DIM), lambda n, h, g: (n, col(h, g)))

    k_col = lambda h, g: h
    v_col = lambda h, g: ATT_HEADS + h
    outs, bufs = _call(
        body, name="attn_fwd", jobs=jobs,
        out_shape=[jax.ShapeDtypeStruct((s, ATT_HEADS * HEAD_DIM), F32)] * 2,
        grid=(nreg, ATT_HEADS, len(DILATIONS)),
        in_specs=[pl.BlockSpec((None,) + SLOPE_TILE, lambda n, h, g: (h, 0, 0)),
                  blk(lambda h, g: g * ATT_HEADS + h), blk(k_col), blk(k_col, True), blk(v_col), blk(v_col, True)],
        out_specs=[blk(k_col), blk(k_col)],
        scratch_shapes=[pltpu.VMEM((len(DILATIONS), REGION, HEAD_DIM), F32)] * 2,
        args=[_slope_table(), q, kv, kv, kv, kv], sem=("parallel", "parallel", "arbitrary"))
    return outs if jobs is None else (outs, bufs)


def attn_bwd(q, kv, o, lse, do, jobs=None):
    s = q.shape[0]
    nreg = s // REGION
    scale = HEAD_DIM ** -0.5

    def body(sl_ref, q_ref, kc_ref, kp_ref, vc_ref, vp_ref, o_ref, lse_ref, do_ref,
             qn_ref, on_ref, lsen_ref, don_ref, dq_ref, dkv_ref, dlt, dltn):
        n = pl.program_id(0)
        g = pl.program_id(2)
        slope = sl_ref[0:1, 0:1]
        first_key = jnp.where(n > 0, 0, ATT_BLOCK)
        has_next = n + 1 < nreg

        @pl.when(g == 0)
        def _():
            dkv_ref[...] = jnp.zeros_like(dkv_ref)

            def deltas(c, carry):
                rows = pl.ds(pl.multiple_of(c * ATT_BLOCK, ATT_BLOCK), ATT_BLOCK)
                dlt[rows, :] = jnp.sum(do_ref[rows, :] * o_ref[rows, :], axis=-1, keepdims=True)
                dltn[rows, :] = jnp.sum(don_ref[rows, :] * on_ref[rows, :], axis=-1, keepdims=True)
                return carry
            lax.fori_loop(0, REGION // ATT_BLOCK, deltas, 0)

        def branch(d):
            nsub = REGION // (ATT_BLOCK * d)
            bias = _att_bias(slope, d, 0)
            bias0 = _att_bias(slope, d, first_key)

            def residue(r):
                for i in range(nsub):
                    rows = _sub(r, i, d)
                    kcat = _keys(kc_ref, kp_ref, r, i, d, nsub)
                    vcat = _keys(vc_ref, vp_ref, r, i, d, nsub)
                    qb = q_ref[rows, :]
                    dob = do_ref[rows, :]
                    sc = _dot(qb, kcat, 1, 1) * scale + (bias0 if i == 0 else bias)
                    p = jnp.exp(sc - lse_ref[rows, :][:, 0:1])
                    ds = p * (_dot(dob, vcat, 1, 1) - dlt[rows, :])
                    dq_ref[rows, :] = _dot(ds, kcat, 1, 0) * scale
                    dk = _dot(ds, qb, 0, 0) * scale
                    dv = _dot(p, dob, 0, 0)
                    dkv_ref.at[0][rows, :] += dk[ATT_BLOCK:]
                    dkv_ref.at[1][rows, :] += dv[ATT_BLOCK:]
                    if i > 0:
                        prev = _sub(r, i - 1, d)
                        dkv_ref.at[0][prev, :] += dk[:ATT_BLOCK]
                        dkv_ref.at[1][prev, :] += dv[:ATT_BLOCK]

            _per_residue(d, residue)

            @pl.when(has_next)
            def _():
                bias_prev = bias[:, :ATT_BLOCK]

                def residue_next(r):
                    last = _sub(r, nsub - 1, d)
                    first = _sub(r, 0, d)
                    qb = qn_ref[first, :]
                    dob = don_ref[first, :]
                    sc = _dot(qb, kc_ref[last, :], 1, 1) * scale + bias_prev
                    p = jnp.exp(sc - lsen_ref[first, :][:, 0:1])
                    ds = p * (_dot(dob, vc_ref[last, :], 1, 1) - dltn[first, :])
                    dkv_ref.at[0][last, :] += _dot(ds, qb, 0, 0) * scale
                    dkv_ref.at[1][last, :] += _dot(p, dob, 0, 0)

                _per_residue(d, residue_next)

        for gi, d in enumerate(DILATIONS):
            @pl.when(g == gi)
            def _():
                branch(d)

    last_reg = nreg - 1

    def blk(col, shift=0):
        if shift < 0:
            return pl.BlockSpec((REGION, HEAD_DIM), lambda n, h, g: (jnp.maximum(n - 1, 0), col(h, g)))
        if shift > 0:
            return pl.BlockSpec((REGION, HEAD_DIM), lambda n, h, g: (jnp.minimum(n + 1, last_reg), col(h, g)))
        return pl.BlockSpec((REGION, HEAD_DIM), lambda n, h, g: (n, col(h, g)))

    q_col = lambda h, g: g * ATT_HEADS + h
    k_col = lambda h, g: h
    v_col = lambda h, g: ATT_HEADS + h
    outs, bufs = _call(
        body, name="attn_bwd", jobs=jobs,
        out_shape=[jax.ShapeDtypeStruct(q.shape, F32), jax.ShapeDtypeStruct((2, s, ATT_HEADS * HEAD_DIM), F32)],
        grid=(nreg, ATT_HEADS, len(DILATIONS)),
        in_specs=[pl.BlockSpec((None,) + SLOPE_TILE, lambda n, h, g: (h, 0, 0)),
                  blk(q_col), blk(k_col), blk(k_col, -1), blk(v_col), blk(v_col, -1),
                  blk(k_col), blk(k_col), blk(k_col),
                  blk(q_col, 1), blk(k_col, 1), blk(k_col, 1), blk(k_col, 1)],
        out_specs=[blk(q_col), pl.BlockSpec((2, REGION, HEAD_DIM), lambda n, h, g: (0, n, h))],
        scratch_shapes=[pltpu.VMEM((REGION, 1), F32)] * 2,
        args=[_slope_table(), q, kv, kv, kv, kv, o, lse, do, q, o, lse, do],
        sem=("parallel", "parallel", "arbitrary"))
    return outs if jobs is None else (outs, bufs)


def _adam(w, g, m, v):
    m = ADAM_B1 * m + (1.0 - ADAM_B1) * g
    v = ADAM_B2 * v + (1.0 - ADAM_B2) * (g * g)
    m_hat = m / (1.0 - ADAM_B1 ** ADAM_STEP)
    v_hat = v / (1.0 - ADAM_B2 ** ADAM_STEP)
    delta = -ADAM_LR * (m_hat / (jnp.sqrt(v_hat) + ADAM_EPS) + ADAM_WD * w)
    return delta, m, v


def adam_sharded(recvs, w, m, v, *, name):
    layers = len(recvs)
    n_src, r, c = recvs[0].shape
    tr = _rows(r, c)

    def body(*refs):
        p_refs = refs[:layers]
        w_ref, m_ref, v_ref, g_ref, d_ref, mo_ref, vo_ref = refs[layers:]
        for layer, p_ref in enumerate(p_refs):
            @pl.when(pl.program_id(0) == layer)
            def _():
                g = p_ref[0].astype(F32)
                for src in range(1, n_src):
                    g = g + p_ref[src].astype(F32)
                delta, m_new, v_new = _adam(w_ref[...], g, m_ref[...], v_ref[...])
                g_ref[...] = g
                d_ref[...] = delta
                mo_ref[...] = m_new
                vo_ref[...] = v_new

    blk = pl.BlockSpec((None, tr, c), lambda l, i: (l, i, 0))
    out = jax.ShapeDtypeStruct((layers, r, c), F32)
    part = [pl.BlockSpec((n_src, tr, c), functools.partial(lambda l, i, layer: (0, jnp.where(l == layer, i, 0), 0),
                                                            layer=layer)) for layer in range(layers)]
    return pl.pallas_call(
        body,
        name=name,
        out_shape=[out] * 4,
        grid=(layers, r // tr),
        in_specs=part + [blk, blk, blk],
        out_specs=[blk] * 4,
        compiler_params=_params("parallel", "parallel"),
    )(*recvs, w, m, v)


def sum_partials(parts):
    n_src, r, c = parts.shape

    def body(p_ref, o_ref):
        g = p_ref[0]
        for src in range(1, n_src):
            g = g + p_ref[src]
        o_ref[...] = g

    return pl.pallas_call(
        body,
        name="sum_small_grads",
        out_shape=jax.ShapeDtypeStruct((r, c), F32),
    )(parts)


def adam_packed(w, g, m, v):
    def body(w_ref, g_ref, m_ref, v_ref, d_ref, mo_ref, vo_ref):
        delta, m_new, v_new = _adam(w_ref[...], g_ref[...], m_ref[...], v_ref[...])
        d_ref[...] = delta
        mo_ref[...] = m_new
        vo_ref[...] = v_new

    out = jax.ShapeDtypeStruct(w.shape, F32)
    return pl.pallas_call(body, name="adam_small", out_shape=[out] * 3)(w, g, m, v)


def all_gather(srcs, *, name):
    n = len(srcs)

    def body(*refs):
        src, dst = refs[:n], refs[n:2 * n]
        send_sems, recv_sems, local_sems = refs[2 * n:]
        x, y, c, me = _place()
        sibling = (x, y, 1 - c)
        chips = [(1 - x, y), (x, 1 - y), (1 - x, 1 - y)]

        def index(px, py, pc):
            return 4 * px + 2 * py + pc

        def copy(p, k, block, to, from_src=False):
            slot = dst[p].at[index(*block)]
            return pltpu.make_async_remote_copy(
                src_ref=src[p] if from_src else slot, dst_ref=slot,
                send_sem=send_sems.at[p, k], recv_sem=recv_sems.at[p, k],
                device_id=to, device_id_type=MESH)

        mine = [pltpu.make_async_copy(src[p], dst[p].at[me], local_sems.at[p]) for p in range(n)]
        for cp in mine:
            cp.start()
        first = []
        for p in range(n):
            first.append(copy(p, 0, (x, y, c), sibling, from_src=True))
            for jj, chip in enumerate(chips):
                first.append(copy(p, 1 + jj, (x, y, c), (*chip, c), from_src=True))
        for cp in first:
            cp.start()
        passed = []
        for jj, chip in enumerate(chips):
            for p in range(n):
                copy(p, 1 + jj, (*chip, c), (x, y, c)).wait_recv()
                fwd = copy(p, 4 + jj, (*chip, c), sibling)
                fwd.start()
                passed.append(fwd)
        for p in range(n):
            copy(p, 0, sibling, (x, y, c)).wait_recv()
            for jj, chip in enumerate(chips):
                copy(p, 4 + jj, (*chip, 1 - c), (x, y, c)).wait_recv()
        for cp in first + passed:
            cp.wait_send()
        for cp in mine:
            cp.wait()

    return pl.pallas_call(
        body,
        name=name,
        out_shape=[jax.ShapeDtypeStruct((N_DEV,) + a.shape, a.dtype) for a in srcs],
        in_specs=[ANY] * n,
        out_specs=[ANY] * n,
        scratch_shapes=[pltpu.SemaphoreType.DMA((n, 7)), pltpu.SemaphoreType.DMA((n, 7)),
                        pltpu.SemaphoreType.DMA((n,))],
    )(*srcs)


def exchange_only(*, name, jobs):
    def body(o_ref):
        o_ref[...] = jnp.zeros_like(o_ref)

    _, bufs = _call(body, name=name, jobs=jobs, out_shape=[jax.ShapeDtypeStruct((8, LANE), F32)], grid=(1,),
                    in_specs=[], out_specs=[pl.BlockSpec((8, LANE), lambda i: (0, 0))], args=[], sem=("arbitrary",))
    return None, bufs


def _pack_rows(parts, rows):
    flat = jnp.concatenate([p.reshape(-1) for p in parts])
    return jnp.pad(flat, (0, rows * LANE - flat.shape[0])).reshape(rows, LANE)


def _unpack_rows(packed, shapes):
    flat = packed.reshape(-1)
    out, at = [], 0
    for sh in shapes:
        size = 1
        for dim in sh:
            size *= dim
        out.append(flat[at:at + size].reshape(sh))
        at += size
    return out


CONV_W_PAD = 768
SMALL_W_ROWS = 56


def _pack_small_weights(w_a2, b_a2, hn, conv_w):
    cw = jnp.pad(conv_w.reshape(6, -1), ((0, 0), (0, CONV_W_PAD - conv_w.shape[-1]))).reshape(-1, LANE)
    rows = jnp.concatenate([w_a2[0], b_a2, jnp.pad(hn, ((0, 0), (0, LANE - hn.shape[-1]))), cw], axis=0)
    return jnp.pad(rows, ((0, SMALL_W_ROWS - rows.shape[0]), (0, 0)))


def _unpack_small_weights(gathered):
    w_a2 = gathered[:, 0:GATE_RANK, :].transpose(1, 0, 2).reshape(GATE_RANK, GLA_KEY_DIM)
    b_a2 = gathered[:, GATE_RANK, :].reshape(1, GLA_KEY_DIM)
    hn = gathered[:, GATE_RANK + 1, :GLA_DV // N_DEV].reshape(1, GLA_DV)
    per = D_FF // N_DEV
    cw = gathered[:, GATE_RANK + 2:GATE_RANK + 2 + 6 * CONV_W_PAD // LANE, :].reshape(N_DEV, 6, CONV_W_PAD)[:, :, :per]
    cw = cw.reshape(N_DEV, 2, 3, per).transpose(1, 2, 0, 3).reshape(2, 3, D_FF)
    return w_a2, b_a2, hn, cw


SCHEDULE = {
    "gla_in": [("g1", "gout", None), ("g1", "up0", (0, 1024))],
    "gla_fwd": [("g2", "gout", None), ("g2", "up0", (0, 1024)), ("g1", "up0", (1024, 2048))],
    "gla_out": [("g2", "up0", (1024, 2048)), ("g1", "dn0", (0, 352))],
    "ffn_up0": [("g2", "dn0", (0, 352)), ("g1", "dn0", (352, 704)), ("g1", "kv", None), ("g1", "q", (0, 768))],
    "convglu_fwd0": [("g2", "dn0", (352, 704))],
    "ffn_down0": [("g2", "kv", None), ("g2", "q", (0, 768)), ("g1", "q", (768, 2048)), ("g1", "dout", None)],
    "kv_proj": [("g2", "q", (768, 2048)), ("g2", "dout", None), ("g1", "up1", (0, 704))],
    "q_proj": [("g2", "up1", (0, 704)), ("g1", "up1", (704, 1664))],
    "attn_fwd": [("g2", "up1", (704, 1664)), ("g1", "up1", (1664, 2048)), ("g1", "dn1", None)],
    "dsa_out": [("g2", "up1", (1664, 2048)), ("g2", "dn1", None)],
    "ffn_down_dx1": [("sc", "dn1", (0, 352))],
    "convglu_bwd1": [("sc", "dn1", (352, 704))],
    "ffn_up_dx1": [("sc", "up1", (0, 1024))],
    "attn_bwd": [("sc", "up1", (1024, 2048)), ("sc", "dout", None)],
    "q_proj_dx": [("sc", "q", (0, 1024))],
    "kv_proj_dw": [("sc", "q", (1024, 1792))],
    "kv_proj_dx": [("sc", "q", (1792, 2048)), ("sc", "kv", (0, 768))],
    "ffn_down_dw0": [("sc", "kv", (768, 2048))],
    "ffn_down_dx0": [("sc", "dn0", (0, 384))],
    "convglu_bwd0": [("sc", "dn0", (384, 704))],
    "ffn_up_dx0": [("sc", "up0", (0, 1024))],
    "gla_out_dw": [("sc", "up0", (1024, 1216))],
    "gla_out_dx": [("sc", "up0", (1216, 1408))],
    "gla_bwd": [("sc", "up0", (1408, 2048)), ("sc", "gout", (0, 128))],
    "gla_in_dw": [("sc", "gout", (128, 256))],
    "gla_in_dx": [("sc", "in", (0, 1536))],
    "grads_tail": [("sc", "in", (1536, 2048))],
}
ROW_SHARDED = ("gout", "dout", "dn0", "dn1")


class Plan:
    def __init__(self, weights, srcs=None):
        self.w = dict(weights)
        self.srcs = srcs
        self.grads = {}
        self.recv = {}
        self._names = None

    def weight(self, name):
        buf = self.w[name]
        if name in ROW_SHARDED:
            return buf.reshape(1, buf.shape[0] * buf.shape[1], buf.shape[2])
        return buf

    def jobs(self, call):
        ops = SCHEDULE.get(call)
        if self.srcs is None or not ops:
            return None
        jobs, handles = Jobs(), {}
        for op, name, rows in ops:
            store = self.recv if op == "sc" else self.w
            if name not in handles:
                if name in store:
                    handles[name] = jobs.thru(store[name])
                elif op == "sc":
                    handles[name] = jobs.new(self.grads[name].shape, BF16)
                else:
                    handles[name] = jobs.new((N_DEV,) + self.srcs[name].shape, BF16)
            if op == "g1":
                jobs.gather_ici(self.srcs[name], handles[name], rows)
            elif op == "g2":
                jobs.gather_d2d(handles[name], rows)
            else:
                jobs.scatter(self.grads[name], handles[name], rows)
        self._names = [(name, self.recv if ops[0][0] == "sc" else self.w) for name in handles]
        assert len({op == "sc" for op, _, _ in ops}) == 1
        return jobs

    def run(self, call, fn, *args, **kwargs):
        jobs = self.jobs(call)
        if jobs is None:
            return fn(*args, **kwargs)
        out, bufs = fn(*args, jobs=jobs, **kwargs)
        for (name, store), buf in zip(self._names, bufs):
            store[name] = buf
        return out


def _ffn_fwd(plan, h, norm_g, conv_w, conv_b, tag):
    (n,) = rms_fwd(h, [norm_g], name=f"ffn_norm_fwd{tag}")
    up = plan.run(f"ffn_up{tag}", mm_nn, n, plan.weight(f"up{tag}"), out_dtype=BF16, name=f"ffn_up{tag}")
    act = plan.run(f"convglu_fwd{tag}", convglu_fwd, up, conv_w, conv_b, name=f"convglu_fwd{tag}")
    h_out = plan.run(f"ffn_down{tag}", mm_nn, act, plan.weight(f"dn{tag}"), out_dtype=F32, res=h,
                     name=f"ffn_down{tag}")
    return h_out, (n, up, act)


def _by_rows(dw):
    return dw.reshape(N_DEV, dw.shape[1] // N_DEV, dw.shape[2])


def _ffn_bwd(plan, dh_out, h, saved, norm_g, conv_w, conv_b, tag):
    n, up, act = saved
    plan.grads[f"dn{tag}"] = _by_rows(plan.run(f"ffn_down_dw{tag}", mm_tn, act, dh_out, 1, name=f"ffn_down_dw{tag}"))
    dact = plan.run(f"ffn_down_dx{tag}", mm_nt, dh_out, plan.weight(f"dn{tag}"), out_dtype=BF16,
                    name=f"ffn_down_dx{tag}")
    dup, dconv_w, dconv_b = plan.run(f"convglu_bwd{tag}", convglu_bwd, up, conv_w, conv_b, dact,
                                     name=f"convglu_bwd{tag}")
    plan.grads[f"up{tag}"] = mm_tn(n, dup, N_DEV, name=f"ffn_up_dw{tag}")
    dn = plan.run(f"ffn_up_dx{tag}", mm_nt, dup, plan.weight(f"up{tag}"), out_dtype=F32, name=f"ffn_up_dx{tag}")
    dh, (dnorm,) = rms_bwd(h, [norm_g], [dn], dh_out, name=f"ffn_norm_bwd{tag}")
    return dh, dnorm, dconv_w, dconv_b


def local_step(x, target, wts, plan):
    row = lambda v: v.reshape(1, -1)
    attn_norm, ffn_norm = wts["attn_norm"], wts["ffn_norm"]
    conv_w, conv_b = wts["ffn_conv_w"], wts["ffn_conv_b"]

    (n1,) = rms_fwd(x, [row(attn_norm[0])], name="attn_norm_fwd0")
    proj = plan.run("gla_in", mm_nn, n1, wts["gla_w_in"], out_dtype=F32, name="gla_in")
    la = gate_fwd(proj, wts["gla_w_a2"], wts["gla_b_a2"])
    o_gla, states = plan.run("gla_fwd", gla_fwd, proj, la)
    og = headnorm_fwd(o_gla, proj, wts["gla_head_norm"])
    h1 = plan.run("gla_out", mm_nn, og, plan.weight("gout"), out_dtype=F32, res=x, name="gla_out")
    h2, ffn0 = _ffn_fwd(plan, h1, row(ffn_norm[0]), conv_w[0], row(conv_b[0]), "0")

    kvn, n3 = rms_fwd(h2, [row(wts["kv_norm"]), row(attn_norm[1])], name="kv_attn_norm_fwd")
    kv = plan.run("kv_proj", mm_nn, kvn, plan.weight("kv"), out_dtype=F32, name="kv_proj")
    q = plan.run("q_proj", mm_nn, n3, plan.weight("q"), out_dtype=F32, name="q_proj")
    o_att, lse = plan.run("attn_fwd", attn_fwd, q, kv)
    h3 = plan.run("dsa_out", mm_nn, o_att, plan.weight("dout"), out_dtype=F32, res=h2, name="dsa_out")
    h4, ffn1 = _ffn_fwd(plan, h3, row(ffn_norm[1]), conv_w[1], row(conv_b[1]), "1")

    loss_tile, dh4, d_final = loss_head(h4, row(wts["final_norm"]), target)

    dh3, d_ffn1, dcw1, dcb1 = _ffn_bwd(plan, dh4, h3, ffn1, row(ffn_norm[1]), conv_w[1], row(conv_b[1]), "1")
    plan.grads["dout"] = _by_rows(mm_tn(o_att, dh3, 1, name="dsa_out_dw"))
    do_att = mm_nt(dh3, plan.weight("dout"), out_dtype=F32, name="dsa_out_dx")
    dq, dkv = plan.run("attn_bwd", attn_bwd, q, kv, o_att, lse, do_att)
    plan.grads["q"] = mm_tn(n3, dq, N_DEV, name="q_proj_dw")
    dn3 = plan.run("q_proj_dx", mm_nt, dq, plan.weight("q"), out_dtype=F32, name="q_proj_dx")
    plan.grads["kv"] = plan.run("kv_proj_dw", mm_tn, kvn, dkv, N_DEV, name="kv_proj_dw")
    dkvn = plan.run("kv_proj_dx", mm_nt, dkv, plan.weight("kv"), out_dtype=F32, name="kv_proj_dx")
    dh2, (d_kvnorm, d_attn1) = rms_bwd(h2, [row(wts["kv_norm"]), row(attn_norm[1])], [dkvn, dn3], dh3,
                                       name="kv_attn_norm_bwd")
    dh1, d_ffn0, dcw0, dcb0 = _ffn_bwd(plan, dh2, h1, ffn0, row(ffn_norm[0]), conv_w[0], row(conv_b[0]), "0")
    plan.grads["gout"] = _by_rows(plan.run("gla_out_dw", mm_tn, og, dh1, 1, name="gla_out_dw"))
    dog = plan.run("gla_out_dx", mm_nt, dh1, plan.weight("gout"), out_dtype=F32, name="gla_out_dx")
    do_gla, dr, d_hn = headnorm_bwd(o_gla, proj, wts["gla_head_norm"], dog)
    dq_g, dk_g, dv_g, dla = plan.run("gla_bwd", gla_bwd, proj, la, states, do_gla)
    da, dw_a2p, db_a2 = gate_bwd(proj, wts["gla_w_a2"], wts["gla_b_a2"], dla)
    dproj = jnp.concatenate([dq_g, dk_g, dv_g, dr, da], axis=1)
    assert dproj.shape[1] == GLA_IN_PAD
    dw_in = plan.run("gla_in_dw", mm_tn, n1, dproj, 1, name="gla_in_dw")
    plan.grads["in"] = dw_in[0, :, :GLA_IN_DIM].reshape(D_MODEL, N_DEV, GLA_IN_DIM // N_DEV).transpose(1, 0, 2)
    dn1 = plan.run("gla_in_dx", mm_nt, dproj, wts["gla_w_in"], out_dtype=F32, name="gla_in_dx")
    grad_x, (d_attn0,) = rms_bwd(x, [row(attn_norm[0])], [dn1], dh1, name="attn_norm_bwd0")

    small = dict(
        attn_norm=jnp.concatenate([d_attn0, d_attn1], axis=0),
        ffn_norm=jnp.concatenate([d_ffn0, d_ffn1], axis=0),
        kv_norm=d_kvnorm.reshape(-1),
        final_norm=d_final.reshape(-1),
        ffn_conv_b=jnp.concatenate([dcb0, dcb1], axis=0),
        gla_w_a2=dw_a2p[:GATE_RANK],
        gla_b_a2=db_a2,
        gla_head_norm=d_hn,
        ffn_conv_w=jnp.stack([dcw0, dcw1]),
    )
    return loss_tile, grad_x, small


SMALL_ORDER = ("attn_norm", "ffn_norm", "kv_norm", "final_norm", "ffn_conv_b",
               "gla_w_a2", "gla_b_a2", "gla_head_norm", "ffn_conv_w")
SMALL_FULL = dict(attn_norm=(2, D_MODEL), ffn_norm=(2, D_MODEL), kv_norm=(D_MODEL,), final_norm=(D_MODEL,),
                  ffn_conv_b=(2, D_FF), gla_w_a2=(GATE_RANK, GLA_KEY_DIM), gla_b_a2=(1, GLA_KEY_DIM),
                  gla_head_norm=(1, GLA_DV), ffn_conv_w=(2, 3, D_FF))
SMALL_SHARDED = ("gla_w_a2", "gla_b_a2", "gla_head_norm", "ffn_conv_w")
SMALL_GRAD_ROWS = 592
SMALL_ADAM_ROWS = 240


def kernel(x, attn_norm, gla_w_in, gla_w_a2, gla_b_a2, gla_head_norm, gla_w_out, kv_norm, w_kv, dsa_w_q, dsa_w_out, ffn_norm, ffn_w_up, ffn_conv_w, ffn_conv_b, ffn_w_down, final_norm, loss_target, m_attn_norm, m_gla_w_in, m_gla_w_a2, m_gla_b_a2, m_gla_head_norm, m_gla_w_out, m_kv_norm, m_w_kv, m_dsa_w_q, m_dsa_w_out, m_ffn_norm, m_ffn_w_up, m_ffn_conv_w, m_ffn_conv_b, m_ffn_w_down, m_final_norm, v_attn_norm, v_gla_w_in, v_gla_w_a2, v_gla_b_a2, v_gla_head_norm, v_gla_w_out, v_kv_norm, v_w_kv, v_dsa_w_q, v_dsa_w_out, v_ffn_norm, v_ffn_w_up, v_ffn_conv_w, v_ffn_conv_b, v_ffn_w_down, v_final_norm):
    me = 4 * lax.axis_index("x") + 2 * lax.axis_index("y") + lax.axis_index("c")
    bf = lambda a: a.astype(BF16)

    g_in, g_small = all_gather([bf(gla_w_in[0]), _pack_small_weights(gla_w_a2, gla_b_a2, gla_head_norm, ffn_conv_w)],
                               name="gather_first")
    w_a2_full, b_a2_full, hn_full, conv_w_full = _unpack_small_weights(g_small)
    w_in_full = jnp.pad(g_in.transpose(1, 0, 2).reshape(D_MODEL, GLA_IN_DIM), ((0, 0), (0, GLA_IN_PAD - GLA_IN_DIM)))
    wts = dict(
        attn_norm=attn_norm, ffn_norm=ffn_norm, kv_norm=kv_norm, final_norm=final_norm, ffn_conv_b=ffn_conv_b,
        gla_w_in=w_in_full[None],
        gla_w_a2=jnp.pad(bf(w_a2_full), ((0, LANE - GATE_RANK), (0, 0))),
        gla_b_a2=b_a2_full, gla_head_norm=hn_full, ffn_conv_w=conv_w_full,
    )
    plan = Plan({}, srcs=dict(gout=bf(gla_w_out[0]), kv=bf(w_kv), q=bf(dsa_w_q[0]), dout=bf(dsa_w_out[0]),
                              up0=bf(ffn_w_up[0]), up1=bf(ffn_w_up[1]), dn0=bf(ffn_w_down[0]), dn1=bf(ffn_w_down[1])))

    loss_tile, grad_x, small = local_step(x[0], loss_target[0], wts, plan)
    loss = lax.psum(loss_tile[0, 0], ("x", "y", "c"))

    plan.run("grads_tail", exchange_only, name="grads_tail")
    shard3 = lambda a: a.reshape((-1,) + a.shape[-2:])
    big_params = dict(gla_w_in=(("in",), gla_w_in, m_gla_w_in, v_gla_w_in),
                      gla_w_out=(("gout",), gla_w_out, m_gla_w_out, v_gla_w_out),
                      w_kv=(("kv",), w_kv, m_w_kv, v_w_kv),
                      dsa_w_q=(("q",), dsa_w_q, m_dsa_w_q, v_dsa_w_q),
                      dsa_w_out=(("dout",), dsa_w_out, m_dsa_w_out, v_dsa_w_out),
                      ffn_w_up=(("up0", "up1"), ffn_w_up, m_ffn_w_up, v_ffn_w_up),
                      ffn_w_down=(("dn0", "dn1"), ffn_w_down, m_ffn_w_down, v_ffn_w_down))
    res = {}
    for nm, (parts, w, m, v) in big_params.items():
        outs = adam_sharded([plan.recv[p] for p in parts], shard3(w), shard3(m), shard3(v), name=f"adam_{nm}")
        res[nm] = [o.reshape(w.shape) for o in outs]

    packed = _pack_rows([small[nm] for nm in SMALL_ORDER], SMALL_GRAD_ROWS)
    (parts,) = all_gather([packed], name="gather_small_grads")
    full = dict(zip(SMALL_ORDER, _unpack_rows(sum_partials(parts), [SMALL_FULL[nm] for nm in SMALL_ORDER])))
    local_w = dict(attn_norm=attn_norm, ffn_norm=ffn_norm, kv_norm=kv_norm, final_norm=final_norm,
                   ffn_conv_b=ffn_conv_b, gla_w_a2=gla_w_a2, gla_b_a2=gla_b_a2, gla_head_norm=gla_head_norm,
                   ffn_conv_w=ffn_conv_w)
    local_m = dict(attn_norm=m_attn_norm, ffn_norm=m_ffn_norm, kv_norm=m_kv_norm, final_norm=m_final_norm,
                   ffn_conv_b=m_ffn_conv_b, gla_w_a2=m_gla_w_a2, gla_b_a2=m_gla_b_a2, gla_head_norm=m_gla_head_norm,
                   ffn_conv_w=m_ffn_conv_w)
    local_v = dict(attn_norm=v_attn_norm, ffn_norm=v_ffn_norm, kv_norm=v_kv_norm, final_norm=v_final_norm,
                   ffn_conv_b=v_ffn_conv_b, gla_w_a2=v_gla_w_a2, gla_b_a2=v_gla_b_a2, gla_head_norm=v_gla_head_norm,
                   ffn_conv_w=v_ffn_conv_w)
    local_g = {}
    for nm in SMALL_ORDER:
        gfull = full[nm]
        if nm in SMALL_SHARDED:
            per = gfull.shape[-1] // N_DEV
            gfull = lax.dynamic_slice_in_dim(gfull, me * per, per, axis=gfull.ndim - 1)
        local_g[nm] = gfull.reshape(local_w[nm].shape)
    shapes = [local_w[nm].shape for nm in SMALL_ORDER]
    pk = lambda dd: _pack_rows([dd[nm] for nm in SMALL_ORDER], SMALL_ADAM_ROWS)
    d_p, m_p, v_p = adam_packed(pk(local_w), pk(local_g), pk(local_m), pk(local_v))
    for nm, dl, mn, vn in zip(SMALL_ORDER, _unpack_rows(d_p, shapes), _unpack_rows(m_p, shapes),
                              _unpack_rows(v_p, shapes)):
        res[nm] = [local_g[nm], dl, mn, vn]

    order = ("attn_norm", "gla_w_in", "gla_w_a2", "gla_b_a2", "gla_head_norm", "gla_w_out", "kv_norm", "w_kv",
             "dsa_w_q", "dsa_w_out", "ffn_norm", "ffn_w_up", "ffn_conv_w", "ffn_conv_b", "ffn_w_down", "final_norm")
    outs = [loss, grad_x[None]]
    for kind in range(4):
        outs.extend(res[nm][kind] for nm in order)
    return tuple(outs)
```

```python
import functools

import jax
import jax.numpy as jnp
from jax import lax
from jax.experimental import pallas as pl
from jax.experimental.pallas import tpu as pltpu

F32 = jnp.float32
BF16 = jnp.bfloat16
MESH = pl.DeviceIdType.MESH
ANY = pl.BlockSpec(memory_space=pl.ANY)

N_DEV = 8
D_MODEL = 2048
GLA_HEADS = 4
GLA_KEY_DIM = 1024
GLA_VAL_DIM = 2048
GLA_DK = 256
GLA_DV = 512
GATE_RANK = 16
GATE_NORMALIZER = 16.0
GLA_CHUNK = 64
GLA_IN_DIM = 2 * GLA_KEY_DIM + 2 * GLA_VAL_DIM + GATE_RANK
GLA_IN_PAD = 6272
ATT_HEADS = 16
HEAD_DIM = 128
DILATIONS = (1, 4, 16)
ATT_BLOCK = 128
D_FF = 5632
EPS = 1e-6
ADAM_LR = 0.001
ADAM_B1 = 0.9
ADAM_B2 = 0.999
ADAM_EPS = 1e-08
ADAM_WD = 0.01
ADAM_STEP = 10
NEG = -1e30
LANE = 128
NORM_ROWS = 64
VMEM_LIMIT = 52 * 1024 * 1024
ALIBI_SLOPES = tuple(2.0 ** (-0.5 * (i + 1)) for i in range(ATT_HEADS))


def _params(*sem):
    return pltpu.CompilerParams(dimension_semantics=sem, vmem_limit_bytes=VMEM_LIMIT)


def _tile(n, cap):
    best = None
    for t in range(LANE, min(n, cap) + 1, LANE):
        if n % t == 0:
            best = t
    return best if best is not None else n


def _shard_group(j, ns, cap):
    best = 1
    for g in range(1, j + 1):
        if j % g == 0 and g * ns <= cap:
            best = g
    return best


def _rows(r, c, budget=256 * 1024):
    best = None
    for t in range(16, r + 1, 16):
        if r % t == 0 and t * c <= budget:
            best = t
    return best if best is not None else r


def _flip(coord, bit):
    return 1 - coord if bit else coord


def _place():
    x, y, c = lax.axis_index("x"), lax.axis_index("y"), lax.axis_index("c")
    return x, y, c, 4 * x + 2 * y + c


def _rows_of(ref, rows):
    return ref if rows is None else ref.at[pl.ds(rows[0], rows[1] - rows[0])]


class Jobs:
    def __init__(self):
        self.srcs = []
        self.bufs = []
        self.sems = []
        self.steps = []

    def _src(self, a):
        for i, b in enumerate(self.srcs):
            if b is a:
                return i
        self.srcs.append(a)
        return len(self.srcs) - 1

    def new(self, shape, dtype):
        self.bufs.append((None, jax.ShapeDtypeStruct(shape, dtype)))
        return len(self.bufs) - 1

    def thru(self, a):
        self.bufs.append((a, jax.ShapeDtypeStruct(a.shape, a.dtype)))
        return len(self.bufs) - 1

    def _sem(self, n):
        self.sems.append(pltpu.SemaphoreType.DMA((n,)))
        return len(self.sems) - 1

    def gather_ici(self, src, buf, rows=None):
        si, send, recv, loc = self._src(src), self._sem(4), self._sem(4), self._sem(1)

        def remote(srcs, bufs, sems, slot_of):
            x, y, c, me = _place()
            peers = [(x, y, 1 - c), (1 - x, y, c), (x, 1 - y, c), (1 - x, 1 - y, c)]
            return [pltpu.make_async_remote_copy(
                src_ref=_rows_of(srcs[si], rows),
                dst_ref=_rows_of(bufs[buf].at[me if slot_of == "mine" else 4 * p[0] + 2 * p[1] + p[2]], rows),
                send_sem=sems[send].at[k], recv_sem=sems[recv].at[k], device_id=p, device_id_type=MESH)
                for k, p in enumerate(peers)]

        def local(srcs, bufs, sems):
            return pltpu.make_async_copy(_rows_of(srcs[si], rows), _rows_of(bufs[buf].at[_place()[3]], rows),
                                         sems[loc].at[0])

        def start(srcs, bufs, sems):
            local(srcs, bufs, sems).start()
            for cp in remote(srcs, bufs, sems, "mine"):
                cp.start()

        def finish(srcs, bufs, sems):
            for cp in remote(srcs, bufs, sems, "peer"):
                cp.wait_recv()
            for cp in remote(srcs, bufs, sems, "mine"):
                cp.wait_send()
            local(srcs, bufs, sems).wait()

        self.steps.append((start, finish))

    def gather_d2d(self, buf, rows=None):
        send, recv = self._sem(3), self._sem(3)

        def copies(bufs, sems, core):
            x, y, c, _ = _place()
            cc = c if core == "mine" else 1 - c
            chips = [(1 - x, y), (x, 1 - y), (1 - x, 1 - y)]
            return [pltpu.make_async_remote_copy(
                src_ref=_rows_of(bufs[buf].at[4 * px + 2 * py + cc], rows),
                dst_ref=_rows_of(bufs[buf].at[4 * px + 2 * py + cc], rows),
                send_sem=sems[send].at[k], recv_sem=sems[recv].at[k],
                device_id=(x, y, 1 - c), device_id_type=MESH) for k, (px, py) in enumerate(chips)]

        def start(srcs, bufs, sems):
            for cp in copies(bufs, sems, "mine"):
                cp.start()

        def finish(srcs, bufs, sems):
            for cp in copies(bufs, sems, "sibling"):
                cp.wait_recv()
            for cp in copies(bufs, sems, "mine"):
                cp.wait_send()

        self.steps.append((start, finish))

    def scatter(self, src, buf, rows=None):
        si, send, recv, loc = self._src(src), self._sem(N_DEV - 1), self._sem(N_DEV - 1), self._sem(1)

        def remote(srcs, bufs, sems, slot_of):
            x, y, c, me = _place()
            out = []
            for k in range(1, N_DEV):
                px, py, pc = _flip(x, k >> 2), _flip(y, (k >> 1) & 1), _flip(c, k & 1)
                peer = 4 * px + 2 * py + pc
                out.append(pltpu.make_async_remote_copy(
                    src_ref=_rows_of(srcs[si].at[peer], rows),
                    dst_ref=_rows_of(bufs[buf].at[me if slot_of == "mine" else peer], rows),
                    send_sem=sems[send].at[k - 1], recv_sem=sems[recv].at[k - 1],
                    device_id=(px, py, pc), device_id_type=MESH))
            return out

        def local(srcs, bufs, sems):
            me = _place()[3]
            return pltpu.make_async_copy(_rows_of(srcs[si].at[me], rows), _rows_of(bufs[buf].at[me], rows),
                                         sems[loc].at[0])

        def start(srcs, bufs, sems):
            local(srcs, bufs, sems).start()
            for cp in remote(srcs, bufs, sems, "mine"):
                cp.start()

        def finish(srcs, bufs, sems):
            for cp in remote(srcs, bufs, sems, "peer"):
                cp.wait_recv()
            for cp in remote(srcs, bufs, sems, "mine"):
                cp.wait_send()
            local(srcs, bufs, sems).wait()

        self.steps.append((start, finish))


def _call(body, *, name, grid, in_specs, out_specs, out_shape, args, sem, scratch_shapes=(), jobs=None):
    in_specs, out_specs, out_shape = list(in_specs), list(out_specs), list(out_shape)
    scratch_shapes = list(scratch_shapes)
    if jobs is None:
        res = pl.pallas_call(body, name=name, out_shape=out_shape, grid=grid, in_specs=in_specs,
                             out_specs=out_specs, scratch_shapes=scratch_shapes,
                             compiler_params=_params(*sem))(*args)
        return list(res), []
    thru = [a for a, _ in jobs.bufs if a is not None]
    n_in, n_src, n_thru = len(args), len(jobs.srcs), len(thru)
    n_out, n_buf, n_scr = len(out_shape), len(jobs.bufs), len(scratch_shapes)
    aliases, t = {}, 0
    for b, (a, _) in enumerate(jobs.bufs):
        if a is not None:
            aliases[n_in + n_src + t] = n_out + b
            t += 1

    def wrapped(*refs):
        at = 0
        ins = refs[at:at + n_in]; at += n_in
        srcs = refs[at:at + n_src]; at += n_src + n_thru
        outs = refs[at:at + n_out]; at += n_out
        bufs = refs[at:at + n_buf]; at += n_buf
        scr = refs[at:at + n_scr]; at += n_scr
        sems = refs[at:]
        first, last = None, None
        for axis, size in enumerate(grid):
            pid = pl.program_id(axis)
            f, l = pid == 0, pid == size - 1
            first = f if first is None else first & f
            last = l if last is None else last & l

        @pl.when(first)
        def _():
            for start, _ in jobs.steps:
                start(srcs, bufs, sems)

        body(*ins, *outs, *scr)

        @pl.when(last)
        def _():
            for _, finish in jobs.steps:
                finish(srcs, bufs, sems)

    res = pl.pallas_call(
        wrapped, name=name,
        out_shape=out_shape + [s for _, s in jobs.bufs],
        grid=grid,
        in_specs=in_specs + [ANY] * (n_src + n_thru),
        out_specs=out_specs + [ANY] * n_buf,
        scratch_shapes=scratch_shapes + jobs.sems,
        input_output_aliases=aliases,
        compiler_params=_params(*(["arbitrary"] * len(grid))),
    )(*args, *jobs.srcs, *thru)
    return res[:n_out], res[n_out:]


def mm_nn(a, w, *, out_dtype, name, res=None, tm=None, jobs=None):
    m, k = a.shape
    j, k2, ns = w.shape
    whole = j == 1 and ns <= 2048 and k <= 2048
    tm = tm or (1024 if a.dtype == BF16 and not whole else 512)
    assert k == k2 and m % tm == 0
    tn = ns if whole else _tile(ns, 1408)
    nsub = ns // tn
    tk = k if k <= 2048 else _tile(k, 1408)
    nk = k // tk
    has_res = res is not None

    def body(*refs):
        if has_res:
            a_ref, w_ref, r_ref, o_ref, acc = refs
        else:
            a_ref, w_ref, o_ref, acc = refs
        kk = pl.program_id(2)

        @pl.when(kk == 0)
        def _():
            acc[...] = jnp.zeros_like(acc)

        acc[...] += jnp.dot(a_ref[...].astype(BF16), w_ref[...], preferred_element_type=F32)

        @pl.when(kk == nk - 1)
        def _():
            r = acc[...]
            if has_res:
                r = r + r_ref[...]
            o_ref[...] = r.astype(out_dtype)

    in_specs = [
        pl.BlockSpec((tm, tk), lambda i, n, kk: (i, kk)),
        pl.BlockSpec((None, tk, tn), lambda i, n, kk: (n // nsub, kk, n % nsub)),
    ]
    args = [a, w]
    if has_res:
        in_specs.append(pl.BlockSpec((tm, tn), lambda i, n, kk: (i, n)))
        args.append(res)
    (out,), bufs = _call(
        body, name=name, jobs=jobs,
        out_shape=[jax.ShapeDtypeStruct((m, j * ns), out_dtype)],
        grid=(m // tm, j * nsub, nk),
        in_specs=in_specs,
        out_specs=[pl.BlockSpec((tm, tn), lambda i, n, kk: (i, n))],
        scratch_shapes=[pltpu.VMEM((tm, tn), F32)],
        args=args, sem=("parallel", "parallel", "arbitrary"))
    return out if jobs is None else (out, bufs)


def mm_nt(dy, w, *, out_dtype, name, tm=None, jobs=None, norm=None):
    parts, m, n = (1,) + dy.shape if dy.ndim == 2 else dy.shape
    n *= parts
    j, k, ns = w.shape
    if norm is not None:
        x, dres, gains, more = norm
        tm = tm or (256 if more else 512)
    tm = tm or 1024
    assert n == j * ns and m % tm == 0
    tn = _tile(ns, 2048)
    nsub = ns // tn
    jb = _shard_group(j // parts, ns, 2048 if norm is None else 1024) if nsub == 1 else 1
    tko = _tile(k, 1408) if norm is None else k
    nn = j * nsub // jb
    per_part = nn // parts
    if dy.ndim == 2:
        dy_spec = pl.BlockSpec((tm, jb * tn), lambda i, ko, nq: (i, nq))
    else:
        dy_spec = pl.BlockSpec((None, tm, jb * tn), lambda i, ko, nq: (nq // per_part, i, nq % per_part))
    if jb == 1:
        w_spec = pl.BlockSpec((None, tko, tn), lambda i, ko, nq: (nq // nsub, ko, nq % nsub))
    else:
        w_spec = pl.BlockSpec((jb, tko, ns), lambda i, ko, nq: (nq, ko, 0))

    n_gain = 0 if norm is None else len(gains)
    n_more = 0 if norm is None else len(more)

    def body(*refs):
        a_ref, w_ref = refs[:2]
        acc = refs[-1]
        nq = pl.program_id(2)
        first = pl.program_id(0) == 0

        @pl.when(nq == 0)
        def _():
            acc[...] = jnp.zeros_like(acc)

        if jb == 1:
            acc[...] += lax.dot_general(a_ref[...].astype(BF16), w_ref[...], (((1,), (1,)), ((), ())),
                                        preferred_element_type=F32)
        else:
            part = acc[...]
            for jj in range(jb):
                part = part + lax.dot_general(a_ref[:, jj * ns:(jj + 1) * ns].astype(BF16), w_ref[jj],
                                              (((1,), (1,)), ((), ())), preferred_element_type=F32)
            acc[...] = part

        @pl.when(nq == nn - 1)
        def _():
            if norm is None:
                refs[2][...] = acc[...].astype(out_dtype)
                return
            x_ref, r_ref = refs[2:4]
            g_refs = refs[4:4 + n_gain]
            e_refs = refs[4 + n_gain:4 + n_gain + n_more]
            dx_ref = refs[4 + n_gain + n_more]
            dg_refs = refs[5 + n_gain + n_more:-1]

            @pl.when(first)
            def _():
                for dg_ref in dg_refs:
                    dg_ref[...] = jnp.zeros_like(dg_ref)

            def rows(c, carry):
                sl = pl.ds(pl.multiple_of(c * NORM_ROWS, NORM_ROWS), NORM_ROWS)
                xv = x_ref[sl, :]
                r = lax.rsqrt(jnp.mean(xv * xv, axis=-1, keepdims=True) + EPS)
                xh = xv * r
                out = r_ref[sl, :]
                for idx, (g_ref, dg_ref) in enumerate(zip(g_refs, dg_refs)):
                    dyv = acc[sl, :] if idx == 0 else e_refs[idx - 1][sl, :].astype(F32)
                    dg_ref[...] += jnp.sum(dyv * xh, axis=0, keepdims=True)
                    dxh = dyv * g_ref[...]
                    out = out + r * (dxh - xh * jnp.mean(dxh * xh, axis=-1, keepdims=True))
                dx_ref[sl, :] = out
                return carry

            lax.fori_loop(0, tm // NORM_ROWS, rows, 0)

    out_tile = pl.BlockSpec((tm, tko), lambda i, ko, nq: (i, ko))
    in_specs, args = [dy_spec, w_spec], [dy, w]
    out_shape, out_specs = [jax.ShapeDtypeStruct((m, k), out_dtype)], [out_tile]
    sem = ("parallel", "parallel", "arbitrary")
    if norm is not None:
        vec = pl.BlockSpec((1, k), lambda i, ko, nq: (0, 0))
        in_specs += [out_tile, out_tile] + [vec] * n_gain + [out_tile] * n_more
        args += [x, dres] + list(gains) + list(more)
        out_shape = [jax.ShapeDtypeStruct((m, k), F32)] + [jax.ShapeDtypeStruct((1, k), F32)] * n_gain
        out_specs = [out_tile] + [vec] * n_gain
        sem = ("arbitrary", "arbitrary", "arbitrary")
    outs, bufs = _call(
        body, name=name, jobs=jobs, out_shape=out_shape, grid=(m // tm, k // tko, nn),
        in_specs=in_specs, out_specs=out_specs, scratch_shapes=[pltpu.VMEM((tm, tko), F32)], args=args, sem=sem)
    out = outs[0] if norm is None else (outs[0], outs[1:])
    return out if jobs is None else (out, bufs)


def mm_tn(x, dy, j, *, name, tm=1024, jobs=None):
    m, k = x.shape
    parts, m2, n = (1,) + dy.shape if dy.ndim == 2 else dy.shape
    n *= parts
    assert m == m2 and n % j == 0 and m % tm == 0
    ns = n // j
    tn = _tile(ns, 1408)
    nsub = ns // tn
    jb = _shard_group(j // parts, ns, 1536) if nsub == 1 else 1
    tk = _tile(k, 1408)
    nm = m // tm
    n_steps = j * nsub // jb
    per_part = n_steps // parts
    if dy.ndim == 2:
        dy_spec = pl.BlockSpec((tm, jb * tn), lambda kq, nq, mi: (mi, nq))
    else:
        dy_spec = pl.BlockSpec((None, tm, jb * tn), lambda kq, nq, mi: (nq // per_part, mi, nq % per_part))
    if jb == 1:
        out_spec = pl.BlockSpec((None, tk, tn), lambda kq, nq, mi: (nq // nsub, kq, nq % nsub))
        acc_shape = (tk, tn)
    else:
        out_spec = pl.BlockSpec((jb, tk, ns), lambda kq, nq, mi: (nq, kq, 0))
        acc_shape = (jb, tk, ns)

    def body(x_ref, dy_ref, o_ref, acc):
        mi = pl.program_id(2)

        @pl.when(mi == 0)
        def _():
            acc[...] = jnp.zeros_like(acc)

        xb = x_ref[...].astype(BF16)
        if jb == 1:
            acc[...] += lax.dot_general(xb, dy_ref[...].astype(BF16), (((0,), (0,)), ((), ())),
                                        preferred_element_type=F32)
        else:
            for jj in range(jb):
                acc[jj] += lax.dot_general(xb, dy_ref[:, jj * ns:(jj + 1) * ns].astype(BF16),
                                           (((0,), (0,)), ((), ())), preferred_element_type=F32)

        @pl.when(mi == nm - 1)
        def _():
            o_ref[...] = acc[...].astype(BF16)

    (out,), bufs = _call(
        body, name=name, jobs=jobs,
        out_shape=[jax.ShapeDtypeStruct((j, k, ns), BF16)],
        grid=(k // tk, n_steps, nm),
        in_specs=[
            pl.BlockSpec((tm, tk), lambda kq, nq, mi: (mi, kq)),
            dy_spec,
        ],
        out_specs=[out_spec],
        scratch_shapes=[pltpu.VMEM(acc_shape, F32)],
        args=[x, dy], sem=("parallel", "parallel", "arbitrary"))
    return out if jobs is None else (out, bufs)


def rms_fwd(x, gains, *, name, ts=512):
    s, d = x.shape
    n = len(gains)

    def body(x_ref, *refs):
        xv = x_ref[...]
        xh = xv * lax.rsqrt(jnp.mean(xv * xv, axis=-1, keepdims=True) + EPS)
        for g_ref, o_ref in zip(refs[:n], refs[n:]):
            o_ref[...] = (xh * g_ref[...]).astype(BF16)

    row = pl.BlockSpec((ts, d), lambda i: (i, 0))
    vec = pl.BlockSpec((1, d), lambda i: (0, 0))
    return pl.pallas_call(
        body,
        name=name,
        out_shape=[jax.ShapeDtypeStruct((s, d), BF16)] * n,
        grid=(s // ts,),
        in_specs=[row] + [vec] * n,
        out_specs=[row] * n,
        compiler_params=_params("parallel"),
    )(x, *gains)


def loss_head(h, gain, target, *, ts=256):
    s, d = h.shape

    def body(h_ref, g_ref, t_ref, l_ref, dh_ref, dg_ref):
        i = pl.program_id(0)

        @pl.when(i == 0)
        def _():
            l_ref[...] = jnp.zeros_like(l_ref)
            dg_ref[...] = jnp.zeros_like(dg_ref)

        xv = h_ref[...]
        r = lax.rsqrt(jnp.mean(xv * xv, axis=-1, keepdims=True) + EPS)
        xh = xv * r
        g = g_ref[...]
        err = xh * g - t_ref[...]
        l_ref[...] += 0.5 * jnp.sum(jnp.mean(err * err, axis=-1, keepdims=True))
        dy = err * (1.0 / d)
        dg_ref[...] += jnp.sum(dy * xh, axis=0, keepdims=True)
        dxh = dy * g
        dh_ref[...] = r * (dxh - xh * jnp.mean(dxh * xh, axis=-1, keepdims=True))

    row = pl.BlockSpec((ts, d), lambda i: (i, 0))
    vec = pl.BlockSpec((1, d), lambda i: (0, 0))
    return pl.pallas_call(
        body,
        name="loss_head",
        out_shape=[jax.ShapeDtypeStruct((8, LANE), F32), jax.ShapeDtypeStruct((s, d), F32),
                   jax.ShapeDtypeStruct((1, d), F32)],
        grid=(s // ts,),
        in_specs=[row, vec, row],
        out_specs=[pl.BlockSpec((8, LANE), lambda i: (0, 0)), row, vec],
        compiler_params=_params("arbitrary"),
    )(h, gain, target)


A_BLOCK = (2 * GLA_KEY_DIM + 2 * GLA_VAL_DIM) // LANE


def gate_fwd(proj, w_a2p, b_a2, *, ts=512):
    s = proj.shape[0]

    def body(a_ref, w_ref, b_ref, o_ref):
        z = jnp.dot(a_ref[...].astype(BF16), w_ref[...], preferred_element_type=F32) + b_ref[...]
        o_ref[...] = (jnp.minimum(z, 0.0) - jnp.log(1.0 + jnp.exp(-jnp.abs(z)))) * (1.0 / GATE_NORMALIZER)

    return pl.pallas_call(
        body,
        name="gate_fwd",
        out_shape=jax.ShapeDtypeStruct((s, GLA_KEY_DIM), F32),
        grid=(s // ts,),
        in_specs=[pl.BlockSpec((ts, LANE), lambda i: (i, A_BLOCK)),
                  pl.BlockSpec((LANE, GLA_KEY_DIM), lambda i: (0, 0)),
                  pl.BlockSpec((1, GLA_KEY_DIM), lambda i: (0, 0))],
        out_specs=pl.BlockSpec((ts, GLA_KEY_DIM), lambda i: (i, 0)),
        compiler_params=_params("parallel"),
    )(proj, w_a2p, b_a2)


def gate_bwd(proj, w_a2p, b_a2, dla, *, ts=512):
    s = proj.shape[0]

    def body(a_ref, w_ref, b_ref, dla_ref, da_ref, dw_ref, db_ref):
        i = pl.program_id(0)

        @pl.when(i == 0)
        def _():
            dw_ref[...] = jnp.zeros_like(dw_ref)
            db_ref[...] = jnp.zeros_like(db_ref)

        a = a_ref[...].astype(BF16)
        w = w_ref[...]
        z = jnp.dot(a, w, preferred_element_type=F32) + b_ref[...]
        dz = dla_ref[...] * (1.0 / GATE_NORMALIZER) / (1.0 + jnp.exp(z))
        dzb = dz.astype(BF16)
        da_ref[...] = lax.dot_general(dzb, w, (((1,), (1,)), ((), ())), preferred_element_type=F32).astype(BF16)
        dw_ref[...] += lax.dot_general(a, dzb, (((0,), (0,)), ((), ())), preferred_element_type=F32)
        db_ref[...] += jnp.sum(dz, axis=0, keepdims=True)

    return pl.pallas_call(
        body,
        name="gate_bwd",
        out_shape=[jax.ShapeDtypeStruct((s, LANE), BF16), jax.ShapeDtypeStruct((LANE, GLA_KEY_DIM), F32),
                   jax.ShapeDtypeStruct((1, GLA_KEY_DIM), F32)],
        grid=(s // ts,),
        in_specs=[pl.BlockSpec((ts, LANE), lambda i: (i, A_BLOCK)),
                  pl.BlockSpec((LANE, GLA_KEY_DIM), lambda i: (0, 0)),
                  pl.BlockSpec((1, GLA_KEY_DIM), lambda i: (0, 0)),
                  pl.BlockSpec((ts, GLA_KEY_DIM), lambda i: (i, 0))],
        out_specs=[pl.BlockSpec((ts, LANE), lambda i: (i, 0)),
                   pl.BlockSpec((LANE, GLA_KEY_DIM), lambda i: (0, 0)),
                   pl.BlockSpec((1, GLA_KEY_DIM), lambda i: (0, 0))],
        compiler_params=_params("arbitrary"),
    )(proj, w_a2p, b_a2, dla)


def _chunk_terms(q, k, la):
    c_len = GLA_CHUNK
    row = lax.broadcasted_iota(jnp.int32, (c_len, c_len), 0)
    col = lax.broadcasted_iota(jnp.int32, (c_len, c_len), 1)
    tri = row >= col
    c = jnp.dot(tri.astype(F32), la, preferred_element_type=F32, precision=lax.Precision.HIGHEST)
    last = jnp.sum(la, axis=0, keepdims=True)
    q_dec = q * (GLA_DK ** -0.5) * jnp.exp(c)
    k_inv = k * jnp.exp(-c)
    k_end = k * jnp.exp(last - c)
    return c, last, q_dec, k_inv, k_end, tri


def _dot(a, b, ca, cb):
    return lax.dot_general(a.astype(BF16), b.astype(BF16), (((ca,), (cb,)), ((), ())), preferred_element_type=F32)


def gla_fwd(proj, la, jobs=None):
    s = proj.shape[0]
    n_chunks = s // GLA_CHUNK

    def body(q_ref, k_ref, v_ref, la_ref, o_ref, st_out, st):
        @pl.when(pl.program_id(0) == 0)
        def _():
            st[...] = jnp.zeros_like(st)

        for h in range(GLA_HEADS):
            hk = slice(h * GLA_DK, (h + 1) * GLA_DK)
            hv = slice(h * GLA_DV, (h + 1) * GLA_DV)
            _, last, q_dec, k_inv, k_end, tri = _chunk_terms(q_ref[:, hk], k_ref[:, hk], la_ref[:, hk])
            v = v_ref[:, hv]
            a = jnp.where(tri, _dot(q_dec, k_inv, 1, 1), 0.0)
            state = st[h]
            st_out[h] = state
            o_ref[:, hv] = _dot(a, v, 1, 0) + _dot(q_dec, state, 1, 1)
            st[h] = state * jnp.exp(last) + _dot(v, k_end, 0, 0)

    key = lambda col: pl.BlockSpec((GLA_CHUNK, GLA_KEY_DIM), lambda n: (n, col))
    outs, bufs = _call(
        body, name="gla_fwd", jobs=jobs,
        out_shape=[jax.ShapeDtypeStruct((s, GLA_VAL_DIM), F32),
                   jax.ShapeDtypeStruct((GLA_HEADS, n_chunks, GLA_DV, GLA_DK), F32)],
        grid=(n_chunks,),
        in_specs=[key(0), key(1), pl.BlockSpec((GLA_CHUNK, GLA_VAL_DIM), lambda n: (n, 1)), key(0)],
        out_specs=[pl.BlockSpec((GLA_CHUNK, GLA_VAL_DIM), lambda n: (n, 0)),
                   pl.BlockSpec((GLA_HEADS, None, GLA_DV, GLA_DK), lambda n: (0, n, 0, 0))],
        scratch_shapes=[pltpu.VMEM((GLA_HEADS, GLA_DV, GLA_DK), F32)],
        args=[proj, proj, proj, la], sem=("arbitrary",))
    return outs if jobs is None else (outs, bufs)


def gla_bwd(proj, la, states, do, jobs=None):
    s = proj.shape[0]
    n_chunks = s // GLA_CHUNK
    lastc = n_chunks - 1

    def body(q_ref, k_ref, v_ref, la_ref, do_ref, st_ref, dq_ref, dk_ref, dv_ref, dla_ref, dst):
        @pl.when(pl.program_id(0) == 0)
        def _():
            dst[...] = jnp.zeros_like(dst)

        upper = (lax.broadcasted_iota(jnp.int32, (GLA_CHUNK, GLA_CHUNK), 0)
                 <= lax.broadcasted_iota(jnp.int32, (GLA_CHUNK, GLA_CHUNK), 1)).astype(F32)
        for h in range(GLA_HEADS):
            hk = slice(h * GLA_DK, (h + 1) * GLA_DK)
            hv = slice(h * GLA_DV, (h + 1) * GLA_DV)
            c, last, q_dec, k_inv, k_end, tri = _chunk_terms(q_ref[:, hk], k_ref[:, hk], la_ref[:, hk])
            v = v_ref[:, hv]
            dout = do_ref[:, hv]
            state = st_ref[h]
            dstate = dst[h]
            e_last = jnp.exp(last)
            a = jnp.where(tri, _dot(q_dec, k_inv, 1, 1), 0.0)
            da = jnp.where(tri, _dot(dout, v, 1, 1), 0.0)
            dv_ref[:, hv] = (_dot(a, dout, 0, 0) + _dot(k_end, dstate, 1, 1)).astype(BF16)
            dq_dec = _dot(da, k_inv, 1, 0) + _dot(dout, state, 1, 0)
            dk_inv = _dot(da, q_dec, 0, 0)
            dk_end = _dot(v, dstate, 1, 0)
            dst[h] = dstate * e_last + _dot(dout, q_dec, 0, 0)
            dq_ref[:, hk] = (dq_dec * (GLA_DK ** -0.5) * jnp.exp(c)).astype(BF16)
            dk_ref[:, hk] = (dk_inv * jnp.exp(-c) + dk_end * jnp.exp(last - c)).astype(BF16)
            ke_term = dk_end * k_end
            dc = dq_dec * q_dec - dk_inv * k_inv - ke_term
            dlast = (jnp.sum(ke_term, axis=0, keepdims=True)
                     + e_last * jnp.sum(dstate * state, axis=0, keepdims=True))
            dla_ref[:, hk] = jnp.dot(upper, dc, preferred_element_type=F32,
                                     precision=lax.Precision.HIGHEST) + dlast

    key = lambda col: pl.BlockSpec((GLA_CHUNK, GLA_KEY_DIM), lambda n: (lastc - n, col))
    val = lambda col: pl.BlockSpec((GLA_CHUNK, GLA_VAL_DIM), lambda n: (lastc - n, col))
    outs, bufs = _call(
        body, name="gla_bwd", jobs=jobs,
        out_shape=[jax.ShapeDtypeStruct((s, GLA_KEY_DIM), BF16), jax.ShapeDtypeStruct((s, GLA_KEY_DIM), BF16),
                   jax.ShapeDtypeStruct((s, GLA_VAL_DIM), BF16), jax.ShapeDtypeStruct((s, GLA_KEY_DIM), F32)],
        grid=(n_chunks,),
        in_specs=[key(0), key(1), val(1), key(0), val(0),
                  pl.BlockSpec((GLA_HEADS, None, GLA_DV, GLA_DK), lambda n: (0, lastc - n, 0, 0))],
        out_specs=[key(0), key(0), val(0), key(0)],
        scratch_shapes=[pltpu.VMEM((GLA_HEADS, GLA_DV, GLA_DK), F32)],
        args=[proj, proj, proj, la, do, states], sem=("arbitrary",))
    return outs if jobs is None else (outs, bufs)


R_BLOCK = (2 * GLA_KEY_DIM + GLA_VAL_DIM) // GLA_DV


def headnorm_fwd(o, proj, hn, *, ts=512):
    s = o.shape[0]

    def body(o_ref, r_ref, g_ref, out_ref):
        ov = o_ref[...]
        oh = ov * lax.rsqrt(jnp.mean(ov * ov, axis=-1, keepdims=True) + EPS)
        r = r_ref[...]
        out_ref[...] = (oh * g_ref[...] * (r * jax.nn.sigmoid(r))).astype(BF16)

    return pl.pallas_call(
        body,
        name="headnorm_fwd",
        out_shape=jax.ShapeDtypeStruct((s, GLA_VAL_DIM), BF16),
        grid=(s // ts, GLA_HEADS),
        in_specs=[pl.BlockSpec((ts, GLA_DV), lambda i, h: (i, h)),
                  pl.BlockSpec((ts, GLA_DV), lambda i, h: (i, R_BLOCK + h)),
                  pl.BlockSpec((1, GLA_DV), lambda i, h: (0, 0))],
        out_specs=pl.BlockSpec((ts, GLA_DV), lambda i, h: (i, h)),
        compiler_params=_params("parallel", "parallel"),
    )(o, proj, hn)


def headnorm_bwd(o, proj, hn, dog, *, ts=512):
    s = o.shape[0]

    def body(o_ref, r_ref, g_ref, dog_ref, do_ref, dr_ref, dg_ref):
        @pl.when((pl.program_id(0) == 0) & (pl.program_id(1) == 0))
        def _():
            dg_ref[...] = jnp.zeros_like(dg_ref)

        ov = o_ref[...]
        rr = lax.rsqrt(jnp.mean(ov * ov, axis=-1, keepdims=True) + EPS)
        oh = ov * rr
        g = g_ref[...]
        r = r_ref[...]
        sig = jax.nn.sigmoid(r)
        gate = r * sig
        dog_v = dog_ref[...]
        d_on = dog_v * gate
        dr_ref[...] = (dog_v * (oh * g) * (sig * (1.0 + r * (1.0 - sig)))).astype(BF16)
        dg_ref[...] += jnp.sum(d_on * oh, axis=0, keepdims=True)
        doh = d_on * g
        do_ref[...] = rr * (doh - oh * jnp.mean(doh * oh, axis=-1, keepdims=True))

    return pl.pallas_call(
        body,
        name="headnorm_bwd",
        out_shape=[jax.ShapeDtypeStruct((s, GLA_VAL_DIM), F32), jax.ShapeDtypeStruct((s, GLA_VAL_DIM), BF16),
                   jax.ShapeDtypeStruct((1, GLA_DV), F32)],
        grid=(s // ts, GLA_HEADS),
        in_specs=[pl.BlockSpec((ts, GLA_DV), lambda i, h: (i, h)),
                  pl.BlockSpec((ts, GLA_DV), lambda i, h: (i, R_BLOCK + h)),
                  pl.BlockSpec((1, GLA_DV), lambda i, h: (0, 0)),
                  pl.BlockSpec((ts, GLA_DV), lambda i, h: (i, h))],
        out_specs=[pl.BlockSpec((ts, GLA_DV), lambda i, h: (i, h)),
                   pl.BlockSpec((ts, GLA_DV), lambda i, h: (i, h)),
                   pl.BlockSpec((1, GLA_DV), lambda i, h: (0, 0))],
        compiler_params=_params("arbitrary", "arbitrary"),
    )(o, proj, hn, dog)


CONV_TC = 128
SQRT_HALF = 0.7071067811865476
INV_SQRT_2PI = 0.3989422804014327


def _conv_gate(g_ref, cw_ref, cb_ref):
    g0 = g_ref[...].astype(F32)
    t = lax.broadcasted_iota(jnp.int32, g0.shape, 0)
    g1 = jnp.where(t >= 1, pltpu.roll(g0, 1, 0), 0.0)
    g2 = jnp.where(t >= 2, pltpu.roll(g0, 2, 0), 0.0)
    gc = cw_ref[0:1, :] * g2 + cw_ref[1:2, :] * g1 + cw_ref[2:3, :] * g0 + cb_ref[...]
    return g0, g1, g2, gc, t


def convglu_fwd(up, conv_w, conv_b, *, name, jobs=None):
    s = up.shape[0]
    nc = D_FF // CONV_TC

    def body(u_ref, g_ref, cw_ref, cb_ref, o_ref):
        _, _, _, gc, _ = _conv_gate(g_ref, cw_ref, cb_ref)
        gelu = 0.5 * gc * (1.0 + lax.erf(gc * SQRT_HALF))
        o_ref[...] = (gelu * u_ref[...].astype(F32)).astype(BF16)

    (out,), bufs = _call(
        body, name=name, jobs=jobs,
        out_shape=[jax.ShapeDtypeStruct((s, D_FF), BF16)],
        grid=(nc,),
        in_specs=[pl.BlockSpec((s, CONV_TC), lambda c: (0, c)),
                  pl.BlockSpec((s, CONV_TC), lambda c: (0, nc + c)),
                  pl.BlockSpec((3, CONV_TC), lambda c: (0, c)),
                  pl.BlockSpec((1, CONV_TC), lambda c: (0, c))],
        out_specs=[pl.BlockSpec((s, CONV_TC), lambda c: (0, c))],
        args=[up, up, conv_w, conv_b], sem=("parallel",))
    return out if jobs is None else (out, bufs)


def convglu_bwd(up, conv_w, conv_b, dact, *, name, jobs=None):
    s = up.shape[0]
    nc = D_FF // CONV_TC

    def body(u_ref, g_ref, cw_ref, cb_ref, da_ref, dup_ref, dcw_ref, dcb_ref):
        du_ref, dg_ref = dup_ref.at[0], dup_ref.at[1]
        g0, g1, g2, gc, t = _conv_gate(g_ref, cw_ref, cb_ref)
        cdf = 0.5 * (1.0 + lax.erf(gc * SQRT_HALF))
        da = da_ref[...].astype(F32)
        du_ref[...] = (da * gc * cdf).astype(BF16)
        dgc = da * u_ref[...].astype(F32) * (cdf + gc * jnp.exp(-0.5 * gc * gc) * INV_SQRT_2PI)
        dcb_ref[...] = jnp.sum(dgc, axis=0, keepdims=True)
        dcw_ref[0:1, :] = jnp.sum(dgc * g2, axis=0, keepdims=True)
        dcw_ref[1:2, :] = jnp.sum(dgc * g1, axis=0, keepdims=True)
        dcw_ref[2:3, :] = jnp.sum(dgc * g0, axis=0, keepdims=True)
        n1 = jnp.where(t < s - 1, pltpu.roll(dgc, s - 1, 0), 0.0)
        n2 = jnp.where(t < s - 2, pltpu.roll(dgc, s - 2, 0), 0.0)
        dg_ref[...] = (cw_ref[2:3, :] * dgc + cw_ref[1:2, :] * n1 + cw_ref[0:1, :] * n2).astype(BF16)

    col = pl.BlockSpec((s, CONV_TC), lambda c: (0, c))
    outs, bufs = _call(
        body, name=name, jobs=jobs,
        out_shape=[jax.ShapeDtypeStruct((2, s, D_FF), BF16),
                   jax.ShapeDtypeStruct((3, D_FF), F32), jax.ShapeDtypeStruct((1, D_FF), F32)],
        grid=(nc,),
        in_specs=[col, pl.BlockSpec((s, CONV_TC), lambda c: (0, nc + c)),
                  pl.BlockSpec((3, CONV_TC), lambda c: (0, c)),
                  pl.BlockSpec((1, CONV_TC), lambda c: (0, c)), col],
        out_specs=[pl.BlockSpec((2, s, CONV_TC), lambda c: (0, 0, c)), pl.BlockSpec((3, CONV_TC), lambda c: (0, c)),
                   pl.BlockSpec((1, CONV_TC), lambda c: (0, c))],
        args=[up, up, conv_w, conv_b, dact], sem=("parallel",))
    return outs if jobs is None else (outs, bufs)


REGION = ATT_BLOCK * DILATIONS[-1]
SLOPE_TILE = (8, LANE)


def _slope_table():
    return jnp.broadcast_to(jnp.asarray(ALIBI_SLOPES, F32)[:, None, None], (ATT_HEADS,) + SLOPE_TILE)


def _sub(r, i, d):
    start = r + d * ATT_BLOCK * i
    return pl.ds(start, ATT_BLOCK) if d == 1 else pl.ds(start, ATT_BLOCK, stride=d)


def _att_bias(slope, d, first_key):
    qa = lax.broadcasted_iota(jnp.int32, (ATT_BLOCK, 2 * ATT_BLOCK), 0)
    cc = lax.broadcasted_iota(jnp.int32, (ATT_BLOCK, 2 * ATT_BLOCK), 1)
    dist = qa - cc + ATT_BLOCK
    ok = (dist >= 0) & (dist <= ATT_BLOCK) & (cc >= first_key)
    return jnp.where(ok, (slope * (-float(d))) * dist.astype(F32), NEG)


def _keys(kc_ref, kp_ref, r, i, d, nsub):
    prev = kp_ref[_sub(r, nsub - 1, d), :] if i == 0 else kc_ref[_sub(r, i - 1, d), :]
    return jnp.concatenate([prev, kc_ref[_sub(r, i, d), :]], axis=0)


def _per_residue(d, body):
    for r in range(d):
        body(r)


def attn_fwd(q, kv, jobs=None):
    s = q.shape[0]
    nreg = s // REGION
    scale = HEAD_DIM ** -0.5

    def body(sl_ref, q_ref, kc_ref, kp_ref, vc_ref, vp_ref, o_ref, lse_ref, ob, lb):
        n = pl.program_id(0)
        g = pl.program_id(2)
        slope = sl_ref[0:1, 0:1]
        first_key = jnp.where(n > 0, 0, ATT_BLOCK)

        def branch(gi, d):
            nsub = REGION // (ATT_BLOCK * d)
            bias = _att_bias(slope, d, 0)
            bias0 = _att_bias(slope, d, first_key)

            def residue(r):
                for i in range(nsub):
                    rows = _sub(r, i, d)
                    kcat = _keys(kc_ref, kp_ref, r, i, d, nsub)
                    vcat = _keys(vc_ref, vp_ref, r, i, d, nsub)
                    sc = _dot(q_ref[rows, :], kcat, 1, 1) * scale + (bias0 if i == 0 else bias)
                    m = jnp.max(sc, axis=-1, keepdims=True)
                    p = jnp.exp(sc - m)
                    l = jnp.sum(p, axis=-1, keepdims=True)
                    ob.at[gi][rows, :] = _dot(p, vcat, 1, 0) / l
                    lb.at[gi][rows, :] = jnp.broadcast_to(m + jnp.log(l), (ATT_BLOCK, HEAD_DIM))

            _per_residue(d, residue)

        for gi, d in enumerate(DILATIONS):
            @pl.when(g == gi)
            def _():
                branch(gi, d)

        @pl.when(g == len(DILATIONS) - 1)
        def _():
            def merge(c, carry):
                rows = pl.ds(pl.multiple_of(c * ATT_BLOCK, ATT_BLOCK), ATT_BLOCK)
                l0, l1, l2 = lb[0, rows, :], lb[1, rows, :], lb[2, rows, :]
                m = jnp.maximum(jnp.maximum(l0, l1), l2)
                e0, e1, e2 = jnp.exp(l0 - m), jnp.exp(l1 - m), jnp.exp(l2 - m)
                den = e0 + e1 + e2
                o_ref[rows, :] = (e0 * ob[0, rows, :] + e1 * ob[1, rows, :] + e2 * ob[2, rows, :]) / den
                lse_ref[rows, :] = m + jnp.log(den)
                return carry
            lax.fori_loop(0, REGION // ATT_BLOCK, merge, 0)

    def blk(col, prev=False):
        if prev:
            return pl.BlockSpec((REGION, HEAD_DIM), lambda n, h, g: (jnp.maximum(n - 1, 0), col(h, g)))
        return pl.BlockSpec((REGION, HEAD_DIM), lambda n, h, g: (n, col(h, g)))

    k_col = lambda h, g: h
    v_col = lambda h, g: ATT_HEADS + h
    outs, bufs = _call(
        body, name="attn_fwd", jobs=jobs,
        out_shape=[jax.ShapeDtypeStruct((s, ATT_HEADS * HEAD_DIM), F32)] * 2,
        grid=(nreg, ATT_HEADS, len(DILATIONS)),
        in_specs=[pl.BlockSpec((None,) + SLOPE_TILE, lambda n, h, g: (h, 0, 0)),
                  blk(lambda h, g: g * ATT_HEADS + h), blk(k_col), blk(k_col, True), blk(v_col), blk(v_col, True)],
        out_specs=[blk(k_col), blk(k_col)],
        scratch_shapes=[pltpu.VMEM((len(DILATIONS), REGION, HEAD_DIM), F32)] * 2,
        args=[_slope_table(), q, kv, kv, kv, kv], sem=("parallel", "parallel", "arbitrary"))
    return outs if jobs is None else (outs, bufs)


def attn_bwd(q, kv, o, lse, do, jobs=None):
    s = q.shape[0]
    nreg = s // REGION
    scale = HEAD_DIM ** -0.5

    def body(sl_ref, q_ref, kc_ref, kp_ref, vc_ref, vp_ref, o_ref, lse_ref, do_ref,
             qn_ref, on_ref, lsen_ref, don_ref, dq_ref, dkv_ref, dlt, dltn):
        n = pl.program_id(0)
        g = pl.program_id(2)
        slope = sl_ref[0:1, 0:1]
        first_key = jnp.where(n > 0, 0, ATT_BLOCK)
        has_next = n + 1 < nreg

        @pl.when(g == 0)
        def _():
            dkv_ref[...] = jnp.zeros_like(dkv_ref)

            def deltas(c, carry):
                rows = pl.ds(pl.multiple_of(c * ATT_BLOCK, ATT_BLOCK), ATT_BLOCK)
                dlt[rows, :] = jnp.sum(do_ref[rows, :] * o_ref[rows, :], axis=-1, keepdims=True)
                dltn[rows, :] = jnp.sum(don_ref[rows, :] * on_ref[rows, :], axis=-1, keepdims=True)
                return carry
            lax.fori_loop(0, REGION // ATT_BLOCK, deltas, 0)

        def branch(d):
            nsub = REGION // (ATT_BLOCK * d)
            bias = _att_bias(slope, d, 0)
            bias0 = _att_bias(slope, d, first_key)

            def residue(r):
                for i in range(nsub):
                    rows = _sub(r, i, d)
                    kcat = _keys(kc_ref, kp_ref, r, i, d, nsub)
                    vcat = _keys(vc_ref, vp_ref, r, i, d, nsub)
                    qb = q_ref[rows, :]
                    dob = do_ref[rows, :]
                    sc = _dot(qb, kcat, 1, 1) * scale + (bias0 if i == 0 else bias)
                    p = jnp.exp(sc - lse_ref[rows, :][:, 0:1])
                    ds = p * (_dot(dob, vcat, 1, 1) - dlt[rows, :])
                    dq_ref[rows, :] = _dot(ds, kcat, 1, 0) * scale
                    dk = _dot(ds, qb, 0, 0) * scale
                    dv = _dot(p, dob, 0, 0)
                    dkv_ref.at[0][rows, :] += dk[ATT_BLOCK:]
                    dkv_ref.at[1][rows, :] += dv[ATT_BLOCK:]
                    if i > 0:
                        prev = _sub(r, i - 1, d)
                        dkv_ref.at[0][prev, :] += dk[:ATT_BLOCK]
                        dkv_ref.at[1][prev, :] += dv[:ATT_BLOCK]

            _per_residue(d, residue)

            @pl.when(has_next)
            def _():
                bias_prev = bias[:, :ATT_BLOCK]

                def residue_next(r):
                    last = _sub(r, nsub - 1, d)
                    first = _sub(r, 0, d)
                    qb = qn_ref[first, :]
                    dob = don_ref[first, :]
                    sc = _dot(qb, kc_ref[last, :], 1, 1) * scale + bias_prev
                    p = jnp.exp(sc - lsen_ref[first, :][:, 0:1])
                    ds = p * (_dot(dob, vc_ref[last, :], 1, 1) - dltn[first, :])
                    dkv_ref.at[0][last, :] += _dot(ds, qb, 0, 0) * scale
                    dkv_ref.at[1][last, :] += _dot(p, dob, 0, 0)

                _per_residue(d, residue_next)

        for gi, d in enumerate(DILATIONS):
            @pl.when(g == gi)
            def _():
                branch(d)

    last_reg = nreg - 1

    def blk(col, shift=0):
        if shift < 0:
            return pl.BlockSpec((REGION, HEAD_DIM), lambda n, h, g: (jnp.maximum(n - 1, 0), col(h, g)))
        if shift > 0:
            return pl.BlockSpec((REGION, HEAD_DIM), lambda n, h, g: (jnp.minimum(n + 1, last_reg), col(h, g)))
        return pl.BlockSpec((REGION, HEAD_DIM), lambda n, h, g: (n, col(h, g)))

    q_col = lambda h, g: g * ATT_HEADS + h
    k_col = lambda h, g: h
    v_col = lambda h, g: ATT_HEADS + h
    outs, bufs = _call(
        body, name="attn_bwd", jobs=jobs,
        out_shape=[jax.ShapeDtypeStruct(q.shape, F32), jax.ShapeDtypeStruct((2, s, ATT_HEADS * HEAD_DIM), F32)],
        grid=(nreg, ATT_HEADS, len(DILATIONS)),
        in_specs=[pl.BlockSpec((None,) + SLOPE_TILE, lambda n, h, g: (h, 0, 0)),
                  blk(q_col), blk(k_col), blk(k_col, -1), blk(v_col), blk(v_col, -1),
                  blk(k_col), blk(k_col), blk(k_col),
                  blk(q_col, 1), blk(k_col, 1), blk(k_col, 1), blk(k_col, 1)],
        out_specs=[blk(q_col), pl.BlockSpec((2, REGION, HEAD_DIM), lambda n, h, g: (0, n, h))],
        scratch_shapes=[pltpu.VMEM((REGION, 1), F32)] * 2,
        args=[_slope_table(), q, kv, kv, kv, kv, o, lse, do, q, o, lse, do],
        sem=("parallel", "parallel", "arbitrary"))
    return outs if jobs is None else (outs, bufs)


def _adam(w, g, m, v):
    m = ADAM_B1 * m + (1.0 - ADAM_B1) * g
    v = ADAM_B2 * v + (1.0 - ADAM_B2) * (g * g)
    m_hat = m / (1.0 - ADAM_B1 ** ADAM_STEP)
    v_hat = v / (1.0 - ADAM_B2 ** ADAM_STEP)
    delta = -ADAM_LR * (m_hat / (jnp.sqrt(v_hat) + ADAM_EPS) + ADAM_WD * w)
    return delta, m, v


def adam_sharded(recvs, w, m, v, *, name):
    layers = len(recvs)
    n_src, r, c = recvs[0].shape
    tr = _rows(r, c)

    def body(*refs):
        p_refs = refs[:layers]
        w_ref, m_ref, v_ref, g_ref, d_ref, mo_ref, vo_ref = refs[layers:]
        for layer, p_ref in enumerate(p_refs):
            @pl.when(pl.program_id(0) == layer)
            def _():
                g = p_ref[0].astype(F32)
                for src in range(1, n_src):
                    g = g + p_ref[src].astype(F32)
                delta, m_new, v_new = _adam(w_ref[...], g, m_ref[...], v_ref[...])
                g_ref[...] = g
                d_ref[...] = delta
                mo_ref[...] = m_new
                vo_ref[...] = v_new

    blk = pl.BlockSpec((None, tr, c), lambda l, i: (l, i, 0))
    out = jax.ShapeDtypeStruct((layers, r, c), F32)
    part = [pl.BlockSpec((n_src, tr, c), functools.partial(lambda l, i, layer: (0, jnp.where(l == layer, i, 0), 0),
                                                            layer=layer)) for layer in range(layers)]
    return pl.pallas_call(
        body,
        name=name,
        out_shape=[out] * 4,
        grid=(layers, r // tr),
        in_specs=part + [blk, blk, blk],
        out_specs=[blk] * 4,
        compiler_params=_params("parallel", "parallel"),
    )(*recvs, w, m, v)


def sum_partials(parts):
    n_src, r, c = parts.shape

    def body(p_ref, o_ref):
        g = p_ref[0]
        for src in range(1, n_src):
            g = g + p_ref[src]
        o_ref[...] = g

    return pl.pallas_call(
        body,
        name="sum_small_grads",
        out_shape=jax.ShapeDtypeStruct((r, c), F32),
    )(parts)


def adam_packed(w, g, m, v):
    def body(w_ref, g_ref, m_ref, v_ref, d_ref, mo_ref, vo_ref):
        delta, m_new, v_new = _adam(w_ref[...], g_ref[...], m_ref[...], v_ref[...])
        d_ref[...] = delta
        mo_ref[...] = m_new
        vo_ref[...] = v_new

    out = jax.ShapeDtypeStruct(w.shape, F32)
    return pl.pallas_call(body, name="adam_small", out_shape=[out] * 3)(w, g, m, v)


def all_gather(srcs, *, name):
    n = len(srcs)

    def body(*refs):
        src, dst = refs[:n], refs[n:2 * n]
        send_sems, recv_sems, local_sems = refs[2 * n:]
        x, y, c, me = _place()
        sibling = (x, y, 1 - c)
        chips = [(1 - x, y), (x, 1 - y), (1 - x, 1 - y)]

        def index(px, py, pc):
            return 4 * px + 2 * py + pc

        def copy(p, k, block, to, from_src=False):
            slot = dst[p].at[index(*block)]
            return pltpu.make_async_remote_copy(
                src_ref=src[p] if from_src else slot, dst_ref=slot,
                send_sem=send_sems.at[p, k], recv_sem=recv_sems.at[p, k],
                device_id=to, device_id_type=MESH)

        mine = [pltpu.make_async_copy(src[p], dst[p].at[me], local_sems.at[p]) for p in range(n)]
        for cp in mine:
            cp.start()
        first = []
        for p in range(n):
            first.append(copy(p, 0, (x, y, c), sibling, from_src=True))
            for jj, chip in enumerate(chips):
                first.append(copy(p, 1 + jj, (x, y, c), (*chip, c), from_src=True))
        for cp in first:
            cp.start()
        passed = []
        for jj, chip in enumerate(chips):
            for p in range(n):
                copy(p, 1 + jj, (*chip, c), (x, y, c)).wait_recv()
                fwd = copy(p, 4 + jj, (*chip, c), sibling)
                fwd.start()
                passed.append(fwd)
        for p in range(n):
            copy(p, 0, sibling, (x, y, c)).wait_recv()
            for jj, chip in enumerate(chips):
                copy(p, 4 + jj, (*chip, 1 - c), (x, y, c)).wait_recv()
        for cp in first + passed:
            cp.wait_send()
        for cp in mine:
            cp.wait()

    return pl.pallas_call(
        body,
        name=name,
        out_shape=[jax.ShapeDtypeStruct((N_DEV,) + a.shape, a.dtype) for a in srcs],
        in_specs=[ANY] * n,
        out_specs=[ANY] * n,
        scratch_shapes=[pltpu.SemaphoreType.DMA((n, 7)), pltpu.SemaphoreType.DMA((n, 7)),
                        pltpu.SemaphoreType.DMA((n,))],
    )(*srcs)


def exchange_only(*, name, jobs):
    def body(o_ref):
        o_ref[...] = jnp.zeros_like(o_ref)

    _, bufs = _call(body, name=name, jobs=jobs, out_shape=[jax.ShapeDtypeStruct((8, LANE), F32)], grid=(1,),
                    in_specs=[], out_specs=[pl.BlockSpec((8, LANE), lambda i: (0, 0))], args=[], sem=("arbitrary",))
    return None, bufs


def _pack_rows(parts, rows):
    flat = jnp.concatenate([p.reshape(-1) for p in parts])
    return jnp.pad(flat, (0, rows * LANE - flat.shape[0])).reshape(rows, LANE)


def _unpack_rows(packed, shapes):
    flat = packed.reshape(-1)
    out, at = [], 0
    for sh in shapes:
        size = 1
        for dim in sh:
            size *= dim
        out.append(flat[at:at + size].reshape(sh))
        at += size
    return out


CONV_W_PAD = 768
SMALL_W_ROWS = 56


def _pack_small_weights(w_a2, b_a2, hn, conv_w):
    cw = jnp.pad(conv_w.reshape(6, -1), ((0, 0), (0, CONV_W_PAD - conv_w.shape[-1]))).reshape(-1, LANE)
    rows = jnp.concatenate([w_a2[0], b_a2, jnp.pad(hn, ((0, 0), (0, LANE - hn.shape[-1]))), cw], axis=0)
    return jnp.pad(rows, ((0, SMALL_W_ROWS - rows.shape[0]), (0, 0)))


def _unpack_small_weights(gathered):
    w_a2 = gathered[:, 0:GATE_RANK, :].transpose(1, 0, 2).reshape(GATE_RANK, GLA_KEY_DIM)
    b_a2 = gathered[:, GATE_RANK, :].reshape(1, GLA_KEY_DIM)
    hn = gathered[:, GATE_RANK + 1, :GLA_DV // N_DEV].reshape(1, GLA_DV)
    per = D_FF // N_DEV
    cw = gathered[:, GATE_RANK + 2:GATE_RANK + 2 + 6 * CONV_W_PAD // LANE, :].reshape(N_DEV, 6, CONV_W_PAD)[:, :, :per]
    cw = cw.reshape(N_DEV, 2, 3, per).transpose(1, 2, 0, 3).reshape(2, 3, D_FF)
    return w_a2, b_a2, hn, cw


SCHEDULE = {
    "gla_in": [("g1", "gout", None), ("g1", "up0", (0, 1024))],
    "gla_fwd": [("g2", "gout", None), ("g2", "up0", (0, 1024)), ("g1", "up0", (1024, 2048))],
    "gla_out": [("g2", "up0", (1024, 2048)), ("g1", "dn0", (0, 352))],
    "ffn_up0": [("g2", "dn0", (0, 352)), ("g1", "dn0", (352, 704)), ("g1", "kv", None), ("g1", "q", (0, 768))],
    "convglu_fwd0": [("g2", "dn0", (352, 704))],
    "ffn_down0": [("g2", "kv", None), ("g2", "q", (0, 768)), ("g1", "q", (768, 2048)), ("g1", "dout", None)],
    "kv_proj": [("g2", "q", (768, 2048)), ("g2", "dout", None), ("g1", "up1", (0, 704))],
    "q_proj": [("g2", "up1", (0, 704)), ("g1", "up1", (704, 1664))],
    "attn_fwd": [("g2", "up1", (704, 1664)), ("g1", "up1", (1664, 2048)), ("g1", "dn1", None)],
    "dsa_out": [("g2", "up1", (1664, 2048)), ("g2", "dn1", None)],
    "ffn_down_dx1": [("sc", "dn1", (0, 352))],
    "convglu_bwd1": [("sc", "dn1", (352, 704))],
    "ffn_up_dx1": [("sc", "up1", (0, 1024))],
    "attn_bwd": [("sc", "up1", (1024, 2048)), ("sc", "dout", None)],
    "q_proj_dx": [("sc", "q", (0, 1024))],
    "kv_proj_dw": [("sc", "q", (1024, 1792))],
    "kv_proj_dx": [("sc", "q", (1792, 2048)), ("sc", "kv", (0, 768))],
    "ffn_down_dw0": [("sc", "kv", (768, 2048))],
    "ffn_down_dx0": [("sc", "dn0", (0, 384))],
    "convglu_bwd0": [("sc", "dn0", (384, 704))],
    "ffn_up_dx0": [("sc", "up0", (0, 1024))],
    "gla_out_dw": [("sc", "up0", (1024, 1216))],
    "gla_out_dx": [("sc", "up0", (1216, 1408))],
    "gla_bwd": [("sc", "up0", (1408, 2048)), ("sc", "gout", (0, 128))],
    "gla_in_dw": [("sc", "gout", (128, 256))],
    "gla_in_dx": [("sc", "in", (0, 1536))],
    "grads_tail": [("sc", "in", (1536, 2048))],
}
ROW_SHARDED = ("gout", "dout", "dn0", "dn1")


class Plan:
    def __init__(self, weights, srcs=None):
        self.w = dict(weights)
        self.srcs = srcs
        self.grads = {}
        self.recv = {}
        self._names = None

    def weight(self, name):
        buf = self.w[name]
        if name in ROW_SHARDED:
            return buf.reshape(1, buf.shape[0] * buf.shape[1], buf.shape[2])
        return buf

    def jobs(self, call):
        ops = SCHEDULE.get(call)
        if self.srcs is None or not ops:
            return None
        jobs, handles = Jobs(), {}
        for op, name, rows in ops:
            store = self.recv if op == "sc" else self.w
            if name not in handles:
                if name in store:
                    handles[name] = jobs.thru(store[name])
                elif op == "sc":
                    handles[name] = jobs.new(self.grads[name].shape, BF16)
                else:
                    handles[name] = jobs.new((N_DEV,) + self.srcs[name].shape, BF16)
            if op == "g1":
                jobs.gather_ici(self.srcs[name], handles[name], rows)
            elif op == "g2":
                jobs.gather_d2d(handles[name], rows)
            else:
                jobs.scatter(self.grads[name], handles[name], rows)
        self._names = [(name, self.recv if ops[0][0] == "sc" else self.w) for name in handles]
        assert len({op == "sc" for op, _, _ in ops}) == 1
        return jobs

    def run(self, call, fn, *args, **kwargs):
        jobs = self.jobs(call)
        if jobs is None:
            return fn(*args, **kwargs)
        out, bufs = fn(*args, jobs=jobs, **kwargs)
        for (name, store), buf in zip(self._names, bufs):
            store[name] = buf
        return out


def _ffn_fwd(plan, h, norm_g, conv_w, conv_b, tag):
    (n,) = rms_fwd(h, [norm_g], name=f"ffn_norm_fwd{tag}")
    up = plan.run(f"ffn_up{tag}", mm_nn, n, plan.weight(f"up{tag}"), out_dtype=BF16, name=f"ffn_up{tag}")
    act = plan.run(f"convglu_fwd{tag}", convglu_fwd, up, conv_w, conv_b, name=f"convglu_fwd{tag}")
    h_out = plan.run(f"ffn_down{tag}", mm_nn, act, plan.weight(f"dn{tag}"), out_dtype=F32, res=h,
                     name=f"ffn_down{tag}")
    return h_out, (n, up, act)


def _by_rows(dw):
    return dw.reshape(N_DEV, dw.shape[1] // N_DEV, dw.shape[2])


def _ffn_bwd(plan, dh_out, h, saved, norm_g, conv_w, conv_b, tag):
    n, up, act = saved
    plan.grads[f"dn{tag}"] = _by_rows(plan.run(f"ffn_down_dw{tag}", mm_tn, act, dh_out, 1, name=f"ffn_down_dw{tag}"))
    dact = plan.run(f"ffn_down_dx{tag}", mm_nt, dh_out, plan.weight(f"dn{tag}"), out_dtype=BF16,
                    name=f"ffn_down_dx{tag}")
    dup, dconv_w, dconv_b = plan.run(f"convglu_bwd{tag}", convglu_bwd, up, conv_w, conv_b, dact,
                                     name=f"convglu_bwd{tag}")
    plan.grads[f"up{tag}"] = mm_tn(n, dup, N_DEV, name=f"ffn_up_dw{tag}")
    dh, (dnorm,) = plan.run(f"ffn_up_dx{tag}", mm_nt, dup, plan.weight(f"up{tag}"), out_dtype=F32,
                            name=f"ffn_up_dx{tag}", norm=(h, dh_out, [norm_g], []))
    return dh, dnorm, dconv_w, dconv_b


def local_step(x, target, wts, plan):
    row = lambda v: v.reshape(1, -1)
    attn_norm, ffn_norm = wts["attn_norm"], wts["ffn_norm"]
    conv_w, conv_b = wts["ffn_conv_w"], wts["ffn_conv_b"]

    (n1,) = rms_fwd(x, [row(attn_norm[0])], name="attn_norm_fwd0")
    proj = plan.run("gla_in", mm_nn, n1, wts["gla_w_in"], out_dtype=F32, name="gla_in")
    la = gate_fwd(proj, wts["gla_w_a2"], wts["gla_b_a2"])
    o_gla, states = plan.run("gla_fwd", gla_fwd, proj, la)
    og = headnorm_fwd(o_gla, proj, wts["gla_head_norm"])
    h1 = plan.run("gla_out", mm_nn, og, plan.weight("gout"), out_dtype=F32, res=x, name="gla_out")
    h2, ffn0 = _ffn_fwd(plan, h1, row(ffn_norm[0]), conv_w[0], row(conv_b[0]), "0")

    kvn, n3 = rms_fwd(h2, [row(wts["kv_norm"]), row(attn_norm[1])], name="kv_attn_norm_fwd")
    kv = plan.run("kv_proj", mm_nn, kvn, plan.weight("kv"), out_dtype=F32, name="kv_proj")
    q = plan.run("q_proj", mm_nn, n3, plan.weight("q"), out_dtype=F32, name="q_proj")
    o_att, lse = plan.run("attn_fwd", attn_fwd, q, kv)
    h3 = plan.run("dsa_out", mm_nn, o_att, plan.weight("dout"), out_dtype=F32, res=h2, name="dsa_out")
    h4, ffn1 = _ffn_fwd(plan, h3, row(ffn_norm[1]), conv_w[1], row(conv_b[1]), "1")

    loss_tile, dh4, d_final = loss_head(h4, row(wts["final_norm"]), target)

    dh3, d_ffn1, dcw1, dcb1 = _ffn_bwd(plan, dh4, h3, ffn1, row(ffn_norm[1]), conv_w[1], row(conv_b[1]), "1")
    plan.grads["dout"] = _by_rows(mm_tn(o_att, dh3, 1, name="dsa_out_dw"))
    do_att = mm_nt(dh3, plan.weight("dout"), out_dtype=F32, name="dsa_out_dx")
    dq, dkv = plan.run("attn_bwd", attn_bwd, q, kv, o_att, lse, do_att)
    plan.grads["q"] = mm_tn(n3, dq, N_DEV, name="q_proj_dw")
    dn3 = plan.run("q_proj_dx", mm_nt, dq, plan.weight("q"), out_dtype=F32, name="q_proj_dx")
    plan.grads["kv"] = plan.run("kv_proj_dw", mm_tn, kvn, dkv, N_DEV, name="kv_proj_dw")
    dh2, (d_kvnorm, d_attn1) = plan.run("kv_proj_dx", mm_nt, dkv, plan.weight("kv"), out_dtype=F32, name="kv_proj_dx",
                                        norm=(h2, dh3, [row(wts["kv_norm"]), row(attn_norm[1])], [dn3]))
    dh1, d_ffn0, dcw0, dcb0 = _ffn_bwd(plan, dh2, h1, ffn0, row(ffn_norm[0]), conv_w[0], row(conv_b[0]), "0")
    plan.grads["gout"] = _by_rows(plan.run("gla_out_dw", mm_tn, og, dh1, 1, name="gla_out_dw"))
    dog = plan.run("gla_out_dx", mm_nt, dh1, plan.weight("gout"), out_dtype=F32, name="gla_out_dx")
    do_gla, dr, d_hn = headnorm_bwd(o_gla, proj, wts["gla_head_norm"], dog)
    dq_g, dk_g, dv_g, dla = plan.run("gla_bwd", gla_bwd, proj, la, states, do_gla)
    da, dw_a2p, db_a2 = gate_bwd(proj, wts["gla_w_a2"], wts["gla_b_a2"], dla)
    dproj = jnp.concatenate([dq_g, dk_g, dv_g, dr, da], axis=1)
    assert dproj.shape[1] == GLA_IN_PAD
    dw_in = plan.run("gla_in_dw", mm_tn, n1, dproj, 1, name="gla_in_dw")
    plan.grads["in"] = dw_in[0, :, :GLA_IN_DIM].reshape(D_MODEL, N_DEV, GLA_IN_DIM // N_DEV).transpose(1, 0, 2)
    grad_x, (d_attn0,) = plan.run("gla_in_dx", mm_nt, dproj, wts["gla_w_in"], out_dtype=F32, name="gla_in_dx",
                                  norm=(x, dh1, [row(attn_norm[0])], []))

    small = dict(
        attn_norm=jnp.concatenate([d_attn0, d_attn1], axis=0),
        ffn_norm=jnp.concatenate([d_ffn0, d_ffn1], axis=0),
        kv_norm=d_kvnorm.reshape(-1),
        final_norm=d_final.reshape(-1),
        ffn_conv_b=jnp.concatenate([dcb0, dcb1], axis=0),
        gla_w_a2=dw_a2p[:GATE_RANK],
        gla_b_a2=db_a2,
        gla_head_norm=d_hn,
        ffn_conv_w=jnp.stack([dcw0, dcw1]),
    )
    return loss_tile, grad_x, small


SMALL_ORDER = ("attn_norm", "ffn_norm", "kv_norm", "final_norm", "ffn_conv_b",
               "gla_w_a2", "gla_b_a2", "gla_head_norm", "ffn_conv_w")
SMALL_FULL = dict(attn_norm=(2, D_MODEL), ffn_norm=(2, D_MODEL), kv_norm=(D_MODEL,), final_norm=(D_MODEL,),
                  ffn_conv_b=(2, D_FF), gla_w_a2=(GATE_RANK, GLA_KEY_DIM), gla_b_a2=(1, GLA_KEY_DIM),
                  gla_head_norm=(1, GLA_DV), ffn_conv_w=(2, 3, D_FF))
SMALL_SHARDED = ("gla_w_a2", "gla_b_a2", "gla_head_norm", "ffn_conv_w")
SMALL_GRAD_ROWS = 592
SMALL_ADAM_ROWS = 240


def kernel(x, attn_norm, gla_w_in, gla_w_a2, gla_b_a2, gla_head_norm, gla_w_out, kv_norm, w_kv, dsa_w_q, dsa_w_out, ffn_norm, ffn_w_up, ffn_conv_w, ffn_conv_b, ffn_w_down, final_norm, loss_target, m_attn_norm, m_gla_w_in, m_gla_w_a2, m_gla_b_a2, m_gla_head_norm, m_gla_w_out, m_kv_norm, m_w_kv, m_dsa_w_q, m_dsa_w_out, m_ffn_norm, m_ffn_w_up, m_ffn_conv_w, m_ffn_conv_b, m_ffn_w_down, m_final_norm, v_attn_norm, v_gla_w_in, v_gla_w_a2, v_gla_b_a2, v_gla_head_norm, v_gla_w_out, v_kv_norm, v_w_kv, v_dsa_w_q, v_dsa_w_out, v_ffn_norm, v_ffn_w_up, v_ffn_conv_w, v_ffn_conv_b, v_ffn_w_down, v_final_norm):
    me = 4 * lax.axis_index("x") + 2 * lax.axis_index("y") + lax.axis_index("c")
    bf = lambda a: a.astype(BF16)

    g_in, g_small = all_gather([bf(gla_w_in[0]), _pack_small_weights(gla_w_a2, gla_b_a2, gla_head_norm, ffn_conv_w)],
                               name="gather_first")
    w_a2_full, b_a2_full, hn_full, conv_w_full = _unpack_small_weights(g_small)
    w_in_full = jnp.pad(g_in.transpose(1, 0, 2).reshape(D_MODEL, GLA_IN_DIM), ((0, 0), (0, GLA_IN_PAD - GLA_IN_DIM)))
    wts = dict(
        attn_norm=attn_norm, ffn_norm=ffn_norm, kv_norm=kv_norm, final_norm=final_norm, ffn_conv_b=ffn_conv_b,
        gla_w_in=w_in_full[None],
        gla_w_a2=jnp.pad(bf(w_a2_full), ((0, LANE - GATE_RANK), (0, 0))),
        gla_b_a2=b_a2_full, gla_head_norm=hn_full, ffn_conv_w=conv_w_full,
    )
    plan = Plan({}, srcs=dict(gout=bf(gla_w_out[0]), kv=bf(w_kv), q=bf(dsa_w_q[0]), dout=bf(dsa_w_out[0]),
                              up0=bf(ffn_w_up[0]), up1=bf(ffn_w_up[1]), dn0=bf(ffn_w_down[0]), dn1=bf(ffn_w_down[1])))

    loss_tile, grad_x, small = local_step(x[0], loss_target[0], wts, plan)
    loss = lax.psum(loss_tile[0, 0], ("x", "y", "c"))

    plan.run("grads_tail", exchange_only, name="grads_tail")
    shard3 = lambda a: a.reshape((-1,) + a.shape[-2:])
    big_params = dict(gla_w_in=(("in",), gla_w_in, m_gla_w_in, v_gla_w_in),
                      gla_w_out=(("gout",), gla_w_out, m_gla_w_out, v_gla_w_out),
                      w_kv=(("kv",), w_kv, m_w_kv, v_w_kv),
                      dsa_w_q=(("q",), dsa_w_q, m_dsa_w_q, v_dsa_w_q),
                      dsa_w_out=(("dout",), dsa_w_out, m_dsa_w_out, v_dsa_w_out),
                      ffn_w_up=(("up0", "up1"), ffn_w_up, m_ffn_w_up, v_ffn_w_up),
                      ffn_w_down=(("dn0", "dn1"), ffn_w_down, m_ffn_w_down, v_ffn_w_down))
    res = {}
    for nm, (parts, w, m, v) in big_params.items():
        outs = adam_sharded([plan.recv[p] for p in parts], shard3(w), shard3(m), shard3(v), name=f"adam_{nm}")
        res[nm] = [o.reshape(w.shape) for o in outs]

    packed = _pack_rows([small[nm] for nm in SMALL_ORDER], SMALL_GRAD_ROWS)
    (parts,) = all_gather([packed], name="gather_small_grads")
    full = dict(zip(SMALL_ORDER, _unpack_rows(sum_partials(parts), [SMALL_FULL[nm] for nm in SMALL_ORDER])))
    local_w = dict(attn_norm=attn_norm, ffn_norm=ffn_norm, kv_norm=kv_norm, final_norm=final_norm,
                   ffn_conv_b=ffn_conv_b, gla_w_a2=gla_w_a2, gla_b_a2=gla_b_a2, gla_head_norm=gla_head_norm,
                   ffn_conv_w=ffn_conv_w)
    local_m = dict(attn_norm=m_attn_norm, ffn_norm=m_ffn_norm, kv_norm=m_kv_norm, final_norm=m_final_norm,
                   ffn_conv_b=m_ffn_conv_b, gla_w_a2=m_gla_w_a2, gla_b_a2=m_gla_b_a2, gla_head_norm=m_gla_head_norm,
                   ffn_conv_w=m_ffn_conv_w)
    local_v = dict(attn_norm=v_attn_norm, ffn_norm=v_ffn_norm, kv_norm=v_kv_norm, final_norm=v_final_norm,
                   ffn_conv_b=v_ffn_conv_b, gla_w_a2=v_gla_w_a2, gla_b_a2=v_gla_b_a2, gla_head_norm=v_gla_head_norm,
                   ffn_conv_w=v_ffn_conv_w)
    local_g = {}
    for nm in SMALL_ORDER:
        gfull = full[nm]
        if nm in SMALL_SHARDED:
            per = gfull.shape[-1] // N_DEV
            gfull = lax.dynamic_slice_in_dim(gfull, me * per, per, axis=gfull.ndim - 1)
        local_g[nm] = gfull.reshape(local_w[nm].shape)
    shapes = [local_w[nm].shape for nm in SMALL_ORDER]
    pk = lambda dd: _pack_rows([dd[nm] for nm in SMALL_ORDER], SMALL_ADAM_ROWS)
    d_p, m_p, v_p = adam_packed(pk(local_w), pk(local_g), pk(local_m), pk(local_v))
    for nm, dl, mn, vn in zip(SMALL_ORDER, _unpack_rows(d_p, shapes), _unpack_rows(m_p, shapes),
                              _unpack_rows(v_p, shapes)):
        res[nm] = [local_g[nm], dl, mn, vn]

    order = ("attn_norm", "gla_w_in", "gla_w_a2", "gla_b_a2", "gla_head_norm", "gla_w_out", "kv_norm", "w_kv",
             "dsa_w_q", "dsa_w_out", "ffn_norm", "ffn_w_up", "ffn_conv_w", "ffn_conv_b", "ffn_w_down", "final_norm")
    outs = [loss, grad_x[None]]
    for kind in range(4):
        outs.extend(res[nm][kind] for nm in order)
    return tuple(outs)
```

```python
import functools

import jax
import jax.numpy as jnp
from jax import lax
from jax.experimental import pallas as pl
from jax.experimental.pallas import tpu as pltpu

F32 = jnp.float32
BF16 = jnp.bfloat16
MESH = pl.DeviceIdType.MESH
ANY = pl.BlockSpec(memory_space=pl.ANY)

N_DEV = 8
D_MODEL = 2048
GLA_HEADS = 4
GLA_KEY_DIM = 1024
GLA_VAL_DIM = 2048
GLA_DK = 256
GLA_DV = 512
GATE_RANK = 16
GATE_NORMALIZER = 16.0
GLA_CHUNK = 64
GLA_STEP_CHUNKS = 2
GLA_IN_DIM = 2 * GLA_KEY_DIM + 2 * GLA_VAL_DIM + GATE_RANK
GLA_IN_PAD = 6272
ATT_HEADS = 16
HEAD_DIM = 128
DILATIONS = (1, 4, 16)
ATT_BLOCK = 128
D_FF = 5632
EPS = 1e-6
ADAM_LR = 0.001
ADAM_B1 = 0.9
ADAM_B2 = 0.999
ADAM_EPS = 1e-08
ADAM_WD = 0.01
ADAM_STEP = 10
NEG = -1e30
LANE = 128
NORM_ROWS = 64
VMEM_LIMIT = 52 * 1024 * 1024
ALIBI_SLOPES = tuple(2.0 ** (-0.5 * (i + 1)) for i in range(ATT_HEADS))


def _params(*sem):
    return pltpu.CompilerParams(dimension_semantics=sem, vmem_limit_bytes=VMEM_LIMIT)


def _tile(n, cap):
    best = None
    for t in range(LANE, min(n, cap) + 1, LANE):
        if n % t == 0:
            best = t
    return best if best is not None else n


def _shard_group(j, ns, cap):
    best = 1
    for g in range(1, j + 1):
        if j % g == 0 and g * ns <= cap:
            best = g
    return best


def _rows(r, c, budget=256 * 1024):
    best = None
    for t in range(16, r + 1, 16):
        if r % t == 0 and t * c <= budget:
            best = t
    return best if best is not None else r


def _flip(coord, bit):
    return 1 - coord if bit else coord


def _place():
    x, y, c = lax.axis_index("x"), lax.axis_index("y"), lax.axis_index("c")
    return x, y, c, 4 * x + 2 * y + c


def _rows_of(ref, rows):
    return ref if rows is None else ref.at[pl.ds(rows[0], rows[1] - rows[0])]


class Jobs:
    def __init__(self):
        self.srcs = []
        self.bufs = []
        self.sems = []
        self.steps = []

    def _src(self, a):
        for i, b in enumerate(self.srcs):
            if b is a:
                return i
        self.srcs.append(a)
        return len(self.srcs) - 1

    def new(self, shape, dtype):
        self.bufs.append((None, jax.ShapeDtypeStruct(shape, dtype)))
        return len(self.bufs) - 1

    def thru(self, a):
        self.bufs.append((a, jax.ShapeDtypeStruct(a.shape, a.dtype)))
        return len(self.bufs) - 1

    def _sem(self, n):
        self.sems.append(pltpu.SemaphoreType.DMA((n,)))
        return len(self.sems) - 1

    def gather_ici(self, src, buf, rows=None):
        si, send, recv, loc = self._src(src), self._sem(4), self._sem(4), self._sem(1)

        def remote(srcs, bufs, sems, slot_of):
            x, y, c, me = _place()
            peers = [(x, y, 1 - c), (1 - x, y, c), (x, 1 - y, c), (1 - x, 1 - y, c)]
            return [pltpu.make_async_remote_copy(
                src_ref=_rows_of(srcs[si], rows),
                dst_ref=_rows_of(bufs[buf].at[me if slot_of == "mine" else 4 * p[0] + 2 * p[1] + p[2]], rows),
                send_sem=sems[send].at[k], recv_sem=sems[recv].at[k], device_id=p, device_id_type=MESH)
                for k, p in enumerate(peers)]

        def local(srcs, bufs, sems):
            return pltpu.make_async_copy(_rows_of(srcs[si], rows), _rows_of(bufs[buf].at[_place()[3]], rows),
                                         sems[loc].at[0])

        def start(srcs, bufs, sems):
            local(srcs, bufs, sems).start()
            for cp in remote(srcs, bufs, sems, "mine"):
                cp.start()

        def finish(srcs, bufs, sems):
            for cp in remote(srcs, bufs, sems, "peer"):
                cp.wait_recv()
            for cp in remote(srcs, bufs, sems, "mine"):
                cp.wait_send()
            local(srcs, bufs, sems).wait()

        self.steps.append((start, finish))

    def gather_d2d(self, buf, rows=None):
        send, recv = self._sem(3), self._sem(3)

        def copies(bufs, sems, core):
            x, y, c, _ = _place()
            cc = c if core == "mine" else 1 - c
            chips = [(1 - x, y), (x, 1 - y), (1 - x, 1 - y)]
            return [pltpu.make_async_remote_copy(
                src_ref=_rows_of(bufs[buf].at[4 * px + 2 * py + cc], rows),
                dst_ref=_rows_of(bufs[buf].at[4 * px + 2 * py + cc], rows),
                send_sem=sems[send].at[k], recv_sem=sems[recv].at[k],
                device_id=(x, y, 1 - c), device_id_type=MESH) for k, (px, py) in enumerate(chips)]

        def start(srcs, bufs, sems):
            for cp in copies(bufs, sems, "mine"):
                cp.start()

        def finish(srcs, bufs, sems):
            for cp in copies(bufs, sems, "sibling"):
                cp.wait_recv()
            for cp in copies(bufs, sems, "mine"):
                cp.wait_send()

        self.steps.append((start, finish))

    def scatter(self, src, buf, rows=None):
        si, send, recv, loc = self._src(src), self._sem(N_DEV - 1), self._sem(N_DEV - 1), self._sem(1)

        def remote(srcs, bufs, sems, slot_of):
            x, y, c, me = _place()
            out = []
            for k in range(1, N_DEV):
                px, py, pc = _flip(x, k >> 2), _flip(y, (k >> 1) & 1), _flip(c, k & 1)
                peer = 4 * px + 2 * py + pc
                out.append(pltpu.make_async_remote_copy(
                    src_ref=_rows_of(srcs[si].at[peer], rows),
                    dst_ref=_rows_of(bufs[buf].at[me if slot_of == "mine" else peer], rows),
                    send_sem=sems[send].at[k - 1], recv_sem=sems[recv].at[k - 1],
                    device_id=(px, py, pc), device_id_type=MESH))
            return out

        def local(srcs, bufs, sems):
            me = _place()[3]
            return pltpu.make_async_copy(_rows_of(srcs[si].at[me], rows), _rows_of(bufs[buf].at[me], rows),
                                         sems[loc].at[0])

        def start(srcs, bufs, sems):
            local(srcs, bufs, sems).start()
            for cp in remote(srcs, bufs, sems, "mine"):
                cp.start()

        def finish(srcs, bufs, sems):
            for cp in remote(srcs, bufs, sems, "peer"):
                cp.wait_recv()
            for cp in remote(srcs, bufs, sems, "mine"):
                cp.wait_send()
            local(srcs, bufs, sems).wait()

        self.steps.append((start, finish))


def _call(body, *, name, grid, in_specs, out_specs, out_shape, args, sem, scratch_shapes=(), jobs=None):
    in_specs, out_specs, out_shape = list(in_specs), list(out_specs), list(out_shape)
    scratch_shapes = list(scratch_shapes)
    if jobs is None:
        res = pl.pallas_call(body, name=name, out_shape=out_shape, grid=grid, in_specs=in_specs,
                             out_specs=out_specs, scratch_shapes=scratch_shapes,
                             compiler_params=_params(*sem))(*args)
        return list(res), []
    thru = [a for a, _ in jobs.bufs if a is not None]
    n_in, n_src, n_thru = len(args), len(jobs.srcs), len(thru)
    n_out, n_buf, n_scr = len(out_shape), len(jobs.bufs), len(scratch_shapes)
    aliases, t = {}, 0
    for b, (a, _) in enumerate(jobs.bufs):
        if a is not None:
            aliases[n_in + n_src + t] = n_out + b
            t += 1

    def wrapped(*refs):
        at = 0
        ins = refs[at:at + n_in]; at += n_in
        srcs = refs[at:at + n_src]; at += n_src + n_thru
        outs = refs[at:at + n_out]; at += n_out
        bufs = refs[at:at + n_buf]; at += n_buf
        scr = refs[at:at + n_scr]; at += n_scr
        sems = refs[at:]
        first, last = None, None
        for axis, size in enumerate(grid):
            pid = pl.program_id(axis)
            f, l = pid == 0, pid == size - 1
            first = f if first is None else first & f
            last = l if last is None else last & l

        @pl.when(first)
        def _():
            for start, _ in jobs.steps:
                start(srcs, bufs, sems)

        body(*ins, *outs, *scr)

        @pl.when(last)
        def _():
            for _, finish in jobs.steps:
                finish(srcs, bufs, sems)

    res = pl.pallas_call(
        wrapped, name=name,
        out_shape=out_shape + [s for _, s in jobs.bufs],
        grid=grid,
        in_specs=in_specs + [ANY] * (n_src + n_thru),
        out_specs=out_specs + [ANY] * n_buf,
        scratch_shapes=scratch_shapes + jobs.sems,
        input_output_aliases=aliases,
        compiler_params=_params(*(["arbitrary"] * len(grid))),
    )(*args, *jobs.srcs, *thru)
    return res[:n_out], res[n_out:]


def mm_nn(a, w, *, out_dtype, name, res=None, tm=None, jobs=None):
    m, k = a.shape
    j, k2, ns = w.shape
    whole = j == 1 and ns <= 2048 and k <= 2048
    tm = tm or (1024 if a.dtype == BF16 and not whole else 512)
    assert k == k2 and m % tm == 0
    tn = ns if whole else _tile(ns, 1408)
    nsub = ns // tn
    tk = k if k <= 2048 else _tile(k, 1408)
    nk = k // tk
    has_res = res is not None

    def body(*refs):
        if has_res:
            a_ref, w_ref, r_ref, o_ref, acc = refs
        else:
            a_ref, w_ref, o_ref, acc = refs
        kk = pl.program_id(2)

        @pl.when(kk == 0)
        def _():
            acc[...] = jnp.zeros_like(acc)

        acc[...] += jnp.dot(a_ref[...].astype(BF16), w_ref[...], preferred_element_type=F32)

        @pl.when(kk == nk - 1)
        def _():
            r = acc[...]
            if has_res:
                r = r + r_ref[...]
            o_ref[...] = r.astype(out_dtype)

    in_specs = [
        pl.BlockSpec((tm, tk), lambda i, n, kk: (i, kk)),
        pl.BlockSpec((None, tk, tn), lambda i, n, kk: (n // nsub, kk, n % nsub)),
    ]
    args = [a, w]
    if has_res:
        in_specs.append(pl.BlockSpec((tm, tn), lambda i, n, kk: (i, n)))
        args.append(res)
    (out,), bufs = _call(
        body, name=name, jobs=jobs,
        out_shape=[jax.ShapeDtypeStruct((m, j * ns), out_dtype)],
        grid=(m // tm, j * nsub, nk),
        in_specs=in_specs,
        out_specs=[pl.BlockSpec((tm, tn), lambda i, n, kk: (i, n))],
        scratch_shapes=[pltpu.VMEM((tm, tn), F32)],
        args=args, sem=("parallel", "parallel", "arbitrary"))
    return out if jobs is None else (out, bufs)


def mm_nt(dy, w, *, out_dtype, name, tm=None, jobs=None, norm=None):
    parts, m, n = (1,) + dy.shape if dy.ndim == 2 else dy.shape
    n *= parts
    j, k, ns = w.shape
    if norm is not None:
        x, dres, gains, more = norm
        tm = tm or 512
    tm = tm or 1024
    assert n == j * ns and m % tm == 0
    tn = _tile(ns, 2048)
    nsub = ns // tn
    jb = _shard_group(j // parts, ns, 2048 if norm is None else 1024) if nsub == 1 else 1
    tko = _tile(k, 1408) if norm is None else k
    nn = j * nsub // jb
    per_part = nn // parts
    if dy.ndim == 2:
        dy_spec = pl.BlockSpec((tm, jb * tn), lambda i, ko, nq: (i, nq))
    else:
        dy_spec = pl.BlockSpec((None, tm, jb * tn), lambda i, ko, nq: (nq // per_part, i, nq % per_part))
    if jb == 1:
        w_spec = pl.BlockSpec((None, tko, tn), lambda i, ko, nq: (nq // nsub, ko, nq % nsub))
    else:
        w_spec = pl.BlockSpec((jb, tko, ns), lambda i, ko, nq: (nq, ko, 0))

    n_gain = 0 if norm is None else len(gains)
    n_more = 0 if norm is None else len(more)

    def body(*refs):
        a_ref, w_ref = refs[:2]
        acc = refs[-1]
        nq = pl.program_id(2)
        first = pl.program_id(0) == 0

        @pl.when(nq == 0)
        def _():
            acc[...] = jnp.zeros_like(acc)

        if jb == 1:
            acc[...] += lax.dot_general(a_ref[...].astype(BF16), w_ref[...], (((1,), (1,)), ((), ())),
                                        preferred_element_type=F32)
        else:
            part = acc[...]
            for jj in range(jb):
                part = part + lax.dot_general(a_ref[:, jj * ns:(jj + 1) * ns].astype(BF16), w_ref[jj],
                                              (((1,), (1,)), ((), ())), preferred_element_type=F32)
            acc[...] = part

        @pl.when(nq == nn - 1)
        def _():
            if norm is None:
                refs[2][...] = acc[...].astype(out_dtype)
                return
            x_ref, r_ref = refs[2:4]
            g_refs = refs[4:4 + n_gain]
            e_refs = refs[4 + n_gain:4 + n_gain + n_more]
            dx_ref = refs[4 + n_gain + n_more]
            dg_refs = refs[5 + n_gain + n_more:-1]

            @pl.when(first)
            def _():
                for dg_ref in dg_refs:
                    dg_ref[...] = jnp.zeros_like(dg_ref)

            def rows(c, carry):
                sl = pl.ds(pl.multiple_of(c * NORM_ROWS, NORM_ROWS), NORM_ROWS)
                xv = x_ref[sl, :]
                r = lax.rsqrt(jnp.mean(xv * xv, axis=-1, keepdims=True) + EPS)
                xh = xv * r
                out = r_ref[sl, :]
                for idx, (g_ref, dg_ref) in enumerate(zip(g_refs, dg_refs)):
                    dyv = acc[sl, :] if idx == 0 else e_refs[idx - 1][sl, :].astype(F32)
                    dg_ref[...] += jnp.sum(dyv * xh, axis=0, keepdims=True)
                    dxh = dyv * g_ref[...]
                    out = out + r * (dxh - xh * jnp.mean(dxh * xh, axis=-1, keepdims=True))
                dx_ref[sl, :] = out
                return carry

            lax.fori_loop(0, tm // NORM_ROWS, rows, 0)

    out_tile = pl.BlockSpec((tm, tko), lambda i, ko, nq: (i, ko))
    in_specs, args = [dy_spec, w_spec], [dy, w]
    out_shape, out_specs = [jax.ShapeDtypeStruct((m, k), out_dtype)], [out_tile]
    sem = ("parallel", "parallel", "arbitrary")
    if norm is not None:
        vec = pl.BlockSpec((1, k), lambda i, ko, nq: (0, 0))
        late = pl.BlockSpec((tm, tko), lambda i, ko, nq: (i, ko), pipeline_mode=pl.Buffered(1))
        in_specs += [late, late] + [vec] * n_gain + [late] * n_more
        args += [x, dres] + list(gains) + list(more)
        out_shape = [jax.ShapeDtypeStruct((m, k), F32)] + [jax.ShapeDtypeStruct((1, k), F32)] * n_gain
        out_specs = [out_tile] + [vec] * n_gain
        sem = ("arbitrary", "arbitrary", "arbitrary")
    outs, bufs = _call(
        body, name=name, jobs=jobs, out_shape=out_shape, grid=(m // tm, k // tko, nn),
        in_specs=in_specs, out_specs=out_specs, scratch_shapes=[pltpu.VMEM((tm, tko), F32)], args=args, sem=sem)
    out = outs[0] if norm is None else (outs[0], outs[1:])
    return out if jobs is None else (out, bufs)


def mm_tn(x, dy, j, *, name, tm=1024, jobs=None):
    m, k = x.shape
    parts, m2, n = (1,) + dy.shape if dy.ndim == 2 else dy.shape
    n *= parts
    assert m == m2 and n % j == 0 and m % tm == 0
    ns = n // j
    tn = _tile(ns, 1408)
    nsub = ns // tn
    jb = _shard_group(j // parts, ns, 1536) if nsub == 1 else 1
    tk = _tile(k, 1408)
    nm = m // tm
    n_steps = j * nsub // jb
    per_part = n_steps // parts
    if dy.ndim == 2:
        dy_spec = pl.BlockSpec((tm, jb * tn), lambda kq, nq, mi: (mi, nq))
    else:
        dy_spec = pl.BlockSpec((None, tm, jb * tn), lambda kq, nq, mi: (nq // per_part, mi, nq % per_part))
    if jb == 1:
        out_spec = pl.BlockSpec((None, tk, tn), lambda kq, nq, mi: (nq // nsub, kq, nq % nsub))
        acc_shape = (tk, tn)
    else:
        out_spec = pl.BlockSpec((jb, tk, ns), lambda kq, nq, mi: (nq, kq, 0))
        acc_shape = (jb, tk, ns)

    def body(x_ref, dy_ref, o_ref, acc):
        mi = pl.program_id(2)

        @pl.when(mi == 0)
        def _():
            acc[...] = jnp.zeros_like(acc)

        xb = x_ref[...].astype(BF16)
        if jb == 1:
            acc[...] += lax.dot_general(xb, dy_ref[...].astype(BF16), (((0,), (0,)), ((), ())),
                                        preferred_element_type=F32)
        else:
            for jj in range(jb):
                acc[jj] += lax.dot_general(xb, dy_ref[:, jj * ns:(jj + 1) * ns].astype(BF16),
                                           (((0,), (0,)), ((), ())), preferred_element_type=F32)

        @pl.when(mi == nm - 1)
        def _():
            o_ref[...] = acc[...].astype(BF16)

    (out,), bufs = _call(
        body, name=name, jobs=jobs,
        out_shape=[jax.ShapeDtypeStruct((j, k, ns), BF16)],
        grid=(k // tk, n_steps, nm),
        in_specs=[
            pl.BlockSpec((tm, tk), lambda kq, nq, mi: (mi, kq)),
            dy_spec,
        ],
        out_specs=[out_spec],
        scratch_shapes=[pltpu.VMEM(acc_shape, F32)],
        args=[x, dy], sem=("parallel", "parallel", "arbitrary"))
    return out if jobs is None else (out, bufs)


def rms_fwd(x, gains, *, name, ts=512):
    s, d = x.shape
    n = len(gains)

    def body(x_ref, *refs):
        xv = x_ref[...]
        xh = xv * lax.rsqrt(jnp.mean(xv * xv, axis=-1, keepdims=True) + EPS)
        for g_ref, o_ref in zip(refs[:n], refs[n:]):
            o_ref[...] = (xh * g_ref[...]).astype(BF16)

    row = pl.BlockSpec((ts, d), lambda i: (i, 0))
    vec = pl.BlockSpec((1, d), lambda i: (0, 0))
    return pl.pallas_call(
        body,
        name=name,
        out_shape=[jax.ShapeDtypeStruct((s, d), BF16)] * n,
        grid=(s // ts,),
        in_specs=[row] + [vec] * n,
        out_specs=[row] * n,
        compiler_params=_params("parallel"),
    )(x, *gains)


def loss_head(h, gain, target, *, ts=256):
    s, d = h.shape

    def body(h_ref, g_ref, t_ref, l_ref, dh_ref, dg_ref):
        i = pl.program_id(0)

        @pl.when(i == 0)
        def _():
            l_ref[...] = jnp.zeros_like(l_ref)
            dg_ref[...] = jnp.zeros_like(dg_ref)

        xv = h_ref[...]
        r = lax.rsqrt(jnp.mean(xv * xv, axis=-1, keepdims=True) + EPS)
        xh = xv * r
        g = g_ref[...]
        err = xh * g - t_ref[...]
        l_ref[...] += 0.5 * jnp.sum(jnp.mean(err * err, axis=-1, keepdims=True))
        dy = err * (1.0 / d)
        dg_ref[...] += jnp.sum(dy * xh, axis=0, keepdims=True)
        dxh = dy * g
        dh_ref[...] = r * (dxh - xh * jnp.mean(dxh * xh, axis=-1, keepdims=True))

    row = pl.BlockSpec((ts, d), lambda i: (i, 0))
    vec = pl.BlockSpec((1, d), lambda i: (0, 0))
    return pl.pallas_call(
        body,
        name="loss_head",
        out_shape=[jax.ShapeDtypeStruct((8, LANE), F32), jax.ShapeDtypeStruct((s, d), F32),
                   jax.ShapeDtypeStruct((1, d), F32)],
        grid=(s // ts,),
        in_specs=[row, vec, row],
        out_specs=[pl.BlockSpec((8, LANE), lambda i: (0, 0)), row, vec],
        compiler_params=_params("arbitrary"),
    )(h, gain, target)


A_BLOCK = (2 * GLA_KEY_DIM + 2 * GLA_VAL_DIM) // LANE


def gate_fwd(proj, w_a2p, b_a2, *, ts=512):
    s = proj.shape[0]

    def body(a_ref, w_ref, b_ref, o_ref):
        z = jnp.dot(a_ref[...].astype(BF16), w_ref[...], preferred_element_type=F32) + b_ref[...]
        o_ref[...] = (jnp.minimum(z, 0.0) - jnp.log(1.0 + jnp.exp(-jnp.abs(z)))) * (1.0 / GATE_NORMALIZER)

    return pl.pallas_call(
        body,
        name="gate_fwd",
        out_shape=jax.ShapeDtypeStruct((s, GLA_KEY_DIM), F32),
        grid=(s // ts,),
        in_specs=[pl.BlockSpec((ts, LANE), lambda i: (i, A_BLOCK)),
                  pl.BlockSpec((LANE, GLA_KEY_DIM), lambda i: (0, 0)),
                  pl.BlockSpec((1, GLA_KEY_DIM), lambda i: (0, 0))],
        out_specs=pl.BlockSpec((ts, GLA_KEY_DIM), lambda i: (i, 0)),
        compiler_params=_params("parallel"),
    )(proj, w_a2p, b_a2)


def gate_bwd(proj, w_a2p, b_a2, dla, *, ts=512):
    s = proj.shape[0]

    def body(a_ref, w_ref, b_ref, dla_ref, da_ref, dw_ref, db_ref):
        i = pl.program_id(0)

        @pl.when(i == 0)
        def _():
            dw_ref[...] = jnp.zeros_like(dw_ref)
            db_ref[...] = jnp.zeros_like(db_ref)

        a = a_ref[...].astype(BF16)
        w = w_ref[...]
        z = jnp.dot(a, w, preferred_element_type=F32) + b_ref[...]
        dz = dla_ref[...] * (1.0 / GATE_NORMALIZER) / (1.0 + jnp.exp(z))
        dzb = dz.astype(BF16)
        da_ref[...] = lax.dot_general(dzb, w, (((1,), (1,)), ((), ())), preferred_element_type=F32).astype(BF16)
        dw_ref[...] += lax.dot_general(a, dzb, (((0,), (0,)), ((), ())), preferred_element_type=F32)
        db_ref[...] += jnp.sum(dz, axis=0, keepdims=True)

    return pl.pallas_call(
        body,
        name="gate_bwd",
        out_shape=[jax.ShapeDtypeStruct((s, LANE), BF16), jax.ShapeDtypeStruct((LANE, GLA_KEY_DIM), F32),
                   jax.ShapeDtypeStruct((1, GLA_KEY_DIM), F32)],
        grid=(s // ts,),
        in_specs=[pl.BlockSpec((ts, LANE), lambda i: (i, A_BLOCK)),
                  pl.BlockSpec((LANE, GLA_KEY_DIM), lambda i: (0, 0)),
                  pl.BlockSpec((1, GLA_KEY_DIM), lambda i: (0, 0)),
                  pl.BlockSpec((ts, GLA_KEY_DIM), lambda i: (i, 0))],
        out_specs=[pl.BlockSpec((ts, LANE), lambda i: (i, 0)),
                   pl.BlockSpec((LANE, GLA_KEY_DIM), lambda i: (0, 0)),
                   pl.BlockSpec((1, GLA_KEY_DIM), lambda i: (0, 0))],
        compiler_params=_params("arbitrary"),
    )(proj, w_a2p, b_a2, dla)


def _chunk_terms(q, k, la):
    c_len = GLA_CHUNK
    row = lax.broadcasted_iota(jnp.int32, (c_len, c_len), 0)
    col = lax.broadcasted_iota(jnp.int32, (c_len, c_len), 1)
    tri = row >= col
    c = jnp.dot(tri.astype(F32), la, preferred_element_type=F32, precision=lax.Precision.HIGHEST)
    last = jnp.sum(la, axis=0, keepdims=True)
    q_dec = q * (GLA_DK ** -0.5) * jnp.exp(c)
    k_inv = k * jnp.exp(-c)
    k_end = k * jnp.exp(last - c)
    return c, last, q_dec, k_inv, k_end, tri


def _dot(a, b, ca, cb):
    return lax.dot_general(a.astype(BF16), b.astype(BF16), (((ca,), (cb,)), ((), ())), preferred_element_type=F32)


def gla_fwd(proj, la, jobs=None):
    s = proj.shape[0]
    n_chunks = s // GLA_CHUNK
    rows = GLA_CHUNK * GLA_STEP_CHUNKS

    def body(q_ref, k_ref, v_ref, la_ref, o_ref, st_out, st):
        @pl.when(pl.program_id(0) == 0)
        def _():
            st[...] = jnp.zeros_like(st)

        for h in range(GLA_HEADS):
            hk = slice(h * GLA_DK, (h + 1) * GLA_DK)
            hv = slice(h * GLA_DV, (h + 1) * GLA_DV)
            for cc in range(GLA_STEP_CHUNKS):
                rs = slice(cc * GLA_CHUNK, (cc + 1) * GLA_CHUNK)
                _, last, q_dec, k_inv, k_end, tri = _chunk_terms(q_ref[rs, hk], k_ref[rs, hk], la_ref[rs, hk])
                v = v_ref[rs, hv]
                a = jnp.where(tri, _dot(q_dec, k_inv, 1, 1), 0.0)
                state = st[h]
                st_out[h, cc] = state
                o_ref[rs, hv] = _dot(a, v, 1, 0) + _dot(q_dec, state, 1, 1)
                st[h] = state * jnp.exp(last) + _dot(v, k_end, 0, 0)

    key = lambda col: pl.BlockSpec((rows, GLA_KEY_DIM), lambda n: (n, col))
    outs, bufs = _call(
        body, name="gla_fwd", jobs=jobs,
        out_shape=[jax.ShapeDtypeStruct((s, GLA_VAL_DIM), F32),
                   jax.ShapeDtypeStruct((GLA_HEADS, n_chunks, GLA_DV, GLA_DK), F32)],
        grid=(n_chunks // GLA_STEP_CHUNKS,),
        in_specs=[key(0), key(1), pl.BlockSpec((rows, GLA_VAL_DIM), lambda n: (n, 1)), key(0)],
        out_specs=[pl.BlockSpec((rows, GLA_VAL_DIM), lambda n: (n, 0)),
                   pl.BlockSpec((GLA_HEADS, GLA_STEP_CHUNKS, GLA_DV, GLA_DK), lambda n: (0, n, 0, 0))],
        scratch_shapes=[pltpu.VMEM((GLA_HEADS, GLA_DV, GLA_DK), F32)],
        args=[proj, proj, proj, la], sem=("arbitrary",))
    return outs if jobs is None else (outs, bufs)


def gla_bwd(proj, la, states, do, jobs=None):
    s = proj.shape[0]
    n_steps = s // GLA_CHUNK // GLA_STEP_CHUNKS
    lastc = n_steps - 1
    rows = GLA_CHUNK * GLA_STEP_CHUNKS

    def body(q_ref, k_ref, v_ref, la_ref, do_ref, st_ref, dq_ref, dk_ref, dv_ref, dla_ref, dst):
        @pl.when(pl.program_id(0) == 0)
        def _():
            dst[...] = jnp.zeros_like(dst)

        upper = (lax.broadcasted_iota(jnp.int32, (GLA_CHUNK, GLA_CHUNK), 0)
                 <= lax.broadcasted_iota(jnp.int32, (GLA_CHUNK, GLA_CHUNK), 1)).astype(F32)
        for h in range(GLA_HEADS):
            hk = slice(h * GLA_DK, (h + 1) * GLA_DK)
            hv = slice(h * GLA_DV, (h + 1) * GLA_DV)
            for cc in reversed(range(GLA_STEP_CHUNKS)):
                rs = slice(cc * GLA_CHUNK, (cc + 1) * GLA_CHUNK)
                c, last, q_dec, k_inv, k_end, tri = _chunk_terms(q_ref[rs, hk], k_ref[rs, hk], la_ref[rs, hk])
                v = v_ref[rs, hv]
                dout = do_ref[rs, hv]
                state = st_ref[h, cc]
                dstate = dst[h]
                e_last = jnp.exp(last)
                a = jnp.where(tri, _dot(q_dec, k_inv, 1, 1), 0.0)
                da = jnp.where(tri, _dot(dout, v, 1, 1), 0.0)
                dv_ref[rs, hv] = (_dot(a, dout, 0, 0) + _dot(k_end, dstate, 1, 1)).astype(BF16)
                dq_dec = _dot(da, k_inv, 1, 0) + _dot(dout, state, 1, 0)
                dk_inv = _dot(da, q_dec, 0, 0)
                dk_end = _dot(v, dstate, 1, 0)
                dst[h] = dstate * e_last + _dot(dout, q_dec, 0, 0)
                dq_ref[rs, hk] = (dq_dec * (GLA_DK ** -0.5) * jnp.exp(c)).astype(BF16)
                dk_ref[rs, hk] = (dk_inv * jnp.exp(-c) + dk_end * jnp.exp(last - c)).astype(BF16)
                ke_term = dk_end * k_end
                dc = dq_dec * q_dec - dk_inv * k_inv - ke_term
                dlast = (jnp.sum(ke_term, axis=0, keepdims=True)
                         + e_last * jnp.sum(dstate * state, axis=0, keepdims=True))
                dla_ref[rs, hk] = jnp.dot(upper, dc, preferred_element_type=F32,
                                          precision=lax.Precision.HIGHEST) + dlast

    key = lambda col: pl.BlockSpec((rows, GLA_KEY_DIM), lambda n: (lastc - n, col))
    val = lambda col: pl.BlockSpec((rows, GLA_VAL_DIM), lambda n: (lastc - n, col))
    outs, bufs = _call(
        body, name="gla_bwd", jobs=jobs,
        out_shape=[jax.ShapeDtypeStruct((s, GLA_KEY_DIM), BF16), jax.ShapeDtypeStruct((s, GLA_KEY_DIM), BF16),
                   jax.ShapeDtypeStruct((s, GLA_VAL_DIM), BF16), jax.ShapeDtypeStruct((s, GLA_KEY_DIM), F32)],
        grid=(n_steps,),
        in_specs=[key(0), key(1), val(1), key(0), val(0),
                  pl.BlockSpec((GLA_HEADS, GLA_STEP_CHUNKS, GLA_DV, GLA_DK), lambda n: (0, lastc - n, 0, 0))],
        out_specs=[key(0), key(0), val(0), key(0)],
        scratch_shapes=[pltpu.VMEM((GLA_HEADS, GLA_DV, GLA_DK), F32)],
        args=[proj, proj, proj, la, do, states], sem=("arbitrary",))
    return outs if jobs is None else (outs, bufs)


R_BLOCK = (2 * GLA_KEY_DIM + GLA_VAL_DIM) // GLA_DV


def headnorm_fwd(o, proj, hn, *, ts=512):
    s = o.shape[0]

    def body(o_ref, r_ref, g_ref, out_ref):
        ov = o_ref[...]
        oh = ov * lax.rsqrt(jnp.mean(ov * ov, axis=-1, keepdims=True) + EPS)
        r = r_ref[...]
        out_ref[...] = (oh * g_ref[...] * (r * jax.nn.sigmoid(r))).astype(BF16)

    return pl.pallas_call(
        body,
        name="headnorm_fwd",
        out_shape=jax.ShapeDtypeStruct((s, GLA_VAL_DIM), BF16),
        grid=(s // ts, GLA_HEADS),
        in_specs=[pl.BlockSpec((ts, GLA_DV), lambda i, h: (i, h)),
                  pl.BlockSpec((ts, GLA_DV), lambda i, h: (i, R_BLOCK + h)),
                  pl.BlockSpec((1, GLA_DV), lambda i, h: (0, 0))],
        out_specs=pl.BlockSpec((ts, GLA_DV), lambda i, h: (i, h)),
        compiler_params=_params("parallel", "parallel"),
    )(o, proj, hn)


def headnorm_bwd(o, proj, hn, dog, *, ts=512):
    s = o.shape[0]

    def body(o_ref, r_ref, g_ref, dog_ref, do_ref, dr_ref, dg_ref):
        @pl.when((pl.program_id(0) == 0) & (pl.program_id(1) == 0))
        def _():
            dg_ref[...] = jnp.zeros_like(dg_ref)

        ov = o_ref[...]
        rr = lax.rsqrt(jnp.mean(ov * ov, axis=-1, keepdims=True) + EPS)
        oh = ov * rr
        g = g_ref[...]
        r = r_ref[...]
        sig = jax.nn.sigmoid(r)
        gate = r * sig
        dog_v = dog_ref[...]
        d_on = dog_v * gate
        dr_ref[...] = (dog_v * (oh * g) * (sig * (1.0 + r * (1.0 - sig)))).astype(BF16)
        dg_ref[...] += jnp.sum(d_on * oh, axis=0, keepdims=True)
        doh = d_on * g
        do_ref[...] = rr * (doh - oh * jnp.mean(doh * oh, axis=-1, keepdims=True))

    return pl.pallas_call(
        body,
        name="headnorm_bwd",
        out_shape=[jax.ShapeDtypeStruct((s, GLA_VAL_DIM), F32), jax.ShapeDtypeStruct((s, GLA_VAL_DIM), BF16),
                   jax.ShapeDtypeStruct((1, GLA_DV), F32)],
        grid=(s // ts, GLA_HEADS),
        in_specs=[pl.BlockSpec((ts, GLA_DV), lambda i, h: (i, h)),
                  pl.BlockSpec((ts, GLA_DV), lambda i, h: (i, R_BLOCK + h)),
                  pl.BlockSpec((1, GLA_DV), lambda i, h: (0, 0)),
                  pl.BlockSpec((ts, GLA_DV), lambda i, h: (i, h))],
        out_specs=[pl.BlockSpec((ts, GLA_DV), lambda i, h: (i, h)),
                   pl.BlockSpec((ts, GLA_DV), lambda i, h: (i, h)),
                   pl.BlockSpec((1, GLA_DV), lambda i, h: (0, 0))],
        compiler_params=_params("arbitrary", "arbitrary"),
    )(o, proj, hn, dog)


CONV_TC = 128
SQRT_HALF = 0.7071067811865476
INV_SQRT_2PI = 0.3989422804014327


def _conv_gate(g_ref, cw_ref, cb_ref):
    g0 = g_ref[...].astype(F32)
    t = lax.broadcasted_iota(jnp.int32, g0.shape, 0)
    g1 = jnp.where(t >= 1, pltpu.roll(g0, 1, 0), 0.0)
    g2 = jnp.where(t >= 2, pltpu.roll(g0, 2, 0), 0.0)
    gc = cw_ref[0:1, :] * g2 + cw_ref[1:2, :] * g1 + cw_ref[2:3, :] * g0 + cb_ref[...]
    return g0, g1, g2, gc, t


def convglu_fwd(up, conv_w, conv_b, *, name, jobs=None):
    s = up.shape[0]
    nc = D_FF // CONV_TC

    def body(u_ref, g_ref, cw_ref, cb_ref, o_ref):
        _, _, _, gc, _ = _conv_gate(g_ref, cw_ref, cb_ref)
        gelu = 0.5 * gc * (1.0 + lax.erf(gc * SQRT_HALF))
        o_ref[...] = (gelu * u_ref[...].astype(F32)).astype(BF16)

    (out,), bufs = _call(
        body, name=name, jobs=jobs,
        out_shape=[jax.ShapeDtypeStruct((s, D_FF), BF16)],
        grid=(nc,),
        in_specs=[pl.BlockSpec((s, CONV_TC), lambda c: (0, c)),
                  pl.BlockSpec((s, CONV_TC), lambda c: (0, nc + c)),
                  pl.BlockSpec((3, CONV_TC), lambda c: (0, c)),
                  pl.BlockSpec((1, CONV_TC), lambda c: (0, c))],
        out_specs=[pl.BlockSpec((s, CONV_TC), lambda c: (0, c))],
        args=[up, up, conv_w, conv_b], sem=("parallel",))
    return out if jobs is None else (out, bufs)


def convglu_bwd(up, conv_w, conv_b, dact, *, name, jobs=None):
    s = up.shape[0]
    nc = D_FF // CONV_TC

    def body(u_ref, g_ref, cw_ref, cb_ref, da_ref, dup_ref, dcw_ref, dcb_ref):
        du_ref, dg_ref = dup_ref.at[0], dup_ref.at[1]
        g0, g1, g2, gc, t = _conv_gate(g_ref, cw_ref, cb_ref)
        cdf = 0.5 * (1.0 + lax.erf(gc * SQRT_HALF))
        da = da_ref[...].astype(F32)
        du_ref[...] = (da * gc * cdf).astype(BF16)
        dgc = da * u_ref[...].astype(F32) * (cdf + gc * jnp.exp(-0.5 * gc * gc) * INV_SQRT_2PI)
        dcb_ref[...] = jnp.sum(dgc, axis=0, keepdims=True)
        dcw_ref[0:1, :] = jnp.sum(dgc * g2, axis=0, keepdims=True)
        dcw_ref[1:2, :] = jnp.sum(dgc * g1, axis=0, keepdims=True)
        dcw_ref[2:3, :] = jnp.sum(dgc * g0, axis=0, keepdims=True)
        n1 = jnp.where(t < s - 1, pltpu.roll(dgc, s - 1, 0), 0.0)
        n2 = jnp.where(t < s - 2, pltpu.roll(dgc, s - 2, 0), 0.0)
        dg_ref[...] = (cw_ref[2:3, :] * dgc + cw_ref[1:2, :] * n1 + cw_ref[0:1, :] * n2).astype(BF16)

    col = pl.BlockSpec((s, CONV_TC), lambda c: (0, c))
    outs, bufs = _call(
        body, name=name, jobs=jobs,
        out_shape=[jax.ShapeDtypeStruct((2, s, D_FF), BF16),
                   jax.ShapeDtypeStruct((3, D_FF), F32), jax.ShapeDtypeStruct((1, D_FF), F32)],
        grid=(nc,),
        in_specs=[col, pl.BlockSpec((s, CONV_TC), lambda c: (0, nc + c)),
                  pl.BlockSpec((3, CONV_TC), lambda c: (0, c)),
                  pl.BlockSpec((1, CONV_TC), lambda c: (0, c)), col],
        out_specs=[pl.BlockSpec((2, s, CONV_TC), lambda c: (0, 0, c)), pl.BlockSpec((3, CONV_TC), lambda c: (0, c)),
                   pl.BlockSpec((1, CONV_TC), lambda c: (0, c))],
        args=[up, up, conv_w, conv_b, dact], sem=("parallel",))
    return outs if jobs is None else (outs, bufs)


REGION = ATT_BLOCK * DILATIONS[-1]
SLOPE_TILE = (8, LANE)


def _slope_table():
    return jnp.broadcast_to(jnp.asarray(ALIBI_SLOPES, F32)[:, None, None], (ATT_HEADS,) + SLOPE_TILE)


def _sub(r, i, d):
    start = r + d * ATT_BLOCK * i
    return pl.ds(start, ATT_BLOCK) if d == 1 else pl.ds(start, ATT_BLOCK, stride=d)


def _att_bias(slope, d, first_key):
    qa = lax.broadcasted_iota(jnp.int32, (ATT_BLOCK, 2 * ATT_BLOCK), 0)
    cc = lax.broadcasted_iota(jnp.int32, (ATT_BLOCK, 2 * ATT_BLOCK), 1)
    dist = qa - cc + ATT_BLOCK
    ok = (dist >= 0) & (dist <= ATT_BLOCK) & (cc >= first_key)
    return jnp.where(ok, (slope * (-float(d))) * dist.astype(F32), NEG)


def _keys(kc_ref, kp_ref, r, i, d, nsub):
    prev = kp_ref[_sub(r, nsub - 1, d), :] if i == 0 else kc_ref[_sub(r, i - 1, d), :]
    return jnp.concatenate([prev, kc_ref[_sub(r, i, d), :]], axis=0)


def _per_residue(d, body):
    for r in range(d):
        body(r)


def attn_fwd(q, kv, jobs=None):
    s = q.shape[0]
    nreg = s // REGION
    scale = HEAD_DIM ** -0.5

    def body(sl_ref, q_ref, kc_ref, kp_ref, vc_ref, vp_ref, o_ref, lse_ref, ob, lb):
        n = pl.program_id(0)
        g = pl.program_id(2)
        slope = sl_ref[0:1, 0:1]
        first_key = jnp.where(n > 0, 0, ATT_BLOCK)

        def branch(gi, d):
            nsub = REGION // (ATT_BLOCK * d)
            bias = _att_bias(slope, d, 0)
            bias0 = _att_bias(slope, d, first_key)

            def residue(r):
                for i in range(nsub):
                    rows = _sub(r, i, d)
                    kcat = _keys(kc_ref, kp_ref, r, i, d, nsub)
                    vcat = _keys(vc_ref, vp_ref, r, i, d, nsub)
                    sc = _dot(q_ref[rows, :], kcat, 1, 1) * scale + (bias0 if i == 0 else bias)
                    m = jnp.max(sc, axis=-1, keepdims=True)
                    p = jnp.exp(sc - m)
                    l = jnp.sum(p, axis=-1, keepdims=True)
                    ob.at[gi][rows, :] = _dot(p, vcat, 1, 0) / l
                    lb.at[gi][rows, :] = jnp.broadcast_to(m + jnp.log(l), (ATT_BLOCK, HEAD_DIM))

            _per_residue(d, residue)

        for gi, d in enumerate(DILATIONS):
            @pl.when(g == gi)
            def _():
                branch(gi, d)

        @pl.when(g == len(DILATIONS) - 1)
        def _():
            def merge(c, carry):
                rows = pl.ds(pl.multiple_of(c * ATT_BLOCK, ATT_BLOCK), ATT_BLOCK)
                l0, l1, l2 = lb[0, rows, :], lb[1, rows, :], lb[2, rows, :]
                m = jnp.maximum(jnp.maximum(l0, l1), l2)
                e0, e1, e2 = jnp.exp(l0 - m), jnp.exp(l1 - m), jnp.exp(l2 - m)
                den = e0 + e1 + e2
                o_ref[rows, :] = (e0 * ob[0, rows, :] + e1 * ob[1, rows, :] + e2 * ob[2, rows, :]) / den
                lse_ref[rows, :] = m + jnp.log(den)
                return carry
            lax.fori_loop(0, REGION // ATT_BLOCK, merge, 0)

    def blk(col, prev=False):
        if prev:
            return pl.BlockSpec((REGION, HEAD_DIM), lambda n, h, g: (jnp.maximum(n - 1, 0), col(h, g)))
        return pl.BlockSpec((REGION, HEAD_DIM), lambda n, h, g: (n, col(h, g)))

    k_col = lambda h, g: h
    v_col = lambda h, g: ATT_HEADS + h
    outs, bufs = _call(
        body, name="attn_fwd", jobs=jobs,
        out_shape=[jax.ShapeDtypeStruct((s, ATT_HEADS * HEAD_DIM), F32)] * 2,
        grid=(nreg, ATT_HEADS, len(DILATIONS)),
        in_specs=[pl.BlockSpec((None,) + SLOPE_TILE, lambda n, h, g: (h, 0, 0)),
                  blk(lambda h, g: g * ATT_HEADS + h), blk(k_col), blk(k_col, True), blk(v_col), blk(v_col, True)],
        out_specs=[blk(k_col), blk(k_col)],
        scratch_shapes=[pltpu.VMEM((len(DILATIONS), REGION, HEAD_DIM), F32)] * 2,
        args=[_slope_table(), q, kv, kv, kv, kv], sem=("parallel", "parallel", "arbitrary"))
    return outs if jobs is None else (outs, bufs)


def attn_bwd(q, kv, o, lse, do, jobs=None):
    s = q.shape[0]
    nreg = s // REGION
    scale = HEAD_DIM ** -0.5

    def body(sl_ref, q_ref, kc_ref, kp_ref, vc_ref, vp_ref, o_ref, lse_ref, do_ref,
             qn_ref, on_ref, lsen_ref, don_ref, dq_ref, dkv_ref, dlt, dltn):
        n = pl.program_id(0)
        g = pl.program_id(2)
        slope = sl_ref[0:1, 0:1]
        first_key = jnp.where(n > 0, 0, ATT_BLOCK)
        has_next = n + 1 < nreg

        @pl.when(g == 0)
        def _():
            dkv_ref[...] = jnp.zeros_like(dkv_ref)

            def deltas(c, carry):
                rows = pl.ds(pl.multiple_of(c * ATT_BLOCK, ATT_BLOCK), ATT_BLOCK)
                dlt[rows, :] = jnp.sum(do_ref[rows, :] * o_ref[rows, :], axis=-1, keepdims=True)
                dltn[rows, :] = jnp.sum(don_ref[rows, :] * on_ref[rows, :], axis=-1, keepdims=True)
                return carry
            lax.fori_loop(0, REGION // ATT_BLOCK, deltas, 0)

        def branch(d):
            nsub = REGION // (ATT_BLOCK * d)
            bias = _att_bias(slope, d, 0)
            bias0 = _att_bias(slope, d, first_key)

            def residue(r):
                for i in range(nsub):
                    rows = _sub(r, i, d)
                    kcat = _keys(kc_ref, kp_ref, r, i, d, nsub)
                    vcat = _keys(vc_ref, vp_ref, r, i, d, nsub)
                    qb = q_ref[rows, :]
                    dob = do_ref[rows, :]
                    sc = _dot(qb, kcat, 1, 1) * scale + (bias0 if i == 0 else bias)
                    p = jnp.exp(sc - lse_ref[rows, :][:, 0:1])
                    ds = p * (_dot(dob, vcat, 1, 1) - dlt[rows, :])
                    dq_ref[rows, :] = _dot(ds, kcat, 1, 0) * scale
                    dk = _dot(ds, qb, 0, 0) * scale
                    dv = _dot(p, dob, 0, 0)
                    dkv_ref.at[0][rows, :] += dk[ATT_BLOCK:]
                    dkv_ref.at[1][rows, :] += dv[ATT_BLOCK:]
                    if i > 0:
                        prev = _sub(r, i - 1, d)
                        dkv_ref.at[0][prev, :] += dk[:ATT_BLOCK]
                        dkv_ref.at[1][prev, :] += dv[:ATT_BLOCK]

            _per_residue(d, residue)

            @pl.when(has_next)
            def _():
                bias_prev = bias[:, :ATT_BLOCK]

                def residue_next(r):
                    last = _sub(r, nsub - 1, d)
                    first = _sub(r, 0, d)
                    qb = qn_ref[first, :]
                    dob = don_ref[first, :]
                    sc = _dot(qb, kc_ref[last, :], 1, 1) * scale + bias_prev
                    p = jnp.exp(sc - lsen_ref[first, :][:, 0:1])
                    ds = p * (_dot(dob, vc_ref[last, :], 1, 1) - dltn[first, :])
                    dkv_ref.at[0][last, :] += _dot(ds, qb, 0, 0) * scale
                    dkv_ref.at[1][last, :] += _dot(p, dob, 0, 0)

                _per_residue(d, residue_next)

        for gi, d in enumerate(DILATIONS):
            @pl.when(g == gi)
            def _():
                branch(d)

    last_reg = nreg - 1

    def blk(col, shift=0):
        if shift < 0:
            return pl.BlockSpec((REGION, HEAD_DIM), lambda n, h, g: (jnp.maximum(n - 1, 0), col(h, g)))
        if shift > 0:
            return pl.BlockSpec((REGION, HEAD_DIM), lambda n, h, g: (jnp.minimum(n + 1, last_reg), col(h, g)))
        return pl.BlockSpec((REGION, HEAD_DIM), lambda n, h, g: (n, col(h, g)))

    q_col = lambda h, g: g * ATT_HEADS + h
    k_col = lambda h, g: h
    v_col = lambda h, g: ATT_HEADS + h
    outs, bufs = _call(
        body, name="attn_bwd", jobs=jobs,
        out_shape=[jax.ShapeDtypeStruct(q.shape, F32), jax.ShapeDtypeStruct((2, s, ATT_HEADS * HEAD_DIM), F32)],
        grid=(nreg, ATT_HEADS, len(DILATIONS)),
        in_specs=[pl.BlockSpec((None,) + SLOPE_TILE, lambda n, h, g: (h, 0, 0)),
                  blk(q_col), blk(k_col), blk(k_col, -1), blk(v_col), blk(v_col, -1),
                  blk(k_col), blk(k_col), blk(k_col),
                  blk(q_col, 1), blk(k_col, 1), blk(k_col, 1), blk(k_col, 1)],
        out_specs=[blk(q_col), pl.BlockSpec((2, REGION, HEAD_DIM), lambda n, h, g: (0, n, h))],
        scratch_shapes=[pltpu.VMEM((REGION, 1), F32)] * 2,
        args=[_slope_table(), q, kv, kv, kv, kv, o, lse, do, q, o, lse, do],
        sem=("parallel", "parallel", "arbitrary"))
    return outs if jobs is None else (outs, bufs)


def _adam(w, g, m, v):
    m = ADAM_B1 * m + (1.0 - ADAM_B1) * g
    v = ADAM_B2 * v + (1.0 - ADAM_B2) * (g * g)
    m_hat = m / (1.0 - ADAM_B1 ** ADAM_STEP)
    v_hat = v / (1.0 - ADAM_B2 ** ADAM_STEP)
    delta = -ADAM_LR * (m_hat / (jnp.sqrt(v_hat) + ADAM_EPS) + ADAM_WD * w)
    return delta, m, v


def adam_sharded(recvs, w, m, v, *, name):
    layers = len(recvs)
    n_src, r, c = recvs[0].shape
    tr = _rows(r, c)

    def body(*refs):
        p_refs = refs[:layers]
        w_ref, m_ref, v_ref, g_ref, d_ref, mo_ref, vo_ref = refs[layers:]
        for layer, p_ref in enumerate(p_refs):
            @pl.when(pl.program_id(0) == layer)
            def _():
                g = p_ref[0].astype(F32)
                for src in range(1, n_src):
                    g = g + p_ref[src].astype(F32)
                delta, m_new, v_new = _adam(w_ref[...], g, m_ref[...], v_ref[...])
                g_ref[...] = g
                d_ref[...] = delta
                mo_ref[...] = m_new
                vo_ref[...] = v_new

    blk = pl.BlockSpec((None, tr, c), lambda l, i: (l, i, 0))
    out = jax.ShapeDtypeStruct((layers, r, c), F32)
    part = [pl.BlockSpec((n_src, tr, c), functools.partial(lambda l, i, layer: (0, jnp.where(l == layer, i, 0), 0),
                                                            layer=layer)) for layer in range(layers)]
    return pl.pallas_call(
        body,
        name=name,
        out_shape=[out] * 4,
        grid=(layers, r // tr),
        in_specs=part + [blk, blk, blk],
        out_specs=[blk] * 4,
        compiler_params=_params("parallel", "parallel"),
    )(*recvs, w, m, v)


def sum_partials(parts):
    n_src, r, c = parts.shape

    def body(p_ref, o_ref):
        g = p_ref[0]
        for src in range(1, n_src):
            g = g + p_ref[src]
        o_ref[...] = g

    return pl.pallas_call(
        body,
        name="sum_small_grads",
        out_shape=jax.ShapeDtypeStruct((r, c), F32),
    )(parts)


def adam_packed(w, g, m, v):
    def body(w_ref, g_ref, m_ref, v_ref, d_ref, mo_ref, vo_ref):
        delta, m_new, v_new = _adam(w_ref[...], g_ref[...], m_ref[...], v_ref[...])
        d_ref[...] = delta
        mo_ref[...] = m_new
        vo_ref[...] = v_new

    out = jax.ShapeDtypeStruct(w.shape, F32)
    return pl.pallas_call(body, name="adam_small", out_shape=[out] * 3)(w, g, m, v)


def all_gather(srcs, *, name):
    n = len(srcs)

    def body(*refs):
        src, dst = refs[:n], refs[n:2 * n]
        send_sems, recv_sems, local_sems = refs[2 * n:]
        x, y, c, me = _place()
        sibling = (x, y, 1 - c)
        chips = [(1 - x, y), (x, 1 - y), (1 - x, 1 - y)]

        def index(px, py, pc):
            return 4 * px + 2 * py + pc

        def copy(p, k, block, to, from_src=False):
            slot = dst[p].at[index(*block)]
            return pltpu.make_async_remote_copy(
                src_ref=src[p] if from_src else slot, dst_ref=slot,
                send_sem=send_sems.at[p, k], recv_sem=recv_sems.at[p, k],
                device_id=to, device_id_type=MESH)

        mine = [pltpu.make_async_copy(src[p], dst[p].at[me], local_sems.at[p]) for p in range(n)]
        for cp in mine:
            cp.start()
        first = []
        for p in range(n):
            first.append(copy(p, 0, (x, y, c), sibling, from_src=True))
            for jj, chip in enumerate(chips):
                first.append(copy(p, 1 + jj, (x, y, c), (*chip, c), from_src=True))
        for cp in first:
            cp.start()
        passed = []
        for jj, chip in enumerate(chips):
            for p in range(n):
                copy(p, 1 + jj, (*chip, c), (x, y, c)).wait_recv()
                fwd = copy(p, 4 + jj, (*chip, c), sibling)
                fwd.start()
                passed.append(fwd)
        for p in range(n):
            copy(p, 0, sibling, (x, y, c)).wait_recv()
            for jj, chip in enumerate(chips):
                copy(p, 4 + jj, (*chip, 1 - c), (x, y, c)).wait_recv()
        for cp in first + passed:
            cp.wait_send()
        for cp in mine:
            cp.wait()

    return pl.pallas_call(
        body,
        name=name,
        out_shape=[jax.ShapeDtypeStruct((N_DEV,) + a.shape, a.dtype) for a in srcs],
        in_specs=[ANY] * n,
        out_specs=[ANY] * n,
        scratch_shapes=[pltpu.SemaphoreType.DMA((n, 7)), pltpu.SemaphoreType.DMA((n, 7)),
                        pltpu.SemaphoreType.DMA((n,))],
    )(*srcs)


def exchange_only(*, name, jobs):
    def body(o_ref):
        o_ref[...] = jnp.zeros_like(o_ref)

    _, bufs = _call(body, name=name, jobs=jobs, out_shape=[jax.ShapeDtypeStruct((8, LANE), F32)], grid=(1,),
                    in_specs=[], out_specs=[pl.BlockSpec((8, LANE), lambda i: (0, 0))], args=[], sem=("arbitrary",))
    return None, bufs


def _pack_rows(parts, rows):
    flat = jnp.concatenate([p.reshape(-1) for p in parts])
    return jnp.pad(flat, (0, rows * LANE - flat.shape[0])).reshape(rows, LANE)


def _unpack_rows(packed, shapes):
    flat = packed.reshape(-1)
    out, at = [], 0
    for sh in shapes:
        size = 1
        for dim in sh:
            size *= dim
        out.append(flat[at:at + size].reshape(sh))
        at += size
    return out


CONV_W_PAD = 768
SMALL_W_ROWS = 56


def _pack_small_weights(w_a2, b_a2, hn, conv_w):
    cw = jnp.pad(conv_w.reshape(6, -1), ((0, 0), (0, CONV_W_PAD - conv_w.shape[-1]))).reshape(-1, LANE)
    rows = jnp.concatenate([w_a2[0], b_a2, jnp.pad(hn, ((0, 0), (0, LANE - hn.shape[-1]))), cw], axis=0)
    return jnp.pad(rows, ((0, SMALL_W_ROWS - rows.shape[0]), (0, 0)))


def _unpack_small_weights(gathered):
    w_a2 = gathered[:, 0:GATE_RANK, :].transpose(1, 0, 2).reshape(GATE_RANK, GLA_KEY_DIM)
    b_a2 = gathered[:, GATE_RANK, :].reshape(1, GLA_KEY_DIM)
    hn = gathered[:, GATE_RANK + 1, :GLA_DV // N_DEV].reshape(1, GLA_DV)
    per = D_FF // N_DEV
    cw = gathered[:, GATE_RANK + 2:GATE_RANK + 2 + 6 * CONV_W_PAD // LANE, :].reshape(N_DEV, 6, CONV_W_PAD)[:, :, :per]
    cw = cw.reshape(N_DEV, 2, 3, per).transpose(1, 2, 0, 3).reshape(2, 3, D_FF)
    return w_a2, b_a2, hn, cw


SCHEDULE = {
    "gla_in": [("g1", "gout", None), ("g1", "up0", (0, 1024))],
    "gla_fwd": [("g2", "gout", None), ("g2", "up0", (0, 1024)), ("g1", "up0", (1024, 2048))],
    "gla_out": [("g2", "up0", (1024, 2048)), ("g1", "dn0", (0, 352))],
    "ffn_up0": [("g2", "dn0", (0, 352)), ("g1", "dn0", (352, 704)), ("g1", "kv", None), ("g1", "q", (0, 768))],
    "convglu_fwd0": [("g2", "dn0", (352, 704))],
    "ffn_down0": [("g2", "kv", None), ("g2", "q", (0, 768)), ("g1", "q", (768, 2048)), ("g1", "dout", None)],
    "kv_proj": [("g2", "q", (768, 2048)), ("g2", "dout", None), ("g1", "up1", (0, 704))],
    "q_proj": [("g2", "up1", (0, 704)), ("g1", "up1", (704, 1664))],
    "attn_fwd": [("g2", "up1", (704, 1664)), ("g1", "up1", (1664, 2048)), ("g1", "dn1", None)],
    "dsa_out": [("g2", "up1", (1664, 2048)), ("g2", "dn1", None)],
    "ffn_down_dx1": [("sc", "dn1", (0, 352))],
    "convglu_bwd1": [("sc", "dn1", (352, 704))],
    "ffn_up_dx1": [("sc", "up1", (0, 1024))],
    "attn_bwd": [("sc", "up1", (1024, 2048)), ("sc", "dout", None)],
    "q_proj_dx": [("sc", "q", (0, 1024))],
    "kv_proj_dw": [("sc", "q", (1024, 1792))],
    "kv_proj_dx": [("sc", "q", (1792, 2048)), ("sc", "kv", (0, 768))],
    "ffn_down_dw0": [("sc", "kv", (768, 2048))],
    "ffn_down_dx0": [("sc", "dn0", (0, 384))],
    "convglu_bwd0": [("sc", "dn0", (384, 704))],
    "ffn_up_dx0": [("sc", "up0", (0, 1024))],
    "gla_out_dw": [("sc", "up0", (1024, 1216))],
    "gla_out_dx": [("sc", "up0", (1216, 1408))],
    "gla_bwd": [("sc", "up0", (1408, 2048)), ("sc", "gout", (0, 128))],
    "gla_in_dw": [("sc", "gout", (128, 256))],
    "gla_in_dx": [("sc", "in", (0, 1536))],
    "grads_tail": [("sc", "in", (1536, 2048))],
}
ROW_SHARDED = ("gout", "dout", "dn0", "dn1")


class Plan:
    def __init__(self, weights, srcs=None):
        self.w = dict(weights)
        self.srcs = srcs
        self.grads = {}
        self.recv = {}
        self._names = None

    def weight(self, name):
        buf = self.w[name]
        if name in ROW_SHARDED:
            return buf.reshape(1, buf.shape[0] * buf.shape[1], buf.shape[2])
        return buf

    def jobs(self, call):
        ops = SCHEDULE.get(call)
        if self.srcs is None or not ops:
            return None
        jobs, handles = Jobs(), {}
        for op, name, rows in ops:
            store = self.recv if op == "sc" else self.w
            if name not in handles:
                if name in store:
                    handles[name] = jobs.thru(store[name])
                elif op == "sc":
                    handles[name] = jobs.new(self.grads[name].shape, BF16)
                else:
                    handles[name] = jobs.new((N_DEV,) + self.srcs[name].shape, BF16)
            if op == "g1":
                jobs.gather_ici(self.srcs[name], handles[name], rows)
            elif op == "g2":
                jobs.gather_d2d(handles[name], rows)
            else:
                jobs.scatter(self.grads[name], handles[name], rows)
        self._names = [(name, self.recv if ops[0][0] == "sc" else self.w) for name in handles]
        assert len({op == "sc" for op, _, _ in ops}) == 1
        return jobs

    def run(self, call, fn, *args, **kwargs):
        jobs = self.jobs(call)
        if jobs is None:
            return fn(*args, **kwargs)
        out, bufs = fn(*args, jobs=jobs, **kwargs)
        for (name, store), buf in zip(self._names, bufs):
            store[name] = buf
        return out


def _ffn_fwd(plan, h, norm_g, conv_w, conv_b, tag):
    (n,) = rms_fwd(h, [norm_g], name=f"ffn_norm_fwd{tag}")
    up = plan.run(f"ffn_up{tag}", mm_nn, n, plan.weight(f"up{tag}"), out_dtype=BF16, name=f"ffn_up{tag}")
    act = plan.run(f"convglu_fwd{tag}", convglu_fwd, up, conv_w, conv_b, name=f"convglu_fwd{tag}")
    h_out = plan.run(f"ffn_down{tag}", mm_nn, act, plan.weight(f"dn{tag}"), out_dtype=F32, res=h,
                     name=f"ffn_down{tag}")
    return h_out, (n, up, act)


def _by_rows(dw):
    return dw.reshape(N_DEV, dw.shape[1] // N_DEV, dw.shape[2])


def _ffn_bwd(plan, dh_out, h, saved, norm_g, conv_w, conv_b, tag):
    n, up, act = saved
    plan.grads[f"dn{tag}"] = _by_rows(plan.run(f"ffn_down_dw{tag}", mm_tn, act, dh_out, 1, name=f"ffn_down_dw{tag}"))
    dact = plan.run(f"ffn_down_dx{tag}", mm_nt, dh_out, plan.weight(f"dn{tag}"), out_dtype=BF16,
                    name=f"ffn_down_dx{tag}")
    dup, dconv_w, dconv_b = plan.run(f"convglu_bwd{tag}", convglu_bwd, up, conv_w, conv_b, dact,
                                     name=f"convglu_bwd{tag}")
    plan.grads[f"up{tag}"] = mm_tn(n, dup, N_DEV, name=f"ffn_up_dw{tag}")
    dh, (dnorm,) = plan.run(f"ffn_up_dx{tag}", mm_nt, dup, plan.weight(f"up{tag}"), out_dtype=F32,
                            name=f"ffn_up_dx{tag}", norm=(h, dh_out, [norm_g], []))
    return dh, dnorm, dconv_w, dconv_b


def local_step(x, target, wts, plan):
    row = lambda v: v.reshape(1, -1)
    attn_norm, ffn_norm = wts["attn_norm"], wts["ffn_norm"]
    conv_w, conv_b = wts["ffn_conv_w"], wts["ffn_conv_b"]

    (n1,) = rms_fwd(x, [row(attn_norm[0])], name="attn_norm_fwd0")
    proj = plan.run("gla_in", mm_nn, n1, wts["gla_w_in"], out_dtype=F32, name="gla_in")
    la = gate_fwd(proj, wts["gla_w_a2"], wts["gla_b_a2"])
    o_gla, states = plan.run("gla_fwd", gla_fwd, proj, la)
    og = headnorm_fwd(o_gla, proj, wts["gla_head_norm"])
    h1 = plan.run("gla_out", mm_nn, og, plan.weight("gout"), out_dtype=F32, res=x, name="gla_out")
    h2, ffn0 = _ffn_fwd(plan, h1, row(ffn_norm[0]), conv_w[0], row(conv_b[0]), "0")

    kvn, n3 = rms_fwd(h2, [row(wts["kv_norm"]), row(attn_norm[1])], name="kv_attn_norm_fwd")
    kv = plan.run("kv_proj", mm_nn, kvn, plan.weight("kv"), out_dtype=F32, name="kv_proj")
    q = plan.run("q_proj", mm_nn, n3, plan.weight("q"), out_dtype=F32, name="q_proj")
    o_att, lse = plan.run("attn_fwd", attn_fwd, q, kv)
    h3 = plan.run("dsa_out", mm_nn, o_att, plan.weight("dout"), out_dtype=F32, res=h2, name="dsa_out")
    h4, ffn1 = _ffn_fwd(plan, h3, row(ffn_norm[1]), conv_w[1], row(conv_b[1]), "1")

    loss_tile, dh4, d_final = loss_head(h4, row(wts["final_norm"]), target)

    dh3, d_ffn1, dcw1, dcb1 = _ffn_bwd(plan, dh4, h3, ffn1, row(ffn_norm[1]), conv_w[1], row(conv_b[1]), "1")
    plan.grads["dout"] = _by_rows(mm_tn(o_att, dh3, 1, name="dsa_out_dw"))
    do_att = mm_nt(dh3, plan.weight("dout"), out_dtype=F32, name="dsa_out_dx")
    dq, dkv = plan.run("attn_bwd", attn_bwd, q, kv, o_att, lse, do_att)
    plan.grads["q"] = mm_tn(n3, dq, N_DEV, name="q_proj_dw")
    dn3 = plan.run("q_proj_dx", mm_nt, dq, plan.weight("q"), out_dtype=F32, name="q_proj_dx")
    plan.grads["kv"] = plan.run("kv_proj_dw", mm_tn, kvn, dkv, N_DEV, name="kv_proj_dw")
    dh2, (d_kvnorm, d_attn1) = plan.run("kv_proj_dx", mm_nt, dkv, plan.weight("kv"), out_dtype=F32, name="kv_proj_dx",
                                        norm=(h2, dh3, [row(wts["kv_norm"]), row(attn_norm[1])], [dn3]))
    dh1, d_ffn0, dcw0, dcb0 = _ffn_bwd(plan, dh2, h1, ffn0, row(ffn_norm[0]), conv_w[0], row(conv_b[0]), "0")
    plan.grads["gout"] = _by_rows(plan.run("gla_out_dw", mm_tn, og, dh1, 1, name="gla_out_dw"))
    dog = plan.run("gla_out_dx", mm_nt, dh1, plan.weight("gout"), out_dtype=F32, name="gla_out_dx")
    do_gla, dr, d_hn = headnorm_bwd(o_gla, proj, wts["gla_head_norm"], dog)
    dq_g, dk_g, dv_g, dla = plan.run("gla_bwd", gla_bwd, proj, la, states, do_gla)
    da, dw_a2p, db_a2 = gate_bwd(proj, wts["gla_w_a2"], wts["gla_b_a2"], dla)
    dproj = jnp.concatenate([dq_g, dk_g, dv_g, dr, da], axis=1)
    assert dproj.shape[1] == GLA_IN_PAD
    dw_in = plan.run("gla_in_dw", mm_tn, n1, dproj, 1, name="gla_in_dw")
    plan.grads["in"] = dw_in[0, :, :GLA_IN_DIM].reshape(D_MODEL, N_DEV, GLA_IN_DIM // N_DEV).transpose(1, 0, 2)
    grad_x, (d_attn0,) = plan.run("gla_in_dx", mm_nt, dproj, wts["gla_w_in"], out_dtype=F32, name="gla_in_dx",
                                  norm=(x, dh1, [row(attn_norm[0])], []))

    small = dict(
        attn_norm=jnp.concatenate([d_attn0, d_attn1], axis=0),
        ffn_norm=jnp.concatenate([d_ffn0, d_ffn1], axis=0),
        kv_norm=d_kvnorm.reshape(-1),
        final_norm=d_final.reshape(-1),
        ffn_conv_b=jnp.concatenate([dcb0, dcb1], axis=0),
        gla_w_a2=dw_a2p[:GATE_RANK],
        gla_b_a2=db_a2,
        gla_head_norm=d_hn,
        ffn_conv_w=jnp.stack([dcw0, dcw1]),
    )
    return loss_tile, grad_x, small


SMALL_ORDER = ("attn_norm", "ffn_norm", "kv_norm", "final_norm", "ffn_conv_b",
               "gla_w_a2", "gla_b_a2", "gla_head_norm", "ffn_conv_w")
SMALL_FULL = dict(attn_norm=(2, D_MODEL), ffn_norm=(2, D_MODEL), kv_norm=(D_MODEL,), final_norm=(D_MODEL,),
                  ffn_conv_b=(2, D_FF), gla_w_a2=(GATE_RANK, GLA_KEY_DIM), gla_b_a2=(1, GLA_KEY_DIM),
                  gla_head_norm=(1, GLA_DV), ffn_conv_w=(2, 3, D_FF))
SMALL_SHARDED = ("gla_w_a2", "gla_b_a2", "gla_head_norm", "ffn_conv_w")
SMALL_GRAD_ROWS = 592
SMALL_ADAM_ROWS = 240


def kernel(x, attn_norm, gla_w_in, gla_w_a2, gla_b_a2, gla_head_norm, gla_w_out, kv_norm, w_kv, dsa_w_q, dsa_w_out, ffn_norm, ffn_w_up, ffn_conv_w, ffn_conv_b, ffn_w_down, final_norm, loss_target, m_attn_norm, m_gla_w_in, m_gla_w_a2, m_gla_b_a2, m_gla_head_norm, m_gla_w_out, m_kv_norm, m_w_kv, m_dsa_w_q, m_dsa_w_out, m_ffn_norm, m_ffn_w_up, m_ffn_conv_w, m_ffn_conv_b, m_ffn_w_down, m_final_norm, v_attn_norm, v_gla_w_in, v_gla_w_a2, v_gla_b_a2, v_gla_head_norm, v_gla_w_out, v_kv_norm, v_w_kv, v_dsa_w_q, v_dsa_w_out, v_ffn_norm, v_ffn_w_up, v_ffn_conv_w, v_ffn_conv_b, v_ffn_w_down, v_final_norm):
    me = 4 * lax.axis_index("x") + 2 * lax.axis_index("y") + lax.axis_index("c")
    bf = lambda a: a.astype(BF16)

    g_in, g_small = all_gather([bf(gla_w_in[0]), _pack_small_weights(gla_w_a2, gla_b_a2, gla_head_norm, ffn_conv_w)],
                               name="gather_first")
    w_a2_full, b_a2_full, hn_full, conv_w_full = _unpack_small_weights(g_small)
    w_in_full = jnp.pad(g_in.transpose(1, 0, 2).reshape(D_MODEL, GLA_IN_DIM), ((0, 0), (0, GLA_IN_PAD - GLA_IN_DIM)))
    wts = dict(
        attn_norm=attn_norm, ffn_norm=ffn_norm, kv_norm=kv_norm, final_norm=final_norm, ffn_conv_b=ffn_conv_b,
        gla_w_in=w_in_full[None],
        gla_w_a2=jnp.pad(bf(w_a2_full), ((0, LANE - GATE_RANK), (0, 0))),
        gla_b_a2=b_a2_full, gla_head_norm=hn_full, ffn_conv_w=conv_w_full,
    )
    plan = Plan({}, srcs=dict(gout=bf(gla_w_out[0]), kv=bf(w_kv), q=bf(dsa_w_q[0]), dout=bf(dsa_w_out[0]),
                              up0=bf(ffn_w_up[0]), up1=bf(ffn_w_up[1]), dn0=bf(ffn_w_down[0]), dn1=bf(ffn_w_down[1])))

    loss_tile, grad_x, small = local_step(x[0], loss_target[0], wts, plan)
    loss = lax.psum(loss_tile[0, 0], ("x", "y", "c"))

    plan.run("grads_tail", exchange_only, name="grads_tail")
    shard3 = lambda a: a.reshape((-1,) + a.shape[-2:])
    big_params = dict(gla_w_in=(("in",), gla_w_in, m_gla_w_in, v_gla_w_in),
                      gla_w_out=(("gout",), gla_w_out, m_gla_w_out, v_gla_w_out),
                      w_kv=(("kv",), w_kv, m_w_kv, v_w_kv),
                      dsa_w_q=(("q",), dsa_w_q, m_dsa_w_q, v_dsa_w_q),
                      dsa_w_out=(("dout",), dsa_w_out, m_dsa_w_out, v_dsa_w_out),
                      ffn_w_up=(("up0", "up1"), ffn_w_up, m_ffn_w_up, v_ffn_w_up),
                      ffn_w_down=(("dn0", "dn1"), ffn_w_down, m_ffn_w_down, v_ffn_w_down))
    res = {}
    for nm, (parts, w, m, v) in big_params.items():
        outs = adam_sharded([plan.recv[p] for p in parts], shard3(w), shard3(m), shard3(v), name=f"adam_{nm}")
        res[nm] = [o.reshape(w.shape) for o in outs]

    packed = _pack_rows([small[nm] for nm in SMALL_ORDER], SMALL_GRAD_ROWS)
    (parts,) = all_gather([packed], name="gather_small_grads")
    full = dict(zip(SMALL_ORDER, _unpack_rows(sum_partials(parts), [SMALL_FULL[nm] for nm in SMALL_ORDER])))
    local_w = dict(attn_norm=attn_norm, ffn_norm=ffn_norm, kv_norm=kv_norm, final_norm=final_norm,
                   ffn_conv_b=ffn_conv_b, gla_w_a2=gla_w_a2, gla_b_a2=gla_b_a2, gla_head_norm=gla_head_norm,
                   ffn_conv_w=ffn_conv_w)
    local_m = dict(attn_norm=m_attn_norm, ffn_norm=m_ffn_norm, kv_norm=m_kv_norm, final_norm=m_final_norm,
                   ffn_conv_b=m_ffn_conv_b, gla_w_a2=m_gla_w_a2, gla_b_a2=m_gla_b_a2, gla_head_norm=m_gla_head_norm,
                   ffn_conv_w=m_ffn_conv_w)
    local_v = dict(attn_norm=v_attn_norm, ffn_norm=v_ffn_norm, kv_norm=v_kv_norm, final_norm=v_final_norm,
                   ffn_conv_b=v_ffn_conv_b, gla_w_a2=v_gla_w_a2, gla_b_a2=v_gla_b_a2, gla_head_norm=v_gla_head_norm,
                   ffn_conv_w=v_ffn_conv_w)
    local_g = {}
    for nm in SMALL_ORDER:
        gfull = full[nm]
        if nm in SMALL_SHARDED:
            per = gfull.shape[-1] // N_DEV
            gfull = lax.dynamic_slice_in_dim(gfull, me * per, per, axis=gfull.ndim - 1)
        local_g[nm] = gfull.reshape(local_w[nm].shape)
    shapes = [local_w[nm].shape for nm in SMALL_ORDER]
    pk = lambda dd: _pack_rows([dd[nm] for nm in SMALL_ORDER], SMALL_ADAM_ROWS)
    d_p, m_p, v_p = adam_packed(pk(local_w), pk(local_g), pk(local_m), pk(local_v))
    for nm, dl, mn, vn in zip(SMALL_ORDER, _unpack_rows(d_p, shapes), _unpack_rows(m_p, shapes),
                              _unpack_rows(v_p, shapes)):
        res[nm] = [local_g[nm], dl, mn, vn]

    order = ("attn_norm", "gla_w_in", "gla_w_a2", "gla_b_a2", "gla_head_norm", "gla_w_out", "kv_norm", "w_kv",
             "dsa_w_q", "dsa_w_out", "ffn_norm", "ffn_w_up", "ffn_conv_w", "ffn_conv_b", "ffn_w_down", "final_norm")
    outs = [loss, grad_x[None]]
    for kind in range(4):
        outs.extend(res[nm][kind] for nm in order)
    return tuple(outs)
```

```python
import functools

import jax
import jax.numpy as jnp
from jax import lax
from jax.experimental import pallas as pl
from jax.experimental.pallas import tpu as pltpu

F32 = jnp.float32
BF16 = jnp.bfloat16
MESH = pl.DeviceIdType.MESH
ANY = pl.BlockSpec(memory_space=pl.ANY)

N_DEV = 8
D_MODEL = 2048
GLA_HEADS = 4
GLA_KEY_DIM = 1024
GLA_VAL_DIM = 2048
GLA_DK = 256
GLA_DV = 512
GATE_RANK = 16
GATE_NORMALIZER = 16.0
GLA_CHUNK = 64
GLA_STEP_CHUNKS = 2
GLA_IN_DIM = 2 * GLA_KEY_DIM + 2 * GLA_VAL_DIM + GATE_RANK
GLA_IN_PAD = 6272
ATT_HEADS = 16
HEAD_DIM = 128
DILATIONS = (1, 4, 16)
STREAMS = DILATIONS[-1]
ATT_BLOCK = 128
D_FF = 5632
EPS = 1e-6
ADAM_LR = 0.001
ADAM_B1 = 0.9
ADAM_B2 = 0.999
ADAM_EPS = 1e-08
ADAM_WD = 0.01
ADAM_STEP = 10
NEG = -1e30
LANE = 128
NORM_ROWS = 64
VMEM_LIMIT = 52 * 1024 * 1024
ALIBI_SLOPES = tuple(2.0 ** (-0.5 * (i + 1)) for i in range(ATT_HEADS))


def _params(*sem):
    return pltpu.CompilerParams(dimension_semantics=sem, vmem_limit_bytes=VMEM_LIMIT)


def _tile(n, cap):
    best = None
    for t in range(LANE, min(n, cap) + 1, LANE):
        if n % t == 0:
            best = t
    return best if best is not None else n


def _shard_group(j, ns, cap):
    best = 1
    for g in range(1, j + 1):
        if j % g == 0 and g * ns <= cap:
            best = g
    return best


def _rows(r, c, budget=256 * 1024):
    best = None
    for t in range(16, r + 1, 16):
        if r % t == 0 and t * c <= budget:
            best = t
    return best if best is not None else r


def _flip(coord, bit):
    return 1 - coord if bit else coord


def _place():
    x, y, c = lax.axis_index("x"), lax.axis_index("y"), lax.axis_index("c")
    return x, y, c, 4 * x + 2 * y + c


def _rows_of(ref, rows):
    return ref if rows is None else ref.at[pl.ds(rows[0], rows[1] - rows[0])]


class Jobs:
    def __init__(self):
        self.srcs = []
        self.bufs = []
        self.sems = []
        self.steps = []

    def _src(self, a):
        for i, b in enumerate(self.srcs):
            if b is a:
                return i
        self.srcs.append(a)
        return len(self.srcs) - 1

    def new(self, shape, dtype):
        self.bufs.append((None, jax.ShapeDtypeStruct(shape, dtype)))
        return len(self.bufs) - 1

    def thru(self, a):
        self.bufs.append((a, jax.ShapeDtypeStruct(a.shape, a.dtype)))
        return len(self.bufs) - 1

    def _sem(self, n):
        self.sems.append(pltpu.SemaphoreType.DMA((n,)))
        return len(self.sems) - 1

    def gather_ici(self, src, buf, rows=None):
        si, send, recv, loc = self._src(src), self._sem(4), self._sem(4), self._sem(1)

        def remote(srcs, bufs, sems, slot_of):
            x, y, c, me = _place()
            peers = [(x, y, 1 - c), (1 - x, y, c), (x, 1 - y, c), (1 - x, 1 - y, c)]
            return [pltpu.make_async_remote_copy(
                src_ref=_rows_of(srcs[si], rows),
                dst_ref=_rows_of(bufs[buf].at[me if slot_of == "mine" else 4 * p[0] + 2 * p[1] + p[2]], rows),
                send_sem=sems[send].at[k], recv_sem=sems[recv].at[k], device_id=p, device_id_type=MESH)
                for k, p in enumerate(peers)]

        def local(srcs, bufs, sems):
            return pltpu.make_async_copy(_rows_of(srcs[si], rows), _rows_of(bufs[buf].at[_place()[3]], rows),
                                         sems[loc].at[0])

        def start(srcs, bufs, sems):
            local(srcs, bufs, sems).start()
            for cp in remote(srcs, bufs, sems, "mine"):
                cp.start()

        def finish(srcs, bufs, sems):
            for cp in remote(srcs, bufs, sems, "peer"):
                cp.wait_recv()
            for cp in remote(srcs, bufs, sems, "mine"):
                cp.wait_send()
            local(srcs, bufs, sems).wait()

        self.steps.append((start, finish))

    def gather_d2d(self, buf, rows=None):
        send, recv = self._sem(3), self._sem(3)

        def copies(bufs, sems, core):
            x, y, c, _ = _place()
            cc = c if core == "mine" else 1 - c
            chips = [(1 - x, y), (x, 1 - y), (1 - x, 1 - y)]
            return [pltpu.make_async_remote_copy(
                src_ref=_rows_of(bufs[buf].at[4 * px + 2 * py + cc], rows),
                dst_ref=_rows_of(bufs[buf].at[4 * px + 2 * py + cc], rows),
                send_sem=sems[send].at[k], recv_sem=sems[recv].at[k],
                device_id=(x, y, 1 - c), device_id_type=MESH) for k, (px, py) in enumerate(chips)]

        def start(srcs, bufs, sems):
            for cp in copies(bufs, sems, "mine"):
                cp.start()

        def finish(srcs, bufs, sems):
            for cp in copies(bufs, sems, "sibling"):
                cp.wait_recv()
            for cp in copies(bufs, sems, "mine"):
                cp.wait_send()

        self.steps.append((start, finish))

    def scatter(self, src, buf, rows=None):
        si, send, recv, loc = self._src(src), self._sem(N_DEV - 1), self._sem(N_DEV - 1), self._sem(1)

        def remote(srcs, bufs, sems, slot_of):
            x, y, c, me = _place()
            out = []
            for k in range(1, N_DEV):
                px, py, pc = _flip(x, k >> 2), _flip(y, (k >> 1) & 1), _flip(c, k & 1)
                peer = 4 * px + 2 * py + pc
                out.append(pltpu.make_async_remote_copy(
                    src_ref=_rows_of(srcs[si].at[peer], rows),
                    dst_ref=_rows_of(bufs[buf].at[me if slot_of == "mine" else peer], rows),
                    send_sem=sems[send].at[k - 1], recv_sem=sems[recv].at[k - 1],
                    device_id=(px, py, pc), device_id_type=MESH))
            return out

        def local(srcs, bufs, sems):
            me = _place()[3]
            return pltpu.make_async_copy(_rows_of(srcs[si].at[me], rows), _rows_of(bufs[buf].at[me], rows),
                                         sems[loc].at[0])

        def start(srcs, bufs, sems):
            local(srcs, bufs, sems).start()
            for cp in remote(srcs, bufs, sems, "mine"):
                cp.start()

        def finish(srcs, bufs, sems):
            for cp in remote(srcs, bufs, sems, "peer"):
                cp.wait_recv()
            for cp in remote(srcs, bufs, sems, "mine"):
                cp.wait_send()
            local(srcs, bufs, sems).wait()

        self.steps.append((start, finish))


def _call(body, *, name, grid, in_specs, out_specs, out_shape, args, sem, scratch_shapes=(), jobs=None):
    in_specs, out_specs, out_shape = list(in_specs), list(out_specs), list(out_shape)
    scratch_shapes = list(scratch_shapes)
    if jobs is None:
        res = pl.pallas_call(body, name=name, out_shape=out_shape, grid=grid, in_specs=in_specs,
                             out_specs=out_specs, scratch_shapes=scratch_shapes,
                             compiler_params=_params(*sem))(*args)
        return list(res), []
    thru = [a for a, _ in jobs.bufs if a is not None]
    n_in, n_src, n_thru = len(args), len(jobs.srcs), len(thru)
    n_out, n_buf, n_scr = len(out_shape), len(jobs.bufs), len(scratch_shapes)
    aliases, t = {}, 0
    for b, (a, _) in enumerate(jobs.bufs):
        if a is not None:
            aliases[n_in + n_src + t] = n_out + b
            t += 1

    def wrapped(*refs):
        at = 0
        ins = refs[at:at + n_in]; at += n_in
        srcs = refs[at:at + n_src]; at += n_src + n_thru
        outs = refs[at:at + n_out]; at += n_out
        bufs = refs[at:at + n_buf]; at += n_buf
        scr = refs[at:at + n_scr]; at += n_scr
        sems = refs[at:]
        first, last = None, None
        for axis, size in enumerate(grid):
            pid = pl.program_id(axis)
            f, l = pid == 0, pid == size - 1
            first = f if first is None else first & f
            last = l if last is None else last & l

        @pl.when(first)
        def _():
            for start, _ in jobs.steps:
                start(srcs, bufs, sems)

        body(*ins, *outs, *scr)

        @pl.when(last)
        def _():
            for _, finish in jobs.steps:
                finish(srcs, bufs, sems)

    res = pl.pallas_call(
        wrapped, name=name,
        out_shape=out_shape + [s for _, s in jobs.bufs],
        grid=grid,
        in_specs=in_specs + [ANY] * (n_src + n_thru),
        out_specs=out_specs + [ANY] * n_buf,
        scratch_shapes=scratch_shapes + jobs.sems,
        input_output_aliases=aliases,
        compiler_params=_params(*(["arbitrary"] * len(grid))),
    )(*args, *jobs.srcs, *thru)
    return res[:n_out], res[n_out:]


def mm_nn(a, w, *, out_dtype, name, res=None, tm=None, jobs=None):
    m, k = a.shape
    j, k2, ns = w.shape
    whole = j == 1 and ns <= 2048 and k <= 2048
    tm = tm or (1024 if a.dtype == BF16 and not whole else 512)
    assert k == k2 and m % tm == 0
    tn = ns if whole else _tile(ns, 1408)
    nsub = ns // tn
    tk = k if k <= 2048 else _tile(k, 1408)
    nk = k // tk
    has_res = res is not None

    def body(*refs):
        if has_res:
            a_ref, w_ref, r_ref, o_ref, acc = refs
        else:
            a_ref, w_ref, o_ref, acc = refs
        kk = pl.program_id(2)

        @pl.when(kk == 0)
        def _():
            acc[...] = jnp.zeros_like(acc)

        acc[...] += jnp.dot(a_ref[...].astype(BF16), w_ref[...], preferred_element_type=F32)

        @pl.when(kk == nk - 1)
        def _():
            r = acc[...]
            if has_res:
                r = r + r_ref[...]
            o_ref[...] = r.astype(out_dtype)

    in_specs = [
        pl.BlockSpec((tm, tk), lambda i, n, kk: (i, kk)),
        pl.BlockSpec((None, tk, tn), lambda i, n, kk: (n // nsub, kk, n % nsub)),
    ]
    args = [a, w]
    out_tile = pl.BlockSpec((tm, tn), lambda i, n, kk: (i, n))
    if has_res:
        in_specs.append(out_tile)
        args.append(res)
    (out,), bufs = _call(
        body, name=name, jobs=jobs,
        out_shape=[jax.ShapeDtypeStruct((m, j * ns), out_dtype)],
        grid=(m // tm, j * nsub, nk),
        in_specs=in_specs,
        out_specs=[out_tile],
        scratch_shapes=[pltpu.VMEM((tm, tn), F32)],
        args=args, sem=("parallel", "parallel", "arbitrary"))
    return out if jobs is None else (out, bufs)


def mm_nt(dy, w, *, out_dtype, name, tm=None, jobs=None, norm=None):
    parts, m, n = (1,) + dy.shape if dy.ndim == 2 else dy.shape
    n *= parts
    j, k, ns = w.shape
    if norm is not None:
        x, dres, gains, more = norm
        tm = tm or (256 if more else 512)
    tm = tm or 1024
    assert n == j * ns and m % tm == 0
    tn = _tile(ns, 2048)
    nsub = ns // tn
    jb = _shard_group(j // parts, ns, 2048 if norm is None else 1024) if nsub == 1 else 1
    tko = _tile(k, 1408) if norm is None else k
    nn = j * nsub // jb
    per_part = nn // parts
    if dy.ndim == 2:
        dy_spec = pl.BlockSpec((tm, jb * tn), lambda i, ko, nq: (i, nq))
    else:
        dy_spec = pl.BlockSpec((None, tm, jb * tn), lambda i, ko, nq: (nq // per_part, i, nq % per_part))
    if jb == 1:
        w_spec = pl.BlockSpec((None, tko, tn), lambda i, ko, nq: (nq // nsub, ko, nq % nsub))
    else:
        w_spec = pl.BlockSpec((jb, tko, ns), lambda i, ko, nq: (nq, ko, 0))

    n_gain = 0 if norm is None else len(gains)
    n_more = 0 if norm is None else len(more)

    def body(*refs):
        a_ref, w_ref = refs[:2]
        acc = refs[-1]
        nq = pl.program_id(2)
        first = pl.program_id(0) == 0

        @pl.when(nq == 0)
        def _():
            acc[...] = jnp.zeros_like(acc)

        if jb == 1:
            acc[...] += lax.dot_general(a_ref[...].astype(BF16), w_ref[...], (((1,), (1,)), ((), ())),
                                        preferred_element_type=F32)
        else:
            part = acc[...]
            for jj in range(jb):
                part = part + lax.dot_general(a_ref[:, jj * ns:(jj + 1) * ns].astype(BF16), w_ref[jj],
                                              (((1,), (1,)), ((), ())), preferred_element_type=F32)
            acc[...] = part

        @pl.when(nq == nn - 1)
        def _():
            if norm is None:
                refs[2][...] = acc[...].astype(out_dtype)
                return
            x_ref, r_ref = refs[2:4]
            g_refs = refs[4:4 + n_gain]
            e_refs = refs[4 + n_gain:4 + n_gain + n_more]
            dx_ref = refs[4 + n_gain + n_more]
            dg_refs = refs[5 + n_gain + n_more:-1]

            @pl.when(first)
            def _():
                for dg_ref in dg_refs:
                    dg_ref[...] = jnp.zeros_like(dg_ref)

            def rows(c, carry):
                sl = pl.ds(pl.multiple_of(c * NORM_ROWS, NORM_ROWS), NORM_ROWS)
                xv = x_ref[sl, :]
                r = lax.rsqrt(jnp.mean(xv * xv, axis=-1, keepdims=True) + EPS)
                xh = xv * r
                out = r_ref[sl, :]
                for idx, (g_ref, dg_ref) in enumerate(zip(g_refs, dg_refs)):
                    dyv = acc[sl, :] if idx == 0 else e_refs[idx - 1][sl, :].astype(F32)
                    dg_ref[...] += jnp.sum(dyv * xh, axis=0, keepdims=True)
                    dxh = dyv * g_ref[...]
                    out = out + r * (dxh - xh * jnp.mean(dxh * xh, axis=-1, keepdims=True))
                dx_ref[sl, :] = out
                return carry

            lax.fori_loop(0, tm // NORM_ROWS, rows, 0)

    out_tile = pl.BlockSpec((tm, tko), lambda i, ko, nq: (i, ko))
    in_specs, args = [dy_spec, w_spec], [dy, w]
    out_shape, out_specs = [jax.ShapeDtypeStruct((m, k), out_dtype)], [out_tile]
    sem = ("parallel", "parallel", "arbitrary")
    if norm is not None:
        vec = pl.BlockSpec((1, k), lambda i, ko, nq: (0, 0))
        in_specs += [out_tile, out_tile] + [vec] * n_gain + [out_tile] * n_more
        args += [x, dres] + list(gains) + list(more)
        out_shape = [jax.ShapeDtypeStruct((m, k), F32)] + [jax.ShapeDtypeStruct((1, k), F32)] * n_gain
        out_specs = [out_tile] + [vec] * n_gain
        sem = ("arbitrary", "arbitrary", "arbitrary")
    outs, bufs = _call(
        body, name=name, jobs=jobs, out_shape=out_shape, grid=(m // tm, k // tko, nn),
        in_specs=in_specs, out_specs=out_specs, scratch_shapes=[pltpu.VMEM((tm, tko), F32)], args=args, sem=sem)
    out = outs[0] if norm is None else (outs[0], outs[1:])
    return out if jobs is None else (out, bufs)


def mm_tn(x, dy, j, *, name, tm=1024, jobs=None):
    m, k = x.shape
    parts, m2, n = (1,) + dy.shape if dy.ndim == 2 else dy.shape
    n *= parts
    assert m == m2 and n % j == 0 and m % tm == 0
    ns = n // j
    tn = _tile(ns, 1408)
    nsub = ns // tn
    jb = _shard_group(j // parts, ns, 1536) if nsub == 1 else 1
    tk = _tile(k, 1408)
    nm = m // tm
    n_steps = j * nsub // jb
    per_part = n_steps // parts
    if dy.ndim == 2:
        dy_spec = pl.BlockSpec((tm, jb * tn), lambda kq, nq, mi: (mi, nq))
    else:
        dy_spec = pl.BlockSpec((None, tm, jb * tn), lambda kq, nq, mi: (nq // per_part, mi, nq % per_part))
    if jb == 1:
        out_spec = pl.BlockSpec((None, tk, tn), lambda kq, nq, mi: (nq // nsub, kq, nq % nsub))
        acc_shape = (tk, tn)
    else:
        out_spec = pl.BlockSpec((jb, tk, ns), lambda kq, nq, mi: (nq, kq, 0))
        acc_shape = (jb, tk, ns)

    def body(x_ref, dy_ref, o_ref, acc):
        mi = pl.program_id(2)

        @pl.when(mi == 0)
        def _():
            acc[...] = jnp.zeros_like(acc)

        xb = x_ref[...].astype(BF16)
        if jb == 1:
            acc[...] += lax.dot_general(xb, dy_ref[...].astype(BF16), (((0,), (0,)), ((), ())),
                                        preferred_element_type=F32)
        else:
            for jj in range(jb):
                acc[jj] += lax.dot_general(xb, dy_ref[:, jj * ns:(jj + 1) * ns].astype(BF16),
                                           (((0,), (0,)), ((), ())), preferred_element_type=F32)

        @pl.when(mi == nm - 1)
        def _():
            o_ref[...] = acc[...].astype(BF16)

    (out,), bufs = _call(
        body, name=name, jobs=jobs,
        out_shape=[jax.ShapeDtypeStruct((j, k, ns), BF16)],
        grid=(k // tk, n_steps, nm),
        in_specs=[
            pl.BlockSpec((tm, tk), lambda kq, nq, mi: (mi, kq)),
            dy_spec,
        ],
        out_specs=[out_spec],
        scratch_shapes=[pltpu.VMEM(acc_shape, F32)],
        args=[x, dy], sem=("parallel", "parallel", "arbitrary"))
    return out if jobs is None else (out, bufs)


def to_streams(x, *, name):
    s, c = x.shape
    per = s // STREAMS

    def body(x_hbm, o_hbm, sems):
        copies = [pltpu.make_async_copy(x_hbm.at[:, st, :], o_hbm.at[st], sems.at[st]) for st in range(STREAMS)]
        for cp in copies:
            cp.start()
        for cp in copies:
            cp.wait()

    out = pl.pallas_call(body, name=name, out_shape=jax.ShapeDtypeStruct((STREAMS, per, c), x.dtype),
                         in_specs=[ANY], out_specs=ANY, scratch_shapes=[pltpu.SemaphoreType.DMA((STREAMS,))],
                         )(x.reshape(per, STREAMS, c))
    return out.reshape(s, c)


def from_streams(x, *, name):
    s, c = x.shape
    per = s // STREAMS

    def body(x_hbm, o_hbm, sems):
        copies = [pltpu.make_async_copy(x_hbm.at[st], o_hbm.at[:, st, :], sems.at[st]) for st in range(STREAMS)]
        for cp in copies:
            cp.start()
        for cp in copies:
            cp.wait()

    out = pl.pallas_call(body, name=name, out_shape=jax.ShapeDtypeStruct((per, STREAMS, c), x.dtype),
                         in_specs=[ANY], out_specs=ANY, scratch_shapes=[pltpu.SemaphoreType.DMA((STREAMS,))],
                         )(x.reshape(STREAMS, per, c))
    return out.reshape(s, c)


def rms_fwd(x, gains, *, name, ts=512):
    s, d = x.shape
    n = len(gains)

    def body(x_ref, *refs):
        xv = x_ref[...]
        xh = xv * lax.rsqrt(jnp.mean(xv * xv, axis=-1, keepdims=True) + EPS)
        for g_ref, o_ref in zip(refs[:n], refs[n:]):
            o_ref[...] = (xh * g_ref[...]).astype(BF16)

    row = pl.BlockSpec((ts, d), lambda i: (i, 0))
    vec = pl.BlockSpec((1, d), lambda i: (0, 0))
    return pl.pallas_call(
        body,
        name=name,
        out_shape=[jax.ShapeDtypeStruct((s, d), BF16)] * n,
        grid=(s // ts,),
        in_specs=[row] + [vec] * n,
        out_specs=[row] * n,
        compiler_params=_params("parallel"),
    )(x, *gains)


def loss_head(h, gain, target, *, ts=256):
    s, d = h.shape

    def body(h_ref, g_ref, t_ref, l_ref, dh_ref, dg_ref):
        i = pl.program_id(0)

        @pl.when(i == 0)
        def _():
            l_ref[...] = jnp.zeros_like(l_ref)
            dg_ref[...] = jnp.zeros_like(dg_ref)

        xv = h_ref[...]
        r = lax.rsqrt(jnp.mean(xv * xv, axis=-1, keepdims=True) + EPS)
        xh = xv * r
        g = g_ref[...]
        err = xh * g - t_ref[...]
        l_ref[...] += 0.5 * jnp.sum(jnp.mean(err * err, axis=-1, keepdims=True))
        dy = err * (1.0 / d)
        dg_ref[...] += jnp.sum(dy * xh, axis=0, keepdims=True)
        dxh = dy * g
        dh_ref[...] = r * (dxh - xh * jnp.mean(dxh * xh, axis=-1, keepdims=True))

    row = pl.BlockSpec((ts, d), lambda i: (i, 0))
    vec = pl.BlockSpec((1, d), lambda i: (0, 0))
    return pl.pallas_call(
        body,
        name="loss_head",
        out_shape=[jax.ShapeDtypeStruct((8, LANE), F32), jax.ShapeDtypeStruct((s, d), F32),
                   jax.ShapeDtypeStruct((1, d), F32)],
        grid=(s // ts,),
        in_specs=[row, vec, row],
        out_specs=[pl.BlockSpec((8, LANE), lambda i: (0, 0)), row, vec],
        compiler_params=_params("arbitrary"),
    )(h, gain, target)


A_BLOCK = (2 * GLA_KEY_DIM + 2 * GLA_VAL_DIM) // LANE


def gate_fwd(proj, w_a2p, b_a2, *, ts=512):
    s = proj.shape[0]

    def body(a_ref, w_ref, b_ref, o_ref):
        z = jnp.dot(a_ref[...].astype(BF16), w_ref[...], preferred_element_type=F32) + b_ref[...]
        o_ref[...] = (jnp.minimum(z, 0.0) - jnp.log(1.0 + jnp.exp(-jnp.abs(z)))) * (1.0 / GATE_NORMALIZER)

    return pl.pallas_call(
        body,
        name="gate_fwd",
        out_shape=jax.ShapeDtypeStruct((s, GLA_KEY_DIM), F32),
        grid=(s // ts,),
        in_specs=[pl.BlockSpec((ts, LANE), lambda i: (i, A_BLOCK)),
                  pl.BlockSpec((LANE, GLA_KEY_DIM), lambda i: (0, 0)),
                  pl.BlockSpec((1, GLA_KEY_DIM), lambda i: (0, 0))],
        out_specs=pl.BlockSpec((ts, GLA_KEY_DIM), lambda i: (i, 0)),
        compiler_params=_params("parallel"),
    )(proj, w_a2p, b_a2)


def gate_bwd(proj, w_a2p, b_a2, dla, *, ts=512):
    s = proj.shape[0]

    def body(a_ref, w_ref, b_ref, dla_ref, da_ref, dw_ref, db_ref):
        i = pl.program_id(0)

        @pl.when(i == 0)
        def _():
            dw_ref[...] = jnp.zeros_like(dw_ref)
            db_ref[...] = jnp.zeros_like(db_ref)

        a = a_ref[...].astype(BF16)
        w = w_ref[...]
        z = jnp.dot(a, w, preferred_element_type=F32) + b_ref[...]
        dz = dla_ref[...] * (1.0 / GATE_NORMALIZER) / (1.0 + jnp.exp(z))
        dzb = dz.astype(BF16)
        da_ref[...] = lax.dot_general(dzb, w, (((1,), (1,)), ((), ())), preferred_element_type=F32).astype(BF16)
        dw_ref[...] += lax.dot_general(a, dzb, (((0,), (0,)), ((), ())), preferred_element_type=F32)
        db_ref[...] += jnp.sum(dz, axis=0, keepdims=True)

    return pl.pallas_call(
        body,
        name="gate_bwd",
        out_shape=[jax.ShapeDtypeStruct((s, LANE), BF16), jax.ShapeDtypeStruct((LANE, GLA_KEY_DIM), F32),
                   jax.ShapeDtypeStruct((1, GLA_KEY_DIM), F32)],
        grid=(s // ts,),
        in_specs=[pl.BlockSpec((ts, LANE), lambda i: (i, A_BLOCK)),
                  pl.BlockSpec((LANE, GLA_KEY_DIM), lambda i: (0, 0)),
                  pl.BlockSpec((1, GLA_KEY_DIM), lambda i: (0, 0)),
                  pl.BlockSpec((ts, GLA_KEY_DIM), lambda i: (i, 0))],
        out_specs=[pl.BlockSpec((ts, LANE), lambda i: (i, 0)),
                   pl.BlockSpec((LANE, GLA_KEY_DIM), lambda i: (0, 0)),
                   pl.BlockSpec((1, GLA_KEY_DIM), lambda i: (0, 0))],
        compiler_params=_params("arbitrary"),
    )(proj, w_a2p, b_a2, dla)


def _chunk_terms(q, k, la):
    c_len = GLA_CHUNK
    row = lax.broadcasted_iota(jnp.int32, (c_len, c_len), 0)
    col = lax.broadcasted_iota(jnp.int32, (c_len, c_len), 1)
    tri = row >= col
    c = jnp.dot(tri.astype(F32), la, preferred_element_type=F32, precision=lax.Precision.HIGHEST)
    last = jnp.sum(la, axis=0, keepdims=True)
    q_dec = q * (GLA_DK ** -0.5) * jnp.exp(c)
    k_inv = k * jnp.exp(-c)
    k_end = k * jnp.exp(last - c)
    return c, last, q_dec, k_inv, k_end, tri


def _dot(a, b, ca, cb):
    return lax.dot_general(a.astype(BF16), b.astype(BF16), (((ca,), (cb,)), ((), ())), preferred_element_type=F32)


def gla_fwd(proj, la, jobs=None):
    s = proj.shape[0]
    n_chunks = s // GLA_CHUNK
    rows = GLA_CHUNK * GLA_STEP_CHUNKS

    def body(q_ref, k_ref, v_ref, la_ref, o_ref, st_out, st):
        @pl.when(pl.program_id(0) == 0)
        def _():
            st[...] = jnp.zeros_like(st)

        for h in range(GLA_HEADS):
            hk = slice(h * GLA_DK, (h + 1) * GLA_DK)
            hv = slice(h * GLA_DV, (h + 1) * GLA_DV)
            for cc in range(GLA_STEP_CHUNKS):
                rs = slice(cc * GLA_CHUNK, (cc + 1) * GLA_CHUNK)
                _, last, q_dec, k_inv, k_end, tri = _chunk_terms(q_ref[rs, hk], k_ref[rs, hk], la_ref[rs, hk])
                v = v_ref[rs, hv]
                a = jnp.where(tri, _dot(q_dec, k_inv, 1, 1), 0.0)
                state = st[h]
                st_out[h, cc] = state
                o_ref[rs, hv] = _dot(a, v, 1, 0) + _dot(q_dec, state, 1, 1)
                st[h] = state * jnp.exp(last) + _dot(v, k_end, 0, 0)

    key = lambda col: pl.BlockSpec((rows, GLA_KEY_DIM), lambda n: (n, col))
    outs, bufs = _call(
        body, name="gla_fwd", jobs=jobs,
        out_shape=[jax.ShapeDtypeStruct((s, GLA_VAL_DIM), F32),
                   jax.ShapeDtypeStruct((GLA_HEADS, n_chunks, GLA_DV, GLA_DK), F32)],
        grid=(n_chunks // GLA_STEP_CHUNKS,),
        in_specs=[key(0), key(1), pl.BlockSpec((rows, GLA_VAL_DIM), lambda n: (n, 1)), key(0)],
        out_specs=[pl.BlockSpec((rows, GLA_VAL_DIM), lambda n: (n, 0)),
                   pl.BlockSpec((GLA_HEADS, GLA_STEP_CHUNKS, GLA_DV, GLA_DK), lambda n: (0, n, 0, 0))],
        scratch_shapes=[pltpu.VMEM((GLA_HEADS, GLA_DV, GLA_DK), F32)],
        args=[proj, proj, proj, la], sem=("arbitrary",))
    return outs if jobs is None else (outs, bufs)


def gla_bwd(proj, la, states, do, jobs=None):
    s = proj.shape[0]
    n_steps = s // GLA_CHUNK // GLA_STEP_CHUNKS
    lastc = n_steps - 1
    rows = GLA_CHUNK * GLA_STEP_CHUNKS

    def body(q_ref, k_ref, v_ref, la_ref, do_ref, st_ref, dq_ref, dk_ref, dv_ref, dla_ref, dst):
        @pl.when(pl.program_id(0) == 0)
        def _():
            dst[...] = jnp.zeros_like(dst)

        upper = (lax.broadcasted_iota(jnp.int32, (GLA_CHUNK, GLA_CHUNK), 0)
                 <= lax.broadcasted_iota(jnp.int32, (GLA_CHUNK, GLA_CHUNK), 1)).astype(F32)
        for h in range(GLA_HEADS):
            hk = slice(h * GLA_DK, (h + 1) * GLA_DK)
            hv = slice(h * GLA_DV, (h + 1) * GLA_DV)
            for cc in reversed(range(GLA_STEP_CHUNKS)):
                rs = slice(cc * GLA_CHUNK, (cc + 1) * GLA_CHUNK)
                c, last, q_dec, k_inv, k_end, tri = _chunk_terms(q_ref[rs, hk], k_ref[rs, hk], la_ref[rs, hk])
                v = v_ref[rs, hv]
                dout = do_ref[rs, hv]
                state = st_ref[h, cc]
                dstate = dst[h]
                e_last = jnp.exp(last)
                a = jnp.where(tri, _dot(q_dec, k_inv, 1, 1), 0.0)
                da = jnp.where(tri, _dot(dout, v, 1, 1), 0.0)
                dv_ref[rs, hv] = (_dot(a, dout, 0, 0) + _dot(k_end, dstate, 1, 1)).astype(BF16)
                dq_dec = _dot(da, k_inv, 1, 0) + _dot(dout, state, 1, 0)
                dk_inv = _dot(da, q_dec, 0, 0)
                dk_end = _dot(v, dstate, 1, 0)
                dst[h] = dstate * e_last + _dot(dout, q_dec, 0, 0)
                dq_ref[rs, hk] = (dq_dec * (GLA_DK ** -0.5) * jnp.exp(c)).astype(BF16)
                dk_ref[rs, hk] = (dk_inv * jnp.exp(-c) + dk_end * jnp.exp(last - c)).astype(BF16)
                ke_term = dk_end * k_end
                dc = dq_dec * q_dec - dk_inv * k_inv - ke_term
                dlast = (jnp.sum(ke_term, axis=0, keepdims=True)
                         + e_last * jnp.sum(dstate * state, axis=0, keepdims=True))
                dla_ref[rs, hk] = jnp.dot(upper, dc, preferred_element_type=F32,
                                          precision=lax.Precision.HIGHEST) + dlast

    key = lambda col: pl.BlockSpec((rows, GLA_KEY_DIM), lambda n: (lastc - n, col))
    val = lambda col: pl.BlockSpec((rows, GLA_VAL_DIM), lambda n: (lastc - n, col))
    outs, bufs = _call(
        body, name="gla_bwd", jobs=jobs,
        out_shape=[jax.ShapeDtypeStruct((s, GLA_KEY_DIM), BF16), jax.ShapeDtypeStruct((s, GLA_KEY_DIM), BF16),
                   jax.ShapeDtypeStruct((s, GLA_VAL_DIM), BF16), jax.ShapeDtypeStruct((s, GLA_KEY_DIM), F32)],
        grid=(n_steps,),
        in_specs=[key(0), key(1), val(1), key(0), val(0),
                  pl.BlockSpec((GLA_HEADS, GLA_STEP_CHUNKS, GLA_DV, GLA_DK), lambda n: (0, lastc - n, 0, 0))],
        out_specs=[key(0), key(0), val(0), key(0)],
        scratch_shapes=[pltpu.VMEM((GLA_HEADS, GLA_DV, GLA_DK), F32)],
        args=[proj, proj, proj, la, do, states], sem=("arbitrary",))
    return outs if jobs is None else (outs, bufs)


R_BLOCK = (2 * GLA_KEY_DIM + GLA_VAL_DIM) // GLA_DV


def headnorm_fwd(o, proj, hn, *, ts=512):
    s = o.shape[0]

    def body(o_ref, r_ref, g_ref, out_ref):
        ov = o_ref[...]
        oh = ov * lax.rsqrt(jnp.mean(ov * ov, axis=-1, keepdims=True) + EPS)
        r = r_ref[...]
        out_ref[...] = (oh * g_ref[...] * (r * jax.nn.sigmoid(r))).astype(BF16)

    return pl.pallas_call(
        body,
        name="headnorm_fwd",
        out_shape=jax.ShapeDtypeStruct((s, GLA_VAL_DIM), BF16),
        grid=(s // ts, GLA_HEADS),
        in_specs=[pl.BlockSpec((ts, GLA_DV), lambda i, h: (i, h)),
                  pl.BlockSpec((ts, GLA_DV), lambda i, h: (i, R_BLOCK + h)),
                  pl.BlockSpec((1, GLA_DV), lambda i, h: (0, 0))],
        out_specs=pl.BlockSpec((ts, GLA_DV), lambda i, h: (i, h)),
        compiler_params=_params("parallel", "parallel"),
    )(o, proj, hn)


def headnorm_bwd(o, proj, hn, dog, *, ts=512):
    s = o.shape[0]

    def body(o_ref, r_ref, g_ref, dog_ref, do_ref, dr_ref, dg_ref):
        @pl.when((pl.program_id(0) == 0) & (pl.program_id(1) == 0))
        def _():
            dg_ref[...] = jnp.zeros_like(dg_ref)

        ov = o_ref[...]
        rr = lax.rsqrt(jnp.mean(ov * ov, axis=-1, keepdims=True) + EPS)
        oh = ov * rr
        g = g_ref[...]
        r = r_ref[...]
        sig = jax.nn.sigmoid(r)
        gate = r * sig
        dog_v = dog_ref[...]
        d_on = dog_v * gate
        dr_ref[...] = (dog_v * (oh * g) * (sig * (1.0 + r * (1.0 - sig)))).astype(BF16)
        dg_ref[...] += jnp.sum(d_on * oh, axis=0, keepdims=True)
        doh = d_on * g
        do_ref[...] = rr * (doh - oh * jnp.mean(doh * oh, axis=-1, keepdims=True))

    return pl.pallas_call(
        body,
        name="headnorm_bwd",
        out_shape=[jax.ShapeDtypeStruct((s, GLA_VAL_DIM), F32), jax.ShapeDtypeStruct((s, GLA_VAL_DIM), BF16),
                   jax.ShapeDtypeStruct((1, GLA_DV), F32)],
        grid=(s // ts, GLA_HEADS),
        in_specs=[pl.BlockSpec((ts, GLA_DV), lambda i, h: (i, h)),
                  pl.BlockSpec((ts, GLA_DV), lambda i, h: (i, R_BLOCK + h)),
                  pl.BlockSpec((1, GLA_DV), lambda i, h: (0, 0)),
                  pl.BlockSpec((ts, GLA_DV), lambda i, h: (i, h))],
        out_specs=[pl.BlockSpec((ts, GLA_DV), lambda i, h: (i, h)),
                   pl.BlockSpec((ts, GLA_DV), lambda i, h: (i, h)),
                   pl.BlockSpec((1, GLA_DV), lambda i, h: (0, 0))],
        compiler_params=_params("arbitrary", "arbitrary"),
    )(o, proj, hn, dog)


CONV_TC = 128
SQRT_HALF = 0.7071067811865476
INV_SQRT_2PI = 0.3989422804014327


def _conv_gate(g_ref, cw_ref, cb_ref):
    g0 = g_ref[...].astype(F32)
    t = lax.broadcasted_iota(jnp.int32, g0.shape, 0)
    g1 = jnp.where(t >= 1, pltpu.roll(g0, 1, 0), 0.0)
    g2 = jnp.where(t >= 2, pltpu.roll(g0, 2, 0), 0.0)
    gc = cw_ref[0:1, :] * g2 + cw_ref[1:2, :] * g1 + cw_ref[2:3, :] * g0 + cb_ref[...]
    return g0, g1, g2, gc, t


def convglu_fwd(up, conv_w, conv_b, *, name, jobs=None):
    s = up.shape[0]
    nc = D_FF // CONV_TC

    def body(u_ref, g_ref, cw_ref, cb_ref, o_ref):
        _, _, _, gc, _ = _conv_gate(g_ref, cw_ref, cb_ref)
        gelu = 0.5 * gc * (1.0 + lax.erf(gc * SQRT_HALF))
        o_ref[...] = (gelu * u_ref[...].astype(F32)).astype(BF16)

    (out,), bufs = _call(
        body, name=name, jobs=jobs,
        out_shape=[jax.ShapeDtypeStruct((s, D_FF), BF16)],
        grid=(nc,),
        in_specs=[pl.BlockSpec((s, CONV_TC), lambda c: (0, c)),
                  pl.BlockSpec((s, CONV_TC), lambda c: (0, nc + c)),
                  pl.BlockSpec((3, CONV_TC), lambda c: (0, c)),
                  pl.BlockSpec((1, CONV_TC), lambda c: (0, c))],
        out_specs=[pl.BlockSpec((s, CONV_TC), lambda c: (0, c))],
        args=[up, up, conv_w, conv_b], sem=("parallel",))
    return out if jobs is None else (out, bufs)


def convglu_bwd(up, conv_w, conv_b, dact, *, name, jobs=None):
    s = up.shape[0]
    nc = D_FF // CONV_TC

    def body(u_ref, g_ref, cw_ref, cb_ref, da_ref, dup_ref, dcw_ref, dcb_ref):
        du_ref, dg_ref = dup_ref.at[0], dup_ref.at[1]
        g0, g1, g2, gc, t = _conv_gate(g_ref, cw_ref, cb_ref)
        cdf = 0.5 * (1.0 + lax.erf(gc * SQRT_HALF))
        da = da_ref[...].astype(F32)
        du_ref[...] = (da * gc * cdf).astype(BF16)
        dgc = da * u_ref[...].astype(F32) * (cdf + gc * jnp.exp(-0.5 * gc * gc) * INV_SQRT_2PI)
        dcb_ref[...] = jnp.sum(dgc, axis=0, keepdims=True)
        dcw_ref[0:1, :] = jnp.sum(dgc * g2, axis=0, keepdims=True)
        dcw_ref[1:2, :] = jnp.sum(dgc * g1, axis=0, keepdims=True)
        dcw_ref[2:3, :] = jnp.sum(dgc * g0, axis=0, keepdims=True)
        n1 = jnp.where(t < s - 1, pltpu.roll(dgc, s - 1, 0), 0.0)
        n2 = jnp.where(t < s - 2, pltpu.roll(dgc, s - 2, 0), 0.0)
        dg_ref[...] = (cw_ref[2:3, :] * dgc + cw_ref[1:2, :] * n1 + cw_ref[0:1, :] * n2).astype(BF16)

    col = pl.BlockSpec((s, CONV_TC), lambda c: (0, c))
    outs, bufs = _call(
        body, name=name, jobs=jobs,
        out_shape=[jax.ShapeDtypeStruct((2, s, D_FF), BF16),
                   jax.ShapeDtypeStruct((3, D_FF), F32), jax.ShapeDtypeStruct((1, D_FF), F32)],
        grid=(nc,),
        in_specs=[col, pl.BlockSpec((s, CONV_TC), lambda c: (0, nc + c)),
                  pl.BlockSpec((3, CONV_TC), lambda c: (0, c)),
                  pl.BlockSpec((1, CONV_TC), lambda c: (0, c)), col],
        out_specs=[pl.BlockSpec((2, s, CONV_TC), lambda c: (0, 0, c)), pl.BlockSpec((3, CONV_TC), lambda c: (0, c)),
                   pl.BlockSpec((1, CONV_TC), lambda c: (0, c))],
        args=[up, up, conv_w, conv_b, dact], sem=("parallel",))
    return outs if jobs is None else (outs, bufs)


SLOPE_TILE = (8, LANE)


def _slope_table():
    return jnp.broadcast_to(jnp.asarray(ALIBI_SLOPES, F32)[:, None, None], (ATT_HEADS,) + SLOPE_TILE)


def _pieces(s_len, d):
    npc = STREAMS // d
    lp = ATT_BLOCK // npc
    return npc, lp, (s_len // STREAMS) // lp


def _gather(ref, r, b, d, s_len):
    npc, lp, _ = _pieces(s_len, d)
    per = s_len // STREAMS
    parts = [ref[pl.ds((r + d * k) * per + b * lp, lp), :] for k in range(npc)]
    return parts[0] if npc == 1 else jnp.concatenate(parts, axis=0)


def _scatter(ref, r, b, d, s_len, val, add=False):
    npc, lp, _ = _pieces(s_len, d)
    per = s_len // STREAMS
    for k in range(npc):
        rows = pl.ds((r + d * k) * per + b * lp, lp)
        piece = val[k * lp:(k + 1) * lp]
        if add:
            ref[rows, :] += piece
        else:
            ref[rows, :] = piece


def _stream_bias(slope, d, s_len):
    npc, lp, _ = _pieces(s_len, d)
    qi = lax.broadcasted_iota(jnp.int32, (ATT_BLOCK, 2 * ATT_BLOCK), 0)
    c = lax.broadcasted_iota(jnp.int32, (ATT_BLOCK, 2 * ATT_BLOCK), 1)
    own = c // ATT_BLOCK
    cc = c - own * ATT_BLOCK
    dist = npc * ((qi % lp) - (cc % lp) + lp * (1 - own)) + (qi // lp - cc // lp)
    ok = (dist >= 0) & (dist <= ATT_BLOCK)
    return jnp.where(ok, (slope * (-float(d))) * dist.astype(F32), NEG)


def attn_fwd(q, kv, jobs=None):
    s_len = q.shape[0]
    scale = HEAD_DIM ** -0.5

    def body(sl_ref, q_ref, k_ref, v_ref, o_ref, lse_ref):
        g = pl.program_id(1)
        slope = sl_ref[0:1, 0:1]

        def branch(gi, d):
            _, _, nblk = _pieces(s_len, d)
            bias = _stream_bias(slope, d, s_len)
            for r in range(d):
                for b in range(nblk):
                    qb = _gather(q_ref, r, b, d, s_len)
                    kc, vc = _gather(k_ref, r, b, d, s_len), _gather(v_ref, r, b, d, s_len)
                    if b == 0:
                        kcat, vcat, bb = kc, vc, bias[:, ATT_BLOCK:]
                    else:
                        kcat = jnp.concatenate([_gather(k_ref, r, b - 1, d, s_len), kc], axis=0)
                        vcat = jnp.concatenate([_gather(v_ref, r, b - 1, d, s_len), vc], axis=0)
                        bb = bias
                    sc = _dot(qb, kcat, 1, 1) * scale + bb
                    m = jnp.max(sc, axis=-1, keepdims=True)
                    p = jnp.exp(sc - m)
                    l = jnp.sum(p, axis=-1, keepdims=True)
                    o_new = _dot(p, vcat, 1, 0) / l
                    lse_new = m + jnp.log(l)
                    if gi > 0:
                        lse_old = _gather(lse_ref, r, b, d, s_len)[:, 0:1]
                        top = jnp.maximum(lse_old, lse_new)
                        e_old, e_new = jnp.exp(lse_old - top), jnp.exp(lse_new - top)
                        den = e_old + e_new
                        o_new = (e_old * _gather(o_ref, r, b, d, s_len) + e_new * o_new) / den
                        lse_new = top + jnp.log(den)
                    _scatter(o_ref, r, b, d, s_len, o_new)
                    _scatter(lse_ref, r, b, d, s_len, jnp.broadcast_to(lse_new, (ATT_BLOCK, HEAD_DIM)))

        for gi, d in enumerate(DILATIONS):
            @pl.when(g == gi)
            def _():
                branch(gi, d)

    blk = lambda col: pl.BlockSpec((s_len, HEAD_DIM), lambda h, g: (0, col(h, g)))
    head = lambda h, g: h
    outs, bufs = _call(
        body, name="attn_fwd", jobs=jobs,
        out_shape=[jax.ShapeDtypeStruct((s_len, ATT_HEADS * HEAD_DIM), F32)] * 2,
        grid=(ATT_HEADS, len(DILATIONS)),
        in_specs=[pl.BlockSpec((None,) + SLOPE_TILE, lambda h, g: (h, 0, 0)),
                  blk(lambda h, g: g * ATT_HEADS + h), blk(head), blk(lambda h, g: ATT_HEADS + h)],
        out_specs=[blk(head), blk(head)],
        args=[_slope_table(), q, kv, kv], sem=("parallel", "arbitrary"))
    return outs if jobs is None else (outs, bufs)


def attn_bwd(q, kv, o, lse, do, jobs=None):
    s_len = q.shape[0]
    scale = HEAD_DIM ** -0.5
    chunks = s_len // ATT_BLOCK

    def body(sl_ref, q_ref, k_ref, v_ref, o_ref, lse_ref, do_ref, dq_ref, dkv_ref, dlt):
        g = pl.program_id(1)
        slope = sl_ref[0:1, 0:1]
        dk_ref, dv_ref = dkv_ref.at[0], dkv_ref.at[1]

        @pl.when(g == 0)
        def _():
            dkv_ref[...] = jnp.zeros_like(dkv_ref)

            def deltas(c, carry):
                rows = pl.ds(pl.multiple_of(c * ATT_BLOCK, ATT_BLOCK), ATT_BLOCK)
                dlt[rows, :] = jnp.sum(do_ref[rows, :] * o_ref[rows, :], axis=-1, keepdims=True)
                return carry
            lax.fori_loop(0, chunks, deltas, 0)

        def branch(d):
            _, _, nblk = _pieces(s_len, d)
            bias = _stream_bias(slope, d, s_len)
            for r in range(d):
                for b in range(nblk):
                    qb = _gather(q_ref, r, b, d, s_len)
                    dob = _gather(do_ref, r, b, d, s_len)
                    kc, vc = _gather(k_ref, r, b, d, s_len), _gather(v_ref, r, b, d, s_len)
                    if b == 0:
                        kcat, vcat, bb = kc, vc, bias[:, ATT_BLOCK:]
                    else:
                        kcat = jnp.concatenate([_gather(k_ref, r, b - 1, d, s_len), kc], axis=0)
                        vcat = jnp.concatenate([_gather(v_ref, r, b - 1, d, s_len), vc], axis=0)
                        bb = bias
                    sc = _dot(qb, kcat, 1, 1) * scale + bb
                    p = jnp.exp(sc - _gather(lse_ref, r, b, d, s_len)[:, 0:1])
                    ds = p * (_dot(dob, vcat, 1, 1) - _gather(dlt, r, b, d, s_len))
                    _scatter(dq_ref, r, b, d, s_len, _dot(ds, kcat, 1, 0) * scale)
                    dk = _dot(ds, qb, 0, 0) * scale
                    dv = _dot(p, dob, 0, 0)
                    if b == 0:
                        _scatter(dk_ref, r, b, d, s_len, dk, add=True)
                        _scatter(dv_ref, r, b, d, s_len, dv, add=True)
                    else:
                        _scatter(dk_ref, r, b - 1, d, s_len, dk[:ATT_BLOCK], add=True)
                        _scatter(dv_ref, r, b - 1, d, s_len, dv[:ATT_BLOCK], add=True)
                        _scatter(dk_ref, r, b, d, s_len, dk[ATT_BLOCK:], add=True)
                        _scatter(dv_ref, r, b, d, s_len, dv[ATT_BLOCK:], add=True)

        for gi, d in enumerate(DILATIONS):
            @pl.when(g == gi)
            def _():
                branch(d)

    blk = lambda col: pl.BlockSpec((s_len, HEAD_DIM), lambda h, g: (0, col(h, g)))
    head = lambda h, g: h
    q_col = lambda h, g: g * ATT_HEADS + h
    outs, bufs = _call(
        body, name="attn_bwd", jobs=jobs,
        out_shape=[jax.ShapeDtypeStruct(q.shape, F32), jax.ShapeDtypeStruct((2, s_len, ATT_HEADS * HEAD_DIM), F32)],
        grid=(ATT_HEADS, len(DILATIONS)),
        in_specs=[pl.BlockSpec((None,) + SLOPE_TILE, lambda h, g: (h, 0, 0)),
                  blk(q_col), blk(head), blk(lambda h, g: ATT_HEADS + h), blk(head), blk(head), blk(head)],
        out_specs=[blk(q_col), pl.BlockSpec((2, s_len, HEAD_DIM), lambda h, g: (0, 0, h))],
        scratch_shapes=[pltpu.VMEM((s_len, 1), F32)],
        args=[_slope_table(), q, kv, kv, o, lse, do], sem=("parallel", "arbitrary"))
    return outs if jobs is None else (outs, bufs)


def _adam(w, g, m, v):
    m = ADAM_B1 * m + (1.0 - ADAM_B1) * g
    v = ADAM_B2 * v + (1.0 - ADAM_B2) * (g * g)
    m_hat = m / (1.0 - ADAM_B1 ** ADAM_STEP)
    v_hat = v / (1.0 - ADAM_B2 ** ADAM_STEP)
    delta = -ADAM_LR * (m_hat / (jnp.sqrt(v_hat) + ADAM_EPS) + ADAM_WD * w)
    return delta, m, v


def adam_sharded(recvs, w, m, v, *, name):
    layers = len(recvs)
    n_src, r, c = recvs[0].shape
    tr = _rows(r, c)

    def body(*refs):
        p_refs = refs[:layers]
        w_ref, m_ref, v_ref, g_ref, d_ref, mo_ref, vo_ref = refs[layers:]
        for layer, p_ref in enumerate(p_refs):
            @pl.when(pl.program_id(0) == layer)
            def _():
                g = p_ref[0].astype(F32)
                for src in range(1, n_src):
                    g = g + p_ref[src].astype(F32)
                delta, m_new, v_new = _adam(w_ref[...], g, m_ref[...], v_ref[...])
                g_ref[...] = g
                d_ref[...] = delta
                mo_ref[...] = m_new
                vo_ref[...] = v_new

    blk = pl.BlockSpec((None, tr, c), lambda l, i: (l, i, 0))
    out = jax.ShapeDtypeStruct((layers, r, c), F32)
    part = [pl.BlockSpec((n_src, tr, c), functools.partial(lambda l, i, layer: (0, jnp.where(l == layer, i, 0), 0),
                                                            layer=layer)) for layer in range(layers)]
    return pl.pallas_call(
        body,
        name=name,
        out_shape=[out] * 4,
        grid=(layers, r // tr),
        in_specs=part + [blk, blk, blk],
        out_specs=[blk] * 4,
        compiler_params=_params("parallel", "parallel"),
    )(*recvs, w, m, v)


def sum_partials(parts):
    n_src, r, c = parts.shape

    def body(p_ref, o_ref):
        g = p_ref[0]
        for src in range(1, n_src):
            g = g + p_ref[src]
        o_ref[...] = g

    return pl.pallas_call(
        body,
        name="sum_small_grads",
        out_shape=jax.ShapeDtypeStruct((r, c), F32),
    )(parts)


def adam_packed(w, g, m, v):
    def body(w_ref, g_ref, m_ref, v_ref, d_ref, mo_ref, vo_ref):
        delta, m_new, v_new = _adam(w_ref[...], g_ref[...], m_ref[...], v_ref[...])
        d_ref[...] = delta
        mo_ref[...] = m_new
        vo_ref[...] = v_new

    out = jax.ShapeDtypeStruct(w.shape, F32)
    return pl.pallas_call(body, name="adam_small", out_shape=[out] * 3)(w, g, m, v)


def all_gather(srcs, *, name):
    n = len(srcs)

    def body(*refs):
        src, dst = refs[:n], refs[n:2 * n]
        send_sems, recv_sems, local_sems = refs[2 * n:]
        x, y, c, me = _place()
        sibling = (x, y, 1 - c)
        chips = [(1 - x, y), (x, 1 - y), (1 - x, 1 - y)]

        def index(px, py, pc):
            return 4 * px + 2 * py + pc

        def copy(p, k, block, to, from_src=False):
            slot = dst[p].at[index(*block)]
            return pltpu.make_async_remote_copy(
                src_ref=src[p] if from_src else slot, dst_ref=slot,
                send_sem=send_sems.at[p, k], recv_sem=recv_sems.at[p, k],
                device_id=to, device_id_type=MESH)

        mine = [pltpu.make_async_copy(src[p], dst[p].at[me], local_sems.at[p]) for p in range(n)]
        for cp in mine:
            cp.start()
        first = []
        for p in range(n):
            first.append(copy(p, 0, (x, y, c), sibling, from_src=True))
            for jj, chip in enumerate(chips):
                first.append(copy(p, 1 + jj, (x, y, c), (*chip, c), from_src=True))
        for cp in first:
            cp.start()
        passed = []
        for jj, chip in enumerate(chips):
            for p in range(n):
                copy(p, 1 + jj, (*chip, c), (x, y, c)).wait_recv()
                fwd = copy(p, 4 + jj, (*chip, c), sibling)
                fwd.start()
                passed.append(fwd)
        for p in range(n):
            copy(p, 0, sibling, (x, y, c)).wait_recv()
            for jj, chip in enumerate(chips):
                copy(p, 4 + jj, (*chip, 1 - c), (x, y, c)).wait_recv()
        for cp in first + passed:
            cp.wait_send()
        for cp in mine:
            cp.wait()

    return pl.pallas_call(
        body,
        name=name,
        out_shape=[jax.ShapeDtypeStruct((N_DEV,) + a.shape, a.dtype) for a in srcs],
        in_specs=[ANY] * n,
        out_specs=[ANY] * n,
        scratch_shapes=[pltpu.SemaphoreType.DMA((n, 7)), pltpu.SemaphoreType.DMA((n, 7)),
                        pltpu.SemaphoreType.DMA((n,))],
    )(*srcs)


def exchange_only(*, name, jobs):
    def body(o_ref):
        o_ref[...] = jnp.zeros_like(o_ref)

    _, bufs = _call(body, name=name, jobs=jobs, out_shape=[jax.ShapeDtypeStruct((8, LANE), F32)], grid=(1,),
                    in_specs=[], out_specs=[pl.BlockSpec((8, LANE), lambda i: (0, 0))], args=[], sem=("arbitrary",))
    return None, bufs


def _pack_rows(parts, rows):
    flat = jnp.concatenate([p.reshape(-1) for p in parts])
    return jnp.pad(flat, (0, rows * LANE - flat.shape[0])).reshape(rows, LANE)


def _unpack_rows(packed, shapes):
    flat = packed.reshape(-1)
    out, at = [], 0
    for sh in shapes:
        size = 1
        for dim in sh:
            size *= dim
        out.append(flat[at:at + size].reshape(sh))
        at += size
    return out


CONV_W_PAD = 768
SMALL_W_ROWS = 56


def _pack_small_weights(w_a2, b_a2, hn, conv_w):
    cw = jnp.pad(conv_w.reshape(6, -1), ((0, 0), (0, CONV_W_PAD - conv_w.shape[-1]))).reshape(-1, LANE)
    rows = jnp.concatenate([w_a2[0], b_a2, jnp.pad(hn, ((0, 0), (0, LANE - hn.shape[-1]))), cw], axis=0)
    return jnp.pad(rows, ((0, SMALL_W_ROWS - rows.shape[0]), (0, 0)))


def _unpack_small_weights(gathered):
    w_a2 = gathered[:, 0:GATE_RANK, :].transpose(1, 0, 2).reshape(GATE_RANK, GLA_KEY_DIM)
    b_a2 = gathered[:, GATE_RANK, :].reshape(1, GLA_KEY_DIM)
    hn = gathered[:, GATE_RANK + 1, :GLA_DV // N_DEV].reshape(1, GLA_DV)
    per = D_FF // N_DEV
    cw = gathered[:, GATE_RANK + 2:GATE_RANK + 2 + 6 * CONV_W_PAD // LANE, :].reshape(N_DEV, 6, CONV_W_PAD)[:, :, :per]
    cw = cw.reshape(N_DEV, 2, 3, per).transpose(1, 2, 0, 3).reshape(2, 3, D_FF)
    return w_a2, b_a2, hn, cw


SCHEDULE = {
    "gla_in": [("g1", "gout", None), ("g1", "up0", (0, 1024))],
    "gla_fwd": [("g2", "gout", None), ("g2", "up0", (0, 1024)), ("g1", "up0", (1024, 2048))],
    "gla_out": [("g2", "up0", (1024, 2048)), ("g1", "dn0", (0, 352))],
    "ffn_up0": [("g2", "dn0", (0, 352)), ("g1", "dn0", (352, 704)), ("g1", "kv", None), ("g1", "q", (0, 768))],
    "convglu_fwd0": [("g2", "dn0", (352, 704))],
    "ffn_down0": [("g2", "kv", None), ("g2", "q", (0, 768)), ("g1", "q", (768, 2048)), ("g1", "dout", None)],
    "kv_proj": [("g2", "q", (768, 2048)), ("g2", "dout", None), ("g1", "up1", (0, 704))],
    "q_proj": [("g2", "up1", (0, 704)), ("g1", "up1", (704, 1664))],
    "attn_fwd": [("g2", "up1", (704, 1664)), ("g1", "up1", (1664, 2048)), ("g1", "dn1", None)],
    "dsa_out": [("g2", "up1", (1664, 2048)), ("g2", "dn1", None)],
    "ffn_down_dx1": [("sc", "dn1", (0, 352))],
    "convglu_bwd1": [("sc", "dn1", (352, 704))],
    "ffn_up_dx1": [("sc", "up1", (0, 1024))],
    "attn_bwd": [("sc", "up1", (1024, 2048)), ("sc", "dout", None)],
    "q_proj_dx": [("sc", "q", (0, 1024))],
    "kv_proj_dw": [("sc", "q", (1024, 1792))],
    "kv_proj_dx": [("sc", "q", (1792, 2048)), ("sc", "kv", (0, 768))],
    "ffn_down_dw0": [("sc", "kv", (768, 2048))],
    "ffn_down_dx0": [("sc", "dn0", (0, 384))],
    "convglu_bwd0": [("sc", "dn0", (384, 704))],
    "ffn_up_dx0": [("sc", "up0", (0, 1024))],
    "gla_out_dw": [("sc", "up0", (1024, 1216))],
    "gla_out_dx": [("sc", "up0", (1216, 1408))],
    "gla_bwd": [("sc", "up0", (1408, 2048)), ("sc", "gout", (0, 128))],
    "gla_in_dw": [("sc", "gout", (128, 256))],
    "gla_in_dx": [("sc", "in", (0, 1536))],
    "grads_tail": [("sc", "in", (1536, 2048))],
}
ROW_SHARDED = ("gout", "dout", "dn0", "dn1")


class Plan:
    def __init__(self, weights, srcs=None):
        self.w = dict(weights)
        self.srcs = srcs
        self.grads = {}
        self.recv = {}
        self._names = None

    def weight(self, name):
        buf = self.w[name]
        if name in ROW_SHARDED:
            return buf.reshape(1, buf.shape[0] * buf.shape[1], buf.shape[2])
        return buf

    def jobs(self, call):
        ops = SCHEDULE.get(call)
        if self.srcs is None or not ops:
            return None
        jobs, handles = Jobs(), {}
        for op, name, rows in ops:
            store = self.recv if op == "sc" else self.w
            if name not in handles:
                if name in store:
                    handles[name] = jobs.thru(store[name])
                elif op == "sc":
                    handles[name] = jobs.new(self.grads[name].shape, BF16)
                else:
                    handles[name] = jobs.new((N_DEV,) + self.srcs[name].shape, BF16)
            if op == "g1":
                jobs.gather_ici(self.srcs[name], handles[name], rows)
            elif op == "g2":
                jobs.gather_d2d(handles[name], rows)
            else:
                jobs.scatter(self.grads[name], handles[name], rows)
        self._names = [(name, self.recv if ops[0][0] == "sc" else self.w) for name in handles]
        assert len({op == "sc" for op, _, _ in ops}) == 1
        return jobs

    def run(self, call, fn, *args, **kwargs):
        jobs = self.jobs(call)
        if jobs is None:
            return fn(*args, **kwargs)
        out, bufs = fn(*args, jobs=jobs, **kwargs)
        for (name, store), buf in zip(self._names, bufs):
            store[name] = buf
        return out


def _ffn_fwd(plan, h, norm_g, conv_w, conv_b, tag):
    (n,) = rms_fwd(h, [norm_g], name=f"ffn_norm_fwd{tag}")
    up = plan.run(f"ffn_up{tag}", mm_nn, n, plan.weight(f"up{tag}"), out_dtype=BF16, name=f"ffn_up{tag}")
    act = plan.run(f"convglu_fwd{tag}", convglu_fwd, up, conv_w, conv_b, name=f"convglu_fwd{tag}")
    h_out = plan.run(f"ffn_down{tag}", mm_nn, act, plan.weight(f"dn{tag}"), out_dtype=F32, res=h,
                     name=f"ffn_down{tag}")
    return h_out, (n, up, act)


def _by_rows(dw):
    return dw.reshape(N_DEV, dw.shape[1] // N_DEV, dw.shape[2])


def _ffn_bwd(plan, dh_out, h, saved, norm_g, conv_w, conv_b, tag):
    n, up, act = saved
    plan.grads[f"dn{tag}"] = _by_rows(plan.run(f"ffn_down_dw{tag}", mm_tn, act, dh_out, 1, name=f"ffn_down_dw{tag}"))
    dact = plan.run(f"ffn_down_dx{tag}", mm_nt, dh_out, plan.weight(f"dn{tag}"), out_dtype=BF16,
                    name=f"ffn_down_dx{tag}")
    dup, dconv_w, dconv_b = plan.run(f"convglu_bwd{tag}", convglu_bwd, up, conv_w, conv_b, dact,
                                     name=f"convglu_bwd{tag}")
    plan.grads[f"up{tag}"] = mm_tn(n, dup, N_DEV, name=f"ffn_up_dw{tag}")
    dh, (dnorm,) = plan.run(f"ffn_up_dx{tag}", mm_nt, dup, plan.weight(f"up{tag}"), out_dtype=F32,
                            name=f"ffn_up_dx{tag}", norm=(h, dh_out, [norm_g], []))
    return dh, dnorm, dconv_w, dconv_b


def local_step(x, target, wts, plan):
    row = lambda v: v.reshape(1, -1)
    attn_norm, ffn_norm = wts["attn_norm"], wts["ffn_norm"]
    conv_w, conv_b = wts["ffn_conv_w"], wts["ffn_conv_b"]

    (n1,) = rms_fwd(x, [row(attn_norm[0])], name="attn_norm_fwd0")
    proj = plan.run("gla_in", mm_nn, n1, wts["gla_w_in"], out_dtype=F32, name="gla_in")
    la = gate_fwd(proj, wts["gla_w_a2"], wts["gla_b_a2"])
    o_gla, states = plan.run("gla_fwd", gla_fwd, proj, la)
    og = headnorm_fwd(o_gla, proj, wts["gla_head_norm"])
    h1 = plan.run("gla_out", mm_nn, og, plan.weight("gout"), out_dtype=F32, res=x, name="gla_out")
    h2, ffn0 = _ffn_fwd(plan, h1, row(ffn_norm[0]), conv_w[0], row(conv_b[0]), "0")

    h2s = to_streams(h2, name="h2_to_streams")
    kvn, n3 = rms_fwd(h2s, [row(wts["kv_norm"]), row(attn_norm[1])], name="kv_attn_norm_fwd")
    kv = plan.run("kv_proj", mm_nn, kvn, plan.weight("kv"), out_dtype=F32, name="kv_proj")
    q = plan.run("q_proj", mm_nn, n3, plan.weight("q"), out_dtype=F32, name="q_proj")
    o_att, lse = plan.run("attn_fwd", attn_fwd, q, kv)
    h3 = from_streams(plan.run("dsa_out", mm_nn, o_att, plan.weight("dout"), out_dtype=F32, res=h2s, name="dsa_out"),
                      name="h3_from_streams")
    h4, ffn1 = _ffn_fwd(plan, h3, row(ffn_norm[1]), conv_w[1], row(conv_b[1]), "1")

    loss_tile, dh4, d_final = loss_head(h4, row(wts["final_norm"]), target)

    dh3, d_ffn1, dcw1, dcb1 = _ffn_bwd(plan, dh4, h3, ffn1, row(ffn_norm[1]), conv_w[1], row(conv_b[1]), "1")
    dh3s = to_streams(dh3, name="dh3_to_streams")
    plan.grads["dout"] = _by_rows(mm_tn(o_att, dh3s, 1, name="dsa_out_dw"))
    do_att = mm_nt(dh3s, plan.weight("dout"), out_dtype=F32, name="dsa_out_dx")
    dq, dkv = plan.run("attn_bwd", attn_bwd, q, kv, o_att, lse, do_att)
    plan.grads["q"] = mm_tn(n3, dq, N_DEV, name="q_proj_dw")
    dn3 = plan.run("q_proj_dx", mm_nt, dq, plan.weight("q"), out_dtype=F32, name="q_proj_dx")
    plan.grads["kv"] = plan.run("kv_proj_dw", mm_tn, kvn, dkv, N_DEV, name="kv_proj_dw")
    dh2s, (d_kvnorm, d_attn1) = plan.run("kv_proj_dx", mm_nt, dkv, plan.weight("kv"), out_dtype=F32, name="kv_proj_dx",
                                         norm=(h2s, dh3s, [row(wts["kv_norm"]), row(attn_norm[1])], [dn3]))
    dh2 = from_streams(dh2s, name="dh2_from_streams")
    dh1, d_ffn0, dcw0, dcb0 = _ffn_bwd(plan, dh2, h1, ffn0, row(ffn_norm[0]), conv_w[0], row(conv_b[0]), "0")
    plan.grads["gout"] = _by_rows(plan.run("gla_out_dw", mm_tn, og, dh1, 1, name="gla_out_dw"))
    dog = plan.run("gla_out_dx", mm_nt, dh1, plan.weight("gout"), out_dtype=F32, name="gla_out_dx")
    do_gla, dr, d_hn = headnorm_bwd(o_gla, proj, wts["gla_head_norm"], dog)
    dq_g, dk_g, dv_g, dla = plan.run("gla_bwd", gla_bwd, proj, la, states, do_gla)
    da, dw_a2p, db_a2 = gate_bwd(proj, wts["gla_w_a2"], wts["gla_b_a2"], dla)
    dproj = jnp.concatenate([dq_g, dk_g, dv_g, dr, da], axis=1)
    assert dproj.shape[1] == GLA_IN_PAD
    dw_in = plan.run("gla_in_dw", mm_tn, n1, dproj, 1, name="gla_in_dw")
    plan.grads["in"] = dw_in[0, :, :GLA_IN_DIM].reshape(D_MODEL, N_DEV, GLA_IN_DIM // N_DEV).transpose(1, 0, 2)
    grad_x, (d_attn0,) = plan.run("gla_in_dx", mm_nt, dproj, wts["gla_w_in"], out_dtype=F32, name="gla_in_dx",
                                  norm=(x, dh1, [row(attn_norm[0])], []))

    small = dict(
        attn_norm=jnp.concatenate([d_attn0, d_attn1], axis=0),
        ffn_norm=jnp.concatenate([d_ffn0, d_ffn1], axis=0),
        kv_norm=d_kvnorm.reshape(-1),
        final_norm=d_final.reshape(-1),
        ffn_conv_b=jnp.concatenate([dcb0, dcb1], axis=0),
        gla_w_a2=dw_a2p[:GATE_RANK],
        gla_b_a2=db_a2,
        gla_head_norm=d_hn,
        ffn_conv_w=jnp.stack([dcw0, dcw1]),
    )
    return loss_tile, grad_x, small


SMALL_ORDER = ("attn_norm", "ffn_norm", "kv_norm", "final_norm", "ffn_conv_b",
               "gla_w_a2", "gla_b_a2", "gla_head_norm", "ffn_conv_w")
SMALL_FULL = dict(attn_norm=(2, D_MODEL), ffn_norm=(2, D_MODEL), kv_norm=(D_MODEL,), final_norm=(D_MODEL,),
                  ffn_conv_b=(2, D_FF), gla_w_a2=(GATE_RANK, GLA_KEY_DIM), gla_b_a2=(1, GLA_KEY_DIM),
                  gla_head_norm=(1, GLA_DV), ffn_conv_w=(2, 3, D_FF))
SMALL_SHARDED = ("gla_w_a2", "gla_b_a2", "gla_head_norm", "ffn_conv_w")
SMALL_GRAD_ROWS = 592
SMALL_ADAM_ROWS = 240


def kernel(x, attn_norm, gla_w_in, gla_w_a2, gla_b_a2, gla_head_norm, gla_w_out, kv_norm, w_kv, dsa_w_q, dsa_w_out, ffn_norm, ffn_w_up, ffn_conv_w, ffn_conv_b, ffn_w_down, final_norm, loss_target, m_attn_norm, m_gla_w_in, m_gla_w_a2, m_gla_b_a2, m_gla_head_norm, m_gla_w_out, m_kv_norm, m_w_kv, m_dsa_w_q, m_dsa_w_out, m_ffn_norm, m_ffn_w_up, m_ffn_conv_w, m_ffn_conv_b, m_ffn_w_down, m_final_norm, v_attn_norm, v_gla_w_in, v_gla_w_a2, v_gla_b_a2, v_gla_head_norm, v_gla_w_out, v_kv_norm, v_w_kv, v_dsa_w_q, v_dsa_w_out, v_ffn_norm, v_ffn_w_up, v_ffn_conv_w, v_ffn_conv_b, v_ffn_w_down, v_final_norm):
    me = 4 * lax.axis_index("x") + 2 * lax.axis_index("y") + lax.axis_index("c")
    bf = lambda a: a.astype(BF16)

    g_in, g_small = all_gather([bf(gla_w_in[0]), _pack_small_weights(gla_w_a2, gla_b_a2, gla_head_norm, ffn_conv_w)],
                               name="gather_first")
    w_a2_full, b_a2_full, hn_full, conv_w_full = _unpack_small_weights(g_small)
    w_in_full = jnp.pad(g_in.transpose(1, 0, 2).reshape(D_MODEL, GLA_IN_DIM), ((0, 0), (0, GLA_IN_PAD - GLA_IN_DIM)))
    wts = dict(
        attn_norm=attn_norm, ffn_norm=ffn_norm, kv_norm=kv_norm, final_norm=final_norm, ffn_conv_b=ffn_conv_b,
        gla_w_in=w_in_full[None],
        gla_w_a2=jnp.pad(bf(w_a2_full), ((0, LANE - GATE_RANK), (0, 0))),
        gla_b_a2=b_a2_full, gla_head_norm=hn_full, ffn_conv_w=conv_w_full,
    )
    plan = Plan({}, srcs=dict(gout=bf(gla_w_out[0]), kv=bf(w_kv), q=bf(dsa_w_q[0]), dout=bf(dsa_w_out[0]),
                              up0=bf(ffn_w_up[0]), up1=bf(ffn_w_up[1]), dn0=bf(ffn_w_down[0]), dn1=bf(ffn_w_down[1])))

    loss_tile, grad_x, small = local_step(x[0], loss_target[0], wts, plan)
    loss = lax.psum(loss_tile[0, 0], ("x", "y", "c"))

    plan.run("grads_tail", exchange_only, name="grads_tail")
    shard3 = lambda a: a.reshape((-1,) + a.shape[-2:])
    big_params = dict(gla_w_in=(("in",), gla_w_in, m_gla_w_in, v_gla_w_in),
                      gla_w_out=(("gout",), gla_w_out, m_gla_w_out, v_gla_w_out),
                      w_kv=(("kv",), w_kv, m_w_kv, v_w_kv),
                      dsa_w_q=(("q",), dsa_w_q, m_dsa_w_q, v_dsa_w_q),
                      dsa_w_out=(("dout",), dsa_w_out, m_dsa_w_out, v_dsa_w_out),
                      ffn_w_up=(("up0", "up1"), ffn_w_up, m_ffn_w_up, v_ffn_w_up),
                      ffn_w_down=(("dn0", "dn1"), ffn_w_down, m_ffn_w_down, v_ffn_w_down))
    res = {}
    for nm, (parts, w, m, v) in big_params.items():
        outs = adam_sharded([plan.recv[p] for p in parts], shard3(w), shard3(m), shard3(v), name=f"adam_{nm}")
        res[nm] = [o.reshape(w.shape) for o in outs]

    packed = _pack_rows([small[nm] for nm in SMALL_ORDER], SMALL_GRAD_ROWS)
    (parts,) = all_gather([packed], name="gather_small_grads")
    full = dict(zip(SMALL_ORDER, _unpack_rows(sum_partials(parts), [SMALL_FULL[nm] for nm in SMALL_ORDER])))
    local_w = dict(attn_norm=attn_norm, ffn_norm=ffn_norm, kv_norm=kv_norm, final_norm=final_norm,
                   ffn_conv_b=ffn_conv_b, gla_w_a2=gla_w_a2, gla_b_a2=gla_b_a2, gla_head_norm=gla_head_norm,
                   ffn_conv_w=ffn_conv_w)
    local_m = dict(attn_norm=m_attn_norm, ffn_norm=m_ffn_norm, kv_norm=m_kv_norm, final_norm=m_final_norm,
                   ffn_conv_b=m_ffn_conv_b, gla_w_a2=m_gla_w_a2, gla_b_a2=m_gla_b_a2, gla_head_norm=m_gla_head_norm,
                   ffn_conv_w=m_ffn_conv_w)
    local_v = dict(attn_norm=v_attn_norm, ffn_norm=v_ffn_norm, kv_norm=v_kv_norm, final_norm=v_final_norm,
                   ffn_conv_b=v_ffn_conv_b, gla_w_a2=v_gla_w_a2, gla_b_a2=v_gla_b_a2, gla_head_norm=v_gla_head_norm,
                   ffn_conv_w=v_ffn_conv_w)
    local_g = {}
    for nm in SMALL_ORDER:
        gfull = full[nm]
        if nm in SMALL_SHARDED:
            per = gfull.shape[-1] // N_DEV
            gfull = lax.dynamic_slice_in_dim(gfull, me * per, per, axis=gfull.ndim - 1)
        local_g[nm] = gfull.reshape(local_w[nm].shape)
    shapes = [local_w[nm].shape for nm in SMALL_ORDER]
    pk = lambda dd: _pack_rows([dd[nm] for nm in SMALL_ORDER], SMALL_ADAM_ROWS)
    d_p, m_p, v_p = adam_packed(pk(local_w), pk(local_g), pk(local_m), pk(local_v))
    for nm, dl, mn, vn in zip(SMALL_ORDER, _unpack_rows(d_p, shapes), _unpack_rows(m_p, shapes),
                              _unpack_rows(v_p, shapes)):
        res[nm] = [local_g[nm], dl, mn, vn]

    order = ("attn_norm", "gla_w_in", "gla_w_a2", "gla_b_a2", "gla_head_norm", "gla_w_out", "kv_norm", "w_kv",
             "dsa_w_q", "dsa_w_out", "ffn_norm", "ffn_w_up", "ffn_conv_w", "ffn_conv_b", "ffn_w_down", "final_norm")
    outs = [loss, grad_x[None]]
    for kind in range(4):
        outs.extend(res[nm][kind] for nm in order)
    return tuple(outs)
```

```python
import functools

import jax
import jax.numpy as jnp
from jax import lax
from jax.experimental import pallas as pl
from jax.experimental.pallas import tpu as pltpu

F32 = jnp.float32
BF16 = jnp.bfloat16
MESH = pl.DeviceIdType.MESH
ANY = pl.BlockSpec(memory_space=pl.ANY)

N_DEV = 8
D_MODEL = 2048
GLA_HEADS = 4
GLA_KEY_DIM = 1024
GLA_VAL_DIM = 2048
GLA_DK = 256
GLA_DV = 512
GATE_RANK = 16
GATE_NORMALIZER = 16.0
GLA_CHUNK = 64
GLA_STEP_CHUNKS = 2
GLA_IN_DIM = 2 * GLA_KEY_DIM + 2 * GLA_VAL_DIM + GATE_RANK
GLA_IN_PAD = 6272
ATT_HEADS = 16
HEAD_DIM = 128
DILATIONS = (1, 4, 16)
STREAMS = DILATIONS[-1]
ATT_BLOCK = 128
D_FF = 5632
EPS = 1e-6
ADAM_LR = 0.001
ADAM_B1 = 0.9
ADAM_B2 = 0.999
ADAM_EPS = 1e-08
ADAM_WD = 0.01
ADAM_STEP = 10
NEG = -1e30
LANE = 128
NORM_ROWS = 64
VMEM_LIMIT = 52 * 1024 * 1024
ALIBI_SLOPES = tuple(2.0 ** (-0.5 * (i + 1)) for i in range(ATT_HEADS))


def _params(*sem):
    return pltpu.CompilerParams(dimension_semantics=sem, vmem_limit_bytes=VMEM_LIMIT)


def _tile(n, cap):
    best = None
    for t in range(LANE, min(n, cap) + 1, LANE):
        if n % t == 0:
            best = t
    return best if best is not None else n


def _shard_group(j, ns, cap):
    best = 1
    for g in range(1, j + 1):
        if j % g == 0 and g * ns <= cap:
            best = g
    return best


def _rows(r, c, budget=256 * 1024):
    best = None
    for t in range(16, r + 1, 16):
        if r % t == 0 and t * c <= budget:
            best = t
    return best if best is not None else r


def _flip(coord, bit):
    return 1 - coord if bit else coord


def _place():
    x, y, c = lax.axis_index("x"), lax.axis_index("y"), lax.axis_index("c")
    return x, y, c, 4 * x + 2 * y + c


def _rows_of(ref, rows):
    return ref if rows is None else ref.at[pl.ds(rows[0], rows[1] - rows[0])]


class Jobs:
    def __init__(self):
        self.srcs = []
        self.bufs = []
        self.sems = []
        self.steps = []

    def _src(self, a):
        for i, b in enumerate(self.srcs):
            if b is a:
                return i
        self.srcs.append(a)
        return len(self.srcs) - 1

    def new(self, shape, dtype):
        self.bufs.append((None, jax.ShapeDtypeStruct(shape, dtype)))
        return len(self.bufs) - 1

    def thru(self, a):
        self.bufs.append((a, jax.ShapeDtypeStruct(a.shape, a.dtype)))
        return len(self.bufs) - 1

    def _sem(self, n):
        self.sems.append(pltpu.SemaphoreType.DMA((n,)))
        return len(self.sems) - 1

    def gather_ici(self, src, buf, rows=None):
        si, send, recv, loc = self._src(src), self._sem(4), self._sem(4), self._sem(1)

        def remote(srcs, bufs, sems, slot_of):
            x, y, c, me = _place()
            peers = [(x, y, 1 - c), (1 - x, y, c), (x, 1 - y, c), (1 - x, 1 - y, c)]
            return [pltpu.make_async_remote_copy(
                src_ref=_rows_of(srcs[si], rows),
                dst_ref=_rows_of(bufs[buf].at[me if slot_of == "mine" else 4 * p[0] + 2 * p[1] + p[2]], rows),
                send_sem=sems[send].at[k], recv_sem=sems[recv].at[k], device_id=p, device_id_type=MESH)
                for k, p in enumerate(peers)]

        def local(srcs, bufs, sems):
            return pltpu.make_async_copy(_rows_of(srcs[si], rows), _rows_of(bufs[buf].at[_place()[3]], rows),
                                         sems[loc].at[0])

        def start(srcs, bufs, sems):
            local(srcs, bufs, sems).start()
            for cp in remote(srcs, bufs, sems, "mine"):
                cp.start()

        def finish(srcs, bufs, sems):
            for cp in remote(srcs, bufs, sems, "peer"):
                cp.wait_recv()
            for cp in remote(srcs, bufs, sems, "mine"):
                cp.wait_send()
            local(srcs, bufs, sems).wait()

        self.steps.append((start, finish))

    def gather_d2d(self, buf, rows=None):
        send, recv = self._sem(3), self._sem(3)

        def copies(bufs, sems, core):
            x, y, c, _ = _place()
            cc = c if core == "mine" else 1 - c
            chips = [(1 - x, y), (x, 1 - y), (1 - x, 1 - y)]
            return [pltpu.make_async_remote_copy(
                src_ref=_rows_of(bufs[buf].at[4 * px + 2 * py + cc], rows),
                dst_ref=_rows_of(bufs[buf].at[4 * px + 2 * py + cc], rows),
                send_sem=sems[send].at[k], recv_sem=sems[recv].at[k],
                device_id=(x, y, 1 - c), device_id_type=MESH) for k, (px, py) in enumerate(chips)]

        def start(srcs, bufs, sems):
            for cp in copies(bufs, sems, "mine"):
                cp.start()

        def finish(srcs, bufs, sems):
            for cp in copies(bufs, sems, "sibling"):
                cp.wait_recv()
            for cp in copies(bufs, sems, "mine"):
                cp.wait_send()

        self.steps.append((start, finish))

    def scatter(self, src, buf, rows=None):
        si, send, recv, loc = self._src(src), self._sem(N_DEV - 1), self._sem(N_DEV - 1), self._sem(1)

        def remote(srcs, bufs, sems, slot_of):
            x, y, c, me = _place()
            out = []
            for k in range(1, N_DEV):
                px, py, pc = _flip(x, k >> 2), _flip(y, (k >> 1) & 1), _flip(c, k & 1)
                peer = 4 * px + 2 * py + pc
                out.append(pltpu.make_async_remote_copy(
                    src_ref=_rows_of(srcs[si].at[peer], rows),
                    dst_ref=_rows_of(bufs[buf].at[me if slot_of == "mine" else peer], rows),
                    send_sem=sems[send].at[k - 1], recv_sem=sems[recv].at[k - 1],
                    device_id=(px, py, pc), device_id_type=MESH))
            return out

        def local(srcs, bufs, sems):
            me = _place()[3]
            return pltpu.make_async_copy(_rows_of(srcs[si].at[me], rows), _rows_of(bufs[buf].at[me], rows),
                                         sems[loc].at[0])

        def start(srcs, bufs, sems):
            local(srcs, bufs, sems).start()
            for cp in remote(srcs, bufs, sems, "mine"):
                cp.start()

        def finish(srcs, bufs, sems):
            for cp in remote(srcs, bufs, sems, "peer"):
                cp.wait_recv()
            for cp in remote(srcs, bufs, sems, "mine"):
                cp.wait_send()
            local(srcs, bufs, sems).wait()

        self.steps.append((start, finish))


def _call(body, *, name, grid, in_specs, out_specs, out_shape, args, sem, scratch_shapes=(), jobs=None):
    in_specs, out_specs, out_shape = list(in_specs), list(out_specs), list(out_shape)
    scratch_shapes = list(scratch_shapes)
    if jobs is None:
        res = pl.pallas_call(body, name=name, out_shape=out_shape, grid=grid, in_specs=in_specs,
                             out_specs=out_specs, scratch_shapes=scratch_shapes,
                             compiler_params=_params(*sem))(*args)
        return list(res), []
    thru = [a for a, _ in jobs.bufs if a is not None]
    n_in, n_src, n_thru = len(args), len(jobs.srcs), len(thru)
    n_out, n_buf, n_scr = len(out_shape), len(jobs.bufs), len(scratch_shapes)
    aliases, t = {}, 0
    for b, (a, _) in enumerate(jobs.bufs):
        if a is not None:
            aliases[n_in + n_src + t] = n_out + b
            t += 1

    def wrapped(*refs):
        at = 0
        ins = refs[at:at + n_in]; at += n_in
        srcs = refs[at:at + n_src]; at += n_src + n_thru
        outs = refs[at:at + n_out]; at += n_out
        bufs = refs[at:at + n_buf]; at += n_buf
        scr = refs[at:at + n_scr]; at += n_scr
        sems = refs[at:]
        first, last = None, None
        for axis, size in enumerate(grid):
            pid = pl.program_id(axis)
            f, l = pid == 0, pid == size - 1
            first = f if first is None else first & f
            last = l if last is None else last & l

        @pl.when(first)
        def _():
            for start, _ in jobs.steps:
                start(srcs, bufs, sems)

        body(*ins, *outs, *scr)

        @pl.when(last)
        def _():
            for _, finish in jobs.steps:
                finish(srcs, bufs, sems)

    res = pl.pallas_call(
        wrapped, name=name,
        out_shape=out_shape + [s for _, s in jobs.bufs],
        grid=grid,
        in_specs=in_specs + [ANY] * (n_src + n_thru),
        out_specs=out_specs + [ANY] * n_buf,
        scratch_shapes=scratch_shapes + jobs.sems,
        input_output_aliases=aliases,
        compiler_params=_params(*(["arbitrary"] * len(grid))),
    )(*args, *jobs.srcs, *thru)
    return res[:n_out], res[n_out:]


def mm_nn(a, w, *, out_dtype, name, res=None, tm=None, jobs=None):
    m, k = a.shape
    j, k2, ns = w.shape
    whole = j == 1 and ns <= 2048 and k <= 2048
    tm = tm or (1024 if a.dtype == BF16 and not whole else 512)
    assert k == k2 and m % tm == 0
    tn = ns if whole else _tile(ns, 1408)
    nsub = ns // tn
    tk = k if k <= 2048 else _tile(k, 1408)
    nk = k // tk
    has_res = res is not None

    def body(*refs):
        if has_res:
            a_ref, w_ref, r_ref, o_ref, acc = refs
        else:
            a_ref, w_ref, o_ref, acc = refs
        kk = pl.program_id(2)

        @pl.when(kk == 0)
        def _():
            acc[...] = jnp.zeros_like(acc)

        acc[...] += jnp.dot(a_ref[...].astype(BF16), w_ref[...], preferred_element_type=F32)

        @pl.when(kk == nk - 1)
        def _():
            r = acc[...]
            if has_res:
                r = r + r_ref[...]
            o_ref[...] = r.astype(out_dtype)

    in_specs = [
        pl.BlockSpec((tm, tk), lambda i, n, kk: (i, kk)),
        pl.BlockSpec((None, tk, tn), lambda i, n, kk: (n // nsub, kk, n % nsub)),
    ]
    args = [a, w]
    out_tile = pl.BlockSpec((tm, tn), lambda i, n, kk: (i, n))
    if has_res:
        in_specs.append(out_tile)
        args.append(res)
    (out,), bufs = _call(
        body, name=name, jobs=jobs,
        out_shape=[jax.ShapeDtypeStruct((m, j * ns), out_dtype)],
        grid=(m // tm, j * nsub, nk),
        in_specs=in_specs,
        out_specs=[out_tile],
        scratch_shapes=[pltpu.VMEM((tm, tn), F32)],
        args=args, sem=("parallel", "parallel", "arbitrary"))
    return out if jobs is None else (out, bufs)


def mm_nt(dy, w, *, out_dtype, name, tm=None, jobs=None, norm=None):
    parts, m, n = (1,) + dy.shape if dy.ndim == 2 else dy.shape
    n *= parts
    j, k, ns = w.shape
    if norm is not None:
        x, dres, gains, more = norm
        tm = tm or (256 if more else 512)
    tm = tm or 1024
    assert n == j * ns and m % tm == 0
    tn = _tile(ns, 2048)
    nsub = ns // tn
    jb = _shard_group(j // parts, ns, 2048 if norm is None else 1024) if nsub == 1 else 1
    tko = _tile(k, 1408) if norm is None else k
    nn = j * nsub // jb
    per_part = nn // parts
    if dy.ndim == 2:
        dy_spec = pl.BlockSpec((tm, jb * tn), lambda i, ko, nq: (i, nq))
    else:
        dy_spec = pl.BlockSpec((None, tm, jb * tn), lambda i, ko, nq: (nq // per_part, i, nq % per_part))
    if jb == 1:
        w_spec = pl.BlockSpec((None, tko, tn), lambda i, ko, nq: (nq // nsub, ko, nq % nsub))
    else:
        w_spec = pl.BlockSpec((jb, tko, ns), lambda i, ko, nq: (nq, ko, 0))

    n_gain = 0 if norm is None else len(gains)
    n_more = 0 if norm is None else len(more)

    def body(*refs):
        a_ref, w_ref = refs[:2]
        acc = refs[-1]
        nq = pl.program_id(2)
        first = pl.program_id(0) == 0

        @pl.when(nq == 0)
        def _():
            acc[...] = jnp.zeros_like(acc)

        if jb == 1:
            acc[...] += lax.dot_general(a_ref[...].astype(BF16), w_ref[...], (((1,), (1,)), ((), ())),
                                        preferred_element_type=F32)
        else:
            part = acc[...]
            for jj in range(jb):
                part = part + lax.dot_general(a_ref[:, jj * ns:(jj + 1) * ns].astype(BF16), w_ref[jj],
                                              (((1,), (1,)), ((), ())), preferred_element_type=F32)
            acc[...] = part

        @pl.when(nq == nn - 1)
        def _():
            if norm is None:
                refs[2][...] = acc[...].astype(out_dtype)
                return
            x_ref, r_ref = refs[2:4]
            g_refs = refs[4:4 + n_gain]
            e_refs = refs[4 + n_gain:4 + n_gain + n_more]
            dx_ref = refs[4 + n_gain + n_more]
            dg_refs = refs[5 + n_gain + n_more:-1]

            @pl.when(first)
            def _():
                for dg_ref in dg_refs:
                    dg_ref[...] = jnp.zeros_like(dg_ref)

            def rows(c, carry):
                sl = pl.ds(pl.multiple_of(c * NORM_ROWS, NORM_ROWS), NORM_ROWS)
                xv = x_ref[sl, :]
                r = lax.rsqrt(jnp.mean(xv * xv, axis=-1, keepdims=True) + EPS)
                xh = xv * r
                out = r_ref[sl, :]
                for idx, (g_ref, dg_ref) in enumerate(zip(g_refs, dg_refs)):
                    dyv = acc[sl, :] if idx == 0 else e_refs[idx - 1][sl, :].astype(F32)
                    dg_ref[...] += jnp.sum(dyv * xh, axis=0, keepdims=True)
                    dxh = dyv * g_ref[...]
                    out = out + r * (dxh - xh * jnp.mean(dxh * xh, axis=-1, keepdims=True))
                dx_ref[sl, :] = out
                return carry

            lax.fori_loop(0, tm // NORM_ROWS, rows, 0)

    out_tile = pl.BlockSpec((tm, tko), lambda i, ko, nq: (i, ko))
    in_specs, args = [dy_spec, w_spec], [dy, w]
    out_shape, out_specs = [jax.ShapeDtypeStruct((m, k), out_dtype)], [out_tile]
    sem = ("parallel", "parallel", "arbitrary")
    if norm is not None:
        vec = pl.BlockSpec((1, k), lambda i, ko, nq: (0, 0))
        in_specs += [out_tile, out_tile] + [vec] * n_gain + [out_tile] * n_more
        args += [x, dres] + list(gains) + list(more)
        out_shape = [jax.ShapeDtypeStruct((m, k), F32)] + [jax.ShapeDtypeStruct((1, k), F32)] * n_gain
        out_specs = [out_tile] + [vec] * n_gain
        sem = ("arbitrary", "arbitrary", "arbitrary")
    outs, bufs = _call(
        body, name=name, jobs=jobs, out_shape=out_shape, grid=(m // tm, k // tko, nn),
        in_specs=in_specs, out_specs=out_specs, scratch_shapes=[pltpu.VMEM((tm, tko), F32)], args=args, sem=sem)
    out = outs[0] if norm is None else (outs[0], outs[1:])
    return out if jobs is None else (out, bufs)


def mm_tn(x, dy, j, *, name, tm=1024, jobs=None):
    m, k = x.shape
    parts, m2, n = (1,) + dy.shape if dy.ndim == 2 else dy.shape
    n *= parts
    assert m == m2 and n % j == 0 and m % tm == 0
    ns = n // j
    tn = _tile(ns, 1408)
    nsub = ns // tn
    jb = _shard_group(j // parts, ns, 1536) if nsub == 1 else 1
    tk = _tile(k, 1408)
    nm = m // tm
    n_steps = j * nsub // jb
    per_part = n_steps // parts
    if dy.ndim == 2:
        dy_spec = pl.BlockSpec((tm, jb * tn), lambda kq, nq, mi: (mi, nq))
    else:
        dy_spec = pl.BlockSpec((None, tm, jb * tn), lambda kq, nq, mi: (nq // per_part, mi, nq % per_part))
    if jb == 1:
        out_spec = pl.BlockSpec((None, tk, tn), lambda kq, nq, mi: (nq // nsub, kq, nq % nsub))
        acc_shape = (tk, tn)
    else:
        out_spec = pl.BlockSpec((jb, tk, ns), lambda kq, nq, mi: (nq, kq, 0))
        acc_shape = (jb, tk, ns)

    def body(x_ref, dy_ref, o_ref, acc):
        mi = pl.program_id(2)

        @pl.when(mi == 0)
        def _():
            acc[...] = jnp.zeros_like(acc)

        xb = x_ref[...].astype(BF16)
        if jb == 1:
            acc[...] += lax.dot_general(xb, dy_ref[...].astype(BF16), (((0,), (0,)), ((), ())),
                                        preferred_element_type=F32)
        else:
            for jj in range(jb):
                acc[jj] += lax.dot_general(xb, dy_ref[:, jj * ns:(jj + 1) * ns].astype(BF16),
                                           (((0,), (0,)), ((), ())), preferred_element_type=F32)

        @pl.when(mi == nm - 1)
        def _():
            o_ref[...] = acc[...].astype(BF16)

    (out,), bufs = _call(
        body, name=name, jobs=jobs,
        out_shape=[jax.ShapeDtypeStruct((j, k, ns), BF16)],
        grid=(k // tk, n_steps, nm),
        in_specs=[
            pl.BlockSpec((tm, tk), lambda kq, nq, mi: (mi, kq)),
            dy_spec,
        ],
        out_specs=[out_spec],
        scratch_shapes=[pltpu.VMEM(acc_shape, F32)],
        args=[x, dy], sem=("parallel", "parallel", "arbitrary"))
    return out if jobs is None else (out, bufs)


STREAM_TC = LANE


def to_streams(x, *, name):
    s, c = x.shape
    per = s // STREAMS

    def body(x_ref, o_ref):
        for st in range(STREAMS):
            o_ref[pl.ds(st * per, per), :] = x_ref[pl.ds(st, per, stride=STREAMS), :]

    blk = pl.BlockSpec((s, STREAM_TC), lambda i: (0, i))
    return pl.pallas_call(body, name=name, out_shape=jax.ShapeDtypeStruct((s, c), x.dtype), grid=(c // STREAM_TC,),
                          in_specs=[blk], out_specs=blk, compiler_params=_params("parallel"))(x)


def from_streams(x, *, name):
    s, c = x.shape
    per = s // STREAMS

    def body(x_ref, o_ref):
        for st in range(STREAMS):
            o_ref[pl.ds(st, per, stride=STREAMS), :] = x_ref[pl.ds(st * per, per), :]

    blk = pl.BlockSpec((s, STREAM_TC), lambda i: (0, i))
    return pl.pallas_call(body, name=name, out_shape=jax.ShapeDtypeStruct((s, c), x.dtype), grid=(c // STREAM_TC,),
                          in_specs=[blk], out_specs=blk, compiler_params=_params("parallel"))(x)


def rms_fwd(x, gains, *, name, ts=512):
    s, d = x.shape
    n = len(gains)

    def body(x_ref, *refs):
        xv = x_ref[...]
        xh = xv * lax.rsqrt(jnp.mean(xv * xv, axis=-1, keepdims=True) + EPS)
        for g_ref, o_ref in zip(refs[:n], refs[n:]):
            o_ref[...] = (xh * g_ref[...]).astype(BF16)

    row = pl.BlockSpec((ts, d), lambda i: (i, 0))
    vec = pl.BlockSpec((1, d), lambda i: (0, 0))
    return pl.pallas_call(
        body,
        name=name,
        out_shape=[jax.ShapeDtypeStruct((s, d), BF16)] * n,
        grid=(s // ts,),
        in_specs=[row] + [vec] * n,
        out_specs=[row] * n,
        compiler_params=_params("parallel"),
    )(x, *gains)


def loss_head(h, gain, target, *, ts=256):
    s, d = h.shape

    def body(h_ref, g_ref, t_ref, l_ref, dh_ref, dg_ref):
        i = pl.program_id(0)

        @pl.when(i == 0)
        def _():
            l_ref[...] = jnp.zeros_like(l_ref)
            dg_ref[...] = jnp.zeros_like(dg_ref)

        xv = h_ref[...]
        r = lax.rsqrt(jnp.mean(xv * xv, axis=-1, keepdims=True) + EPS)
        xh = xv * r
        g = g_ref[...]
        err = xh * g - t_ref[...]
        l_ref[...] += 0.5 * jnp.sum(jnp.mean(err * err, axis=-1, keepdims=True))
        dy = err * (1.0 / d)
        dg_ref[...] += jnp.sum(dy * xh, axis=0, keepdims=True)
        dxh = dy * g
        dh_ref[...] = r * (dxh - xh * jnp.mean(dxh * xh, axis=-1, keepdims=True))

    row = pl.BlockSpec((ts, d), lambda i: (i, 0))
    vec = pl.BlockSpec((1, d), lambda i: (0, 0))
    return pl.pallas_call(
        body,
        name="loss_head",
        out_shape=[jax.ShapeDtypeStruct((8, LANE), F32), jax.ShapeDtypeStruct((s, d), F32),
                   jax.ShapeDtypeStruct((1, d), F32)],
        grid=(s // ts,),
        in_specs=[row, vec, row],
        out_specs=[pl.BlockSpec((8, LANE), lambda i: (0, 0)), row, vec],
        compiler_params=_params("arbitrary"),
    )(h, gain, target)


A_BLOCK = (2 * GLA_KEY_DIM + 2 * GLA_VAL_DIM) // LANE


def gate_fwd(proj, w_a2p, b_a2, *, ts=512):
    s = proj.shape[0]

    def body(a_ref, w_ref, b_ref, o_ref):
        z = jnp.dot(a_ref[...].astype(BF16), w_ref[...], preferred_element_type=F32) + b_ref[...]
        o_ref[...] = (jnp.minimum(z, 0.0) - jnp.log(1.0 + jnp.exp(-jnp.abs(z)))) * (1.0 / GATE_NORMALIZER)

    return pl.pallas_call(
        body,
        name="gate_fwd",
        out_shape=jax.ShapeDtypeStruct((s, GLA_KEY_DIM), F32),
        grid=(s // ts,),
        in_specs=[pl.BlockSpec((ts, LANE), lambda i: (i, A_BLOCK)),
                  pl.BlockSpec((LANE, GLA_KEY_DIM), lambda i: (0, 0)),
                  pl.BlockSpec((1, GLA_KEY_DIM), lambda i: (0, 0))],
        out_specs=pl.BlockSpec((ts, GLA_KEY_DIM), lambda i: (i, 0)),
        compiler_params=_params("parallel"),
    )(proj, w_a2p, b_a2)


def gate_bwd(proj, w_a2p, b_a2, dla, *, ts=512):
    s = proj.shape[0]

    def body(a_ref, w_ref, b_ref, dla_ref, da_ref, dw_ref, db_ref):
        i = pl.program_id(0)

        @pl.when(i == 0)
        def _():
            dw_ref[...] = jnp.zeros_like(dw_ref)
            db_ref[...] = jnp.zeros_like(db_ref)

        a = a_ref[...].astype(BF16)
        w = w_ref[...]
        z = jnp.dot(a, w, preferred_element_type=F32) + b_ref[...]
        dz = dla_ref[...] * (1.0 / GATE_NORMALIZER) / (1.0 + jnp.exp(z))
        dzb = dz.astype(BF16)
        da_ref[...] = lax.dot_general(dzb, w, (((1,), (1,)), ((), ())), preferred_element_type=F32).astype(BF16)
        dw_ref[...] += lax.dot_general(a, dzb, (((0,), (0,)), ((), ())), preferred_element_type=F32)
        db_ref[...] += jnp.sum(dz, axis=0, keepdims=True)

    return pl.pallas_call(
        body,
        name="gate_bwd",
        out_shape=[jax.ShapeDtypeStruct((s, LANE), BF16), jax.ShapeDtypeStruct((LANE, GLA_KEY_DIM), F32),
                   jax.ShapeDtypeStruct((1, GLA_KEY_DIM), F32)],
        grid=(s // ts,),
        in_specs=[pl.BlockSpec((ts, LANE), lambda i: (i, A_BLOCK)),
                  pl.BlockSpec((LANE, GLA_KEY_DIM), lambda i: (0, 0)),
                  pl.BlockSpec((1, GLA_KEY_DIM), lambda i: (0, 0)),
                  pl.BlockSpec((ts, GLA_KEY_DIM), lambda i: (i, 0))],
        out_specs=[pl.BlockSpec((ts, LANE), lambda i: (i, 0)),
                   pl.BlockSpec((LANE, GLA_KEY_DIM), lambda i: (0, 0)),
                   pl.BlockSpec((1, GLA_KEY_DIM), lambda i: (0, 0))],
        compiler_params=_params("arbitrary"),
    )(proj, w_a2p, b_a2, dla)


def _chunk_terms(q, k, la):
    c_len = GLA_CHUNK
    row = lax.broadcasted_iota(jnp.int32, (c_len, c_len), 0)
    col = lax.broadcasted_iota(jnp.int32, (c_len, c_len), 1)
    tri = row >= col
    c = jnp.dot(tri.astype(F32), la, preferred_element_type=F32, precision=lax.Precision.HIGHEST)
    last = jnp.sum(la, axis=0, keepdims=True)
    q_dec = q * (GLA_DK ** -0.5) * jnp.exp(c)
    k_inv = k * jnp.exp(-c)
    k_end = k * jnp.exp(last - c)
    return c, last, q_dec, k_inv, k_end, tri


def _dot(a, b, ca, cb):
    return lax.dot_general(a.astype(BF16), b.astype(BF16), (((ca,), (cb,)), ((), ())), preferred_element_type=F32)


def gla_fwd(proj, la, jobs=None):
    s = proj.shape[0]
    n_chunks = s // GLA_CHUNK
    rows = GLA_CHUNK * GLA_STEP_CHUNKS

    def body(q_ref, k_ref, v_ref, la_ref, o_ref, st_out, st):
        @pl.when(pl.program_id(0) == 0)
        def _():
            st[...] = jnp.zeros_like(st)

        for h in range(GLA_HEADS):
            hk = slice(h * GLA_DK, (h + 1) * GLA_DK)
            hv = slice(h * GLA_DV, (h + 1) * GLA_DV)
            for cc in range(GLA_STEP_CHUNKS):
                rs = slice(cc * GLA_CHUNK, (cc + 1) * GLA_CHUNK)
                _, last, q_dec, k_inv, k_end, tri = _chunk_terms(q_ref[rs, hk], k_ref[rs, hk], la_ref[rs, hk])
                v = v_ref[rs, hv]
                a = jnp.where(tri, _dot(q_dec, k_inv, 1, 1), 0.0)
                state = st[h]
                st_out[h, cc] = state
                o_ref[rs, hv] = _dot(a, v, 1, 0) + _dot(q_dec, state, 1, 1)
                st[h] = state * jnp.exp(last) + _dot(v, k_end, 0, 0)

    key = lambda col: pl.BlockSpec((rows, GLA_KEY_DIM), lambda n: (n, col))
    outs, bufs = _call(
        body, name="gla_fwd", jobs=jobs,
        out_shape=[jax.ShapeDtypeStruct((s, GLA_VAL_DIM), F32),
                   jax.ShapeDtypeStruct((GLA_HEADS, n_chunks, GLA_DV, GLA_DK), F32)],
        grid=(n_chunks // GLA_STEP_CHUNKS,),
        in_specs=[key(0), key(1), pl.BlockSpec((rows, GLA_VAL_DIM), lambda n: (n, 1)), key(0)],
        out_specs=[pl.BlockSpec((rows, GLA_VAL_DIM), lambda n: (n, 0)),
                   pl.BlockSpec((GLA_HEADS, GLA_STEP_CHUNKS, GLA_DV, GLA_DK), lambda n: (0, n, 0, 0))],
        scratch_shapes=[pltpu.VMEM((GLA_HEADS, GLA_DV, GLA_DK), F32)],
        args=[proj, proj, proj, la], sem=("arbitrary",))
    return outs if jobs is None else (outs, bufs)


def gla_bwd(proj, la, states, do, jobs=None):
    s = proj.shape[0]
    n_steps = s // GLA_CHUNK // GLA_STEP_CHUNKS
    lastc = n_steps - 1
    rows = GLA_CHUNK * GLA_STEP_CHUNKS

    def body(q_ref, k_ref, v_ref, la_ref, do_ref, st_ref, dq_ref, dk_ref, dv_ref, dla_ref, dst):
        @pl.when(pl.program_id(0) == 0)
        def _():
            dst[...] = jnp.zeros_like(dst)

        upper = (lax.broadcasted_iota(jnp.int32, (GLA_CHUNK, GLA_CHUNK), 0)
                 <= lax.broadcasted_iota(jnp.int32, (GLA_CHUNK, GLA_CHUNK), 1)).astype(F32)
        for h in range(GLA_HEADS):
            hk = slice(h * GLA_DK, (h + 1) * GLA_DK)
            hv = slice(h * GLA_DV, (h + 1) * GLA_DV)
            for cc in reversed(range(GLA_STEP_CHUNKS)):
                rs = slice(cc * GLA_CHUNK, (cc + 1) * GLA_CHUNK)
                c, last, q_dec, k_inv, k_end, tri = _chunk_terms(q_ref[rs, hk], k_ref[rs, hk], la_ref[rs, hk])
                v = v_ref[rs, hv]
                dout = do_ref[rs, hv]
                state = st_ref[h, cc]
                dstate = dst[h]
                e_last = jnp.exp(last)
                a = jnp.where(tri, _dot(q_dec, k_inv, 1, 1), 0.0)
                da = jnp.where(tri, _dot(dout, v, 1, 1), 0.0)
                dv_ref[rs, hv] = (_dot(a, dout, 0, 0) + _dot(k_end, dstate, 1, 1)).astype(BF16)
                dq_dec = _dot(da, k_inv, 1, 0) + _dot(dout, state, 1, 0)
                dk_inv = _dot(da, q_dec, 0, 0)
                dk_end = _dot(v, dstate, 1, 0)
                dst[h] = dstate * e_last + _dot(dout, q_dec, 0, 0)
                dq_ref[rs, hk] = (dq_dec * (GLA_DK ** -0.5) * jnp.exp(c)).astype(BF16)
                dk_ref[rs, hk] = (dk_inv * jnp.exp(-c) + dk_end * jnp.exp(last - c)).astype(BF16)
                ke_term = dk_end * k_end
                dc = dq_dec * q_dec - dk_inv * k_inv - ke_term
                dlast = (jnp.sum(ke_term, axis=0, keepdims=True)
                         + e_last * jnp.sum(dstate * state, axis=0, keepdims=True))
                dla_ref[rs, hk] = jnp.dot(upper, dc, preferred_element_type=F32,
                                          precision=lax.Precision.HIGHEST) + dlast

    key = lambda col: pl.BlockSpec((rows, GLA_KEY_DIM), lambda n: (lastc - n, col))
    val = lambda col: pl.BlockSpec((rows, GLA_VAL_DIM), lambda n: (lastc - n, col))
    outs, bufs = _call(
        body, name="gla_bwd", jobs=jobs,
        out_shape=[jax.ShapeDtypeStruct((s, GLA_KEY_DIM), BF16), jax.ShapeDtypeStruct((s, GLA_KEY_DIM), BF16),
                   jax.ShapeDtypeStruct((s, GLA_VAL_DIM), BF16), jax.ShapeDtypeStruct((s, GLA_KEY_DIM), F32)],
        grid=(n_steps,),
        in_specs=[key(0), key(1), val(1), key(0), val(0),
                  pl.BlockSpec((GLA_HEADS, GLA_STEP_CHUNKS, GLA_DV, GLA_DK), lambda n: (0, lastc - n, 0, 0))],
        out_specs=[key(0), key(0), val(0), key(0)],
        scratch_shapes=[pltpu.VMEM((GLA_HEADS, GLA_DV, GLA_DK), F32)],
        args=[proj, proj, proj, la, do, states], sem=("arbitrary",))
    return outs if jobs is None else (outs, bufs)


R_BLOCK = (2 * GLA_KEY_DIM + GLA_VAL_DIM) // GLA_DV


def headnorm_fwd(o, proj, hn, *, ts=512):
    s = o.shape[0]

    def body(o_ref, r_ref, g_ref, out_ref):
        ov = o_ref[...]
        oh = ov * lax.rsqrt(jnp.mean(ov * ov, axis=-1, keepdims=True) + EPS)
        r = r_ref[...]
        out_ref[...] = (oh * g_ref[...] * (r * jax.nn.sigmoid(r))).astype(BF16)

    return pl.pallas_call(
        body,
        name="headnorm_fwd",
        out_shape=jax.ShapeDtypeStruct((s, GLA_VAL_DIM), BF16),
        grid=(s // ts, GLA_HEADS),
        in_specs=[pl.BlockSpec((ts, GLA_DV), lambda i, h: (i, h)),
                  pl.BlockSpec((ts, GLA_DV), lambda i, h: (i, R_BLOCK + h)),
                  pl.BlockSpec((1, GLA_DV), lambda i, h: (0, 0))],
        out_specs=pl.BlockSpec((ts, GLA_DV), lambda i, h: (i, h)),
        compiler_params=_params("parallel", "parallel"),
    )(o, proj, hn)


def headnorm_bwd(o, proj, hn, dog, *, ts=512):
    s = o.shape[0]

    def body(o_ref, r_ref, g_ref, dog_ref, do_ref, dr_ref, dg_ref):
        @pl.when((pl.program_id(0) == 0) & (pl.program_id(1) == 0))
        def _():
            dg_ref[...] = jnp.zeros_like(dg_ref)

        ov = o_ref[...]
        rr = lax.rsqrt(jnp.mean(ov * ov, axis=-1, keepdims=True) + EPS)
        oh = ov * rr
        g = g_ref[...]
        r = r_ref[...]
        sig = jax.nn.sigmoid(r)
        gate = r * sig
        dog_v = dog_ref[...]
        d_on = dog_v * gate
        dr_ref[...] = (dog_v * (oh * g) * (sig * (1.0 + r * (1.0 - sig)))).astype(BF16)
        dg_ref[...] += jnp.sum(d_on * oh, axis=0, keepdims=True)
        doh = d_on * g
        do_ref[...] = rr * (doh - oh * jnp.mean(doh * oh, axis=-1, keepdims=True))

    return pl.pallas_call(
        body,
        name="headnorm_bwd",
        out_shape=[jax.ShapeDtypeStruct((s, GLA_VAL_DIM), F32), jax.ShapeDtypeStruct((s, GLA_VAL_DIM), BF16),
                   jax.ShapeDtypeStruct((1, GLA_DV), F32)],
        grid=(s // ts, GLA_HEADS),
        in_specs=[pl.BlockSpec((ts, GLA_DV), lambda i, h: (i, h)),
                  pl.BlockSpec((ts, GLA_DV), lambda i, h: (i, R_BLOCK + h)),
                  pl.BlockSpec((1, GLA_DV), lambda i, h: (0, 0)),
                  pl.BlockSpec((ts, GLA_DV), lambda i, h: (i, h))],
        out_specs=[pl.BlockSpec((ts, GLA_DV), lambda i, h: (i, h)),
                   pl.BlockSpec((ts, GLA_DV), lambda i, h: (i, h)),
                   pl.BlockSpec((1, GLA_DV), lambda i, h: (0, 0))],
        compiler_params=_params("arbitrary", "arbitrary"),
    )(o, proj, hn, dog)


CONV_TC = 128
SQRT_HALF = 0.7071067811865476
INV_SQRT_2PI = 0.3989422804014327


def _conv_gate(g_ref, cw_ref, cb_ref):
    g0 = g_ref[...].astype(F32)
    t = lax.broadcasted_iota(jnp.int32, g0.shape, 0)
    g1 = jnp.where(t >= 1, pltpu.roll(g0, 1, 0), 0.0)
    g2 = jnp.where(t >= 2, pltpu.roll(g0, 2, 0), 0.0)
    gc = cw_ref[0:1, :] * g2 + cw_ref[1:2, :] * g1 + cw_ref[2:3, :] * g0 + cb_ref[...]
    return g0, g1, g2, gc, t


def convglu_fwd(up, conv_w, conv_b, *, name, jobs=None):
    s = up.shape[0]
    nc = D_FF // CONV_TC

    def body(u_ref, g_ref, cw_ref, cb_ref, o_ref):
        _, _, _, gc, _ = _conv_gate(g_ref, cw_ref, cb_ref)
        gelu = 0.5 * gc * (1.0 + lax.erf(gc * SQRT_HALF))
        o_ref[...] = (gelu * u_ref[...].astype(F32)).astype(BF16)

    (out,), bufs = _call(
        body, name=name, jobs=jobs,
        out_shape=[jax.ShapeDtypeStruct((s, D_FF), BF16)],
        grid=(nc,),
        in_specs=[pl.BlockSpec((s, CONV_TC), lambda c: (0, c)),
                  pl.BlockSpec((s, CONV_TC), lambda c: (0, nc + c)),
                  pl.BlockSpec((3, CONV_TC), lambda c: (0, c)),
                  pl.BlockSpec((1, CONV_TC), lambda c: (0, c))],
        out_specs=[pl.BlockSpec((s, CONV_TC), lambda c: (0, c))],
        args=[up, up, conv_w, conv_b], sem=("parallel",))
    return out if jobs is None else (out, bufs)


def convglu_bwd(up, conv_w, conv_b, dact, *, name, jobs=None):
    s = up.shape[0]
    nc = D_FF // CONV_TC

    def body(u_ref, g_ref, cw_ref, cb_ref, da_ref, dup_ref, dcw_ref, dcb_ref):
        du_ref, dg_ref = dup_ref.at[0], dup_ref.at[1]
        g0, g1, g2, gc, t = _conv_gate(g_ref, cw_ref, cb_ref)
        cdf = 0.5 * (1.0 + lax.erf(gc * SQRT_HALF))
        da = da_ref[...].astype(F32)
        du_ref[...] = (da * gc * cdf).astype(BF16)
        dgc = da * u_ref[...].astype(F32) * (cdf + gc * jnp.exp(-0.5 * gc * gc) * INV_SQRT_2PI)
        dcb_ref[...] = jnp.sum(dgc, axis=0, keepdims=True)
        dcw_ref[0:1, :] = jnp.sum(dgc * g2, axis=0, keepdims=True)
        dcw_ref[1:2, :] = jnp.sum(dgc * g1, axis=0, keepdims=True)
        dcw_ref[2:3, :] = jnp.sum(dgc * g0, axis=0, keepdims=True)
        n1 = jnp.where(t < s - 1, pltpu.roll(dgc, s - 1, 0), 0.0)
        n2 = jnp.where(t < s - 2, pltpu.roll(dgc, s - 2, 0), 0.0)
        dg_ref[...] = (cw_ref[2:3, :] * dgc + cw_ref[1:2, :] * n1 + cw_ref[0:1, :] * n2).astype(BF16)

    col = pl.BlockSpec((s, CONV_TC), lambda c: (0, c))
    outs, bufs = _call(
        body, name=name, jobs=jobs,
        out_shape=[jax.ShapeDtypeStruct((2, s, D_FF), BF16),
                   jax.ShapeDtypeStruct((3, D_FF), F32), jax.ShapeDtypeStruct((1, D_FF), F32)],
        grid=(nc,),
        in_specs=[col, pl.BlockSpec((s, CONV_TC), lambda c: (0, nc + c)),
                  pl.BlockSpec((3, CONV_TC), lambda c: (0, c)),
                  pl.BlockSpec((1, CONV_TC), lambda c: (0, c)), col],
        out_specs=[pl.BlockSpec((2, s, CONV_TC), lambda c: (0, 0, c)), pl.BlockSpec((3, CONV_TC), lambda c: (0, c)),
                   pl.BlockSpec((1, CONV_TC), lambda c: (0, c))],
        args=[up, up, conv_w, conv_b, dact], sem=("parallel",))
    return outs if jobs is None else (outs, bufs)


SLOPE_TILE = (8, LANE)


def _slope_table():
    return jnp.broadcast_to(jnp.asarray(ALIBI_SLOPES, F32)[:, None, None], (ATT_HEADS,) + SLOPE_TILE)


def _pieces(s_len, d):
    npc = STREAMS // d
    lp = ATT_BLOCK // npc
    return npc, lp, (s_len // STREAMS) // lp


def _gather(ref, r, b, d, s_len):
    npc, lp, _ = _pieces(s_len, d)
    per = s_len // STREAMS
    parts = [ref[pl.ds((r + d * k) * per + b * lp, lp), :] for k in range(npc)]
    return parts[0] if npc == 1 else jnp.concatenate(parts, axis=0)


def _scatter(ref, r, b, d, s_len, val, add=False):
    npc, lp, _ = _pieces(s_len, d)
    per = s_len // STREAMS
    for k in range(npc):
        rows = pl.ds((r + d * k) * per + b * lp, lp)
        piece = val[k * lp:(k + 1) * lp]
        if add:
            ref[rows, :] += piece
        else:
            ref[rows, :] = piece


def _stream_bias(slope, d, s_len):
    npc, lp, _ = _pieces(s_len, d)
    qi = lax.broadcasted_iota(jnp.int32, (ATT_BLOCK, 2 * ATT_BLOCK), 0)
    c = lax.broadcasted_iota(jnp.int32, (ATT_BLOCK, 2 * ATT_BLOCK), 1)
    own = c // ATT_BLOCK
    cc = c - own * ATT_BLOCK
    dist = npc * ((qi % lp) - (cc % lp) + lp * (1 - own)) + (qi // lp - cc // lp)
    ok = (dist >= 0) & (dist <= ATT_BLOCK)
    return jnp.where(ok, (slope * (-float(d))) * dist.astype(F32), NEG)


def attn_fwd(q, kv, jobs=None):
    s_len = q.shape[0]
    scale = HEAD_DIM ** -0.5

    def body(sl_ref, q_ref, k_ref, v_ref, o_ref, lse_ref):
        g = pl.program_id(1)
        slope = sl_ref[0:1, 0:1]

        def branch(gi, d):
            _, _, nblk = _pieces(s_len, d)
            bias = _stream_bias(slope, d, s_len)
            for r in range(d):
                for b in range(nblk):
                    qb = _gather(q_ref, r, b, d, s_len)
                    kc, vc = _gather(k_ref, r, b, d, s_len), _gather(v_ref, r, b, d, s_len)
                    if b == 0:
                        kcat, vcat, bb = kc, vc, bias[:, ATT_BLOCK:]
                    else:
                        kcat = jnp.concatenate([_gather(k_ref, r, b - 1, d, s_len), kc], axis=0)
                        vcat = jnp.concatenate([_gather(v_ref, r, b - 1, d, s_len), vc], axis=0)
                        bb = bias
                    sc = _dot(qb, kcat, 1, 1) * scale + bb
                    m = jnp.max(sc, axis=-1, keepdims=True)
                    p = jnp.exp(sc - m)
                    l = jnp.sum(p, axis=-1, keepdims=True)
                    o_new = _dot(p, vcat, 1, 0) / l
                    lse_new = m + jnp.log(l)
                    if gi > 0:
                        lse_old = _gather(lse_ref, r, b, d, s_len)[:, 0:1]
                        top = jnp.maximum(lse_old, lse_new)
                        e_old, e_new = jnp.exp(lse_old - top), jnp.exp(lse_new - top)
                        den = e_old + e_new
                        o_new = (e_old * _gather(o_ref, r, b, d, s_len) + e_new * o_new) / den
                        lse_new = top + jnp.log(den)
                    _scatter(o_ref, r, b, d, s_len, o_new)
                    _scatter(lse_ref, r, b, d, s_len, jnp.broadcast_to(lse_new, (ATT_BLOCK, HEAD_DIM)))

        for gi, d in enumerate(DILATIONS):
            @pl.when(g == gi)
            def _():
                branch(gi, d)

    blk = lambda col: pl.BlockSpec((s_len, HEAD_DIM), lambda h, g: (0, col(h, g)))
    head = lambda h, g: h
    outs, bufs = _call(
        body, name="attn_fwd", jobs=jobs,
        out_shape=[jax.ShapeDtypeStruct((s_len, ATT_HEADS * HEAD_DIM), F32)] * 2,
        grid=(ATT_HEADS, len(DILATIONS)),
        in_specs=[pl.BlockSpec((None,) + SLOPE_TILE, lambda h, g: (h, 0, 0)),
                  blk(lambda h, g: g * ATT_HEADS + h), blk(head), blk(lambda h, g: ATT_HEADS + h)],
        out_specs=[blk(head), blk(head)],
        args=[_slope_table(), q, kv, kv], sem=("parallel", "arbitrary"))
    return outs if jobs is None else (outs, bufs)


def attn_bwd(q, kv, o, lse, do, jobs=None):
    s_len = q.shape[0]
    scale = HEAD_DIM ** -0.5
    chunks = s_len // ATT_BLOCK

    def body(sl_ref, q_ref, k_ref, v_ref, o_ref, lse_ref, do_ref, dq_ref, dkv_ref, dlt):
        g = pl.program_id(1)
        slope = sl_ref[0:1, 0:1]
        dk_ref, dv_ref = dkv_ref.at[0], dkv_ref.at[1]

        @pl.when(g == 0)
        def _():
            dkv_ref[...] = jnp.zeros_like(dkv_ref)

            def deltas(c, carry):
                rows = pl.ds(pl.multiple_of(c * ATT_BLOCK, ATT_BLOCK), ATT_BLOCK)
                dlt[rows, :] = jnp.sum(do_ref[rows, :] * o_ref[rows, :], axis=-1, keepdims=True)
                return carry
            lax.fori_loop(0, chunks, deltas, 0)

        def branch(d):
            _, _, nblk = _pieces(s_len, d)
            bias = _stream_bias(slope, d, s_len)
            for r in range(d):
                for b in range(nblk):
                    qb = _gather(q_ref, r, b, d, s_len)
                    dob = _gather(do_ref, r, b, d, s_len)
                    kc, vc = _gather(k_ref, r, b, d, s_len), _gather(v_ref, r, b, d, s_len)
                    if b == 0:
                        kcat, vcat, bb = kc, vc, bias[:, ATT_BLOCK:]
                    else:
                        kcat = jnp.concatenate([_gather(k_ref, r, b - 1, d, s_len), kc], axis=0)
                        vcat = jnp.concatenate([_gather(v_ref, r, b - 1, d, s_len), vc], axis=0)
                        bb = bias
                    sc = _dot(qb, kcat, 1, 1) * scale + bb
                    p = jnp.exp(sc - _gather(lse_ref, r, b, d, s_len)[:, 0:1])
                    ds = p * (_dot(dob, vcat, 1, 1) - _gather(dlt, r, b, d, s_len))
                    _scatter(dq_ref, r, b, d, s_len, _dot(ds, kcat, 1, 0) * scale)
                    dk = _dot(ds, qb, 0, 0) * scale
                    dv = _dot(p, dob, 0, 0)
                    if b == 0:
                        _scatter(dk_ref, r, b, d, s_len, dk, add=True)
                        _scatter(dv_ref, r, b, d, s_len, dv, add=True)
                    else:
                        _scatter(dk_ref, r, b - 1, d, s_len, dk[:ATT_BLOCK], add=True)
                        _scatter(dv_ref, r, b - 1, d, s_len, dv[:ATT_BLOCK], add=True)
                        _scatter(dk_ref, r, b, d, s_len, dk[ATT_BLOCK:], add=True)
                        _scatter(dv_ref, r, b, d, s_len, dv[ATT_BLOCK:], add=True)

        for gi, d in enumerate(DILATIONS):
            @pl.when(g == gi)
            def _():
                branch(d)

    blk = lambda col: pl.BlockSpec((s_len, HEAD_DIM), lambda h, g: (0, col(h, g)))
    head = lambda h, g: h
    q_col = lambda h, g: g * ATT_HEADS + h
    outs, bufs = _call(
        body, name="attn_bwd", jobs=jobs,
        out_shape=[jax.ShapeDtypeStruct(q.shape, F32), jax.ShapeDtypeStruct((2, s_len, ATT_HEADS * HEAD_DIM), F32)],
        grid=(ATT_HEADS, len(DILATIONS)),
        in_specs=[pl.BlockSpec((None,) + SLOPE_TILE, lambda h, g: (h, 0, 0)),
                  blk(q_col), blk(head), blk(lambda h, g: ATT_HEADS + h), blk(head), blk(head), blk(head)],
        out_specs=[blk(q_col), pl.BlockSpec((2, s_len, HEAD_DIM), lambda h, g: (0, 0, h))],
        scratch_shapes=[pltpu.VMEM((s_len, 1), F32)],
        args=[_slope_table(), q, kv, kv, o, lse, do], sem=("parallel", "arbitrary"))
    return outs if jobs is None else (outs, bufs)


def _adam(w, g, m, v):
    m = ADAM_B1 * m + (1.0 - ADAM_B1) * g
    v = ADAM_B2 * v + (1.0 - ADAM_B2) * (g * g)
    m_hat = m / (1.0 - ADAM_B1 ** ADAM_STEP)
    v_hat = v / (1.0 - ADAM_B2 ** ADAM_STEP)
    delta = -ADAM_LR * (m_hat / (jnp.sqrt(v_hat) + ADAM_EPS) + ADAM_WD * w)
    return delta, m, v


def adam_sharded(recvs, w, m, v, *, name):
    layers = len(recvs)
    n_src, r, c = recvs[0].shape
    tr = _rows(r, c)

    def body(*refs):
        p_refs = refs[:layers]
        w_ref, m_ref, v_ref, g_ref, d_ref, mo_ref, vo_ref = refs[layers:]
        for layer, p_ref in enumerate(p_refs):
            @pl.when(pl.program_id(0) == layer)
            def _():
                g = p_ref[0].astype(F32)
                for src in range(1, n_src):
                    g = g + p_ref[src].astype(F32)
                delta, m_new, v_new = _adam(w_ref[...], g, m_ref[...], v_ref[...])
                g_ref[...] = g
                d_ref[...] = delta
                mo_ref[...] = m_new
                vo_ref[...] = v_new

    blk = pl.BlockSpec((None, tr, c), lambda l, i: (l, i, 0))
    out = jax.ShapeDtypeStruct((layers, r, c), F32)
    part = [pl.BlockSpec((n_src, tr, c), functools.partial(lambda l, i, layer: (0, jnp.where(l == layer, i, 0), 0),
                                                            layer=layer)) for layer in range(layers)]
    return pl.pallas_call(
        body,
        name=name,
        out_shape=[out] * 4,
        grid=(layers, r // tr),
        in_specs=part + [blk, blk, blk],
        out_specs=[blk] * 4,
        compiler_params=_params("parallel", "parallel"),
    )(*recvs, w, m, v)


def sum_partials(parts):
    n_src, r, c = parts.shape

    def body(p_ref, o_ref):
        g = p_ref[0]
        for src in range(1, n_src):
            g = g + p_ref[src]
        o_ref[...] = g

    return pl.pallas_call(
        body,
        name="sum_small_grads",
        out_shape=jax.ShapeDtypeStruct((r, c), F32),
    )(parts)


def adam_packed(w, g, m, v):
    def body(w_ref, g_ref, m_ref, v_ref, d_ref, mo_ref, vo_ref):
        delta, m_new, v_new = _adam(w_ref[...], g_ref[...], m_ref[...], v_ref[...])
        d_ref[...] = delta
        mo_ref[...] = m_new
        vo_ref[...] = v_new

    out = jax.ShapeDtypeStruct(w.shape, F32)
    return pl.pallas_call(body, name="adam_small", out_shape=[out] * 3)(w, g, m, v)


def all_gather(srcs, *, name):
    n = len(srcs)

    def body(*refs):
        src, dst = refs[:n], refs[n:2 * n]
        send_sems, recv_sems, local_sems = refs[2 * n:]
        x, y, c, me = _place()
        sibling = (x, y, 1 - c)
        chips = [(1 - x, y), (x, 1 - y), (1 - x, 1 - y)]

        def index(px, py, pc):
            return 4 * px + 2 * py + pc

        def copy(p, k, block, to, from_src=False):
            slot = dst[p].at[index(*block)]
            return pltpu.make_async_remote_copy(
                src_ref=src[p] if from_src else slot, dst_ref=slot,
                send_sem=send_sems.at[p, k], recv_sem=recv_sems.at[p, k],
                device_id=to, device_id_type=MESH)

        mine = [pltpu.make_async_copy(src[p], dst[p].at[me], local_sems.at[p]) for p in range(n)]
        for cp in mine:
            cp.start()
        first = []
        for p in range(n):
            first.append(copy(p, 0, (x, y, c), sibling, from_src=True))
            for jj, chip in enumerate(chips):
                first.append(copy(p, 1 + jj, (x, y, c), (*chip, c), from_src=True))
        for cp in first:
            cp.start()
        passed = []
        for jj, chip in enumerate(chips):
            for p in range(n):
                copy(p, 1 + jj, (*chip, c), (x, y, c)).wait_recv()
                fwd = copy(p, 4 + jj, (*chip, c), sibling)
                fwd.start()
                passed.append(fwd)
        for p in range(n):
            copy(p, 0, sibling, (x, y, c)).wait_recv()
            for jj, chip in enumerate(chips):
                copy(p, 4 + jj, (*chip, 1 - c), (x, y, c)).wait_recv()
        for cp in first + passed:
            cp.wait_send()
        for cp in mine:
            cp.wait()

    return pl.pallas_call(
        body,
        name=name,
        out_shape=[jax.ShapeDtypeStruct((N_DEV,) + a.shape, a.dtype) for a in srcs],
        in_specs=[ANY] * n,
        out_specs=[ANY] * n,
        scratch_shapes=[pltpu.SemaphoreType.DMA((n, 7)), pltpu.SemaphoreType.DMA((n, 7)),
                        pltpu.SemaphoreType.DMA((n,))],
    )(*srcs)


def exchange_only(*, name, jobs):
    def body(o_ref):
        o_ref[...] = jnp.zeros_like(o_ref)

    _, bufs = _call(body, name=name, jobs=jobs, out_shape=[jax.ShapeDtypeStruct((8, LANE), F32)], grid=(1,),
                    in_specs=[], out_specs=[pl.BlockSpec((8, LANE), lambda i: (0, 0))], args=[], sem=("arbitrary",))
    return None, bufs


def _pack_rows(parts, rows):
    flat = jnp.concatenate([p.reshape(-1) for p in parts])
    return jnp.pad(flat, (0, rows * LANE - flat.shape[0])).reshape(rows, LANE)


def _unpack_rows(packed, shapes):
    flat = packed.reshape(-1)
    out, at = [], 0
    for sh in shapes:
        size = 1
        for dim in sh:
            size *= dim
        out.append(flat[at:at + size].reshape(sh))
        at += size
    return out


CONV_W_PAD = 768
SMALL_W_ROWS = 56


def _pack_small_weights(w_a2, b_a2, hn, conv_w):
    cw = jnp.pad(conv_w.reshape(6, -1), ((0, 0), (0, CONV_W_PAD - conv_w.shape[-1]))).reshape(-1, LANE)
    rows = jnp.concatenate([w_a2[0], b_a2, jnp.pad(hn, ((0, 0), (0, LANE - hn.shape[-1]))), cw], axis=0)
    return jnp.pad(rows, ((0, SMALL_W_ROWS - rows.shape[0]), (0, 0)))


def _unpack_small_weights(gathered):
    w_a2 = gathered[:, 0:GATE_RANK, :].transpose(1, 0, 2).reshape(GATE_RANK, GLA_KEY_DIM)
    b_a2 = gathered[:, GATE_RANK, :].reshape(1, GLA_KEY_DIM)
    hn = gathered[:, GATE_RANK + 1, :GLA_DV // N_DEV].reshape(1, GLA_DV)
    per = D_FF // N_DEV
    cw = gathered[:, GATE_RANK + 2:GATE_RANK + 2 + 6 * CONV_W_PAD // LANE, :].reshape(N_DEV, 6, CONV_W_PAD)[:, :, :per]
    cw = cw.reshape(N_DEV, 2, 3, per).transpose(1, 2, 0, 3).reshape(2, 3, D_FF)
    return w_a2, b_a2, hn, cw


SCHEDULE = {
    "gla_in": [("g1", "gout", None), ("g1", "up0", (0, 1024))],
    "gla_fwd": [("g2", "gout", None), ("g2", "up0", (0, 1024)), ("g1", "up0", (1024, 2048))],
    "gla_out": [("g2", "up0", (1024, 2048)), ("g1", "dn0", (0, 352))],
    "ffn_up0": [("g2", "dn0", (0, 352)), ("g1", "dn0", (352, 704)), ("g1", "kv", None), ("g1", "q", (0, 768))],
    "convglu_fwd0": [("g2", "dn0", (352, 704))],
    "ffn_down0": [("g2", "kv", None), ("g2", "q", (0, 768)), ("g1", "q", (768, 2048)), ("g1", "dout", None)],
    "kv_proj": [("g2", "q", (768, 2048)), ("g2", "dout", None), ("g1", "up1", (0, 704))],
    "q_proj": [("g2", "up1", (0, 704)), ("g1", "up1", (704, 1664))],
    "attn_fwd": [("g2", "up1", (704, 1664)), ("g1", "up1", (1664, 2048)), ("g1", "dn1", None)],
    "dsa_out": [("g2", "up1", (1664, 2048)), ("g2", "dn1", None)],
    "ffn_down_dx1": [("sc", "dn1", (0, 352))],
    "convglu_bwd1": [("sc", "dn1", (352, 704))],
    "ffn_up_dx1": [("sc", "up1", (0, 1024))],
    "attn_bwd": [("sc", "up1", (1024, 2048)), ("sc", "dout", None)],
    "q_proj_dx": [("sc", "q", (0, 1024))],
    "kv_proj_dw": [("sc", "q", (1024, 1792))],
    "kv_proj_dx": [("sc", "q", (1792, 2048)), ("sc", "kv", (0, 768))],
    "ffn_down_dw0": [("sc", "kv", (768, 2048))],
    "ffn_down_dx0": [("sc", "dn0", (0, 384))],
    "convglu_bwd0": [("sc", "dn0", (384, 704))],
    "ffn_up_dx0": [("sc", "up0", (0, 1024))],
    "gla_out_dw": [("sc", "up0", (1024, 1216))],
    "gla_out_dx": [("sc", "up0", (1216, 1408))],
    "gla_bwd": [("sc", "up0", (1408, 2048)), ("sc", "gout", (0, 128))],
    "gla_in_dw": [("sc", "gout", (128, 256))],
    "gla_in_dx": [("sc", "in", (0, 1536))],
    "grads_tail": [("sc", "in", (1536, 2048))],
}
ROW_SHARDED = ("gout", "dout", "dn0", "dn1")


class Plan:
    def __init__(self, weights, srcs=None):
        self.w = dict(weights)
        self.srcs = srcs
        self.grads = {}
        self.recv = {}
        self._names = None

    def weight(self, name):
        buf = self.w[name]
        if name in ROW_SHARDED:
            return buf.reshape(1, buf.shape[0] * buf.shape[1], buf.shape[2])
        return buf

    def jobs(self, call):
        ops = SCHEDULE.get(call)
        if self.srcs is None or not ops:
            return None
        jobs, handles = Jobs(), {}
        for op, name, rows in ops:
            store = self.recv if op == "sc" else self.w
            if name not in handles:
                if name in store:
                    handles[name] = jobs.thru(store[name])
                elif op == "sc":
                    handles[name] = jobs.new(self.grads[name].shape, BF16)
                else:
                    handles[name] = jobs.new((N_DEV,) + self.srcs[name].shape, BF16)
            if op == "g1":
                jobs.gather_ici(self.srcs[name], handles[name], rows)
            elif op == "g2":
                jobs.gather_d2d(handles[name], rows)
            else:
                jobs.scatter(self.grads[name], handles[name], rows)
        self._names = [(name, self.recv if ops[0][0] == "sc" else self.w) for name in handles]
        assert len({op == "sc" for op, _, _ in ops}) == 1
        return jobs

    def run(self, call, fn, *args, **kwargs):
        jobs = self.jobs(call)
        if jobs is None:
            return fn(*args, **kwargs)
        out, bufs = fn(*args, jobs=jobs, **kwargs)
        for (name, store), buf in zip(self._names, bufs):
            store[name] = buf
        return out


def _ffn_fwd(plan, h, norm_g, conv_w, conv_b, tag):
    (n,) = rms_fwd(h, [norm_g], name=f"ffn_norm_fwd{tag}")
    up = plan.run(f"ffn_up{tag}", mm_nn, n, plan.weight(f"up{tag}"), out_dtype=BF16, name=f"ffn_up{tag}")
    act = plan.run(f"convglu_fwd{tag}", convglu_fwd, up, conv_w, conv_b, name=f"convglu_fwd{tag}")
    h_out = plan.run(f"ffn_down{tag}", mm_nn, act, plan.weight(f"dn{tag}"), out_dtype=F32, res=h,
                     name=f"ffn_down{tag}")
    return h_out, (n, up, act)


def _by_rows(dw):
    return dw.reshape(N_DEV, dw.shape[1] // N_DEV, dw.shape[2])


def _ffn_bwd(plan, dh_out, h, saved, norm_g, conv_w, conv_b, tag):
    n, up, act = saved
    plan.grads[f"dn{tag}"] = _by_rows(plan.run(f"ffn_down_dw{tag}", mm_tn, act, dh_out, 1, name=f"ffn_down_dw{tag}"))
    dact = plan.run(f"ffn_down_dx{tag}", mm_nt, dh_out, plan.weight(f"dn{tag}"), out_dtype=BF16,
                    name=f"ffn_down_dx{tag}")
    dup, dconv_w, dconv_b = plan.run(f"convglu_bwd{tag}", convglu_bwd, up, conv_w, conv_b, dact,
                                     name=f"convglu_bwd{tag}")
    plan.grads[f"up{tag}"] = mm_tn(n, dup, N_DEV, name=f"ffn_up_dw{tag}")
    dh, (dnorm,) = plan.run(f"ffn_up_dx{tag}", mm_nt, dup, plan.weight(f"up{tag}"), out_dtype=F32,
                            name=f"ffn_up_dx{tag}", norm=(h, dh_out, [norm_g], []))
    return dh, dnorm, dconv_w, dconv_b


def local_step(x, target, wts, plan):
    row = lambda v: v.reshape(1, -1)
    attn_norm, ffn_norm = wts["attn_norm"], wts["ffn_norm"]
    conv_w, conv_b = wts["ffn_conv_w"], wts["ffn_conv_b"]

    (n1,) = rms_fwd(x, [row(attn_norm[0])], name="attn_norm_fwd0")
    proj = plan.run("gla_in", mm_nn, n1, wts["gla_w_in"], out_dtype=F32, name="gla_in")
    la = gate_fwd(proj, wts["gla_w_a2"], wts["gla_b_a2"])
    o_gla, states = plan.run("gla_fwd", gla_fwd, proj, la)
    og = headnorm_fwd(o_gla, proj, wts["gla_head_norm"])
    h1 = plan.run("gla_out", mm_nn, og, plan.weight("gout"), out_dtype=F32, res=x, name="gla_out")
    h2, ffn0 = _ffn_fwd(plan, h1, row(ffn_norm[0]), conv_w[0], row(conv_b[0]), "0")

    h2s = to_streams(h2, name="h2_to_streams")
    kvn, n3 = rms_fwd(h2s, [row(wts["kv_norm"]), row(attn_norm[1])], name="kv_attn_norm_fwd")
    kv = plan.run("kv_proj", mm_nn, kvn, plan.weight("kv"), out_dtype=F32, name="kv_proj")
    q = plan.run("q_proj", mm_nn, n3, plan.weight("q"), out_dtype=F32, name="q_proj")
    o_att, lse = plan.run("attn_fwd", attn_fwd, q, kv)
    h3 = from_streams(plan.run("dsa_out", mm_nn, o_att, plan.weight("dout"), out_dtype=F32, res=h2s, name="dsa_out"),
                      name="h3_from_streams")
    h4, ffn1 = _ffn_fwd(plan, h3, row(ffn_norm[1]), conv_w[1], row(conv_b[1]), "1")

    loss_tile, dh4, d_final = loss_head(h4, row(wts["final_norm"]), target)

    dh3, d_ffn1, dcw1, dcb1 = _ffn_bwd(plan, dh4, h3, ffn1, row(ffn_norm[1]), conv_w[1], row(conv_b[1]), "1")
    dh3s = to_streams(dh3, name="dh3_to_streams")
    plan.grads["dout"] = _by_rows(mm_tn(o_att, dh3s, 1, name="dsa_out_dw"))
    do_att = mm_nt(dh3s, plan.weight("dout"), out_dtype=F32, name="dsa_out_dx")
    dq, dkv = plan.run("attn_bwd", attn_bwd, q, kv, o_att, lse, do_att)
    plan.grads["q"] = mm_tn(n3, dq, N_DEV, name="q_proj_dw")
    dn3 = plan.run("q_proj_dx", mm_nt, dq, plan.weight("q"), out_dtype=F32, name="q_proj_dx")
    plan.grads["kv"] = plan.run("kv_proj_dw", mm_tn, kvn, dkv, N_DEV, name="kv_proj_dw")
    dh2s, (d_kvnorm, d_attn1) = plan.run("kv_proj_dx", mm_nt, dkv, plan.weight("kv"), out_dtype=F32, name="kv_proj_dx",
                                         norm=(h2s, dh3s, [row(wts["kv_norm"]), row(attn_norm[1])], [dn3]))
    dh2 = from_streams(dh2s, name="dh2_from_streams")
    dh1, d_ffn0, dcw0, dcb0 = _ffn_bwd(plan, dh2, h1, ffn0, row(ffn_norm[0]), conv_w[0], row(conv_b[0]), "0")
    plan.grads["gout"] = _by_rows(plan.run("gla_out_dw", mm_tn, og, dh1, 1, name="gla_out_dw"))
    dog = plan.run("gla_out_dx", mm_nt, dh1, plan.weight("gout"), out_dtype=F32, name="gla_out_dx")
    do_gla, dr, d_hn = headnorm_bwd(o_gla, proj, wts["gla_head_norm"], dog)
    dq_g, dk_g, dv_g, dla = plan.run("gla_bwd", gla_bwd, proj, la, states, do_gla)
    da, dw_a2p, db_a2 = gate_bwd(proj, wts["gla_w_a2"], wts["gla_b_a2"], dla)
    dproj = jnp.concatenate([dq_g, dk_g, dv_g, dr, da], axis=1)
    assert dproj.shape[1] == GLA_IN_PAD
    dw_in = plan.run("gla_in_dw", mm_tn, n1, dproj, 1, name="gla_in_dw")
    plan.grads["in"] = dw_in[0, :, :GLA_IN_DIM].reshape(D_MODEL, N_DEV, GLA_IN_DIM // N_DEV).transpose(1, 0, 2)
    grad_x, (d_attn0,) = plan.run("gla_in_dx", mm_nt, dproj, wts["gla_w_in"], out_dtype=F32, name="gla_in_dx",
                                  norm=(x, dh1, [row(attn_norm[0])], []))

    small = dict(
        attn_norm=jnp.concatenate([d_attn0, d_attn1], axis=0),
        ffn_norm=jnp.concatenate([d_ffn0, d_ffn1], axis=0),
        kv_norm=d_kvnorm.reshape(-1),
        final_norm=d_final.reshape(-1),
        ffn_conv_b=jnp.concatenate([dcb0, dcb1], axis=0),
        gla_w_a2=dw_a2p[:GATE_RANK],
        gla_b_a2=db_a2,
        gla_head_norm=d_hn,
        ffn_conv_w=jnp.stack([dcw0, dcw1]),
    )
    return loss_tile, grad_x, small


SMALL_ORDER = ("attn_norm", "ffn_norm", "kv_norm", "final_norm", "ffn_conv_b",
               "gla_w_a2", "gla_b_a2", "gla_head_norm", "ffn_conv_w")
SMALL_FULL = dict(attn_norm=(2, D_MODEL), ffn_norm=(2, D_MODEL), kv_norm=(D_MODEL,), final_norm=(D_MODEL,),
                  ffn_conv_b=(2, D_FF), gla_w_a2=(GATE_RANK, GLA_KEY_DIM), gla_b_a2=(1, GLA_KEY_DIM),
                  gla_head_norm=(1, GLA_DV), ffn_conv_w=(2, 3, D_FF))
SMALL_SHARDED = ("gla_w_a2", "gla_b_a2", "gla_head_norm", "ffn_conv_w")
SMALL_GRAD_ROWS = 592
SMALL_ADAM_ROWS = 240


def kernel(x, attn_norm, gla_w_in, gla_w_a2, gla_b_a2, gla_head_norm, gla_w_out, kv_norm, w_kv, dsa_w_q, dsa_w_out, ffn_norm, ffn_w_up, ffn_conv_w, ffn_conv_b, ffn_w_down, final_norm, loss_target, m_attn_norm, m_gla_w_in, m_gla_w_a2, m_gla_b_a2, m_gla_head_norm, m_gla_w_out, m_kv_norm, m_w_kv, m_dsa_w_q, m_dsa_w_out, m_ffn_norm, m_ffn_w_up, m_ffn_conv_w, m_ffn_conv_b, m_ffn_w_down, m_final_norm, v_attn_norm, v_gla_w_in, v_gla_w_a2, v_gla_b_a2, v_gla_head_norm, v_gla_w_out, v_kv_norm, v_w_kv, v_dsa_w_q, v_dsa_w_out, v_ffn_norm, v_ffn_w_up, v_ffn_conv_w, v_ffn_conv_b, v_ffn_w_down, v_final_norm):
    me = 4 * lax.axis_index("x") + 2 * lax.axis_index("y") + lax.axis_index("c")
    bf = lambda a: a.astype(BF16)

    g_in, g_small = all_gather([bf(gla_w_in[0]), _pack_small_weights(gla_w_a2, gla_b_a2, gla_head_norm, ffn_conv_w)],
                               name="gather_first")
    w_a2_full, b_a2_full, hn_full, conv_w_full = _unpack_small_weights(g_small)
    w_in_full = jnp.pad(g_in.transpose(1, 0, 2).reshape(D_MODEL, GLA_IN_DIM), ((0, 0), (0, GLA_IN_PAD - GLA_IN_DIM)))
    wts = dict(
        attn_norm=attn_norm, ffn_norm=ffn_norm, kv_norm=kv_norm, final_norm=final_norm, ffn_conv_b=ffn_conv_b,
        gla_w_in=w_in_full[None],
        gla_w_a2=jnp.pad(bf(w_a2_full), ((0, LANE - GATE_RANK), (0, 0))),
        gla_b_a2=b_a2_full, gla_head_norm=hn_full, ffn_conv_w=conv_w_full,
    )
    plan = Plan({}, srcs=dict(gout=bf(gla_w_out[0]), kv=bf(w_kv), q=bf(dsa_w_q[0]), dout=bf(dsa_w_out[0]),
                              up0=bf(ffn_w_up[0]), up1=bf(ffn_w_up[1]), dn0=bf(ffn_w_down[0]), dn1=bf(ffn_w_down[1])))

    loss_tile, grad_x, small = local_step(x[0], loss_target[0], wts, plan)
    loss = lax.psum(loss_tile[0, 0], ("x", "y", "c"))

    plan.run("grads_tail", exchange_only, name="grads_tail")
    shard3 = lambda a: a.reshape((-1,) + a.shape[-2:])
    big_params = dict(gla_w_in=(("in",), gla_w_in, m_gla_w_in, v_gla_w_in),
                      gla_w_out=(("gout",), gla_w_out, m_gla_w_out, v_gla_w_out),
                      w_kv=(("kv",), w_kv, m_w_kv, v_w_kv),
                      dsa_w_q=(("q",), dsa_w_q, m_dsa_w_q, v_dsa_w_q),
                      dsa_w_out=(("dout",), dsa_w_out, m_dsa_w_out, v_dsa_w_out),
                      ffn_w_up=(("up0", "up1"), ffn_w_up, m_ffn_w_up, v_ffn_w_up),
                      ffn_w_down=(("dn0", "dn1"), ffn_w_down, m_ffn_w_down, v_ffn_w_down))
    res = {}
    for nm, (parts, w, m, v) in big_params.items():
        outs = adam_sharded([plan.recv[p] for p in parts], shard3(w), shard3(m), shard3(v), name=f"adam_{nm}")
        res[nm] = [o.reshape(w.shape) for o in outs]

    packed = _pack_rows([small[nm] for nm in SMALL_ORDER], SMALL_GRAD_ROWS)
    (parts,) = all_gather([packed], name="gather_small_grads")
    full = dict(zip(SMALL_ORDER, _unpack_rows(sum_partials(parts), [SMALL_FULL[nm] for nm in SMALL_ORDER])))
    local_w = dict(attn_norm=attn_norm, ffn_norm=ffn_norm, kv_norm=kv_norm, final_norm=final_norm,
                   ffn_conv_b=ffn_conv_b, gla_w_a2=gla_w_a2, gla_b_a2=gla_b_a2, gla_head_norm=gla_head_norm,
                   ffn_conv_w=ffn_conv_w)
    local_m = dict(attn_norm=m_attn_norm, ffn_norm=m_ffn_norm, kv_norm=m_kv_norm, final_norm=m_final_norm,
                   ffn_conv_b=m_ffn_conv_b, gla_w_a2=m_gla_w_a2, gla_b_a2=m_gla_b_a2, gla_head_norm=m_gla_head_norm,
                   ffn_conv_w=m_ffn_conv_w)
    local_v = dict(attn_norm=v_attn_norm, ffn_norm=v_ffn_norm, kv_norm=v_kv_norm, final_norm=v_final_norm,
                   ffn_conv_b=v_ffn_conv_b, gla_w_a2=v_gla_w_a2, gla_b_a2=v_gla_b_a2, gla_head_norm=v_gla_head_norm,
                   ffn_conv_w=v_ffn_conv_w)
    local_g = {}
    for nm in SMALL_ORDER:
        gfull = full[nm]
        if nm in SMALL_SHARDED:
            per = gfull.shape[-1] // N_DEV
            gfull = lax.dynamic_slice_in_dim(gfull, me * per, per, axis=gfull.ndim - 1)
        local_g[nm] = gfull.reshape(local_w[nm].shape)
    shapes = [local_w[nm].shape for nm in SMALL_ORDER]
    pk = lambda dd: _pack_rows([dd[nm] for nm in SMALL_ORDER], SMALL_ADAM_ROWS)
    d_p, m_p, v_p = adam_packed(pk(local_w), pk(local_g), pk(local_m), pk(local_v))
    for nm, dl, mn, vn in zip(SMALL_ORDER, _unpack_rows(d_p, shapes), _unpack_rows(m_p, shapes),
                              _unpack_rows(v_p, shapes)):
        res[nm] = [local_g[nm], dl, mn, vn]

    order = ("attn_norm", "gla_w_in", "gla_w_a2", "gla_b_a2", "gla_head_norm", "gla_w_out", "kv_norm", "w_kv",
             "dsa_w_q", "dsa_w_out", "ffn_norm", "ffn_w_up", "ffn_conv_w", "ffn_conv_b", "ffn_w_down", "final_norm")
    outs = [loss, grad_x[None]]
    for kind in range(4):
        outs.extend(res[nm][kind] for nm in order)
    return tuple(outs)
```

```python
import functools

import jax
import jax.numpy as jnp
from jax import lax
from jax.experimental import pallas as pl
from jax.experimental.pallas import tpu as pltpu

F32 = jnp.float32
BF16 = jnp.bfloat16
MESH = pl.DeviceIdType.MESH
ANY = pl.BlockSpec(memory_space=pl.ANY)

N_DEV = 8
D_MODEL = 2048
GLA_HEADS = 4
GLA_KEY_DIM = 1024
GLA_VAL_DIM = 2048
GLA_DK = 256
GLA_DV = 512
GATE_RANK = 16
GATE_NORMALIZER = 16.0
GLA_CHUNK = 64
GLA_STEP_CHUNKS = 2
GLA_IN_DIM = 2 * GLA_KEY_DIM + 2 * GLA_VAL_DIM + GATE_RANK
GLA_IN_PAD = 6272
ATT_HEADS = 16
HEAD_DIM = 128
DILATIONS = (1, 4, 16)
STREAMS = DILATIONS[-1]
ATT_BLOCK = 128
D_FF = 5632
EPS = 1e-6
ADAM_LR = 0.001
ADAM_B1 = 0.9
ADAM_B2 = 0.999
ADAM_EPS = 1e-08
ADAM_WD = 0.01
ADAM_STEP = 10
NEG = -1e30
LANE = 128
NORM_ROWS = 64
VMEM_LIMIT = 52 * 1024 * 1024
ALIBI_SLOPES = tuple(2.0 ** (-0.5 * (i + 1)) for i in range(ATT_HEADS))


def _params(*sem):
    return pltpu.CompilerParams(dimension_semantics=sem, vmem_limit_bytes=VMEM_LIMIT)


def _tile(n, cap):
    best = None
    for t in range(LANE, min(n, cap) + 1, LANE):
        if n % t == 0:
            best = t
    return best if best is not None else n


def _shard_group(j, ns, cap):
    best = 1
    for g in range(1, j + 1):
        if j % g == 0 and g * ns <= cap:
            best = g
    return best


def _rows(r, c, budget=256 * 1024):
    best = None
    for t in range(16, r + 1, 16):
        if r % t == 0 and t * c <= budget:
            best = t
    return best if best is not None else r


def _flip(coord, bit):
    return 1 - coord if bit else coord


def _place():
    x, y, c = lax.axis_index("x"), lax.axis_index("y"), lax.axis_index("c")
    return x, y, c, 4 * x + 2 * y + c


def _rows_of(ref, rows):
    return ref if rows is None else ref.at[pl.ds(rows[0], rows[1] - rows[0])]


class Jobs:
    def __init__(self):
        self.srcs = []
        self.bufs = []
        self.sems = []
        self.steps = []

    def _src(self, a):
        for i, b in enumerate(self.srcs):
            if b is a:
                return i
        self.srcs.append(a)
        return len(self.srcs) - 1

    def new(self, shape, dtype):
        self.bufs.append((None, jax.ShapeDtypeStruct(shape, dtype)))
        return len(self.bufs) - 1

    def thru(self, a):
        self.bufs.append((a, jax.ShapeDtypeStruct(a.shape, a.dtype)))
        return len(self.bufs) - 1

    def _sem(self, n):
        self.sems.append(pltpu.SemaphoreType.DMA((n,)))
        return len(self.sems) - 1

    def gather_ici(self, src, buf, rows=None):
        si, send, recv, loc = self._src(src), self._sem(4), self._sem(4), self._sem(1)

        def remote(srcs, bufs, sems, slot_of):
            x, y, c, me = _place()
            peers = [(x, y, 1 - c), (1 - x, y, c), (x, 1 - y, c), (1 - x, 1 - y, c)]
            return [pltpu.make_async_remote_copy(
                src_ref=_rows_of(srcs[si], rows),
                dst_ref=_rows_of(bufs[buf].at[me if slot_of == "mine" else 4 * p[0] + 2 * p[1] + p[2]], rows),
                send_sem=sems[send].at[k], recv_sem=sems[recv].at[k], device_id=p, device_id_type=MESH)
                for k, p in enumerate(peers)]

        def local(srcs, bufs, sems):
            return pltpu.make_async_copy(_rows_of(srcs[si], rows), _rows_of(bufs[buf].at[_place()[3]], rows),
                                         sems[loc].at[0])

        def start(srcs, bufs, sems):
            local(srcs, bufs, sems).start()
            for cp in remote(srcs, bufs, sems, "mine"):
                cp.start()

        def finish(srcs, bufs, sems):
            for cp in remote(srcs, bufs, sems, "peer"):
                cp.wait_recv()
            for cp in remote(srcs, bufs, sems, "mine"):
                cp.wait_send()
            local(srcs, bufs, sems).wait()

        self.steps.append((start, finish))

    def gather_d2d(self, buf, rows=None):
        send, recv = self._sem(3), self._sem(3)

        def copies(bufs, sems, core):
            x, y, c, _ = _place()
            cc = c if core == "mine" else 1 - c
            chips = [(1 - x, y), (x, 1 - y), (1 - x, 1 - y)]
            return [pltpu.make_async_remote_copy(
                src_ref=_rows_of(bufs[buf].at[4 * px + 2 * py + cc], rows),
                dst_ref=_rows_of(bufs[buf].at[4 * px + 2 * py + cc], rows),
                send_sem=sems[send].at[k], recv_sem=sems[recv].at[k],
                device_id=(x, y, 1 - c), device_id_type=MESH) for k, (px, py) in enumerate(chips)]

        def start(srcs, bufs, sems):
            for cp in copies(bufs, sems, "mine"):
                cp.start()

        def finish(srcs, bufs, sems):
            for cp in copies(bufs, sems, "sibling"):
                cp.wait_recv()
            for cp in copies(bufs, sems, "mine"):
                cp.wait_send()

        self.steps.append((start, finish))

    def scatter(self, src, buf, rows=None):
        si, send, recv, loc = self._src(src), self._sem(N_DEV - 1), self._sem(N_DEV - 1), self._sem(1)

        def remote(srcs, bufs, sems, slot_of):
            x, y, c, me = _place()
            out = []
            for k in range(1, N_DEV):
                px, py, pc = _flip(x, k >> 2), _flip(y, (k >> 1) & 1), _flip(c, k & 1)
                peer = 4 * px + 2 * py + pc
                out.append(pltpu.make_async_remote_copy(
                    src_ref=_rows_of(srcs[si].at[peer], rows),
                    dst_ref=_rows_of(bufs[buf].at[me if slot_of == "mine" else peer], rows),
                    send_sem=sems[send].at[k - 1], recv_sem=sems[recv].at[k - 1],
                    device_id=(px, py, pc), device_id_type=MESH))
            return out

        def local(srcs, bufs, sems):
            me = _place()[3]
            return pltpu.make_async_copy(_rows_of(srcs[si].at[me], rows), _rows_of(bufs[buf].at[me], rows),
                                         sems[loc].at[0])

        def start(srcs, bufs, sems):
            local(srcs, bufs, sems).start()
            for cp in remote(srcs, bufs, sems, "mine"):
                cp.start()

        def finish(srcs, bufs, sems):
            for cp in remote(srcs, bufs, sems, "peer"):
                cp.wait_recv()
            for cp in remote(srcs, bufs, sems, "mine"):
                cp.wait_send()
            local(srcs, bufs, sems).wait()

        self.steps.append((start, finish))


def _call(body, *, name, grid, in_specs, out_specs, out_shape, args, sem, scratch_shapes=(), jobs=None):
    in_specs, out_specs, out_shape = list(in_specs), list(out_specs), list(out_shape)
    scratch_shapes = list(scratch_shapes)
    if jobs is None:
        res = pl.pallas_call(body, name=name, out_shape=out_shape, grid=grid, in_specs=in_specs,
                             out_specs=out_specs, scratch_shapes=scratch_shapes,
                             compiler_params=_params(*sem))(*args)
        return list(res), []
    thru = [a for a, _ in jobs.bufs if a is not None]
    n_in, n_src, n_thru = len(args), len(jobs.srcs), len(thru)
    n_out, n_buf, n_scr = len(out_shape), len(jobs.bufs), len(scratch_shapes)
    aliases, t = {}, 0
    for b, (a, _) in enumerate(jobs.bufs):
        if a is not None:
            aliases[n_in + n_src + t] = n_out + b
            t += 1

    def wrapped(*refs):
        at = 0
        ins = refs[at:at + n_in]; at += n_in
        srcs = refs[at:at + n_src]; at += n_src + n_thru
        outs = refs[at:at + n_out]; at += n_out
        bufs = refs[at:at + n_buf]; at += n_buf
        scr = refs[at:at + n_scr]; at += n_scr
        sems = refs[at:]
        first, last = None, None
        for axis, size in enumerate(grid):
            pid = pl.program_id(axis)
            f, l = pid == 0, pid == size - 1
            first = f if first is None else first & f
            last = l if last is None else last & l

        @pl.when(first)
        def _():
            for start, _ in jobs.steps:
                start(srcs, bufs, sems)

        body(*ins, *outs, *scr)

        @pl.when(last)
        def _():
            for _, finish in jobs.steps:
                finish(srcs, bufs, sems)

    res = pl.pallas_call(
        wrapped, name=name,
        out_shape=out_shape + [s for _, s in jobs.bufs],
        grid=grid,
        in_specs=in_specs + [ANY] * (n_src + n_thru),
        out_specs=out_specs + [ANY] * n_buf,
        scratch_shapes=scratch_shapes + jobs.sems,
        input_output_aliases=aliases,
        compiler_params=_params(*(["arbitrary"] * len(grid))),
    )(*args, *jobs.srcs, *thru)
    return res[:n_out], res[n_out:]


def mm_nn(a, w, *, out_dtype, name, res=None, tm=None, jobs=None):
    m, k = a.shape
    j, k2, ns = w.shape
    whole = j == 1 and ns <= 2048 and k <= 2048
    tm = tm or (1024 if a.dtype == BF16 and not whole else 512)
    assert k == k2 and m % tm == 0
    tn = ns if whole else _tile(ns, 1408)
    nsub = ns // tn
    tk = k if k <= 2048 else _tile(k, 1408)
    nk = k // tk
    has_res = res is not None

    def body(*refs):
        if has_res:
            a_ref, w_ref, r_ref, o_ref, acc = refs
        else:
            a_ref, w_ref, o_ref, acc = refs
        kk = pl.program_id(2)

        @pl.when(kk == 0)
        def _():
            acc[...] = jnp.zeros_like(acc)

        acc[...] += jnp.dot(a_ref[...].astype(BF16), w_ref[...], preferred_element_type=F32)

        @pl.when(kk == nk - 1)
        def _():
            r = acc[...]
            if has_res:
                r = r + r_ref[...]
            o_ref[...] = r.astype(out_dtype)

    in_specs = [
        pl.BlockSpec((tm, tk), lambda i, n, kk: (i, kk)),
        pl.BlockSpec((None, tk, tn), lambda i, n, kk: (n // nsub, kk, n % nsub)),
    ]
    args = [a, w]
    out_tile = pl.BlockSpec((tm, tn), lambda i, n, kk: (i, n))
    if has_res:
        in_specs.append(out_tile)
        args.append(res)
    (out,), bufs = _call(
        body, name=name, jobs=jobs,
        out_shape=[jax.ShapeDtypeStruct((m, j * ns), out_dtype)],
        grid=(m // tm, j * nsub, nk),
        in_specs=in_specs,
        out_specs=[out_tile],
        scratch_shapes=[pltpu.VMEM((tm, tn), F32)],
        args=args, sem=("parallel", "parallel", "arbitrary"))
    return out if jobs is None else (out, bufs)


def mm_nt(dy, w, *, out_dtype, name, tm=None, jobs=None, norm=None):
    parts, m, n = (1,) + dy.shape if dy.ndim == 2 else dy.shape
    n *= parts
    j, k, ns = w.shape
    if norm is not None:
        x, dres, gains, more = norm
        tm = tm or (256 if more else 512)
    tm = tm or 1024
    assert n == j * ns and m % tm == 0
    tn = _tile(ns, 2048)
    nsub = ns // tn
    jb = _shard_group(j // parts, ns, 2048 if norm is None else 1024) if nsub == 1 else 1
    tko = _tile(k, 1408) if norm is None else k
    nn = j * nsub // jb
    per_part = nn // parts
    if dy.ndim == 2:
        dy_spec = pl.BlockSpec((tm, jb * tn), lambda i, ko, nq: (i, nq))
    else:
        dy_spec = pl.BlockSpec((None, tm, jb * tn), lambda i, ko, nq: (nq // per_part, i, nq % per_part))
    if jb == 1:
        w_spec = pl.BlockSpec((None, tko, tn), lambda i, ko, nq: (nq // nsub, ko, nq % nsub))
    else:
        w_spec = pl.BlockSpec((jb, tko, ns), lambda i, ko, nq: (nq, ko, 0))

    n_gain = 0 if norm is None else len(gains)
    n_more = 0 if norm is None else len(more)

    def body(*refs):
        a_ref, w_ref = refs[:2]
        acc = refs[-1]
        nq = pl.program_id(2)
        first = pl.program_id(0) == 0

        @pl.when(nq == 0)
        def _():
            acc[...] = jnp.zeros_like(acc)

        if jb == 1:
            acc[...] += lax.dot_general(a_ref[...].astype(BF16), w_ref[...], (((1,), (1,)), ((), ())),
                                        preferred_element_type=F32)
        else:
            part = acc[...]
            for jj in range(jb):
                part = part + lax.dot_general(a_ref[:, jj * ns:(jj + 1) * ns].astype(BF16), w_ref[jj],
                                              (((1,), (1,)), ((), ())), preferred_element_type=F32)
            acc[...] = part

        @pl.when(nq == nn - 1)
        def _():
            if norm is None:
                refs[2][...] = acc[...].astype(out_dtype)
                return
            x_ref, r_ref = refs[2:4]
            g_refs = refs[4:4 + n_gain]
            e_refs = refs[4 + n_gain:4 + n_gain + n_more]
            dx_ref = refs[4 + n_gain + n_more]
            dg_refs = refs[5 + n_gain + n_more:-1]

            @pl.when(first)
            def _():
                for dg_ref in dg_refs:
                    dg_ref[...] = jnp.zeros_like(dg_ref)

            def rows(c, carry):
                sl = pl.ds(pl.multiple_of(c * NORM_ROWS, NORM_ROWS), NORM_ROWS)
                xv = x_ref[sl, :]
                r = lax.rsqrt(jnp.mean(xv * xv, axis=-1, keepdims=True) + EPS)
                xh = xv * r
                out = r_ref[sl, :]
                for idx, (g_ref, dg_ref) in enumerate(zip(g_refs, dg_refs)):
                    dyv = acc[sl, :] if idx == 0 else e_refs[idx - 1][sl, :].astype(F32)
                    dg_ref[...] += jnp.sum(dyv * xh, axis=0, keepdims=True)
                    dxh = dyv * g_ref[...]
                    out = out + r * (dxh - xh * jnp.mean(dxh * xh, axis=-1, keepdims=True))
                dx_ref[sl, :] = out
                return carry

            lax.fori_loop(0, tm // NORM_ROWS, rows, 0)

    out_tile = pl.BlockSpec((tm, tko), lambda i, ko, nq: (i, ko))
    in_specs, args = [dy_spec, w_spec], [dy, w]
    out_shape, out_specs = [jax.ShapeDtypeStruct((m, k), out_dtype)], [out_tile]
    sem = ("parallel", "parallel", "arbitrary")
    if norm is not None:
        vec = pl.BlockSpec((1, k), lambda i, ko, nq: (0, 0))
        in_specs += [out_tile, out_tile] + [vec] * n_gain + [out_tile] * n_more
        args += [x, dres] + list(gains) + list(more)
        out_shape = [jax.ShapeDtypeStruct((m, k), F32)] + [jax.ShapeDtypeStruct((1, k), F32)] * n_gain
        out_specs = [out_tile] + [vec] * n_gain
        sem = ("arbitrary", "arbitrary", "arbitrary")
    outs, bufs = _call(
        body, name=name, jobs=jobs, out_shape=out_shape, grid=(m // tm, k // tko, nn),
        in_specs=in_specs, out_specs=out_specs, scratch_shapes=[pltpu.VMEM((tm, tko), F32)], args=args, sem=sem)
    out = outs[0] if norm is None else (outs[0], outs[1:])
    return out if jobs is None else (out, bufs)


def mm_tn(x, dy, j, *, name, tm=1024, jobs=None):
    m, k = x.shape
    parts, m2, n = (1,) + dy.shape if dy.ndim == 2 else dy.shape
    n *= parts
    assert m == m2 and n % j == 0 and m % tm == 0
    ns = n // j
    tn = _tile(ns, 1408)
    nsub = ns // tn
    jb = _shard_group(j // parts, ns, 1536) if nsub == 1 else 1
    tk = _tile(k, 1408)
    nm = m // tm
    n_steps = j * nsub // jb
    per_part = n_steps // parts
    if dy.ndim == 2:
        dy_spec = pl.BlockSpec((tm, jb * tn), lambda kq, nq, mi: (mi, nq))
    else:
        dy_spec = pl.BlockSpec((None, tm, jb * tn), lambda kq, nq, mi: (nq // per_part, mi, nq % per_part))
    if jb == 1:
        out_spec = pl.BlockSpec((None, tk, tn), lambda kq, nq, mi: (nq // nsub, kq, nq % nsub))
        acc_shape = (tk, tn)
    else:
        out_spec = pl.BlockSpec((jb, tk, ns), lambda kq, nq, mi: (nq, kq, 0))
        acc_shape = (jb, tk, ns)

    def body(x_ref, dy_ref, o_ref, acc):
        mi = pl.program_id(2)

        @pl.when(mi == 0)
        def _():
            acc[...] = jnp.zeros_like(acc)

        xb = x_ref[...].astype(BF16)
        if jb == 1:
            acc[...] += lax.dot_general(xb, dy_ref[...].astype(BF16), (((0,), (0,)), ((), ())),
                                        preferred_element_type=F32)
        else:
            for jj in range(jb):
                acc[jj] += lax.dot_general(xb, dy_ref[:, jj * ns:(jj + 1) * ns].astype(BF16),
                                           (((0,), (0,)), ((), ())), preferred_element_type=F32)

        @pl.when(mi == nm - 1)
        def _():
            o_ref[...] = acc[...].astype(BF16)

    (out,), bufs = _call(
        body, name=name, jobs=jobs,
        out_shape=[jax.ShapeDtypeStruct((j, k, ns), BF16)],
        grid=(k // tk, n_steps, nm),
        in_specs=[
            pl.BlockSpec((tm, tk), lambda kq, nq, mi: (mi, kq)),
            dy_spec,
        ],
        out_specs=[out_spec],
        scratch_shapes=[pltpu.VMEM(acc_shape, F32)],
        args=[x, dy], sem=("parallel", "parallel", "arbitrary"))
    return out if jobs is None else (out, bufs)


STREAM_TC = LANE


def to_streams(x, *, name):
    s, c = x.shape
    per = s // STREAMS

    def body(x_ref, o_ref):
        for st in range(STREAMS):
            o_ref[pl.ds(st * per, per), :] = x_ref[pl.ds(st, per, stride=STREAMS), :]

    blk = pl.BlockSpec((s, STREAM_TC), lambda i: (0, i))
    return pl.pallas_call(body, name=name, out_shape=jax.ShapeDtypeStruct((s, c), x.dtype), grid=(c // STREAM_TC,),
                          in_specs=[blk], out_specs=blk, compiler_params=_params("parallel"))(x)


def from_streams(x, *, name):
    s, c = x.shape
    per = s // STREAMS

    def body(x_ref, o_ref):
        for st in range(STREAMS):
            o_ref[pl.ds(st, per, stride=STREAMS), :] = x_ref[pl.ds(st * per, per), :]

    blk = pl.BlockSpec((s, STREAM_TC), lambda i: (0, i))
    return pl.pallas_call(body, name=name, out_shape=jax.ShapeDtypeStruct((s, c), x.dtype), grid=(c // STREAM_TC,),
                          in_specs=[blk], out_specs=blk, compiler_params=_params("parallel"))(x)


def rms_fwd(x, gains, *, name, ts=512):
    s, d = x.shape
    n = len(gains)

    def body(x_ref, *refs):
        xv = x_ref[...]
        xh = xv * lax.rsqrt(jnp.mean(xv * xv, axis=-1, keepdims=True) + EPS)
        for g_ref, o_ref in zip(refs[:n], refs[n:]):
            o_ref[...] = (xh * g_ref[...]).astype(BF16)

    row = pl.BlockSpec((ts, d), lambda i: (i, 0))
    vec = pl.BlockSpec((1, d), lambda i: (0, 0))
    return pl.pallas_call(
        body,
        name=name,
        out_shape=[jax.ShapeDtypeStruct((s, d), BF16)] * n,
        grid=(s // ts,),
        in_specs=[row] + [vec] * n,
        out_specs=[row] * n,
        compiler_params=_params("parallel"),
    )(x, *gains)


def loss_head(h, gain, target, *, ts=256):
    s, d = h.shape

    def body(h_ref, g_ref, t_ref, l_ref, dh_ref, dg_ref):
        i = pl.program_id(0)

        @pl.when(i == 0)
        def _():
            l_ref[...] = jnp.zeros_like(l_ref)
            dg_ref[...] = jnp.zeros_like(dg_ref)

        xv = h_ref[...]
        r = lax.rsqrt(jnp.mean(xv * xv, axis=-1, keepdims=True) + EPS)
        xh = xv * r
        g = g_ref[...]
        err = xh * g - t_ref[...]
        l_ref[...] += 0.5 * jnp.sum(jnp.mean(err * err, axis=-1, keepdims=True))
        dy = err * (1.0 / d)
        dg_ref[...] += jnp.sum(dy * xh, axis=0, keepdims=True)
        dxh = dy * g
        dh_ref[...] = r * (dxh - xh * jnp.mean(dxh * xh, axis=-1, keepdims=True))

    row = pl.BlockSpec((ts, d), lambda i: (i, 0))
    vec = pl.BlockSpec((1, d), lambda i: (0, 0))
    return pl.pallas_call(
        body,
        name="loss_head",
        out_shape=[jax.ShapeDtypeStruct((8, LANE), F32), jax.ShapeDtypeStruct((s, d), F32),
                   jax.ShapeDtypeStruct((1, d), F32)],
        grid=(s // ts,),
        in_specs=[row, vec, row],
        out_specs=[pl.BlockSpec((8, LANE), lambda i: (0, 0)), row, vec],
        compiler_params=_params("arbitrary"),
    )(h, gain, target)


A_BLOCK = (2 * GLA_KEY_DIM + 2 * GLA_VAL_DIM) // LANE


def gate_fwd(proj, w_a2p, b_a2, *, ts=512):
    s = proj.shape[0]

    def body(a_ref, w_ref, b_ref, o_ref):
        z = jnp.dot(a_ref[...].astype(BF16), w_ref[...], preferred_element_type=F32) + b_ref[...]
        o_ref[...] = (jnp.minimum(z, 0.0) - jnp.log(1.0 + jnp.exp(-jnp.abs(z)))) * (1.0 / GATE_NORMALIZER)

    return pl.pallas_call(
        body,
        name="gate_fwd",
        out_shape=jax.ShapeDtypeStruct((s, GLA_KEY_DIM), F32),
        grid=(s // ts,),
        in_specs=[pl.BlockSpec((ts, LANE), lambda i: (i, A_BLOCK)),
                  pl.BlockSpec((LANE, GLA_KEY_DIM), lambda i: (0, 0)),
                  pl.BlockSpec((1, GLA_KEY_DIM), lambda i: (0, 0))],
        out_specs=pl.BlockSpec((ts, GLA_KEY_DIM), lambda i: (i, 0)),
        compiler_params=_params("parallel"),
    )(proj, w_a2p, b_a2)


def gate_bwd(proj, w_a2p, b_a2, dla, *, ts=512):
    s = proj.shape[0]

    def body(a_ref, w_ref, b_ref, dla_ref, da_ref, dw_ref, db_ref):
        i = pl.program_id(0)

        @pl.when(i == 0)
        def _():
            dw_ref[...] = jnp.zeros_like(dw_ref)
            db_ref[...] = jnp.zeros_like(db_ref)

        a = a_ref[...].astype(BF16)
        w = w_ref[...]
        z = jnp.dot(a, w, preferred_element_type=F32) + b_ref[...]
        dz = dla_ref[...] * (1.0 / GATE_NORMALIZER) / (1.0 + jnp.exp(z))
        dzb = dz.astype(BF16)
        da_ref[...] = lax.dot_general(dzb, w, (((1,), (1,)), ((), ())), preferred_element_type=F32).astype(BF16)
        dw_ref[...] += lax.dot_general(a, dzb, (((0,), (0,)), ((), ())), preferred_element_type=F32)
        db_ref[...] += jnp.sum(dz, axis=0, keepdims=True)

    return pl.pallas_call(
        body,
        name="gate_bwd",
        out_shape=[jax.ShapeDtypeStruct((s, LANE), BF16), jax.ShapeDtypeStruct((LANE, GLA_KEY_DIM), F32),
                   jax.ShapeDtypeStruct((1, GLA_KEY_DIM), F32)],
        grid=(s // ts,),
        in_specs=[pl.BlockSpec((ts, LANE), lambda i: (i, A_BLOCK)),
                  pl.BlockSpec((LANE, GLA_KEY_DIM), lambda i: (0, 0)),
                  pl.BlockSpec((1, GLA_KEY_DIM), lambda i: (0, 0)),
                  pl.BlockSpec((ts, GLA_KEY_DIM), lambda i: (i, 0))],
        out_specs=[pl.BlockSpec((ts, LANE), lambda i: (i, 0)),
                   pl.BlockSpec((LANE, GLA_KEY_DIM), lambda i: (0, 0)),
                   pl.BlockSpec((1, GLA_KEY_DIM), lambda i: (0, 0))],
        compiler_params=_params("arbitrary"),
    )(proj, w_a2p, b_a2, dla)


def _masked_sum(mask, x):
    m = mask.astype(BF16)
    hi = x.astype(BF16)
    rest = x - hi.astype(F32)
    mid = rest.astype(BF16)
    lo = (rest - mid.astype(F32)).astype(BF16)
    dot = lambda t: jnp.dot(m, t, preferred_element_type=F32)
    return dot(hi) + dot(mid) + dot(lo)


def _chunk_terms(q, k, la):
    c_len = GLA_CHUNK
    row = lax.broadcasted_iota(jnp.int32, (c_len, c_len), 0)
    col = lax.broadcasted_iota(jnp.int32, (c_len, c_len), 1)
    tri = row >= col
    c = _masked_sum(tri, la)
    last = jnp.sum(la, axis=0, keepdims=True)
    q_dec = q * (GLA_DK ** -0.5) * jnp.exp(c)
    k_inv = k * jnp.exp(-c)
    k_end = k * jnp.exp(last - c)
    return c, last, q_dec, k_inv, k_end, tri


def _dot(a, b, ca, cb):
    return lax.dot_general(a.astype(BF16), b.astype(BF16), (((ca,), (cb,)), ((), ())), preferred_element_type=F32)


def gla_fwd(proj, la, hn, jobs=None):
    s = proj.shape[0]
    n_chunks = s // GLA_CHUNK
    rows = GLA_CHUNK * GLA_STEP_CHUNKS

    def body(q_ref, k_ref, v_ref, r_ref, la_ref, hn_ref, o_ref, st_out, og_ref, st):
        @pl.when(pl.program_id(0) == 0)
        def _():
            st[...] = jnp.zeros_like(st)

        for h in range(GLA_HEADS):
            hk = slice(h * GLA_DK, (h + 1) * GLA_DK)
            hv = slice(h * GLA_DV, (h + 1) * GLA_DV)
            for cc in range(GLA_STEP_CHUNKS):
                rs = slice(cc * GLA_CHUNK, (cc + 1) * GLA_CHUNK)
                _, last, q_dec, k_inv, k_end, tri = _chunk_terms(q_ref[rs, hk], k_ref[rs, hk], la_ref[rs, hk])
                v = v_ref[rs, hv]
                a = jnp.where(tri, _dot(q_dec, k_inv, 1, 1), 0.0)
                state = st[h]
                st_out[h, cc] = state
                ov = _dot(a, v, 1, 0) + _dot(q_dec, state, 1, 1)
                o_ref[rs, hv] = ov
                st[h] = state * jnp.exp(last) + _dot(v, k_end, 0, 0)
                oh = ov * lax.rsqrt(jnp.mean(ov * ov, axis=-1, keepdims=True) + EPS)
                r = r_ref[rs, hv]
                og_ref[rs, hv] = (oh * hn_ref[...] * (r * jax.nn.sigmoid(r))).astype(BF16)

    key = lambda col: pl.BlockSpec((rows, GLA_KEY_DIM), lambda n: (n, col))
    val = lambda col: pl.BlockSpec((rows, GLA_VAL_DIM), lambda n: (n, col))
    outs, bufs = _call(
        body, name="gla_fwd", jobs=jobs,
        out_shape=[jax.ShapeDtypeStruct((s, GLA_VAL_DIM), F32),
                   jax.ShapeDtypeStruct((GLA_HEADS, n_chunks, GLA_DV, GLA_DK), F32),
                   jax.ShapeDtypeStruct((s, GLA_VAL_DIM), BF16)],
        grid=(n_chunks // GLA_STEP_CHUNKS,),
        in_specs=[key(0), key(1), val(1), val(R_BLOCK // GLA_HEADS), key(0), pl.BlockSpec((1, GLA_DV), lambda n: (0, 0))],
        out_specs=[val(0), pl.BlockSpec((GLA_HEADS, GLA_STEP_CHUNKS, GLA_DV, GLA_DK), lambda n: (0, n, 0, 0)), val(0)],
        scratch_shapes=[pltpu.VMEM((GLA_HEADS, GLA_DV, GLA_DK), F32)],
        args=[proj, proj, proj, proj, la, hn], sem=("arbitrary",))
    return outs if jobs is None else (outs, bufs)


def gla_bwd(proj, la, states, o, hn, dog, jobs=None):
    s = proj.shape[0]
    n_steps = s // GLA_CHUNK // GLA_STEP_CHUNKS
    lastc = n_steps - 1
    rows = GLA_CHUNK * GLA_STEP_CHUNKS

    def body(q_ref, k_ref, v_ref, r_ref, la_ref, o_ref, hn_ref, dog_ref, st_ref,
             dq_ref, dk_ref, dv_ref, dr_ref, dla_ref, dhn_ref, dst):
        @pl.when(pl.program_id(0) == 0)
        def _():
            dst[...] = jnp.zeros_like(dst)
            dhn_ref[...] = jnp.zeros_like(dhn_ref)

        upper = (lax.broadcasted_iota(jnp.int32, (GLA_CHUNK, GLA_CHUNK), 0)
                 <= lax.broadcasted_iota(jnp.int32, (GLA_CHUNK, GLA_CHUNK), 1))
        gain = hn_ref[...]
        for h in range(GLA_HEADS):
            hk = slice(h * GLA_DK, (h + 1) * GLA_DK)
            hv = slice(h * GLA_DV, (h + 1) * GLA_DV)
            for cc in reversed(range(GLA_STEP_CHUNKS)):
                rs = slice(cc * GLA_CHUNK, (cc + 1) * GLA_CHUNK)
                ov = o_ref[rs, hv]
                inv = lax.rsqrt(jnp.mean(ov * ov, axis=-1, keepdims=True) + EPS)
                oh = ov * inv
                r = r_ref[rs, hv]
                sig = jax.nn.sigmoid(r)
                dgv = dog_ref[rs, hv]
                d_on = dgv * (r * sig)
                dr_ref[rs, hv] = (dgv * (oh * gain) * (sig * (1.0 + r * (1.0 - sig)))).astype(BF16)
                dhn_ref[...] += jnp.sum(d_on * oh, axis=0, keepdims=True)
                doh = d_on * gain
                dout = inv * (doh - oh * jnp.mean(doh * oh, axis=-1, keepdims=True))
                c, last, q_dec, k_inv, k_end, tri = _chunk_terms(q_ref[rs, hk], k_ref[rs, hk], la_ref[rs, hk])
                v = v_ref[rs, hv]
                state = st_ref[h, cc]
                dstate = dst[h]
                e_last = jnp.exp(last)
                a = jnp.where(tri, _dot(q_dec, k_inv, 1, 1), 0.0)
                da = jnp.where(tri, _dot(dout, v, 1, 1), 0.0)
                dv_ref[rs, hv] = (_dot(a, dout, 0, 0) + _dot(k_end, dstate, 1, 1)).astype(BF16)
                dq_dec = _dot(da, k_inv, 1, 0) + _dot(dout, state, 1, 0)
                dk_inv = _dot(da, q_dec, 0, 0)
                dk_end = _dot(v, dstate, 1, 0)
                dst[h] = dstate * e_last + _dot(dout, q_dec, 0, 0)
                dq_ref[rs, hk] = (dq_dec * (GLA_DK ** -0.5) * jnp.exp(c)).astype(BF16)
                dk_ref[rs, hk] = (dk_inv * jnp.exp(-c) + dk_end * jnp.exp(last - c)).astype(BF16)
                ke_term = dk_end * k_end
                dc = dq_dec * q_dec - dk_inv * k_inv - ke_term
                dlast = (jnp.sum(ke_term, axis=0, keepdims=True)
                         + e_last * jnp.sum(dstate * state, axis=0, keepdims=True))
                dla_ref[rs, hk] = _masked_sum(upper, dc) + dlast

    key = lambda col: pl.BlockSpec((rows, GLA_KEY_DIM), lambda n: (lastc - n, col))
    val = lambda col: pl.BlockSpec((rows, GLA_VAL_DIM), lambda n: (lastc - n, col))
    vec = pl.BlockSpec((1, GLA_DV), lambda n: (0, 0))
    outs, bufs = _call(
        body, name="gla_bwd", jobs=jobs,
        out_shape=[jax.ShapeDtypeStruct((s, GLA_KEY_DIM), BF16), jax.ShapeDtypeStruct((s, GLA_KEY_DIM), BF16),
                   jax.ShapeDtypeStruct((s, GLA_VAL_DIM), BF16), jax.ShapeDtypeStruct((s, GLA_VAL_DIM), BF16),
                   jax.ShapeDtypeStruct((s, GLA_KEY_DIM), F32), jax.ShapeDtypeStruct((1, GLA_DV), F32)],
        grid=(n_steps,),
        in_specs=[key(0), key(1), val(1), val(R_BLOCK // GLA_HEADS), key(0), val(0), vec, val(0),
                  pl.BlockSpec((GLA_HEADS, GLA_STEP_CHUNKS, GLA_DV, GLA_DK), lambda n: (0, lastc - n, 0, 0))],
        out_specs=[key(0), key(0), val(0), val(0), key(0), vec],
        scratch_shapes=[pltpu.VMEM((GLA_HEADS, GLA_DV, GLA_DK), F32)],
        args=[proj, proj, proj, proj, la, o, hn, dog, states], sem=("arbitrary",))
    return outs if jobs is None else (outs, bufs)


R_BLOCK = (2 * GLA_KEY_DIM + GLA_VAL_DIM) // GLA_DV


CONV_TC = 128
SQRT_HALF = 0.7071067811865476
INV_SQRT_2PI = 0.3989422804014327


def _conv_gate(g_ref, cw_ref, cb_ref):
    g0 = g_ref[...].astype(F32)
    t = lax.broadcasted_iota(jnp.int32, g0.shape, 0)
    g1 = jnp.where(t >= 1, pltpu.roll(g0, 1, 0), 0.0)
    g2 = jnp.where(t >= 2, pltpu.roll(g0, 2, 0), 0.0)
    gc = cw_ref[0:1, :] * g2 + cw_ref[1:2, :] * g1 + cw_ref[2:3, :] * g0 + cb_ref[...]
    return g0, g1, g2, gc, t


def convglu_fwd(up, conv_w, conv_b, *, name, jobs=None):
    s = up.shape[0]
    nc = D_FF // CONV_TC

    def body(u_ref, g_ref, cw_ref, cb_ref, o_ref):
        _, _, _, gc, _ = _conv_gate(g_ref, cw_ref, cb_ref)
        gelu = 0.5 * gc * (1.0 + lax.erf(gc * SQRT_HALF))
        o_ref[...] = (gelu * u_ref[...].astype(F32)).astype(BF16)

    (out,), bufs = _call(
        body, name=name, jobs=jobs,
        out_shape=[jax.ShapeDtypeStruct((s, D_FF), BF16)],
        grid=(nc,),
        in_specs=[pl.BlockSpec((s, CONV_TC), lambda c: (0, c)),
                  pl.BlockSpec((s, CONV_TC), lambda c: (0, nc + c)),
                  pl.BlockSpec((3, CONV_TC), lambda c: (0, c)),
                  pl.BlockSpec((1, CONV_TC), lambda c: (0, c))],
        out_specs=[pl.BlockSpec((s, CONV_TC), lambda c: (0, c))],
        args=[up, up, conv_w, conv_b], sem=("parallel",))
    return out if jobs is None else (out, bufs)


def convglu_bwd(up, conv_w, conv_b, dact, *, name, jobs=None):
    s = up.shape[0]
    nc = D_FF // CONV_TC

    def body(u_ref, g_ref, cw_ref, cb_ref, da_ref, dup_ref, dcw_ref, dcb_ref):
        du_ref, dg_ref = dup_ref.at[0], dup_ref.at[1]
        g0, g1, g2, gc, t = _conv_gate(g_ref, cw_ref, cb_ref)
        cdf = 0.5 * (1.0 + lax.erf(gc * SQRT_HALF))
        da = da_ref[...].astype(F32)
        du_ref[...] = (da * gc * cdf).astype(BF16)
        dgc = da * u_ref[...].astype(F32) * (cdf + gc * jnp.exp(-0.5 * gc * gc) * INV_SQRT_2PI)
        dcb_ref[...] = jnp.sum(dgc, axis=0, keepdims=True)
        dcw_ref[0:1, :] = jnp.sum(dgc * g2, axis=0, keepdims=True)
        dcw_ref[1:2, :] = jnp.sum(dgc * g1, axis=0, keepdims=True)
        dcw_ref[2:3, :] = jnp.sum(dgc * g0, axis=0, keepdims=True)
        n1 = jnp.where(t < s - 1, pltpu.roll(dgc, s - 1, 0), 0.0)
        n2 = jnp.where(t < s - 2, pltpu.roll(dgc, s - 2, 0), 0.0)
        dg_ref[...] = (cw_ref[2:3, :] * dgc + cw_ref[1:2, :] * n1 + cw_ref[0:1, :] * n2).astype(BF16)

    col = pl.BlockSpec((s, CONV_TC), lambda c: (0, c))
    outs, bufs = _call(
        body, name=name, jobs=jobs,
        out_shape=[jax.ShapeDtypeStruct((2, s, D_FF), BF16),
                   jax.ShapeDtypeStruct((3, D_FF), F32), jax.ShapeDtypeStruct((1, D_FF), F32)],
        grid=(nc,),
        in_specs=[col, pl.BlockSpec((s, CONV_TC), lambda c: (0, nc + c)),
                  pl.BlockSpec((3, CONV_TC), lambda c: (0, c)),
                  pl.BlockSpec((1, CONV_TC), lambda c: (0, c)), col],
        out_specs=[pl.BlockSpec((2, s, CONV_TC), lambda c: (0, 0, c)), pl.BlockSpec((3, CONV_TC), lambda c: (0, c)),
                   pl.BlockSpec((1, CONV_TC), lambda c: (0, c))],
        args=[up, up, conv_w, conv_b, dact], sem=("parallel",))
    return outs if jobs is None else (outs, bufs)


SLOPE_TILE = (8, LANE)


def _slope_table():
    return jnp.broadcast_to(jnp.asarray(ALIBI_SLOPES, F32)[:, None, None], (ATT_HEADS,) + SLOPE_TILE)


def _pieces(s_len, d):
    npc = STREAMS // d
    lp = ATT_BLOCK // npc
    return npc, lp, (s_len // STREAMS) // lp


def _gather(ref, r, b, d, s_len):
    npc, lp, _ = _pieces(s_len, d)
    per = s_len // STREAMS
    parts = [ref[pl.ds((r + d * k) * per + b * lp, lp), :] for k in range(npc)]
    return parts[0] if npc == 1 else jnp.concatenate(parts, axis=0)


def _scatter(ref, r, b, d, s_len, val, add=False):
    npc, lp, _ = _pieces(s_len, d)
    per = s_len // STREAMS
    for k in range(npc):
        rows = pl.ds((r + d * k) * per + b * lp, lp)
        piece = val[k * lp:(k + 1) * lp]
        if add:
            ref[rows, :] += piece
        else:
            ref[rows, :] = piece


def _stream_bias(slope, d, s_len):
    npc, lp, _ = _pieces(s_len, d)
    qi = lax.broadcasted_iota(jnp.int32, (ATT_BLOCK, 2 * ATT_BLOCK), 0)
    c = lax.broadcasted_iota(jnp.int32, (ATT_BLOCK, 2 * ATT_BLOCK), 1)
    own = c // ATT_BLOCK
    cc = c - own * ATT_BLOCK
    dist = npc * ((qi % lp) - (cc % lp) + lp * (1 - own)) + (qi // lp - cc // lp)
    ok = (dist >= 0) & (dist <= ATT_BLOCK)
    return jnp.where(ok, (slope * (-float(d))) * dist.astype(F32), NEG)


def attn_fwd(q, kv, jobs=None):
    s_len = q.shape[0]
    scale = HEAD_DIM ** -0.5

    def body(sl_ref, q_ref, k_ref, v_ref, o_ref, lse_ref):
        g = pl.program_id(1)
        slope = sl_ref[0:1, 0:1]

        def branch(gi, d):
            _, _, nblk = _pieces(s_len, d)
            bias = _stream_bias(slope, d, s_len)
            for r in range(d):
                for b in range(nblk):
                    qb = _gather(q_ref, r, b, d, s_len)
                    kc, vc = _gather(k_ref, r, b, d, s_len), _gather(v_ref, r, b, d, s_len)
                    if b == 0:
                        kcat, vcat, bb = kc, vc, bias[:, ATT_BLOCK:]
                    else:
                        kcat = jnp.concatenate([_gather(k_ref, r, b - 1, d, s_len), kc], axis=0)
                        vcat = jnp.concatenate([_gather(v_ref, r, b - 1, d, s_len), vc], axis=0)
                        bb = bias
                    sc = _dot(qb, kcat, 1, 1) * scale + bb
                    m = jnp.max(sc, axis=-1, keepdims=True)
                    p = jnp.exp(sc - m)
                    l = jnp.sum(p, axis=-1, keepdims=True)
                    o_new = _dot(p, vcat, 1, 0) / l
                    lse_new = m + jnp.log(l)
                    if gi > 0:
                        lse_old = _gather(lse_ref, r, b, d, s_len)[:, 0:1]
                        top = jnp.maximum(lse_old, lse_new)
                        e_old, e_new = jnp.exp(lse_old - top), jnp.exp(lse_new - top)
                        den = e_old + e_new
                        o_new = (e_old * _gather(o_ref, r, b, d, s_len) + e_new * o_new) / den
                        lse_new = top + jnp.log(den)
                    _scatter(o_ref, r, b, d, s_len, o_new)
                    _scatter(lse_ref, r, b, d, s_len, jnp.broadcast_to(lse_new, (ATT_BLOCK, HEAD_DIM)))

        for gi, d in enumerate(DILATIONS):
            @pl.when(g == gi)
            def _():
                branch(gi, d)

    blk = lambda col: pl.BlockSpec((s_len, HEAD_DIM), lambda h, g: (0, col(h, g)))
    head = lambda h, g: h
    outs, bufs = _call(
        body, name="attn_fwd", jobs=jobs,
        out_shape=[jax.ShapeDtypeStruct((s_len, ATT_HEADS * HEAD_DIM), F32)] * 2,
        grid=(ATT_HEADS, len(DILATIONS)),
        in_specs=[pl.BlockSpec((None,) + SLOPE_TILE, lambda h, g: (h, 0, 0)),
                  blk(lambda h, g: g * ATT_HEADS + h), blk(head), blk(lambda h, g: ATT_HEADS + h)],
        out_specs=[blk(head), blk(head)],
        args=[_slope_table(), q, kv, kv], sem=("parallel", "arbitrary"))
    return outs if jobs is None else (outs, bufs)


def attn_bwd(q, kv, o, lse, do, jobs=None):
    s_len = q.shape[0]
    scale = HEAD_DIM ** -0.5
    chunks = s_len // ATT_BLOCK

    def body(sl_ref, q_ref, k_ref, v_ref, o_ref, lse_ref, do_ref, dq_ref, dkv_ref, dlt):
        g = pl.program_id(1)
        slope = sl_ref[0:1, 0:1]
        dk_ref, dv_ref = dkv_ref.at[0], dkv_ref.at[1]

        @pl.when(g == 0)
        def _():
            dkv_ref[...] = jnp.zeros_like(dkv_ref)

            def deltas(c, carry):
                rows = pl.ds(pl.multiple_of(c * ATT_BLOCK, ATT_BLOCK), ATT_BLOCK)
                dlt[rows, :] = jnp.sum(do_ref[rows, :] * o_ref[rows, :], axis=-1, keepdims=True)
                return carry
            lax.fori_loop(0, chunks, deltas, 0)

        def branch(d):
            _, _, nblk = _pieces(s_len, d)
            bias = _stream_bias(slope, d, s_len)
            for r in range(d):
                for b in range(nblk):
                    qb = _gather(q_ref, r, b, d, s_len)
                    dob = _gather(do_ref, r, b, d, s_len)
                    kc, vc = _gather(k_ref, r, b, d, s_len), _gather(v_ref, r, b, d, s_len)
                    if b == 0:
                        kcat, vcat, bb = kc, vc, bias[:, ATT_BLOCK:]
                    else:
                        kcat = jnp.concatenate([_gather(k_ref, r, b - 1, d, s_len), kc], axis=0)
                        vcat = jnp.concatenate([_gather(v_ref, r, b - 1, d, s_len), vc], axis=0)
                        bb = bias
                    sc = _dot(qb, kcat, 1, 1) * scale + bb
                    p = jnp.exp(sc - _gather(lse_ref, r, b, d, s_len)[:, 0:1])
                    ds = p * (_dot(dob, vcat, 1, 1) - _gather(dlt, r, b, d, s_len))
                    _scatter(dq_ref, r, b, d, s_len, _dot(ds, kcat, 1, 0) * scale)
                    dk = _dot(ds, qb, 0, 0) * scale
                    dv = _dot(p, dob, 0, 0)
                    if b == 0:
                        _scatter(dk_ref, r, b, d, s_len, dk, add=True)
                        _scatter(dv_ref, r, b, d, s_len, dv, add=True)
                    else:
                        _scatter(dk_ref, r, b - 1, d, s_len, dk[:ATT_BLOCK], add=True)
                        _scatter(dv_ref, r, b - 1, d, s_len, dv[:ATT_BLOCK], add=True)
                        _scatter(dk_ref, r, b, d, s_len, dk[ATT_BLOCK:], add=True)
                        _scatter(dv_ref, r, b, d, s_len, dv[ATT_BLOCK:], add=True)

        for gi, d in enumerate(DILATIONS):
            @pl.when(g == gi)
            def _():
                branch(d)

    blk = lambda col: pl.BlockSpec((s_len, HEAD_DIM), lambda h, g: (0, col(h, g)))
    head = lambda h, g: h
    q_col = lambda h, g: g * ATT_HEADS + h
    outs, bufs = _call(
        body, name="attn_bwd", jobs=jobs,
        out_shape=[jax.ShapeDtypeStruct(q.shape, F32), jax.ShapeDtypeStruct((2, s_len, ATT_HEADS * HEAD_DIM), F32)],
        grid=(ATT_HEADS, len(DILATIONS)),
        in_specs=[pl.BlockSpec((None,) + SLOPE_TILE, lambda h, g: (h, 0, 0)),
                  blk(q_col), blk(head), blk(lambda h, g: ATT_HEADS + h), blk(head), blk(head), blk(head)],
        out_specs=[blk(q_col), pl.BlockSpec((2, s_len, HEAD_DIM), lambda h, g: (0, 0, h))],
        scratch_shapes=[pltpu.VMEM((s_len, 1), F32)],
        args=[_slope_table(), q, kv, kv, o, lse, do], sem=("parallel", "arbitrary"))
    return outs if jobs is None else (outs, bufs)


def _adam(w, g, m, v):
    m = ADAM_B1 * m + (1.0 - ADAM_B1) * g
    v = ADAM_B2 * v + (1.0 - ADAM_B2) * (g * g)
    m_hat = m / (1.0 - ADAM_B1 ** ADAM_STEP)
    v_hat = v / (1.0 - ADAM_B2 ** ADAM_STEP)
    delta = -ADAM_LR * (m_hat / (jnp.sqrt(v_hat) + ADAM_EPS) + ADAM_WD * w)
    return delta, m, v


def adam_sharded(recvs, w, m, v, *, name):
    layers = len(recvs)
    n_src, r, c = recvs[0].shape
    tr = _rows(r, c)

    def body(*refs):
        p_refs = refs[:layers]
        w_ref, m_ref, v_ref, g_ref, d_ref, mo_ref, vo_ref = refs[layers:]
        for layer, p_ref in enumerate(p_refs):
            @pl.when(pl.program_id(0) == layer)
            def _():
                g = p_ref[0].astype(F32)
                for src in range(1, n_src):
                    g = g + p_ref[src].astype(F32)
                delta, m_new, v_new = _adam(w_ref[...], g, m_ref[...], v_ref[...])
                g_ref[...] = g
                d_ref[...] = delta
                mo_ref[...] = m_new
                vo_ref[...] = v_new

    blk = pl.BlockSpec((None, tr, c), lambda l, i: (l, i, 0))
    out = jax.ShapeDtypeStruct((layers, r, c), F32)
    part = [pl.BlockSpec((n_src, tr, c), functools.partial(lambda l, i, layer: (0, jnp.where(l == layer, i, 0), 0),
                                                            layer=layer)) for layer in range(layers)]
    return pl.pallas_call(
        body,
        name=name,
        out_shape=[out] * 4,
        grid=(layers, r // tr),
        in_specs=part + [blk, blk, blk],
        out_specs=[blk] * 4,
        compiler_params=_params("parallel", "parallel"),
    )(*recvs, w, m, v)


def sum_partials(parts):
    n_src, r, c = parts.shape

    def body(p_ref, o_ref):
        g = p_ref[0]
        for src in range(1, n_src):
            g = g + p_ref[src]
        o_ref[...] = g

    return pl.pallas_call(
        body,
        name="sum_small_grads",
        out_shape=jax.ShapeDtypeStruct((r, c), F32),
    )(parts)


def adam_packed(w, g, m, v):
    def body(w_ref, g_ref, m_ref, v_ref, d_ref, mo_ref, vo_ref):
        delta, m_new, v_new = _adam(w_ref[...], g_ref[...], m_ref[...], v_ref[...])
        d_ref[...] = delta
        mo_ref[...] = m_new
        vo_ref[...] = v_new

    out = jax.ShapeDtypeStruct(w.shape, F32)
    return pl.pallas_call(body, name="adam_small", out_shape=[out] * 3)(w, g, m, v)


def all_gather(srcs, *, name):
    n = len(srcs)

    def body(*refs):
        src, dst = refs[:n], refs[n:2 * n]
        send_sems, recv_sems, local_sems = refs[2 * n:]
        x, y, c, me = _place()
        sibling = (x, y, 1 - c)
        chips = [(1 - x, y), (x, 1 - y), (1 - x, 1 - y)]

        def index(px, py, pc):
            return 4 * px + 2 * py + pc

        def copy(p, k, block, to, from_src=False):
            slot = dst[p].at[index(*block)]
            return pltpu.make_async_remote_copy(
                src_ref=src[p] if from_src else slot, dst_ref=slot,
                send_sem=send_sems.at[p, k], recv_sem=recv_sems.at[p, k],
                device_id=to, device_id_type=MESH)

        mine = [pltpu.make_async_copy(src[p], dst[p].at[me], local_sems.at[p]) for p in range(n)]
        for cp in mine:
            cp.start()
        first = []
        for p in range(n):
            first.append(copy(p, 0, (x, y, c), sibling, from_src=True))
            for jj, chip in enumerate(chips):
                first.append(copy(p, 1 + jj, (x, y, c), (*chip, c), from_src=True))
        for cp in first:
            cp.start()
        passed = []
        for jj, chip in enumerate(chips):
            for p in range(n):
                copy(p, 1 + jj, (*chip, c), (x, y, c)).wait_recv()
                fwd = copy(p, 4 + jj, (*chip, c), sibling)
                fwd.start()
                passed.append(fwd)
        for p in range(n):
            copy(p, 0, sibling, (x, y, c)).wait_recv()
            for jj, chip in enumerate(chips):
                copy(p, 4 + jj, (*chip, 1 - c), (x, y, c)).wait_recv()
        for cp in first + passed:
            cp.wait_send()
        for cp in mine:
            cp.wait()

    return pl.pallas_call(
        body,
        name=name,
        out_shape=[jax.ShapeDtypeStruct((N_DEV,) + a.shape, a.dtype) for a in srcs],
        in_specs=[ANY] * n,
        out_specs=[ANY] * n,
        scratch_shapes=[pltpu.SemaphoreType.DMA((n, 7)), pltpu.SemaphoreType.DMA((n, 7)),
                        pltpu.SemaphoreType.DMA((n,))],
    )(*srcs)


def exchange_only(*, name, jobs):
    def body(o_ref):
        o_ref[...] = jnp.zeros_like(o_ref)

    _, bufs = _call(body, name=name, jobs=jobs, out_shape=[jax.ShapeDtypeStruct((8, LANE), F32)], grid=(1,),
                    in_specs=[], out_specs=[pl.BlockSpec((8, LANE), lambda i: (0, 0))], args=[], sem=("arbitrary",))
    return None, bufs


def _pack_rows(parts, rows):
    flat = jnp.concatenate([p.reshape(-1) for p in parts])
    return jnp.pad(flat, (0, rows * LANE - flat.shape[0])).reshape(rows, LANE)


def _unpack_rows(packed, shapes):
    flat = packed.reshape(-1)
    out, at = [], 0
    for sh in shapes:
        size = 1
        for dim in sh:
            size *= dim
        out.append(flat[at:at + size].reshape(sh))
        at += size
    return out


CONV_W_PAD = 768
SMALL_W_ROWS = 56


def _pack_small_weights(w_a2, b_a2, hn, conv_w):
    cw = jnp.pad(conv_w.reshape(6, -1), ((0, 0), (0, CONV_W_PAD - conv_w.shape[-1]))).reshape(-1, LANE)
    rows = jnp.concatenate([w_a2[0], b_a2, jnp.pad(hn, ((0, 0), (0, LANE - hn.shape[-1]))), cw], axis=0)
    return jnp.pad(rows, ((0, SMALL_W_ROWS - rows.shape[0]), (0, 0)))


def _unpack_small_weights(gathered):
    w_a2 = gathered[:, 0:GATE_RANK, :].transpose(1, 0, 2).reshape(GATE_RANK, GLA_KEY_DIM)
    b_a2 = gathered[:, GATE_RANK, :].reshape(1, GLA_KEY_DIM)
    hn = gathered[:, GATE_RANK + 1, :GLA_DV // N_DEV].reshape(1, GLA_DV)
    per = D_FF // N_DEV
    cw = gathered[:, GATE_RANK + 2:GATE_RANK + 2 + 6 * CONV_W_PAD // LANE, :].reshape(N_DEV, 6, CONV_W_PAD)[:, :, :per]
    cw = cw.reshape(N_DEV, 2, 3, per).transpose(1, 2, 0, 3).reshape(2, 3, D_FF)
    return w_a2, b_a2, hn, cw


SCHEDULE = {
    "gla_in": [("g1", "gout", None), ("g1", "up0", (0, 1024))],
    "gla_fwd": [("g2", "gout", None), ("g2", "up0", (0, 1024)), ("g1", "up0", (1024, 2048))],
    "gla_out": [("g2", "up0", (1024, 2048)), ("g1", "dn0", (0, 352))],
    "ffn_up0": [("g2", "dn0", (0, 352)), ("g1", "dn0", (352, 704)), ("g1", "kv", None), ("g1", "q", (0, 768))],
    "convglu_fwd0": [("g2", "dn0", (352, 704))],
    "ffn_down0": [("g2", "kv", None), ("g2", "q", (0, 768)), ("g1", "q", (768, 2048)), ("g1", "dout", None)],
    "kv_proj": [("g2", "q", (768, 2048)), ("g2", "dout", None), ("g1", "up1", (0, 704))],
    "q_proj": [("g2", "up1", (0, 704)), ("g1", "up1", (704, 1664))],
    "attn_fwd": [("g2", "up1", (704, 1664)), ("g1", "up1", (1664, 2048)), ("g1", "dn1", None)],
    "dsa_out": [("g2", "up1", (1664, 2048)), ("g2", "dn1", None)],
    "ffn_down_dx1": [("sc", "dn1", (0, 352))],
    "convglu_bwd1": [("sc", "dn1", (352, 704))],
    "ffn_up_dx1": [("sc", "up1", (0, 1024))],
    "attn_bwd": [("sc", "up1", (1024, 2048)), ("sc", "dout", None)],
    "q_proj_dx": [("sc", "q", (0, 1024))],
    "kv_proj_dw": [("sc", "q", (1024, 1792))],
    "kv_proj_dx": [("sc", "q", (1792, 2048)), ("sc", "kv", (0, 768))],
    "ffn_down_dw0": [("sc", "kv", (768, 2048))],
    "ffn_down_dx0": [("sc", "dn0", (0, 384))],
    "convglu_bwd0": [("sc", "dn0", (384, 704))],
    "ffn_up_dx0": [("sc", "up0", (0, 1024))],
    "gla_out_dw": [("sc", "up0", (1024, 1216))],
    "gla_out_dx": [("sc", "up0", (1216, 1408))],
    "gla_bwd": [("sc", "up0", (1408, 2048)), ("sc", "gout", (0, 128))],
    "gla_in_dw": [("sc", "gout", (128, 256))],
    "gla_in_dx": [("sc", "in", (0, 1536))],
    "grads_tail": [("sc", "in", (1536, 2048))],
}
ROW_SHARDED = ("gout", "dout", "dn0", "dn1")


class Plan:
    def __init__(self, weights, srcs=None):
        self.w = dict(weights)
        self.srcs = srcs
        self.grads = {}
        self.recv = {}
        self._names = None

    def weight(self, name):
        buf = self.w[name]
        if name in ROW_SHARDED:
            return buf.reshape(1, buf.shape[0] * buf.shape[1], buf.shape[2])
        return buf

    def jobs(self, call):
        ops = SCHEDULE.get(call)
        if self.srcs is None or not ops:
            return None
        jobs, handles = Jobs(), {}
        for op, name, rows in ops:
            store = self.recv if op == "sc" else self.w
            if name not in handles:
                if name in store:
                    handles[name] = jobs.thru(store[name])
                elif op == "sc":
                    handles[name] = jobs.new(self.grads[name].shape, BF16)
                else:
                    handles[name] = jobs.new((N_DEV,) + self.srcs[name].shape, BF16)
            if op == "g1":
                jobs.gather_ici(self.srcs[name], handles[name], rows)
            elif op == "g2":
                jobs.gather_d2d(handles[name], rows)
            else:
                jobs.scatter(self.grads[name], handles[name], rows)
        self._names = [(name, self.recv if ops[0][0] == "sc" else self.w) for name in handles]
        assert len({op == "sc" for op, _, _ in ops}) == 1
        return jobs

    def run(self, call, fn, *args, **kwargs):
        jobs = self.jobs(call)
        if jobs is None:
            return fn(*args, **kwargs)
        out, bufs = fn(*args, jobs=jobs, **kwargs)
        for (name, store), buf in zip(self._names, bufs):
            store[name] = buf
        return out


def _ffn_fwd(plan, h, norm_g, conv_w, conv_b, tag):
    (n,) = rms_fwd(h, [norm_g], name=f"ffn_norm_fwd{tag}")
    up = plan.run(f"ffn_up{tag}", mm_nn, n, plan.weight(f"up{tag}"), out_dtype=BF16, name=f"ffn_up{tag}")
    act = plan.run(f"convglu_fwd{tag}", convglu_fwd, up, conv_w, conv_b, name=f"convglu_fwd{tag}")
    h_out = plan.run(f"ffn_down{tag}", mm_nn, act, plan.weight(f"dn{tag}"), out_dtype=F32, res=h,
                     name=f"ffn_down{tag}")
    return h_out, (n, up, act)


def _by_rows(dw):
    return dw.reshape(N_DEV, dw.shape[1] // N_DEV, dw.shape[2])


def _ffn_bwd(plan, dh_out, h, saved, norm_g, conv_w, conv_b, tag):
    n, up, act = saved
    plan.grads[f"dn{tag}"] = _by_rows(plan.run(f"ffn_down_dw{tag}", mm_tn, act, dh_out, 1, name=f"ffn_down_dw{tag}"))
    dact = plan.run(f"ffn_down_dx{tag}", mm_nt, dh_out, plan.weight(f"dn{tag}"), out_dtype=BF16,
                    name=f"ffn_down_dx{tag}")
    dup, dconv_w, dconv_b = plan.run(f"convglu_bwd{tag}", convglu_bwd, up, conv_w, conv_b, dact,
                                     name=f"convglu_bwd{tag}")
    plan.grads[f"up{tag}"] = mm_tn(n, dup, N_DEV, name=f"ffn_up_dw{tag}")
    dh, (dnorm,) = plan.run(f"ffn_up_dx{tag}", mm_nt, dup, plan.weight(f"up{tag}"), out_dtype=F32,
                            name=f"ffn_up_dx{tag}", norm=(h, dh_out, [norm_g], []))
    return dh, dnorm, dconv_w, dconv_b


def local_step(x, target, wts, plan):
    row = lambda v: v.reshape(1, -1)
    attn_norm, ffn_norm = wts["attn_norm"], wts["ffn_norm"]
    conv_w, conv_b = wts["ffn_conv_w"], wts["ffn_conv_b"]

    (n1,) = rms_fwd(x, [row(attn_norm[0])], name="attn_norm_fwd0")
    proj = plan.run("gla_in", mm_nn, n1, wts["gla_w_in"], out_dtype=F32, name="gla_in")
    la = gate_fwd(proj, wts["gla_w_a2"], wts["gla_b_a2"])
    o_gla, states, og = plan.run("gla_fwd", gla_fwd, proj, la, wts["gla_head_norm"])
    h1 = plan.run("gla_out", mm_nn, og, plan.weight("gout"), out_dtype=F32, res=x, name="gla_out")
    h2, ffn0 = _ffn_fwd(plan, h1, row(ffn_norm[0]), conv_w[0], row(conv_b[0]), "0")

    h2s = to_streams(h2, name="h2_to_streams")
    kvn, n3 = rms_fwd(h2s, [row(wts["kv_norm"]), row(attn_norm[1])], name="kv_attn_norm_fwd")
    kv = plan.run("kv_proj", mm_nn, kvn, plan.weight("kv"), out_dtype=F32, name="kv_proj")
    q = plan.run("q_proj", mm_nn, n3, plan.weight("q"), out_dtype=F32, name="q_proj")
    o_att, lse = plan.run("attn_fwd", attn_fwd, q, kv)
    h3 = from_streams(plan.run("dsa_out", mm_nn, o_att, plan.weight("dout"), out_dtype=F32, res=h2s, name="dsa_out"),
                      name="h3_from_streams")
    h4, ffn1 = _ffn_fwd(plan, h3, row(ffn_norm[1]), conv_w[1], row(conv_b[1]), "1")

    loss_tile, dh4, d_final = loss_head(h4, row(wts["final_norm"]), target)

    dh3, d_ffn1, dcw1, dcb1 = _ffn_bwd(plan, dh4, h3, ffn1, row(ffn_norm[1]), conv_w[1], row(conv_b[1]), "1")
    dh3s = to_streams(dh3, name="dh3_to_streams")
    plan.grads["dout"] = _by_rows(mm_tn(o_att, dh3s, 1, name="dsa_out_dw"))
    do_att = mm_nt(dh3s, plan.weight("dout"), out_dtype=F32, name="dsa_out_dx")
    dq, dkv = plan.run("attn_bwd", attn_bwd, q, kv, o_att, lse, do_att)
    plan.grads["q"] = mm_tn(n3, dq, N_DEV, name="q_proj_dw")
    dh2_part, (d_attn1,) = plan.run("q_proj_dx", mm_nt, dq, plan.weight("q"), out_dtype=F32, name="q_proj_dx",
                                    norm=(h2s, dh3s, [row(attn_norm[1])], []))
    plan.grads["kv"] = plan.run("kv_proj_dw", mm_tn, kvn, dkv, N_DEV, name="kv_proj_dw")
    dh2s, (d_kvnorm,) = plan.run("kv_proj_dx", mm_nt, dkv, plan.weight("kv"), out_dtype=F32, name="kv_proj_dx",
                                 norm=(h2s, dh2_part, [row(wts["kv_norm"])], []))
    dh2 = from_streams(dh2s, name="dh2_from_streams")
    dh1, d_ffn0, dcw0, dcb0 = _ffn_bwd(plan, dh2, h1, ffn0, row(ffn_norm[0]), conv_w[0], row(conv_b[0]), "0")
    plan.grads["gout"] = _by_rows(plan.run("gla_out_dw", mm_tn, og, dh1, 1, name="gla_out_dw"))
    dog = plan.run("gla_out_dx", mm_nt, dh1, plan.weight("gout"), out_dtype=F32, name="gla_out_dx")
    dq_g, dk_g, dv_g, dr, dla, d_hn = plan.run("gla_bwd", gla_bwd, proj, la, states, o_gla, wts["gla_head_norm"], dog)
    da, dw_a2p, db_a2 = gate_bwd(proj, wts["gla_w_a2"], wts["gla_b_a2"], dla)
    dproj = jnp.concatenate([dq_g, dk_g, dv_g, dr, da], axis=1)
    assert dproj.shape[1] == GLA_IN_PAD
    dw_in = plan.run("gla_in_dw", mm_tn, n1, dproj, 1, name="gla_in_dw")
    plan.grads["in"] = dw_in[0, :, :GLA_IN_DIM].reshape(D_MODEL, N_DEV, GLA_IN_DIM // N_DEV).transpose(1, 0, 2)
    grad_x, (d_attn0,) = plan.run("gla_in_dx", mm_nt, dproj, wts["gla_w_in"], out_dtype=F32, name="gla_in_dx",
                                  norm=(x, dh1, [row(attn_norm[0])], []))

    small = dict(
        attn_norm=jnp.concatenate([d_attn0, d_attn1], axis=0),
        ffn_norm=jnp.concatenate([d_ffn0, d_ffn1], axis=0),
        kv_norm=d_kvnorm.reshape(-1),
        final_norm=d_final.reshape(-1),
        ffn_conv_b=jnp.concatenate([dcb0, dcb1], axis=0),
        gla_w_a2=dw_a2p[:GATE_RANK],
        gla_b_a2=db_a2,
        gla_head_norm=d_hn,
        ffn_conv_w=jnp.stack([dcw0, dcw1]),
    )
    return loss_tile, grad_x, small


SMALL_ORDER = ("attn_norm", "ffn_norm", "kv_norm", "final_norm", "ffn_conv_b",
               "gla_w_a2", "gla_b_a2", "gla_head_norm", "ffn_conv_w")
SMALL_FULL = dict(attn_norm=(2, D_MODEL), ffn_norm=(2, D_MODEL), kv_norm=(D_MODEL,), final_norm=(D_MODEL,),
                  ffn_conv_b=(2, D_FF), gla_w_a2=(GATE_RANK, GLA_KEY_DIM), gla_b_a2=(1, GLA_KEY_DIM),
                  gla_head_norm=(1, GLA_DV), ffn_conv_w=(2, 3, D_FF))
SMALL_SHARDED = ("gla_w_a2", "gla_b_a2", "gla_head_norm", "ffn_conv_w")
SMALL_GRAD_ROWS = 592
SMALL_ADAM_ROWS = 240


def kernel(x, attn_norm, gla_w_in, gla_w_a2, gla_b_a2, gla_head_norm, gla_w_out, kv_norm, w_kv, dsa_w_q, dsa_w_out, ffn_norm, ffn_w_up, ffn_conv_w, ffn_conv_b, ffn_w_down, final_norm, loss_target, m_attn_norm, m_gla_w_in, m_gla_w_a2, m_gla_b_a2, m_gla_head_norm, m_gla_w_out, m_kv_norm, m_w_kv, m_dsa_w_q, m_dsa_w_out, m_ffn_norm, m_ffn_w_up, m_ffn_conv_w, m_ffn_conv_b, m_ffn_w_down, m_final_norm, v_attn_norm, v_gla_w_in, v_gla_w_a2, v_gla_b_a2, v_gla_head_norm, v_gla_w_out, v_kv_norm, v_w_kv, v_dsa_w_q, v_dsa_w_out, v_ffn_norm, v_ffn_w_up, v_ffn_conv_w, v_ffn_conv_b, v_ffn_w_down, v_final_norm):
    me = 4 * lax.axis_index("x") + 2 * lax.axis_index("y") + lax.axis_index("c")
    bf = lambda a: a.astype(BF16)

    g_in, g_small = all_gather([bf(gla_w_in[0]), _pack_small_weights(gla_w_a2, gla_b_a2, gla_head_norm, ffn_conv_w)],
                               name="gather_first")
    w_a2_full, b_a2_full, hn_full, conv_w_full = _unpack_small_weights(g_small)
    w_in_full = jnp.pad(g_in.transpose(1, 0, 2).reshape(D_MODEL, GLA_IN_DIM), ((0, 0), (0, GLA_IN_PAD - GLA_IN_DIM)))
    wts = dict(
        attn_norm=attn_norm, ffn_norm=ffn_norm, kv_norm=kv_norm, final_norm=final_norm, ffn_conv_b=ffn_conv_b,
        gla_w_in=w_in_full[None],
        gla_w_a2=jnp.pad(bf(w_a2_full), ((0, LANE - GATE_RANK), (0, 0))),
        gla_b_a2=b_a2_full, gla_head_norm=hn_full, ffn_conv_w=conv_w_full,
    )
    plan = Plan({}, srcs=dict(gout=bf(gla_w_out[0]), kv=bf(w_kv), q=bf(dsa_w_q[0]), dout=bf(dsa_w_out[0]),
                              up0=bf(ffn_w_up[0]), up1=bf(ffn_w_up[1]), dn0=bf(ffn_w_down[0]), dn1=bf(ffn_w_down[1])))

    loss_tile, grad_x, small = local_step(x[0], loss_target[0], wts, plan)
    loss = lax.psum(loss_tile[0, 0], ("x", "y", "c"))

    plan.run("grads_tail", exchange_only, name="grads_tail")
    shard3 = lambda a: a.reshape((-1,) + a.shape[-2:])
    big_params = dict(gla_w_in=(("in",), gla_w_in, m_gla_w_in, v_gla_w_in),
                      gla_w_out=(("gout",), gla_w_out, m_gla_w_out, v_gla_w_out),
                      w_kv=(("kv",), w_kv, m_w_kv, v_w_kv),
                      dsa_w_q=(("q",), dsa_w_q, m_dsa_w_q, v_dsa_w_q),
                      dsa_w_out=(("dout",), dsa_w_out, m_dsa_w_out, v_dsa_w_out),
                      ffn_w_up=(("up0", "up1"), ffn_w_up, m_ffn_w_up, v_ffn_w_up),
                      ffn_w_down=(("dn0", "dn1"), ffn_w_down, m_ffn_w_down, v_ffn_w_down))
    res = {}
    for nm, (parts, w, m, v) in big_params.items():
        outs = adam_sharded([plan.recv[p] for p in parts], shard3(w), shard3(m), shard3(v), name=f"adam_{nm}")
        res[nm] = [o.reshape(w.shape) for o in outs]

    packed = _pack_rows([small[nm] for nm in SMALL_ORDER], SMALL_GRAD_ROWS)
    (parts,) = all_gather([packed], name="gather_small_grads")
    full = dict(zip(SMALL_ORDER, _unpack_rows(sum_partials(parts), [SMALL_FULL[nm] for nm in SMALL_ORDER])))
    local_w = dict(attn_norm=attn_norm, ffn_norm=ffn_norm, kv_norm=kv_norm, final_norm=final_norm,
                   ffn_conv_b=ffn_conv_b, gla_w_a2=gla_w_a2, gla_b_a2=gla_b_a2, gla_head_norm=gla_head_norm,
                   ffn_conv_w=ffn_conv_w)
    local_m = dict(attn_norm=m_attn_norm, ffn_norm=m_ffn_norm, kv_norm=m_kv_norm, final_norm=m_final_norm,
                   ffn_conv_b=m_ffn_conv_b, gla_w_a2=m_gla_w_a2, gla_b_a2=m_gla_b_a2, gla_head_norm=m_gla_head_norm,
                   ffn_conv_w=m_ffn_conv_w)
    local_v = dict(attn_norm=v_attn_norm, ffn_norm=v_ffn_norm, kv_norm=v_kv_norm, final_norm=v_final_norm,
                   ffn_conv_b=v_ffn_conv_b, gla_w_a2=v_gla_w_a2, gla_b_a2=v_gla_b_a2, gla_head_norm=v_gla_head_norm,
                   ffn_conv_w=v_ffn_conv_w)
    local_g = {}
    for nm in SMALL_ORDER:
        gfull = full[nm]
        if nm in SMALL_SHARDED:
            per = gfull.shape[-1] // N_DEV
            gfull = lax.dynamic_slice_in_dim(gfull, me * per, per, axis=gfull.ndim - 1)
        local_g[nm] = gfull.reshape(local_w[nm].shape)
    shapes = [local_w[nm].shape for nm in SMALL_ORDER]
    pk = lambda dd: _pack_rows([dd[nm] for nm in SMALL_ORDER], SMALL_ADAM_ROWS)
    d_p, m_p, v_p = adam_packed(pk(local_w), pk(local_g), pk(local_m), pk(local_v))
    for nm, dl, mn, vn in zip(SMALL_ORDER, _unpack_rows(d_p, shapes), _unpack_rows(m_p, shapes),
                              _unpack_rows(v_p, shapes)):
        res[nm] = [local_g[nm], dl, mn, vn]

    order = ("attn_norm", "gla_w_in", "gla_w_a2", "gla_b_a2", "gla_head_norm", "gla_w_out", "kv_norm", "w_kv",
             "dsa_w_q", "dsa_w_out", "ffn_norm", "ffn_w_up", "ffn_conv_w", "ffn_conv_b", "ffn_w_down", "final_norm")
    outs = [loss, grad_x[None]]
    for kind in range(4):
        outs.extend(res[nm][kind] for nm in order)
    return tuple(outs)
```

```python
import functools

import jax
import jax.numpy as jnp
from jax import lax
from jax.experimental import pallas as pl
from jax.experimental.pallas import tpu as pltpu

F32 = jnp.float32
BF16 = jnp.bfloat16
MESH = pl.DeviceIdType.MESH
ANY = pl.BlockSpec(memory_space=pl.ANY)

N_DEV = 8
D_MODEL = 2048
GLA_HEADS = 4
GLA_KEY_DIM = 1024
GLA_VAL_DIM = 2048
GLA_DK = 256
GLA_DV = 512
GATE_RANK = 16
GATE_NORMALIZER = 16.0
GLA_CHUNK = 64
GLA_STEP_CHUNKS = 2
GLA_IN_DIM = 2 * GLA_KEY_DIM + 2 * GLA_VAL_DIM + GATE_RANK
GLA_IN_PAD = 6272
ATT_HEADS = 16
HEAD_DIM = 128
DILATIONS = (1, 4, 16)
STREAMS = DILATIONS[-1]
ATT_BLOCK = 128
D_FF = 5632
EPS = 1e-6
ADAM_LR = 0.001
ADAM_B1 = 0.9
ADAM_B2 = 0.999
ADAM_EPS = 1e-08
ADAM_WD = 0.01
ADAM_STEP = 10
NEG = -1e30
LANE = 128
NORM_ROWS = 64
VMEM_LIMIT = 52 * 1024 * 1024
ALIBI_SLOPES = tuple(2.0 ** (-0.5 * (i + 1)) for i in range(ATT_HEADS))


def _params(*sem):
    return pltpu.CompilerParams(dimension_semantics=sem, vmem_limit_bytes=VMEM_LIMIT)


def _tile(n, cap):
    best = None
    for t in range(LANE, min(n, cap) + 1, LANE):
        if n % t == 0:
            best = t
    return best if best is not None else n


def _shard_group(j, ns, cap):
    best = 1
    for g in range(1, j + 1):
        if j % g == 0 and g * ns <= cap:
            best = g
    return best


def _rows(r, c, budget=256 * 1024):
    best = None
    for t in range(16, r + 1, 16):
        if r % t == 0 and t * c <= budget:
            best = t
    return best if best is not None else r


def _flip(coord, bit):
    return 1 - coord if bit else coord


def _place():
    x, y, c = lax.axis_index("x"), lax.axis_index("y"), lax.axis_index("c")
    return x, y, c, 4 * x + 2 * y + c


def _rows_of(ref, rows):
    return ref if rows is None else ref.at[pl.ds(rows[0], rows[1] - rows[0])]


class Jobs:
    def __init__(self):
        self.srcs = []
        self.bufs = []
        self.sems = []
        self.steps = []

    def _src(self, a):
        for i, b in enumerate(self.srcs):
            if b is a:
                return i
        self.srcs.append(a)
        return len(self.srcs) - 1

    def new(self, shape, dtype):
        self.bufs.append((None, jax.ShapeDtypeStruct(shape, dtype)))
        return len(self.bufs) - 1

    def thru(self, a):
        self.bufs.append((a, jax.ShapeDtypeStruct(a.shape, a.dtype)))
        return len(self.bufs) - 1

    def _sem(self, n):
        self.sems.append(pltpu.SemaphoreType.DMA((n,)))
        return len(self.sems) - 1

    def gather_ici(self, src, buf, rows=None):
        si, send, recv, loc = self._src(src), self._sem(4), self._sem(4), self._sem(1)

        def remote(srcs, bufs, sems, slot_of):
            x, y, c, me = _place()
            peers = [(x, y, 1 - c), (1 - x, y, c), (x, 1 - y, c), (1 - x, 1 - y, c)]
            return [pltpu.make_async_remote_copy(
                src_ref=_rows_of(srcs[si], rows),
                dst_ref=_rows_of(bufs[buf].at[me if slot_of == "mine" else 4 * p[0] + 2 * p[1] + p[2]], rows),
                send_sem=sems[send].at[k], recv_sem=sems[recv].at[k], device_id=p, device_id_type=MESH)
                for k, p in enumerate(peers)]

        def local(srcs, bufs, sems):
            return pltpu.make_async_copy(_rows_of(srcs[si], rows), _rows_of(bufs[buf].at[_place()[3]], rows),
                                         sems[loc].at[0])

        def start(srcs, bufs, sems):
            local(srcs, bufs, sems).start()
            for cp in remote(srcs, bufs, sems, "mine"):
                cp.start()

        def finish(srcs, bufs, sems):
            for cp in remote(srcs, bufs, sems, "peer"):
                cp.wait_recv()
            for cp in remote(srcs, bufs, sems, "mine"):
                cp.wait_send()
            local(srcs, bufs, sems).wait()

        self.steps.append((start, finish))

    def gather_d2d(self, buf, rows=None):
        send, recv = self._sem(3), self._sem(3)

        def copies(bufs, sems, core):
            x, y, c, _ = _place()
            cc = c if core == "mine" else 1 - c
            chips = [(1 - x, y), (x, 1 - y), (1 - x, 1 - y)]
            return [pltpu.make_async_remote_copy(
                src_ref=_rows_of(bufs[buf].at[4 * px + 2 * py + cc], rows),
                dst_ref=_rows_of(bufs[buf].at[4 * px + 2 * py + cc], rows),
                send_sem=sems[send].at[k], recv_sem=sems[recv].at[k],
                device_id=(x, y, 1 - c), device_id_type=MESH) for k, (px, py) in enumerate(chips)]

        def start(srcs, bufs, sems):
            for cp in copies(bufs, sems, "mine"):
                cp.start()

        def finish(srcs, bufs, sems):
            for cp in copies(bufs, sems, "sibling"):
                cp.wait_recv()
            for cp in copies(bufs, sems, "mine"):
                cp.wait_send()

        self.steps.append((start, finish))

    def scatter(self, src, buf, rows=None, same=False):
        si, send, recv, loc = self._src(src), self._sem(N_DEV - 1), self._sem(N_DEV - 1), self._sem(1)

        def block(srcs, dev):
            return _rows_of(srcs[si] if same else srcs[si].at[dev], rows)

        def remote(srcs, bufs, sems, slot_of):
            x, y, c, me = _place()
            out = []
            for k in range(1, N_DEV):
                px, py, pc = _flip(x, k >> 2), _flip(y, (k >> 1) & 1), _flip(c, k & 1)
                peer = 4 * px + 2 * py + pc
                out.append(pltpu.make_async_remote_copy(
                    src_ref=block(srcs, peer),
                    dst_ref=_rows_of(bufs[buf].at[me if slot_of == "mine" else peer], rows),
                    send_sem=sems[send].at[k - 1], recv_sem=sems[recv].at[k - 1],
                    device_id=(px, py, pc), device_id_type=MESH))
            return out

        def local(srcs, bufs, sems):
            me = _place()[3]
            return pltpu.make_async_copy(block(srcs, me), _rows_of(bufs[buf].at[me], rows), sems[loc].at[0])

        def start(srcs, bufs, sems):
            local(srcs, bufs, sems).start()
            for cp in remote(srcs, bufs, sems, "mine"):
                cp.start()

        def finish(srcs, bufs, sems):
            for cp in remote(srcs, bufs, sems, "peer"):
                cp.wait_recv()
            for cp in remote(srcs, bufs, sems, "mine"):
                cp.wait_send()
            local(srcs, bufs, sems).wait()

        self.steps.append((start, finish))


def _call(body, *, name, grid, in_specs, out_specs, out_shape, args, sem, scratch_shapes=(), jobs=None):
    in_specs, out_specs, out_shape = list(in_specs), list(out_specs), list(out_shape)
    scratch_shapes = list(scratch_shapes)
    if jobs is None:
        res = pl.pallas_call(body, name=name, out_shape=out_shape, grid=grid, in_specs=in_specs,
                             out_specs=out_specs, scratch_shapes=scratch_shapes,
                             compiler_params=_params(*sem))(*args)
        return list(res), []
    thru = [a for a, _ in jobs.bufs if a is not None]
    n_in, n_src, n_thru = len(args), len(jobs.srcs), len(thru)
    n_out, n_buf, n_scr = len(out_shape), len(jobs.bufs), len(scratch_shapes)
    aliases, t = {}, 0
    for b, (a, _) in enumerate(jobs.bufs):
        if a is not None:
            aliases[n_in + n_src + t] = n_out + b
            t += 1

    def wrapped(*refs):
        at = 0
        ins = refs[at:at + n_in]; at += n_in
        srcs = refs[at:at + n_src]; at += n_src + n_thru
        outs = refs[at:at + n_out]; at += n_out
        bufs = refs[at:at + n_buf]; at += n_buf
        scr = refs[at:at + n_scr]; at += n_scr
        sems = refs[at:]
        first, last = None, None
        for axis, size in enumerate(grid):
            pid = pl.program_id(axis)
            f, l = pid == 0, pid == size - 1
            first = f if first is None else first & f
            last = l if last is None else last & l

        @pl.when(first)
        def _():
            for start, _ in jobs.steps:
                start(srcs, bufs, sems)

        body(*ins, *outs, *scr)

        @pl.when(last)
        def _():
            for _, finish in jobs.steps:
                finish(srcs, bufs, sems)

    res = pl.pallas_call(
        wrapped, name=name,
        out_shape=out_shape + [s for _, s in jobs.bufs],
        grid=grid,
        in_specs=in_specs + [ANY] * (n_src + n_thru),
        out_specs=out_specs + [ANY] * n_buf,
        scratch_shapes=scratch_shapes + jobs.sems,
        input_output_aliases=aliases,
        compiler_params=_params(*(["arbitrary"] * len(grid))),
    )(*args, *jobs.srcs, *thru)
    return res[:n_out], res[n_out:]


def mm_nn(a, w, *, out_dtype, name, res=None, tm=None, jobs=None):
    m, k = a.shape
    j, k2, ns = w.shape
    whole = j == 1 and ns <= 2048 and k <= 2048
    tm = tm or (1024 if a.dtype == BF16 and not whole else 512)
    assert k == k2 and m % tm == 0
    tn = ns if whole else _tile(ns, 1408)
    nsub = ns // tn
    tk = k if k <= 2048 else _tile(k, 1408)
    nk = k // tk
    has_res = res is not None

    def body(*refs):
        if has_res:
            a_ref, w_ref, r_ref, o_ref, acc = refs
        else:
            a_ref, w_ref, o_ref, acc = refs
        kk = pl.program_id(2)

        @pl.when(kk == 0)
        def _():
            acc[...] = jnp.zeros_like(acc)

        acc[...] += jnp.dot(a_ref[...].astype(BF16), w_ref[...], preferred_element_type=F32)

        @pl.when(kk == nk - 1)
        def _():
            r = acc[...]
            if has_res:
                r = r + r_ref[...]
            o_ref[...] = r.astype(out_dtype)

    in_specs = [
        pl.BlockSpec((tm, tk), lambda i, n, kk: (i, kk)),
        pl.BlockSpec((None, tk, tn), lambda i, n, kk: (n // nsub, kk, n % nsub)),
    ]
    args = [a, w]
    out_tile = pl.BlockSpec((tm, tn), lambda i, n, kk: (i, n))
    if has_res:
        in_specs.append(out_tile)
        args.append(res)
    (out,), bufs = _call(
        body, name=name, jobs=jobs,
        out_shape=[jax.ShapeDtypeStruct((m, j * ns), out_dtype)],
        grid=(m // tm, j * nsub, nk),
        in_specs=in_specs,
        out_specs=[out_tile],
        scratch_shapes=[pltpu.VMEM((tm, tn), F32)],
        args=args, sem=("parallel", "parallel", "arbitrary"))
    return out if jobs is None else (out, bufs)


def mm_nt(dy, w, *, out_dtype, name, tm=None, jobs=None, norm=None):
    parts, m, n = (1,) + dy.shape if dy.ndim == 2 else dy.shape
    n *= parts
    j, k, ns = w.shape
    if norm is not None:
        x, dres, gains, more = norm
        tm = tm or (256 if more else 512)
    tm = tm or 1024
    assert n == j * ns and m % tm == 0
    tn = _tile(ns, 2048)
    nsub = ns // tn
    jb = _shard_group(j // parts, ns, 2048 if norm is None else 1024) if nsub == 1 else 1
    tko = _tile(k, 1408) if norm is None else k
    nn = j * nsub // jb
    per_part = nn // parts
    if dy.ndim == 2:
        dy_spec = pl.BlockSpec((tm, jb * tn), lambda i, ko, nq: (i, nq))
    else:
        dy_spec = pl.BlockSpec((None, tm, jb * tn), lambda i, ko, nq: (nq // per_part, i, nq % per_part))
    if jb == 1:
        w_spec = pl.BlockSpec((None, tko, tn), lambda i, ko, nq: (nq // nsub, ko, nq % nsub))
    else:
        w_spec = pl.BlockSpec((jb, tko, ns), lambda i, ko, nq: (nq, ko, 0))

    n_gain = 0 if norm is None else len(gains)
    n_more = 0 if norm is None else len(more)

    def body(*refs):
        a_ref, w_ref = refs[:2]
        acc = refs[-1]
        nq = pl.program_id(2)
        first = pl.program_id(0) == 0

        @pl.when(nq == 0)
        def _():
            acc[...] = jnp.zeros_like(acc)

        if jb == 1:
            acc[...] += lax.dot_general(a_ref[...].astype(BF16), w_ref[...], (((1,), (1,)), ((), ())),
                                        preferred_element_type=F32)
        else:
            part = acc[...]
            for jj in range(jb):
                part = part + lax.dot_general(a_ref[:, jj * ns:(jj + 1) * ns].astype(BF16), w_ref[jj],
                                              (((1,), (1,)), ((), ())), preferred_element_type=F32)
            acc[...] = part

        @pl.when(nq == nn - 1)
        def _():
            if norm is None:
                refs[2][...] = acc[...].astype(out_dtype)
                return
            x_ref, r_ref = refs[2:4]
            g_refs = refs[4:4 + n_gain]
            e_refs = refs[4 + n_gain:4 + n_gain + n_more]
            dx_ref = refs[4 + n_gain + n_more]
            dg_refs = refs[5 + n_gain + n_more:-1]

            @pl.when(first)
            def _():
                for dg_ref in dg_refs:
                    dg_ref[...] = jnp.zeros_like(dg_ref)

            def rows(c, carry):
                sl = pl.ds(pl.multiple_of(c * NORM_ROWS, NORM_ROWS), NORM_ROWS)
                xv = x_ref[sl, :]
                r = lax.rsqrt(jnp.mean(xv * xv, axis=-1, keepdims=True) + EPS)
                xh = xv * r
                out = r_ref[sl, :]
                for idx, (g_ref, dg_ref) in enumerate(zip(g_refs, dg_refs)):
                    dyv = acc[sl, :] if idx == 0 else e_refs[idx - 1][sl, :].astype(F32)
                    dg_ref[...] += jnp.sum(dyv * xh, axis=0, keepdims=True)
                    dxh = dyv * g_ref[...]
                    out = out + r * (dxh - xh * jnp.mean(dxh * xh, axis=-1, keepdims=True))
                dx_ref[sl, :] = out
                return carry

            lax.fori_loop(0, tm // NORM_ROWS, rows, 0)

    out_tile = pl.BlockSpec((tm, tko), lambda i, ko, nq: (i, ko))
    in_specs, args = [dy_spec, w_spec], [dy, w]
    out_shape, out_specs = [jax.ShapeDtypeStruct((m, k), out_dtype)], [out_tile]
    sem = ("parallel", "parallel", "arbitrary")
    if norm is not None:
        vec = pl.BlockSpec((1, k), lambda i, ko, nq: (0, 0))
        in_specs += [out_tile, out_tile] + [vec] * n_gain + [out_tile] * n_more
        args += [x, dres] + list(gains) + list(more)
        out_shape = [jax.ShapeDtypeStruct((m, k), F32)] + [jax.ShapeDtypeStruct((1, k), F32)] * n_gain
        out_specs = [out_tile] + [vec] * n_gain
        sem = ("arbitrary", "arbitrary", "arbitrary")
    outs, bufs = _call(
        body, name=name, jobs=jobs, out_shape=out_shape, grid=(m // tm, k // tko, nn),
        in_specs=in_specs, out_specs=out_specs, scratch_shapes=[pltpu.VMEM((tm, tko), F32)], args=args, sem=sem)
    out = outs[0] if norm is None else (outs[0], outs[1:])
    return out if jobs is None else (out, bufs)


def mm_tn(x, dy, j, *, name, tm=1024, jobs=None):
    m, k = x.shape
    parts, m2, n = (1,) + dy.shape if dy.ndim == 2 else dy.shape
    n *= parts
    assert m == m2 and n % j == 0 and m % tm == 0
    ns = n // j
    tn = _tile(ns, 1408)
    nsub = ns // tn
    jb = _shard_group(j // parts, ns, 1536) if nsub == 1 else 1
    tk = _tile(k, 1408)
    nm = m // tm
    n_steps = j * nsub // jb
    per_part = n_steps // parts
    if dy.ndim == 2:
        dy_spec = pl.BlockSpec((tm, jb * tn), lambda kq, nq, mi: (mi, nq))
    else:
        dy_spec = pl.BlockSpec((None, tm, jb * tn), lambda kq, nq, mi: (nq // per_part, mi, nq % per_part))
    if jb == 1:
        out_spec = pl.BlockSpec((None, tk, tn), lambda kq, nq, mi: (nq // nsub, kq, nq % nsub))
        acc_shape = (tk, tn)
    else:
        out_spec = pl.BlockSpec((jb, tk, ns), lambda kq, nq, mi: (nq, kq, 0))
        acc_shape = (jb, tk, ns)

    def body(x_ref, dy_ref, o_ref, acc):
        mi = pl.program_id(2)

        @pl.when(mi == 0)
        def _():
            acc[...] = jnp.zeros_like(acc)

        xb = x_ref[...].astype(BF16)
        if jb == 1:
            acc[...] += lax.dot_general(xb, dy_ref[...].astype(BF16), (((0,), (0,)), ((), ())),
                                        preferred_element_type=F32)
        else:
            for jj in range(jb):
                acc[jj] += lax.dot_general(xb, dy_ref[:, jj * ns:(jj + 1) * ns].astype(BF16),
                                           (((0,), (0,)), ((), ())), preferred_element_type=F32)

        @pl.when(mi == nm - 1)
        def _():
            o_ref[...] = acc[...].astype(BF16)

    (out,), bufs = _call(
        body, name=name, jobs=jobs,
        out_shape=[jax.ShapeDtypeStruct((j, k, ns), BF16)],
        grid=(k // tk, n_steps, nm),
        in_specs=[
            pl.BlockSpec((tm, tk), lambda kq, nq, mi: (mi, kq)),
            dy_spec,
        ],
        out_specs=[out_spec],
        scratch_shapes=[pltpu.VMEM(acc_shape, F32)],
        args=[x, dy], sem=("parallel", "parallel", "arbitrary"))
    return out if jobs is None else (out, bufs)


STREAM_TC = LANE


def to_streams(x, *, name):
    s, c = x.shape
    per = s // STREAMS

    def body(x_ref, o_ref):
        for st in range(STREAMS):
            o_ref[pl.ds(st * per, per), :] = x_ref[pl.ds(st, per, stride=STREAMS), :]

    blk = pl.BlockSpec((s, STREAM_TC), lambda i: (0, i))
    return pl.pallas_call(body, name=name, out_shape=jax.ShapeDtypeStruct((s, c), x.dtype), grid=(c // STREAM_TC,),
                          in_specs=[blk], out_specs=blk, compiler_params=_params("parallel"))(x)


def from_streams(x, *, name):
    s, c = x.shape
    per = s // STREAMS

    def body(x_ref, o_ref):
        for st in range(STREAMS):
            o_ref[pl.ds(st, per, stride=STREAMS), :] = x_ref[pl.ds(st * per, per), :]

    blk = pl.BlockSpec((s, STREAM_TC), lambda i: (0, i))
    return pl.pallas_call(body, name=name, out_shape=jax.ShapeDtypeStruct((s, c), x.dtype), grid=(c // STREAM_TC,),
                          in_specs=[blk], out_specs=blk, compiler_params=_params("parallel"))(x)


def rms_fwd(x, gains, *, name, ts=512):
    s, d = x.shape
    n = len(gains)

    def body(x_ref, *refs):
        xv = x_ref[...]
        xh = xv * lax.rsqrt(jnp.mean(xv * xv, axis=-1, keepdims=True) + EPS)
        for g_ref, o_ref in zip(refs[:n], refs[n:]):
            o_ref[...] = (xh * g_ref[...]).astype(BF16)

    row = pl.BlockSpec((ts, d), lambda i: (i, 0))
    vec = pl.BlockSpec((1, d), lambda i: (0, 0))
    return pl.pallas_call(
        body,
        name=name,
        out_shape=[jax.ShapeDtypeStruct((s, d), BF16)] * n,
        grid=(s // ts,),
        in_specs=[row] + [vec] * n,
        out_specs=[row] * n,
        compiler_params=_params("parallel"),
    )(x, *gains)


def loss_head(h, gain, target, *, ts=256):
    s, d = h.shape

    def body(h_ref, g_ref, t_ref, l_ref, dh_ref, dg_ref):
        i = pl.program_id(0)

        @pl.when(i == 0)
        def _():
            l_ref[...] = jnp.zeros_like(l_ref)
            dg_ref[...] = jnp.zeros_like(dg_ref)

        xv = h_ref[...]
        r = lax.rsqrt(jnp.mean(xv * xv, axis=-1, keepdims=True) + EPS)
        xh = xv * r
        g = g_ref[...]
        err = xh * g - t_ref[...]
        l_ref[...] += 0.5 * jnp.sum(jnp.mean(err * err, axis=-1, keepdims=True))
        dy = err * (1.0 / d)
        dg_ref[...] += jnp.sum(dy * xh, axis=0, keepdims=True)
        dxh = dy * g
        dh_ref[...] = r * (dxh - xh * jnp.mean(dxh * xh, axis=-1, keepdims=True))

    row = pl.BlockSpec((ts, d), lambda i: (i, 0))
    vec = pl.BlockSpec((1, d), lambda i: (0, 0))
    return pl.pallas_call(
        body,
        name="loss_head",
        out_shape=[jax.ShapeDtypeStruct((8, LANE), F32), jax.ShapeDtypeStruct((s, d), F32),
                   jax.ShapeDtypeStruct((1, d), F32)],
        grid=(s // ts,),
        in_specs=[row, vec, row],
        out_specs=[pl.BlockSpec((8, LANE), lambda i: (0, 0)), row, vec],
        compiler_params=_params("arbitrary"),
    )(h, gain, target)


A_BLOCK = (2 * GLA_KEY_DIM + 2 * GLA_VAL_DIM) // LANE


def gate_fwd(proj, w_a2p, b_a2, *, ts=512):
    s = proj.shape[0]

    def body(a_ref, w_ref, b_ref, o_ref):
        z = jnp.dot(a_ref[...].astype(BF16), w_ref[...], preferred_element_type=F32) + b_ref[...]
        o_ref[...] = (jnp.minimum(z, 0.0) - jnp.log(1.0 + jnp.exp(-jnp.abs(z)))) * (1.0 / GATE_NORMALIZER)

    return pl.pallas_call(
        body,
        name="gate_fwd",
        out_shape=jax.ShapeDtypeStruct((s, GLA_KEY_DIM), F32),
        grid=(s // ts,),
        in_specs=[pl.BlockSpec((ts, LANE), lambda i: (i, A_BLOCK)),
                  pl.BlockSpec((LANE, GLA_KEY_DIM), lambda i: (0, 0)),
                  pl.BlockSpec((1, GLA_KEY_DIM), lambda i: (0, 0))],
        out_specs=pl.BlockSpec((ts, GLA_KEY_DIM), lambda i: (i, 0)),
        compiler_params=_params("parallel"),
    )(proj, w_a2p, b_a2)


def gate_bwd(proj, w_a2p, b_a2, dla, *, ts=512):
    s = proj.shape[0]

    def body(a_ref, w_ref, b_ref, dla_ref, da_ref, dw_ref, db_ref):
        i = pl.program_id(0)

        @pl.when(i == 0)
        def _():
            dw_ref[...] = jnp.zeros_like(dw_ref)
            db_ref[...] = jnp.zeros_like(db_ref)

        a = a_ref[...].astype(BF16)
        w = w_ref[...]
        z = jnp.dot(a, w, preferred_element_type=F32) + b_ref[...]
        dz = dla_ref[...] * (1.0 / GATE_NORMALIZER) / (1.0 + jnp.exp(z))
        dzb = dz.astype(BF16)
        da_ref[...] = lax.dot_general(dzb, w, (((1,), (1,)), ((), ())), preferred_element_type=F32).astype(BF16)
        dw_ref[...] += lax.dot_general(a, dzb, (((0,), (0,)), ((), ())), preferred_element_type=F32)
        db_ref[...] += jnp.sum(dz, axis=0, keepdims=True)

    return pl.pallas_call(
        body,
        name="gate_bwd",
        out_shape=[jax.ShapeDtypeStruct((s, LANE), BF16), jax.ShapeDtypeStruct((LANE, GLA_KEY_DIM), F32),
                   jax.ShapeDtypeStruct((1, GLA_KEY_DIM), F32)],
        grid=(s // ts,),
        in_specs=[pl.BlockSpec((ts, LANE), lambda i: (i, A_BLOCK)),
                  pl.BlockSpec((LANE, GLA_KEY_DIM), lambda i: (0, 0)),
                  pl.BlockSpec((1, GLA_KEY_DIM), lambda i: (0, 0)),
                  pl.BlockSpec((ts, GLA_KEY_DIM), lambda i: (i, 0))],
        out_specs=[pl.BlockSpec((ts, LANE), lambda i: (i, 0)),
                   pl.BlockSpec((LANE, GLA_KEY_DIM), lambda i: (0, 0)),
                   pl.BlockSpec((1, GLA_KEY_DIM), lambda i: (0, 0))],
        compiler_params=_params("arbitrary"),
    )(proj, w_a2p, b_a2, dla)


def _masked_sum(mask, x):
    m = mask.astype(BF16)
    hi = x.astype(BF16)
    rest = x - hi.astype(F32)
    mid = rest.astype(BF16)
    lo = (rest - mid.astype(F32)).astype(BF16)
    dot = lambda t: jnp.dot(m, t, preferred_element_type=F32)
    return dot(hi) + dot(mid) + dot(lo)


def _chunk_terms(q, k, la):
    c_len = GLA_CHUNK
    row = lax.broadcasted_iota(jnp.int32, (c_len, c_len), 0)
    col = lax.broadcasted_iota(jnp.int32, (c_len, c_len), 1)
    tri = row >= col
    c = _masked_sum(tri, la)
    last = jnp.sum(la, axis=0, keepdims=True)
    q_dec = q * (GLA_DK ** -0.5) * jnp.exp(c)
    k_inv = k * jnp.exp(-c)
    k_end = k * jnp.exp(last - c)
    return c, last, q_dec, k_inv, k_end, tri


def _dot(a, b, ca, cb):
    return lax.dot_general(a.astype(BF16), b.astype(BF16), (((ca,), (cb,)), ((), ())), preferred_element_type=F32)


def gla_fwd(proj, la, hn, jobs=None):
    s = proj.shape[0]
    n_chunks = s // GLA_CHUNK
    rows = GLA_CHUNK * GLA_STEP_CHUNKS

    def body(q_ref, k_ref, v_ref, r_ref, la_ref, hn_ref, o_ref, st_out, og_ref, st):
        @pl.when(pl.program_id(0) == 0)
        def _():
            st[...] = jnp.zeros_like(st)

        for h in range(GLA_HEADS):
            hk = slice(h * GLA_DK, (h + 1) * GLA_DK)
            hv = slice(h * GLA_DV, (h + 1) * GLA_DV)
            for cc in range(GLA_STEP_CHUNKS):
                rs = slice(cc * GLA_CHUNK, (cc + 1) * GLA_CHUNK)
                _, last, q_dec, k_inv, k_end, tri = _chunk_terms(q_ref[rs, hk], k_ref[rs, hk], la_ref[rs, hk])
                v = v_ref[rs, hv]
                a = jnp.where(tri, _dot(q_dec, k_inv, 1, 1), 0.0)
                state = st[h]
                st_out[h, cc] = state
                ov = _dot(a, v, 1, 0) + _dot(q_dec, state, 1, 1)
                o_ref[rs, hv] = ov
                st[h] = state * jnp.exp(last) + _dot(v, k_end, 0, 0)
                oh = ov * lax.rsqrt(jnp.mean(ov * ov, axis=-1, keepdims=True) + EPS)
                r = r_ref[rs, hv]
                og_ref[rs, hv] = (oh * hn_ref[...] * (r * jax.nn.sigmoid(r))).astype(BF16)

    key = lambda col: pl.BlockSpec((rows, GLA_KEY_DIM), lambda n: (n, col))
    val = lambda col: pl.BlockSpec((rows, GLA_VAL_DIM), lambda n: (n, col))
    outs, bufs = _call(
        body, name="gla_fwd", jobs=jobs,
        out_shape=[jax.ShapeDtypeStruct((s, GLA_VAL_DIM), F32),
                   jax.ShapeDtypeStruct((GLA_HEADS, n_chunks, GLA_DV, GLA_DK), F32),
                   jax.ShapeDtypeStruct((s, GLA_VAL_DIM), BF16)],
        grid=(n_chunks // GLA_STEP_CHUNKS,),
        in_specs=[key(0), key(1), val(1), val(R_BLOCK // GLA_HEADS), key(0), pl.BlockSpec((1, GLA_DV), lambda n: (0, 0))],
        out_specs=[val(0), pl.BlockSpec((GLA_HEADS, GLA_STEP_CHUNKS, GLA_DV, GLA_DK), lambda n: (0, n, 0, 0)), val(0)],
        scratch_shapes=[pltpu.VMEM((GLA_HEADS, GLA_DV, GLA_DK), F32)],
        args=[proj, proj, proj, proj, la, hn], sem=("arbitrary",))
    return outs if jobs is None else (outs, bufs)


def gla_bwd(proj, la, states, o, hn, dog, jobs=None):
    s = proj.shape[0]
    n_steps = s // GLA_CHUNK // GLA_STEP_CHUNKS
    lastc = n_steps - 1
    rows = GLA_CHUNK * GLA_STEP_CHUNKS

    def body(q_ref, k_ref, v_ref, r_ref, la_ref, o_ref, hn_ref, dog_ref, st_ref,
             dq_ref, dk_ref, dv_ref, dr_ref, dla_ref, dhn_ref, dst):
        @pl.when(pl.program_id(0) == 0)
        def _():
            dst[...] = jnp.zeros_like(dst)
            dhn_ref[...] = jnp.zeros_like(dhn_ref)

        upper = (lax.broadcasted_iota(jnp.int32, (GLA_CHUNK, GLA_CHUNK), 0)
                 <= lax.broadcasted_iota(jnp.int32, (GLA_CHUNK, GLA_CHUNK), 1))
        gain = hn_ref[...]
        for h in range(GLA_HEADS):
            hk = slice(h * GLA_DK, (h + 1) * GLA_DK)
            hv = slice(h * GLA_DV, (h + 1) * GLA_DV)
            for cc in reversed(range(GLA_STEP_CHUNKS)):
                rs = slice(cc * GLA_CHUNK, (cc + 1) * GLA_CHUNK)
                ov = o_ref[rs, hv]
                inv = lax.rsqrt(jnp.mean(ov * ov, axis=-1, keepdims=True) + EPS)
                oh = ov * inv
                r = r_ref[rs, hv]
                sig = jax.nn.sigmoid(r)
                dgv = dog_ref[rs, hv]
                d_on = dgv * (r * sig)
                dr_ref[rs, hv] = (dgv * (oh * gain) * (sig * (1.0 + r * (1.0 - sig)))).astype(BF16)
                dhn_ref[...] += jnp.sum(d_on * oh, axis=0, keepdims=True)
                doh = d_on * gain
                dout = inv * (doh - oh * jnp.mean(doh * oh, axis=-1, keepdims=True))
                c, last, q_dec, k_inv, k_end, tri = _chunk_terms(q_ref[rs, hk], k_ref[rs, hk], la_ref[rs, hk])
                v = v_ref[rs, hv]
                state = st_ref[h, cc]
                dstate = dst[h]
                e_last = jnp.exp(last)
                a = jnp.where(tri, _dot(q_dec, k_inv, 1, 1), 0.0)
                da = jnp.where(tri, _dot(dout, v, 1, 1), 0.0)
                dv_ref[rs, hv] = (_dot(a, dout, 0, 0) + _dot(k_end, dstate, 1, 1)).astype(BF16)
                dq_dec = _dot(da, k_inv, 1, 0) + _dot(dout, state, 1, 0)
                dk_inv = _dot(da, q_dec, 0, 0)
                dk_end = _dot(v, dstate, 1, 0)
                dst[h] = dstate * e_last + _dot(dout, q_dec, 0, 0)
                dq_ref[rs, hk] = (dq_dec * (GLA_DK ** -0.5) * jnp.exp(c)).astype(BF16)
                dk_ref[rs, hk] = (dk_inv * jnp.exp(-c) + dk_end * jnp.exp(last - c)).astype(BF16)
                ke_term = dk_end * k_end
                dc = dq_dec * q_dec - dk_inv * k_inv - ke_term
                dlast = (jnp.sum(ke_term, axis=0, keepdims=True)
                         + e_last * jnp.sum(dstate * state, axis=0, keepdims=True))
                dla_ref[rs, hk] = _masked_sum(upper, dc) + dlast

    key = lambda col: pl.BlockSpec((rows, GLA_KEY_DIM), lambda n: (lastc - n, col))
    val = lambda col: pl.BlockSpec((rows, GLA_VAL_DIM), lambda n: (lastc - n, col))
    vec = pl.BlockSpec((1, GLA_DV), lambda n: (0, 0))
    outs, bufs = _call(
        body, name="gla_bwd", jobs=jobs,
        out_shape=[jax.ShapeDtypeStruct((s, GLA_KEY_DIM), BF16), jax.ShapeDtypeStruct((s, GLA_KEY_DIM), BF16),
                   jax.ShapeDtypeStruct((s, GLA_VAL_DIM), BF16), jax.ShapeDtypeStruct((s, GLA_VAL_DIM), BF16),
                   jax.ShapeDtypeStruct((s, GLA_KEY_DIM), F32), jax.ShapeDtypeStruct((1, GLA_DV), F32)],
        grid=(n_steps,),
        in_specs=[key(0), key(1), val(1), val(R_BLOCK // GLA_HEADS), key(0), val(0), vec, val(0),
                  pl.BlockSpec((GLA_HEADS, GLA_STEP_CHUNKS, GLA_DV, GLA_DK), lambda n: (0, lastc - n, 0, 0))],
        out_specs=[key(0), key(0), val(0), val(0), key(0), vec],
        scratch_shapes=[pltpu.VMEM((GLA_HEADS, GLA_DV, GLA_DK), F32)],
        args=[proj, proj, proj, proj, la, o, hn, dog, states], sem=("arbitrary",))
    return outs if jobs is None else (outs, bufs)


R_BLOCK = (2 * GLA_KEY_DIM + GLA_VAL_DIM) // GLA_DV


CONV_TC = 128
SQRT_HALF = 0.7071067811865476
INV_SQRT_2PI = 0.3989422804014327


def _conv_gate(g_ref, cw_ref, cb_ref):
    g0 = g_ref[...].astype(F32)
    t = lax.broadcasted_iota(jnp.int32, g0.shape, 0)
    g1 = jnp.where(t >= 1, pltpu.roll(g0, 1, 0), 0.0)
    g2 = jnp.where(t >= 2, pltpu.roll(g0, 2, 0), 0.0)
    gc = cw_ref[0:1, :] * g2 + cw_ref[1:2, :] * g1 + cw_ref[2:3, :] * g0 + cb_ref[...]
    return g0, g1, g2, gc, t


def convglu_fwd(up, conv_w, conv_b, *, name, jobs=None):
    s = up.shape[0]
    nc = D_FF // CONV_TC

    def body(u_ref, g_ref, cw_ref, cb_ref, o_ref):
        _, _, _, gc, _ = _conv_gate(g_ref, cw_ref, cb_ref)
        gelu = 0.5 * gc * (1.0 + lax.erf(gc * SQRT_HALF))
        o_ref[...] = (gelu * u_ref[...].astype(F32)).astype(BF16)

    (out,), bufs = _call(
        body, name=name, jobs=jobs,
        out_shape=[jax.ShapeDtypeStruct((s, D_FF), BF16)],
        grid=(nc,),
        in_specs=[pl.BlockSpec((s, CONV_TC), lambda c: (0, c)),
                  pl.BlockSpec((s, CONV_TC), lambda c: (0, nc + c)),
                  pl.BlockSpec((3, CONV_TC), lambda c: (0, c)),
                  pl.BlockSpec((1, CONV_TC), lambda c: (0, c))],
        out_specs=[pl.BlockSpec((s, CONV_TC), lambda c: (0, c))],
        args=[up, up, conv_w, conv_b], sem=("parallel",))
    return out if jobs is None else (out, bufs)


def convglu_bwd(up, conv_w, conv_b, dact, *, name, jobs=None):
    s = up.shape[0]
    nc = D_FF // CONV_TC

    def body(u_ref, g_ref, cw_ref, cb_ref, da_ref, dup_ref, dcw_ref, dcb_ref):
        du_ref, dg_ref = dup_ref.at[0], dup_ref.at[1]
        g0, g1, g2, gc, t = _conv_gate(g_ref, cw_ref, cb_ref)
        cdf = 0.5 * (1.0 + lax.erf(gc * SQRT_HALF))
        da = da_ref[...].astype(F32)
        du_ref[...] = (da * gc * cdf).astype(BF16)
        dgc = da * u_ref[...].astype(F32) * (cdf + gc * jnp.exp(-0.5 * gc * gc) * INV_SQRT_2PI)
        dcb_ref[...] = jnp.sum(dgc, axis=0, keepdims=True)
        dcw_ref[0:1, :] = jnp.sum(dgc * g2, axis=0, keepdims=True)
        dcw_ref[1:2, :] = jnp.sum(dgc * g1, axis=0, keepdims=True)
        dcw_ref[2:3, :] = jnp.sum(dgc * g0, axis=0, keepdims=True)
        n1 = jnp.where(t < s - 1, pltpu.roll(dgc, s - 1, 0), 0.0)
        n2 = jnp.where(t < s - 2, pltpu.roll(dgc, s - 2, 0), 0.0)
        dg_ref[...] = (cw_ref[2:3, :] * dgc + cw_ref[1:2, :] * n1 + cw_ref[0:1, :] * n2).astype(BF16)

    col = pl.BlockSpec((s, CONV_TC), lambda c: (0, c))
    outs, bufs = _call(
        body, name=name, jobs=jobs,
        out_shape=[jax.ShapeDtypeStruct((2, s, D_FF), BF16),
                   jax.ShapeDtypeStruct((3, D_FF), F32), jax.ShapeDtypeStruct((1, D_FF), F32)],
        grid=(nc,),
        in_specs=[col, pl.BlockSpec((s, CONV_TC), lambda c: (0, nc + c)),
                  pl.BlockSpec((3, CONV_TC), lambda c: (0, c)),
                  pl.BlockSpec((1, CONV_TC), lambda c: (0, c)), col],
        out_specs=[pl.BlockSpec((2, s, CONV_TC), lambda c: (0, 0, c)), pl.BlockSpec((3, CONV_TC), lambda c: (0, c)),
                   pl.BlockSpec((1, CONV_TC), lambda c: (0, c))],
        args=[up, up, conv_w, conv_b, dact], sem=("parallel",))
    return outs if jobs is None else (outs, bufs)


SLOPE_TILE = (8, LANE)


def _slope_table():
    return jnp.broadcast_to(jnp.asarray(ALIBI_SLOPES, F32)[:, None, None], (ATT_HEADS,) + SLOPE_TILE)


def _pieces(s_len, d):
    npc = STREAMS // d
    lp = ATT_BLOCK // npc
    return npc, lp, (s_len // STREAMS) // lp


def _gather(ref, r, b, d, s_len):
    npc, lp, _ = _pieces(s_len, d)
    per = s_len // STREAMS
    parts = [ref[pl.ds((r + d * k) * per + b * lp, lp), :] for k in range(npc)]
    return parts[0] if npc == 1 else jnp.concatenate(parts, axis=0)


def _scatter(ref, r, b, d, s_len, val, add=False):
    npc, lp, _ = _pieces(s_len, d)
    per = s_len // STREAMS
    for k in range(npc):
        rows = pl.ds((r + d * k) * per + b * lp, lp)
        piece = val[k * lp:(k + 1) * lp]
        if add:
            ref[rows, :] += piece
        else:
            ref[rows, :] = piece


def _stream_bias(slope, d, s_len):
    npc, lp, _ = _pieces(s_len, d)
    qi = lax.broadcasted_iota(jnp.int32, (ATT_BLOCK, 2 * ATT_BLOCK), 0)
    c = lax.broadcasted_iota(jnp.int32, (ATT_BLOCK, 2 * ATT_BLOCK), 1)
    own = c // ATT_BLOCK
    cc = c - own * ATT_BLOCK
    dist = npc * ((qi % lp) - (cc % lp) + lp * (1 - own)) + (qi // lp - cc // lp)
    ok = (dist >= 0) & (dist <= ATT_BLOCK)
    return jnp.where(ok, (slope * (-float(d))) * dist.astype(F32), NEG)


def attn_fwd(q, kv, jobs=None):
    s_len = q.shape[0]
    scale = HEAD_DIM ** -0.5

    def body(sl_ref, q_ref, k_ref, v_ref, o_ref, lse_ref):
        g = pl.program_id(1)
        slope = sl_ref[0:1, 0:1]

        def branch(gi, d):
            _, _, nblk = _pieces(s_len, d)
            bias = _stream_bias(slope, d, s_len)
            for r in range(d):
                for b in range(nblk):
                    qb = _gather(q_ref, r, b, d, s_len)
                    kc, vc = _gather(k_ref, r, b, d, s_len), _gather(v_ref, r, b, d, s_len)
                    if b == 0:
                        kcat, vcat, bb = kc, vc, bias[:, ATT_BLOCK:]
                    else:
                        kcat = jnp.concatenate([_gather(k_ref, r, b - 1, d, s_len), kc], axis=0)
                        vcat = jnp.concatenate([_gather(v_ref, r, b - 1, d, s_len), vc], axis=0)
                        bb = bias
                    sc = _dot(qb, kcat, 1, 1) * scale + bb
                    m = jnp.max(sc, axis=-1, keepdims=True)
                    p = jnp.exp(sc - m)
                    l = jnp.sum(p, axis=-1, keepdims=True)
                    o_new = _dot(p, vcat, 1, 0) / l
                    lse_new = m + jnp.log(l)
                    if gi > 0:
                        lse_old = _gather(lse_ref, r, b, d, s_len)[:, 0:1]
                        top = jnp.maximum(lse_old, lse_new)
                        e_old, e_new = jnp.exp(lse_old - top), jnp.exp(lse_new - top)
                        den = e_old + e_new
                        o_new = (e_old * _gather(o_ref, r, b, d, s_len) + e_new * o_new) / den
                        lse_new = top + jnp.log(den)
                    _scatter(o_ref, r, b, d, s_len, o_new)
                    _scatter(lse_ref, r, b, d, s_len, jnp.broadcast_to(lse_new, (ATT_BLOCK, HEAD_DIM)))

        for gi, d in enumerate(DILATIONS):
            @pl.when(g == gi)
            def _():
                branch(gi, d)

    blk = lambda col: pl.BlockSpec((s_len, HEAD_DIM), lambda h, g: (0, col(h, g)))
    head = lambda h, g: h
    outs, bufs = _call(
        body, name="attn_fwd", jobs=jobs,
        out_shape=[jax.ShapeDtypeStruct((s_len, ATT_HEADS * HEAD_DIM), F32)] * 2,
        grid=(ATT_HEADS, len(DILATIONS)),
        in_specs=[pl.BlockSpec((None,) + SLOPE_TILE, lambda h, g: (h, 0, 0)),
                  blk(lambda h, g: g * ATT_HEADS + h), blk(head), blk(lambda h, g: ATT_HEADS + h)],
        out_specs=[blk(head), blk(head)],
        args=[_slope_table(), q, kv, kv], sem=("parallel", "arbitrary"))
    return outs if jobs is None else (outs, bufs)


def attn_bwd(q, kv, o, lse, do, jobs=None):
    s_len = q.shape[0]
    scale = HEAD_DIM ** -0.5
    chunks = s_len // ATT_BLOCK

    def body(sl_ref, q_ref, k_ref, v_ref, o_ref, lse_ref, do_ref, dq_ref, dkv_ref, dlt):
        g = pl.program_id(1)
        slope = sl_ref[0:1, 0:1]
        dk_ref, dv_ref = dkv_ref.at[0], dkv_ref.at[1]

        @pl.when(g == 0)
        def _():
            dkv_ref[...] = jnp.zeros_like(dkv_ref)

            def deltas(c, carry):
                rows = pl.ds(pl.multiple_of(c * ATT_BLOCK, ATT_BLOCK), ATT_BLOCK)
                dlt[rows, :] = jnp.sum(do_ref[rows, :] * o_ref[rows, :], axis=-1, keepdims=True)
                return carry
            lax.fori_loop(0, chunks, deltas, 0)

        def branch(d):
            _, _, nblk = _pieces(s_len, d)
            bias = _stream_bias(slope, d, s_len)
            for r in range(d):
                for b in range(nblk):
                    qb = _gather(q_ref, r, b, d, s_len)
                    dob = _gather(do_ref, r, b, d, s_len)
                    kc, vc = _gather(k_ref, r, b, d, s_len), _gather(v_ref, r, b, d, s_len)
                    if b == 0:
                        kcat, vcat, bb = kc, vc, bias[:, ATT_BLOCK:]
                    else:
                        kcat = jnp.concatenate([_gather(k_ref, r, b - 1, d, s_len), kc], axis=0)
                        vcat = jnp.concatenate([_gather(v_ref, r, b - 1, d, s_len), vc], axis=0)
                        bb = bias
                    sc = _dot(qb, kcat, 1, 1) * scale + bb
                    p = jnp.exp(sc - _gather(lse_ref, r, b, d, s_len)[:, 0:1])
                    ds = p * (_dot(dob, vcat, 1, 1) - _gather(dlt, r, b, d, s_len))
                    _scatter(dq_ref, r, b, d, s_len, _dot(ds, kcat, 1, 0) * scale)
                    dk = _dot(ds, qb, 0, 0) * scale
                    dv = _dot(p, dob, 0, 0)
                    if b == 0:
                        _scatter(dk_ref, r, b, d, s_len, dk, add=True)
                        _scatter(dv_ref, r, b, d, s_len, dv, add=True)
                    else:
                        _scatter(dk_ref, r, b - 1, d, s_len, dk[:ATT_BLOCK], add=True)
                        _scatter(dv_ref, r, b - 1, d, s_len, dv[:ATT_BLOCK], add=True)
                        _scatter(dk_ref, r, b, d, s_len, dk[ATT_BLOCK:], add=True)
                        _scatter(dv_ref, r, b, d, s_len, dv[ATT_BLOCK:], add=True)

        for gi, d in enumerate(DILATIONS):
            @pl.when(g == gi)
            def _():
                branch(d)

    blk = lambda col: pl.BlockSpec((s_len, HEAD_DIM), lambda h, g: (0, col(h, g)))
    head = lambda h, g: h
    q_col = lambda h, g: g * ATT_HEADS + h
    outs, bufs = _call(
        body, name="attn_bwd", jobs=jobs,
        out_shape=[jax.ShapeDtypeStruct(q.shape, F32), jax.ShapeDtypeStruct((2, s_len, ATT_HEADS * HEAD_DIM), F32)],
        grid=(ATT_HEADS, len(DILATIONS)),
        in_specs=[pl.BlockSpec((None,) + SLOPE_TILE, lambda h, g: (h, 0, 0)),
                  blk(q_col), blk(head), blk(lambda h, g: ATT_HEADS + h), blk(head), blk(head), blk(head)],
        out_specs=[blk(q_col), pl.BlockSpec((2, s_len, HEAD_DIM), lambda h, g: (0, 0, h))],
        scratch_shapes=[pltpu.VMEM((s_len, 1), F32)],
        args=[_slope_table(), q, kv, kv, o, lse, do], sem=("parallel", "arbitrary"))
    return outs if jobs is None else (outs, bufs)


def _adam(w, g, m, v):
    m = ADAM_B1 * m + (1.0 - ADAM_B1) * g
    v = ADAM_B2 * v + (1.0 - ADAM_B2) * (g * g)
    m_hat = m / (1.0 - ADAM_B1 ** ADAM_STEP)
    v_hat = v / (1.0 - ADAM_B2 ** ADAM_STEP)
    delta = -ADAM_LR * (m_hat / (jnp.sqrt(v_hat) + ADAM_EPS) + ADAM_WD * w)
    return delta, m, v


def adam_sharded(recvs, w, m, v, *, name):
    layers = len(recvs)
    n_src, r, c = recvs[0].shape
    tr = _rows(r, c)

    def body(*refs):
        p_refs = refs[:layers]
        w_ref, m_ref, v_ref, g_ref, d_ref, mo_ref, vo_ref = refs[layers:]
        for layer, p_ref in enumerate(p_refs):
            @pl.when(pl.program_id(0) == layer)
            def _():
                g = p_ref[0].astype(F32)
                for src in range(1, n_src):
                    g = g + p_ref[src].astype(F32)
                delta, m_new, v_new = _adam(w_ref[...], g, m_ref[...], v_ref[...])
                g_ref[...] = g
                d_ref[...] = delta
                mo_ref[...] = m_new
                vo_ref[...] = v_new

    blk = pl.BlockSpec((None, tr, c), lambda l, i: (l, i, 0))
    out = jax.ShapeDtypeStruct((layers, r, c), F32)
    part = [pl.BlockSpec((n_src, tr, c), functools.partial(lambda l, i, layer: (0, jnp.where(l == layer, i, 0), 0),
                                                            layer=layer)) for layer in range(layers)]
    return pl.pallas_call(
        body,
        name=name,
        out_shape=[out] * 4,
        grid=(layers, r // tr),
        in_specs=part + [blk, blk, blk],
        out_specs=[blk] * 4,
        compiler_params=_params("parallel", "parallel"),
    )(*recvs, w, m, v)


def sum_partials(parts):
    n_src, r, c = parts.shape

    def body(p_ref, o_ref):
        g = p_ref[0]
        for src in range(1, n_src):
            g = g + p_ref[src]
        o_ref[...] = g

    return pl.pallas_call(
        body,
        name="sum_small_grads",
        out_shape=jax.ShapeDtypeStruct((r, c), F32),
    )(parts)


def adam_packed(w, g, m, v):
    def body(w_ref, g_ref, m_ref, v_ref, d_ref, mo_ref, vo_ref):
        delta, m_new, v_new = _adam(w_ref[...], g_ref[...], m_ref[...], v_ref[...])
        d_ref[...] = delta
        mo_ref[...] = m_new
        vo_ref[...] = v_new

    out = jax.ShapeDtypeStruct(w.shape, F32)
    return pl.pallas_call(body, name="adam_small", out_shape=[out] * 3)(w, g, m, v)


def all_gather(srcs, *, name):
    n = len(srcs)

    def body(*refs):
        src, dst = refs[:n], refs[n:2 * n]
        send_sems, recv_sems, local_sems = refs[2 * n:]
        x, y, c, me = _place()
        sibling = (x, y, 1 - c)
        chips = [(1 - x, y), (x, 1 - y), (1 - x, 1 - y)]

        def index(px, py, pc):
            return 4 * px + 2 * py + pc

        def copy(p, k, block, to, from_src=False):
            slot = dst[p].at[index(*block)]
            return pltpu.make_async_remote_copy(
                src_ref=src[p] if from_src else slot, dst_ref=slot,
                send_sem=send_sems.at[p, k], recv_sem=recv_sems.at[p, k],
                device_id=to, device_id_type=MESH)

        mine = [pltpu.make_async_copy(src[p], dst[p].at[me], local_sems.at[p]) for p in range(n)]
        for cp in mine:
            cp.start()
        first = []
        for p in range(n):
            first.append(copy(p, 0, (x, y, c), sibling, from_src=True))
            for jj, chip in enumerate(chips):
                first.append(copy(p, 1 + jj, (x, y, c), (*chip, c), from_src=True))
        for cp in first:
            cp.start()
        passed = []
        for jj, chip in enumerate(chips):
            for p in range(n):
                copy(p, 1 + jj, (*chip, c), (x, y, c)).wait_recv()
                fwd = copy(p, 4 + jj, (*chip, c), sibling)
                fwd.start()
                passed.append(fwd)
        for p in range(n):
            copy(p, 0, sibling, (x, y, c)).wait_recv()
            for jj, chip in enumerate(chips):
                copy(p, 4 + jj, (*chip, 1 - c), (x, y, c)).wait_recv()
        for cp in first + passed:
            cp.wait_send()
        for cp in mine:
            cp.wait()

    return pl.pallas_call(
        body,
        name=name,
        out_shape=[jax.ShapeDtypeStruct((N_DEV,) + a.shape, a.dtype) for a in srcs],
        in_specs=[ANY] * n,
        out_specs=[ANY] * n,
        scratch_shapes=[pltpu.SemaphoreType.DMA((n, 7)), pltpu.SemaphoreType.DMA((n, 7)),
                        pltpu.SemaphoreType.DMA((n,))],
    )(*srcs)


def exchange_only(*, name, jobs):
    def body(o_ref):
        o_ref[...] = jnp.zeros_like(o_ref)

    _, bufs = _call(body, name=name, jobs=jobs, out_shape=[jax.ShapeDtypeStruct((8, LANE), F32)], grid=(1,),
                    in_specs=[], out_specs=[pl.BlockSpec((8, LANE), lambda i: (0, 0))], args=[], sem=("arbitrary",))
    return None, bufs


def _pack_rows(parts, rows):
    flat = jnp.concatenate([p.reshape(-1) for p in parts])
    return jnp.pad(flat, (0, rows * LANE - flat.shape[0])).reshape(rows, LANE)


def _unpack_rows(packed, shapes):
    flat = packed.reshape(-1)
    out, at = [], 0
    for sh in shapes:
        size = 1
        for dim in sh:
            size *= dim
        out.append(flat[at:at + size].reshape(sh))
        at += size
    return out


CONV_W_PAD = 768
SMALL_W_ROWS = 56


def _pack_small_weights(w_a2, b_a2, hn, conv_w):
    cw = jnp.pad(conv_w.reshape(6, -1), ((0, 0), (0, CONV_W_PAD - conv_w.shape[-1]))).reshape(-1, LANE)
    rows = jnp.concatenate([w_a2[0], b_a2, jnp.pad(hn, ((0, 0), (0, LANE - hn.shape[-1]))), cw], axis=0)
    return jnp.pad(rows, ((0, SMALL_W_ROWS - rows.shape[0]), (0, 0)))


def _unpack_small_weights(gathered):
    w_a2 = gathered[:, 0:GATE_RANK, :].transpose(1, 0, 2).reshape(GATE_RANK, GLA_KEY_DIM)
    b_a2 = gathered[:, GATE_RANK, :].reshape(1, GLA_KEY_DIM)
    hn = gathered[:, GATE_RANK + 1, :GLA_DV // N_DEV].reshape(1, GLA_DV)
    per = D_FF // N_DEV
    cw = gathered[:, GATE_RANK + 2:GATE_RANK + 2 + 6 * CONV_W_PAD // LANE, :].reshape(N_DEV, 6, CONV_W_PAD)[:, :, :per]
    cw = cw.reshape(N_DEV, 2, 3, per).transpose(1, 2, 0, 3).reshape(2, 3, D_FF)
    return w_a2, b_a2, hn, cw


SCHEDULE = {
    "gla_in": [("g1", "gout", None), ("g1", "up0", (0, 1024))],
    "gla_fwd": [("g2", "gout", None), ("g2", "up0", (0, 1024)), ("g1", "up0", (1024, 2048))],
    "gla_out": [("g2", "up0", (1024, 2048)), ("g1", "dn0", (0, 352))],
    "ffn_up0": [("g2", "dn0", (0, 352)), ("g1", "dn0", (352, 704)), ("g1", "kv", None), ("g1", "q", (0, 768))],
    "convglu_fwd0": [("g2", "dn0", (352, 704))],
    "ffn_down0": [("g2", "kv", None), ("g2", "q", (0, 768)), ("g1", "q", (768, 2048)), ("g1", "dout", None)],
    "kv_proj": [("g2", "q", (768, 2048)), ("g2", "dout", None), ("g1", "up1", (0, 704))],
    "q_proj": [("g2", "up1", (0, 704)), ("g1", "up1", (704, 1664))],
    "attn_fwd": [("g2", "up1", (704, 1664)), ("g1", "up1", (1664, 2048)), ("g1", "dn1", None)],
    "dsa_out": [("g2", "up1", (1664, 2048)), ("g2", "dn1", None)],
    "ffn_down_dx1": [("sc", "dn1", (0, 352))],
    "convglu_bwd1": [("sc", "dn1", (352, 704))],
    "ffn_up_dx1": [("sc", "up1", (0, 1024))],
    "attn_bwd": [("sc", "up1", (1024, 2048)), ("sc", "dout", None)],
    "q_proj_dx": [("sc", "q", (0, 1024))],
    "kv_proj_dw": [("sc", "q", (1024, 1792))],
    "kv_proj_dx": [("sc", "q", (1792, 2048)), ("sc", "kv", (0, 768))],
    "ffn_down_dw0": [("sc", "kv", (768, 2048))],
    "ffn_down_dx0": [("sc", "dn0", (0, 384))],
    "convglu_bwd0": [("sc", "dn0", (384, 704))],
    "ffn_up_dx0": [("sc", "up0", (0, 1024))],
    "gla_out_dw": [("sc", "up0", (1024, 1216))],
    "gla_out_dx": [("sc", "up0", (1216, 1408))],
    "gla_bwd": [("sc", "up0", (1408, 2048))],
    "gla_in_dw": [("sc", "gout", None)],
    "gla_in_dx": [("sc", "in", (0, 1536))],
    "grads_tail": [("sc", "in", (1536, 2048)), ("all", "small", None)],
}
ROW_SHARDED = ("gout", "dout", "dn0", "dn1")


class Plan:
    def __init__(self, weights, srcs=None):
        self.w = dict(weights)
        self.srcs = srcs
        self.grads = {}
        self.recv = {}
        self._names = None

    def weight(self, name):
        buf = self.w[name]
        if name in ROW_SHARDED:
            return buf.reshape(1, buf.shape[0] * buf.shape[1], buf.shape[2])
        return buf

    def jobs(self, call):
        ops = SCHEDULE.get(call)
        if self.srcs is None or not ops:
            return None
        jobs, handles = Jobs(), {}
        backward = ops[0][0] in ("sc", "all")
        for op, name, rows in ops:
            assert (op in ("sc", "all")) == backward
            store = self.recv if backward else self.w
            if name not in handles:
                if name in store:
                    handles[name] = jobs.thru(store[name])
                elif op == "sc":
                    handles[name] = jobs.new(self.grads[name].shape, BF16)
                elif op == "all":
                    handles[name] = jobs.new((N_DEV,) + self.grads[name].shape, self.grads[name].dtype)
                else:
                    handles[name] = jobs.new((N_DEV,) + self.srcs[name].shape, BF16)
            if op == "g1":
                jobs.gather_ici(self.srcs[name], handles[name], rows)
            elif op == "g2":
                jobs.gather_d2d(handles[name], rows)
            else:
                jobs.scatter(self.grads[name], handles[name], rows, same=op == "all")
        self._names = [(name, self.recv if backward else self.w) for name in handles]
        return jobs

    def run(self, call, fn, *args, **kwargs):
        jobs = self.jobs(call)
        if jobs is None:
            return fn(*args, **kwargs)
        out, bufs = fn(*args, jobs=jobs, **kwargs)
        for (name, store), buf in zip(self._names, bufs):
            store[name] = buf
        return out


def _ffn_fwd(plan, h, norm_g, conv_w, conv_b, tag):
    (n,) = rms_fwd(h, [norm_g], name=f"ffn_norm_fwd{tag}")
    up = plan.run(f"ffn_up{tag}", mm_nn, n, plan.weight(f"up{tag}"), out_dtype=BF16, name=f"ffn_up{tag}")
    act = plan.run(f"convglu_fwd{tag}", convglu_fwd, up, conv_w, conv_b, name=f"convglu_fwd{tag}")
    h_out = plan.run(f"ffn_down{tag}", mm_nn, act, plan.weight(f"dn{tag}"), out_dtype=F32, res=h,
                     name=f"ffn_down{tag}")
    return h_out, (n, up, act)


def _by_rows(dw):
    return dw.reshape(N_DEV, dw.shape[1] // N_DEV, dw.shape[2])


def _ffn_bwd(plan, dh_out, h, saved, norm_g, conv_w, conv_b, tag):
    n, up, act = saved
    plan.grads[f"dn{tag}"] = _by_rows(plan.run(f"ffn_down_dw{tag}", mm_tn, act, dh_out, 1, name=f"ffn_down_dw{tag}"))
    dact = plan.run(f"ffn_down_dx{tag}", mm_nt, dh_out, plan.weight(f"dn{tag}"), out_dtype=BF16,
                    name=f"ffn_down_dx{tag}")
    dup, dconv_w, dconv_b = plan.run(f"convglu_bwd{tag}", convglu_bwd, up, conv_w, conv_b, dact,
                                     name=f"convglu_bwd{tag}")
    plan.grads[f"up{tag}"] = mm_tn(n, dup, N_DEV, name=f"ffn_up_dw{tag}")
    dh, (dnorm,) = plan.run(f"ffn_up_dx{tag}", mm_nt, dup, plan.weight(f"up{tag}"), out_dtype=F32,
                            name=f"ffn_up_dx{tag}", norm=(h, dh_out, [norm_g], []))
    return dh, dnorm, dconv_w, dconv_b


def local_step(x, target, wts, plan):
    row = lambda v: v.reshape(1, -1)
    attn_norm, ffn_norm = wts["attn_norm"], wts["ffn_norm"]
    conv_w, conv_b = wts["ffn_conv_w"], wts["ffn_conv_b"]

    (n1,) = rms_fwd(x, [row(attn_norm[0])], name="attn_norm_fwd0")
    proj = plan.run("gla_in", mm_nn, n1, wts["gla_w_in"], out_dtype=F32, name="gla_in")
    la = gate_fwd(proj, wts["gla_w_a2"], wts["gla_b_a2"])
    o_gla, states, og = plan.run("gla_fwd", gla_fwd, proj, la, wts["gla_head_norm"])
    h1 = plan.run("gla_out", mm_nn, og, plan.weight("gout"), out_dtype=F32, res=x, name="gla_out")
    h2, ffn0 = _ffn_fwd(plan, h1, row(ffn_norm[0]), conv_w[0], row(conv_b[0]), "0")

    h2s = to_streams(h2, name="h2_to_streams")
    kvn, n3 = rms_fwd(h2s, [row(wts["kv_norm"]), row(attn_norm[1])], name="kv_attn_norm_fwd")
    kv = plan.run("kv_proj", mm_nn, kvn, plan.weight("kv"), out_dtype=F32, name="kv_proj")
    q = plan.run("q_proj", mm_nn, n3, plan.weight("q"), out_dtype=F32, name="q_proj")
    o_att, lse = plan.run("attn_fwd", attn_fwd, q, kv)
    h3 = from_streams(plan.run("dsa_out", mm_nn, o_att, plan.weight("dout"), out_dtype=F32, res=h2s, name="dsa_out"),
                      name="h3_from_streams")
    h4, ffn1 = _ffn_fwd(plan, h3, row(ffn_norm[1]), conv_w[1], row(conv_b[1]), "1")

    loss_tile, dh4, d_final = loss_head(h4, row(wts["final_norm"]), target)

    dh3, d_ffn1, dcw1, dcb1 = _ffn_bwd(plan, dh4, h3, ffn1, row(ffn_norm[1]), conv_w[1], row(conv_b[1]), "1")
    dh3s = to_streams(dh3, name="dh3_to_streams")
    plan.grads["dout"] = _by_rows(mm_tn(o_att, dh3s, 1, name="dsa_out_dw"))
    do_att = mm_nt(dh3s, plan.weight("dout"), out_dtype=F32, name="dsa_out_dx")
    dq, dkv = plan.run("attn_bwd", attn_bwd, q, kv, o_att, lse, do_att)
    plan.grads["q"] = mm_tn(n3, dq, N_DEV, name="q_proj_dw")
    dh2_part, (d_attn1,) = plan.run("q_proj_dx", mm_nt, dq, plan.weight("q"), out_dtype=F32, name="q_proj_dx",
                                    norm=(h2s, dh3s, [row(attn_norm[1])], []))
    plan.grads["kv"] = plan.run("kv_proj_dw", mm_tn, kvn, dkv, N_DEV, name="kv_proj_dw")
    dh2s, (d_kvnorm,) = plan.run("kv_proj_dx", mm_nt, dkv, plan.weight("kv"), out_dtype=F32, name="kv_proj_dx",
                                 norm=(h2s, dh2_part, [row(wts["kv_norm"])], []))
    dh2 = from_streams(dh2s, name="dh2_from_streams")
    dh1, d_ffn0, dcw0, dcb0 = _ffn_bwd(plan, dh2, h1, ffn0, row(ffn_norm[0]), conv_w[0], row(conv_b[0]), "0")
    plan.grads["gout"] = _by_rows(plan.run("gla_out_dw", mm_tn, og, dh1, 1, name="gla_out_dw"))
    dog = plan.run("gla_out_dx", mm_nt, dh1, plan.weight("gout"), out_dtype=F32, name="gla_out_dx")
    dq_g, dk_g, dv_g, dr, dla, d_hn = plan.run("gla_bwd", gla_bwd, proj, la, states, o_gla, wts["gla_head_norm"], dog)
    da, dw_a2p, db_a2 = gate_bwd(proj, wts["gla_w_a2"], wts["gla_b_a2"], dla)
    dproj = jnp.concatenate([dq_g, dk_g, dv_g, dr, da], axis=1)
    assert dproj.shape[1] == GLA_IN_PAD
    dw_in = plan.run("gla_in_dw", mm_tn, n1, dproj, 1, name="gla_in_dw")
    plan.grads["in"] = dw_in[0, :, :GLA_IN_DIM].reshape(D_MODEL, N_DEV, GLA_IN_DIM // N_DEV).transpose(1, 0, 2)
    grad_x, (d_attn0,) = plan.run("gla_in_dx", mm_nt, dproj, wts["gla_w_in"], out_dtype=F32, name="gla_in_dx",
                                  norm=(x, dh1, [row(attn_norm[0])], []))

    small = dict(
        attn_norm=jnp.concatenate([d_attn0, d_attn1], axis=0),
        ffn_norm=jnp.concatenate([d_ffn0, d_ffn1], axis=0),
        kv_norm=d_kvnorm.reshape(-1),
        final_norm=d_final.reshape(-1),
        ffn_conv_b=jnp.concatenate([dcb0, dcb1], axis=0),
        gla_w_a2=dw_a2p[:GATE_RANK],
        gla_b_a2=db_a2,
        gla_head_norm=d_hn,
        ffn_conv_w=jnp.stack([dcw0, dcw1]),
    )
    return loss_tile, grad_x, small


SMALL_ORDER = ("attn_norm", "ffn_norm", "kv_norm", "final_norm", "ffn_conv_b",
               "gla_w_a2", "gla_b_a2", "gla_head_norm", "ffn_conv_w")
SMALL_FULL = dict(attn_norm=(2, D_MODEL), ffn_norm=(2, D_MODEL), kv_norm=(D_MODEL,), final_norm=(D_MODEL,),
                  ffn_conv_b=(2, D_FF), gla_w_a2=(GATE_RANK, GLA_KEY_DIM), gla_b_a2=(1, GLA_KEY_DIM),
                  gla_head_norm=(1, GLA_DV), ffn_conv_w=(2, 3, D_FF))
SMALL_SHARDED = ("gla_w_a2", "gla_b_a2", "gla_head_norm", "ffn_conv_w")
SMALL_GRAD_ROWS = 592
SMALL_ADAM_ROWS = 240


def kernel(x, attn_norm, gla_w_in, gla_w_a2, gla_b_a2, gla_head_norm, gla_w_out, kv_norm, w_kv, dsa_w_q, dsa_w_out, ffn_norm, ffn_w_up, ffn_conv_w, ffn_conv_b, ffn_w_down, final_norm, loss_target, m_attn_norm, m_gla_w_in, m_gla_w_a2, m_gla_b_a2, m_gla_head_norm, m_gla_w_out, m_kv_norm, m_w_kv, m_dsa_w_q, m_dsa_w_out, m_ffn_norm, m_ffn_w_up, m_ffn_conv_w, m_ffn_conv_b, m_ffn_w_down, m_final_norm, v_attn_norm, v_gla_w_in, v_gla_w_a2, v_gla_b_a2, v_gla_head_norm, v_gla_w_out, v_kv_norm, v_w_kv, v_dsa_w_q, v_dsa_w_out, v_ffn_norm, v_ffn_w_up, v_ffn_conv_w, v_ffn_conv_b, v_ffn_w_down, v_final_norm):
    me = 4 * lax.axis_index("x") + 2 * lax.axis_index("y") + lax.axis_index("c")
    bf = lambda a: a.astype(BF16)

    g_in, g_small = all_gather([bf(gla_w_in[0]), _pack_small_weights(gla_w_a2, gla_b_a2, gla_head_norm, ffn_conv_w)],
                               name="gather_first")
    w_a2_full, b_a2_full, hn_full, conv_w_full = _unpack_small_weights(g_small)
    w_in_full = jnp.pad(g_in.transpose(1, 0, 2).reshape(D_MODEL, GLA_IN_DIM), ((0, 0), (0, GLA_IN_PAD - GLA_IN_DIM)))
    wts = dict(
        attn_norm=attn_norm, ffn_norm=ffn_norm, kv_norm=kv_norm, final_norm=final_norm, ffn_conv_b=ffn_conv_b,
        gla_w_in=w_in_full[None],
        gla_w_a2=jnp.pad(bf(w_a2_full), ((0, LANE - GATE_RANK), (0, 0))),
        gla_b_a2=b_a2_full, gla_head_norm=hn_full, ffn_conv_w=conv_w_full,
    )
    plan = Plan({}, srcs=dict(gout=bf(gla_w_out[0]), kv=bf(w_kv), q=bf(dsa_w_q[0]), dout=bf(dsa_w_out[0]),
                              up0=bf(ffn_w_up[0]), up1=bf(ffn_w_up[1]), dn0=bf(ffn_w_down[0]), dn1=bf(ffn_w_down[1])))

    loss_tile, grad_x, small = local_step(x[0], loss_target[0], wts, plan)
    loss = lax.psum(loss_tile[0, 0], ("x", "y", "c"))

    plan.grads["small"] = _pack_rows([small[nm] for nm in SMALL_ORDER], SMALL_GRAD_ROWS)
    plan.run("grads_tail", exchange_only, name="grads_tail")
    shard3 = lambda a: a.reshape((-1,) + a.shape[-2:])
    big_params = dict(gla_w_in=(("in",), gla_w_in, m_gla_w_in, v_gla_w_in),
                      gla_w_out=(("gout",), gla_w_out, m_gla_w_out, v_gla_w_out),
                      w_kv=(("kv",), w_kv, m_w_kv, v_w_kv),
                      dsa_w_q=(("q",), dsa_w_q, m_dsa_w_q, v_dsa_w_q),
                      dsa_w_out=(("dout",), dsa_w_out, m_dsa_w_out, v_dsa_w_out),
                      ffn_w_up=(("up0", "up1"), ffn_w_up, m_ffn_w_up, v_ffn_w_up),
                      ffn_w_down=(("dn0", "dn1"), ffn_w_down, m_ffn_w_down, v_ffn_w_down))
    res = {}
    for nm, (parts, w, m, v) in big_params.items():
        outs = adam_sharded([plan.recv[p] for p in parts], shard3(w), shard3(m), shard3(v), name=f"adam_{nm}")
        res[nm] = [o.reshape(w.shape) for o in outs]

    full = dict(zip(SMALL_ORDER, _unpack_rows(sum_partials(plan.recv["small"]),
                                              [SMALL_FULL[nm] for nm in SMALL_ORDER])))
    local_w = dict(attn_norm=attn_norm, ffn_norm=ffn_norm, kv_norm=kv_norm, final_norm=final_norm,
                   ffn_conv_b=ffn_conv_b, gla_w_a2=gla_w_a2, gla_b_a2=gla_b_a2, gla_head_norm=gla_head_norm,
                   ffn_conv_w=ffn_conv_w)
    local_m = dict(attn_norm=m_attn_norm, ffn_norm=m_ffn_norm, kv_norm=m_kv_norm, final_norm=m_final_norm,
                   ffn_conv_b=m_ffn_conv_b, gla_w_a2=m_gla_w_a2, gla_b_a2=m_gla_b_a2, gla_head_norm=m_gla_head_norm,
                   ffn_conv_w=m_ffn_conv_w)
    local_v = dict(attn_norm=v_attn_norm, ffn_norm=v_ffn_norm, kv_norm=v_kv_norm, final_norm=v_final_norm,
                   ffn_conv_b=v_ffn_conv_b, gla_w_a2=v_gla_w_a2, gla_b_a2=v_gla_b_a2, gla_head_norm=v_gla_head_norm,
                   ffn_conv_w=v_ffn_conv_w)
    local_g = {}
    for nm in SMALL_ORDER:
        gfull = full[nm]
        if nm in SMALL_SHARDED:
            per = gfull.shape[-1] // N_DEV
            gfull = lax.dynamic_slice_in_dim(gfull, me * per, per, axis=gfull.ndim - 1)
        local_g[nm] = gfull.reshape(local_w[nm].shape)
    shapes = [local_w[nm].shape for nm in SMALL_ORDER]
    pk = lambda dd: _pack_rows([dd[nm] for nm in SMALL_ORDER], SMALL_ADAM_ROWS)
    d_p, m_p, v_p = adam_packed(pk(local_w), pk(local_g), pk(local_m), pk(local_v))
    for nm, dl, mn, vn in zip(SMALL_ORDER, _unpack_rows(d_p, shapes), _unpack_rows(m_p, shapes),
                              _unpack_rows(v_p, shapes)):
        res[nm] = [local_g[nm], dl, mn, vn]

    order = ("attn_norm", "gla_w_in", "gla_w_a2", "gla_b_a2", "gla_head_norm", "gla_w_out", "kv_norm", "w_kv",
             "dsa_w_q", "dsa_w_out", "ffn_norm", "ffn_w_up", "ffn_conv_w", "ffn_conv_b", "ffn_w_down", "final_norm")
    outs = [loss, grad_x[None]]
    for kind in range(4):
        outs.extend(res[nm][kind] for nm in order)
    return tuple(outs)
```

```python
import functools

import jax
import jax.numpy as jnp
from jax import lax
from jax.experimental import pallas as pl
from jax.experimental.pallas import tpu as pltpu

F32 = jnp.float32
BF16 = jnp.bfloat16
MESH = pl.DeviceIdType.MESH
ANY = pl.BlockSpec(memory_space=pl.ANY)

N_DEV = 8
D_MODEL = 2048
GLA_HEADS = 4
GLA_KEY_DIM = 1024
GLA_VAL_DIM = 2048
GLA_DK = 256
GLA_DV = 512
GATE_RANK = 16
GATE_NORMALIZER = 16.0
GLA_CHUNK = 64
GLA_STEP_CHUNKS = 2
GLA_IN_DIM = 2 * GLA_KEY_DIM + 2 * GLA_VAL_DIM + GATE_RANK
GLA_IN_PAD = 6272
ATT_HEADS = 16
HEAD_DIM = 128
DILATIONS = (1, 4, 16)
STREAMS = DILATIONS[-1]
ATT_BLOCK = 128
D_FF = 5632
EPS = 1e-6
ADAM_LR = 0.001
ADAM_B1 = 0.9
ADAM_B2 = 0.999
ADAM_EPS = 1e-08
ADAM_WD = 0.01
ADAM_STEP = 10
NEG = -1e30
LANE = 128
NORM_ROWS = 64
VMEM_LIMIT = 52 * 1024 * 1024
ALIBI_SLOPES = tuple(2.0 ** (-0.5 * (i + 1)) for i in range(ATT_HEADS))


def _params(*sem):
    return pltpu.CompilerParams(dimension_semantics=sem, vmem_limit_bytes=VMEM_LIMIT)


def _tile(n, cap):
    best = None
    for t in range(LANE, min(n, cap) + 1, LANE):
        if n % t == 0:
            best = t
    return best if best is not None else n


def _shard_group(j, ns, cap):
    best = 1
    for g in range(1, j + 1):
        if j % g == 0 and g * ns <= cap:
            best = g
    return best


def _rows(r, c, budget=256 * 1024):
    best = None
    for t in range(16, r + 1, 16):
        if r % t == 0 and t * c <= budget:
            best = t
    return best if best is not None else r


def _flip(coord, bit):
    return 1 - coord if bit else coord


def _place():
    x, y, c = lax.axis_index("x"), lax.axis_index("y"), lax.axis_index("c")
    return x, y, c, 4 * x + 2 * y + c


def _rows_of(ref, rows):
    return ref if rows is None else ref.at[pl.ds(rows[0], rows[1] - rows[0])]


class Jobs:
    def __init__(self):
        self.srcs = []
        self.bufs = []
        self.sems = []
        self.steps = []

    def _src(self, a):
        for i, b in enumerate(self.srcs):
            if b is a:
                return i
        self.srcs.append(a)
        return len(self.srcs) - 1

    def new(self, shape, dtype):
        self.bufs.append((None, jax.ShapeDtypeStruct(shape, dtype)))
        return len(self.bufs) - 1

    def thru(self, a):
        self.bufs.append((a, jax.ShapeDtypeStruct(a.shape, a.dtype)))
        return len(self.bufs) - 1

    def _sem(self, n):
        self.sems.append(pltpu.SemaphoreType.DMA((n,)))
        return len(self.sems) - 1

    def gather_ici(self, src, buf, rows=None):
        si, send, recv, loc = self._src(src), self._sem(4), self._sem(4), self._sem(1)

        def remote(srcs, bufs, sems, slot_of):
            x, y, c, me = _place()
            peers = [(x, y, 1 - c), (1 - x, y, c), (x, 1 - y, c), (1 - x, 1 - y, c)]
            return [pltpu.make_async_remote_copy(
                src_ref=_rows_of(srcs[si], rows),
                dst_ref=_rows_of(bufs[buf].at[me if slot_of == "mine" else 4 * p[0] + 2 * p[1] + p[2]], rows),
                send_sem=sems[send].at[k], recv_sem=sems[recv].at[k], device_id=p, device_id_type=MESH)
                for k, p in enumerate(peers)]

        def local(srcs, bufs, sems):
            return pltpu.make_async_copy(_rows_of(srcs[si], rows), _rows_of(bufs[buf].at[_place()[3]], rows),
                                         sems[loc].at[0])

        def start(srcs, bufs, sems):
            local(srcs, bufs, sems).start()
            for cp in remote(srcs, bufs, sems, "mine"):
                cp.start()

        def finish(srcs, bufs, sems):
            for cp in remote(srcs, bufs, sems, "peer"):
                cp.wait_recv()
            for cp in remote(srcs, bufs, sems, "mine"):
                cp.wait_send()
            local(srcs, bufs, sems).wait()

        self.steps.append((start, finish))

    def gather_d2d(self, buf, rows=None):
        send, recv = self._sem(3), self._sem(3)

        def copies(bufs, sems, core):
            x, y, c, _ = _place()
            cc = c if core == "mine" else 1 - c
            chips = [(1 - x, y), (x, 1 - y), (1 - x, 1 - y)]
            return [pltpu.make_async_remote_copy(
                src_ref=_rows_of(bufs[buf].at[4 * px + 2 * py + cc], rows),
                dst_ref=_rows_of(bufs[buf].at[4 * px + 2 * py + cc], rows),
                send_sem=sems[send].at[k], recv_sem=sems[recv].at[k],
                device_id=(x, y, 1 - c), device_id_type=MESH) for k, (px, py) in enumerate(chips)]

        def start(srcs, bufs, sems):
            for cp in copies(bufs, sems, "mine"):
                cp.start()

        def finish(srcs, bufs, sems):
            for cp in copies(bufs, sems, "sibling"):
                cp.wait_recv()
            for cp in copies(bufs, sems, "mine"):
                cp.wait_send()

        self.steps.append((start, finish))

    def scatter(self, src, buf, rows=None, same=False):
        si, send, recv, loc = self._src(src), self._sem(N_DEV - 1), self._sem(N_DEV - 1), self._sem(1)

        def block(srcs, dev):
            return _rows_of(srcs[si] if same else srcs[si].at[dev], rows)

        def remote(srcs, bufs, sems, slot_of):
            x, y, c, me = _place()
            out = []
            for k in range(1, N_DEV):
                px, py, pc = _flip(x, k >> 2), _flip(y, (k >> 1) & 1), _flip(c, k & 1)
                peer = 4 * px + 2 * py + pc
                out.append(pltpu.make_async_remote_copy(
                    src_ref=block(srcs, peer),
                    dst_ref=_rows_of(bufs[buf].at[me if slot_of == "mine" else peer], rows),
                    send_sem=sems[send].at[k - 1], recv_sem=sems[recv].at[k - 1],
                    device_id=(px, py, pc), device_id_type=MESH))
            return out

        def local(srcs, bufs, sems):
            me = _place()[3]
            return pltpu.make_async_copy(block(srcs, me), _rows_of(bufs[buf].at[me], rows), sems[loc].at[0])

        def start(srcs, bufs, sems):
            local(srcs, bufs, sems).start()
            for cp in remote(srcs, bufs, sems, "mine"):
                cp.start()

        def finish(srcs, bufs, sems):
            for cp in remote(srcs, bufs, sems, "peer"):
                cp.wait_recv()
            for cp in remote(srcs, bufs, sems, "mine"):
                cp.wait_send()
            local(srcs, bufs, sems).wait()

        self.steps.append((start, finish))


def _call(body, *, name, grid, in_specs, out_specs, out_shape, args, sem, scratch_shapes=(), jobs=None):
    in_specs, out_specs, out_shape = list(in_specs), list(out_specs), list(out_shape)
    scratch_shapes = list(scratch_shapes)
    if jobs is None:
        res = pl.pallas_call(body, name=name, out_shape=out_shape, grid=grid, in_specs=in_specs,
                             out_specs=out_specs, scratch_shapes=scratch_shapes,
                             compiler_params=_params(*sem))(*args)
        return list(res), []
    thru = [a for a, _ in jobs.bufs if a is not None]
    n_in, n_src, n_thru = len(args), len(jobs.srcs), len(thru)
    n_out, n_buf, n_scr = len(out_shape), len(jobs.bufs), len(scratch_shapes)
    aliases, t = {}, 0
    for b, (a, _) in enumerate(jobs.bufs):
        if a is not None:
            aliases[n_in + n_src + t] = n_out + b
            t += 1

    def wrapped(*refs):
        at = 0
        ins = refs[at:at + n_in]; at += n_in
        srcs = refs[at:at + n_src]; at += n_src + n_thru
        outs = refs[at:at + n_out]; at += n_out
        bufs = refs[at:at + n_buf]; at += n_buf
        scr = refs[at:at + n_scr]; at += n_scr
        sems = refs[at:]
        first, last = None, None
        for axis, size in enumerate(grid):
            pid = pl.program_id(axis)
            f, l = pid == 0, pid == size - 1
            first = f if first is None else first & f
            last = l if last is None else last & l

        @pl.when(first)
        def _():
            for start, _ in jobs.steps:
                start(srcs, bufs, sems)

        body(*ins, *outs, *scr)

        @pl.when(last)
        def _():
            for _, finish in jobs.steps:
                finish(srcs, bufs, sems)

    res = pl.pallas_call(
        wrapped, name=name,
        out_shape=out_shape + [s for _, s in jobs.bufs],
        grid=grid,
        in_specs=in_specs + [ANY] * (n_src + n_thru),
        out_specs=out_specs + [ANY] * n_buf,
        scratch_shapes=scratch_shapes + jobs.sems,
        input_output_aliases=aliases,
        compiler_params=_params(*(["arbitrary"] * len(grid))),
    )(*args, *jobs.srcs, *thru)
    return res[:n_out], res[n_out:]


def mm_nn(a, w, *, out_dtype, name, res=None, tm=None, jobs=None):
    m, k = a.shape
    j, k2, ns = w.shape
    whole = j == 1 and ns <= 2048 and k <= 2048
    tm = tm or (1024 if a.dtype == BF16 and not whole else 512)
    assert k == k2 and m % tm == 0
    tn = ns if whole else _tile(ns, 1408)
    nsub = ns // tn
    tk = k if k <= 2048 else _tile(k, 1408)
    nk = k // tk
    has_res = res is not None

    def body(*refs):
        if has_res:
            a_ref, w_ref, r_ref, o_ref, acc = refs
        else:
            a_ref, w_ref, o_ref, acc = refs
        kk = pl.program_id(2)

        @pl.when(kk == 0)
        def _():
            acc[...] = jnp.zeros_like(acc)

        acc[...] += jnp.dot(a_ref[...].astype(BF16), w_ref[...], preferred_element_type=F32)

        @pl.when(kk == nk - 1)
        def _():
            r = acc[...]
            if has_res:
                r = r + r_ref[...]
            o_ref[...] = r.astype(out_dtype)

    in_specs = [
        pl.BlockSpec((tm, tk), lambda i, n, kk: (i, kk)),
        pl.BlockSpec((None, tk, tn), lambda i, n, kk: (n // nsub, kk, n % nsub)),
    ]
    args = [a, w]
    out_tile = pl.BlockSpec((tm, tn), lambda i, n, kk: (i, n))
    if has_res:
        in_specs.append(out_tile)
        args.append(res)
    (out,), bufs = _call(
        body, name=name, jobs=jobs,
        out_shape=[jax.ShapeDtypeStruct((m, j * ns), out_dtype)],
        grid=(m // tm, j * nsub, nk),
        in_specs=in_specs,
        out_specs=[out_tile],
        scratch_shapes=[pltpu.VMEM((tm, tn), F32)],
        args=args, sem=("parallel", "parallel", "arbitrary"))
    return out if jobs is None else (out, bufs)


def mm_nt(dy, w, *, out_dtype, name, tm=None, jobs=None, norm=None):
    parts, m, n = (1,) + dy.shape if dy.ndim == 2 else dy.shape
    n *= parts
    j, k, ns = w.shape
    if norm is not None:
        x, dres, gains, more = norm
        tm = tm or (256 if more else 512)
    tm = tm or 1024
    assert n == j * ns and m % tm == 0
    tn = _tile(ns, 2048)
    nsub = ns // tn
    jb = _shard_group(j // parts, ns, 2048 if norm is None else 1024) if nsub == 1 else 1
    tko = _tile(k, 1408) if norm is None else k
    nn = j * nsub // jb
    per_part = nn // parts
    if dy.ndim == 2:
        dy_spec = pl.BlockSpec((tm, jb * tn), lambda i, ko, nq: (i, nq))
    else:
        dy_spec = pl.BlockSpec((None, tm, jb * tn), lambda i, ko, nq: (nq // per_part, i, nq % per_part))
    if jb == 1:
        w_spec = pl.BlockSpec((None, tko, tn), lambda i, ko, nq: (nq // nsub, ko, nq % nsub))
    else:
        w_spec = pl.BlockSpec((jb, tko, ns), lambda i, ko, nq: (nq, ko, 0))

    n_gain = 0 if norm is None else len(gains)
    n_more = 0 if norm is None else len(more)

    def body(*refs):
        a_ref, w_ref = refs[:2]
        acc = refs[-1]
        nq = pl.program_id(2)
        first = pl.program_id(0) == 0

        @pl.when(nq == 0)
        def _():
            acc[...] = jnp.zeros_like(acc)

        if jb == 1:
            acc[...] += lax.dot_general(a_ref[...].astype(BF16), w_ref[...], (((1,), (1,)), ((), ())),
                                        preferred_element_type=F32)
        else:
            part = acc[...]
            for jj in range(jb):
                part = part + lax.dot_general(a_ref[:, jj * ns:(jj + 1) * ns].astype(BF16), w_ref[jj],
                                              (((1,), (1,)), ((), ())), preferred_element_type=F32)
            acc[...] = part

        @pl.when(nq == nn - 1)
        def _():
            if norm is None:
                refs[2][...] = acc[...].astype(out_dtype)
                return
            x_ref, r_ref = refs[2:4]
            g_refs = refs[4:4 + n_gain]
            e_refs = refs[4 + n_gain:4 + n_gain + n_more]
            dx_ref = refs[4 + n_gain + n_more]
            dg_refs = refs[5 + n_gain + n_more:-1]

            @pl.when(first)
            def _():
                for dg_ref in dg_refs:
                    dg_ref[...] = jnp.zeros_like(dg_ref)

            def rows(c, carry):
                sl = pl.ds(pl.multiple_of(c * NORM_ROWS, NORM_ROWS), NORM_ROWS)
                xv = x_ref[sl, :]
                r = lax.rsqrt(jnp.mean(xv * xv, axis=-1, keepdims=True) + EPS)
                xh = xv * r
                out = r_ref[sl, :]
                for idx, (g_ref, dg_ref) in enumerate(zip(g_refs, dg_refs)):
                    dyv = acc[sl, :] if idx == 0 else e_refs[idx - 1][sl, :].astype(F32)
                    dg_ref[...] += jnp.sum(dyv * xh, axis=0, keepdims=True)
                    dxh = dyv * g_ref[...]
                    out = out + r * (dxh - xh * jnp.mean(dxh * xh, axis=-1, keepdims=True))
                dx_ref[sl, :] = out
                return carry

            lax.fori_loop(0, tm // NORM_ROWS, rows, 0)

    out_tile = pl.BlockSpec((tm, tko), lambda i, ko, nq: (i, ko))
    in_specs, args = [dy_spec, w_spec], [dy, w]
    out_shape, out_specs = [jax.ShapeDtypeStruct((m, k), out_dtype)], [out_tile]
    sem = ("parallel", "parallel", "arbitrary")
    if norm is not None:
        vec = pl.BlockSpec((1, k), lambda i, ko, nq: (0, 0))
        in_specs += [out_tile, out_tile] + [vec] * n_gain + [out_tile] * n_more
        args += [x, dres] + list(gains) + list(more)
        out_shape = [jax.ShapeDtypeStruct((m, k), F32)] + [jax.ShapeDtypeStruct((1, k), F32)] * n_gain
        out_specs = [out_tile] + [vec] * n_gain
        sem = ("arbitrary", "arbitrary", "arbitrary")
    outs, bufs = _call(
        body, name=name, jobs=jobs, out_shape=out_shape, grid=(m // tm, k // tko, nn),
        in_specs=in_specs, out_specs=out_specs, scratch_shapes=[pltpu.VMEM((tm, tko), F32)], args=args, sem=sem)
    out = outs[0] if norm is None else (outs[0], outs[1:])
    return out if jobs is None else (out, bufs)


def mm_tn(x, dy, j, *, name, tm=1024, jobs=None):
    m, k = x.shape
    parts, m2, n = (1,) + dy.shape if dy.ndim == 2 else dy.shape
    n *= parts
    assert m == m2 and n % j == 0 and m % tm == 0
    ns = n // j
    tn = _tile(ns, 1408)
    nsub = ns // tn
    jb = _shard_group(j // parts, ns, 1536) if nsub == 1 else 1
    tk = _tile(k, 1408)
    nm = m // tm
    n_steps = j * nsub // jb
    per_part = n_steps // parts
    if dy.ndim == 2:
        dy_spec = pl.BlockSpec((tm, jb * tn), lambda kq, nq, mi: (mi, nq))
    else:
        dy_spec = pl.BlockSpec((None, tm, jb * tn), lambda kq, nq, mi: (nq // per_part, mi, nq % per_part))
    if jb == 1:
        out_spec = pl.BlockSpec((None, tk, tn), lambda kq, nq, mi: (nq // nsub, kq, nq % nsub))
        acc_shape = (tk, tn)
    else:
        out_spec = pl.BlockSpec((jb, tk, ns), lambda kq, nq, mi: (nq, kq, 0))
        acc_shape = (jb, tk, ns)

    def body(x_ref, dy_ref, o_ref, acc):
        mi = pl.program_id(2)

        @pl.when(mi == 0)
        def _():
            acc[...] = jnp.zeros_like(acc)

        xb = x_ref[...].astype(BF16)
        if jb == 1:
            acc[...] += lax.dot_general(xb, dy_ref[...].astype(BF16), (((0,), (0,)), ((), ())),
                                        preferred_element_type=F32)
        else:
            for jj in range(jb):
                acc[jj] += lax.dot_general(xb, dy_ref[:, jj * ns:(jj + 1) * ns].astype(BF16),
                                           (((0,), (0,)), ((), ())), preferred_element_type=F32)

        @pl.when(mi == nm - 1)
        def _():
            o_ref[...] = acc[...].astype(BF16)

    (out,), bufs = _call(
        body, name=name, jobs=jobs,
        out_shape=[jax.ShapeDtypeStruct((j, k, ns), BF16)],
        grid=(k // tk, n_steps, nm),
        in_specs=[
            pl.BlockSpec((tm, tk), lambda kq, nq, mi: (mi, kq)),
            dy_spec,
        ],
        out_specs=[out_spec],
        scratch_shapes=[pltpu.VMEM(acc_shape, F32)],
        args=[x, dy], sem=("parallel", "parallel", "arbitrary"))
    return out if jobs is None else (out, bufs)


STREAM_TC = LANE


def to_streams(x, *, name):
    s, c = x.shape
    per = s // STREAMS

    def body(x_ref, o_ref):
        for st in range(STREAMS):
            o_ref[pl.ds(st * per, per), :] = x_ref[pl.ds(st, per, stride=STREAMS), :]

    blk = pl.BlockSpec((s, STREAM_TC), lambda i: (0, i))
    return pl.pallas_call(body, name=name, out_shape=jax.ShapeDtypeStruct((s, c), x.dtype), grid=(c // STREAM_TC,),
                          in_specs=[blk], out_specs=blk, compiler_params=_params("parallel"))(x)


def from_streams(x, *, name):
    s, c = x.shape
    per = s // STREAMS

    def body(x_ref, o_ref):
        for st in range(STREAMS):
            o_ref[pl.ds(st, per, stride=STREAMS), :] = x_ref[pl.ds(st * per, per), :]

    blk = pl.BlockSpec((s, STREAM_TC), lambda i: (0, i))
    return pl.pallas_call(body, name=name, out_shape=jax.ShapeDtypeStruct((s, c), x.dtype), grid=(c // STREAM_TC,),
                          in_specs=[blk], out_specs=blk, compiler_params=_params("parallel"))(x)


def rms_fwd(x, gains, *, name, ts=512):
    s, d = x.shape
    n = len(gains)

    def body(x_ref, *refs):
        xv = x_ref[...]
        xh = xv * lax.rsqrt(jnp.mean(xv * xv, axis=-1, keepdims=True) + EPS)
        for g_ref, o_ref in zip(refs[:n], refs[n:]):
            o_ref[...] = (xh * g_ref[...]).astype(BF16)

    row = pl.BlockSpec((ts, d), lambda i: (i, 0))
    vec = pl.BlockSpec((1, d), lambda i: (0, 0))
    return pl.pallas_call(
        body,
        name=name,
        out_shape=[jax.ShapeDtypeStruct((s, d), BF16)] * n,
        grid=(s // ts,),
        in_specs=[row] + [vec] * n,
        out_specs=[row] * n,
        compiler_params=_params("parallel"),
    )(x, *gains)


def loss_head(h, gain, target, *, ts=256):
    s, d = h.shape

    def body(h_ref, g_ref, t_ref, l_ref, dh_ref, dg_ref):
        i = pl.program_id(0)

        @pl.when(i == 0)
        def _():
            l_ref[...] = jnp.zeros_like(l_ref)
            dg_ref[...] = jnp.zeros_like(dg_ref)

        xv = h_ref[...]
        r = lax.rsqrt(jnp.mean(xv * xv, axis=-1, keepdims=True) + EPS)
        xh = xv * r
        g = g_ref[...]
        err = xh * g - t_ref[...]
        l_ref[...] += 0.5 * jnp.sum(jnp.mean(err * err, axis=-1, keepdims=True))
        dy = err * (1.0 / d)
        dg_ref[...] += jnp.sum(dy * xh, axis=0, keepdims=True)
        dxh = dy * g
        dh_ref[...] = r * (dxh - xh * jnp.mean(dxh * xh, axis=-1, keepdims=True))

    row = pl.BlockSpec((ts, d), lambda i: (i, 0))
    vec = pl.BlockSpec((1, d), lambda i: (0, 0))
    return pl.pallas_call(
        body,
        name="loss_head",
        out_shape=[jax.ShapeDtypeStruct((8, LANE), F32), jax.ShapeDtypeStruct((s, d), F32),
                   jax.ShapeDtypeStruct((1, d), F32)],
        grid=(s // ts,),
        in_specs=[row, vec, row],
        out_specs=[pl.BlockSpec((8, LANE), lambda i: (0, 0)), row, vec],
        compiler_params=_params("arbitrary"),
    )(h, gain, target)


A_BLOCK = (2 * GLA_KEY_DIM + 2 * GLA_VAL_DIM) // LANE


def gate_fwd(proj, w_a2p, b_a2, *, ts=512):
    s = proj.shape[0]

    def body(a_ref, w_ref, b_ref, o_ref):
        z = jnp.dot(a_ref[...].astype(BF16), w_ref[...], preferred_element_type=F32) + b_ref[...]
        o_ref[...] = (jnp.minimum(z, 0.0) - jnp.log(1.0 + jnp.exp(-jnp.abs(z)))) * (1.0 / GATE_NORMALIZER)

    return pl.pallas_call(
        body,
        name="gate_fwd",
        out_shape=jax.ShapeDtypeStruct((s, GLA_KEY_DIM), F32),
        grid=(s // ts,),
        in_specs=[pl.BlockSpec((ts, LANE), lambda i: (i, A_BLOCK)),
                  pl.BlockSpec((LANE, GLA_KEY_DIM), lambda i: (0, 0)),
                  pl.BlockSpec((1, GLA_KEY_DIM), lambda i: (0, 0))],
        out_specs=pl.BlockSpec((ts, GLA_KEY_DIM), lambda i: (i, 0)),
        compiler_params=_params("parallel"),
    )(proj, w_a2p, b_a2)


def gate_bwd(proj, w_a2p, b_a2, dla, *, ts=512):
    s = proj.shape[0]

    def body(a_ref, w_ref, b_ref, dla_ref, da_ref, dw_ref, db_ref):
        i = pl.program_id(0)

        @pl.when(i == 0)
        def _():
            dw_ref[...] = jnp.zeros_like(dw_ref)
            db_ref[...] = jnp.zeros_like(db_ref)

        a = a_ref[...].astype(BF16)
        w = w_ref[...]
        z = jnp.dot(a, w, preferred_element_type=F32) + b_ref[...]
        dz = dla_ref[...] * (1.0 / GATE_NORMALIZER) / (1.0 + jnp.exp(z))
        dzb = dz.astype(BF16)
        da_ref[...] = lax.dot_general(dzb, w, (((1,), (1,)), ((), ())), preferred_element_type=F32).astype(BF16)
        dw_ref[...] += lax.dot_general(a, dzb, (((0,), (0,)), ((), ())), preferred_element_type=F32)
        db_ref[...] += jnp.sum(dz, axis=0, keepdims=True)

    return pl.pallas_call(
        body,
        name="gate_bwd",
        out_shape=[jax.ShapeDtypeStruct((s, LANE), BF16), jax.ShapeDtypeStruct((LANE, GLA_KEY_DIM), F32),
                   jax.ShapeDtypeStruct((1, GLA_KEY_DIM), F32)],
        grid=(s // ts,),
        in_specs=[pl.BlockSpec((ts, LANE), lambda i: (i, A_BLOCK)),
                  pl.BlockSpec((LANE, GLA_KEY_DIM), lambda i: (0, 0)),
                  pl.BlockSpec((1, GLA_KEY_DIM), lambda i: (0, 0)),
                  pl.BlockSpec((ts, GLA_KEY_DIM), lambda i: (i, 0))],
        out_specs=[pl.BlockSpec((ts, LANE), lambda i: (i, 0)),
                   pl.BlockSpec((LANE, GLA_KEY_DIM), lambda i: (0, 0)),
                   pl.BlockSpec((1, GLA_KEY_DIM), lambda i: (0, 0))],
        compiler_params=_params("arbitrary"),
    )(proj, w_a2p, b_a2, dla)


def _masked_sum(mask, x):
    m = mask.astype(BF16)
    hi = x.astype(BF16)
    rest = x - hi.astype(F32)
    mid = rest.astype(BF16)
    lo = (rest - mid.astype(F32)).astype(BF16)
    dot = lambda t: jnp.dot(m, t, preferred_element_type=F32)
    return dot(hi) + dot(mid) + dot(lo)


def _chunk_terms(q, k, la):
    c_len = GLA_CHUNK
    row = lax.broadcasted_iota(jnp.int32, (c_len, c_len), 0)
    col = lax.broadcasted_iota(jnp.int32, (c_len, c_len), 1)
    tri = row >= col
    c = _masked_sum(tri, la)
    last = jnp.sum(la, axis=0, keepdims=True)
    q_dec = q * (GLA_DK ** -0.5) * jnp.exp(c)
    k_inv = k * jnp.exp(-c)
    k_end = k * jnp.exp(last - c)
    return c, last, q_dec, k_inv, k_end, tri


def _dot(a, b, ca, cb):
    return lax.dot_general(a.astype(BF16), b.astype(BF16), (((ca,), (cb,)), ((), ())), preferred_element_type=F32)


def gla_fwd(proj, la, hn, jobs=None):
    s = proj.shape[0]
    n_chunks = s // GLA_CHUNK
    rows = GLA_CHUNK * GLA_STEP_CHUNKS

    def body(q_ref, k_ref, v_ref, r_ref, la_ref, hn_ref, o_ref, st_out, og_ref, st):
        @pl.when(pl.program_id(0) == 0)
        def _():
            st[...] = jnp.zeros_like(st)

        for h in range(GLA_HEADS):
            hk = slice(h * GLA_DK, (h + 1) * GLA_DK)
            hv = slice(h * GLA_DV, (h + 1) * GLA_DV)
            for cc in range(GLA_STEP_CHUNKS):
                rs = slice(cc * GLA_CHUNK, (cc + 1) * GLA_CHUNK)
                _, last, q_dec, k_inv, k_end, tri = _chunk_terms(q_ref[rs, hk], k_ref[rs, hk], la_ref[rs, hk])
                v = v_ref[rs, hv]
                a = jnp.where(tri, _dot(q_dec, k_inv, 1, 1), 0.0)
                state = st[h]
                st_out[h, cc] = state
                ov = _dot(a, v, 1, 0) + _dot(q_dec, state, 1, 1)
                o_ref[rs, hv] = ov
                st[h] = state * jnp.exp(last) + _dot(v, k_end, 0, 0)
                oh = ov * lax.rsqrt(jnp.mean(ov * ov, axis=-1, keepdims=True) + EPS)
                r = r_ref[rs, hv]
                og_ref[rs, hv] = (oh * hn_ref[...] * (r * jax.nn.sigmoid(r))).astype(BF16)

    key = lambda col: pl.BlockSpec((rows, GLA_KEY_DIM), lambda n: (n, col))
    val = lambda col: pl.BlockSpec((rows, GLA_VAL_DIM), lambda n: (n, col))
    outs, bufs = _call(
        body, name="gla_fwd", jobs=jobs,
        out_shape=[jax.ShapeDtypeStruct((s, GLA_VAL_DIM), F32),
                   jax.ShapeDtypeStruct((GLA_HEADS, n_chunks, GLA_DV, GLA_DK), F32),
                   jax.ShapeDtypeStruct((s, GLA_VAL_DIM), BF16)],
        grid=(n_chunks // GLA_STEP_CHUNKS,),
        in_specs=[key(0), key(1), val(1), val(R_BLOCK // GLA_HEADS), key(0), pl.BlockSpec((1, GLA_DV), lambda n: (0, 0))],
        out_specs=[val(0), pl.BlockSpec((GLA_HEADS, GLA_STEP_CHUNKS, GLA_DV, GLA_DK), lambda n: (0, n, 0, 0)), val(0)],
        scratch_shapes=[pltpu.VMEM((GLA_HEADS, GLA_DV, GLA_DK), F32)],
        args=[proj, proj, proj, proj, la, hn], sem=("arbitrary",))
    return outs if jobs is None else (outs, bufs)


def gla_bwd(proj, la, states, o, hn, dog, jobs=None):
    s = proj.shape[0]
    n_steps = s // GLA_CHUNK // GLA_STEP_CHUNKS
    lastc = n_steps - 1
    rows = GLA_CHUNK * GLA_STEP_CHUNKS

    def body(q_ref, k_ref, v_ref, r_ref, la_ref, o_ref, hn_ref, dog_ref, st_ref,
             dq_ref, dk_ref, dv_ref, dr_ref, dla_ref, dhn_ref, dst):
        @pl.when(pl.program_id(0) == 0)
        def _():
            dst[...] = jnp.zeros_like(dst)
            dhn_ref[...] = jnp.zeros_like(dhn_ref)

        upper = (lax.broadcasted_iota(jnp.int32, (GLA_CHUNK, GLA_CHUNK), 0)
                 <= lax.broadcasted_iota(jnp.int32, (GLA_CHUNK, GLA_CHUNK), 1))
        gain = hn_ref[...]
        for h in range(GLA_HEADS):
            hk = slice(h * GLA_DK, (h + 1) * GLA_DK)
            hv = slice(h * GLA_DV, (h + 1) * GLA_DV)
            for cc in reversed(range(GLA_STEP_CHUNKS)):
                rs = slice(cc * GLA_CHUNK, (cc + 1) * GLA_CHUNK)
                ov = o_ref[rs, hv]
                inv = lax.rsqrt(jnp.mean(ov * ov, axis=-1, keepdims=True) + EPS)
                oh = ov * inv
                r = r_ref[rs, hv]
                sig = jax.nn.sigmoid(r)
                dgv = dog_ref[rs, hv]
                d_on = dgv * (r * sig)
                dr_ref[rs, hv] = (dgv * (oh * gain) * (sig * (1.0 + r * (1.0 - sig)))).astype(BF16)
                dhn_ref[...] += jnp.sum(d_on * oh, axis=0, keepdims=True)
                doh = d_on * gain
                dout = inv * (doh - oh * jnp.mean(doh * oh, axis=-1, keepdims=True))
                c, last, q_dec, k_inv, k_end, tri = _chunk_terms(q_ref[rs, hk], k_ref[rs, hk], la_ref[rs, hk])
                v = v_ref[rs, hv]
                state = st_ref[h, cc]
                dstate = dst[h]
                e_last = jnp.exp(last)
                a = jnp.where(tri, _dot(q_dec, k_inv, 1, 1), 0.0)
                da = jnp.where(tri, _dot(dout, v, 1, 1), 0.0)
                dv_ref[rs, hv] = (_dot(a, dout, 0, 0) + _dot(k_end, dstate, 1, 1)).astype(BF16)
                dq_dec = _dot(da, k_inv, 1, 0) + _dot(dout, state, 1, 0)
                dk_inv = _dot(da, q_dec, 0, 0)
                dk_end = _dot(v, dstate, 1, 0)
                dst[h] = dstate * e_last + _dot(dout, q_dec, 0, 0)
                dq_ref[rs, hk] = (dq_dec * (GLA_DK ** -0.5) * jnp.exp(c)).astype(BF16)
                dk_ref[rs, hk] = (dk_inv * jnp.exp(-c) + dk_end * jnp.exp(last - c)).astype(BF16)
                ke_term = dk_end * k_end
                dc = dq_dec * q_dec - dk_inv * k_inv - ke_term
                dlast = (jnp.sum(ke_term, axis=0, keepdims=True)
                         + e_last * jnp.sum(dstate * state, axis=0, keepdims=True))
                dla_ref[rs, hk] = _masked_sum(upper, dc) + dlast

    key = lambda col: pl.BlockSpec((rows, GLA_KEY_DIM), lambda n: (lastc - n, col))
    val = lambda col: pl.BlockSpec((rows, GLA_VAL_DIM), lambda n: (lastc - n, col))
    vec = pl.BlockSpec((1, GLA_DV), lambda n: (0, 0))
    outs, bufs = _call(
        body, name="gla_bwd", jobs=jobs,
        out_shape=[jax.ShapeDtypeStruct((s, GLA_KEY_DIM), BF16), jax.ShapeDtypeStruct((s, GLA_KEY_DIM), BF16),
                   jax.ShapeDtypeStruct((s, GLA_VAL_DIM), BF16), jax.ShapeDtypeStruct((s, GLA_VAL_DIM), BF16),
                   jax.ShapeDtypeStruct((s, GLA_KEY_DIM), F32), jax.ShapeDtypeStruct((1, GLA_DV), F32)],
        grid=(n_steps,),
        in_specs=[key(0), key(1), val(1), val(R_BLOCK // GLA_HEADS), key(0), val(0), vec, val(0),
                  pl.BlockSpec((GLA_HEADS, GLA_STEP_CHUNKS, GLA_DV, GLA_DK), lambda n: (0, lastc - n, 0, 0))],
        out_specs=[key(0), key(0), val(0), val(0), key(0), vec],
        scratch_shapes=[pltpu.VMEM((GLA_HEADS, GLA_DV, GLA_DK), F32)],
        args=[proj, proj, proj, proj, la, o, hn, dog, states], sem=("arbitrary",))
    return outs if jobs is None else (outs, bufs)


R_BLOCK = (2 * GLA_KEY_DIM + GLA_VAL_DIM) // GLA_DV


CONV_TC = 128
SQRT_HALF = 0.7071067811865476
INV_SQRT_2PI = 0.3989422804014327


def _conv_gate(g_ref, cw_ref, cb_ref):
    g0 = g_ref[...].astype(F32)
    t = lax.broadcasted_iota(jnp.int32, g0.shape, 0)
    g1 = jnp.where(t >= 1, pltpu.roll(g0, 1, 0), 0.0)
    g2 = jnp.where(t >= 2, pltpu.roll(g0, 2, 0), 0.0)
    gc = cw_ref[0:1, :] * g2 + cw_ref[1:2, :] * g1 + cw_ref[2:3, :] * g0 + cb_ref[...]
    return g0, g1, g2, gc, t


def convglu_fwd(up, conv_w, conv_b, *, name, jobs=None):
    s = up.shape[0]
    nc = D_FF // CONV_TC

    def body(u_ref, g_ref, cw_ref, cb_ref, o_ref):
        _, _, _, gc, _ = _conv_gate(g_ref, cw_ref, cb_ref)
        gelu = 0.5 * gc * (1.0 + lax.erf(gc * SQRT_HALF))
        o_ref[...] = (gelu * u_ref[...].astype(F32)).astype(BF16)

    (out,), bufs = _call(
        body, name=name, jobs=jobs,
        out_shape=[jax.ShapeDtypeStruct((s, D_FF), BF16)],
        grid=(nc,),
        in_specs=[pl.BlockSpec((s, CONV_TC), lambda c: (0, c)),
                  pl.BlockSpec((s, CONV_TC), lambda c: (0, nc + c)),
                  pl.BlockSpec((3, CONV_TC), lambda c: (0, c)),
                  pl.BlockSpec((1, CONV_TC), lambda c: (0, c))],
        out_specs=[pl.BlockSpec((s, CONV_TC), lambda c: (0, c))],
        args=[up, up, conv_w, conv_b], sem=("parallel",))
    return out if jobs is None else (out, bufs)


def convglu_bwd(up, conv_w, conv_b, dact, *, name, jobs=None):
    s = up.shape[0]
    nc = D_FF // CONV_TC

    def body(u_ref, g_ref, cw_ref, cb_ref, da_ref, dup_ref, dcw_ref, dcb_ref):
        du_ref, dg_ref = dup_ref.at[0], dup_ref.at[1]
        g0, g1, g2, gc, t = _conv_gate(g_ref, cw_ref, cb_ref)
        cdf = 0.5 * (1.0 + lax.erf(gc * SQRT_HALF))
        da = da_ref[...].astype(F32)
        du_ref[...] = (da * gc * cdf).astype(BF16)
        dgc = da * u_ref[...].astype(F32) * (cdf + gc * jnp.exp(-0.5 * gc * gc) * INV_SQRT_2PI)
        dcb_ref[...] = jnp.sum(dgc, axis=0, keepdims=True)
        dcw_ref[0:1, :] = jnp.sum(dgc * g2, axis=0, keepdims=True)
        dcw_ref[1:2, :] = jnp.sum(dgc * g1, axis=0, keepdims=True)
        dcw_ref[2:3, :] = jnp.sum(dgc * g0, axis=0, keepdims=True)
        n1 = jnp.where(t < s - 1, pltpu.roll(dgc, s - 1, 0), 0.0)
        n2 = jnp.where(t < s - 2, pltpu.roll(dgc, s - 2, 0), 0.0)
        dg_ref[...] = (cw_ref[2:3, :] * dgc + cw_ref[1:2, :] * n1 + cw_ref[0:1, :] * n2).astype(BF16)

    col = pl.BlockSpec((s, CONV_TC), lambda c: (0, c))
    outs, bufs = _call(
        body, name=name, jobs=jobs,
        out_shape=[jax.ShapeDtypeStruct((2, s, D_FF), BF16),
                   jax.ShapeDtypeStruct((3, D_FF), F32), jax.ShapeDtypeStruct((1, D_FF), F32)],
        grid=(nc,),
        in_specs=[col, pl.BlockSpec((s, CONV_TC), lambda c: (0, nc + c)),
                  pl.BlockSpec((3, CONV_TC), lambda c: (0, c)),
                  pl.BlockSpec((1, CONV_TC), lambda c: (0, c)), col],
        out_specs=[pl.BlockSpec((2, s, CONV_TC), lambda c: (0, 0, c)), pl.BlockSpec((3, CONV_TC), lambda c: (0, c)),
                   pl.BlockSpec((1, CONV_TC), lambda c: (0, c))],
        args=[up, up, conv_w, conv_b, dact], sem=("parallel",))
    return outs if jobs is None else (outs, bufs)


SLOPE_TILE = (8, LANE)


def _slope_table():
    return jnp.broadcast_to(jnp.asarray(ALIBI_SLOPES, F32)[:, None, None], (ATT_HEADS,) + SLOPE_TILE)


def _pieces(s_len, d):
    npc = STREAMS // d
    lp = ATT_BLOCK // npc
    return npc, lp, (s_len // STREAMS) // lp


def _gather(ref, r, b, d, s_len):
    npc, lp, _ = _pieces(s_len, d)
    per = s_len // STREAMS
    parts = [ref[pl.ds((r + d * k) * per + b * lp, lp), :] for k in range(npc)]
    return parts[0] if npc == 1 else jnp.concatenate(parts, axis=0)


def _scatter(ref, r, b, d, s_len, val, add=False):
    npc, lp, _ = _pieces(s_len, d)
    per = s_len // STREAMS
    for k in range(npc):
        rows = pl.ds((r + d * k) * per + b * lp, lp)
        piece = val[k * lp:(k + 1) * lp]
        if add:
            ref[rows, :] += piece
        else:
            ref[rows, :] = piece


def _stream_bias(slope, d, s_len):
    npc, lp, _ = _pieces(s_len, d)
    qi = lax.broadcasted_iota(jnp.int32, (ATT_BLOCK, 2 * ATT_BLOCK), 0)
    c = lax.broadcasted_iota(jnp.int32, (ATT_BLOCK, 2 * ATT_BLOCK), 1)
    own = c // ATT_BLOCK
    cc = c - own * ATT_BLOCK
    dist = npc * ((qi % lp) - (cc % lp) + lp * (1 - own)) + (qi // lp - cc // lp)
    ok = (dist >= 0) & (dist <= ATT_BLOCK)
    return jnp.where(ok, (slope * (-float(d))) * dist.astype(F32), NEG)


def attn_fwd(q, kv, jobs=None):
    s_len = q.shape[0]
    scale = HEAD_DIM ** -0.5

    def body(sl_ref, q_ref, k_ref, v_ref, o_ref, lse_ref):
        g = pl.program_id(1)
        slope = sl_ref[0:1, 0:1]

        def branch(gi, d):
            _, _, nblk = _pieces(s_len, d)
            bias = _stream_bias(slope, d, s_len)
            for r in range(d):
                for b in range(nblk):
                    qb = _gather(q_ref, r, b, d, s_len)
                    kc, vc = _gather(k_ref, r, b, d, s_len), _gather(v_ref, r, b, d, s_len)
                    if b == 0:
                        kcat, vcat, bb = kc, vc, bias[:, ATT_BLOCK:]
                    else:
                        kcat = jnp.concatenate([_gather(k_ref, r, b - 1, d, s_len), kc], axis=0)
                        vcat = jnp.concatenate([_gather(v_ref, r, b - 1, d, s_len), vc], axis=0)
                        bb = bias
                    sc = _dot(qb, kcat, 1, 1) * scale + bb
                    m = jnp.max(sc, axis=-1, keepdims=True)
                    p = jnp.exp(sc - m)
                    l = jnp.sum(p, axis=-1, keepdims=True)
                    o_new = _dot(p, vcat, 1, 0) / l
                    lse_new = m + jnp.log(l)
                    if gi > 0:
                        lse_old = _gather(lse_ref, r, b, d, s_len)[:, 0:1]
                        top = jnp.maximum(lse_old, lse_new)
                        e_old, e_new = jnp.exp(lse_old - top), jnp.exp(lse_new - top)
                        den = e_old + e_new
                        o_new = (e_old * _gather(o_ref, r, b, d, s_len) + e_new * o_new) / den
                        lse_new = top + jnp.log(den)
                    _scatter(o_ref, r, b, d, s_len, o_new)
                    _scatter(lse_ref, r, b, d, s_len, jnp.broadcast_to(lse_new, (ATT_BLOCK, HEAD_DIM)))

        for gi, d in enumerate(DILATIONS):
            @pl.when(g == gi)
            def _():
                branch(gi, d)

    blk = lambda col: pl.BlockSpec((s_len, HEAD_DIM), lambda h, g: (0, col(h, g)))
    head = lambda h, g: h
    outs, bufs = _call(
        body, name="attn_fwd", jobs=jobs,
        out_shape=[jax.ShapeDtypeStruct((s_len, ATT_HEADS * HEAD_DIM), F32)] * 2,
        grid=(ATT_HEADS, len(DILATIONS)),
        in_specs=[pl.BlockSpec((None,) + SLOPE_TILE, lambda h, g: (h, 0, 0)),
                  blk(lambda h, g: g * ATT_HEADS + h), blk(head), blk(lambda h, g: ATT_HEADS + h)],
        out_specs=[blk(head), blk(head)],
        args=[_slope_table(), q, kv, kv], sem=("parallel", "arbitrary"))
    return outs if jobs is None else (outs, bufs)


def attn_bwd(q, kv, o, lse, do, jobs=None):
    s_len = q.shape[0]
    scale = HEAD_DIM ** -0.5
    chunks = s_len // ATT_BLOCK

    def body(sl_ref, q_ref, k_ref, v_ref, o_ref, lse_ref, do_ref, dq_ref, dkv_ref, dlt):
        g = pl.program_id(1)
        slope = sl_ref[0:1, 0:1]
        dk_ref, dv_ref = dkv_ref.at[0], dkv_ref.at[1]

        @pl.when(g == 0)
        def _():
            dkv_ref[...] = jnp.zeros_like(dkv_ref)

            def deltas(c, carry):
                rows = pl.ds(pl.multiple_of(c * ATT_BLOCK, ATT_BLOCK), ATT_BLOCK)
                dlt[rows, :] = jnp.sum(do_ref[rows, :] * o_ref[rows, :], axis=-1, keepdims=True)
                return carry
            lax.fori_loop(0, chunks, deltas, 0)

        def branch(d):
            _, _, nblk = _pieces(s_len, d)
            bias = _stream_bias(slope, d, s_len)
            for r in range(d):
                for b in range(nblk):
                    qb = _gather(q_ref, r, b, d, s_len)
                    dob = _gather(do_ref, r, b, d, s_len)
                    kc, vc = _gather(k_ref, r, b, d, s_len), _gather(v_ref, r, b, d, s_len)
                    if b == 0:
                        kcat, vcat, bb = kc, vc, bias[:, ATT_BLOCK:]
                    else:
                        kcat = jnp.concatenate([_gather(k_ref, r, b - 1, d, s_len), kc], axis=0)
                        vcat = jnp.concatenate([_gather(v_ref, r, b - 1, d, s_len), vc], axis=0)
                        bb = bias
                    sc = _dot(qb, kcat, 1, 1) * scale + bb
                    p = jnp.exp(sc - _gather(lse_ref, r, b, d, s_len)[:, 0:1])
                    ds = p * (_dot(dob, vcat, 1, 1) - _gather(dlt, r, b, d, s_len))
                    _scatter(dq_ref, r, b, d, s_len, _dot(ds, kcat, 1, 0) * scale)
                    dk = _dot(ds, qb, 0, 0) * scale
                    dv = _dot(p, dob, 0, 0)
                    if b == 0:
                        _scatter(dk_ref, r, b, d, s_len, dk, add=True)
                        _scatter(dv_ref, r, b, d, s_len, dv, add=True)
                    else:
                        _scatter(dk_ref, r, b - 1, d, s_len, dk[:ATT_BLOCK], add=True)
                        _scatter(dv_ref, r, b - 1, d, s_len, dv[:ATT_BLOCK], add=True)
                        _scatter(dk_ref, r, b, d, s_len, dk[ATT_BLOCK:], add=True)
                        _scatter(dv_ref, r, b, d, s_len, dv[ATT_BLOCK:], add=True)

        for gi, d in enumerate(DILATIONS):
            @pl.when(g == gi)
            def _():
                branch(d)

    blk = lambda col: pl.BlockSpec((s_len, HEAD_DIM), lambda h, g: (0, col(h, g)))
    head = lambda h, g: h
    q_col = lambda h, g: g * ATT_HEADS + h
    outs, bufs = _call(
        body, name="attn_bwd", jobs=jobs,
        out_shape=[jax.ShapeDtypeStruct(q.shape, F32), jax.ShapeDtypeStruct((2, s_len, ATT_HEADS * HEAD_DIM), F32)],
        grid=(ATT_HEADS, len(DILATIONS)),
        in_specs=[pl.BlockSpec((None,) + SLOPE_TILE, lambda h, g: (h, 0, 0)),
                  blk(q_col), blk(head), blk(lambda h, g: ATT_HEADS + h), blk(head), blk(head), blk(head)],
        out_specs=[blk(q_col), pl.BlockSpec((2, s_len, HEAD_DIM), lambda h, g: (0, 0, h))],
        scratch_shapes=[pltpu.VMEM((s_len, 1), F32)],
        args=[_slope_table(), q, kv, kv, o, lse, do], sem=("parallel", "arbitrary"))
    return outs if jobs is None else (outs, bufs)


def _adam(w, g, m, v):
    m = ADAM_B1 * m + (1.0 - ADAM_B1) * g
    v = ADAM_B2 * v + (1.0 - ADAM_B2) * (g * g)
    m_hat = m / (1.0 - ADAM_B1 ** ADAM_STEP)
    v_hat = v / (1.0 - ADAM_B2 ** ADAM_STEP)
    delta = -ADAM_LR * (m_hat / (jnp.sqrt(v_hat) + ADAM_EPS) + ADAM_WD * w)
    return delta, m, v


def adam_sharded(recvs, w, m, v, *, name):
    layers = len(recvs)
    n_src, r, c = recvs[0].shape
    tr = _rows(r, c)

    def body(*refs):
        p_refs = refs[:layers]
        w_ref, m_ref, v_ref, g_ref, d_ref, mo_ref, vo_ref = refs[layers:]
        for layer, p_ref in enumerate(p_refs):
            @pl.when(pl.program_id(0) == layer)
            def _():
                g = p_ref[0].astype(F32)
                for src in range(1, n_src):
                    g = g + p_ref[src].astype(F32)
                delta, m_new, v_new = _adam(w_ref[...], g, m_ref[...], v_ref[...])
                g_ref[...] = g
                d_ref[...] = delta
                mo_ref[...] = m_new
                vo_ref[...] = v_new

    blk = pl.BlockSpec((None, tr, c), lambda l, i: (l, i, 0))
    out = jax.ShapeDtypeStruct((layers, r, c), F32)
    part = [pl.BlockSpec((n_src, tr, c), functools.partial(lambda l, i, layer: (0, jnp.where(l == layer, i, 0), 0),
                                                            layer=layer)) for layer in range(layers)]
    return pl.pallas_call(
        body,
        name=name,
        out_shape=[out] * 4,
        grid=(layers, r // tr),
        in_specs=part + [blk, blk, blk],
        out_specs=[blk] * 4,
        compiler_params=_params("parallel", "parallel"),
    )(*recvs, w, m, v)


def sum_partials(parts):
    n_src, r, c = parts.shape

    def body(p_ref, o_ref):
        g = p_ref[0]
        for src in range(1, n_src):
            g = g + p_ref[src]
        o_ref[...] = g

    return pl.pallas_call(
        body,
        name="sum_small_grads",
        out_shape=jax.ShapeDtypeStruct((r, c), F32),
    )(parts)


def adam_packed(w, g, m, v):
    def body(w_ref, g_ref, m_ref, v_ref, d_ref, mo_ref, vo_ref):
        delta, m_new, v_new = _adam(w_ref[...], g_ref[...], m_ref[...], v_ref[...])
        d_ref[...] = delta
        mo_ref[...] = m_new
        vo_ref[...] = v_new

    out = jax.ShapeDtypeStruct(w.shape, F32)
    return pl.pallas_call(body, name="adam_small", out_shape=[out] * 3)(w, g, m, v)


def all_gather(srcs, *, name):
    n = len(srcs)

    def body(*refs):
        src, dst = refs[:n], refs[n:2 * n]
        send_sems, recv_sems, local_sems = refs[2 * n:]
        x, y, c, me = _place()
        sibling = (x, y, 1 - c)
        chips = [(1 - x, y), (x, 1 - y), (1 - x, 1 - y)]

        def index(px, py, pc):
            return 4 * px + 2 * py + pc

        def copy(p, k, block, to, from_src=False):
            slot = dst[p].at[index(*block)]
            return pltpu.make_async_remote_copy(
                src_ref=src[p] if from_src else slot, dst_ref=slot,
                send_sem=send_sems.at[p, k], recv_sem=recv_sems.at[p, k],
                device_id=to, device_id_type=MESH)

        mine = [pltpu.make_async_copy(src[p], dst[p].at[me], local_sems.at[p]) for p in range(n)]
        for cp in mine:
            cp.start()
        first = []
        for p in range(n):
            first.append(copy(p, 0, (x, y, c), sibling, from_src=True))
            for jj, chip in enumerate(chips):
                first.append(copy(p, 1 + jj, (x, y, c), (*chip, c), from_src=True))
        for cp in first:
            cp.start()
        passed = []
        for jj, chip in enumerate(chips):
            for p in range(n):
                copy(p, 1 + jj, (*chip, c), (x, y, c)).wait_recv()
                fwd = copy(p, 4 + jj, (*chip, c), sibling)
                fwd.start()
                passed.append(fwd)
        for p in range(n):
            copy(p, 0, sibling, (x, y, c)).wait_recv()
            for jj, chip in enumerate(chips):
                copy(p, 4 + jj, (*chip, 1 - c), (x, y, c)).wait_recv()
        for cp in first + passed:
            cp.wait_send()
        for cp in mine:
            cp.wait()

    return pl.pallas_call(
        body,
        name=name,
        out_shape=[jax.ShapeDtypeStruct((N_DEV,) + a.shape, a.dtype) for a in srcs],
        in_specs=[ANY] * n,
        out_specs=[ANY] * n,
        scratch_shapes=[pltpu.SemaphoreType.DMA((n, 7)), pltpu.SemaphoreType.DMA((n, 7)),
                        pltpu.SemaphoreType.DMA((n,))],
    )(*srcs)


def exchange_only(*, name, jobs):
    def body(o_ref):
        o_ref[...] = jnp.zeros_like(o_ref)

    _, bufs = _call(body, name=name, jobs=jobs, out_shape=[jax.ShapeDtypeStruct((8, LANE), F32)], grid=(1,),
                    in_specs=[], out_specs=[pl.BlockSpec((8, LANE), lambda i: (0, 0))], args=[], sem=("arbitrary",))
    return None, bufs


def _pack_rows(parts, rows):
    flat = jnp.concatenate([p.reshape(-1) for p in parts])
    return jnp.pad(flat, (0, rows * LANE - flat.shape[0])).reshape(rows, LANE)


def _unpack_rows(packed, shapes):
    flat = packed.reshape(-1)
    out, at = [], 0
    for sh in shapes:
        size = 1
        for dim in sh:
            size *= dim
        out.append(flat[at:at + size].reshape(sh))
        at += size
    return out


CONV_W_PAD = 768
SMALL_W_ROWS = 56


def _pack_small_weights(w_a2, b_a2, hn, conv_w):
    cw = jnp.pad(conv_w.reshape(6, -1), ((0, 0), (0, CONV_W_PAD - conv_w.shape[-1]))).reshape(-1, LANE)
    rows = jnp.concatenate([w_a2[0], b_a2, jnp.pad(hn, ((0, 0), (0, LANE - hn.shape[-1]))), cw], axis=0)
    return jnp.pad(rows, ((0, SMALL_W_ROWS - rows.shape[0]), (0, 0)))


def _unpack_small_weights(gathered):
    w_a2 = gathered[:, 0:GATE_RANK, :].transpose(1, 0, 2).reshape(GATE_RANK, GLA_KEY_DIM)
    b_a2 = gathered[:, GATE_RANK, :].reshape(1, GLA_KEY_DIM)
    hn = gathered[:, GATE_RANK + 1, :GLA_DV // N_DEV].reshape(1, GLA_DV)
    per = D_FF // N_DEV
    cw = gathered[:, GATE_RANK + 2:GATE_RANK + 2 + 6 * CONV_W_PAD // LANE, :].reshape(N_DEV, 6, CONV_W_PAD)[:, :, :per]
    cw = cw.reshape(N_DEV, 2, 3, per).transpose(1, 2, 0, 3).reshape(2, 3, D_FF)
    return w_a2, b_a2, hn, cw


SCHEDULE = {
    "gla_in": [("g1", "gout", None), ("g1", "up0", (0, 1024))],
    "gla_fwd": [("g2", "gout", None), ("g2", "up0", (0, 1024)), ("g1", "up0", (1024, 2048))],
    "gla_out": [("g2", "up0", (1024, 2048)), ("g1", "dn0", (0, 352))],
    "ffn_up0": [("g2", "dn0", (0, 352)), ("g1", "dn0", (352, 704)), ("g1", "kv", None), ("g1", "q", (0, 768))],
    "convglu_fwd0": [("g2", "dn0", (352, 704))],
    "ffn_down0": [("g2", "kv", None), ("g2", "q", (0, 768)), ("g1", "q", (768, 2048)), ("g1", "dout", None)],
    "kv_proj": [("g2", "q", (768, 2048)), ("g2", "dout", None), ("g1", "up1", (0, 704))],
    "q_proj": [("g2", "up1", (0, 704)), ("g1", "up1", (704, 1664))],
    "attn_fwd": [("g2", "up1", (704, 1664)), ("g1", "up1", (1664, 2048)), ("g1", "dn1", None)],
    "dsa_out": [("g2", "up1", (1664, 2048)), ("g2", "dn1", None)],
    "ffn_down_dx1": [("sc", "dn1", (0, 352))],
    "convglu_bwd1": [("sc", "dn1", (352, 704))],
    "ffn_up_dx1": [("sc", "up1", (0, 1024))],
    "attn_bwd": [("sc", "up1", (1024, 2048)), ("sc", "dout", None)],
    "q_proj_dx": [("sc", "q", (0, 1024))],
    "kv_proj_dw": [("sc", "q", (1024, 1792))],
    "kv_proj_dx": [("sc", "q", (1792, 2048)), ("sc", "kv", (0, 768))],
    "ffn_down_dw0": [("sc", "kv", (768, 2048))],
    "ffn_down_dx0": [("sc", "dn0", (0, 384))],
    "convglu_bwd0": [("sc", "dn0", (384, 704))],
    "ffn_up_dx0": [("sc", "up0", (0, 1024))],
    "gla_out_dw": [("sc", "up0", (1024, 1216))],
    "gla_out_dx": [("sc", "up0", (1216, 1408))],
    "gla_bwd": [("sc", "up0", (1408, 2048))],
    "gla_in_dw": [("sc", "gout", None)],
    "gla_in_dx": [("sc", "in", (0, 1536))],
    "grads_tail": [("sc", "in", (1536, 2048)), ("all", "small", None)],
}
ROW_SHARDED = ("gout", "dout", "dn0", "dn1")


class Plan:
    def __init__(self, weights, srcs=None):
        self.w = dict(weights)
        self.srcs = srcs
        self.grads = {}
        self.recv = {}
        self._names = None

    def weight(self, name):
        buf = self.w[name]
        if name in ROW_SHARDED:
            return buf.reshape(1, buf.shape[0] * buf.shape[1], buf.shape[2])
        return buf

    def jobs(self, call):
        ops = SCHEDULE.get(call)
        if self.srcs is None or not ops:
            return None
        jobs, handles = Jobs(), {}
        backward = ops[0][0] in ("sc", "all")
        for op, name, rows in ops:
            assert (op in ("sc", "all")) == backward
            store = self.recv if backward else self.w
            if name not in handles:
                if name in store:
                    handles[name] = jobs.thru(store[name])
                elif op == "sc":
                    handles[name] = jobs.new(self.grads[name].shape, BF16)
                elif op == "all":
                    handles[name] = jobs.new((N_DEV,) + self.grads[name].shape, self.grads[name].dtype)
                else:
                    handles[name] = jobs.new((N_DEV,) + self.srcs[name].shape, BF16)
            if op == "g1":
                jobs.gather_ici(self.srcs[name], handles[name], rows)
            elif op == "g2":
                jobs.gather_d2d(handles[name], rows)
            else:
                jobs.scatter(self.grads[name], handles[name], rows, same=op == "all")
        self._names = [(name, self.recv if backward else self.w) for name in handles]
        return jobs

    def run(self, call, fn, *args, **kwargs):
        jobs = self.jobs(call)
        if jobs is None:
            return fn(*args, **kwargs)
        out, bufs = fn(*args, jobs=jobs, **kwargs)
        for (name, store), buf in zip(self._names, bufs):
            store[name] = buf
        return out


def _ffn_fwd(plan, h, norm_g, conv_w, conv_b, tag):
    (n,) = rms_fwd(h, [norm_g], name=f"ffn_norm_fwd{tag}")
    up = plan.run(f"ffn_up{tag}", mm_nn, n, plan.weight(f"up{tag}"), out_dtype=BF16, name=f"ffn_up{tag}")
    act = plan.run(f"convglu_fwd{tag}", convglu_fwd, up, conv_w, conv_b, name=f"convglu_fwd{tag}")
    h_out = plan.run(f"ffn_down{tag}", mm_nn, act, plan.weight(f"dn{tag}"), out_dtype=F32, res=h,
                     name=f"ffn_down{tag}")
    return h_out, (n, up, act)


def _by_rows(dw):
    return dw.reshape(N_DEV, dw.shape[1] // N_DEV, dw.shape[2])


def _ffn_bwd(plan, dh_out, h, saved, norm_g, conv_w, conv_b, tag):
    n, up, act = saved
    plan.grads[f"dn{tag}"] = _by_rows(plan.run(f"ffn_down_dw{tag}", mm_tn, act, dh_out, 1, name=f"ffn_down_dw{tag}"))
    dact = plan.run(f"ffn_down_dx{tag}", mm_nt, dh_out, plan.weight(f"dn{tag}"), out_dtype=BF16,
                    name=f"ffn_down_dx{tag}")
    dup, dconv_w, dconv_b = plan.run(f"convglu_bwd{tag}", convglu_bwd, up, conv_w, conv_b, dact,
                                     name=f"convglu_bwd{tag}")
    plan.grads[f"up{tag}"] = mm_tn(n, dup, N_DEV, name=f"ffn_up_dw{tag}")
    dh, (dnorm,) = plan.run(f"ffn_up_dx{tag}", mm_nt, dup, plan.weight(f"up{tag}"), out_dtype=F32,
                            name=f"ffn_up_dx{tag}", norm=(h, dh_out, [norm_g], []))
    return dh, dnorm, dconv_w, dconv_b


def local_step(x, target, wts, plan):
    row = lambda v: v.reshape(1, -1)
    attn_norm, ffn_norm = wts["attn_norm"], wts["ffn_norm"]
    conv_w, conv_b = wts["ffn_conv_w"], wts["ffn_conv_b"]

    (n1,) = rms_fwd(x, [row(attn_norm[0])], name="attn_norm_fwd0")
    proj = plan.run("gla_in", mm_nn, n1, wts["gla_w_in"], out_dtype=F32, name="gla_in")
    la = gate_fwd(proj, wts["gla_w_a2"], wts["gla_b_a2"])
    o_gla, states, og = plan.run("gla_fwd", gla_fwd, proj, la, wts["gla_head_norm"])
    h1 = plan.run("gla_out", mm_nn, og, plan.weight("gout"), out_dtype=F32, res=x, name="gla_out")
    h2, ffn0 = _ffn_fwd(plan, h1, row(ffn_norm[0]), conv_w[0], row(conv_b[0]), "0")

    h2s = to_streams(h2, name="h2_to_streams")
    kvn, n3 = rms_fwd(h2s, [row(wts["kv_norm"]), row(attn_norm[1])], name="kv_attn_norm_fwd")
    kv = plan.run("kv_proj", mm_nn, kvn, plan.weight("kv"), out_dtype=BF16, name="kv_proj")
    q = plan.run("q_proj", mm_nn, n3, plan.weight("q"), out_dtype=BF16, name="q_proj")
    o_att, lse = plan.run("attn_fwd", attn_fwd, q, kv)
    h3 = from_streams(plan.run("dsa_out", mm_nn, o_att, plan.weight("dout"), out_dtype=F32, res=h2s, name="dsa_out"),
                      name="h3_from_streams")
    h4, ffn1 = _ffn_fwd(plan, h3, row(ffn_norm[1]), conv_w[1], row(conv_b[1]), "1")

    loss_tile, dh4, d_final = loss_head(h4, row(wts["final_norm"]), target)

    dh3, d_ffn1, dcw1, dcb1 = _ffn_bwd(plan, dh4, h3, ffn1, row(ffn_norm[1]), conv_w[1], row(conv_b[1]), "1")
    dh3s = to_streams(dh3, name="dh3_to_streams")
    plan.grads["dout"] = _by_rows(mm_tn(o_att, dh3s, 1, name="dsa_out_dw"))
    do_att = mm_nt(dh3s, plan.weight("dout"), out_dtype=F32, name="dsa_out_dx")
    dq, dkv = plan.run("attn_bwd", attn_bwd, q, kv, o_att, lse, do_att)
    plan.grads["q"] = mm_tn(n3, dq, N_DEV, name="q_proj_dw")
    dh2_part, (d_attn1,) = plan.run("q_proj_dx", mm_nt, dq, plan.weight("q"), out_dtype=F32, name="q_proj_dx",
                                    norm=(h2s, dh3s, [row(attn_norm[1])], []))
    plan.grads["kv"] = plan.run("kv_proj_dw", mm_tn, kvn, dkv, N_DEV, name="kv_proj_dw")
    dh2s, (d_kvnorm,) = plan.run("kv_proj_dx", mm_nt, dkv, plan.weight("kv"), out_dtype=F32, name="kv_proj_dx",
                                 norm=(h2s, dh2_part, [row(wts["kv_norm"])], []))
    dh2 = from_streams(dh2s, name="dh2_from_streams")
    dh1, d_ffn0, dcw0, dcb0 = _ffn_bwd(plan, dh2, h1, ffn0, row(ffn_norm[0]), conv_w[0], row(conv_b[0]), "0")
    plan.grads["gout"] = _by_rows(plan.run("gla_out_dw", mm_tn, og, dh1, 1, name="gla_out_dw"))
    dog = plan.run("gla_out_dx", mm_nt, dh1, plan.weight("gout"), out_dtype=F32, name="gla_out_dx")
    dq_g, dk_g, dv_g, dr, dla, d_hn = plan.run("gla_bwd", gla_bwd, proj, la, states, o_gla, wts["gla_head_norm"], dog)
    da, dw_a2p, db_a2 = gate_bwd(proj, wts["gla_w_a2"], wts["gla_b_a2"], dla)
    dproj = jnp.concatenate([dq_g, dk_g, dv_g, dr, da], axis=1)
    assert dproj.shape[1] == GLA_IN_PAD
    dw_in = plan.run("gla_in_dw", mm_tn, n1, dproj, 1, name="gla_in_dw")
    plan.grads["in"] = dw_in[0, :, :GLA_IN_DIM].reshape(D_MODEL, N_DEV, GLA_IN_DIM // N_DEV).transpose(1, 0, 2)
    grad_x, (d_attn0,) = plan.run("gla_in_dx", mm_nt, dproj, wts["gla_w_in"], out_dtype=F32, name="gla_in_dx",
                                  norm=(x, dh1, [row(attn_norm[0])], []))

    small = dict(
        attn_norm=jnp.concatenate([d_attn0, d_attn1], axis=0),
        ffn_norm=jnp.concatenate([d_ffn0, d_ffn1], axis=0),
        kv_norm=d_kvnorm.reshape(-1),
        final_norm=d_final.reshape(-1),
        ffn_conv_b=jnp.concatenate([dcb0, dcb1], axis=0),
        gla_w_a2=dw_a2p[:GATE_RANK],
        gla_b_a2=db_a2,
        gla_head_norm=d_hn,
        ffn_conv_w=jnp.stack([dcw0, dcw1]),
    )
    return loss_tile, grad_x, small


SMALL_ORDER = ("attn_norm", "ffn_norm", "kv_norm", "final_norm", "ffn_conv_b",
               "gla_w_a2", "gla_b_a2", "gla_head_norm", "ffn_conv_w")
SMALL_FULL = dict(attn_norm=(2, D_MODEL), ffn_norm=(2, D_MODEL), kv_norm=(D_MODEL,), final_norm=(D_MODEL,),
                  ffn_conv_b=(2, D_FF), gla_w_a2=(GATE_RANK, GLA_KEY_DIM), gla_b_a2=(1, GLA_KEY_DIM),
                  gla_head_norm=(1, GLA_DV), ffn_conv_w=(2, 3, D_FF))
SMALL_SHARDED = ("gla_w_a2", "gla_b_a2", "gla_head_norm", "ffn_conv_w")
SMALL_GRAD_ROWS = 592
SMALL_ADAM_ROWS = 240


def kernel(x, attn_norm, gla_w_in, gla_w_a2, gla_b_a2, gla_head_norm, gla_w_out, kv_norm, w_kv, dsa_w_q, dsa_w_out, ffn_norm, ffn_w_up, ffn_conv_w, ffn_conv_b, ffn_w_down, final_norm, loss_target, m_attn_norm, m_gla_w_in, m_gla_w_a2, m_gla_b_a2, m_gla_head_norm, m_gla_w_out, m_kv_norm, m_w_kv, m_dsa_w_q, m_dsa_w_out, m_ffn_norm, m_ffn_w_up, m_ffn_conv_w, m_ffn_conv_b, m_ffn_w_down, m_final_norm, v_attn_norm, v_gla_w_in, v_gla_w_a2, v_gla_b_a2, v_gla_head_norm, v_gla_w_out, v_kv_norm, v_w_kv, v_dsa_w_q, v_dsa_w_out, v_ffn_norm, v_ffn_w_up, v_ffn_conv_w, v_ffn_conv_b, v_ffn_w_down, v_final_norm):
    me = 4 * lax.axis_index("x") + 2 * lax.axis_index("y") + lax.axis_index("c")
    bf = lambda a: a.astype(BF16)

    g_in, g_small = all_gather([bf(gla_w_in[0]), _pack_small_weights(gla_w_a2, gla_b_a2, gla_head_norm, ffn_conv_w)],
                               name="gather_first")
    w_a2_full, b_a2_full, hn_full, conv_w_full = _unpack_small_weights(g_small)
    w_in_full = jnp.pad(g_in.transpose(1, 0, 2).reshape(D_MODEL, GLA_IN_DIM), ((0, 0), (0, GLA_IN_PAD - GLA_IN_DIM)))
    wts = dict(
        attn_norm=attn_norm, ffn_norm=ffn_norm, kv_norm=kv_norm, final_norm=final_norm, ffn_conv_b=ffn_conv_b,
        gla_w_in=w_in_full[None],
        gla_w_a2=jnp.pad(bf(w_a2_full), ((0, LANE - GATE_RANK), (0, 0))),
        gla_b_a2=b_a2_full, gla_head_norm=hn_full, ffn_conv_w=conv_w_full,
    )
    plan = Plan({}, srcs=dict(gout=bf(gla_w_out[0]), kv=bf(w_kv), q=bf(dsa_w_q[0]), dout=bf(dsa_w_out[0]),
                              up0=bf(ffn_w_up[0]), up1=bf(ffn_w_up[1]), dn0=bf(ffn_w_down[0]), dn1=bf(ffn_w_down[1])))

    loss_tile, grad_x, small = local_step(x[0], loss_target[0], wts, plan)
    loss = lax.psum(loss_tile[0, 0], ("x", "y", "c"))

    plan.grads["small"] = _pack_rows([small[nm] for nm in SMALL_ORDER], SMALL_GRAD_ROWS)
    plan.run("grads_tail", exchange_only, name="grads_tail")
    shard3 = lambda a: a.reshape((-1,) + a.shape[-2:])
    big_params = dict(gla_w_in=(("in",), gla_w_in, m_gla_w_in, v_gla_w_in),
                      gla_w_out=(("gout",), gla_w_out, m_gla_w_out, v_gla_w_out),
                      w_kv=(("kv",), w_kv, m_w_kv, v_w_kv),
                      dsa_w_q=(("q",), dsa_w_q, m_dsa_w_q, v_dsa_w_q),
                      dsa_w_out=(("dout",), dsa_w_out, m_dsa_w_out, v_dsa_w_out),
                      ffn_w_up=(("up0", "up1"), ffn_w_up, m_ffn_w_up, v_ffn_w_up),
                      ffn_w_down=(("dn0", "dn1"), ffn_w_down, m_ffn_w_down, v_ffn_w_down))
    res = {}
    for nm, (parts, w, m, v) in big_params.items():
        outs = adam_sharded([plan.recv[p] for p in parts], shard3(w), shard3(m), shard3(v), name=f"adam_{nm}")
        res[nm] = [o.reshape(w.shape) for o in outs]

    full = dict(zip(SMALL_ORDER, _unpack_rows(sum_partials(plan.recv["small"]),
                                              [SMALL_FULL[nm] for nm in SMALL_ORDER])))
    local_w = dict(attn_norm=attn_norm, ffn_norm=ffn_norm, kv_norm=kv_norm, final_norm=final_norm,
                   ffn_conv_b=ffn_conv_b, gla_w_a2=gla_w_a2, gla_b_a2=gla_b_a2, gla_head_norm=gla_head_norm,
                   ffn_conv_w=ffn_conv_w)
    local_m = dict(attn_norm=m_attn_norm, ffn_norm=m_ffn_norm, kv_norm=m_kv_norm, final_norm=m_final_norm,
                   ffn_conv_b=m_ffn_conv_b, gla_w_a2=m_gla_w_a2, gla_b_a2=m_gla_b_a2, gla_head_norm=m_gla_head_norm,
                   ffn_conv_w=m_ffn_conv_w)
    local_v = dict(attn_norm=v_attn_norm, ffn_norm=v_ffn_norm, kv_norm=v_kv_norm, final_norm=v_final_norm,
                   ffn_conv_b=v_ffn_conv_b, gla_w_a2=v_gla_w_a2, gla_b_a2=v_gla_b_a2, gla_head_norm=v_gla_head_norm,
                   ffn_conv_w=v_ffn_conv_w)
    local_g = {}
    for nm in SMALL_ORDER:
        gfull = full[nm]
        if nm in SMALL_SHARDED:
            per = gfull.shape[-1] // N_DEV
            gfull = lax.dynamic_slice_in_dim(gfull, me * per, per, axis=gfull.ndim - 1)
        local_g[nm] = gfull.reshape(local_w[nm].shape)
    shapes = [local_w[nm].shape for nm in SMALL_ORDER]
    pk = lambda dd: _pack_rows([dd[nm] for nm in SMALL_ORDER], SMALL_ADAM_ROWS)
    d_p, m_p, v_p = adam_packed(pk(local_w), pk(local_g), pk(local_m), pk(local_v))
    for nm, dl, mn, vn in zip(SMALL_ORDER, _unpack_rows(d_p, shapes), _unpack_rows(m_p, shapes),
                              _unpack_rows(v_p, shapes)):
        res[nm] = [local_g[nm], dl, mn, vn]

    order = ("attn_norm", "gla_w_in", "gla_w_a2", "gla_b_a2", "gla_head_norm", "gla_w_out", "kv_norm", "w_kv",
             "dsa_w_q", "dsa_w_out", "ffn_norm", "ffn_w_up", "ffn_conv_w", "ffn_conv_b", "ffn_w_down", "final_norm")
    outs = [loss, grad_x[None]]
    for kind in range(4):
        outs.extend(res[nm][kind] for nm in order)
    return tuple(outs)
```

```python
import functools

import jax
import jax.numpy as jnp
from jax import lax
from jax.experimental import pallas as pl
from jax.experimental.pallas import tpu as pltpu

F32 = jnp.float32
BF16 = jnp.bfloat16
MESH = pl.DeviceIdType.MESH
ANY = pl.BlockSpec(memory_space=pl.ANY)

N_DEV = 8
D_MODEL = 2048
GLA_HEADS = 4
GLA_KEY_DIM = 1024
GLA_VAL_DIM = 2048
GLA_DK = 256
GLA_DV = 512
GATE_RANK = 16
GATE_NORMALIZER = 16.0
GLA_CHUNK = 64
GLA_STEP_CHUNKS = 4
GLA_IN_DIM = 2 * GLA_KEY_DIM + 2 * GLA_VAL_DIM + GATE_RANK
GLA_IN_PAD = 6272
ATT_HEADS = 16
HEAD_DIM = 128
DILATIONS = (1, 4, 16)
STREAMS = DILATIONS[-1]
ATT_BLOCK = 128
D_FF = 5632
EPS = 1e-6
ADAM_LR = 0.001
ADAM_B1 = 0.9
ADAM_B2 = 0.999
ADAM_EPS = 1e-08
ADAM_WD = 0.01
ADAM_STEP = 10
NEG = -1e30
LANE = 128
NORM_ROWS = 64
VMEM_LIMIT = 52 * 1024 * 1024
ALIBI_SLOPES = tuple(2.0 ** (-0.5 * (i + 1)) for i in range(ATT_HEADS))


def _params(*sem):
    return pltpu.CompilerParams(dimension_semantics=sem, vmem_limit_bytes=VMEM_LIMIT)


def _tile(n, cap):
    best = None
    for t in range(LANE, min(n, cap) + 1, LANE):
        if n % t == 0:
            best = t
    return best if best is not None else n


def _shard_group(j, ns, cap):
    best = 1
    for g in range(1, j + 1):
        if j % g == 0 and g * ns <= cap:
            best = g
    return best


def _rows(r, c, budget=256 * 1024):
    best = None
    for t in range(16, r + 1, 16):
        if r % t == 0 and t * c <= budget:
            best = t
    return best if best is not None else r


def _flip(coord, bit):
    return 1 - coord if bit else coord


def _place():
    x, y, c = lax.axis_index("x"), lax.axis_index("y"), lax.axis_index("c")
    return x, y, c, 4 * x + 2 * y + c


def _rows_of(ref, rows):
    return ref if rows is None else ref.at[pl.ds(rows[0], rows[1] - rows[0])]


class Jobs:
    def __init__(self):
        self.srcs = []
        self.bufs = []
        self.sems = []
        self.steps = []

    def _src(self, a):
        for i, b in enumerate(self.srcs):
            if b is a:
                return i
        self.srcs.append(a)
        return len(self.srcs) - 1

    def new(self, shape, dtype):
        self.bufs.append((None, jax.ShapeDtypeStruct(shape, dtype)))
        return len(self.bufs) - 1

    def thru(self, a):
        self.bufs.append((a, jax.ShapeDtypeStruct(a.shape, a.dtype)))
        return len(self.bufs) - 1

    def _sem(self, n):
        self.sems.append(pltpu.SemaphoreType.DMA((n,)))
        return len(self.sems) - 1

    def gather_ici(self, src, buf, rows=None):
        si, send, recv, loc = self._src(src), self._sem(4), self._sem(4), self._sem(1)

        def remote(srcs, bufs, sems, slot_of):
            x, y, c, me = _place()
            peers = [(x, y, 1 - c), (1 - x, y, c), (x, 1 - y, c), (1 - x, 1 - y, c)]
            return [pltpu.make_async_remote_copy(
                src_ref=_rows_of(srcs[si], rows),
                dst_ref=_rows_of(bufs[buf].at[me if slot_of == "mine" else 4 * p[0] + 2 * p[1] + p[2]], rows),
                send_sem=sems[send].at[k], recv_sem=sems[recv].at[k], device_id=p, device_id_type=MESH)
                for k, p in enumerate(peers)]

        def local(srcs, bufs, sems):
            return pltpu.make_async_copy(_rows_of(srcs[si], rows), _rows_of(bufs[buf].at[_place()[3]], rows),
                                         sems[loc].at[0])

        def start(srcs, bufs, sems):
            local(srcs, bufs, sems).start()
            for cp in remote(srcs, bufs, sems, "mine"):
                cp.start()

        def finish(srcs, bufs, sems):
            for cp in remote(srcs, bufs, sems, "peer"):
                cp.wait_recv()
            for cp in remote(srcs, bufs, sems, "mine"):
                cp.wait_send()
            local(srcs, bufs, sems).wait()

        self.steps.append((start, finish))

    def gather_d2d(self, buf, rows=None):
        send, recv = self._sem(3), self._sem(3)

        def copies(bufs, sems, core):
            x, y, c, _ = _place()
            cc = c if core == "mine" else 1 - c
            chips = [(1 - x, y), (x, 1 - y), (1 - x, 1 - y)]
            return [pltpu.make_async_remote_copy(
                src_ref=_rows_of(bufs[buf].at[4 * px + 2 * py + cc], rows),
                dst_ref=_rows_of(bufs[buf].at[4 * px + 2 * py + cc], rows),
                send_sem=sems[send].at[k], recv_sem=sems[recv].at[k],
                device_id=(x, y, 1 - c), device_id_type=MESH) for k, (px, py) in enumerate(chips)]

        def start(srcs, bufs, sems):
            for cp in copies(bufs, sems, "mine"):
                cp.start()

        def finish(srcs, bufs, sems):
            for cp in copies(bufs, sems, "sibling"):
                cp.wait_recv()
            for cp in copies(bufs, sems, "mine"):
                cp.wait_send()

        self.steps.append((start, finish))

    def scatter(self, src, buf, rows=None, same=False):
        si, send, recv, loc = self._src(src), self._sem(N_DEV - 1), self._sem(N_DEV - 1), self._sem(1)

        def block(srcs, dev):
            return _rows_of(srcs[si] if same else srcs[si].at[dev], rows)

        def remote(srcs, bufs, sems, slot_of):
            x, y, c, me = _place()
            out = []
            for k in range(1, N_DEV):
                px, py, pc = _flip(x, k >> 2), _flip(y, (k >> 1) & 1), _flip(c, k & 1)
                peer = 4 * px + 2 * py + pc
                out.append(pltpu.make_async_remote_copy(
                    src_ref=block(srcs, peer),
                    dst_ref=_rows_of(bufs[buf].at[me if slot_of == "mine" else peer], rows),
                    send_sem=sems[send].at[k - 1], recv_sem=sems[recv].at[k - 1],
                    device_id=(px, py, pc), device_id_type=MESH))
            return out

        def local(srcs, bufs, sems):
            me = _place()[3]
            return pltpu.make_async_copy(block(srcs, me), _rows_of(bufs[buf].at[me], rows), sems[loc].at[0])

        def start(srcs, bufs, sems):
            local(srcs, bufs, sems).start()
            for cp in remote(srcs, bufs, sems, "mine"):
                cp.start()

        def finish(srcs, bufs, sems):
            for cp in remote(srcs, bufs, sems, "peer"):
                cp.wait_recv()
            for cp in remote(srcs, bufs, sems, "mine"):
                cp.wait_send()
            local(srcs, bufs, sems).wait()

        self.steps.append((start, finish))


def _call(body, *, name, grid, in_specs, out_specs, out_shape, args, sem, scratch_shapes=(), jobs=None):
    in_specs, out_specs, out_shape = list(in_specs), list(out_specs), list(out_shape)
    scratch_shapes = list(scratch_shapes)
    if jobs is None:
        res = pl.pallas_call(body, name=name, out_shape=out_shape, grid=grid, in_specs=in_specs,
                             out_specs=out_specs, scratch_shapes=scratch_shapes,
                             compiler_params=_params(*sem))(*args)
        return list(res), []
    thru = [a for a, _ in jobs.bufs if a is not None]
    n_in, n_src, n_thru = len(args), len(jobs.srcs), len(thru)
    n_out, n_buf, n_scr = len(out_shape), len(jobs.bufs), len(scratch_shapes)
    aliases, t = {}, 0
    for b, (a, _) in enumerate(jobs.bufs):
        if a is not None:
            aliases[n_in + n_src + t] = n_out + b
            t += 1

    def wrapped(*refs):
        at = 0
        ins = refs[at:at + n_in]; at += n_in
        srcs = refs[at:at + n_src]; at += n_src + n_thru
        outs = refs[at:at + n_out]; at += n_out
        bufs = refs[at:at + n_buf]; at += n_buf
        scr = refs[at:at + n_scr]; at += n_scr
        sems = refs[at:]
        first, last = None, None
        for axis, size in enumerate(grid):
            pid = pl.program_id(axis)
            f, l = pid == 0, pid == size - 1
            first = f if first is None else first & f
            last = l if last is None else last & l

        @pl.when(first)
        def _():
            for start, _ in jobs.steps:
                start(srcs, bufs, sems)

        body(*ins, *outs, *scr)

        @pl.when(last)
        def _():
            for _, finish in jobs.steps:
                finish(srcs, bufs, sems)

    res = pl.pallas_call(
        wrapped, name=name,
        out_shape=out_shape + [s for _, s in jobs.bufs],
        grid=grid,
        in_specs=in_specs + [ANY] * (n_src + n_thru),
        out_specs=out_specs + [ANY] * n_buf,
        scratch_shapes=scratch_shapes + jobs.sems,
        input_output_aliases=aliases,
        compiler_params=_params(*(["arbitrary"] * len(grid))),
    )(*args, *jobs.srcs, *thru)
    return res[:n_out], res[n_out:]


def mm_nn(a, w, *, out_dtype, name, res=None, tm=None, jobs=None):
    m, k = a.shape
    j, k2, ns = w.shape
    whole = j == 1 and ns <= 2048 and k <= 2048
    tm = tm or (1024 if a.dtype == BF16 and not whole else 512)
    assert k == k2 and m % tm == 0
    tn = ns if whole else _tile(ns, 1408)
    nsub = ns // tn
    tk = k if k <= 2048 else _tile(k, 1408)
    nk = k // tk
    has_res = res is not None

    def body(*refs):
        if has_res:
            a_ref, w_ref, r_ref, o_ref, acc = refs
        else:
            a_ref, w_ref, o_ref, acc = refs
        kk = pl.program_id(2)

        @pl.when(kk == 0)
        def _():
            acc[...] = jnp.zeros_like(acc)

        acc[...] += jnp.dot(a_ref[...].astype(BF16), w_ref[...], preferred_element_type=F32)

        @pl.when(kk == nk - 1)
        def _():
            r = acc[...]
            if has_res:
                r = r + r_ref[...]
            o_ref[...] = r.astype(out_dtype)

    in_specs = [
        pl.BlockSpec((tm, tk), lambda i, n, kk: (i, kk)),
        pl.BlockSpec((None, tk, tn), lambda i, n, kk: (n // nsub, kk, n % nsub)),
    ]
    args = [a, w]
    out_tile = pl.BlockSpec((tm, tn), lambda i, n, kk: (i, n))
    if has_res:
        in_specs.append(out_tile)
        args.append(res)
    (out,), bufs = _call(
        body, name=name, jobs=jobs,
        out_shape=[jax.ShapeDtypeStruct((m, j * ns), out_dtype)],
        grid=(m // tm, j * nsub, nk),
        in_specs=in_specs,
        out_specs=[out_tile],
        scratch_shapes=[pltpu.VMEM((tm, tn), F32)],
        args=args, sem=("parallel", "parallel", "arbitrary"))
    return out if jobs is None else (out, bufs)


def mm_nt(dy, w, *, out_dtype, name, tm=None, jobs=None, norm=None):
    parts, m, n = (1,) + dy.shape if dy.ndim == 2 else dy.shape
    n *= parts
    j, k, ns = w.shape
    if norm is not None:
        x, dres, gains, more = norm
        tm = tm or (256 if more else 512)
    tm = tm or 1024
    assert n == j * ns and m % tm == 0
    tn = _tile(ns, 2048)
    nsub = ns // tn
    jb = _shard_group(j // parts, ns, 2048 if norm is None else 1024) if nsub == 1 else 1
    tko = _tile(k, 1408) if norm is None else k
    nn = j * nsub // jb
    per_part = nn // parts
    if dy.ndim == 2:
        dy_spec = pl.BlockSpec((tm, jb * tn), lambda i, ko, nq: (i, nq))
    else:
        dy_spec = pl.BlockSpec((None, tm, jb * tn), lambda i, ko, nq: (nq // per_part, i, nq % per_part))
    if jb == 1:
        w_spec = pl.BlockSpec((None, tko, tn), lambda i, ko, nq: (nq // nsub, ko, nq % nsub))
    else:
        w_spec = pl.BlockSpec((jb, tko, ns), lambda i, ko, nq: (nq, ko, 0))

    n_gain = 0 if norm is None else len(gains)
    n_more = 0 if norm is None else len(more)

    def body(*refs):
        a_ref, w_ref = refs[:2]
        acc = refs[-1]
        nq = pl.program_id(2)
        first = pl.program_id(0) == 0

        @pl.when(nq == 0)
        def _():
            acc[...] = jnp.zeros_like(acc)

        if jb == 1:
            acc[...] += lax.dot_general(a_ref[...].astype(BF16), w_ref[...], (((1,), (1,)), ((), ())),
                                        preferred_element_type=F32)
        else:
            part = acc[...]
            for jj in range(jb):
                part = part + lax.dot_general(a_ref[:, jj * ns:(jj + 1) * ns].astype(BF16), w_ref[jj],
                                              (((1,), (1,)), ((), ())), preferred_element_type=F32)
            acc[...] = part

        @pl.when(nq == nn - 1)
        def _():
            if norm is None:
                refs[2][...] = acc[...].astype(out_dtype)
                return
            x_ref, r_ref = refs[2:4]
            g_refs = refs[4:4 + n_gain]
            e_refs = refs[4 + n_gain:4 + n_gain + n_more]
            dx_ref = refs[4 + n_gain + n_more]
            dg_refs = refs[5 + n_gain + n_more:-1]

            @pl.when(first)
            def _():
                for dg_ref in dg_refs:
                    dg_ref[...] = jnp.zeros_like(dg_ref)

            def rows(c, carry):
                sl = pl.ds(pl.multiple_of(c * NORM_ROWS, NORM_ROWS), NORM_ROWS)
                xv = x_ref[sl, :]
                r = lax.rsqrt(jnp.mean(xv * xv, axis=-1, keepdims=True) + EPS)
                xh = xv * r
                out = r_ref[sl, :]
                for idx, (g_ref, dg_ref) in enumerate(zip(g_refs, dg_refs)):
                    dyv = acc[sl, :] if idx == 0 else e_refs[idx - 1][sl, :].astype(F32)
                    dg_ref[...] += jnp.sum(dyv * xh, axis=0, keepdims=True)
                    dxh = dyv * g_ref[...]
                    out = out + r * (dxh - xh * jnp.mean(dxh * xh, axis=-1, keepdims=True))
                dx_ref[sl, :] = out
                return carry

            lax.fori_loop(0, tm // NORM_ROWS, rows, 0)

    out_tile = pl.BlockSpec((tm, tko), lambda i, ko, nq: (i, ko))
    in_specs, args = [dy_spec, w_spec], [dy, w]
    out_shape, out_specs = [jax.ShapeDtypeStruct((m, k), out_dtype)], [out_tile]
    sem = ("parallel", "parallel", "arbitrary")
    if norm is not None:
        vec = pl.BlockSpec((1, k), lambda i, ko, nq: (0, 0))
        in_specs += [out_tile, out_tile] + [vec] * n_gain + [out_tile] * n_more
        args += [x, dres] + list(gains) + list(more)
        out_shape = [jax.ShapeDtypeStruct((m, k), F32)] + [jax.ShapeDtypeStruct((1, k), F32)] * n_gain
        out_specs = [out_tile] + [vec] * n_gain
        sem = ("arbitrary", "arbitrary", "arbitrary")
    outs, bufs = _call(
        body, name=name, jobs=jobs, out_shape=out_shape, grid=(m // tm, k // tko, nn),
        in_specs=in_specs, out_specs=out_specs, scratch_shapes=[pltpu.VMEM((tm, tko), F32)], args=args, sem=sem)
    out = outs[0] if norm is None else (outs[0], outs[1:])
    return out if jobs is None else (out, bufs)


def mm_tn(x, dy, j, *, name, tm=2048, jobs=None):
    m, k = x.shape
    parts, m2, n = (1,) + dy.shape if dy.ndim == 2 else dy.shape
    n *= parts
    assert m == m2 and n % j == 0 and m % tm == 0
    ns = n // j
    tn = _tile(ns, 1408)
    nsub = ns // tn
    jb = _shard_group(j // parts, ns, 1536) if nsub == 1 else 1
    tk = _tile(k, 1408)
    nm = m // tm
    n_steps = j * nsub // jb
    per_part = n_steps // parts
    if dy.ndim == 2:
        dy_spec = pl.BlockSpec((tm, jb * tn), lambda kq, nq, mi: (mi, nq))
    else:
        dy_spec = pl.BlockSpec((None, tm, jb * tn), lambda kq, nq, mi: (nq // per_part, mi, nq % per_part))
    if jb == 1:
        out_spec = pl.BlockSpec((None, tk, tn), lambda kq, nq, mi: (nq // nsub, kq, nq % nsub))
        acc_shape = (tk, tn)
    else:
        out_spec = pl.BlockSpec((jb, tk, ns), lambda kq, nq, mi: (nq, kq, 0))
        acc_shape = (jb, tk, ns)

    def body(x_ref, dy_ref, o_ref, acc):
        mi = pl.program_id(2)

        @pl.when(mi == 0)
        def _():
            acc[...] = jnp.zeros_like(acc)

        xb = x_ref[...].astype(BF16)
        if jb == 1:
            acc[...] += lax.dot_general(xb, dy_ref[...].astype(BF16), (((0,), (0,)), ((), ())),
                                        preferred_element_type=F32)
        else:
            for jj in range(jb):
                acc[jj] += lax.dot_general(xb, dy_ref[:, jj * ns:(jj + 1) * ns].astype(BF16),
                                           (((0,), (0,)), ((), ())), preferred_element_type=F32)

        @pl.when(mi == nm - 1)
        def _():
            o_ref[...] = acc[...].astype(BF16)

    (out,), bufs = _call(
        body, name=name, jobs=jobs,
        out_shape=[jax.ShapeDtypeStruct((j, k, ns), BF16)],
        grid=(k // tk, n_steps, nm),
        in_specs=[
            pl.BlockSpec((tm, tk), lambda kq, nq, mi: (mi, kq)),
            dy_spec,
        ],
        out_specs=[out_spec],
        scratch_shapes=[pltpu.VMEM(acc_shape, F32)],
        args=[x, dy], sem=("parallel", "parallel", "arbitrary"))
    return out if jobs is None else (out, bufs)


STREAM_TC = LANE


def to_streams(x, *, name):
    s, c = x.shape
    per = s // STREAMS

    def body(x_ref, o_ref):
        for st in range(STREAMS):
            o_ref[pl.ds(st * per, per), :] = x_ref[pl.ds(st, per, stride=STREAMS), :]

    blk = pl.BlockSpec((s, STREAM_TC), lambda i: (0, i))
    return pl.pallas_call(body, name=name, out_shape=jax.ShapeDtypeStruct((s, c), x.dtype), grid=(c // STREAM_TC,),
                          in_specs=[blk], out_specs=blk, compiler_params=_params("parallel"))(x)


def from_streams(x, *, name):
    s, c = x.shape
    per = s // STREAMS

    def body(x_ref, o_ref):
        for st in range(STREAMS):
            o_ref[pl.ds(st, per, stride=STREAMS), :] = x_ref[pl.ds(st * per, per), :]

    blk = pl.BlockSpec((s, STREAM_TC), lambda i: (0, i))
    return pl.pallas_call(body, name=name, out_shape=jax.ShapeDtypeStruct((s, c), x.dtype), grid=(c // STREAM_TC,),
                          in_specs=[blk], out_specs=blk, compiler_params=_params("parallel"))(x)


def rms_fwd(x, gains, *, name, ts=512):
    s, d = x.shape
    n = len(gains)

    def body(x_ref, *refs):
        xv = x_ref[...]
        xh = xv * lax.rsqrt(jnp.mean(xv * xv, axis=-1, keepdims=True) + EPS)
        for g_ref, o_ref in zip(refs[:n], refs[n:]):
            o_ref[...] = (xh * g_ref[...]).astype(BF16)

    row = pl.BlockSpec((ts, d), lambda i: (i, 0))
    vec = pl.BlockSpec((1, d), lambda i: (0, 0))
    return pl.pallas_call(
        body,
        name=name,
        out_shape=[jax.ShapeDtypeStruct((s, d), BF16)] * n,
        grid=(s // ts,),
        in_specs=[row] + [vec] * n,
        out_specs=[row] * n,
        compiler_params=_params("parallel"),
    )(x, *gains)


def loss_head(h, gain, target, *, ts=256):
    s, d = h.shape

    def body(h_ref, g_ref, t_ref, l_ref, dh_ref, dg_ref):
        i = pl.program_id(0)

        @pl.when(i == 0)
        def _():
            l_ref[...] = jnp.zeros_like(l_ref)
            dg_ref[...] = jnp.zeros_like(dg_ref)

        xv = h_ref[...]
        r = lax.rsqrt(jnp.mean(xv * xv, axis=-1, keepdims=True) + EPS)
        xh = xv * r
        g = g_ref[...]
        err = xh * g - t_ref[...]
        l_ref[...] += 0.5 * jnp.sum(jnp.mean(err * err, axis=-1, keepdims=True))
        dy = err * (1.0 / d)
        dg_ref[...] += jnp.sum(dy * xh, axis=0, keepdims=True)
        dxh = dy * g
        dh_ref[...] = r * (dxh - xh * jnp.mean(dxh * xh, axis=-1, keepdims=True))

    row = pl.BlockSpec((ts, d), lambda i: (i, 0))
    vec = pl.BlockSpec((1, d), lambda i: (0, 0))
    return pl.pallas_call(
        body,
        name="loss_head",
        out_shape=[jax.ShapeDtypeStruct((8, LANE), F32), jax.ShapeDtypeStruct((s, d), F32),
                   jax.ShapeDtypeStruct((1, d), F32)],
        grid=(s // ts,),
        in_specs=[row, vec, row],
        out_specs=[pl.BlockSpec((8, LANE), lambda i: (0, 0)), row, vec],
        compiler_params=_params("arbitrary"),
    )(h, gain, target)


A_BLOCK = (2 * GLA_KEY_DIM + 2 * GLA_VAL_DIM) // LANE


def gate_fwd(proj, w_a2p, b_a2, *, ts=512):
    s = proj.shape[0]

    def body(a_ref, w_ref, b_ref, o_ref):
        z = jnp.dot(a_ref[...].astype(BF16), w_ref[...], preferred_element_type=F32) + b_ref[...]
        o_ref[...] = (jnp.minimum(z, 0.0) - jnp.log(1.0 + jnp.exp(-jnp.abs(z)))) * (1.0 / GATE_NORMALIZER)

    return pl.pallas_call(
        body,
        name="gate_fwd",
        out_shape=jax.ShapeDtypeStruct((s, GLA_KEY_DIM), F32),
        grid=(s // ts,),
        in_specs=[pl.BlockSpec((ts, LANE), lambda i: (i, A_BLOCK)),
                  pl.BlockSpec((LANE, GLA_KEY_DIM), lambda i: (0, 0)),
                  pl.BlockSpec((1, GLA_KEY_DIM), lambda i: (0, 0))],
        out_specs=pl.BlockSpec((ts, GLA_KEY_DIM), lambda i: (i, 0)),
        compiler_params=_params("parallel"),
    )(proj, w_a2p, b_a2)


def gate_bwd(proj, w_a2p, b_a2, dla, *, ts=512):
    s = proj.shape[0]

    def body(a_ref, w_ref, b_ref, dla_ref, da_ref, dw_ref, db_ref):
        i = pl.program_id(0)

        @pl.when(i == 0)
        def _():
            dw_ref[...] = jnp.zeros_like(dw_ref)
            db_ref[...] = jnp.zeros_like(db_ref)

        a = a_ref[...].astype(BF16)
        w = w_ref[...]
        z = jnp.dot(a, w, preferred_element_type=F32) + b_ref[...]
        dz = dla_ref[...] * (1.0 / GATE_NORMALIZER) / (1.0 + jnp.exp(z))
        dzb = dz.astype(BF16)
        da_ref[...] = lax.dot_general(dzb, w, (((1,), (1,)), ((), ())), preferred_element_type=F32).astype(BF16)
        dw_ref[...] += lax.dot_general(a, dzb, (((0,), (0,)), ((), ())), preferred_element_type=F32)
        db_ref[...] += jnp.sum(dz, axis=0, keepdims=True)

    return pl.pallas_call(
        body,
        name="gate_bwd",
        out_shape=[jax.ShapeDtypeStruct((s, LANE), BF16), jax.ShapeDtypeStruct((LANE, GLA_KEY_DIM), F32),
                   jax.ShapeDtypeStruct((1, GLA_KEY_DIM), F32)],
        grid=(s // ts,),
        in_specs=[pl.BlockSpec((ts, LANE), lambda i: (i, A_BLOCK)),
                  pl.BlockSpec((LANE, GLA_KEY_DIM), lambda i: (0, 0)),
                  pl.BlockSpec((1, GLA_KEY_DIM), lambda i: (0, 0)),
                  pl.BlockSpec((ts, GLA_KEY_DIM), lambda i: (i, 0))],
        out_specs=[pl.BlockSpec((ts, LANE), lambda i: (i, 0)),
                   pl.BlockSpec((LANE, GLA_KEY_DIM), lambda i: (0, 0)),
                   pl.BlockSpec((1, GLA_KEY_DIM), lambda i: (0, 0))],
        compiler_params=_params("arbitrary"),
    )(proj, w_a2p, b_a2, dla)


def _masked_sum(mask, x):
    m = mask.astype(BF16)
    hi = x.astype(BF16)
    rest = x - hi.astype(F32)
    mid = rest.astype(BF16)
    lo = (rest - mid.astype(F32)).astype(BF16)
    dot = lambda t: jnp.dot(m, t, preferred_element_type=F32)
    return dot(hi) + dot(mid) + dot(lo)


def _chunk_terms(q, k, la):
    c_len = GLA_CHUNK
    row = lax.broadcasted_iota(jnp.int32, (c_len, c_len), 0)
    col = lax.broadcasted_iota(jnp.int32, (c_len, c_len), 1)
    tri = row >= col
    c = _masked_sum(tri, la)
    last = jnp.sum(la, axis=0, keepdims=True)
    q_dec = q * (GLA_DK ** -0.5) * jnp.exp(c)
    k_inv = k * jnp.exp(-c)
    k_end = k * jnp.exp(last - c)
    return c, last, q_dec, k_inv, k_end, tri


def _dot(a, b, ca, cb):
    return lax.dot_general(a.astype(BF16), b.astype(BF16), (((ca,), (cb,)), ((), ())), preferred_element_type=F32)


def gla_fwd(proj, la, hn, jobs=None):
    s = proj.shape[0]
    n_chunks = s // GLA_CHUNK
    rows = GLA_CHUNK * GLA_STEP_CHUNKS

    def body(q_ref, k_ref, v_ref, r_ref, la_ref, hn_ref, o_ref, st_out, og_ref, st):
        @pl.when(pl.program_id(0) == 0)
        def _():
            st[...] = jnp.zeros_like(st)

        for h in range(GLA_HEADS):
            hk = slice(h * GLA_DK, (h + 1) * GLA_DK)
            hv = slice(h * GLA_DV, (h + 1) * GLA_DV)
            for cc in range(GLA_STEP_CHUNKS):
                rs = slice(cc * GLA_CHUNK, (cc + 1) * GLA_CHUNK)
                _, last, q_dec, k_inv, k_end, tri = _chunk_terms(q_ref[rs, hk], k_ref[rs, hk], la_ref[rs, hk])
                v = v_ref[rs, hv]
                a = jnp.where(tri, _dot(q_dec, k_inv, 1, 1), 0.0)
                state = st[h]
                st_out[h, cc] = state
                ov = _dot(a, v, 1, 0) + _dot(q_dec, state, 1, 1)
                o_ref[rs, hv] = ov
                st[h] = state * jnp.exp(last) + _dot(v, k_end, 0, 0)
                oh = ov * lax.rsqrt(jnp.mean(ov * ov, axis=-1, keepdims=True) + EPS)
                r = r_ref[rs, hv]
                og_ref[rs, hv] = (oh * hn_ref[...] * (r * jax.nn.sigmoid(r))).astype(BF16)

    key = lambda col: pl.BlockSpec((rows, GLA_KEY_DIM), lambda n: (n, col))
    val = lambda col: pl.BlockSpec((rows, GLA_VAL_DIM), lambda n: (n, col))
    outs, bufs = _call(
        body, name="gla_fwd", jobs=jobs,
        out_shape=[jax.ShapeDtypeStruct((s, GLA_VAL_DIM), F32),
                   jax.ShapeDtypeStruct((GLA_HEADS, n_chunks, GLA_DV, GLA_DK), F32),
                   jax.ShapeDtypeStruct((s, GLA_VAL_DIM), BF16)],
        grid=(n_chunks // GLA_STEP_CHUNKS,),
        in_specs=[key(0), key(1), val(1), val(R_BLOCK // GLA_HEADS), key(0), pl.BlockSpec((1, GLA_DV), lambda n: (0, 0))],
        out_specs=[val(0), pl.BlockSpec((GLA_HEADS, GLA_STEP_CHUNKS, GLA_DV, GLA_DK), lambda n: (0, n, 0, 0)), val(0)],
        scratch_shapes=[pltpu.VMEM((GLA_HEADS, GLA_DV, GLA_DK), F32)],
        args=[proj, proj, proj, proj, la, hn], sem=("arbitrary",))
    return outs if jobs is None else (outs, bufs)


def gla_bwd(proj, la, states, o, hn, dog, jobs=None):
    s = proj.shape[0]
    n_steps = s // GLA_CHUNK // GLA_STEP_CHUNKS
    lastc = n_steps - 1
    rows = GLA_CHUNK * GLA_STEP_CHUNKS

    def body(q_ref, k_ref, v_ref, r_ref, la_ref, o_ref, hn_ref, dog_ref, st_ref,
             dq_ref, dk_ref, dv_ref, dr_ref, dla_ref, dhn_ref, dst):
        @pl.when(pl.program_id(0) == 0)
        def _():
            dst[...] = jnp.zeros_like(dst)
            dhn_ref[...] = jnp.zeros_like(dhn_ref)

        upper = (lax.broadcasted_iota(jnp.int32, (GLA_CHUNK, GLA_CHUNK), 0)
                 <= lax.broadcasted_iota(jnp.int32, (GLA_CHUNK, GLA_CHUNK), 1))
        gain = hn_ref[...]
        for h in range(GLA_HEADS):
            hk = slice(h * GLA_DK, (h + 1) * GLA_DK)
            hv = slice(h * GLA_DV, (h + 1) * GLA_DV)
            for cc in reversed(range(GLA_STEP_CHUNKS)):
                rs = slice(cc * GLA_CHUNK, (cc + 1) * GLA_CHUNK)
                ov = o_ref[rs, hv]
                inv = lax.rsqrt(jnp.mean(ov * ov, axis=-1, keepdims=True) + EPS)
                oh = ov * inv
                r = r_ref[rs, hv]
                sig = jax.nn.sigmoid(r)
                dgv = dog_ref[rs, hv]
                d_on = dgv * (r * sig)
                dr_ref[rs, hv] = (dgv * (oh * gain) * (sig * (1.0 + r * (1.0 - sig)))).astype(BF16)
                dhn_ref[...] += jnp.sum(d_on * oh, axis=0, keepdims=True)
                doh = d_on * gain
                dout = inv * (doh - oh * jnp.mean(doh * oh, axis=-1, keepdims=True))
                c, last, q_dec, k_inv, k_end, tri = _chunk_terms(q_ref[rs, hk], k_ref[rs, hk], la_ref[rs, hk])
                v = v_ref[rs, hv]
                state = st_ref[h, cc]
                dstate = dst[h]
                e_last = jnp.exp(last)
                a = jnp.where(tri, _dot(q_dec, k_inv, 1, 1), 0.0)
                da = jnp.where(tri, _dot(dout, v, 1, 1), 0.0)
                dv_ref[rs, hv] = (_dot(a, dout, 0, 0) + _dot(k_end, dstate, 1, 1)).astype(BF16)
                dq_dec = _dot(da, k_inv, 1, 0) + _dot(dout, state, 1, 0)
                dk_inv = _dot(da, q_dec, 0, 0)
                dk_end = _dot(v, dstate, 1, 0)
                dst[h] = dstate * e_last + _dot(dout, q_dec, 0, 0)
                dq_ref[rs, hk] = (dq_dec * (GLA_DK ** -0.5) * jnp.exp(c)).astype(BF16)
                dk_ref[rs, hk] = (dk_inv * jnp.exp(-c) + dk_end * jnp.exp(last - c)).astype(BF16)
                ke_term = dk_end * k_end
                dc = dq_dec * q_dec - dk_inv * k_inv - ke_term
                dlast = (jnp.sum(ke_term, axis=0, keepdims=True)
                         + e_last * jnp.sum(dstate * state, axis=0, keepdims=True))
                dla_ref[rs, hk] = _masked_sum(upper, dc) + dlast

    key = lambda col: pl.BlockSpec((rows, GLA_KEY_DIM), lambda n: (lastc - n, col))
    val = lambda col: pl.BlockSpec((rows, GLA_VAL_DIM), lambda n: (lastc - n, col))
    vec = pl.BlockSpec((1, GLA_DV), lambda n: (0, 0))
    outs, bufs = _call(
        body, name="gla_bwd", jobs=jobs,
        out_shape=[jax.ShapeDtypeStruct((s, GLA_KEY_DIM), BF16), jax.ShapeDtypeStruct((s, GLA_KEY_DIM), BF16),
                   jax.ShapeDtypeStruct((s, GLA_VAL_DIM), BF16), jax.ShapeDtypeStruct((s, GLA_VAL_DIM), BF16),
                   jax.ShapeDtypeStruct((s, GLA_KEY_DIM), F32), jax.ShapeDtypeStruct((1, GLA_DV), F32)],
        grid=(n_steps,),
        in_specs=[key(0), key(1), val(1), val(R_BLOCK // GLA_HEADS), key(0), val(0), vec, val(0),
                  pl.BlockSpec((GLA_HEADS, GLA_STEP_CHUNKS, GLA_DV, GLA_DK), lambda n: (0, lastc - n, 0, 0))],
        out_specs=[key(0), key(0), val(0), val(0), key(0), vec],
        scratch_shapes=[pltpu.VMEM((GLA_HEADS, GLA_DV, GLA_DK), F32)],
        args=[proj, proj, proj, proj, la, o, hn, dog, states], sem=("arbitrary",))
    return outs if jobs is None else (outs, bufs)


R_BLOCK = (2 * GLA_KEY_DIM + GLA_VAL_DIM) // GLA_DV


CONV_TC = 128
SQRT_HALF = 0.7071067811865476
INV_SQRT_2PI = 0.3989422804014327


def _conv_gate(g_ref, cw_ref, cb_ref):
    g0 = g_ref[...].astype(F32)
    t = lax.broadcasted_iota(jnp.int32, g0.shape, 0)
    g1 = jnp.where(t >= 1, pltpu.roll(g0, 1, 0), 0.0)
    g2 = jnp.where(t >= 2, pltpu.roll(g0, 2, 0), 0.0)
    gc = cw_ref[0:1, :] * g2 + cw_ref[1:2, :] * g1 + cw_ref[2:3, :] * g0 + cb_ref[...]
    return g0, g1, g2, gc, t


def convglu_fwd(up, conv_w, conv_b, *, name, jobs=None):
    s = up.shape[0]
    nc = D_FF // CONV_TC

    def body(u_ref, g_ref, cw_ref, cb_ref, o_ref):
        _, _, _, gc, _ = _conv_gate(g_ref, cw_ref, cb_ref)
        gelu = 0.5 * gc * (1.0 + lax.erf(gc * SQRT_HALF))
        o_ref[...] = (gelu * u_ref[...].astype(F32)).astype(BF16)

    (out,), bufs = _call(
        body, name=name, jobs=jobs,
        out_shape=[jax.ShapeDtypeStruct((s, D_FF), BF16)],
        grid=(nc,),
        in_specs=[pl.BlockSpec((s, CONV_TC), lambda c: (0, c)),
                  pl.BlockSpec((s, CONV_TC), lambda c: (0, nc + c)),
                  pl.BlockSpec((3, CONV_TC), lambda c: (0, c)),
                  pl.BlockSpec((1, CONV_TC), lambda c: (0, c))],
        out_specs=[pl.BlockSpec((s, CONV_TC), lambda c: (0, c))],
        args=[up, up, conv_w, conv_b], sem=("parallel",))
    return out if jobs is None else (out, bufs)


def convglu_bwd(up, conv_w, conv_b, dact, *, name, jobs=None):
    s = up.shape[0]
    nc = D_FF // CONV_TC

    def body(u_ref, g_ref, cw_ref, cb_ref, da_ref, dup_ref, dcw_ref, dcb_ref):
        du_ref, dg_ref = dup_ref.at[0], dup_ref.at[1]
        g0, g1, g2, gc, t = _conv_gate(g_ref, cw_ref, cb_ref)
        cdf = 0.5 * (1.0 + lax.erf(gc * SQRT_HALF))
        da = da_ref[...].astype(F32)
        du_ref[...] = (da * gc * cdf).astype(BF16)
        dgc = da * u_ref[...].astype(F32) * (cdf + gc * jnp.exp(-0.5 * gc * gc) * INV_SQRT_2PI)
        dcb_ref[...] = jnp.sum(dgc, axis=0, keepdims=True)
        dcw_ref[0:1, :] = jnp.sum(dgc * g2, axis=0, keepdims=True)
        dcw_ref[1:2, :] = jnp.sum(dgc * g1, axis=0, keepdims=True)
        dcw_ref[2:3, :] = jnp.sum(dgc * g0, axis=0, keepdims=True)
        n1 = jnp.where(t < s - 1, pltpu.roll(dgc, s - 1, 0), 0.0)
        n2 = jnp.where(t < s - 2, pltpu.roll(dgc, s - 2, 0), 0.0)
        dg_ref[...] = (cw_ref[2:3, :] * dgc + cw_ref[1:2, :] * n1 + cw_ref[0:1, :] * n2).astype(BF16)

    col = pl.BlockSpec((s, CONV_TC), lambda c: (0, c))
    outs, bufs = _call(
        body, name=name, jobs=jobs,
        out_shape=[jax.ShapeDtypeStruct((2, s, D_FF), BF16),
                   jax.ShapeDtypeStruct((3, D_FF), F32), jax.ShapeDtypeStruct((1, D_FF), F32)],
        grid=(nc,),
        in_specs=[col, pl.BlockSpec((s, CONV_TC), lambda c: (0, nc + c)),
                  pl.BlockSpec((3, CONV_TC), lambda c: (0, c)),
                  pl.BlockSpec((1, CONV_TC), lambda c: (0, c)), col],
        out_specs=[pl.BlockSpec((2, s, CONV_TC), lambda c: (0, 0, c)), pl.BlockSpec((3, CONV_TC), lambda c: (0, c)),
                   pl.BlockSpec((1, CONV_TC), lambda c: (0, c))],
        args=[up, up, conv_w, conv_b, dact], sem=("parallel",))
    return outs if jobs is None else (outs, bufs)


SLOPE_TILE = (8, LANE)


def _slope_table():
    return jnp.broadcast_to(jnp.asarray(ALIBI_SLOPES, F32)[:, None, None], (ATT_HEADS,) + SLOPE_TILE)


def _pieces(s_len, d):
    npc = STREAMS // d
    lp = ATT_BLOCK // npc
    return npc, lp, (s_len // STREAMS) // lp


def _gather(ref, r, b, d, s_len):
    npc, lp, _ = _pieces(s_len, d)
    per = s_len // STREAMS
    parts = [ref[pl.ds((r + d * k) * per + b * lp, lp), :] for k in range(npc)]
    return parts[0] if npc == 1 else jnp.concatenate(parts, axis=0)


def _scatter(ref, r, b, d, s_len, val, add=False):
    npc, lp, _ = _pieces(s_len, d)
    per = s_len // STREAMS
    for k in range(npc):
        rows = pl.ds((r + d * k) * per + b * lp, lp)
        piece = val[k * lp:(k + 1) * lp]
        if add:
            ref[rows, :] += piece
        else:
            ref[rows, :] = piece


def _stream_bias(slope, d, s_len):
    npc, lp, _ = _pieces(s_len, d)
    qi = lax.broadcasted_iota(jnp.int32, (ATT_BLOCK, 2 * ATT_BLOCK), 0)
    c = lax.broadcasted_iota(jnp.int32, (ATT_BLOCK, 2 * ATT_BLOCK), 1)
    own = c // ATT_BLOCK
    cc = c - own * ATT_BLOCK
    dist = npc * ((qi % lp) - (cc % lp) + lp * (1 - own)) + (qi // lp - cc // lp)
    ok = (dist >= 0) & (dist <= ATT_BLOCK)
    return jnp.where(ok, (slope * (-float(d))) * dist.astype(F32), NEG)


def attn_fwd(q, kv, jobs=None):
    s_len = q.shape[0]
    scale = HEAD_DIM ** -0.5

    def body(sl_ref, q_ref, k_ref, v_ref, o_ref, lse_ref):
        g = pl.program_id(1)
        slope = sl_ref[0:1, 0:1]

        def branch(gi, d):
            _, _, nblk = _pieces(s_len, d)
            bias = _stream_bias(slope, d, s_len)
            for r in range(d):
                for b in range(nblk):
                    qb = _gather(q_ref, r, b, d, s_len)
                    kc, vc = _gather(k_ref, r, b, d, s_len), _gather(v_ref, r, b, d, s_len)
                    if b == 0:
                        kcat, vcat, bb = kc, vc, bias[:, ATT_BLOCK:]
                    else:
                        kcat = jnp.concatenate([_gather(k_ref, r, b - 1, d, s_len), kc], axis=0)
                        vcat = jnp.concatenate([_gather(v_ref, r, b - 1, d, s_len), vc], axis=0)
                        bb = bias
                    sc = _dot(qb, kcat, 1, 1) * scale + bb
                    m = jnp.max(sc, axis=-1, keepdims=True)
                    p = jnp.exp(sc - m)
                    l = jnp.sum(p, axis=-1, keepdims=True)
                    o_new = _dot(p, vcat, 1, 0) / l
                    lse_new = m + jnp.log(l)
                    if gi > 0:
                        lse_old = _gather(lse_ref, r, b, d, s_len)[:, 0:1]
                        top = jnp.maximum(lse_old, lse_new)
                        e_old, e_new = jnp.exp(lse_old - top), jnp.exp(lse_new - top)
                        den = e_old + e_new
                        o_new = (e_old * _gather(o_ref, r, b, d, s_len) + e_new * o_new) / den
                        lse_new = top + jnp.log(den)
                    _scatter(o_ref, r, b, d, s_len, o_new)
                    _scatter(lse_ref, r, b, d, s_len, jnp.broadcast_to(lse_new, (ATT_BLOCK, HEAD_DIM)))

        for gi, d in enumerate(DILATIONS):
            @pl.when(g == gi)
            def _():
                branch(gi, d)

    blk = lambda col: pl.BlockSpec((s_len, HEAD_DIM), lambda h, g: (0, col(h, g)))
    head = lambda h, g: h
    outs, bufs = _call(
        body, name="attn_fwd", jobs=jobs,
        out_shape=[jax.ShapeDtypeStruct((s_len, ATT_HEADS * HEAD_DIM), F32)] * 2,
        grid=(ATT_HEADS, len(DILATIONS)),
        in_specs=[pl.BlockSpec((None,) + SLOPE_TILE, lambda h, g: (h, 0, 0)),
                  blk(lambda h, g: g * ATT_HEADS + h), blk(head), blk(lambda h, g: ATT_HEADS + h)],
        out_specs=[blk(head), blk(head)],
        args=[_slope_table(), q, kv, kv], sem=("parallel", "arbitrary"))
    return outs if jobs is None else (outs, bufs)


def attn_bwd(q, kv, o, lse, do, jobs=None):
    s_len = q.shape[0]
    scale = HEAD_DIM ** -0.5
    chunks = s_len // ATT_BLOCK

    def body(sl_ref, q_ref, k_ref, v_ref, o_ref, lse_ref, do_ref, dq_ref, dkv_ref, dlt):
        g = pl.program_id(1)
        slope = sl_ref[0:1, 0:1]
        dk_ref, dv_ref = dkv_ref.at[0], dkv_ref.at[1]

        @pl.when(g == 0)
        def _():
            dkv_ref[...] = jnp.zeros_like(dkv_ref)

            def deltas(c, carry):
                rows = pl.ds(pl.multiple_of(c * ATT_BLOCK, ATT_BLOCK), ATT_BLOCK)
                dlt[rows, :] = jnp.sum(do_ref[rows, :] * o_ref[rows, :], axis=-1, keepdims=True)
                return carry
            lax.fori_loop(0, chunks, deltas, 0)

        def branch(d):
            _, _, nblk = _pieces(s_len, d)
            bias = _stream_bias(slope, d, s_len)
            for r in range(d):
                for b in range(nblk):
                    qb = _gather(q_ref, r, b, d, s_len)
                    dob = _gather(do_ref, r, b, d, s_len)
                    kc, vc = _gather(k_ref, r, b, d, s_len), _gather(v_ref, r, b, d, s_len)
                    if b == 0:
                        kcat, vcat, bb = kc, vc, bias[:, ATT_BLOCK:]
                    else:
                        kcat = jnp.concatenate([_gather(k_ref, r, b - 1, d, s_len), kc], axis=0)
                        vcat = jnp.concatenate([_gather(v_ref, r, b - 1, d, s_len), vc], axis=0)
                        bb = bias
                    sc = _dot(qb, kcat, 1, 1) * scale + bb
                    p = jnp.exp(sc - _gather(lse_ref, r, b, d, s_len)[:, 0:1])
                    ds = p * (_dot(dob, vcat, 1, 1) - _gather(dlt, r, b, d, s_len))
                    _scatter(dq_ref, r, b, d, s_len, _dot(ds, kcat, 1, 0) * scale)
                    dk = _dot(ds, qb, 0, 0) * scale
                    dv = _dot(p, dob, 0, 0)
                    if b == 0:
                        _scatter(dk_ref, r, b, d, s_len, dk, add=True)
                        _scatter(dv_ref, r, b, d, s_len, dv, add=True)
                    else:
                        _scatter(dk_ref, r, b - 1, d, s_len, dk[:ATT_BLOCK], add=True)
                        _scatter(dv_ref, r, b - 1, d, s_len, dv[:ATT_BLOCK], add=True)
                        _scatter(dk_ref, r, b, d, s_len, dk[ATT_BLOCK:], add=True)
                        _scatter(dv_ref, r, b, d, s_len, dv[ATT_BLOCK:], add=True)

        for gi, d in enumerate(DILATIONS):
            @pl.when(g == gi)
            def _():
                branch(d)

    blk = lambda col: pl.BlockSpec((s_len, HEAD_DIM), lambda h, g: (0, col(h, g)))
    head = lambda h, g: h
    q_col = lambda h, g: g * ATT_HEADS + h
    outs, bufs = _call(
        body, name="attn_bwd", jobs=jobs,
        out_shape=[jax.ShapeDtypeStruct(q.shape, F32), jax.ShapeDtypeStruct((2, s_len, ATT_HEADS * HEAD_DIM), F32)],
        grid=(ATT_HEADS, len(DILATIONS)),
        in_specs=[pl.BlockSpec((None,) + SLOPE_TILE, lambda h, g: (h, 0, 0)),
                  blk(q_col), blk(head), blk(lambda h, g: ATT_HEADS + h), blk(head), blk(head), blk(head)],
        out_specs=[blk(q_col), pl.BlockSpec((2, s_len, HEAD_DIM), lambda h, g: (0, 0, h))],
        scratch_shapes=[pltpu.VMEM((s_len, 1), F32)],
        args=[_slope_table(), q, kv, kv, o, lse, do], sem=("parallel", "arbitrary"))
    return outs if jobs is None else (outs, bufs)


def _adam(w, g, m, v):
    m = ADAM_B1 * m + (1.0 - ADAM_B1) * g
    v = ADAM_B2 * v + (1.0 - ADAM_B2) * (g * g)
    m_hat = m / (1.0 - ADAM_B1 ** ADAM_STEP)
    v_hat = v / (1.0 - ADAM_B2 ** ADAM_STEP)
    delta = -ADAM_LR * (m_hat / (jnp.sqrt(v_hat) + ADAM_EPS) + ADAM_WD * w)
    return delta, m, v


def adam_sharded(recvs, w, m, v, *, name):
    layers = len(recvs)
    n_src, r, c = recvs[0].shape
    tr = _rows(r, c)

    def body(*refs):
        p_refs = refs[:layers]
        w_ref, m_ref, v_ref, g_ref, d_ref, mo_ref, vo_ref = refs[layers:]
        for layer, p_ref in enumerate(p_refs):
            @pl.when(pl.program_id(0) == layer)
            def _():
                g = p_ref[0].astype(F32)
                for src in range(1, n_src):
                    g = g + p_ref[src].astype(F32)
                delta, m_new, v_new = _adam(w_ref[...], g, m_ref[...], v_ref[...])
                g_ref[...] = g
                d_ref[...] = delta
                mo_ref[...] = m_new
                vo_ref[...] = v_new

    blk = pl.BlockSpec((None, tr, c), lambda l, i: (l, i, 0))
    out = jax.ShapeDtypeStruct((layers, r, c), F32)
    part = [pl.BlockSpec((n_src, tr, c), functools.partial(lambda l, i, layer: (0, jnp.where(l == layer, i, 0), 0),
                                                            layer=layer)) for layer in range(layers)]
    return pl.pallas_call(
        body,
        name=name,
        out_shape=[out] * 4,
        grid=(layers, r // tr),
        in_specs=part + [blk, blk, blk],
        out_specs=[blk] * 4,
        compiler_params=_params("parallel", "parallel"),
    )(*recvs, w, m, v)


def sum_partials(parts):
    n_src, r, c = parts.shape

    def body(p_ref, o_ref):
        g = p_ref[0]
        for src in range(1, n_src):
            g = g + p_ref[src]
        o_ref[...] = g

    return pl.pallas_call(
        body,
        name="sum_small_grads",
        out_shape=jax.ShapeDtypeStruct((r, c), F32),
    )(parts)


def adam_packed(w, g, m, v):
    def body(w_ref, g_ref, m_ref, v_ref, d_ref, mo_ref, vo_ref):
        delta, m_new, v_new = _adam(w_ref[...], g_ref[...], m_ref[...], v_ref[...])
        d_ref[...] = delta
        mo_ref[...] = m_new
        vo_ref[...] = v_new

    out = jax.ShapeDtypeStruct(w.shape, F32)
    return pl.pallas_call(body, name="adam_small", out_shape=[out] * 3)(w, g, m, v)


def all_gather(srcs, *, name):
    n = len(srcs)

    def body(*refs):
        src, dst = refs[:n], refs[n:2 * n]
        send_sems, recv_sems, local_sems = refs[2 * n:]
        x, y, c, me = _place()
        sibling = (x, y, 1 - c)
        chips = [(1 - x, y), (x, 1 - y), (1 - x, 1 - y)]

        def index(px, py, pc):
            return 4 * px + 2 * py + pc

        def copy(p, k, block, to, from_src=False):
            slot = dst[p].at[index(*block)]
            return pltpu.make_async_remote_copy(
                src_ref=src[p] if from_src else slot, dst_ref=slot,
                send_sem=send_sems.at[p, k], recv_sem=recv_sems.at[p, k],
                device_id=to, device_id_type=MESH)

        mine = [pltpu.make_async_copy(src[p], dst[p].at[me], local_sems.at[p]) for p in range(n)]
        for cp in mine:
            cp.start()
        first = []
        for p in range(n):
            first.append(copy(p, 0, (x, y, c), sibling, from_src=True))
            for jj, chip in enumerate(chips):
                first.append(copy(p, 1 + jj, (x, y, c), (*chip, c), from_src=True))
        for cp in first:
            cp.start()
        passed = []
        for jj, chip in enumerate(chips):
            for p in range(n):
                copy(p, 1 + jj, (*chip, c), (x, y, c)).wait_recv()
                fwd = copy(p, 4 + jj, (*chip, c), sibling)
                fwd.start()
                passed.append(fwd)
        for p in range(n):
            copy(p, 0, sibling, (x, y, c)).wait_recv()
            for jj, chip in enumerate(chips):
                copy(p, 4 + jj, (*chip, 1 - c), (x, y, c)).wait_recv()
        for cp in first + passed:
            cp.wait_send()
        for cp in mine:
            cp.wait()

    return pl.pallas_call(
        body,
        name=name,
        out_shape=[jax.ShapeDtypeStruct((N_DEV,) + a.shape, a.dtype) for a in srcs],
        in_specs=[ANY] * n,
        out_specs=[ANY] * n,
        scratch_shapes=[pltpu.SemaphoreType.DMA((n, 7)), pltpu.SemaphoreType.DMA((n, 7)),
                        pltpu.SemaphoreType.DMA((n,))],
    )(*srcs)


def exchange_only(*, name, jobs):
    def body(o_ref):
        o_ref[...] = jnp.zeros_like(o_ref)

    _, bufs = _call(body, name=name, jobs=jobs, out_shape=[jax.ShapeDtypeStruct((8, LANE), F32)], grid=(1,),
                    in_specs=[], out_specs=[pl.BlockSpec((8, LANE), lambda i: (0, 0))], args=[], sem=("arbitrary",))
    return None, bufs


def _pack_rows(parts, rows):
    flat = jnp.concatenate([p.reshape(-1) for p in parts])
    return jnp.pad(flat, (0, rows * LANE - flat.shape[0])).reshape(rows, LANE)


def _unpack_rows(packed, shapes):
    flat = packed.reshape(-1)
    out, at = [], 0
    for sh in shapes:
        size = 1
        for dim in sh:
            size *= dim
        out.append(flat[at:at + size].reshape(sh))
        at += size
    return out


CONV_W_PAD = 768
SMALL_W_ROWS = 56


def _pack_small_weights(w_a2, b_a2, hn, conv_w):
    cw = jnp.pad(conv_w.reshape(6, -1), ((0, 0), (0, CONV_W_PAD - conv_w.shape[-1]))).reshape(-1, LANE)
    rows = jnp.concatenate([w_a2[0], b_a2, jnp.pad(hn, ((0, 0), (0, LANE - hn.shape[-1]))), cw], axis=0)
    return jnp.pad(rows, ((0, SMALL_W_ROWS - rows.shape[0]), (0, 0)))


def _unpack_small_weights(gathered):
    w_a2 = gathered[:, 0:GATE_RANK, :].transpose(1, 0, 2).reshape(GATE_RANK, GLA_KEY_DIM)
    b_a2 = gathered[:, GATE_RANK, :].reshape(1, GLA_KEY_DIM)
    hn = gathered[:, GATE_RANK + 1, :GLA_DV // N_DEV].reshape(1, GLA_DV)
    per = D_FF // N_DEV
    cw = gathered[:, GATE_RANK + 2:GATE_RANK + 2 + 6 * CONV_W_PAD // LANE, :].reshape(N_DEV, 6, CONV_W_PAD)[:, :, :per]
    cw = cw.reshape(N_DEV, 2, 3, per).transpose(1, 2, 0, 3).reshape(2, 3, D_FF)
    return w_a2, b_a2, hn, cw


SCHEDULE = {
    "gla_in": [("g1", "gout", None), ("g1", "up0", (0, 1024))],
    "gla_fwd": [("g2", "gout", None), ("g2", "up0", (0, 1024)), ("g1", "up0", (1024, 2048))],
    "gla_out": [("g2", "up0", (1024, 2048)), ("g1", "dn0", (0, 352))],
    "ffn_up0": [("g2", "dn0", (0, 352)), ("g1", "dn0", (352, 704)), ("g1", "kv", None), ("g1", "q", (0, 768))],
    "convglu_fwd0": [("g2", "dn0", (352, 704))],
    "ffn_down0": [("g2", "kv", None), ("g2", "q", (0, 768)), ("g1", "q", (768, 2048)), ("g1", "dout", None)],
    "kv_proj": [("g2", "q", (768, 2048)), ("g2", "dout", None), ("g1", "up1", (0, 704))],
    "q_proj": [("g2", "up1", (0, 704)), ("g1", "up1", (704, 1664))],
    "attn_fwd": [("g2", "up1", (704, 1664)), ("g1", "up1", (1664, 2048)), ("g1", "dn1", None)],
    "dsa_out": [("g2", "up1", (1664, 2048)), ("g2", "dn1", None)],
    "ffn_down_dx1": [("sc", "dn1", (0, 352))],
    "convglu_bwd1": [("sc", "dn1", (352, 704))],
    "ffn_up_dx1": [("sc", "up1", (0, 1024))],
    "attn_bwd": [("sc", "up1", (1024, 2048)), ("sc", "dout", None)],
    "q_proj_dx": [("sc", "q", (0, 1024))],
    "kv_proj_dw": [("sc", "q", (1024, 1792))],
    "kv_proj_dx": [("sc", "q", (1792, 2048)), ("sc", "kv", (0, 768))],
    "ffn_down_dw0": [("sc", "kv", (768, 2048))],
    "ffn_down_dx0": [("sc", "dn0", (0, 384))],
    "convglu_bwd0": [("sc", "dn0", (384, 704))],
    "ffn_up_dx0": [("sc", "up0", (0, 1024))],
    "gla_out_dw": [("sc", "up0", (1024, 1216))],
    "gla_out_dx": [("sc", "up0", (1216, 1408))],
    "gla_bwd": [("sc", "up0", (1408, 2048))],
    "gla_in_dw": [("sc", "gout", None)],
    "gla_in_dx": [("sc", "in", (0, 1536))],
    "grads_tail": [("sc", "in", (1536, 2048)), ("all", "small", None)],
}
ROW_SHARDED = ("gout", "dout", "dn0", "dn1")


class Plan:
    def __init__(self, weights, srcs=None):
        self.w = dict(weights)
        self.srcs = srcs
        self.grads = {}
        self.recv = {}
        self._names = None

    def weight(self, name):
        buf = self.w[name]
        if name in ROW_SHARDED:
            return buf.reshape(1, buf.shape[0] * buf.shape[1], buf.shape[2])
        return buf

    def jobs(self, call):
        ops = SCHEDULE.get(call)
        if self.srcs is None or not ops:
            return None
        jobs, handles = Jobs(), {}
        backward = ops[0][0] in ("sc", "all")
        for op, name, rows in ops:
            assert (op in ("sc", "all")) == backward
            store = self.recv if backward else self.w
            if name not in handles:
                if name in store:
                    handles[name] = jobs.thru(store[name])
                elif op == "sc":
                    handles[name] = jobs.new(self.grads[name].shape, BF16)
                elif op == "all":
                    handles[name] = jobs.new((N_DEV,) + self.grads[name].shape, self.grads[name].dtype)
                else:
                    handles[name] = jobs.new((N_DEV,) + self.srcs[name].shape, BF16)
            if op == "g1":
                jobs.gather_ici(self.srcs[name], handles[name], rows)
            elif op == "g2":
                jobs.gather_d2d(handles[name], rows)
            else:
                jobs.scatter(self.grads[name], handles[name], rows, same=op == "all")
        self._names = [(name, self.recv if backward else self.w) for name in handles]
        return jobs

    def run(self, call, fn, *args, **kwargs):
        jobs = self.jobs(call)
        if jobs is None:
            return fn(*args, **kwargs)
        out, bufs = fn(*args, jobs=jobs, **kwargs)
        for (name, store), buf in zip(self._names, bufs):
            store[name] = buf
        return out


def _ffn_fwd(plan, h, norm_g, conv_w, conv_b, tag):
    (n,) = rms_fwd(h, [norm_g], name=f"ffn_norm_fwd{tag}")
    up = plan.run(f"ffn_up{tag}", mm_nn, n, plan.weight(f"up{tag}"), out_dtype=BF16, name=f"ffn_up{tag}")
    act = plan.run(f"convglu_fwd{tag}", convglu_fwd, up, conv_w, conv_b, name=f"convglu_fwd{tag}")
    h_out = plan.run(f"ffn_down{tag}", mm_nn, act, plan.weight(f"dn{tag}"), out_dtype=F32, res=h,
                     name=f"ffn_down{tag}")
    return h_out, (n, up, act)


def _by_rows(dw):
    return dw.reshape(N_DEV, dw.shape[1] // N_DEV, dw.shape[2])


def _ffn_bwd(plan, dh_out, h, saved, norm_g, conv_w, conv_b, tag):
    n, up, act = saved
    plan.grads[f"dn{tag}"] = _by_rows(plan.run(f"ffn_down_dw{tag}", mm_tn, act, dh_out, 1, name=f"ffn_down_dw{tag}"))
    dact = plan.run(f"ffn_down_dx{tag}", mm_nt, dh_out, plan.weight(f"dn{tag}"), out_dtype=BF16,
                    name=f"ffn_down_dx{tag}")
    dup, dconv_w, dconv_b = plan.run(f"convglu_bwd{tag}", convglu_bwd, up, conv_w, conv_b, dact,
                                     name=f"convglu_bwd{tag}")
    plan.grads[f"up{tag}"] = mm_tn(n, dup, N_DEV, name=f"ffn_up_dw{tag}")
    dh, (dnorm,) = plan.run(f"ffn_up_dx{tag}", mm_nt, dup, plan.weight(f"up{tag}"), out_dtype=F32,
                            name=f"ffn_up_dx{tag}", norm=(h, dh_out, [norm_g], []))
    return dh, dnorm, dconv_w, dconv_b


def local_step(x, target, wts, plan):
    row = lambda v: v.reshape(1, -1)
    attn_norm, ffn_norm = wts["attn_norm"], wts["ffn_norm"]
    conv_w, conv_b = wts["ffn_conv_w"], wts["ffn_conv_b"]

    (n1,) = rms_fwd(x, [row(attn_norm[0])], name="attn_norm_fwd0")
    proj = plan.run("gla_in", mm_nn, n1, wts["gla_w_in"], out_dtype=F32, name="gla_in")
    la = gate_fwd(proj, wts["gla_w_a2"], wts["gla_b_a2"])
    o_gla, states, og = plan.run("gla_fwd", gla_fwd, proj, la, wts["gla_head_norm"])
    h1 = plan.run("gla_out", mm_nn, og, plan.weight("gout"), out_dtype=F32, res=x, name="gla_out")
    h2, ffn0 = _ffn_fwd(plan, h1, row(ffn_norm[0]), conv_w[0], row(conv_b[0]), "0")

    h2s = to_streams(h2, name="h2_to_streams")
    kvn, n3 = rms_fwd(h2s, [row(wts["kv_norm"]), row(attn_norm[1])], name="kv_attn_norm_fwd")
    kv = plan.run("kv_proj", mm_nn, kvn, plan.weight("kv"), out_dtype=BF16, name="kv_proj")
    q = plan.run("q_proj", mm_nn, n3, plan.weight("q"), out_dtype=BF16, name="q_proj")
    o_att, lse = plan.run("attn_fwd", attn_fwd, q, kv)
    h3 = from_streams(plan.run("dsa_out", mm_nn, o_att, plan.weight("dout"), out_dtype=F32, res=h2s, name="dsa_out"),
                      name="h3_from_streams")
    h4, ffn1 = _ffn_fwd(plan, h3, row(ffn_norm[1]), conv_w[1], row(conv_b[1]), "1")

    loss_tile, dh4, d_final = loss_head(h4, row(wts["final_norm"]), target)

    dh3, d_ffn1, dcw1, dcb1 = _ffn_bwd(plan, dh4, h3, ffn1, row(ffn_norm[1]), conv_w[1], row(conv_b[1]), "1")
    dh3s = to_streams(dh3, name="dh3_to_streams")
    plan.grads["dout"] = _by_rows(mm_tn(o_att, dh3s, 1, name="dsa_out_dw"))
    do_att = mm_nt(dh3s, plan.weight("dout"), out_dtype=F32, name="dsa_out_dx")
    dq, dkv = plan.run("attn_bwd", attn_bwd, q, kv, o_att, lse, do_att)
    plan.grads["q"] = mm_tn(n3, dq, N_DEV, name="q_proj_dw")
    dh2_part, (d_attn1,) = plan.run("q_proj_dx", mm_nt, dq, plan.weight("q"), out_dtype=F32, name="q_proj_dx",
                                    norm=(h2s, dh3s, [row(attn_norm[1])], []))
    plan.grads["kv"] = plan.run("kv_proj_dw", mm_tn, kvn, dkv, N_DEV, name="kv_proj_dw")
    dh2s, (d_kvnorm,) = plan.run("kv_proj_dx", mm_nt, dkv, plan.weight("kv"), out_dtype=F32, name="kv_proj_dx",
                                 norm=(h2s, dh2_part, [row(wts["kv_norm"])], []))
    dh2 = from_streams(dh2s, name="dh2_from_streams")
    dh1, d_ffn0, dcw0, dcb0 = _ffn_bwd(plan, dh2, h1, ffn0, row(ffn_norm[0]), conv_w[0], row(conv_b[0]), "0")
    plan.grads["gout"] = _by_rows(plan.run("gla_out_dw", mm_tn, og, dh1, 1, name="gla_out_dw"))
    dog = plan.run("gla_out_dx", mm_nt, dh1, plan.weight("gout"), out_dtype=F32, name="gla_out_dx")
    dq_g, dk_g, dv_g, dr, dla, d_hn = plan.run("gla_bwd", gla_bwd, proj, la, states, o_gla, wts["gla_head_norm"], dog)
    da, dw_a2p, db_a2 = gate_bwd(proj, wts["gla_w_a2"], wts["gla_b_a2"], dla)
    dproj = jnp.concatenate([dq_g, dk_g, dv_g, dr, da], axis=1)
    assert dproj.shape[1] == GLA_IN_PAD
    dw_in = plan.run("gla_in_dw", mm_tn, n1, dproj, 1, name="gla_in_dw")
    plan.grads["in"] = dw_in[0, :, :GLA_IN_DIM].reshape(D_MODEL, N_DEV, GLA_IN_DIM // N_DEV).transpose(1, 0, 2)
    grad_x, (d_attn0,) = plan.run("gla_in_dx", mm_nt, dproj, wts["gla_w_in"], out_dtype=F32, name="gla_in_dx",
                                  norm=(x, dh1, [row(attn_norm[0])], []))

    small = dict(
        attn_norm=jnp.concatenate([d_attn0, d_attn1], axis=0),
        ffn_norm=jnp.concatenate([d_ffn0, d_ffn1], axis=0),
        kv_norm=d_kvnorm.reshape(-1),
        final_norm=d_final.reshape(-1),
        ffn_conv_b=jnp.concatenate([dcb0, dcb1], axis=0),
        gla_w_a2=dw_a2p[:GATE_RANK],
        gla_b_a2=db_a2,
        gla_head_norm=d_hn,
        ffn_conv_w=jnp.stack([dcw0, dcw1]),
    )
    return loss_tile, grad_x, small


SMALL_ORDER = ("attn_norm", "ffn_norm", "kv_norm", "final_norm", "ffn_conv_b",
               "gla_w_a2", "gla_b_a2", "gla_head_norm", "ffn_conv_w")
SMALL_FULL = dict(attn_norm=(2, D_MODEL), ffn_norm=(2, D_MODEL), kv_norm=(D_MODEL,), final_norm=(D_MODEL,),
                  ffn_conv_b=(2, D_FF), gla_w_a2=(GATE_RANK, GLA_KEY_DIM), gla_b_a2=(1, GLA_KEY_DIM),
                  gla_head_norm=(1, GLA_DV), ffn_conv_w=(2, 3, D_FF))
SMALL_SHARDED = ("gla_w_a2", "gla_b_a2", "gla_head_norm", "ffn_conv_w")
SMALL_GRAD_ROWS = 592
SMALL_ADAM_ROWS = 240


def kernel(x, attn_norm, gla_w_in, gla_w_a2, gla_b_a2, gla_head_norm, gla_w_out, kv_norm, w_kv, dsa_w_q, dsa_w_out, ffn_norm, ffn_w_up, ffn_conv_w, ffn_conv_b, ffn_w_down, final_norm, loss_target, m_attn_norm, m_gla_w_in, m_gla_w_a2, m_gla_b_a2, m_gla_head_norm, m_gla_w_out, m_kv_norm, m_w_kv, m_dsa_w_q, m_dsa_w_out, m_ffn_norm, m_ffn_w_up, m_ffn_conv_w, m_ffn_conv_b, m_ffn_w_down, m_final_norm, v_attn_norm, v_gla_w_in, v_gla_w_a2, v_gla_b_a2, v_gla_head_norm, v_gla_w_out, v_kv_norm, v_w_kv, v_dsa_w_q, v_dsa_w_out, v_ffn_norm, v_ffn_w_up, v_ffn_conv_w, v_ffn_conv_b, v_ffn_w_down, v_final_norm):
    me = 4 * lax.axis_index("x") + 2 * lax.axis_index("y") + lax.axis_index("c")
    bf = lambda a: a.astype(BF16)

    g_in, g_small = all_gather([bf(gla_w_in[0]), _pack_small_weights(gla_w_a2, gla_b_a2, gla_head_norm, ffn_conv_w)],
                               name="gather_first")
    w_a2_full, b_a2_full, hn_full, conv_w_full = _unpack_small_weights(g_small)
    w_in_full = jnp.pad(g_in.transpose(1, 0, 2).reshape(D_MODEL, GLA_IN_DIM), ((0, 0), (0, GLA_IN_PAD - GLA_IN_DIM)))
    wts = dict(
        attn_norm=attn_norm, ffn_norm=ffn_norm, kv_norm=kv_norm, final_norm=final_norm, ffn_conv_b=ffn_conv_b,
        gla_w_in=w_in_full[None],
        gla_w_a2=jnp.pad(bf(w_a2_full), ((0, LANE - GATE_RANK), (0, 0))),
        gla_b_a2=b_a2_full, gla_head_norm=hn_full, ffn_conv_w=conv_w_full,
    )
    plan = Plan({}, srcs=dict(gout=bf(gla_w_out[0]), kv=bf(w_kv), q=bf(dsa_w_q[0]), dout=bf(dsa_w_out[0]),
                              up0=bf(ffn_w_up[0]), up1=bf(ffn_w_up[1]), dn0=bf(ffn_w_down[0]), dn1=bf(ffn_w_down[1])))

    loss_tile, grad_x, small = local_step(x[0], loss_target[0], wts, plan)
    loss = lax.psum(loss_tile[0, 0], ("x", "y", "c"))

    plan.grads["small"] = _pack_rows([small[nm] for nm in SMALL_ORDER], SMALL_GRAD_ROWS)
    plan.run("grads_tail", exchange_only, name="grads_tail")
    shard3 = lambda a: a.reshape((-1,) + a.shape[-2:])
    big_params = dict(gla_w_in=(("in",), gla_w_in, m_gla_w_in, v_gla_w_in),
                      gla_w_out=(("gout",), gla_w_out, m_gla_w_out, v_gla_w_out),
                      w_kv=(("kv",), w_kv, m_w_kv, v_w_kv),
                      dsa_w_q=(("q",), dsa_w_q, m_dsa_w_q, v_dsa_w_q),
                      dsa_w_out=(("dout",), dsa_w_out, m_dsa_w_out, v_dsa_w_out),
                      ffn_w_up=(("up0", "up1"), ffn_w_up, m_ffn_w_up, v_ffn_w_up),
                      ffn_w_down=(("dn0", "dn1"), ffn_w_down, m_ffn_w_down, v_ffn_w_down))
    res = {}
    for nm, (parts, w, m, v) in big_params.items():
        outs = adam_sharded([plan.recv[p] for p in parts], shard3(w), shard3(m), shard3(v), name=f"adam_{nm}")
        res[nm] = [o.reshape(w.shape) for o in outs]

    full = dict(zip(SMALL_ORDER, _unpack_rows(sum_partials(plan.recv["small"]),
                                              [SMALL_FULL[nm] for nm in SMALL_ORDER])))
    local_w = dict(attn_norm=attn_norm, ffn_norm=ffn_norm, kv_norm=kv_norm, final_norm=final_norm,
                   ffn_conv_b=ffn_conv_b, gla_w_a2=gla_w_a2, gla_b_a2=gla_b_a2, gla_head_norm=gla_head_norm,
                   ffn_conv_w=ffn_conv_w)
    local_m = dict(attn_norm=m_attn_norm, ffn_norm=m_ffn_norm, kv_norm=m_kv_norm, final_norm=m_final_norm,
                   ffn_conv_b=m_ffn_conv_b, gla_w_a2=m_gla_w_a2, gla_b_a2=m_gla_b_a2, gla_head_norm=m_gla_head_norm,
                   ffn_conv_w=m_ffn_conv_w)
    local_v = dict(attn_norm=v_attn_norm, ffn_norm=v_ffn_norm, kv_norm=v_kv_norm, final_norm=v_final_norm,
                   ffn_conv_b=v_ffn_conv_b, gla_w_a2=v_gla_w_a2, gla_b_a2=v_gla_b_a2, gla_head_norm=v_gla_head_norm,
                   ffn_conv_w=v_ffn_conv_w)
    local_g = {}
    for nm in SMALL_ORDER:
        gfull = full[nm]
        if nm in SMALL_SHARDED:
            per = gfull.shape[-1] // N_DEV
            gfull = lax.dynamic_slice_in_dim(gfull, me * per, per, axis=gfull.ndim - 1)
        local_g[nm] = gfull.reshape(local_w[nm].shape)
    shapes = [local_w[nm].shape for nm in SMALL_ORDER]
    pk = lambda dd: _pack_rows([dd[nm] for nm in SMALL_ORDER], SMALL_ADAM_ROWS)
    d_p, m_p, v_p = adam_packed(pk(local_w), pk(local_g), pk(local_m), pk(local_v))
    for nm, dl, mn, vn in zip(SMALL_ORDER, _unpack_rows(d_p, shapes), _unpack_rows(m_p, shapes),
                              _unpack_rows(v_p, shapes)):
        res[nm] = [local_g[nm], dl, mn, vn]

    order = ("attn_norm", "gla_w_in", "gla_w_a2", "gla_b_a2", "gla_head_norm", "gla_w_out", "kv_norm", "w_kv",
             "dsa_w_q", "dsa_w_out", "ffn_norm", "ffn_w_up", "ffn_conv_w", "ffn_conv_b", "ffn_w_down", "final_norm")
    outs = [loss, grad_x[None]]
    for kind in range(4):
        outs.extend(res[nm][kind] for nm in order)
    return tuple(outs)
```

```python
import functools

import jax
import jax.numpy as jnp
from jax import lax
from jax.experimental import pallas as pl
from jax.experimental.pallas import tpu as pltpu

F32 = jnp.float32
BF16 = jnp.bfloat16
MESH = pl.DeviceIdType.MESH
ANY = pl.BlockSpec(memory_space=pl.ANY)

N_DEV = 8
D_MODEL = 2048
GLA_HEADS = 4
GLA_KEY_DIM = 1024
GLA_VAL_DIM = 2048
GLA_DK = 256
GLA_DV = 512
GATE_RANK = 16
GATE_NORMALIZER = 16.0
GLA_CHUNK = 64
GLA_STEP_CHUNKS = 4
GLA_IN_DIM = 2 * GLA_KEY_DIM + 2 * GLA_VAL_DIM + GATE_RANK
GLA_IN_PAD = 6272
ATT_HEADS = 16
HEAD_DIM = 128
DILATIONS = (1, 4, 16)
STREAMS = DILATIONS[-1]
ATT_BLOCK = 128
D_FF = 5632
EPS = 1e-6
ADAM_LR = 0.001
ADAM_B1 = 0.9
ADAM_B2 = 0.999
ADAM_EPS = 1e-08
ADAM_WD = 0.01
ADAM_STEP = 10
NEG = -1e30
LANE = 128
NORM_ROWS = 64
VMEM_LIMIT = 52 * 1024 * 1024
ALIBI_SLOPES = tuple(2.0 ** (-0.5 * (i + 1)) for i in range(ATT_HEADS))


def _params(*sem):
    return pltpu.CompilerParams(dimension_semantics=sem, vmem_limit_bytes=VMEM_LIMIT)


def _tile(n, cap):
    best = None
    for t in range(LANE, min(n, cap) + 1, LANE):
        if n % t == 0:
            best = t
    return best if best is not None else n


def _shard_group(j, ns, cap):
    best = 1
    for g in range(1, j + 1):
        if j % g == 0 and g * ns <= cap:
            best = g
    return best


def _rows(r, c, budget=256 * 1024):
    best = None
    for t in range(16, r + 1, 16):
        if r % t == 0 and t * c <= budget:
            best = t
    return best if best is not None else r


def _flip(coord, bit):
    return 1 - coord if bit else coord


def _place():
    x, y, c = lax.axis_index("x"), lax.axis_index("y"), lax.axis_index("c")
    return x, y, c, 4 * x + 2 * y + c


def _rows_of(ref, rows):
    return ref if rows is None else ref.at[pl.ds(rows[0], rows[1] - rows[0])]


class Jobs:
    def __init__(self):
        self.srcs = []
        self.bufs = []
        self.sems = []
        self.steps = []

    def _src(self, a):
        for i, b in enumerate(self.srcs):
            if b is a:
                return i
        self.srcs.append(a)
        return len(self.srcs) - 1

    def new(self, shape, dtype):
        self.bufs.append((None, jax.ShapeDtypeStruct(shape, dtype)))
        return len(self.bufs) - 1

    def thru(self, a):
        self.bufs.append((a, jax.ShapeDtypeStruct(a.shape, a.dtype)))
        return len(self.bufs) - 1

    def _sem(self, n):
        self.sems.append(pltpu.SemaphoreType.DMA((n,)))
        return len(self.sems) - 1

    def gather_ici(self, src, buf, rows=None):
        si, send, recv, loc = self._src(src), self._sem(4), self._sem(4), self._sem(1)

        def remote(srcs, bufs, sems, slot_of):
            x, y, c, me = _place()
            peers = [(x, y, 1 - c), (1 - x, y, c), (x, 1 - y, c), (1 - x, 1 - y, c)]
            return [pltpu.make_async_remote_copy(
                src_ref=_rows_of(srcs[si], rows),
                dst_ref=_rows_of(bufs[buf].at[me if slot_of == "mine" else 4 * p[0] + 2 * p[1] + p[2]], rows),
                send_sem=sems[send].at[k], recv_sem=sems[recv].at[k], device_id=p, device_id_type=MESH)
                for k, p in enumerate(peers)]

        def local(srcs, bufs, sems):
            return pltpu.make_async_copy(_rows_of(srcs[si], rows), _rows_of(bufs[buf].at[_place()[3]], rows),
                                         sems[loc].at[0])

        def start(srcs, bufs, sems):
            local(srcs, bufs, sems).start()
            for cp in remote(srcs, bufs, sems, "mine"):
                cp.start()

        def finish(srcs, bufs, sems):
            for cp in remote(srcs, bufs, sems, "peer"):
                cp.wait_recv()
            for cp in remote(srcs, bufs, sems, "mine"):
                cp.wait_send()
            local(srcs, bufs, sems).wait()

        self.steps.append((start, finish))

    def gather_d2d(self, buf, rows=None):
        send, recv = self._sem(3), self._sem(3)

        def copies(bufs, sems, core):
            x, y, c, _ = _place()
            cc = c if core == "mine" else 1 - c
            chips = [(1 - x, y), (x, 1 - y), (1 - x, 1 - y)]
            return [pltpu.make_async_remote_copy(
                src_ref=_rows_of(bufs[buf].at[4 * px + 2 * py + cc], rows),
                dst_ref=_rows_of(bufs[buf].at[4 * px + 2 * py + cc], rows),
                send_sem=sems[send].at[k], recv_sem=sems[recv].at[k],
                device_id=(x, y, 1 - c), device_id_type=MESH) for k, (px, py) in enumerate(chips)]

        def start(srcs, bufs, sems):
            for cp in copies(bufs, sems, "mine"):
                cp.start()

        def finish(srcs, bufs, sems):
            for cp in copies(bufs, sems, "sibling"):
                cp.wait_recv()
            for cp in copies(bufs, sems, "mine"):
                cp.wait_send()

        self.steps.append((start, finish))

    def scatter(self, src, buf, rows=None, same=False):
        si, send, recv, loc = self._src(src), self._sem(N_DEV - 1), self._sem(N_DEV - 1), self._sem(1)

        def block(srcs, dev):
            return _rows_of(srcs[si] if same else srcs[si].at[dev], rows)

        def remote(srcs, bufs, sems, slot_of):
            x, y, c, me = _place()
            out = []
            for k in range(1, N_DEV):
                px, py, pc = _flip(x, k >> 2), _flip(y, (k >> 1) & 1), _flip(c, k & 1)
                peer = 4 * px + 2 * py + pc
                out.append(pltpu.make_async_remote_copy(
                    src_ref=block(srcs, peer),
                    dst_ref=_rows_of(bufs[buf].at[me if slot_of == "mine" else peer], rows),
                    send_sem=sems[send].at[k - 1], recv_sem=sems[recv].at[k - 1],
                    device_id=(px, py, pc), device_id_type=MESH))
            return out

        def local(srcs, bufs, sems):
            me = _place()[3]
            return pltpu.make_async_copy(block(srcs, me), _rows_of(bufs[buf].at[me], rows), sems[loc].at[0])

        def start(srcs, bufs, sems):
            local(srcs, bufs, sems).start()
            for cp in remote(srcs, bufs, sems, "mine"):
                cp.start()

        def finish(srcs, bufs, sems):
            for cp in remote(srcs, bufs, sems, "peer"):
                cp.wait_recv()
            for cp in remote(srcs, bufs, sems, "mine"):
                cp.wait_send()
            local(srcs, bufs, sems).wait()

        self.steps.append((start, finish))


def _call(body, *, name, grid, in_specs, out_specs, out_shape, args, sem, scratch_shapes=(), jobs=None):
    in_specs, out_specs, out_shape = list(in_specs), list(out_specs), list(out_shape)
    scratch_shapes = list(scratch_shapes)
    if jobs is None:
        res = pl.pallas_call(body, name=name, out_shape=out_shape, grid=grid, in_specs=in_specs,
                             out_specs=out_specs, scratch_shapes=scratch_shapes,
                             compiler_params=_params(*sem))(*args)
        return list(res), []
    thru = [a for a, _ in jobs.bufs if a is not None]
    n_in, n_src, n_thru = len(args), len(jobs.srcs), len(thru)
    n_out, n_buf, n_scr = len(out_shape), len(jobs.bufs), len(scratch_shapes)
    aliases, t = {}, 0
    for b, (a, _) in enumerate(jobs.bufs):
        if a is not None:
            aliases[n_in + n_src + t] = n_out + b
            t += 1

    def wrapped(*refs):
        at = 0
        ins = refs[at:at + n_in]; at += n_in
        srcs = refs[at:at + n_src]; at += n_src + n_thru
        outs = refs[at:at + n_out]; at += n_out
        bufs = refs[at:at + n_buf]; at += n_buf
        scr = refs[at:at + n_scr]; at += n_scr
        sems = refs[at:]
        first, last = None, None
        for axis, size in enumerate(grid):
            pid = pl.program_id(axis)
            f, l = pid == 0, pid == size - 1
            first = f if first is None else first & f
            last = l if last is None else last & l

        @pl.when(first)
        def _():
            for start, _ in jobs.steps:
                start(srcs, bufs, sems)

        body(*ins, *outs, *scr)

        @pl.when(last)
        def _():
            for _, finish in jobs.steps:
                finish(srcs, bufs, sems)

    res = pl.pallas_call(
        wrapped, name=name,
        out_shape=out_shape + [s for _, s in jobs.bufs],
        grid=grid,
        in_specs=in_specs + [ANY] * (n_src + n_thru),
        out_specs=out_specs + [ANY] * n_buf,
        scratch_shapes=scratch_shapes + jobs.sems,
        input_output_aliases=aliases,
        compiler_params=_params(*(["arbitrary"] * len(grid))),
    )(*args, *jobs.srcs, *thru)
    return res[:n_out], res[n_out:]


def mm_nn(a, w, *, out_dtype, name, res=None, tm=None, jobs=None):
    m, k = a.shape
    j, k2, ns = w.shape
    whole = j == 1 and ns <= 2048 and k <= 2048
    tm = tm or (1024 if a.dtype == BF16 and not whole else 512)
    assert k == k2 and m % tm == 0
    tn = ns if whole else _tile(ns, 1408)
    nsub = ns // tn
    tk = k if k <= 2048 else _tile(k, 1408)
    nk = k // tk
    has_res = res is not None

    def body(*refs):
        if has_res:
            a_ref, w_ref, r_ref, o_ref, acc = refs
        else:
            a_ref, w_ref, o_ref, acc = refs
        kk = pl.program_id(2)

        @pl.when(kk == 0)
        def _():
            acc[...] = jnp.zeros_like(acc)

        acc[...] += jnp.dot(a_ref[...].astype(BF16), w_ref[...], preferred_element_type=F32)

        @pl.when(kk == nk - 1)
        def _():
            r = acc[...]
            if has_res:
                r = r + r_ref[...]
            o_ref[...] = r.astype(out_dtype)

    in_specs = [
        pl.BlockSpec((tm, tk), lambda i, n, kk: (i, kk)),
        pl.BlockSpec((None, tk, tn), lambda i, n, kk: (n // nsub, kk, n % nsub)),
    ]
    args = [a, w]
    out_tile = pl.BlockSpec((tm, tn), lambda i, n, kk: (i, n))
    if has_res:
        in_specs.append(out_tile)
        args.append(res)
    (out,), bufs = _call(
        body, name=name, jobs=jobs,
        out_shape=[jax.ShapeDtypeStruct((m, j * ns), out_dtype)],
        grid=(m // tm, j * nsub, nk),
        in_specs=in_specs,
        out_specs=[out_tile],
        scratch_shapes=[pltpu.VMEM((tm, tn), F32)],
        args=args, sem=("parallel", "parallel", "arbitrary"))
    return out if jobs is None else (out, bufs)


def mm_nt(dy, w, *, out_dtype, name, tm=None, jobs=None, norm=None):
    parts, m, n = (1,) + dy.shape if dy.ndim == 2 else dy.shape
    n *= parts
    j, k, ns = w.shape
    if norm is not None:
        x, dres, gains, more = norm
        tm = tm or (256 if more else 512)
    tm = tm or 1024
    assert n == j * ns and m % tm == 0
    tn = _tile(ns, 2048)
    nsub = ns // tn
    jb = _shard_group(j // parts, ns, 2048 if norm is None else 1024) if nsub == 1 else 1
    tko = _tile(k, 1408) if norm is None else k
    nn = j * nsub // jb
    per_part = nn // parts
    if dy.ndim == 2:
        dy_spec = pl.BlockSpec((tm, jb * tn), lambda i, ko, nq: (i, nq))
    else:
        dy_spec = pl.BlockSpec((None, tm, jb * tn), lambda i, ko, nq: (nq // per_part, i, nq % per_part))
    if jb == 1:
        w_spec = pl.BlockSpec((None, tko, tn), lambda i, ko, nq: (nq // nsub, ko, nq % nsub))
    else:
        w_spec = pl.BlockSpec((jb, tko, ns), lambda i, ko, nq: (nq, ko, 0))

    n_gain = 0 if norm is None else len(gains)
    n_more = 0 if norm is None else len(more)

    def body(*refs):
        a_ref, w_ref = refs[:2]
        acc = refs[-1]
        nq = pl.program_id(2)
        first = pl.program_id(0) == 0

        @pl.when(nq == 0)
        def _():
            acc[...] = jnp.zeros_like(acc)

        if jb == 1:
            acc[...] += lax.dot_general(a_ref[...].astype(BF16), w_ref[...], (((1,), (1,)), ((), ())),
                                        preferred_element_type=F32)
        else:
            part = acc[...]
            for jj in range(jb):
                part = part + lax.dot_general(a_ref[:, jj * ns:(jj + 1) * ns].astype(BF16), w_ref[jj],
                                              (((1,), (1,)), ((), ())), preferred_element_type=F32)
            acc[...] = part

        @pl.when(nq == nn - 1)
        def _():
            if norm is None:
                refs[2][...] = acc[...].astype(out_dtype)
                return
            x_ref, r_ref = refs[2:4]
            g_refs = refs[4:4 + n_gain]
            e_refs = refs[4 + n_gain:4 + n_gain + n_more]
            dx_ref = refs[4 + n_gain + n_more]
            dg_refs = refs[5 + n_gain + n_more:-1]

            @pl.when(first)
            def _():
                for dg_ref in dg_refs:
                    dg_ref[...] = jnp.zeros_like(dg_ref)

            def rows(c, carry):
                sl = pl.ds(pl.multiple_of(c * NORM_ROWS, NORM_ROWS), NORM_ROWS)
                xv = x_ref[sl, :]
                r = lax.rsqrt(jnp.mean(xv * xv, axis=-1, keepdims=True) + EPS)
                xh = xv * r
                out = r_ref[sl, :]
                for idx, (g_ref, dg_ref) in enumerate(zip(g_refs, dg_refs)):
                    dyv = acc[sl, :] if idx == 0 else e_refs[idx - 1][sl, :].astype(F32)
                    dg_ref[...] += jnp.sum(dyv * xh, axis=0, keepdims=True)
                    dxh = dyv * g_ref[...]
                    out = out + r * (dxh - xh * jnp.mean(dxh * xh, axis=-1, keepdims=True))
                dx_ref[sl, :] = out
                return carry

            lax.fori_loop(0, tm // NORM_ROWS, rows, 0)

    out_tile = pl.BlockSpec((tm, tko), lambda i, ko, nq: (i, ko))
    in_specs, args = [dy_spec, w_spec], [dy, w]
    out_shape, out_specs = [jax.ShapeDtypeStruct((m, k), out_dtype)], [out_tile]
    sem = ("parallel", "parallel", "arbitrary")
    if norm is not None:
        vec = pl.BlockSpec((1, k), lambda i, ko, nq: (0, 0))
        in_specs += [out_tile, out_tile] + [vec] * n_gain + [out_tile] * n_more
        args += [x, dres] + list(gains) + list(more)
        out_shape = [jax.ShapeDtypeStruct((m, k), F32)] + [jax.ShapeDtypeStruct((1, k), F32)] * n_gain
        out_specs = [out_tile] + [vec] * n_gain
        sem = ("arbitrary", "arbitrary", "arbitrary")
    outs, bufs = _call(
        body, name=name, jobs=jobs, out_shape=out_shape, grid=(m // tm, k // tko, nn),
        in_specs=in_specs, out_specs=out_specs, scratch_shapes=[pltpu.VMEM((tm, tko), F32)], args=args, sem=sem)
    out = outs[0] if norm is None else (outs[0], outs[1:])
    return out if jobs is None else (out, bufs)


def mm_tn(x, dy, j, *, name, tm=2048, jobs=None):
    m, k = x.shape
    parts, m2, n = (1,) + dy.shape if dy.ndim == 2 else dy.shape
    n *= parts
    assert m == m2 and n % j == 0 and m % tm == 0
    ns = n // j
    tn = _tile(ns, 1408)
    nsub = ns // tn
    jb = _shard_group(j // parts, ns, 1536) if nsub == 1 else 1
    tk = _tile(k, 1408)
    nm = m // tm
    n_steps = j * nsub // jb
    per_part = n_steps // parts
    if dy.ndim == 2:
        dy_spec = pl.BlockSpec((tm, jb * tn), lambda kq, nq, mi: (mi, nq))
    else:
        dy_spec = pl.BlockSpec((None, tm, jb * tn), lambda kq, nq, mi: (nq // per_part, mi, nq % per_part))
    if jb == 1:
        out_spec = pl.BlockSpec((None, tk, tn), lambda kq, nq, mi: (nq // nsub, kq, nq % nsub))
        acc_shape = (tk, tn)
    else:
        out_spec = pl.BlockSpec((jb, tk, ns), lambda kq, nq, mi: (nq, kq, 0))
        acc_shape = (jb, tk, ns)

    def body(x_ref, dy_ref, o_ref, acc):
        mi = pl.program_id(2)

        @pl.when(mi == 0)
        def _():
            acc[...] = jnp.zeros_like(acc)

        xb = x_ref[...].astype(BF16)
        if jb == 1:
            acc[...] += lax.dot_general(xb, dy_ref[...].astype(BF16), (((0,), (0,)), ((), ())),
                                        preferred_element_type=F32)
        else:
            for jj in range(jb):
                acc[jj] += lax.dot_general(xb, dy_ref[:, jj * ns:(jj + 1) * ns].astype(BF16),
                                           (((0,), (0,)), ((), ())), preferred_element_type=F32)

        @pl.when(mi == nm - 1)
        def _():
            o_ref[...] = acc[...].astype(BF16)

    (out,), bufs = _call(
        body, name=name, jobs=jobs,
        out_shape=[jax.ShapeDtypeStruct((j, k, ns), BF16)],
        grid=(k // tk, n_steps, nm),
        in_specs=[
            pl.BlockSpec((tm, tk), lambda kq, nq, mi: (mi, kq)),
            dy_spec,
        ],
        out_specs=[out_spec],
        scratch_shapes=[pltpu.VMEM(acc_shape, F32)],
        args=[x, dy], sem=("parallel", "parallel", "arbitrary"))
    return out if jobs is None else (out, bufs)


STREAM_TC = LANE


def to_streams(x, *, name):
    s, c = x.shape
    per = s // STREAMS

    def body(x_ref, o_ref):
        for st in range(STREAMS):
            o_ref[pl.ds(st * per, per), :] = x_ref[pl.ds(st, per, stride=STREAMS), :]

    blk = pl.BlockSpec((s, STREAM_TC), lambda i: (0, i))
    return pl.pallas_call(body, name=name, out_shape=jax.ShapeDtypeStruct((s, c), x.dtype), grid=(c // STREAM_TC,),
                          in_specs=[blk], out_specs=blk, compiler_params=_params("parallel"))(x)


def from_streams(x, *, name):
    s, c = x.shape
    per = s // STREAMS

    def body(x_ref, o_ref):
        for st in range(STREAMS):
            o_ref[pl.ds(st, per, stride=STREAMS), :] = x_ref[pl.ds(st * per, per), :]

    blk = pl.BlockSpec((s, STREAM_TC), lambda i: (0, i))
    return pl.pallas_call(body, name=name, out_shape=jax.ShapeDtypeStruct((s, c), x.dtype), grid=(c // STREAM_TC,),
                          in_specs=[blk], out_specs=blk, compiler_params=_params("parallel"))(x)


def rms_fwd(x, gains, *, name, ts=512):
    s, d = x.shape
    n = len(gains)

    def body(x_ref, *refs):
        xv = x_ref[...]
        xh = xv * lax.rsqrt(jnp.mean(xv * xv, axis=-1, keepdims=True) + EPS)
        for g_ref, o_ref in zip(refs[:n], refs[n:]):
            o_ref[...] = (xh * g_ref[...]).astype(BF16)

    row = pl.BlockSpec((ts, d), lambda i: (i, 0))
    vec = pl.BlockSpec((1, d), lambda i: (0, 0))
    return pl.pallas_call(
        body,
        name=name,
        out_shape=[jax.ShapeDtypeStruct((s, d), BF16)] * n,
        grid=(s // ts,),
        in_specs=[row] + [vec] * n,
        out_specs=[row] * n,
        compiler_params=_params("parallel"),
    )(x, *gains)


def loss_head(h, gain, target, *, ts=256):
    s, d = h.shape

    def body(h_ref, g_ref, t_ref, l_ref, dh_ref, dg_ref):
        i = pl.program_id(0)

        @pl.when(i == 0)
        def _():
            l_ref[...] = jnp.zeros_like(l_ref)
            dg_ref[...] = jnp.zeros_like(dg_ref)

        xv = h_ref[...]
        r = lax.rsqrt(jnp.mean(xv * xv, axis=-1, keepdims=True) + EPS)
        xh = xv * r
        g = g_ref[...]
        err = xh * g - t_ref[...]
        l_ref[...] += 0.5 * jnp.sum(jnp.mean(err * err, axis=-1, keepdims=True))
        dy = err * (1.0 / d)
        dg_ref[...] += jnp.sum(dy * xh, axis=0, keepdims=True)
        dxh = dy * g
        dh_ref[...] = r * (dxh - xh * jnp.mean(dxh * xh, axis=-1, keepdims=True))

    row = pl.BlockSpec((ts, d), lambda i: (i, 0))
    vec = pl.BlockSpec((1, d), lambda i: (0, 0))
    return pl.pallas_call(
        body,
        name="loss_head",
        out_shape=[jax.ShapeDtypeStruct((8, LANE), F32), jax.ShapeDtypeStruct((s, d), F32),
                   jax.ShapeDtypeStruct((1, d), F32)],
        grid=(s // ts,),
        in_specs=[row, vec, row],
        out_specs=[pl.BlockSpec((8, LANE), lambda i: (0, 0)), row, vec],
        compiler_params=_params("arbitrary"),
    )(h, gain, target)


A_BLOCK = (2 * GLA_KEY_DIM + 2 * GLA_VAL_DIM) // LANE


def gate_fwd(proj, w_a2p, b_a2, *, ts=512):
    s = proj.shape[0]

    def body(a_ref, w_ref, b_ref, o_ref):
        z = jnp.dot(a_ref[...].astype(BF16), w_ref[...], preferred_element_type=F32) + b_ref[...]
        o_ref[...] = (jnp.minimum(z, 0.0) - jnp.log(1.0 + jnp.exp(-jnp.abs(z)))) * (1.0 / GATE_NORMALIZER)

    return pl.pallas_call(
        body,
        name="gate_fwd",
        out_shape=jax.ShapeDtypeStruct((s, GLA_KEY_DIM), F32),
        grid=(s // ts,),
        in_specs=[pl.BlockSpec((ts, LANE), lambda i: (i, A_BLOCK)),
                  pl.BlockSpec((LANE, GLA_KEY_DIM), lambda i: (0, 0)),
                  pl.BlockSpec((1, GLA_KEY_DIM), lambda i: (0, 0))],
        out_specs=pl.BlockSpec((ts, GLA_KEY_DIM), lambda i: (i, 0)),
        compiler_params=_params("parallel"),
    )(proj, w_a2p, b_a2)


def gate_bwd(proj, w_a2p, b_a2, dla, *, ts=512):
    s = proj.shape[0]

    def body(a_ref, w_ref, b_ref, dla_ref, da_ref, dw_ref, db_ref):
        i = pl.program_id(0)

        @pl.when(i == 0)
        def _():
            dw_ref[...] = jnp.zeros_like(dw_ref)
            db_ref[...] = jnp.zeros_like(db_ref)

        a = a_ref[...].astype(BF16)
        w = w_ref[...]
        z = jnp.dot(a, w, preferred_element_type=F32) + b_ref[...]
        dz = dla_ref[...] * (1.0 / GATE_NORMALIZER) / (1.0 + jnp.exp(z))
        dzb = dz.astype(BF16)
        da_ref[...] = lax.dot_general(dzb, w, (((1,), (1,)), ((), ())), preferred_element_type=F32).astype(BF16)
        dw_ref[...] += lax.dot_general(a, dzb, (((0,), (0,)), ((), ())), preferred_element_type=F32)
        db_ref[...] += jnp.sum(dz, axis=0, keepdims=True)

    return pl.pallas_call(
        body,
        name="gate_bwd",
        out_shape=[jax.ShapeDtypeStruct((s, LANE), BF16), jax.ShapeDtypeStruct((LANE, GLA_KEY_DIM), F32),
                   jax.ShapeDtypeStruct((1, GLA_KEY_DIM), F32)],
        grid=(s // ts,),
        in_specs=[pl.BlockSpec((ts, LANE), lambda i: (i, A_BLOCK)),
                  pl.BlockSpec((LANE, GLA_KEY_DIM), lambda i: (0, 0)),
                  pl.BlockSpec((1, GLA_KEY_DIM), lambda i: (0, 0)),
                  pl.BlockSpec((ts, GLA_KEY_DIM), lambda i: (i, 0))],
        out_specs=[pl.BlockSpec((ts, LANE), lambda i: (i, 0)),
                   pl.BlockSpec((LANE, GLA_KEY_DIM), lambda i: (0, 0)),
                   pl.BlockSpec((1, GLA_KEY_DIM), lambda i: (0, 0))],
        compiler_params=_params("arbitrary"),
    )(proj, w_a2p, b_a2, dla)


def _masked_sum(mask, x):
    m = mask.astype(BF16)
    hi = x.astype(BF16)
    rest = x - hi.astype(F32)
    mid = rest.astype(BF16)
    lo = (rest - mid.astype(F32)).astype(BF16)
    dot = lambda t: jnp.dot(m, t, preferred_element_type=F32)
    return dot(hi) + dot(mid) + dot(lo)


def _chunk_terms(q, k, la):
    c_len = GLA_CHUNK
    row = lax.broadcasted_iota(jnp.int32, (c_len, c_len), 0)
    col = lax.broadcasted_iota(jnp.int32, (c_len, c_len), 1)
    tri = row >= col
    c = _masked_sum(tri, la)
    last = jnp.sum(la, axis=0, keepdims=True)
    q_dec = q * (GLA_DK ** -0.5) * jnp.exp(c)
    k_inv = k * jnp.exp(-c)
    k_end = k * jnp.exp(last - c)
    return c, last, q_dec, k_inv, k_end, tri


def _dot(a, b, ca, cb):
    return lax.dot_general(a.astype(BF16), b.astype(BF16), (((ca,), (cb,)), ((), ())), preferred_element_type=F32)


def gla_fwd(proj, la, hn, jobs=None):
    s = proj.shape[0]
    n_chunks = s // GLA_CHUNK
    rows = GLA_CHUNK * GLA_STEP_CHUNKS

    def body(q_ref, k_ref, v_ref, r_ref, la_ref, hn_ref, o_ref, st_out, og_ref, st):
        @pl.when(pl.program_id(0) == 0)
        def _():
            st[...] = jnp.zeros_like(st)

        for h in range(GLA_HEADS):
            hk = slice(h * GLA_DK, (h + 1) * GLA_DK)
            hv = slice(h * GLA_DV, (h + 1) * GLA_DV)
            for cc in range(GLA_STEP_CHUNKS):
                rs = slice(cc * GLA_CHUNK, (cc + 1) * GLA_CHUNK)
                _, last, q_dec, k_inv, k_end, tri = _chunk_terms(q_ref[rs, hk], k_ref[rs, hk], la_ref[rs, hk])
                v = v_ref[rs, hv]
                a = jnp.where(tri, _dot(q_dec, k_inv, 1, 1), 0.0)
                state = st[h]
                st_out[h, cc] = state
                ov = _dot(a, v, 1, 0) + _dot(q_dec, state, 1, 1)
                o_ref[rs, hv] = ov
                st[h] = state * jnp.exp(last) + _dot(v, k_end, 0, 0)
                oh = ov * lax.rsqrt(jnp.mean(ov * ov, axis=-1, keepdims=True) + EPS)
                r = r_ref[rs, hv]
                og_ref[rs, hv] = (oh * hn_ref[...] * (r * jax.nn.sigmoid(r))).astype(BF16)

    key = lambda col: pl.BlockSpec((rows, GLA_KEY_DIM), lambda n: (n, col))
    val = lambda col: pl.BlockSpec((rows, GLA_VAL_DIM), lambda n: (n, col))
    outs, bufs = _call(
        body, name="gla_fwd", jobs=jobs,
        out_shape=[jax.ShapeDtypeStruct((s, GLA_VAL_DIM), F32),
                   jax.ShapeDtypeStruct((GLA_HEADS, n_chunks, GLA_DV, GLA_DK), F32),
                   jax.ShapeDtypeStruct((s, GLA_VAL_DIM), BF16)],
        grid=(n_chunks // GLA_STEP_CHUNKS,),
        in_specs=[key(0), key(1), val(1), val(R_BLOCK // GLA_HEADS), key(0), pl.BlockSpec((1, GLA_DV), lambda n: (0, 0))],
        out_specs=[val(0), pl.BlockSpec((GLA_HEADS, GLA_STEP_CHUNKS, GLA_DV, GLA_DK), lambda n: (0, n, 0, 0)), val(0)],
        scratch_shapes=[pltpu.VMEM((GLA_HEADS, GLA_DV, GLA_DK), F32)],
        args=[proj, proj, proj, proj, la, hn], sem=("arbitrary",))
    return outs if jobs is None else (outs, bufs)


def gla_bwd(proj, la, states, o, hn, dog, jobs=None):
    s = proj.shape[0]
    n_steps = s // GLA_CHUNK // GLA_STEP_CHUNKS
    lastc = n_steps - 1
    rows = GLA_CHUNK * GLA_STEP_CHUNKS

    def body(q_ref, k_ref, v_ref, r_ref, la_ref, o_ref, hn_ref, dog_ref, st_ref,
             dq_ref, dk_ref, dv_ref, dr_ref, dla_ref, dhn_ref, dst):
        @pl.when(pl.program_id(0) == 0)
        def _():
            dst[...] = jnp.zeros_like(dst)
            dhn_ref[...] = jnp.zeros_like(dhn_ref)

        upper = (lax.broadcasted_iota(jnp.int32, (GLA_CHUNK, GLA_CHUNK), 0)
                 <= lax.broadcasted_iota(jnp.int32, (GLA_CHUNK, GLA_CHUNK), 1))
        gain = hn_ref[...]
        for h in range(GLA_HEADS):
            hk = slice(h * GLA_DK, (h + 1) * GLA_DK)
            hv = slice(h * GLA_DV, (h + 1) * GLA_DV)
            for cc in reversed(range(GLA_STEP_CHUNKS)):
                rs = slice(cc * GLA_CHUNK, (cc + 1) * GLA_CHUNK)
                ov = o_ref[rs, hv]
                inv = lax.rsqrt(jnp.mean(ov * ov, axis=-1, keepdims=True) + EPS)
                oh = ov * inv
                r = r_ref[rs, hv]
                sig = jax.nn.sigmoid(r)
                dgv = dog_ref[rs, hv]
                d_on = dgv * (r * sig)
                dr_ref[rs, hv] = (dgv * (oh * gain) * (sig * (1.0 + r * (1.0 - sig)))).astype(BF16)
                dhn_ref[...] += jnp.sum(d_on * oh, axis=0, keepdims=True)
                doh = d_on * gain
                dout = inv * (doh - oh * jnp.mean(doh * oh, axis=-1, keepdims=True))
                c, last, q_dec, k_inv, k_end, tri = _chunk_terms(q_ref[rs, hk], k_ref[rs, hk], la_ref[rs, hk])
                v = v_ref[rs, hv]
                state = st_ref[h, cc]
                dstate = dst[h]
                e_last = jnp.exp(last)
                a = jnp.where(tri, _dot(q_dec, k_inv, 1, 1), 0.0)
                da = jnp.where(tri, _dot(dout, v, 1, 1), 0.0)
                dv_ref[rs, hv] = (_dot(a, dout, 0, 0) + _dot(k_end, dstate, 1, 1)).astype(BF16)
                dq_dec = _dot(da, k_inv, 1, 0) + _dot(dout, state, 1, 0)
                dk_inv = _dot(da, q_dec, 0, 0)
                dk_end = _dot(v, dstate, 1, 0)
                dst[h] = dstate * e_last + _dot(dout, q_dec, 0, 0)
                dq_ref[rs, hk] = (dq_dec * (GLA_DK ** -0.5) * jnp.exp(c)).astype(BF16)
                dk_ref[rs, hk] = (dk_inv * jnp.exp(-c) + dk_end * jnp.exp(last - c)).astype(BF16)
                ke_term = dk_end * k_end
                dc = dq_dec * q_dec - dk_inv * k_inv - ke_term
                dlast = (jnp.sum(ke_term, axis=0, keepdims=True)
                         + e_last * jnp.sum(dstate * state, axis=0, keepdims=True))
                dla_ref[rs, hk] = _masked_sum(upper, dc) + dlast

    key = lambda col: pl.BlockSpec((rows, GLA_KEY_DIM), lambda n: (lastc - n, col))
    val = lambda col: pl.BlockSpec((rows, GLA_VAL_DIM), lambda n: (lastc - n, col))
    vec = pl.BlockSpec((1, GLA_DV), lambda n: (0, 0))
    outs, bufs = _call(
        body, name="gla_bwd", jobs=jobs,
        out_shape=[jax.ShapeDtypeStruct((s, GLA_KEY_DIM), BF16), jax.ShapeDtypeStruct((s, GLA_KEY_DIM), BF16),
                   jax.ShapeDtypeStruct((s, GLA_VAL_DIM), BF16), jax.ShapeDtypeStruct((s, GLA_VAL_DIM), BF16),
                   jax.ShapeDtypeStruct((s, GLA_KEY_DIM), F32), jax.ShapeDtypeStruct((1, GLA_DV), F32)],
        grid=(n_steps,),
        in_specs=[key(0), key(1), val(1), val(R_BLOCK // GLA_HEADS), key(0), val(0), vec, val(0),
                  pl.BlockSpec((GLA_HEADS, GLA_STEP_CHUNKS, GLA_DV, GLA_DK), lambda n: (0, lastc - n, 0, 0))],
        out_specs=[key(0), key(0), val(0), val(0), key(0), vec],
        scratch_shapes=[pltpu.VMEM((GLA_HEADS, GLA_DV, GLA_DK), F32)],
        args=[proj, proj, proj, proj, la, o, hn, dog, states], sem=("arbitrary",))
    return outs if jobs is None else (outs, bufs)


R_BLOCK = (2 * GLA_KEY_DIM + GLA_VAL_DIM) // GLA_DV


CONV_TC = 128
SQRT_HALF = 0.7071067811865476
INV_SQRT_2PI = 0.3989422804014327


def _conv_gate(g_ref, cw_ref, cb_ref):
    g0 = g_ref[...].astype(F32)
    t = lax.broadcasted_iota(jnp.int32, g0.shape, 0)
    g1 = jnp.where(t >= 1, pltpu.roll(g0, 1, 0), 0.0)
    g2 = jnp.where(t >= 2, pltpu.roll(g0, 2, 0), 0.0)
    gc = cw_ref[0:1, :] * g2 + cw_ref[1:2, :] * g1 + cw_ref[2:3, :] * g0 + cb_ref[...]
    return g0, g1, g2, gc, t


def convglu_fwd(up, conv_w, conv_b, *, name, jobs=None):
    s = up.shape[0]
    nc = D_FF // CONV_TC

    def body(u_ref, g_ref, cw_ref, cb_ref, o_ref):
        _, _, _, gc, _ = _conv_gate(g_ref, cw_ref, cb_ref)
        gelu = 0.5 * gc * (1.0 + lax.erf(gc * SQRT_HALF))
        o_ref[...] = (gelu * u_ref[...].astype(F32)).astype(BF16)

    (out,), bufs = _call(
        body, name=name, jobs=jobs,
        out_shape=[jax.ShapeDtypeStruct((s, D_FF), BF16)],
        grid=(nc,),
        in_specs=[pl.BlockSpec((s, CONV_TC), lambda c: (0, c)),
                  pl.BlockSpec((s, CONV_TC), lambda c: (0, nc + c)),
                  pl.BlockSpec((3, CONV_TC), lambda c: (0, c)),
                  pl.BlockSpec((1, CONV_TC), lambda c: (0, c))],
        out_specs=[pl.BlockSpec((s, CONV_TC), lambda c: (0, c))],
        args=[up, up, conv_w, conv_b], sem=("parallel",))
    return out if jobs is None else (out, bufs)


def convglu_bwd(up, conv_w, conv_b, dact, *, name, jobs=None):
    s = up.shape[0]
    nc = D_FF // CONV_TC

    def body(u_ref, g_ref, cw_ref, cb_ref, da_ref, dup_ref, dcw_ref, dcb_ref):
        du_ref, dg_ref = dup_ref.at[0], dup_ref.at[1]
        g0, g1, g2, gc, t = _conv_gate(g_ref, cw_ref, cb_ref)
        cdf = 0.5 * (1.0 + lax.erf(gc * SQRT_HALF))
        da = da_ref[...].astype(F32)
        du_ref[...] = (da * gc * cdf).astype(BF16)
        dgc = da * u_ref[...].astype(F32) * (cdf + gc * jnp.exp(-0.5 * gc * gc) * INV_SQRT_2PI)
        dcb_ref[...] = jnp.sum(dgc, axis=0, keepdims=True)
        dcw_ref[0:1, :] = jnp.sum(dgc * g2, axis=0, keepdims=True)
        dcw_ref[1:2, :] = jnp.sum(dgc * g1, axis=0, keepdims=True)
        dcw_ref[2:3, :] = jnp.sum(dgc * g0, axis=0, keepdims=True)
        n1 = jnp.where(t < s - 1, pltpu.roll(dgc, s - 1, 0), 0.0)
        n2 = jnp.where(t < s - 2, pltpu.roll(dgc, s - 2, 0), 0.0)
        dg_ref[...] = (cw_ref[2:3, :] * dgc + cw_ref[1:2, :] * n1 + cw_ref[0:1, :] * n2).astype(BF16)

    col = pl.BlockSpec((s, CONV_TC), lambda c: (0, c))
    outs, bufs = _call(
        body, name=name, jobs=jobs,
        out_shape=[jax.ShapeDtypeStruct((2, s, D_FF), BF16),
                   jax.ShapeDtypeStruct((3, D_FF), F32), jax.ShapeDtypeStruct((1, D_FF), F32)],
        grid=(nc,),
        in_specs=[col, pl.BlockSpec((s, CONV_TC), lambda c: (0, nc + c)),
                  pl.BlockSpec((3, CONV_TC), lambda c: (0, c)),
                  pl.BlockSpec((1, CONV_TC), lambda c: (0, c)), col],
        out_specs=[pl.BlockSpec((2, s, CONV_TC), lambda c: (0, 0, c)), pl.BlockSpec((3, CONV_TC), lambda c: (0, c)),
                   pl.BlockSpec((1, CONV_TC), lambda c: (0, c))],
        args=[up, up, conv_w, conv_b, dact], sem=("parallel",))
    return outs if jobs is None else (outs, bufs)


SLOPE_TILE = (8, LANE)


def _slope_table():
    return jnp.broadcast_to(jnp.asarray(ALIBI_SLOPES, F32)[:, None, None], (ATT_HEADS,) + SLOPE_TILE)


def _pieces(s_len, d):
    npc = STREAMS // d
    lp = ATT_BLOCK // npc
    return npc, lp, (s_len // STREAMS) // lp


def _gather(ref, r, b, d, s_len):
    npc, lp, _ = _pieces(s_len, d)
    per = s_len // STREAMS
    parts = [ref[pl.ds((r + d * k) * per + b * lp, lp), :] for k in range(npc)]
    return parts[0] if npc == 1 else jnp.concatenate(parts, axis=0)


def _scatter(ref, r, b, d, s_len, val, add=False):
    npc, lp, _ = _pieces(s_len, d)
    per = s_len // STREAMS
    for k in range(npc):
        rows = pl.ds((r + d * k) * per + b * lp, lp)
        piece = val[k * lp:(k + 1) * lp]
        if add:
            ref[rows, :] += piece
        else:
            ref[rows, :] = piece


def _stream_bias(slope, d, s_len):
    npc, lp, _ = _pieces(s_len, d)
    qi = lax.broadcasted_iota(jnp.int32, (ATT_BLOCK, 2 * ATT_BLOCK), 0)
    c = lax.broadcasted_iota(jnp.int32, (ATT_BLOCK, 2 * ATT_BLOCK), 1)
    own = c // ATT_BLOCK
    cc = c - own * ATT_BLOCK
    dist = npc * ((qi % lp) - (cc % lp) + lp * (1 - own)) + (qi // lp - cc // lp)
    ok = (dist >= 0) & (dist <= ATT_BLOCK)
    return jnp.where(ok, (slope * (-float(d))) * dist.astype(F32), NEG)


def attn_fwd(q, kv, jobs=None):
    s_len = q.shape[0]
    scale = HEAD_DIM ** -0.5

    def body(sl_ref, q_ref, k_ref, v_ref, o_ref, lse_ref):
        g = pl.program_id(1)
        slope = sl_ref[0:1, 0:1]

        def branch(gi, d):
            _, _, nblk = _pieces(s_len, d)
            bias = _stream_bias(slope, d, s_len)
            for r in range(d):
                for b in range(nblk):
                    qb = _gather(q_ref, r, b, d, s_len)
                    kc, vc = _gather(k_ref, r, b, d, s_len), _gather(v_ref, r, b, d, s_len)
                    if b == 0:
                        kcat, vcat, bb = kc, vc, bias[:, ATT_BLOCK:]
                    else:
                        kcat = jnp.concatenate([_gather(k_ref, r, b - 1, d, s_len), kc], axis=0)
                        vcat = jnp.concatenate([_gather(v_ref, r, b - 1, d, s_len), vc], axis=0)
                        bb = bias
                    sc = _dot(qb, kcat, 1, 1) * scale + bb
                    m = jnp.max(sc, axis=-1, keepdims=True)
                    p = jnp.exp(sc - m)
                    l = jnp.sum(p, axis=-1, keepdims=True)
                    o_new = _dot(p, vcat, 1, 0) / l
                    lse_new = m + jnp.log(l)
                    if gi > 0:
                        lse_old = _gather(lse_ref, r, b, d, s_len)[:, 0:1]
                        top = jnp.maximum(lse_old, lse_new)
                        e_old, e_new = jnp.exp(lse_old - top), jnp.exp(lse_new - top)
                        den = e_old + e_new
                        o_new = (e_old * _gather(o_ref, r, b, d, s_len) + e_new * o_new) / den
                        lse_new = top + jnp.log(den)
                    _scatter(o_ref, r, b, d, s_len, o_new)
                    _scatter(lse_ref, r, b, d, s_len, jnp.broadcast_to(lse_new, (ATT_BLOCK, HEAD_DIM)))

        for gi, d in enumerate(DILATIONS):
            @pl.when(g == gi)
            def _():
                branch(gi, d)

    blk = lambda col: pl.BlockSpec((s_len, HEAD_DIM), lambda h, g: (0, col(h, g)))
    head = lambda h, g: h
    outs, bufs = _call(
        body, name="attn_fwd", jobs=jobs,
        out_shape=[jax.ShapeDtypeStruct((s_len, ATT_HEADS * HEAD_DIM), F32)] * 2,
        grid=(ATT_HEADS, len(DILATIONS)),
        in_specs=[pl.BlockSpec((None,) + SLOPE_TILE, lambda h, g: (h, 0, 0)),
                  blk(lambda h, g: g * ATT_HEADS + h), blk(head), blk(lambda h, g: ATT_HEADS + h)],
        out_specs=[blk(head), blk(head)],
        args=[_slope_table(), q, kv, kv], sem=("parallel", "arbitrary"))
    return outs if jobs is None else (outs, bufs)


def attn_bwd(q, kv, o, lse, do, jobs=None):
    s_len = q.shape[0]
    scale = HEAD_DIM ** -0.5
    chunks = s_len // ATT_BLOCK

    def body(sl_ref, q_ref, k_ref, v_ref, o_ref, lse_ref, do_ref, dq_ref, dkv_ref, dlt):
        g = pl.program_id(1)
        slope = sl_ref[0:1, 0:1]
        dk_ref, dv_ref = dkv_ref.at[0], dkv_ref.at[1]

        @pl.when(g == 0)
        def _():
            dkv_ref[...] = jnp.zeros_like(dkv_ref)

            def deltas(c, carry):
                rows = pl.ds(pl.multiple_of(c * ATT_BLOCK, ATT_BLOCK), ATT_BLOCK)
                dlt[rows, :] = jnp.sum(do_ref[rows, :] * o_ref[rows, :], axis=-1, keepdims=True)
                return carry
            lax.fori_loop(0, chunks, deltas, 0)

        def branch(d):
            _, _, nblk = _pieces(s_len, d)
            bias = _stream_bias(slope, d, s_len)
            for r in range(d):
                for b in range(nblk):
                    qb = _gather(q_ref, r, b, d, s_len)
                    dob = _gather(do_ref, r, b, d, s_len)
                    kc, vc = _gather(k_ref, r, b, d, s_len), _gather(v_ref, r, b, d, s_len)
                    if b == 0:
                        kcat, vcat, bb = kc, vc, bias[:, ATT_BLOCK:]
                    else:
                        kcat = jnp.concatenate([_gather(k_ref, r, b - 1, d, s_len), kc], axis=0)
                        vcat = jnp.concatenate([_gather(v_ref, r, b - 1, d, s_len), vc], axis=0)
                        bb = bias
                    sc = _dot(qb, kcat, 1, 1) * scale + bb
                    p = jnp.exp(sc - _gather(lse_ref, r, b, d, s_len)[:, 0:1])
                    ds = p * (_dot(dob, vcat, 1, 1) - _gather(dlt, r, b, d, s_len))
                    _scatter(dq_ref, r, b, d, s_len, _dot(ds, kcat, 1, 0) * scale)
                    dk = _dot(ds, qb, 0, 0) * scale
                    dv = _dot(p, dob, 0, 0)
                    if b == 0:
                        _scatter(dk_ref, r, b, d, s_len, dk, add=True)
                        _scatter(dv_ref, r, b, d, s_len, dv, add=True)
                    else:
                        _scatter(dk_ref, r, b - 1, d, s_len, dk[:ATT_BLOCK], add=True)
                        _scatter(dv_ref, r, b - 1, d, s_len, dv[:ATT_BLOCK], add=True)
                        _scatter(dk_ref, r, b, d, s_len, dk[ATT_BLOCK:], add=True)
                        _scatter(dv_ref, r, b, d, s_len, dv[ATT_BLOCK:], add=True)

        for gi, d in enumerate(DILATIONS):
            @pl.when(g == gi)
            def _():
                branch(d)

    blk = lambda col: pl.BlockSpec((s_len, HEAD_DIM), lambda h, g: (0, col(h, g)))
    head = lambda h, g: h
    q_col = lambda h, g: g * ATT_HEADS + h
    outs, bufs = _call(
        body, name="attn_bwd", jobs=jobs,
        out_shape=[jax.ShapeDtypeStruct(q.shape, F32), jax.ShapeDtypeStruct((2, s_len, ATT_HEADS * HEAD_DIM), F32)],
        grid=(ATT_HEADS, len(DILATIONS)),
        in_specs=[pl.BlockSpec((None,) + SLOPE_TILE, lambda h, g: (h, 0, 0)),
                  blk(q_col), blk(head), blk(lambda h, g: ATT_HEADS + h), blk(head), blk(head), blk(head)],
        out_specs=[blk(q_col), pl.BlockSpec((2, s_len, HEAD_DIM), lambda h, g: (0, 0, h))],
        scratch_shapes=[pltpu.VMEM((s_len, 1), F32)],
        args=[_slope_table(), q, kv, kv, o, lse, do], sem=("parallel", "arbitrary"))
    return outs if jobs is None else (outs, bufs)


def _adam(w, g, m, v):
    m = ADAM_B1 * m + (1.0 - ADAM_B1) * g
    v = ADAM_B2 * v + (1.0 - ADAM_B2) * (g * g)
    m_hat = m / (1.0 - ADAM_B1 ** ADAM_STEP)
    v_hat = v / (1.0 - ADAM_B2 ** ADAM_STEP)
    delta = -ADAM_LR * (m_hat / (jnp.sqrt(v_hat) + ADAM_EPS) + ADAM_WD * w)
    return delta, m, v


def adam_sharded(recvs, w, m, v, *, name):
    layers = len(recvs)
    n_src, r, c = recvs[0].shape
    tr = _rows(r, c)

    def body(*refs):
        p_refs = refs[:layers]
        w_ref, m_ref, v_ref, g_ref, d_ref, mo_ref, vo_ref = refs[layers:]
        for layer, p_ref in enumerate(p_refs):
            @pl.when(pl.program_id(0) == layer)
            def _():
                g = p_ref[0].astype(F32)
                for src in range(1, n_src):
                    g = g + p_ref[src].astype(F32)
                delta, m_new, v_new = _adam(w_ref[...], g, m_ref[...], v_ref[...])
                g_ref[...] = g
                d_ref[...] = delta
                mo_ref[...] = m_new
                vo_ref[...] = v_new

    blk = pl.BlockSpec((None, tr, c), lambda l, i: (l, i, 0))
    out = jax.ShapeDtypeStruct((layers, r, c), F32)
    part = [pl.BlockSpec((n_src, tr, c), functools.partial(lambda l, i, layer: (0, jnp.where(l == layer, i, 0), 0),
                                                            layer=layer)) for layer in range(layers)]
    return pl.pallas_call(
        body,
        name=name,
        out_shape=[out] * 4,
        grid=(layers, r // tr),
        in_specs=part + [blk, blk, blk],
        out_specs=[blk] * 4,
        compiler_params=_params("parallel", "parallel"),
    )(*recvs, w, m, v)


def sum_partials(parts):
    n_src, r, c = parts.shape

    def body(p_ref, o_ref):
        g = p_ref[0]
        for src in range(1, n_src):
            g = g + p_ref[src]
        o_ref[...] = g

    return pl.pallas_call(
        body,
        name="sum_small_grads",
        out_shape=jax.ShapeDtypeStruct((r, c), F32),
    )(parts)


def adam_packed(w, g, m, v):
    def body(w_ref, g_ref, m_ref, v_ref, d_ref, mo_ref, vo_ref):
        delta, m_new, v_new = _adam(w_ref[...], g_ref[...], m_ref[...], v_ref[...])
        d_ref[...] = delta
        mo_ref[...] = m_new
        vo_ref[...] = v_new

    out = jax.ShapeDtypeStruct(w.shape, F32)
    return pl.pallas_call(body, name="adam_small", out_shape=[out] * 3)(w, g, m, v)


def all_gather(srcs, *, name):
    n = len(srcs)

    def body(*refs):
        src, dst = refs[:n], refs[n:2 * n]
        send_sems, recv_sems, local_sems = refs[2 * n:]
        x, y, c, me = _place()
        sibling = (x, y, 1 - c)
        chips = [(1 - x, y), (x, 1 - y), (1 - x, 1 - y)]

        def index(px, py, pc):
            return 4 * px + 2 * py + pc

        def copy(p, k, block, to, from_src=False):
            slot = dst[p].at[index(*block)]
            return pltpu.make_async_remote_copy(
                src_ref=src[p] if from_src else slot, dst_ref=slot,
                send_sem=send_sems.at[p, k], recv_sem=recv_sems.at[p, k],
                device_id=to, device_id_type=MESH)

        mine = [pltpu.make_async_copy(src[p], dst[p].at[me], local_sems.at[p]) for p in range(n)]
        for cp in mine:
            cp.start()
        first = []
        for p in range(n):
            first.append(copy(p, 0, (x, y, c), sibling, from_src=True))
            for jj, chip in enumerate(chips):
                first.append(copy(p, 1 + jj, (x, y, c), (*chip, c), from_src=True))
        for cp in first:
            cp.start()
        passed = []
        for jj, chip in enumerate(chips):
            for p in range(n):
                copy(p, 1 + jj, (*chip, c), (x, y, c)).wait_recv()
                fwd = copy(p, 4 + jj, (*chip, c), sibling)
                fwd.start()
                passed.append(fwd)
        for p in range(n):
            copy(p, 0, sibling, (x, y, c)).wait_recv()
            for jj, chip in enumerate(chips):
                copy(p, 4 + jj, (*chip, 1 - c), (x, y, c)).wait_recv()
        for cp in first + passed:
            cp.wait_send()
        for cp in mine:
            cp.wait()

    return pl.pallas_call(
        body,
        name=name,
        out_shape=[jax.ShapeDtypeStruct((N_DEV,) + a.shape, a.dtype) for a in srcs],
        in_specs=[ANY] * n,
        out_specs=[ANY] * n,
        scratch_shapes=[pltpu.SemaphoreType.DMA((n, 7)), pltpu.SemaphoreType.DMA((n, 7)),
                        pltpu.SemaphoreType.DMA((n,))],
    )(*srcs)


def exchange_only(*, name, jobs):
    def body(o_ref):
        o_ref[...] = jnp.zeros_like(o_ref)

    _, bufs = _call(body, name=name, jobs=jobs, out_shape=[jax.ShapeDtypeStruct((8, LANE), F32)], grid=(1,),
                    in_specs=[], out_specs=[pl.BlockSpec((8, LANE), lambda i: (0, 0))], args=[], sem=("arbitrary",))
    return None, bufs


def _pack_rows(parts, rows):
    flat = jnp.concatenate([p.reshape(-1) for p in parts])
    return jnp.pad(flat, (0, rows * LANE - flat.shape[0])).reshape(rows, LANE)


def _unpack_rows(packed, shapes):
    flat = packed.reshape(-1)
    out, at = [], 0
    for sh in shapes:
        size = 1
        for dim in sh:
            size *= dim
        out.append(flat[at:at + size].reshape(sh))
        at += size
    return out


CONV_W_PAD = 768
SMALL_W_ROWS = 56


def _pack_small_weights(w_a2, b_a2, hn, conv_w):
    cw = jnp.pad(conv_w.reshape(6, -1), ((0, 0), (0, CONV_W_PAD - conv_w.shape[-1]))).reshape(-1, LANE)
    rows = jnp.concatenate([w_a2[0], b_a2, jnp.pad(hn, ((0, 0), (0, LANE - hn.shape[-1]))), cw], axis=0)
    return jnp.pad(rows, ((0, SMALL_W_ROWS - rows.shape[0]), (0, 0)))


def _unpack_small_weights(gathered):
    w_a2 = gathered[:, 0:GATE_RANK, :].transpose(1, 0, 2).reshape(GATE_RANK, GLA_KEY_DIM)
    b_a2 = gathered[:, GATE_RANK, :].reshape(1, GLA_KEY_DIM)
    hn = gathered[:, GATE_RANK + 1, :GLA_DV // N_DEV].reshape(1, GLA_DV)
    per = D_FF // N_DEV
    cw = gathered[:, GATE_RANK + 2:GATE_RANK + 2 + 6 * CONV_W_PAD // LANE, :].reshape(N_DEV, 6, CONV_W_PAD)[:, :, :per]
    cw = cw.reshape(N_DEV, 2, 3, per).transpose(1, 2, 0, 3).reshape(2, 3, D_FF)
    return w_a2, b_a2, hn, cw


SCHEDULE = {
    "gla_in": [("g1", "gout", None), ("g1", "up0", (0, 1024))],
    "gla_fwd": [("g2", "gout", None), ("g2", "up0", (0, 1024)), ("g1", "up0", (1024, 2048))],
    "gla_out": [("g2", "up0", (1024, 2048)), ("g1", "dn0", (0, 352))],
    "ffn_up0": [("g2", "dn0", (0, 352)), ("g1", "dn0", (352, 704)), ("g1", "kv", None), ("g1", "q", (0, 768))],
    "convglu_fwd0": [("g2", "dn0", (352, 704))],
    "ffn_down0": [("g2", "kv", None), ("g2", "q", (0, 768)), ("g1", "q", (768, 2048)), ("g1", "dout", None)],
    "kv_proj": [("g2", "q", (768, 2048)), ("g2", "dout", None), ("g1", "up1", (0, 704))],
    "q_proj": [("g2", "up1", (0, 704)), ("g1", "up1", (704, 1664))],
    "attn_fwd": [("g2", "up1", (704, 1664)), ("g1", "up1", (1664, 2048)), ("g1", "dn1", None)],
    "dsa_out": [("g2", "up1", (1664, 2048)), ("g2", "dn1", None)],
    "ffn_down_dx1": [("sc", "dn1", (0, 352))],
    "convglu_bwd1": [("sc", "dn1", (352, 704))],
    "ffn_up_dx1": [("sc", "up1", (0, 1024))],
    "attn_bwd": [("sc", "up1", (1024, 2048)), ("sc", "dout", None)],
    "q_proj_dx": [("sc", "q", (0, 1024))],
    "kv_proj_dw": [("sc", "q", (1024, 1792))],
    "kv_proj_dx": [("sc", "q", (1792, 2048)), ("sc", "kv", (0, 768))],
    "ffn_down_dw0": [("sc", "kv", (768, 2048))],
    "ffn_down_dx0": [("sc", "dn0", (0, 384))],
    "convglu_bwd0": [("sc", "dn0", (384, 704))],
    "ffn_up_dx0": [("sc", "up0", (0, 1024))],
    "gla_out_dw": [("sc", "up0", (1024, 1216))],
    "gla_out_dx": [("sc", "up0", (1216, 1408))],
    "gla_bwd": [("sc", "up0", (1408, 2048))],
    "gla_in_dw": [("sc", "gout", None)],
    "gla_in_dx": [("sc", "in", None)],
    "grads_tail": [("all", "small", None)],
}
ROW_SHARDED = ("gout", "dout", "dn0", "dn1")


class Plan:
    def __init__(self, weights, srcs=None):
        self.w = dict(weights)
        self.srcs = srcs
        self.grads = {}
        self.recv = {}
        self._names = None

    def weight(self, name):
        buf = self.w[name]
        if name in ROW_SHARDED:
            return buf.reshape(1, buf.shape[0] * buf.shape[1], buf.shape[2])
        return buf

    def jobs(self, call):
        ops = SCHEDULE.get(call)
        if self.srcs is None or not ops:
            return None
        jobs, handles = Jobs(), {}
        backward = ops[0][0] in ("sc", "all")
        for op, name, rows in ops:
            assert (op in ("sc", "all")) == backward
            store = self.recv if backward else self.w
            if name not in handles:
                if name in store:
                    handles[name] = jobs.thru(store[name])
                elif op == "sc":
                    handles[name] = jobs.new(self.grads[name].shape, BF16)
                elif op == "all":
                    handles[name] = jobs.new((N_DEV,) + self.grads[name].shape, self.grads[name].dtype)
                else:
                    handles[name] = jobs.new((N_DEV,) + self.srcs[name].shape, BF16)
            if op == "g1":
                jobs.gather_ici(self.srcs[name], handles[name], rows)
            elif op == "g2":
                jobs.gather_d2d(handles[name], rows)
            else:
                jobs.scatter(self.grads[name], handles[name], rows, same=op == "all")
        self._names = [(name, self.recv if backward else self.w) for name in handles]
        return jobs

    def run(self, call, fn, *args, **kwargs):
        jobs = self.jobs(call)
        if jobs is None:
            return fn(*args, **kwargs)
        out, bufs = fn(*args, jobs=jobs, **kwargs)
        for (name, store), buf in zip(self._names, bufs):
            store[name] = buf
        return out


def _ffn_fwd(plan, h, norm_g, conv_w, conv_b, tag):
    (n,) = rms_fwd(h, [norm_g], name=f"ffn_norm_fwd{tag}")
    up = plan.run(f"ffn_up{tag}", mm_nn, n, plan.weight(f"up{tag}"), out_dtype=BF16, name=f"ffn_up{tag}")
    act = plan.run(f"convglu_fwd{tag}", convglu_fwd, up, conv_w, conv_b, name=f"convglu_fwd{tag}")
    h_out = plan.run(f"ffn_down{tag}", mm_nn, act, plan.weight(f"dn{tag}"), out_dtype=F32, res=h,
                     name=f"ffn_down{tag}")
    return h_out, (n, up, act)


def _by_rows(dw):
    return dw.reshape(N_DEV, dw.shape[1] // N_DEV, dw.shape[2])


def _ffn_bwd(plan, dh_out, h, saved, norm_g, conv_w, conv_b, tag):
    n, up, act = saved
    plan.grads[f"dn{tag}"] = _by_rows(plan.run(f"ffn_down_dw{tag}", mm_tn, act, dh_out, 1, name=f"ffn_down_dw{tag}"))
    dact = plan.run(f"ffn_down_dx{tag}", mm_nt, dh_out, plan.weight(f"dn{tag}"), out_dtype=BF16,
                    name=f"ffn_down_dx{tag}")
    dup, dconv_w, dconv_b = plan.run(f"convglu_bwd{tag}", convglu_bwd, up, conv_w, conv_b, dact,
                                     name=f"convglu_bwd{tag}")
    plan.grads[f"up{tag}"] = mm_tn(n, dup, N_DEV, name=f"ffn_up_dw{tag}")
    dh, (dnorm,) = plan.run(f"ffn_up_dx{tag}", mm_nt, dup, plan.weight(f"up{tag}"), out_dtype=F32,
                            name=f"ffn_up_dx{tag}", norm=(h, dh_out, [norm_g], []))
    return dh, dnorm, dconv_w, dconv_b


def local_step(x, target, wts, plan):
    row = lambda v: v.reshape(1, -1)
    attn_norm, ffn_norm = wts["attn_norm"], wts["ffn_norm"]
    conv_w, conv_b = wts["ffn_conv_w"], wts["ffn_conv_b"]

    (n1,) = rms_fwd(x, [row(attn_norm[0])], name="attn_norm_fwd0")
    proj = plan.run("gla_in", mm_nn, n1, wts["gla_w_in"], out_dtype=F32, name="gla_in")
    la = gate_fwd(proj, wts["gla_w_a2"], wts["gla_b_a2"])
    o_gla, states, og = plan.run("gla_fwd", gla_fwd, proj, la, wts["gla_head_norm"])
    h1 = plan.run("gla_out", mm_nn, og, plan.weight("gout"), out_dtype=F32, res=x, name="gla_out")
    h2, ffn0 = _ffn_fwd(plan, h1, row(ffn_norm[0]), conv_w[0], row(conv_b[0]), "0")

    h2s = to_streams(h2, name="h2_to_streams")
    kvn, n3 = rms_fwd(h2s, [row(wts["kv_norm"]), row(attn_norm[1])], name="kv_attn_norm_fwd")
    kv = plan.run("kv_proj", mm_nn, kvn, plan.weight("kv"), out_dtype=BF16, name="kv_proj")
    q = plan.run("q_proj", mm_nn, n3, plan.weight("q"), out_dtype=BF16, name="q_proj")
    o_att, lse = plan.run("attn_fwd", attn_fwd, q, kv)
    h3 = from_streams(plan.run("dsa_out", mm_nn, o_att, plan.weight("dout"), out_dtype=F32, res=h2s, name="dsa_out"),
                      name="h3_from_streams")
    h4, ffn1 = _ffn_fwd(plan, h3, row(ffn_norm[1]), conv_w[1], row(conv_b[1]), "1")

    loss_tile, dh4, d_final = loss_head(h4, row(wts["final_norm"]), target)

    dh3, d_ffn1, dcw1, dcb1 = _ffn_bwd(plan, dh4, h3, ffn1, row(ffn_norm[1]), conv_w[1], row(conv_b[1]), "1")
    dh3s = to_streams(dh3, name="dh3_to_streams")
    plan.grads["dout"] = _by_rows(mm_tn(o_att, dh3s, 1, name="dsa_out_dw"))
    do_att = mm_nt(dh3s, plan.weight("dout"), out_dtype=F32, name="dsa_out_dx")
    dq, dkv = plan.run("attn_bwd", attn_bwd, q, kv, o_att, lse, do_att)
    plan.grads["q"] = mm_tn(n3, dq, N_DEV, name="q_proj_dw")
    dh2_part, (d_attn1,) = plan.run("q_proj_dx", mm_nt, dq, plan.weight("q"), out_dtype=F32, name="q_proj_dx",
                                    norm=(h2s, dh3s, [row(attn_norm[1])], []))
    plan.grads["kv"] = plan.run("kv_proj_dw", mm_tn, kvn, dkv, N_DEV, name="kv_proj_dw")
    dh2s, (d_kvnorm,) = plan.run("kv_proj_dx", mm_nt, dkv, plan.weight("kv"), out_dtype=F32, name="kv_proj_dx",
                                 norm=(h2s, dh2_part, [row(wts["kv_norm"])], []))
    dh2 = from_streams(dh2s, name="dh2_from_streams")
    dh1, d_ffn0, dcw0, dcb0 = _ffn_bwd(plan, dh2, h1, ffn0, row(ffn_norm[0]), conv_w[0], row(conv_b[0]), "0")
    plan.grads["gout"] = _by_rows(plan.run("gla_out_dw", mm_tn, og, dh1, 1, name="gla_out_dw"))
    dog = plan.run("gla_out_dx", mm_nt, dh1, plan.weight("gout"), out_dtype=F32, name="gla_out_dx")
    dq_g, dk_g, dv_g, dr, dla, d_hn = plan.run("gla_bwd", gla_bwd, proj, la, states, o_gla, wts["gla_head_norm"], dog)
    da, dw_a2p, db_a2 = gate_bwd(proj, wts["gla_w_a2"], wts["gla_b_a2"], dla)
    dproj = jnp.concatenate([dq_g, dk_g, dv_g, dr, da], axis=1)
    assert dproj.shape[1] == GLA_IN_PAD
    dw_in = plan.run("gla_in_dw", mm_tn, n1, dproj, 1, name="gla_in_dw")
    plan.grads["in"] = dw_in[0, :, :GLA_IN_DIM].reshape(D_MODEL, N_DEV, GLA_IN_DIM // N_DEV).transpose(1, 0, 2)
    grad_x, (d_attn0,) = plan.run("gla_in_dx", mm_nt, dproj, wts["gla_w_in"], out_dtype=F32, name="gla_in_dx",
                                  norm=(x, dh1, [row(attn_norm[0])], []))

    small = dict(
        attn_norm=jnp.concatenate([d_attn0, d_attn1], axis=0),
        ffn_norm=jnp.concatenate([d_ffn0, d_ffn1], axis=0),
        kv_norm=d_kvnorm.reshape(-1),
        final_norm=d_final.reshape(-1),
        ffn_conv_b=jnp.concatenate([dcb0, dcb1], axis=0),
        gla_w_a2=dw_a2p[:GATE_RANK],
        gla_b_a2=db_a2,
        gla_head_norm=d_hn,
        ffn_conv_w=jnp.stack([dcw0, dcw1]),
    )
    return loss_tile, grad_x, small


SMALL_ORDER = ("attn_norm", "ffn_norm", "kv_norm", "final_norm", "ffn_conv_b",
               "gla_w_a2", "gla_b_a2", "gla_head_norm", "ffn_conv_w")
SMALL_FULL = dict(attn_norm=(2, D_MODEL), ffn_norm=(2, D_MODEL), kv_norm=(D_MODEL,), final_norm=(D_MODEL,),
                  ffn_conv_b=(2, D_FF), gla_w_a2=(GATE_RANK, GLA_KEY_DIM), gla_b_a2=(1, GLA_KEY_DIM),
                  gla_head_norm=(1, GLA_DV), ffn_conv_w=(2, 3, D_FF))
SMALL_SHARDED = ("gla_w_a2", "gla_b_a2", "gla_head_norm", "ffn_conv_w")
SMALL_GRAD_ROWS = 592
SMALL_ADAM_ROWS = 240


def kernel(x, attn_norm, gla_w_in, gla_w_a2, gla_b_a2, gla_head_norm, gla_w_out, kv_norm, w_kv, dsa_w_q, dsa_w_out, ffn_norm, ffn_w_up, ffn_conv_w, ffn_conv_b, ffn_w_down, final_norm, loss_target, m_attn_norm, m_gla_w_in, m_gla_w_a2, m_gla_b_a2, m_gla_head_norm, m_gla_w_out, m_kv_norm, m_w_kv, m_dsa_w_q, m_dsa_w_out, m_ffn_norm, m_ffn_w_up, m_ffn_conv_w, m_ffn_conv_b, m_ffn_w_down, m_final_norm, v_attn_norm, v_gla_w_in, v_gla_w_a2, v_gla_b_a2, v_gla_head_norm, v_gla_w_out, v_kv_norm, v_w_kv, v_dsa_w_q, v_dsa_w_out, v_ffn_norm, v_ffn_w_up, v_ffn_conv_w, v_ffn_conv_b, v_ffn_w_down, v_final_norm):
    me = 4 * lax.axis_index("x") + 2 * lax.axis_index("y") + lax.axis_index("c")
    bf = lambda a: a.astype(BF16)

    g_in, g_small = all_gather([bf(gla_w_in[0]), _pack_small_weights(gla_w_a2, gla_b_a2, gla_head_norm, ffn_conv_w)],
                               name="gather_first")
    w_a2_full, b_a2_full, hn_full, conv_w_full = _unpack_small_weights(g_small)
    w_in_full = jnp.pad(g_in.transpose(1, 0, 2).reshape(D_MODEL, GLA_IN_DIM), ((0, 0), (0, GLA_IN_PAD - GLA_IN_DIM)))
    wts = dict(
        attn_norm=attn_norm, ffn_norm=ffn_norm, kv_norm=kv_norm, final_norm=final_norm, ffn_conv_b=ffn_conv_b,
        gla_w_in=w_in_full[None],
        gla_w_a2=jnp.pad(bf(w_a2_full), ((0, LANE - GATE_RANK), (0, 0))),
        gla_b_a2=b_a2_full, gla_head_norm=hn_full, ffn_conv_w=conv_w_full,
    )
    plan = Plan({}, srcs=dict(gout=bf(gla_w_out[0]), kv=bf(w_kv), q=bf(dsa_w_q[0]), dout=bf(dsa_w_out[0]),
                              up0=bf(ffn_w_up[0]), up1=bf(ffn_w_up[1]), dn0=bf(ffn_w_down[0]), dn1=bf(ffn_w_down[1])))

    loss_tile, grad_x, small = local_step(x[0], loss_target[0], wts, plan)
    loss = lax.psum(loss_tile[0, 0], ("x", "y", "c"))

    plan.grads["small"] = _pack_rows([small[nm] for nm in SMALL_ORDER], SMALL_GRAD_ROWS)
    plan.run("grads_tail", exchange_only, name="grads_tail")
    shard3 = lambda a: a.reshape((-1,) + a.shape[-2:])
    big_params = dict(gla_w_in=(("in",), gla_w_in, m_gla_w_in, v_gla_w_in),
                      gla_w_out=(("gout",), gla_w_out, m_gla_w_out, v_gla_w_out),
                      w_kv=(("kv",), w_kv, m_w_kv, v_w_kv),
                      dsa_w_q=(("q",), dsa_w_q, m_dsa_w_q, v_dsa_w_q),
                      dsa_w_out=(("dout",), dsa_w_out, m_dsa_w_out, v_dsa_w_out),
                      ffn_w_up=(("up0", "up1"), ffn_w_up, m_ffn_w_up, v_ffn_w_up),
                      ffn_w_down=(("dn0", "dn1"), ffn_w_down, m_ffn_w_down, v_ffn_w_down))
    res = {}
    for nm, (parts, w, m, v) in big_params.items():
        outs = adam_sharded([plan.recv[p] for p in parts], shard3(w), shard3(m), shard3(v), name=f"adam_{nm}")
        res[nm] = [o.reshape(w.shape) for o in outs]

    full = dict(zip(SMALL_ORDER, _unpack_rows(sum_partials(plan.recv["small"]),
                                              [SMALL_FULL[nm] for nm in SMALL_ORDER])))
    local_w = dict(attn_norm=attn_norm, ffn_norm=ffn_norm, kv_norm=kv_norm, final_norm=final_norm,
                   ffn_conv_b=ffn_conv_b, gla_w_a2=gla_w_a2, gla_b_a2=gla_b_a2, gla_head_norm=gla_head_norm,
                   ffn_conv_w=ffn_conv_w)
    local_m = dict(attn_norm=m_attn_norm, ffn_norm=m_ffn_norm, kv_norm=m_kv_norm, final_norm=m_final_norm,
                   ffn_conv_b=m_ffn_conv_b, gla_w_a2=m_gla_w_a2, gla_b_a2=m_gla_b_a2, gla_head_norm=m_gla_head_norm,
                   ffn_conv_w=m_ffn_conv_w)
    local_v = dict(attn_norm=v_attn_norm, ffn_norm=v_ffn_norm, kv_norm=v_kv_norm, final_norm=v_final_norm,
                   ffn_conv_b=v_ffn_conv_b, gla_w_a2=v_gla_w_a2, gla_b_a2=v_gla_b_a2, gla_head_norm=v_gla_head_norm,
                   ffn_conv_w=v_ffn_conv_w)
    local_g = {}
    for nm in SMALL_ORDER:
        gfull = full[nm]
        if nm in SMALL_SHARDED:
            per = gfull.shape[-1] // N_DEV
            gfull = lax.dynamic_slice_in_dim(gfull, me * per, per, axis=gfull.ndim - 1)
        local_g[nm] = gfull.reshape(local_w[nm].shape)
    shapes = [local_w[nm].shape for nm in SMALL_ORDER]
    pk = lambda dd: _pack_rows([dd[nm] for nm in SMALL_ORDER], SMALL_ADAM_ROWS)
    d_p, m_p, v_p = adam_packed(pk(local_w), pk(local_g), pk(local_m), pk(local_v))
    for nm, dl, mn, vn in zip(SMALL_ORDER, _unpack_rows(d_p, shapes), _unpack_rows(m_p, shapes),
                              _unpack_rows(v_p, shapes)):
        res[nm] = [local_g[nm], dl, mn, vn]

    order = ("attn_norm", "gla_w_in", "gla_w_a2", "gla_b_a2", "gla_head_norm", "gla_w_out", "kv_norm", "w_kv",
             "dsa_w_q", "dsa_w_out", "ffn_norm", "ffn_w_up", "ffn_conv_w", "ffn_conv_b", "ffn_w_down", "final_norm")
    outs = [loss, grad_x[None]]
    for kind in range(4):
        outs.extend(res[nm][kind] for nm in order)
    return tuple(outs)
```

```python
import functools

import jax
import jax.numpy as jnp
from jax import lax
from jax.experimental import pallas as pl
from jax.experimental.pallas import tpu as pltpu

F32 = jnp.float32
BF16 = jnp.bfloat16
MESH = pl.DeviceIdType.MESH
ANY = pl.BlockSpec(memory_space=pl.ANY)

N_DEV = 8
D_MODEL = 2048
GLA_HEADS = 4
GLA_KEY_DIM = 1024
GLA_VAL_DIM = 2048
GLA_DK = 256
GLA_DV = 512
GATE_RANK = 16
GATE_NORMALIZER = 16.0
GLA_CHUNK = 64
GLA_STEP_CHUNKS = 4
GLA_IN_DIM = 2 * GLA_KEY_DIM + 2 * GLA_VAL_DIM + GATE_RANK
GLA_IN_PAD = 6272
ATT_HEADS = 16
HEAD_DIM = 128
DILATIONS = (1, 4, 16)
STREAMS = DILATIONS[-1]
ATT_BLOCK = 128
D_FF = 5632
EPS = 1e-6
ADAM_LR = 0.001
ADAM_B1 = 0.9
ADAM_B2 = 0.999
ADAM_EPS = 1e-08
ADAM_WD = 0.01
ADAM_STEP = 10
NEG = -1e30
LANE = 128
NORM_ROWS = 64
VMEM_LIMIT = 52 * 1024 * 1024
ALIBI_SLOPES = tuple(2.0 ** (-0.5 * (i + 1)) for i in range(ATT_HEADS))


def _params(*sem):
    return pltpu.CompilerParams(dimension_semantics=sem, vmem_limit_bytes=VMEM_LIMIT)


def _tile(n, cap):
    best = None
    for t in range(LANE, min(n, cap) + 1, LANE):
        if n % t == 0:
            best = t
    return best if best is not None else n


def _shard_group(j, ns, cap):
    best = 1
    for g in range(1, j + 1):
        if j % g == 0 and g * ns <= cap:
            best = g
    return best


def _rows(r, c, budget=256 * 1024):
    best = None
    for t in range(16, r + 1, 16):
        if r % t == 0 and t * c <= budget:
            best = t
    return best if best is not None else r


def _flip(coord, bit):
    return 1 - coord if bit else coord


def _place():
    x, y, c = lax.axis_index("x"), lax.axis_index("y"), lax.axis_index("c")
    return x, y, c, 4 * x + 2 * y + c


def _rows_of(ref, rows):
    return ref if rows is None else ref.at[pl.ds(rows[0], rows[1] - rows[0])]


class Jobs:
    def __init__(self):
        self.srcs = []
        self.bufs = []
        self.sems = []
        self.steps = []

    def _src(self, a):
        for i, b in enumerate(self.srcs):
            if b is a:
                return i
        self.srcs.append(a)
        return len(self.srcs) - 1

    def new(self, shape, dtype):
        self.bufs.append((None, jax.ShapeDtypeStruct(shape, dtype)))
        return len(self.bufs) - 1

    def thru(self, a):
        self.bufs.append((a, jax.ShapeDtypeStruct(a.shape, a.dtype)))
        return len(self.bufs) - 1

    def _sem(self, n):
        self.sems.append(pltpu.SemaphoreType.DMA((n,)))
        return len(self.sems) - 1

    def gather_ici(self, src, buf, rows=None):
        si, send, recv, loc = self._src(src), self._sem(4), self._sem(4), self._sem(1)

        def remote(srcs, bufs, sems, slot_of):
            x, y, c, me = _place()
            peers = [(x, y, 1 - c), (1 - x, y, c), (x, 1 - y, c), (1 - x, 1 - y, c)]
            return [pltpu.make_async_remote_copy(
                src_ref=_rows_of(srcs[si], rows),
                dst_ref=_rows_of(bufs[buf].at[me if slot_of == "mine" else 4 * p[0] + 2 * p[1] + p[2]], rows),
                send_sem=sems[send].at[k], recv_sem=sems[recv].at[k], device_id=p, device_id_type=MESH)
                for k, p in enumerate(peers)]

        def local(srcs, bufs, sems):
            return pltpu.make_async_copy(_rows_of(srcs[si], rows), _rows_of(bufs[buf].at[_place()[3]], rows),
                                         sems[loc].at[0])

        def start(srcs, bufs, sems):
            local(srcs, bufs, sems).start()
            for cp in remote(srcs, bufs, sems, "mine"):
                cp.start()

        def finish(srcs, bufs, sems):
            for cp in remote(srcs, bufs, sems, "peer"):
                cp.wait_recv()
            for cp in remote(srcs, bufs, sems, "mine"):
                cp.wait_send()
            local(srcs, bufs, sems).wait()

        self.steps.append((start, finish))

    def gather_d2d(self, buf, rows=None):
        send, recv = self._sem(3), self._sem(3)

        def copies(bufs, sems, core):
            x, y, c, _ = _place()
            cc = c if core == "mine" else 1 - c
            chips = [(1 - x, y), (x, 1 - y), (1 - x, 1 - y)]
            return [pltpu.make_async_remote_copy(
                src_ref=_rows_of(bufs[buf].at[4 * px + 2 * py + cc], rows),
                dst_ref=_rows_of(bufs[buf].at[4 * px + 2 * py + cc], rows),
                send_sem=sems[send].at[k], recv_sem=sems[recv].at[k],
                device_id=(x, y, 1 - c), device_id_type=MESH) for k, (px, py) in enumerate(chips)]

        def start(srcs, bufs, sems):
            for cp in copies(bufs, sems, "mine"):
                cp.start()

        def finish(srcs, bufs, sems):
            for cp in copies(bufs, sems, "sibling"):
                cp.wait_recv()
            for cp in copies(bufs, sems, "mine"):
                cp.wait_send()

        self.steps.append((start, finish))

    def scatter(self, src, buf, rows=None, same=False):
        si, send, recv, loc = self._src(src), self._sem(N_DEV - 1), self._sem(N_DEV - 1), self._sem(1)

        def block(srcs, dev):
            return _rows_of(srcs[si] if same else srcs[si].at[dev], rows)

        def remote(srcs, bufs, sems, slot_of):
            x, y, c, me = _place()
            out = []
            for k in range(1, N_DEV):
                px, py, pc = _flip(x, k >> 2), _flip(y, (k >> 1) & 1), _flip(c, k & 1)
                peer = 4 * px + 2 * py + pc
                out.append(pltpu.make_async_remote_copy(
                    src_ref=block(srcs, peer),
                    dst_ref=_rows_of(bufs[buf].at[me if slot_of == "mine" else peer], rows),
                    send_sem=sems[send].at[k - 1], recv_sem=sems[recv].at[k - 1],
                    device_id=(px, py, pc), device_id_type=MESH))
            return out

        def local(srcs, bufs, sems):
            me = _place()[3]
            return pltpu.make_async_copy(block(srcs, me), _rows_of(bufs[buf].at[me], rows), sems[loc].at[0])

        def start(srcs, bufs, sems):
            local(srcs, bufs, sems).start()
            for cp in remote(srcs, bufs, sems, "mine"):
                cp.start()

        def finish(srcs, bufs, sems):
            for cp in remote(srcs, bufs, sems, "peer"):
                cp.wait_recv()
            for cp in remote(srcs, bufs, sems, "mine"):
                cp.wait_send()
            local(srcs, bufs, sems).wait()

        self.steps.append((start, finish))


def _call(body, *, name, grid, in_specs, out_specs, out_shape, args, sem, scratch_shapes=(), jobs=None):
    in_specs, out_specs, out_shape = list(in_specs), list(out_specs), list(out_shape)
    scratch_shapes = list(scratch_shapes)
    if jobs is None:
        res = pl.pallas_call(body, name=name, out_shape=out_shape, grid=grid, in_specs=in_specs,
                             out_specs=out_specs, scratch_shapes=scratch_shapes,
                             compiler_params=_params(*sem))(*args)
        return list(res), []
    thru = [a for a, _ in jobs.bufs if a is not None]
    n_in, n_src, n_thru = len(args), len(jobs.srcs), len(thru)
    n_out, n_buf, n_scr = len(out_shape), len(jobs.bufs), len(scratch_shapes)
    aliases, t = {}, 0
    for b, (a, _) in enumerate(jobs.bufs):
        if a is not None:
            aliases[n_in + n_src + t] = n_out + b
            t += 1

    def wrapped(*refs):
        at = 0
        ins = refs[at:at + n_in]; at += n_in
        srcs = refs[at:at + n_src]; at += n_src + n_thru
        outs = refs[at:at + n_out]; at += n_out
        bufs = refs[at:at + n_buf]; at += n_buf
        scr = refs[at:at + n_scr]; at += n_scr
        sems = refs[at:]
        first, last = None, None
        for axis, size in enumerate(grid):
            pid = pl.program_id(axis)
            f, l = pid == 0, pid == size - 1
            first = f if first is None else first & f
            last = l if last is None else last & l

        @pl.when(first)
        def _():
            for start, _ in jobs.steps:
                start(srcs, bufs, sems)

        body(*ins, *outs, *scr)

        @pl.when(last)
        def _():
            for _, finish in jobs.steps:
                finish(srcs, bufs, sems)

    res = pl.pallas_call(
        wrapped, name=name,
        out_shape=out_shape + [s for _, s in jobs.bufs],
        grid=grid,
        in_specs=in_specs + [ANY] * (n_src + n_thru),
        out_specs=out_specs + [ANY] * n_buf,
        scratch_shapes=scratch_shapes + jobs.sems,
        input_output_aliases=aliases,
        compiler_params=_params(*(["arbitrary"] * len(grid))),
    )(*args, *jobs.srcs, *thru)
    return res[:n_out], res[n_out:]


def mm_nn(a, w, *, out_dtype, name, res=None, tm=None, jobs=None):
    m, k = a.shape
    j, k2, ns = w.shape
    whole = j == 1 and ns <= 2048 and k <= 2048
    tm = tm or (1024 if a.dtype == BF16 and not whole else 512)
    assert k == k2 and m % tm == 0
    tn = ns if whole else _tile(ns, 1408)
    nsub = ns // tn
    tk = k if k <= 2048 else _tile(k, 1408)
    nk = k // tk
    has_res = res is not None

    def body(*refs):
        if has_res:
            a_ref, w_ref, r_ref, o_ref, acc = refs
        else:
            a_ref, w_ref, o_ref, acc = refs
        kk = pl.program_id(2)

        @pl.when(kk == 0)
        def _():
            acc[...] = jnp.zeros_like(acc)

        acc[...] += jnp.dot(a_ref[...].astype(BF16), w_ref[...], preferred_element_type=F32)

        @pl.when(kk == nk - 1)
        def _():
            r = acc[...]
            if has_res:
                r = r + r_ref[...]
            o_ref[...] = r.astype(out_dtype)

    in_specs = [
        pl.BlockSpec((tm, tk), lambda i, n, kk: (i, kk)),
        pl.BlockSpec((None, tk, tn), lambda i, n, kk: (n // nsub, kk, n % nsub)),
    ]
    args = [a, w]
    out_tile = pl.BlockSpec((tm, tn), lambda i, n, kk: (i, n))
    if has_res:
        in_specs.append(out_tile)
        args.append(res)
    (out,), bufs = _call(
        body, name=name, jobs=jobs,
        out_shape=[jax.ShapeDtypeStruct((m, j * ns), out_dtype)],
        grid=(m // tm, j * nsub, nk),
        in_specs=in_specs,
        out_specs=[out_tile],
        scratch_shapes=[pltpu.VMEM((tm, tn), F32)],
        args=args, sem=("parallel", "parallel", "arbitrary"))
    return out if jobs is None else (out, bufs)


def mm_nt(dy, w, *, out_dtype, name, tm=None, jobs=None, norm=None):
    parts, m, n = (1,) + dy.shape if dy.ndim == 2 else dy.shape
    n *= parts
    j, k, ns = w.shape
    if norm is not None:
        x, dres, gains, more = norm
        tm = tm or (256 if more else 512)
    tm = tm or 1024
    assert n == j * ns and m % tm == 0
    tn = _tile(ns, 2048)
    nsub = ns // tn
    jb = _shard_group(j // parts, ns, 2048 if norm is None else 1024) if nsub == 1 else 1
    tko = _tile(k, 1408) if norm is None else k
    nn = j * nsub // jb
    per_part = nn // parts
    if dy.ndim == 2:
        dy_spec = pl.BlockSpec((tm, jb * tn), lambda i, ko, nq: (i, nq))
    else:
        dy_spec = pl.BlockSpec((None, tm, jb * tn), lambda i, ko, nq: (nq // per_part, i, nq % per_part))
    if jb == 1:
        w_spec = pl.BlockSpec((None, tko, tn), lambda i, ko, nq: (nq // nsub, ko, nq % nsub))
    else:
        w_spec = pl.BlockSpec((jb, tko, ns), lambda i, ko, nq: (nq, ko, 0))

    n_gain = 0 if norm is None else len(gains)
    n_more = 0 if norm is None else len(more)

    def body(*refs):
        a_ref, w_ref = refs[:2]
        acc = refs[-1]
        nq = pl.program_id(2)
        first = pl.program_id(0) == 0

        @pl.when(nq == 0)
        def _():
            acc[...] = jnp.zeros_like(acc)

        if jb == 1:
            acc[...] += lax.dot_general(a_ref[...].astype(BF16), w_ref[...], (((1,), (1,)), ((), ())),
                                        preferred_element_type=F32)
        else:
            part = acc[...]
            for jj in range(jb):
                part = part + lax.dot_general(a_ref[:, jj * ns:(jj + 1) * ns].astype(BF16), w_ref[jj],
                                              (((1,), (1,)), ((), ())), preferred_element_type=F32)
            acc[...] = part

        @pl.when(nq == nn - 1)
        def _():
            if norm is None:
                refs[2][...] = acc[...].astype(out_dtype)
                return
            x_ref, r_ref = refs[2:4]
            g_refs = refs[4:4 + n_gain]
            e_refs = refs[4 + n_gain:4 + n_gain + n_more]
            dx_ref = refs[4 + n_gain + n_more]
            dg_refs = refs[5 + n_gain + n_more:-1]

            @pl.when(first)
            def _():
                for dg_ref in dg_refs:
                    dg_ref[...] = jnp.zeros_like(dg_ref)

            def rows(c, carry):
                sl = pl.ds(pl.multiple_of(c * NORM_ROWS, NORM_ROWS), NORM_ROWS)
                xv = x_ref[sl, :]
                r = lax.rsqrt(jnp.mean(xv * xv, axis=-1, keepdims=True) + EPS)
                xh = xv * r
                out = r_ref[sl, :]
                for idx, (g_ref, dg_ref) in enumerate(zip(g_refs, dg_refs)):
                    dyv = acc[sl, :] if idx == 0 else e_refs[idx - 1][sl, :].astype(F32)
                    dg_ref[...] += jnp.sum(dyv * xh, axis=0, keepdims=True)
                    dxh = dyv * g_ref[...]
                    out = out + r * (dxh - xh * jnp.mean(dxh * xh, axis=-1, keepdims=True))
                dx_ref[sl, :] = out
                return carry

            lax.fori_loop(0, tm // NORM_ROWS, rows, 0)

    out_tile = pl.BlockSpec((tm, tko), lambda i, ko, nq: (i, ko))
    in_specs, args = [dy_spec, w_spec], [dy, w]
    out_shape, out_specs = [jax.ShapeDtypeStruct((m, k), out_dtype)], [out_tile]
    sem = ("parallel", "parallel", "arbitrary")
    if norm is not None:
        vec = pl.BlockSpec((1, k), lambda i, ko, nq: (0, 0))
        in_specs += [out_tile, out_tile] + [vec] * n_gain + [out_tile] * n_more
        args += [x, dres] + list(gains) + list(more)
        out_shape = [jax.ShapeDtypeStruct((m, k), F32)] + [jax.ShapeDtypeStruct((1, k), F32)] * n_gain
        out_specs = [out_tile] + [vec] * n_gain
        sem = ("arbitrary", "arbitrary", "arbitrary")
    outs, bufs = _call(
        body, name=name, jobs=jobs, out_shape=out_shape, grid=(m // tm, k // tko, nn),
        in_specs=in_specs, out_specs=out_specs, scratch_shapes=[pltpu.VMEM((tm, tko), F32)], args=args, sem=sem)
    out = outs[0] if norm is None else (outs[0], outs[1:])
    return out if jobs is None else (out, bufs)


def mm_tn(x, dy, j, *, name, tm=2048, jobs=None):
    m, k = x.shape
    parts, m2, n = (1,) + dy.shape if dy.ndim == 2 else dy.shape
    n *= parts
    assert m == m2 and n % j == 0 and m % tm == 0
    ns = n // j
    tn = _tile(ns, 1408)
    nsub = ns // tn
    jb = _shard_group(j // parts, ns, 1536) if nsub == 1 else 1
    tk = _tile(k, 1408)
    nm = m // tm
    n_steps = j * nsub // jb
    per_part = n_steps // parts
    if dy.ndim == 2:
        dy_spec = pl.BlockSpec((tm, jb * tn), lambda kq, nq, mi: (mi, nq))
    else:
        dy_spec = pl.BlockSpec((None, tm, jb * tn), lambda kq, nq, mi: (nq // per_part, mi, nq % per_part))
    if jb == 1:
        out_spec = pl.BlockSpec((None, tk, tn), lambda kq, nq, mi: (nq // nsub, kq, nq % nsub))
        acc_shape = (tk, tn)
    else:
        out_spec = pl.BlockSpec((jb, tk, ns), lambda kq, nq, mi: (nq, kq, 0))
        acc_shape = (jb, tk, ns)

    def body(x_ref, dy_ref, o_ref, acc):
        mi = pl.program_id(2)

        @pl.when(mi == 0)
        def _():
            acc[...] = jnp.zeros_like(acc)

        xb = x_ref[...].astype(BF16)
        if jb == 1:
            acc[...] += lax.dot_general(xb, dy_ref[...].astype(BF16), (((0,), (0,)), ((), ())),
                                        preferred_element_type=F32)
        else:
            for jj in range(jb):
                acc[jj] += lax.dot_general(xb, dy_ref[:, jj * ns:(jj + 1) * ns].astype(BF16),
                                           (((0,), (0,)), ((), ())), preferred_element_type=F32)

        @pl.when(mi == nm - 1)
        def _():
            o_ref[...] = acc[...].astype(BF16)

    (out,), bufs = _call(
        body, name=name, jobs=jobs,
        out_shape=[jax.ShapeDtypeStruct((j, k, ns), BF16)],
        grid=(k // tk, n_steps, nm),
        in_specs=[
            pl.BlockSpec((tm, tk), lambda kq, nq, mi: (mi, kq)),
            dy_spec,
        ],
        out_specs=[out_spec],
        scratch_shapes=[pltpu.VMEM(acc_shape, F32)],
        args=[x, dy], sem=("parallel", "parallel", "arbitrary"))
    return out if jobs is None else (out, bufs)


STREAM_TC = LANE


def to_streams(x, *, name):
    s, c = x.shape
    per = s // STREAMS

    def body(x_ref, o_ref):
        for st in range(STREAMS):
            o_ref[pl.ds(st * per, per), :] = x_ref[pl.ds(st, per, stride=STREAMS), :]

    blk = pl.BlockSpec((s, STREAM_TC), lambda i: (0, i))
    return pl.pallas_call(body, name=name, out_shape=jax.ShapeDtypeStruct((s, c), x.dtype), grid=(c // STREAM_TC,),
                          in_specs=[blk], out_specs=blk, compiler_params=_params("parallel"))(x)


def from_streams(x, *, name):
    s, c = x.shape
    per = s // STREAMS

    def body(x_ref, o_ref):
        for st in range(STREAMS):
            o_ref[pl.ds(st, per, stride=STREAMS), :] = x_ref[pl.ds(st * per, per), :]

    blk = pl.BlockSpec((s, STREAM_TC), lambda i: (0, i))
    return pl.pallas_call(body, name=name, out_shape=jax.ShapeDtypeStruct((s, c), x.dtype), grid=(c // STREAM_TC,),
                          in_specs=[blk], out_specs=blk, compiler_params=_params("parallel"))(x)


def rms_fwd(x, gains, *, name, ts=512):
    s, d = x.shape
    n = len(gains)

    def body(x_ref, *refs):
        xv = x_ref[...]
        xh = xv * lax.rsqrt(jnp.mean(xv * xv, axis=-1, keepdims=True) + EPS)
        for g_ref, o_ref in zip(refs[:n], refs[n:]):
            o_ref[...] = (xh * g_ref[...]).astype(BF16)

    row = pl.BlockSpec((ts, d), lambda i: (i, 0))
    vec = pl.BlockSpec((1, d), lambda i: (0, 0))
    return pl.pallas_call(
        body,
        name=name,
        out_shape=[jax.ShapeDtypeStruct((s, d), BF16)] * n,
        grid=(s // ts,),
        in_specs=[row] + [vec] * n,
        out_specs=[row] * n,
        compiler_params=_params("parallel"),
    )(x, *gains)


def loss_head(h, gain, target, *, ts=256):
    s, d = h.shape

    def body(h_ref, g_ref, t_ref, l_ref, dh_ref, dg_ref):
        i = pl.program_id(0)

        @pl.when(i == 0)
        def _():
            l_ref[...] = jnp.zeros_like(l_ref)
            dg_ref[...] = jnp.zeros_like(dg_ref)

        xv = h_ref[...]
        r = lax.rsqrt(jnp.mean(xv * xv, axis=-1, keepdims=True) + EPS)
        xh = xv * r
        g = g_ref[...]
        err = xh * g - t_ref[...]
        l_ref[...] += 0.5 * jnp.sum(jnp.mean(err * err, axis=-1, keepdims=True))
        dy = err * (1.0 / d)
        dg_ref[...] += jnp.sum(dy * xh, axis=0, keepdims=True)
        dxh = dy * g
        dh_ref[...] = r * (dxh - xh * jnp.mean(dxh * xh, axis=-1, keepdims=True))

    row = pl.BlockSpec((ts, d), lambda i: (i, 0))
    vec = pl.BlockSpec((1, d), lambda i: (0, 0))
    return pl.pallas_call(
        body,
        name="loss_head",
        out_shape=[jax.ShapeDtypeStruct((8, LANE), F32), jax.ShapeDtypeStruct((s, d), F32),
                   jax.ShapeDtypeStruct((1, d), F32)],
        grid=(s // ts,),
        in_specs=[row, vec, row],
        out_specs=[pl.BlockSpec((8, LANE), lambda i: (0, 0)), row, vec],
        compiler_params=_params("arbitrary"),
    )(h, gain, target)


A_BLOCK = (2 * GLA_KEY_DIM + 2 * GLA_VAL_DIM) // LANE


def gate_fwd(proj, w_a2p, b_a2, *, ts=512):
    s = proj.shape[0]

    def body(a_ref, w_ref, b_ref, o_ref):
        z = jnp.dot(a_ref[...].astype(BF16), w_ref[...], preferred_element_type=F32) + b_ref[...]
        o_ref[...] = (jnp.minimum(z, 0.0) - jnp.log(1.0 + jnp.exp(-jnp.abs(z)))) * (1.0 / GATE_NORMALIZER)

    return pl.pallas_call(
        body,
        name="gate_fwd",
        out_shape=jax.ShapeDtypeStruct((s, GLA_KEY_DIM), F32),
        grid=(s // ts,),
        in_specs=[pl.BlockSpec((ts, LANE), lambda i: (i, A_BLOCK)),
                  pl.BlockSpec((LANE, GLA_KEY_DIM), lambda i: (0, 0)),
                  pl.BlockSpec((1, GLA_KEY_DIM), lambda i: (0, 0))],
        out_specs=pl.BlockSpec((ts, GLA_KEY_DIM), lambda i: (i, 0)),
        compiler_params=_params("parallel"),
    )(proj, w_a2p, b_a2)


def gate_bwd(proj, w_a2p, b_a2, dla, *, ts=512):
    s = proj.shape[0]

    def body(a_ref, w_ref, b_ref, dla_ref, da_ref, dw_ref, db_ref):
        i = pl.program_id(0)

        @pl.when(i == 0)
        def _():
            dw_ref[...] = jnp.zeros_like(dw_ref)
            db_ref[...] = jnp.zeros_like(db_ref)

        a = a_ref[...].astype(BF16)
        w = w_ref[...]
        z = jnp.dot(a, w, preferred_element_type=F32) + b_ref[...]
        dz = dla_ref[...] * (1.0 / GATE_NORMALIZER) / (1.0 + jnp.exp(z))
        dzb = dz.astype(BF16)
        da_ref[...] = lax.dot_general(dzb, w, (((1,), (1,)), ((), ())), preferred_element_type=F32).astype(BF16)
        dw_ref[...] += lax.dot_general(a, dzb, (((0,), (0,)), ((), ())), preferred_element_type=F32)
        db_ref[...] += jnp.sum(dz, axis=0, keepdims=True)

    return pl.pallas_call(
        body,
        name="gate_bwd",
        out_shape=[jax.ShapeDtypeStruct((s, LANE), BF16), jax.ShapeDtypeStruct((LANE, GLA_KEY_DIM), F32),
                   jax.ShapeDtypeStruct((1, GLA_KEY_DIM), F32)],
        grid=(s // ts,),
        in_specs=[pl.BlockSpec((ts, LANE), lambda i: (i, A_BLOCK)),
                  pl.BlockSpec((LANE, GLA_KEY_DIM), lambda i: (0, 0)),
                  pl.BlockSpec((1, GLA_KEY_DIM), lambda i: (0, 0)),
                  pl.BlockSpec((ts, GLA_KEY_DIM), lambda i: (i, 0))],
        out_specs=[pl.BlockSpec((ts, LANE), lambda i: (i, 0)),
                   pl.BlockSpec((LANE, GLA_KEY_DIM), lambda i: (0, 0)),
                   pl.BlockSpec((1, GLA_KEY_DIM), lambda i: (0, 0))],
        compiler_params=_params("arbitrary"),
    )(proj, w_a2p, b_a2, dla)


def _masked_sum(mask, x):
    m = mask.astype(BF16)
    hi = x.astype(BF16)
    rest = x - hi.astype(F32)
    mid = rest.astype(BF16)
    lo = (rest - mid.astype(F32)).astype(BF16)
    dot = lambda t: jnp.dot(m, t, preferred_element_type=F32)
    return dot(hi) + dot(mid) + dot(lo)


def _chunk_terms(q, k, la):
    c_len = GLA_CHUNK
    row = lax.broadcasted_iota(jnp.int32, (c_len, c_len), 0)
    col = lax.broadcasted_iota(jnp.int32, (c_len, c_len), 1)
    tri = row >= col
    c = _masked_sum(tri, la)
    last = jnp.sum(la, axis=0, keepdims=True)
    q_dec = q * (GLA_DK ** -0.5) * jnp.exp(c)
    k_inv = k * jnp.exp(-c)
    k_end = k * jnp.exp(last - c)
    return c, last, q_dec, k_inv, k_end, tri


def _dot(a, b, ca, cb):
    return lax.dot_general(a.astype(BF16), b.astype(BF16), (((ca,), (cb,)), ((), ())), preferred_element_type=F32)


def gla_fwd(proj, la, hn, jobs=None):
    s = proj.shape[0]
    n_chunks = s // GLA_CHUNK
    rows = GLA_CHUNK * GLA_STEP_CHUNKS

    def body(q_ref, k_ref, v_ref, r_ref, la_ref, hn_ref, o_ref, st_out, og_ref, st):
        @pl.when(pl.program_id(0) == 0)
        def _():
            st[...] = jnp.zeros_like(st)

        for h in range(GLA_HEADS):
            hk = slice(h * GLA_DK, (h + 1) * GLA_DK)
            hv = slice(h * GLA_DV, (h + 1) * GLA_DV)
            for cc in range(GLA_STEP_CHUNKS):
                rs = slice(cc * GLA_CHUNK, (cc + 1) * GLA_CHUNK)
                _, last, q_dec, k_inv, k_end, tri = _chunk_terms(q_ref[rs, hk], k_ref[rs, hk], la_ref[rs, hk])
                v = v_ref[rs, hv]
                a = jnp.where(tri, _dot(q_dec, k_inv, 1, 1), 0.0)
                state = st[h]
                st_out[h, cc] = state
                ov = _dot(a, v, 1, 0) + _dot(q_dec, state, 1, 1)
                o_ref[rs, hv] = ov
                st[h] = state * jnp.exp(last) + _dot(v, k_end, 0, 0)
                oh = ov * lax.rsqrt(jnp.mean(ov * ov, axis=-1, keepdims=True) + EPS)
                r = r_ref[rs, hv]
                og_ref[rs, hv] = (oh * hn_ref[...] * (r * jax.nn.sigmoid(r))).astype(BF16)

    key = lambda col: pl.BlockSpec((rows, GLA_KEY_DIM), lambda n: (n, col))
    val = lambda col: pl.BlockSpec((rows, GLA_VAL_DIM), lambda n: (n, col))
    outs, bufs = _call(
        body, name="gla_fwd", jobs=jobs,
        out_shape=[jax.ShapeDtypeStruct((s, GLA_VAL_DIM), F32),
                   jax.ShapeDtypeStruct((GLA_HEADS, n_chunks, GLA_DV, GLA_DK), F32),
                   jax.ShapeDtypeStruct((s, GLA_VAL_DIM), BF16)],
        grid=(n_chunks // GLA_STEP_CHUNKS,),
        in_specs=[key(0), key(1), val(1), val(R_BLOCK // GLA_HEADS), key(0), pl.BlockSpec((1, GLA_DV), lambda n: (0, 0))],
        out_specs=[val(0), pl.BlockSpec((GLA_HEADS, GLA_STEP_CHUNKS, GLA_DV, GLA_DK), lambda n: (0, n, 0, 0)), val(0)],
        scratch_shapes=[pltpu.VMEM((GLA_HEADS, GLA_DV, GLA_DK), F32)],
        args=[proj, proj, proj, proj, la, hn], sem=("arbitrary",))
    return outs if jobs is None else (outs, bufs)


def gla_bwd(proj, la, states, o, hn, dog, jobs=None):
    s = proj.shape[0]
    n_steps = s // GLA_CHUNK // GLA_STEP_CHUNKS
    lastc = n_steps - 1
    rows = GLA_CHUNK * GLA_STEP_CHUNKS

    def body(q_ref, k_ref, v_ref, r_ref, la_ref, o_ref, hn_ref, dog_ref, st_ref,
             dq_ref, dk_ref, dv_ref, dr_ref, dla_ref, dhn_ref, dst):
        @pl.when(pl.program_id(0) == 0)
        def _():
            dst[...] = jnp.zeros_like(dst)
            dhn_ref[...] = jnp.zeros_like(dhn_ref)

        upper = (lax.broadcasted_iota(jnp.int32, (GLA_CHUNK, GLA_CHUNK), 0)
                 <= lax.broadcasted_iota(jnp.int32, (GLA_CHUNK, GLA_CHUNK), 1))
        gain = hn_ref[...]
        for h in range(GLA_HEADS):
            hk = slice(h * GLA_DK, (h + 1) * GLA_DK)
            hv = slice(h * GLA_DV, (h + 1) * GLA_DV)
            for cc in reversed(range(GLA_STEP_CHUNKS)):
                rs = slice(cc * GLA_CHUNK, (cc + 1) * GLA_CHUNK)
                ov = o_ref[rs, hv]
                inv = lax.rsqrt(jnp.mean(ov * ov, axis=-1, keepdims=True) + EPS)
                oh = ov * inv
                r = r_ref[rs, hv]
                sig = jax.nn.sigmoid(r)
                dgv = dog_ref[rs, hv]
                d_on = dgv * (r * sig)
                dr_ref[rs, hv] = (dgv * (oh * gain) * (sig * (1.0 + r * (1.0 - sig)))).astype(BF16)
                dhn_ref[...] += jnp.sum(d_on * oh, axis=0, keepdims=True)
                doh = d_on * gain
                dout = inv * (doh - oh * jnp.mean(doh * oh, axis=-1, keepdims=True))
                c, last, q_dec, k_inv, k_end, tri = _chunk_terms(q_ref[rs, hk], k_ref[rs, hk], la_ref[rs, hk])
                v = v_ref[rs, hv]
                state = st_ref[h, cc]
                dstate = dst[h]
                e_last = jnp.exp(last)
                a = jnp.where(tri, _dot(q_dec, k_inv, 1, 1), 0.0)
                da = jnp.where(tri, _dot(dout, v, 1, 1), 0.0)
                dv_ref[rs, hv] = (_dot(a, dout, 0, 0) + _dot(k_end, dstate, 1, 1)).astype(BF16)
                dq_dec = _dot(da, k_inv, 1, 0) + _dot(dout, state, 1, 0)
                dk_inv = _dot(da, q_dec, 0, 0)
                dk_end = _dot(v, dstate, 1, 0)
                dst[h] = dstate * e_last + _dot(dout, q_dec, 0, 0)
                dq_ref[rs, hk] = (dq_dec * (GLA_DK ** -0.5) * jnp.exp(c)).astype(BF16)
                dk_ref[rs, hk] = (dk_inv * jnp.exp(-c) + dk_end * jnp.exp(last - c)).astype(BF16)
                ke_term = dk_end * k_end
                dc = dq_dec * q_dec - dk_inv * k_inv - ke_term
                dlast = (jnp.sum(ke_term, axis=0, keepdims=True)
                         + e_last * jnp.sum(dstate * state, axis=0, keepdims=True))
                dla_ref[rs, hk] = _masked_sum(upper, dc) + dlast

    key = lambda col: pl.BlockSpec((rows, GLA_KEY_DIM), lambda n: (lastc - n, col))
    val = lambda col: pl.BlockSpec((rows, GLA_VAL_DIM), lambda n: (lastc - n, col))
    vec = pl.BlockSpec((1, GLA_DV), lambda n: (0, 0))
    outs, bufs = _call(
        body, name="gla_bwd", jobs=jobs,
        out_shape=[jax.ShapeDtypeStruct((s, GLA_KEY_DIM), BF16), jax.ShapeDtypeStruct((s, GLA_KEY_DIM), BF16),
                   jax.ShapeDtypeStruct((s, GLA_VAL_DIM), BF16), jax.ShapeDtypeStruct((s, GLA_VAL_DIM), BF16),
                   jax.ShapeDtypeStruct((s, GLA_KEY_DIM), F32), jax.ShapeDtypeStruct((1, GLA_DV), F32)],
        grid=(n_steps,),
        in_specs=[key(0), key(1), val(1), val(R_BLOCK // GLA_HEADS), key(0), val(0), vec, val(0),
                  pl.BlockSpec((GLA_HEADS, GLA_STEP_CHUNKS, GLA_DV, GLA_DK), lambda n: (0, lastc - n, 0, 0))],
        out_specs=[key(0), key(0), val(0), val(0), key(0), vec],
        scratch_shapes=[pltpu.VMEM((GLA_HEADS, GLA_DV, GLA_DK), F32)],
        args=[proj, proj, proj, proj, la, o, hn, dog, states], sem=("arbitrary",))
    return outs if jobs is None else (outs, bufs)


R_BLOCK = (2 * GLA_KEY_DIM + GLA_VAL_DIM) // GLA_DV


CONV_TC = 128
SQRT_HALF = 0.7071067811865476
INV_SQRT_2PI = 0.3989422804014327


def _conv_gate(g_ref, cw_ref, cb_ref):
    g0 = g_ref[...].astype(F32)
    t = lax.broadcasted_iota(jnp.int32, g0.shape, 0)
    g1 = jnp.where(t >= 1, pltpu.roll(g0, 1, 0), 0.0)
    g2 = jnp.where(t >= 2, pltpu.roll(g0, 2, 0), 0.0)
    gc = cw_ref[0:1, :] * g2 + cw_ref[1:2, :] * g1 + cw_ref[2:3, :] * g0 + cb_ref[...]
    return g0, g1, g2, gc, t


def convglu_fwd(up, conv_w, conv_b, *, name, jobs=None):
    s = up.shape[0]
    nc = D_FF // CONV_TC

    def body(u_ref, g_ref, cw_ref, cb_ref, o_ref):
        _, _, _, gc, _ = _conv_gate(g_ref, cw_ref, cb_ref)
        gelu = 0.5 * gc * (1.0 + lax.erf(gc * SQRT_HALF))
        o_ref[...] = (gelu * u_ref[...].astype(F32)).astype(BF16)

    (out,), bufs = _call(
        body, name=name, jobs=jobs,
        out_shape=[jax.ShapeDtypeStruct((s, D_FF), BF16)],
        grid=(nc,),
        in_specs=[pl.BlockSpec((s, CONV_TC), lambda c: (0, c)),
                  pl.BlockSpec((s, CONV_TC), lambda c: (0, nc + c)),
                  pl.BlockSpec((3, CONV_TC), lambda c: (0, c)),
                  pl.BlockSpec((1, CONV_TC), lambda c: (0, c))],
        out_specs=[pl.BlockSpec((s, CONV_TC), lambda c: (0, c))],
        args=[up, up, conv_w, conv_b], sem=("parallel",))
    return out if jobs is None else (out, bufs)


def convglu_bwd(up, conv_w, conv_b, dact, *, name, jobs=None):
    s = up.shape[0]
    nc = D_FF // CONV_TC

    def body(u_ref, g_ref, cw_ref, cb_ref, da_ref, dup_ref, dcw_ref, dcb_ref):
        du_ref, dg_ref = dup_ref.at[0], dup_ref.at[1]
        g0, g1, g2, gc, t = _conv_gate(g_ref, cw_ref, cb_ref)
        cdf = 0.5 * (1.0 + lax.erf(gc * SQRT_HALF))
        da = da_ref[...].astype(F32)
        du_ref[...] = (da * gc * cdf).astype(BF16)
        dgc = da * u_ref[...].astype(F32) * (cdf + gc * jnp.exp(-0.5 * gc * gc) * INV_SQRT_2PI)
        dcb_ref[...] = jnp.sum(dgc, axis=0, keepdims=True)
        dcw_ref[0:1, :] = jnp.sum(dgc * g2, axis=0, keepdims=True)
        dcw_ref[1:2, :] = jnp.sum(dgc * g1, axis=0, keepdims=True)
        dcw_ref[2:3, :] = jnp.sum(dgc * g0, axis=0, keepdims=True)
        n1 = jnp.where(t < s - 1, pltpu.roll(dgc, s - 1, 0), 0.0)
        n2 = jnp.where(t < s - 2, pltpu.roll(dgc, s - 2, 0), 0.0)
        dg_ref[...] = (cw_ref[2:3, :] * dgc + cw_ref[1:2, :] * n1 + cw_ref[0:1, :] * n2).astype(BF16)

    col = pl.BlockSpec((s, CONV_TC), lambda c: (0, c))
    outs, bufs = _call(
        body, name=name, jobs=jobs,
        out_shape=[jax.ShapeDtypeStruct((2, s, D_FF), BF16),
                   jax.ShapeDtypeStruct((3, D_FF), F32), jax.ShapeDtypeStruct((1, D_FF), F32)],
        grid=(nc,),
        in_specs=[col, pl.BlockSpec((s, CONV_TC), lambda c: (0, nc + c)),
                  pl.BlockSpec((3, CONV_TC), lambda c: (0, c)),
                  pl.BlockSpec((1, CONV_TC), lambda c: (0, c)), col],
        out_specs=[pl.BlockSpec((2, s, CONV_TC), lambda c: (0, 0, c)), pl.BlockSpec((3, CONV_TC), lambda c: (0, c)),
                   pl.BlockSpec((1, CONV_TC), lambda c: (0, c))],
        args=[up, up, conv_w, conv_b, dact], sem=("parallel",))
    return outs if jobs is None else (outs, bufs)


SLOPE_TILE = (8, LANE)


def _slope_table():
    return jnp.broadcast_to(jnp.asarray(ALIBI_SLOPES, F32)[:, None, None], (ATT_HEADS,) + SLOPE_TILE)


def _pieces(s_len, d):
    npc = STREAMS // d
    lp = ATT_BLOCK // npc
    return npc, lp, (s_len // STREAMS) // lp


def _gather(ref, r, b, d, s_len):
    npc, lp, _ = _pieces(s_len, d)
    per = s_len // STREAMS
    parts = [ref[pl.ds((r + d * k) * per + b * lp, lp), :] for k in range(npc)]
    return parts[0] if npc == 1 else jnp.concatenate(parts, axis=0)


def _scatter(ref, r, b, d, s_len, val, add=False):
    npc, lp, _ = _pieces(s_len, d)
    per = s_len // STREAMS
    for k in range(npc):
        rows = pl.ds((r + d * k) * per + b * lp, lp)
        piece = val[k * lp:(k + 1) * lp]
        if add:
            ref[rows, :] += piece
        else:
            ref[rows, :] = piece


def _stream_bias(slope, d, s_len):
    npc, lp, _ = _pieces(s_len, d)
    qi = lax.broadcasted_iota(jnp.int32, (ATT_BLOCK, 2 * ATT_BLOCK), 0)
    c = lax.broadcasted_iota(jnp.int32, (ATT_BLOCK, 2 * ATT_BLOCK), 1)
    own = c // ATT_BLOCK
    cc = c - own * ATT_BLOCK
    dist = npc * ((qi % lp) - (cc % lp) + lp * (1 - own)) + (qi // lp - cc // lp)
    ok = (dist >= 0) & (dist <= ATT_BLOCK)
    return jnp.where(ok, (slope * (-float(d))) * dist.astype(F32), NEG)


def attn_fwd(q, kv, jobs=None):
    s_len = q.shape[0]
    scale = HEAD_DIM ** -0.5

    def body(sl_ref, q_ref, k_ref, v_ref, o_ref, lse_ref):
        g = pl.program_id(1)
        slope = sl_ref[0:1, 0:1]

        def branch(gi, d):
            _, _, nblk = _pieces(s_len, d)
            bias = _stream_bias(slope, d, s_len)
            for r in range(d):
                for b in range(nblk):
                    qb = _gather(q_ref, r, b, d, s_len)
                    kc, vc = _gather(k_ref, r, b, d, s_len), _gather(v_ref, r, b, d, s_len)
                    if b == 0:
                        kcat, vcat, bb = kc, vc, bias[:, ATT_BLOCK:]
                    else:
                        kcat = jnp.concatenate([_gather(k_ref, r, b - 1, d, s_len), kc], axis=0)
                        vcat = jnp.concatenate([_gather(v_ref, r, b - 1, d, s_len), vc], axis=0)
                        bb = bias
                    sc = _dot(qb, kcat, 1, 1) * scale + bb
                    m = jnp.max(sc, axis=-1, keepdims=True)
                    p = jnp.exp(sc - m)
                    l = jnp.sum(p, axis=-1, keepdims=True)
                    o_new = _dot(p, vcat, 1, 0) / l
                    lse_new = m + jnp.log(l)
                    if gi > 0:
                        lse_old = _gather(lse_ref, r, b, d, s_len)[:, 0:1]
                        top = jnp.maximum(lse_old, lse_new)
                        e_old, e_new = jnp.exp(lse_old - top), jnp.exp(lse_new - top)
                        den = e_old + e_new
                        o_new = (e_old * _gather(o_ref, r, b, d, s_len) + e_new * o_new) / den
                        lse_new = top + jnp.log(den)
                    _scatter(o_ref, r, b, d, s_len, o_new)
                    _scatter(lse_ref, r, b, d, s_len, jnp.broadcast_to(lse_new, (ATT_BLOCK, HEAD_DIM)))

        for gi, d in enumerate(DILATIONS):
            @pl.when(g == gi)
            def _():
                branch(gi, d)

    blk = lambda col: pl.BlockSpec((s_len, HEAD_DIM), lambda h, g: (0, col(h, g)))
    head = lambda h, g: h
    outs, bufs = _call(
        body, name="attn_fwd", jobs=jobs,
        out_shape=[jax.ShapeDtypeStruct((s_len, ATT_HEADS * HEAD_DIM), F32)] * 2,
        grid=(ATT_HEADS, len(DILATIONS)),
        in_specs=[pl.BlockSpec((None,) + SLOPE_TILE, lambda h, g: (h, 0, 0)),
                  blk(lambda h, g: g * ATT_HEADS + h), blk(head), blk(lambda h, g: ATT_HEADS + h)],
        out_specs=[blk(head), blk(head)],
        args=[_slope_table(), q, kv, kv], sem=("parallel", "arbitrary"))
    return outs if jobs is None else (outs, bufs)


def attn_bwd(q, kv, o, lse, do, jobs=None):
    s_len = q.shape[0]
    scale = HEAD_DIM ** -0.5
    chunks = s_len // ATT_BLOCK

    def body(sl_ref, q_ref, k_ref, v_ref, o_ref, lse_ref, do_ref, dq_ref, dkv_ref, dlt):
        g = pl.program_id(1)
        slope = sl_ref[0:1, 0:1]
        dk_ref, dv_ref = dkv_ref.at[0], dkv_ref.at[1]

        @pl.when(g == 0)
        def _():
            dkv_ref[...] = jnp.zeros_like(dkv_ref)

            def deltas(c, carry):
                rows = pl.ds(pl.multiple_of(c * ATT_BLOCK, ATT_BLOCK), ATT_BLOCK)
                dlt[rows, :] = jnp.sum(do_ref[rows, :] * o_ref[rows, :], axis=-1, keepdims=True)
                return carry
            lax.fori_loop(0, chunks, deltas, 0)

        def branch(d):
            _, _, nblk = _pieces(s_len, d)
            bias = _stream_bias(slope, d, s_len)
            for r in range(d):
                for b in range(nblk):
                    qb = _gather(q_ref, r, b, d, s_len)
                    dob = _gather(do_ref, r, b, d, s_len)
                    kc, vc = _gather(k_ref, r, b, d, s_len), _gather(v_ref, r, b, d, s_len)
                    if b == 0:
                        kcat, vcat, bb = kc, vc, bias[:, ATT_BLOCK:]
                    else:
                        kcat = jnp.concatenate([_gather(k_ref, r, b - 1, d, s_len), kc], axis=0)
                        vcat = jnp.concatenate([_gather(v_ref, r, b - 1, d, s_len), vc], axis=0)
                        bb = bias
                    sc = _dot(qb, kcat, 1, 1) * scale + bb
                    p = jnp.exp(sc - _gather(lse_ref, r, b, d, s_len)[:, 0:1])
                    ds = p * (_dot(dob, vcat, 1, 1) - _gather(dlt, r, b, d, s_len))
                    _scatter(dq_ref, r, b, d, s_len, _dot(ds, kcat, 1, 0) * scale)
                    dk = _dot(ds, qb, 0, 0) * scale
                    dv = _dot(p, dob, 0, 0)
                    if b == 0:
                        _scatter(dk_ref, r, b, d, s_len, dk, add=True)
                        _scatter(dv_ref, r, b, d, s_len, dv, add=True)
                    else:
                        _scatter(dk_ref, r, b - 1, d, s_len, dk[:ATT_BLOCK], add=True)
                        _scatter(dv_ref, r, b - 1, d, s_len, dv[:ATT_BLOCK], add=True)
                        _scatter(dk_ref, r, b, d, s_len, dk[ATT_BLOCK:], add=True)
                        _scatter(dv_ref, r, b, d, s_len, dv[ATT_BLOCK:], add=True)

        for gi, d in enumerate(DILATIONS):
            @pl.when(g == gi)
            def _():
                branch(d)

    blk = lambda col: pl.BlockSpec((s_len, HEAD_DIM), lambda h, g: (0, col(h, g)))
    head = lambda h, g: h
    q_col = lambda h, g: g * ATT_HEADS + h
    outs, bufs = _call(
        body, name="attn_bwd", jobs=jobs,
        out_shape=[jax.ShapeDtypeStruct(q.shape, F32), jax.ShapeDtypeStruct((2, s_len, ATT_HEADS * HEAD_DIM), F32)],
        grid=(ATT_HEADS, len(DILATIONS)),
        in_specs=[pl.BlockSpec((None,) + SLOPE_TILE, lambda h, g: (h, 0, 0)),
                  blk(q_col), blk(head), blk(lambda h, g: ATT_HEADS + h), blk(head), blk(head), blk(head)],
        out_specs=[blk(q_col), pl.BlockSpec((2, s_len, HEAD_DIM), lambda h, g: (0, 0, h))],
        scratch_shapes=[pltpu.VMEM((s_len, 1), F32)],
        args=[_slope_table(), q, kv, kv, o, lse, do], sem=("parallel", "arbitrary"))
    return outs if jobs is None else (outs, bufs)


def _adam(w, g, m, v):
    m = ADAM_B1 * m + (1.0 - ADAM_B1) * g
    v = ADAM_B2 * v + (1.0 - ADAM_B2) * (g * g)
    m_hat = m / (1.0 - ADAM_B1 ** ADAM_STEP)
    v_hat = v / (1.0 - ADAM_B2 ** ADAM_STEP)
    delta = -ADAM_LR * (m_hat / (jnp.sqrt(v_hat) + ADAM_EPS) + ADAM_WD * w)
    return delta, m, v


def adam_sharded(recvs, w, m, v, *, name):
    layers = len(recvs)
    n_src, r, c = recvs[0].shape
    tr = _rows(r, c)

    def body(*refs):
        p_refs = refs[:layers]
        w_ref, m_ref, v_ref, g_ref, d_ref, mo_ref, vo_ref = refs[layers:]
        for layer, p_ref in enumerate(p_refs):
            @pl.when(pl.program_id(0) == layer)
            def _():
                g = p_ref[0].astype(F32)
                for src in range(1, n_src):
                    g = g + p_ref[src].astype(F32)
                delta, m_new, v_new = _adam(w_ref[...], g, m_ref[...], v_ref[...])
                g_ref[...] = g
                d_ref[...] = delta
                mo_ref[...] = m_new
                vo_ref[...] = v_new

    blk = pl.BlockSpec((None, tr, c), lambda l, i: (l, i, 0))
    out = jax.ShapeDtypeStruct((layers, r, c), F32)
    part = [pl.BlockSpec((n_src, tr, c), functools.partial(lambda l, i, layer: (0, jnp.where(l == layer, i, 0), 0),
                                                            layer=layer)) for layer in range(layers)]
    return pl.pallas_call(
        body,
        name=name,
        out_shape=[out] * 4,
        grid=(layers, r // tr),
        in_specs=part + [blk, blk, blk],
        out_specs=[blk] * 4,
        compiler_params=_params("parallel", "parallel"),
    )(*recvs, w, m, v)


def sum_partials(parts):
    n_src, r, c = parts.shape

    def body(p_ref, o_ref):
        g = p_ref[0]
        for src in range(1, n_src):
            g = g + p_ref[src]
        o_ref[...] = g

    return pl.pallas_call(
        body,
        name="sum_small_grads",
        out_shape=jax.ShapeDtypeStruct((r, c), F32),
    )(parts)


def adam_packed(w, g, m, v):
    def body(w_ref, g_ref, m_ref, v_ref, d_ref, mo_ref, vo_ref):
        delta, m_new, v_new = _adam(w_ref[...], g_ref[...], m_ref[...], v_ref[...])
        d_ref[...] = delta
        mo_ref[...] = m_new
        vo_ref[...] = v_new

    out = jax.ShapeDtypeStruct(w.shape, F32)
    return pl.pallas_call(body, name="adam_small", out_shape=[out] * 3)(w, g, m, v)


def all_gather(srcs, *, name):
    n = len(srcs)

    def body(*refs):
        src, dst = refs[:n], refs[n:2 * n]
        send_sems, recv_sems, local_sems = refs[2 * n:]
        x, y, c, me = _place()
        sibling = (x, y, 1 - c)
        chips = [(1 - x, y), (x, 1 - y), (1 - x, 1 - y)]

        def index(px, py, pc):
            return 4 * px + 2 * py + pc

        def copy(p, k, block, to, from_src=False):
            slot = dst[p].at[index(*block)]
            return pltpu.make_async_remote_copy(
                src_ref=src[p] if from_src else slot, dst_ref=slot,
                send_sem=send_sems.at[p, k], recv_sem=recv_sems.at[p, k],
                device_id=to, device_id_type=MESH)

        mine = [pltpu.make_async_copy(src[p], dst[p].at[me], local_sems.at[p]) for p in range(n)]
        for cp in mine:
            cp.start()
        first = []
        for p in range(n):
            first.append(copy(p, 0, (x, y, c), sibling, from_src=True))
            for jj, chip in enumerate(chips):
                first.append(copy(p, 1 + jj, (x, y, c), (*chip, c), from_src=True))
        for cp in first:
            cp.start()
        passed = []
        for jj, chip in enumerate(chips):
            for p in range(n):
                copy(p, 1 + jj, (*chip, c), (x, y, c)).wait_recv()
                fwd = copy(p, 4 + jj, (*chip, c), sibling)
                fwd.start()
                passed.append(fwd)
        for p in range(n):
            copy(p, 0, sibling, (x, y, c)).wait_recv()
            for jj, chip in enumerate(chips):
                copy(p, 4 + jj, (*chip, 1 - c), (x, y, c)).wait_recv()
        for cp in first + passed:
            cp.wait_send()
        for cp in mine:
            cp.wait()

    return pl.pallas_call(
        body,
        name=name,
        out_shape=[jax.ShapeDtypeStruct((N_DEV,) + a.shape, a.dtype) for a in srcs],
        in_specs=[ANY] * n,
        out_specs=[ANY] * n,
        scratch_shapes=[pltpu.SemaphoreType.DMA((n, 7)), pltpu.SemaphoreType.DMA((n, 7)),
                        pltpu.SemaphoreType.DMA((n,))],
    )(*srcs)


def exchange_only(*, name, jobs):
    def body(o_ref):
        o_ref[...] = jnp.zeros_like(o_ref)

    _, bufs = _call(body, name=name, jobs=jobs, out_shape=[jax.ShapeDtypeStruct((8, LANE), F32)], grid=(1,),
                    in_specs=[], out_specs=[pl.BlockSpec((8, LANE), lambda i: (0, 0))], args=[], sem=("arbitrary",))
    return None, bufs


def _pack_rows(parts, rows):
    flat = jnp.concatenate([p.reshape(-1) for p in parts])
    return jnp.pad(flat, (0, rows * LANE - flat.shape[0])).reshape(rows, LANE)


def _unpack_rows(packed, shapes):
    flat = packed.reshape(-1)
    out, at = [], 0
    for sh in shapes:
        size = 1
        for dim in sh:
            size *= dim
        out.append(flat[at:at + size].reshape(sh))
        at += size
    return out


CONV_W_PAD = 768
SMALL_W_ROWS = 56


def _pack_small_weights(w_a2, b_a2, hn, conv_w):
    cw = jnp.pad(conv_w.reshape(6, -1), ((0, 0), (0, CONV_W_PAD - conv_w.shape[-1]))).reshape(-1, LANE)
    rows = jnp.concatenate([w_a2[0], b_a2, jnp.pad(hn, ((0, 0), (0, LANE - hn.shape[-1]))), cw], axis=0)
    return jnp.pad(rows, ((0, SMALL_W_ROWS - rows.shape[0]), (0, 0)))


def _unpack_small_weights(gathered):
    w_a2 = gathered[:, 0:GATE_RANK, :].transpose(1, 0, 2).reshape(GATE_RANK, GLA_KEY_DIM)
    b_a2 = gathered[:, GATE_RANK, :].reshape(1, GLA_KEY_DIM)
    hn = gathered[:, GATE_RANK + 1, :GLA_DV // N_DEV].reshape(1, GLA_DV)
    per = D_FF // N_DEV
    cw = gathered[:, GATE_RANK + 2:GATE_RANK + 2 + 6 * CONV_W_PAD // LANE, :].reshape(N_DEV, 6, CONV_W_PAD)[:, :, :per]
    cw = cw.reshape(N_DEV, 2, 3, per).transpose(1, 2, 0, 3).reshape(2, 3, D_FF)
    return w_a2, b_a2, hn, cw


SCHEDULE = {
    "gla_in": [("g1", "gout", None), ("g1", "up0", (0, 1024))],
    "gla_fwd": [("g2", "gout", None), ("g2", "up0", (0, 1024)), ("g1", "up0", (1024, 2048))],
    "gla_out": [("g2", "up0", (1024, 2048)), ("g1", "dn0", (0, 352))],
    "ffn_up0": [("g2", "dn0", (0, 352)), ("g1", "dn0", (352, 704)), ("g1", "kv", None), ("g1", "q", (0, 768))],
    "convglu_fwd0": [("g2", "dn0", (352, 704))],
    "ffn_down0": [("g2", "kv", None), ("g2", "q", (0, 768)), ("g1", "q", (768, 2048)), ("g1", "dout", None)],
    "kv_proj": [("g2", "q", (768, 2048)), ("g2", "dout", None), ("g1", "up1", (0, 704))],
    "q_proj": [("g2", "up1", (0, 704)), ("g1", "up1", (704, 1664))],
    "attn_fwd": [("g2", "up1", (704, 1664)), ("g1", "up1", (1664, 2048)), ("g1", "dn1", None)],
    "dsa_out": [("g2", "up1", (1664, 2048)), ("g2", "dn1", None)],
    "ffn_down_dx1": [("sc", "dn1", (0, 352))],
    "convglu_bwd1": [("sc", "dn1", (352, 704))],
    "ffn_up_dx1": [("sc", "up1", (0, 1024))],
    "attn_bwd": [("sc", "up1", (1024, 2048)), ("sc", "dout", None)],
    "q_proj_dx": [("sc", "q", (0, 1024))],
    "kv_proj_dw": [("sc", "q", (1024, 1792))],
    "kv_proj_dx": [("sc", "q", (1792, 2048)), ("sc", "kv", (0, 768))],
    "ffn_down_dw0": [("sc", "kv", (768, 2048))],
    "ffn_down_dx0": [("sc", "dn0", (0, 384))],
    "convglu_bwd0": [("sc", "dn0", (384, 704))],
    "ffn_up_dx0": [("sc", "up0", (0, 1024))],
    "gla_out_dw": [("sc", "up0", (1024, 1216))],
    "gla_out_dx": [("sc", "up0", (1216, 1408))],
    "gla_bwd": [("sc", "up0", (1408, 2048))],
    "gla_in_dw": [("sc", "gout", None)],
    "gla_in_dx": [("sc", "in", (0, 1792))],
    "grads_tail": [("sc", "in", (1792, 2048)), ("all", "small", None)],
}
ROW_SHARDED = ("gout", "dout", "dn0", "dn1")


class Plan:
    def __init__(self, weights, srcs=None):
        self.w = dict(weights)
        self.srcs = srcs
        self.grads = {}
        self.recv = {}
        self._names = None

    def weight(self, name):
        buf = self.w[name]
        if name in ROW_SHARDED:
            return buf.reshape(1, buf.shape[0] * buf.shape[1], buf.shape[2])
        return buf

    def jobs(self, call):
        ops = SCHEDULE.get(call)
        if self.srcs is None or not ops:
            return None
        jobs, handles = Jobs(), {}
        backward = ops[0][0] in ("sc", "all")
        for op, name, rows in ops:
            assert (op in ("sc", "all")) == backward
            store = self.recv if backward else self.w
            if name not in handles:
                if name in store:
                    handles[name] = jobs.thru(store[name])
                elif op == "sc":
                    handles[name] = jobs.new(self.grads[name].shape, BF16)
                elif op == "all":
                    handles[name] = jobs.new((N_DEV,) + self.grads[name].shape, self.grads[name].dtype)
                else:
                    handles[name] = jobs.new((N_DEV,) + self.srcs[name].shape, BF16)
            if op == "g1":
                jobs.gather_ici(self.srcs[name], handles[name], rows)
            elif op == "g2":
                jobs.gather_d2d(handles[name], rows)
            else:
                jobs.scatter(self.grads[name], handles[name], rows, same=op == "all")
        self._names = [(name, self.recv if backward else self.w) for name in handles]
        return jobs

    def run(self, call, fn, *args, **kwargs):
        jobs = self.jobs(call)
        if jobs is None:
            return fn(*args, **kwargs)
        out, bufs = fn(*args, jobs=jobs, **kwargs)
        for (name, store), buf in zip(self._names, bufs):
            store[name] = buf
        return out


def _ffn_fwd(plan, h, norm_g, conv_w, conv_b, tag):
    (n,) = rms_fwd(h, [norm_g], name=f"ffn_norm_fwd{tag}")
    up = plan.run(f"ffn_up{tag}", mm_nn, n, plan.weight(f"up{tag}"), out_dtype=BF16, name=f"ffn_up{tag}")
    act = plan.run(f"convglu_fwd{tag}", convglu_fwd, up, conv_w, conv_b, name=f"convglu_fwd{tag}")
    h_out = plan.run(f"ffn_down{tag}", mm_nn, act, plan.weight(f"dn{tag}"), out_dtype=F32, res=h,
                     name=f"ffn_down{tag}")
    return h_out, (n, up, act)


def _by_rows(dw):
    return dw.reshape(N_DEV, dw.shape[1] // N_DEV, dw.shape[2])


def _ffn_bwd(plan, dh_out, h, saved, norm_g, conv_w, conv_b, tag):
    n, up, act = saved
    plan.grads[f"dn{tag}"] = _by_rows(plan.run(f"ffn_down_dw{tag}", mm_tn, act, dh_out, 1, name=f"ffn_down_dw{tag}"))
    dact = plan.run(f"ffn_down_dx{tag}", mm_nt, dh_out, plan.weight(f"dn{tag}"), out_dtype=BF16,
                    name=f"ffn_down_dx{tag}")
    dup, dconv_w, dconv_b = plan.run(f"convglu_bwd{tag}", convglu_bwd, up, conv_w, conv_b, dact,
                                     name=f"convglu_bwd{tag}")
    plan.grads[f"up{tag}"] = mm_tn(n, dup, N_DEV, name=f"ffn_up_dw{tag}")
    dh, (dnorm,) = plan.run(f"ffn_up_dx{tag}", mm_nt, dup, plan.weight(f"up{tag}"), out_dtype=F32,
                            name=f"ffn_up_dx{tag}", norm=(h, dh_out, [norm_g], []))
    return dh, dnorm, dconv_w, dconv_b


def local_step(x, target, wts, plan):
    row = lambda v: v.reshape(1, -1)
    attn_norm, ffn_norm = wts["attn_norm"], wts["ffn_norm"]
    conv_w, conv_b = wts["ffn_conv_w"], wts["ffn_conv_b"]

    (n1,) = rms_fwd(x, [row(attn_norm[0])], name="attn_norm_fwd0")
    proj = plan.run("gla_in", mm_nn, n1, wts["gla_w_in"], out_dtype=F32, name="gla_in")
    la = gate_fwd(proj, wts["gla_w_a2"], wts["gla_b_a2"])
    o_gla, states, og = plan.run("gla_fwd", gla_fwd, proj, la, wts["gla_head_norm"])
    h1 = plan.run("gla_out", mm_nn, og, plan.weight("gout"), out_dtype=F32, res=x, name="gla_out")
    h2, ffn0 = _ffn_fwd(plan, h1, row(ffn_norm[0]), conv_w[0], row(conv_b[0]), "0")

    h2s = to_streams(h2, name="h2_to_streams")
    kvn, n3 = rms_fwd(h2s, [row(wts["kv_norm"]), row(attn_norm[1])], name="kv_attn_norm_fwd")
    kv = plan.run("kv_proj", mm_nn, kvn, plan.weight("kv"), out_dtype=BF16, name="kv_proj")
    q = plan.run("q_proj", mm_nn, n3, plan.weight("q"), out_dtype=BF16, name="q_proj")
    o_att, lse = plan.run("attn_fwd", attn_fwd, q, kv)
    h3 = from_streams(plan.run("dsa_out", mm_nn, o_att, plan.weight("dout"), out_dtype=F32, res=h2s, name="dsa_out"),
                      name="h3_from_streams")
    h4, ffn1 = _ffn_fwd(plan, h3, row(ffn_norm[1]), conv_w[1], row(conv_b[1]), "1")

    loss_tile, dh4, d_final = loss_head(h4, row(wts["final_norm"]), target)

    dh3, d_ffn1, dcw1, dcb1 = _ffn_bwd(plan, dh4, h3, ffn1, row(ffn_norm[1]), conv_w[1], row(conv_b[1]), "1")
    dh3s = to_streams(dh3, name="dh3_to_streams")
    plan.grads["dout"] = _by_rows(mm_tn(o_att, dh3s, 1, name="dsa_out_dw"))
    do_att = mm_nt(dh3s, plan.weight("dout"), out_dtype=F32, name="dsa_out_dx")
    dq, dkv = plan.run("attn_bwd", attn_bwd, q, kv, o_att, lse, do_att)
    plan.grads["q"] = mm_tn(n3, dq, N_DEV, name="q_proj_dw")
    dh2_part, (d_attn1,) = plan.run("q_proj_dx", mm_nt, dq, plan.weight("q"), out_dtype=F32, name="q_proj_dx",
                                    norm=(h2s, dh3s, [row(attn_norm[1])], []))
    plan.grads["kv"] = plan.run("kv_proj_dw", mm_tn, kvn, dkv, N_DEV, name="kv_proj_dw")
    dh2s, (d_kvnorm,) = plan.run("kv_proj_dx", mm_nt, dkv, plan.weight("kv"), out_dtype=F32, name="kv_proj_dx",
                                 norm=(h2s, dh2_part, [row(wts["kv_norm"])], []))
    dh2 = from_streams(dh2s, name="dh2_from_streams")
    dh1, d_ffn0, dcw0, dcb0 = _ffn_bwd(plan, dh2, h1, ffn0, row(ffn_norm[0]), conv_w[0], row(conv_b[0]), "0")
    plan.grads["gout"] = _by_rows(plan.run("gla_out_dw", mm_tn, og, dh1, 1, name="gla_out_dw"))
    dog = plan.run("gla_out_dx", mm_nt, dh1, plan.weight("gout"), out_dtype=F32, name="gla_out_dx")
    dq_g, dk_g, dv_g, dr, dla, d_hn = plan.run("gla_bwd", gla_bwd, proj, la, states, o_gla, wts["gla_head_norm"], dog)
    da, dw_a2p, db_a2 = gate_bwd(proj, wts["gla_w_a2"], wts["gla_b_a2"], dla)
    dproj = jnp.concatenate([dq_g, dk_g, dv_g, dr, da], axis=1)
    assert dproj.shape[1] == GLA_IN_PAD
    dw_in = plan.run("gla_in_dw", mm_tn, n1, dproj, 1, name="gla_in_dw")
    plan.grads["in"] = dw_in[0, :, :GLA_IN_DIM].reshape(D_MODEL, N_DEV, GLA_IN_DIM // N_DEV).transpose(1, 0, 2)
    grad_x, (d_attn0,) = plan.run("gla_in_dx", mm_nt, dproj, wts["gla_w_in"], out_dtype=F32, name="gla_in_dx",
                                  norm=(x, dh1, [row(attn_norm[0])], []))

    small = dict(
        attn_norm=jnp.concatenate([d_attn0, d_attn1], axis=0),
        ffn_norm=jnp.concatenate([d_ffn0, d_ffn1], axis=0),
        kv_norm=d_kvnorm.reshape(-1),
        final_norm=d_final.reshape(-1),
        ffn_conv_b=jnp.concatenate([dcb0, dcb1], axis=0),
        gla_w_a2=dw_a2p[:GATE_RANK],
        gla_b_a2=db_a2,
        gla_head_norm=d_hn,
        ffn_conv_w=jnp.stack([dcw0, dcw1]),
    )
    return loss_tile, grad_x, small


SMALL_ORDER = ("attn_norm", "ffn_norm", "kv_norm", "final_norm", "ffn_conv_b",
               "gla_w_a2", "gla_b_a2", "gla_head_norm", "ffn_conv_w")
SMALL_FULL = dict(attn_norm=(2, D_MODEL), ffn_norm=(2, D_MODEL), kv_norm=(D_MODEL,), final_norm=(D_MODEL,),
                  ffn_conv_b=(2, D_FF), gla_w_a2=(GATE_RANK, GLA_KEY_DIM), gla_b_a2=(1, GLA_KEY_DIM),
                  gla_head_norm=(1, GLA_DV), ffn_conv_w=(2, 3, D_FF))
SMALL_SHARDED = ("gla_w_a2", "gla_b_a2", "gla_head_norm", "ffn_conv_w")
SMALL_GRAD_ROWS = 592
SMALL_ADAM_ROWS = 240


def kernel(x, attn_norm, gla_w_in, gla_w_a2, gla_b_a2, gla_head_norm, gla_w_out, kv_norm, w_kv, dsa_w_q, dsa_w_out, ffn_norm, ffn_w_up, ffn_conv_w, ffn_conv_b, ffn_w_down, final_norm, loss_target, m_attn_norm, m_gla_w_in, m_gla_w_a2, m_gla_b_a2, m_gla_head_norm, m_gla_w_out, m_kv_norm, m_w_kv, m_dsa_w_q, m_dsa_w_out, m_ffn_norm, m_ffn_w_up, m_ffn_conv_w, m_ffn_conv_b, m_ffn_w_down, m_final_norm, v_attn_norm, v_gla_w_in, v_gla_w_a2, v_gla_b_a2, v_gla_head_norm, v_gla_w_out, v_kv_norm, v_w_kv, v_dsa_w_q, v_dsa_w_out, v_ffn_norm, v_ffn_w_up, v_ffn_conv_w, v_ffn_conv_b, v_ffn_w_down, v_final_norm):
    me = 4 * lax.axis_index("x") + 2 * lax.axis_index("y") + lax.axis_index("c")
    bf = lambda a: a.astype(BF16)

    g_in, g_small = all_gather([bf(gla_w_in[0]), _pack_small_weights(gla_w_a2, gla_b_a2, gla_head_norm, ffn_conv_w)],
                               name="gather_first")
    w_a2_full, b_a2_full, hn_full, conv_w_full = _unpack_small_weights(g_small)
    w_in_full = jnp.pad(g_in.transpose(1, 0, 2).reshape(D_MODEL, GLA_IN_DIM), ((0, 0), (0, GLA_IN_PAD - GLA_IN_DIM)))
    wts = dict(
        attn_norm=attn_norm, ffn_norm=ffn_norm, kv_norm=kv_norm, final_norm=final_norm, ffn_conv_b=ffn_conv_b,
        gla_w_in=w_in_full[None],
        gla_w_a2=jnp.pad(bf(w_a2_full), ((0, LANE - GATE_RANK), (0, 0))),
        gla_b_a2=b_a2_full, gla_head_norm=hn_full, ffn_conv_w=conv_w_full,
    )
    plan = Plan({}, srcs=dict(gout=bf(gla_w_out[0]), kv=bf(w_kv), q=bf(dsa_w_q[0]), dout=bf(dsa_w_out[0]),
                              up0=bf(ffn_w_up[0]), up1=bf(ffn_w_up[1]), dn0=bf(ffn_w_down[0]), dn1=bf(ffn_w_down[1])))

    loss_tile, grad_x, small = local_step(x[0], loss_target[0], wts, plan)
    loss = lax.psum(loss_tile[0, 0], ("x", "y", "c"))

    plan.grads["small"] = _pack_rows([small[nm] for nm in SMALL_ORDER], SMALL_GRAD_ROWS)
    plan.run("grads_tail", exchange_only, name="grads_tail")
    shard3 = lambda a: a.reshape((-1,) + a.shape[-2:])
    big_params = dict(gla_w_in=(("in",), gla_w_in, m_gla_w_in, v_gla_w_in),
                      gla_w_out=(("gout",), gla_w_out, m_gla_w_out, v_gla_w_out),
                      w_kv=(("kv",), w_kv, m_w_kv, v_w_kv),
                      dsa_w_q=(("q",), dsa_w_q, m_dsa_w_q, v_dsa_w_q),
                      dsa_w_out=(("dout",), dsa_w_out, m_dsa_w_out, v_dsa_w_out),
                      ffn_w_up=(("up0", "up1"), ffn_w_up, m_ffn_w_up, v_ffn_w_up),
                      ffn_w_down=(("dn0", "dn1"), ffn_w_down, m_ffn_w_down, v_ffn_w_down))
    res = {}
    for nm, (parts, w, m, v) in big_params.items():
        outs = adam_sharded([plan.recv[p] for p in parts], shard3(w), shard3(m), shard3(v), name=f"adam_{nm}")
        res[nm] = [o.reshape(w.shape) for o in outs]

    full = dict(zip(SMALL_ORDER, _unpack_rows(sum_partials(plan.recv["small"]),
                                              [SMALL_FULL[nm] for nm in SMALL_ORDER])))
    local_w = dict(attn_norm=attn_norm, ffn_norm=ffn_norm, kv_norm=kv_norm, final_norm=final_norm,
                   ffn_conv_b=ffn_conv_b, gla_w_a2=gla_w_a2, gla_b_a2=gla_b_a2, gla_head_norm=gla_head_norm,
                   ffn_conv_w=ffn_conv_w)
    local_m = dict(attn_norm=m_attn_norm, ffn_norm=m_ffn_norm, kv_norm=m_kv_norm, final_norm=m_final_norm,
                   ffn_conv_b=m_ffn_conv_b, gla_w_a2=m_gla_w_a2, gla_b_a2=m_gla_b_a2, gla_head_norm=m_gla_head_norm,
                   ffn_conv_w=m_ffn_conv_w)
    local_v = dict(attn_norm=v_attn_norm, ffn_norm=v_ffn_norm, kv_norm=v_kv_norm, final_norm=v_final_norm,
                   ffn_conv_b=v_ffn_conv_b, gla_w_a2=v_gla_w_a2, gla_b_a2=v_gla_b_a2, gla_head_norm=v_gla_head_norm,
                   ffn_conv_w=v_ffn_conv_w)
    local_g = {}
    for nm in SMALL_ORDER:
        gfull = full[nm]
        if nm in SMALL_SHARDED:
            per = gfull.shape[-1] // N_DEV
            gfull = lax.dynamic_slice_in_dim(gfull, me * per, per, axis=gfull.ndim - 1)
        local_g[nm] = gfull.reshape(local_w[nm].shape)
    shapes = [local_w[nm].shape for nm in SMALL_ORDER]
    pk = lambda dd: _pack_rows([dd[nm] for nm in SMALL_ORDER], SMALL_ADAM_ROWS)
    d_p, m_p, v_p = adam_packed(pk(local_w), pk(local_g), pk(local_m), pk(local_v))
    for nm, dl, mn, vn in zip(SMALL_ORDER, _unpack_rows(d_p, shapes), _unpack_rows(m_p, shapes),
                              _unpack_rows(v_p, shapes)):
        res[nm] = [local_g[nm], dl, mn, vn]

    order = ("attn_norm", "gla_w_in", "gla_w_a2", "gla_b_a2", "gla_head_norm", "gla_w_out", "kv_norm", "w_kv",
             "dsa_w_q", "dsa_w_out", "ffn_norm", "ffn_w_up", "ffn_conv_w", "ffn_conv_b", "ffn_w_down", "final_norm")
    outs = [loss, grad_x[None]]
    for kind in range(4):
        outs.extend(res[nm][kind] for nm in order)
    return tuple(outs)
```

```python
import functools

import jax
import jax.numpy as jnp
from jax import lax
from jax.experimental import pallas as pl
from jax.experimental.pallas import tpu as pltpu

F32 = jnp.float32
BF16 = jnp.bfloat16
MESH = pl.DeviceIdType.MESH
ANY = pl.BlockSpec(memory_space=pl.ANY)

N_DEV = 8
D_MODEL = 2048
GLA_HEADS = 4
GLA_KEY_DIM = 1024
GLA_VAL_DIM = 2048
GLA_DK = 256
GLA_DV = 512
GATE_RANK = 16
GATE_NORMALIZER = 16.0
GLA_CHUNK = 64
GLA_STEP_CHUNKS = 4
GLA_IN_DIM = 2 * GLA_KEY_DIM + 2 * GLA_VAL_DIM + GATE_RANK
GLA_IN_PAD = 6272
ATT_HEADS = 16
HEAD_DIM = 128
DILATIONS = (1, 4, 16)
STREAMS = DILATIONS[-1]
ATT_BLOCK = 128
D_FF = 5632
EPS = 1e-6
ADAM_LR = 0.001
ADAM_B1 = 0.9
ADAM_B2 = 0.999
ADAM_EPS = 1e-08
ADAM_WD = 0.01
ADAM_STEP = 10
NEG = -1e30
LANE = 128
NORM_ROWS = 64
VMEM_LIMIT = 52 * 1024 * 1024
ALIBI_SLOPES = tuple(2.0 ** (-0.5 * (i + 1)) for i in range(ATT_HEADS))


def _params(*sem):
    return pltpu.CompilerParams(dimension_semantics=sem, vmem_limit_bytes=VMEM_LIMIT)


def _tile(n, cap):
    best = None
    for t in range(LANE, min(n, cap) + 1, LANE):
        if n % t == 0:
            best = t
    return best if best is not None else n


def _shard_group(j, ns, cap):
    best = 1
    for g in range(1, j + 1):
        if j % g == 0 and g * ns <= cap:
            best = g
    return best


def _rows(r, c, budget=256 * 1024):
    best = None
    for t in range(16, r + 1, 16):
        if r % t == 0 and t * c <= budget:
            best = t
    return best if best is not None else r


def _flip(coord, bit):
    return 1 - coord if bit else coord


def _place():
    x, y, c = lax.axis_index("x"), lax.axis_index("y"), lax.axis_index("c")
    return x, y, c, 4 * x + 2 * y + c


def _rows_of(ref, rows):
    return ref if rows is None else ref.at[pl.ds(rows[0], rows[1] - rows[0])]


class Jobs:
    def __init__(self):
        self.srcs = []
        self.bufs = []
        self.sems = []
        self.steps = []

    def _src(self, a):
        for i, b in enumerate(self.srcs):
            if b is a:
                return i
        self.srcs.append(a)
        return len(self.srcs) - 1

    def new(self, shape, dtype):
        self.bufs.append((None, jax.ShapeDtypeStruct(shape, dtype)))
        return len(self.bufs) - 1

    def thru(self, a):
        self.bufs.append((a, jax.ShapeDtypeStruct(a.shape, a.dtype)))
        return len(self.bufs) - 1

    def _sem(self, n):
        self.sems.append(pltpu.SemaphoreType.DMA((n,)))
        return len(self.sems) - 1

    def gather_ici(self, src, buf, rows=None):
        si, send, recv, loc = self._src(src), self._sem(4), self._sem(4), self._sem(1)

        def remote(srcs, bufs, sems, slot_of):
            x, y, c, me = _place()
            peers = [(x, y, 1 - c), (1 - x, y, c), (x, 1 - y, c), (1 - x, 1 - y, c)]
            return [pltpu.make_async_remote_copy(
                src_ref=_rows_of(srcs[si], rows),
                dst_ref=_rows_of(bufs[buf].at[me if slot_of == "mine" else 4 * p[0] + 2 * p[1] + p[2]], rows),
                send_sem=sems[send].at[k], recv_sem=sems[recv].at[k], device_id=p, device_id_type=MESH)
                for k, p in enumerate(peers)]

        def local(srcs, bufs, sems):
            return pltpu.make_async_copy(_rows_of(srcs[si], rows), _rows_of(bufs[buf].at[_place()[3]], rows),
                                         sems[loc].at[0])

        def start(srcs, bufs, sems):
            local(srcs, bufs, sems).start()
            for cp in remote(srcs, bufs, sems, "mine"):
                cp.start()

        def finish(srcs, bufs, sems):
            for cp in remote(srcs, bufs, sems, "peer"):
                cp.wait_recv()
            for cp in remote(srcs, bufs, sems, "mine"):
                cp.wait_send()
            local(srcs, bufs, sems).wait()

        self.steps.append((start, finish))

    def gather_d2d(self, buf, rows=None):
        send, recv = self._sem(3), self._sem(3)

        def copies(bufs, sems, core):
            x, y, c, _ = _place()
            cc = c if core == "mine" else 1 - c
            chips = [(1 - x, y), (x, 1 - y), (1 - x, 1 - y)]
            return [pltpu.make_async_remote_copy(
                src_ref=_rows_of(bufs[buf].at[4 * px + 2 * py + cc], rows),
                dst_ref=_rows_of(bufs[buf].at[4 * px + 2 * py + cc], rows),
                send_sem=sems[send].at[k], recv_sem=sems[recv].at[k],
                device_id=(x, y, 1 - c), device_id_type=MESH) for k, (px, py) in enumerate(chips)]

        def start(srcs, bufs, sems):
            for cp in copies(bufs, sems, "mine"):
                cp.start()

        def finish(srcs, bufs, sems):
            for cp in copies(bufs, sems, "sibling"):
                cp.wait_recv()
            for cp in copies(bufs, sems, "mine"):
                cp.wait_send()

        self.steps.append((start, finish))

    def scatter(self, src, buf, rows=None, same=False):
        si, send, recv, loc = self._src(src), self._sem(N_DEV - 1), self._sem(N_DEV - 1), self._sem(1)

        def block(srcs, dev):
            return _rows_of(srcs[si] if same else srcs[si].at[dev], rows)

        def remote(srcs, bufs, sems, slot_of):
            x, y, c, me = _place()
            out = []
            for k in range(1, N_DEV):
                px, py, pc = _flip(x, k >> 2), _flip(y, (k >> 1) & 1), _flip(c, k & 1)
                peer = 4 * px + 2 * py + pc
                out.append(pltpu.make_async_remote_copy(
                    src_ref=block(srcs, peer),
                    dst_ref=_rows_of(bufs[buf].at[me if slot_of == "mine" else peer], rows),
                    send_sem=sems[send].at[k - 1], recv_sem=sems[recv].at[k - 1],
                    device_id=(px, py, pc), device_id_type=MESH))
            return out

        def local(srcs, bufs, sems):
            me = _place()[3]
            return pltpu.make_async_copy(block(srcs, me), _rows_of(bufs[buf].at[me], rows), sems[loc].at[0])

        def start(srcs, bufs, sems):
            local(srcs, bufs, sems).start()
            for cp in remote(srcs, bufs, sems, "mine"):
                cp.start()

        def finish(srcs, bufs, sems):
            for cp in remote(srcs, bufs, sems, "peer"):
                cp.wait_recv()
            for cp in remote(srcs, bufs, sems, "mine"):
                cp.wait_send()
            local(srcs, bufs, sems).wait()

        self.steps.append((start, finish))


def _call(body, *, name, grid, in_specs, out_specs, out_shape, args, sem, scratch_shapes=(), jobs=None):
    in_specs, out_specs, out_shape = list(in_specs), list(out_specs), list(out_shape)
    scratch_shapes = list(scratch_shapes)
    if jobs is None:
        res = pl.pallas_call(body, name=name, out_shape=out_shape, grid=grid, in_specs=in_specs,
                             out_specs=out_specs, scratch_shapes=scratch_shapes,
                             compiler_params=_params(*sem))(*args)
        return list(res), []
    thru = [a for a, _ in jobs.bufs if a is not None]
    n_in, n_src, n_thru = len(args), len(jobs.srcs), len(thru)
    n_out, n_buf, n_scr = len(out_shape), len(jobs.bufs), len(scratch_shapes)
    aliases, t = {}, 0
    for b, (a, _) in enumerate(jobs.bufs):
        if a is not None:
            aliases[n_in + n_src + t] = n_out + b
            t += 1

    def wrapped(*refs):
        at = 0
        ins = refs[at:at + n_in]; at += n_in
        srcs = refs[at:at + n_src]; at += n_src + n_thru
        outs = refs[at:at + n_out]; at += n_out
        bufs = refs[at:at + n_buf]; at += n_buf
        scr = refs[at:at + n_scr]; at += n_scr
        sems = refs[at:]
        first, last = None, None
        for axis, size in enumerate(grid):
            pid = pl.program_id(axis)
            f, l = pid == 0, pid == size - 1
            first = f if first is None else first & f
            last = l if last is None else last & l

        @pl.when(first)
        def _():
            for start, _ in jobs.steps:
                start(srcs, bufs, sems)

        body(*ins, *outs, *scr)

        @pl.when(last)
        def _():
            for _, finish in jobs.steps:
                finish(srcs, bufs, sems)

    res = pl.pallas_call(
        wrapped, name=name,
        out_shape=out_shape + [s for _, s in jobs.bufs],
        grid=grid,
        in_specs=in_specs + [ANY] * (n_src + n_thru),
        out_specs=out_specs + [ANY] * n_buf,
        scratch_shapes=scratch_shapes + jobs.sems,
        input_output_aliases=aliases,
        compiler_params=_params(*(["arbitrary"] * len(grid))),
    )(*args, *jobs.srcs, *thru)
    return res[:n_out], res[n_out:]


def mm_nn(a, w, *, out_dtype, name, res=None, tm=None, jobs=None):
    m, k = a.shape
    j, k2, ns = w.shape
    whole = j == 1 and ns <= 2048 and k <= 2048
    tm = tm or (1024 if a.dtype == BF16 and not whole else 512)
    assert k == k2 and m % tm == 0
    tn = ns if whole else _tile(ns, 1408)
    nsub = ns // tn
    tk = k if k <= 2048 else _tile(k, 1408)
    nk = k // tk
    has_res = res is not None

    def body(*refs):
        if has_res:
            a_ref, w_ref, r_ref, o_ref, acc = refs
        else:
            a_ref, w_ref, o_ref, acc = refs
        kk = pl.program_id(2)

        @pl.when(kk == 0)
        def _():
            acc[...] = jnp.zeros_like(acc)

        acc[...] += jnp.dot(a_ref[...].astype(BF16), w_ref[...], preferred_element_type=F32)

        @pl.when(kk == nk - 1)
        def _():
            r = acc[...]
            if has_res:
                r = r + r_ref[...]
            o_ref[...] = r.astype(out_dtype)

    in_specs = [
        pl.BlockSpec((tm, tk), lambda i, n, kk: (i, kk)),
        pl.BlockSpec((None, tk, tn), lambda i, n, kk: (n // nsub, kk, n % nsub)),
    ]
    args = [a, w]
    out_tile = pl.BlockSpec((tm, tn), lambda i, n, kk: (i, n))
    if has_res:
        in_specs.append(out_tile)
        args.append(res)
    (out,), bufs = _call(
        body, name=name, jobs=jobs,
        out_shape=[jax.ShapeDtypeStruct((m, j * ns), out_dtype)],
        grid=(m // tm, j * nsub, nk),
        in_specs=in_specs,
        out_specs=[out_tile],
        scratch_shapes=[pltpu.VMEM((tm, tn), F32)],
        args=args, sem=("parallel", "parallel", "arbitrary"))
    return out if jobs is None else (out, bufs)


def mm_nt(dy, w, *, out_dtype, name, tm=None, jobs=None, norm=None):
    parts, m, n = (1,) + dy.shape if dy.ndim == 2 else dy.shape
    n *= parts
    j, k, ns = w.shape
    if norm is not None:
        x, dres, gains, more = norm
        tm = tm or (256 if more else 512)
    tm = tm or 1024
    assert n == j * ns and m % tm == 0
    tn = _tile(ns, 2048)
    nsub = ns // tn
    jb = _shard_group(j // parts, ns, 2048 if norm is None else 1024) if nsub == 1 else 1
    tko = _tile(k, 1408) if norm is None else k
    nn = j * nsub // jb
    per_part = nn // parts
    if dy.ndim == 2:
        dy_spec = pl.BlockSpec((tm, jb * tn), lambda i, ko, nq: (i, nq))
    else:
        dy_spec = pl.BlockSpec((None, tm, jb * tn), lambda i, ko, nq: (nq // per_part, i, nq % per_part))
    if jb == 1:
        w_spec = pl.BlockSpec((None, tko, tn), lambda i, ko, nq: (nq // nsub, ko, nq % nsub))
    else:
        w_spec = pl.BlockSpec((jb, tko, ns), lambda i, ko, nq: (nq, ko, 0))

    n_gain = 0 if norm is None else len(gains)
    n_more = 0 if norm is None else len(more)

    def body(*refs):
        a_ref, w_ref = refs[:2]
        acc = refs[-1]
        nq = pl.program_id(2)
        first = pl.program_id(0) == 0

        @pl.when(nq == 0)
        def _():
            acc[...] = jnp.zeros_like(acc)

        if jb == 1:
            acc[...] += lax.dot_general(a_ref[...].astype(BF16), w_ref[...], (((1,), (1,)), ((), ())),
                                        preferred_element_type=F32)
        else:
            part = acc[...]
            for jj in range(jb):
                part = part + lax.dot_general(a_ref[:, jj * ns:(jj + 1) * ns].astype(BF16), w_ref[jj],
                                              (((1,), (1,)), ((), ())), preferred_element_type=F32)
            acc[...] = part

        @pl.when(nq == nn - 1)
        def _():
            if norm is None:
                refs[2][...] = acc[...].astype(out_dtype)
                return
            x_ref, r_ref = refs[2:4]
            g_refs = refs[4:4 + n_gain]
            e_refs = refs[4 + n_gain:4 + n_gain + n_more]
            dx_ref = refs[4 + n_gain + n_more]
            dg_refs = refs[5 + n_gain + n_more:-1]

            @pl.when(first)
            def _():
                for dg_ref in dg_refs:
                    dg_ref[...] = jnp.zeros_like(dg_ref)

            def rows(c, carry):
                sl = pl.ds(pl.multiple_of(c * NORM_ROWS, NORM_ROWS), NORM_ROWS)
                xv = x_ref[sl, :]
                r = lax.rsqrt(jnp.mean(xv * xv, axis=-1, keepdims=True) + EPS)
                xh = xv * r
                out = r_ref[sl, :]
                for idx, (g_ref, dg_ref) in enumerate(zip(g_refs, dg_refs)):
                    dyv = acc[sl, :] if idx == 0 else e_refs[idx - 1][sl, :].astype(F32)
                    dg_ref[...] += jnp.sum(dyv * xh, axis=0, keepdims=True)
                    dxh = dyv * g_ref[...]
                    out = out + r * (dxh - xh * jnp.mean(dxh * xh, axis=-1, keepdims=True))
                dx_ref[sl, :] = out
                return carry

            lax.fori_loop(0, tm // NORM_ROWS, rows, 0)

    out_tile = pl.BlockSpec((tm, tko), lambda i, ko, nq: (i, ko))
    in_specs, args = [dy_spec, w_spec], [dy, w]
    out_shape, out_specs = [jax.ShapeDtypeStruct((m, k), out_dtype)], [out_tile]
    sem = ("parallel", "parallel", "arbitrary")
    if norm is not None:
        vec = pl.BlockSpec((1, k), lambda i, ko, nq: (0, 0))
        in_specs += [out_tile, out_tile] + [vec] * n_gain + [out_tile] * n_more
        args += [x, dres] + list(gains) + list(more)
        out_shape = [jax.ShapeDtypeStruct((m, k), F32)] + [jax.ShapeDtypeStruct((1, k), F32)] * n_gain
        out_specs = [out_tile] + [vec] * n_gain
        sem = ("arbitrary", "arbitrary", "arbitrary")
    outs, bufs = _call(
        body, name=name, jobs=jobs, out_shape=out_shape, grid=(m // tm, k // tko, nn),
        in_specs=in_specs, out_specs=out_specs, scratch_shapes=[pltpu.VMEM((tm, tko), F32)], args=args, sem=sem)
    out = outs[0] if norm is None else (outs[0], outs[1:])
    return out if jobs is None else (out, bufs)


def mm_tn(x, dy, j, *, name, tm=2048, jobs=None):
    m, k = x.shape
    parts, m2, n = (1,) + dy.shape if dy.ndim == 2 else dy.shape
    n *= parts
    assert m == m2 and n % j == 0 and m % tm == 0
    ns = n // j
    tn = _tile(ns, 1408)
    nsub = ns // tn
    jb = _shard_group(j // parts, ns, 1536) if nsub == 1 else 1
    tk = _tile(k, 1408)
    nm = m // tm
    n_steps = j * nsub // jb
    per_part = n_steps // parts
    if dy.ndim == 2:
        dy_spec = pl.BlockSpec((tm, jb * tn), lambda kq, nq, mi: (mi, nq))
    else:
        dy_spec = pl.BlockSpec((None, tm, jb * tn), lambda kq, nq, mi: (nq // per_part, mi, nq % per_part))
    if jb == 1:
        out_spec = pl.BlockSpec((None, tk, tn), lambda kq, nq, mi: (nq // nsub, kq, nq % nsub))
        acc_shape = (tk, tn)
    else:
        out_spec = pl.BlockSpec((jb, tk, ns), lambda kq, nq, mi: (nq, kq, 0))
        acc_shape = (jb, tk, ns)

    def body(x_ref, dy_ref, o_ref, acc):
        mi = pl.program_id(2)

        @pl.when(mi == 0)
        def _():
            acc[...] = jnp.zeros_like(acc)

        xb = x_ref[...].astype(BF16)
        if jb == 1:
            acc[...] += lax.dot_general(xb, dy_ref[...].astype(BF16), (((0,), (0,)), ((), ())),
                                        preferred_element_type=F32)
        else:
            for jj in range(jb):
                acc[jj] += lax.dot_general(xb, dy_ref[:, jj * ns:(jj + 1) * ns].astype(BF16),
                                           (((0,), (0,)), ((), ())), preferred_element_type=F32)

        @pl.when(mi == nm - 1)
        def _():
            o_ref[...] = acc[...].astype(BF16)

    (out,), bufs = _call(
        body, name=name, jobs=jobs,
        out_shape=[jax.ShapeDtypeStruct((j, k, ns), BF16)],
        grid=(k // tk, n_steps, nm),
        in_specs=[
            pl.BlockSpec((tm, tk), lambda kq, nq, mi: (mi, kq)),
            dy_spec,
        ],
        out_specs=[out_spec],
        scratch_shapes=[pltpu.VMEM(acc_shape, F32)],
        args=[x, dy], sem=("parallel", "parallel", "arbitrary"))
    return out if jobs is None else (out, bufs)


STREAM_TC = LANE


def to_streams(x, *, name):
    s, c = x.shape
    per = s // STREAMS

    def body(x_ref, o_ref):
        for st in range(STREAMS):
            o_ref[pl.ds(st * per, per), :] = x_ref[pl.ds(st, per, stride=STREAMS), :]

    blk = pl.BlockSpec((s, STREAM_TC), lambda i: (0, i))
    return pl.pallas_call(body, name=name, out_shape=jax.ShapeDtypeStruct((s, c), x.dtype), grid=(c // STREAM_TC,),
                          in_specs=[blk], out_specs=blk, compiler_params=_params("parallel"))(x)


def from_streams(x, *, name):
    s, c = x.shape
    per = s // STREAMS

    def body(x_ref, o_ref):
        for st in range(STREAMS):
            o_ref[pl.ds(st, per, stride=STREAMS), :] = x_ref[pl.ds(st * per, per), :]

    blk = pl.BlockSpec((s, STREAM_TC), lambda i: (0, i))
    return pl.pallas_call(body, name=name, out_shape=jax.ShapeDtypeStruct((s, c), x.dtype), grid=(c // STREAM_TC,),
                          in_specs=[blk], out_specs=blk, compiler_params=_params("parallel"))(x)


def rms_fwd(x, gains, *, name, ts=512):
    s, d = x.shape
    n = len(gains)

    def body(x_ref, *refs):
        xv = x_ref[...]
        xh = xv * lax.rsqrt(jnp.mean(xv * xv, axis=-1, keepdims=True) + EPS)
        for g_ref, o_ref in zip(refs[:n], refs[n:]):
            o_ref[...] = (xh * g_ref[...]).astype(BF16)

    row = pl.BlockSpec((ts, d), lambda i: (i, 0))
    vec = pl.BlockSpec((1, d), lambda i: (0, 0))
    return pl.pallas_call(
        body,
        name=name,
        out_shape=[jax.ShapeDtypeStruct((s, d), BF16)] * n,
        grid=(s // ts,),
        in_specs=[row] + [vec] * n,
        out_specs=[row] * n,
        compiler_params=_params("parallel"),
    )(x, *gains)


def loss_head(h, gain, target, *, ts=256):
    s, d = h.shape

    def body(h_ref, g_ref, t_ref, l_ref, dh_ref, dg_ref):
        i = pl.program_id(0)

        @pl.when(i == 0)
        def _():
            l_ref[...] = jnp.zeros_like(l_ref)
            dg_ref[...] = jnp.zeros_like(dg_ref)

        xv = h_ref[...]
        r = lax.rsqrt(jnp.mean(xv * xv, axis=-1, keepdims=True) + EPS)
        xh = xv * r
        g = g_ref[...]
        err = xh * g - t_ref[...]
        l_ref[...] += 0.5 * jnp.sum(jnp.mean(err * err, axis=-1, keepdims=True))
        dy = err * (1.0 / d)
        dg_ref[...] += jnp.sum(dy * xh, axis=0, keepdims=True)
        dxh = dy * g
        dh_ref[...] = r * (dxh - xh * jnp.mean(dxh * xh, axis=-1, keepdims=True))

    row = pl.BlockSpec((ts, d), lambda i: (i, 0))
    vec = pl.BlockSpec((1, d), lambda i: (0, 0))
    return pl.pallas_call(
        body,
        name="loss_head",
        out_shape=[jax.ShapeDtypeStruct((8, LANE), F32), jax.ShapeDtypeStruct((s, d), F32),
                   jax.ShapeDtypeStruct((1, d), F32)],
        grid=(s // ts,),
        in_specs=[row, vec, row],
        out_specs=[pl.BlockSpec((8, LANE), lambda i: (0, 0)), row, vec],
        compiler_params=_params("arbitrary"),
    )(h, gain, target)


A_BLOCK = (2 * GLA_KEY_DIM + 2 * GLA_VAL_DIM) // LANE


def gate_fwd(proj, w_a2p, b_a2, *, ts=512):
    s = proj.shape[0]

    def body(a_ref, w_ref, b_ref, o_ref):
        z = jnp.dot(a_ref[...].astype(BF16), w_ref[...], preferred_element_type=F32) + b_ref[...]
        o_ref[...] = (jnp.minimum(z, 0.0) - jnp.log(1.0 + jnp.exp(-jnp.abs(z)))) * (1.0 / GATE_NORMALIZER)

    return pl.pallas_call(
        body,
        name="gate_fwd",
        out_shape=jax.ShapeDtypeStruct((s, GLA_KEY_DIM), F32),
        grid=(s // ts,),
        in_specs=[pl.BlockSpec((ts, LANE), lambda i: (i, A_BLOCK)),
                  pl.BlockSpec((LANE, GLA_KEY_DIM), lambda i: (0, 0)),
                  pl.BlockSpec((1, GLA_KEY_DIM), lambda i: (0, 0))],
        out_specs=pl.BlockSpec((ts, GLA_KEY_DIM), lambda i: (i, 0)),
        compiler_params=_params("parallel"),
    )(proj, w_a2p, b_a2)


def gate_bwd(proj, w_a2p, b_a2, dla, *, ts=512):
    s = proj.shape[0]

    def body(a_ref, w_ref, b_ref, dla_ref, da_ref, dw_ref, db_ref):
        i = pl.program_id(0)

        @pl.when(i == 0)
        def _():
            dw_ref[...] = jnp.zeros_like(dw_ref)
            db_ref[...] = jnp.zeros_like(db_ref)

        a = a_ref[...].astype(BF16)
        w = w_ref[...]
        z = jnp.dot(a, w, preferred_element_type=F32) + b_ref[...]
        dz = dla_ref[...] * (1.0 / GATE_NORMALIZER) / (1.0 + jnp.exp(z))
        dzb = dz.astype(BF16)
        da_ref[...] = lax.dot_general(dzb, w, (((1,), (1,)), ((), ())), preferred_element_type=F32).astype(BF16)
        dw_ref[...] += lax.dot_general(a, dzb, (((0,), (0,)), ((), ())), preferred_element_type=F32)
        db_ref[...] += jnp.sum(dz, axis=0, keepdims=True)

    return pl.pallas_call(
        body,
        name="gate_bwd",
        out_shape=[jax.ShapeDtypeStruct((s, LANE), BF16), jax.ShapeDtypeStruct((LANE, GLA_KEY_DIM), F32),
                   jax.ShapeDtypeStruct((1, GLA_KEY_DIM), F32)],
        grid=(s // ts,),
        in_specs=[pl.BlockSpec((ts, LANE), lambda i: (i, A_BLOCK)),
                  pl.BlockSpec((LANE, GLA_KEY_DIM), lambda i: (0, 0)),
                  pl.BlockSpec((1, GLA_KEY_DIM), lambda i: (0, 0)),
                  pl.BlockSpec((ts, GLA_KEY_DIM), lambda i: (i, 0))],
        out_specs=[pl.BlockSpec((ts, LANE), lambda i: (i, 0)),
                   pl.BlockSpec((LANE, GLA_KEY_DIM), lambda i: (0, 0)),
                   pl.BlockSpec((1, GLA_KEY_DIM), lambda i: (0, 0))],
        compiler_params=_params("arbitrary"),
    )(proj, w_a2p, b_a2, dla)


def _masked_sum(mask, x):
    m = mask.astype(BF16)
    hi = x.astype(BF16)
    rest = x - hi.astype(F32)
    mid = rest.astype(BF16)
    lo = (rest - mid.astype(F32)).astype(BF16)
    dot = lambda t: jnp.dot(m, t, preferred_element_type=F32)
    return dot(hi) + dot(mid) + dot(lo)


def _chunk_terms(q, k, la):
    c_len = GLA_CHUNK
    row = lax.broadcasted_iota(jnp.int32, (c_len, c_len), 0)
    col = lax.broadcasted_iota(jnp.int32, (c_len, c_len), 1)
    tri = row >= col
    c = _masked_sum(tri, la)
    last = jnp.sum(la, axis=0, keepdims=True)
    q_dec = q * (GLA_DK ** -0.5) * jnp.exp(c)
    k_inv = k * jnp.exp(-c)
    k_end = k * jnp.exp(last - c)
    return c, last, q_dec, k_inv, k_end, tri


def _dot(a, b, ca, cb):
    return lax.dot_general(a.astype(BF16), b.astype(BF16), (((ca,), (cb,)), ((), ())), preferred_element_type=F32)


def gla_fwd(proj, la, hn, jobs=None):
    s = proj.shape[0]
    n_chunks = s // GLA_CHUNK
    rows = GLA_CHUNK * GLA_STEP_CHUNKS

    def body(q_ref, k_ref, v_ref, r_ref, la_ref, hn_ref, o_ref, st_out, og_ref, st):
        @pl.when(pl.program_id(0) == 0)
        def _():
            st[...] = jnp.zeros_like(st)

        for h in range(GLA_HEADS):
            hk = slice(h * GLA_DK, (h + 1) * GLA_DK)
            hv = slice(h * GLA_DV, (h + 1) * GLA_DV)
            for cc in range(GLA_STEP_CHUNKS):
                rs = slice(cc * GLA_CHUNK, (cc + 1) * GLA_CHUNK)
                _, last, q_dec, k_inv, k_end, tri = _chunk_terms(q_ref[rs, hk], k_ref[rs, hk], la_ref[rs, hk])
                v = v_ref[rs, hv]
                a = jnp.where(tri, _dot(q_dec, k_inv, 1, 1), 0.0)
                state = st[h]
                st_out[h, cc] = state
                ov = _dot(a, v, 1, 0) + _dot(q_dec, state, 1, 1)
                o_ref[rs, hv] = ov
                st[h] = state * jnp.exp(last) + _dot(v, k_end, 0, 0)
                oh = ov * lax.rsqrt(jnp.mean(ov * ov, axis=-1, keepdims=True) + EPS)
                r = r_ref[rs, hv]
                og_ref[rs, hv] = (oh * hn_ref[...] * (r * jax.nn.sigmoid(r))).astype(BF16)

    key = lambda col: pl.BlockSpec((rows, GLA_KEY_DIM), lambda n: (n, col))
    val = lambda col: pl.BlockSpec((rows, GLA_VAL_DIM), lambda n: (n, col))
    outs, bufs = _call(
        body, name="gla_fwd", jobs=jobs,
        out_shape=[jax.ShapeDtypeStruct((s, GLA_VAL_DIM), F32),
                   jax.ShapeDtypeStruct((GLA_HEADS, n_chunks, GLA_DV, GLA_DK), F32),
                   jax.ShapeDtypeStruct((s, GLA_VAL_DIM), BF16)],
        grid=(n_chunks // GLA_STEP_CHUNKS,),
        in_specs=[key(0), key(1), val(1), val(R_BLOCK // GLA_HEADS), key(0), pl.BlockSpec((1, GLA_DV), lambda n: (0, 0))],
        out_specs=[val(0), pl.BlockSpec((GLA_HEADS, GLA_STEP_CHUNKS, GLA_DV, GLA_DK), lambda n: (0, n, 0, 0)), val(0)],
        scratch_shapes=[pltpu.VMEM((GLA_HEADS, GLA_DV, GLA_DK), F32)],
        args=[proj, proj, proj, proj, la, hn], sem=("arbitrary",))
    return outs if jobs is None else (outs, bufs)


def gla_bwd(proj, la, states, o, hn, dog, jobs=None):
    s = proj.shape[0]
    n_steps = s // GLA_CHUNK // GLA_STEP_CHUNKS
    lastc = n_steps - 1
    rows = GLA_CHUNK * GLA_STEP_CHUNKS

    def body(q_ref, k_ref, v_ref, r_ref, la_ref, o_ref, hn_ref, dog_ref, st_ref,
             dq_ref, dk_ref, dv_ref, dr_ref, dla_ref, dhn_ref, dst):
        @pl.when(pl.program_id(0) == 0)
        def _():
            dst[...] = jnp.zeros_like(dst)
            dhn_ref[...] = jnp.zeros_like(dhn_ref)

        upper = (lax.broadcasted_iota(jnp.int32, (GLA_CHUNK, GLA_CHUNK), 0)
                 <= lax.broadcasted_iota(jnp.int32, (GLA_CHUNK, GLA_CHUNK), 1))
        gain = hn_ref[...]
        for h in range(GLA_HEADS):
            hk = slice(h * GLA_DK, (h + 1) * GLA_DK)
            hv = slice(h * GLA_DV, (h + 1) * GLA_DV)
            for cc in reversed(range(GLA_STEP_CHUNKS)):
                rs = slice(cc * GLA_CHUNK, (cc + 1) * GLA_CHUNK)
                ov = o_ref[rs, hv]
                inv = lax.rsqrt(jnp.mean(ov * ov, axis=-1, keepdims=True) + EPS)
                oh = ov * inv
                r = r_ref[rs, hv]
                sig = jax.nn.sigmoid(r)
                dgv = dog_ref[rs, hv]
                d_on = dgv * (r * sig)
                dr_ref[rs, hv] = (dgv * (oh * gain) * (sig * (1.0 + r * (1.0 - sig)))).astype(BF16)
                dhn_ref[...] += jnp.sum(d_on * oh, axis=0, keepdims=True)
                doh = d_on * gain
                dout = inv * (doh - oh * jnp.mean(doh * oh, axis=-1, keepdims=True))
                c, last, q_dec, k_inv, k_end, tri = _chunk_terms(q_ref[rs, hk], k_ref[rs, hk], la_ref[rs, hk])
                v = v_ref[rs, hv]
                state = st_ref[h, cc]
                dstate = dst[h]
                e_last = jnp.exp(last)
                a = jnp.where(tri, _dot(q_dec, k_inv, 1, 1), 0.0)
                da = jnp.where(tri, _dot(dout, v, 1, 1), 0.0)
                dv_ref[rs, hv] = (_dot(a, dout, 0, 0) + _dot(k_end, dstate, 1, 1)).astype(BF16)
                dq_dec = _dot(da, k_inv, 1, 0) + _dot(dout, state, 1, 0)
                dk_inv = _dot(da, q_dec, 0, 0)
                dk_end = _dot(v, dstate, 1, 0)
                dst[h] = dstate * e_last + _dot(dout, q_dec, 0, 0)
                dq_ref[rs, hk] = (dq_dec * (GLA_DK ** -0.5) * jnp.exp(c)).astype(BF16)
                dk_ref[rs, hk] = (dk_inv * jnp.exp(-c) + dk_end * jnp.exp(last - c)).astype(BF16)
                ke_term = dk_end * k_end
                dc = dq_dec * q_dec - dk_inv * k_inv - ke_term
                dlast = (jnp.sum(ke_term, axis=0, keepdims=True)
                         + e_last * jnp.sum(dstate * state, axis=0, keepdims=True))
                dla_ref[rs, hk] = _masked_sum(upper, dc) + dlast

    key = lambda col: pl.BlockSpec((rows, GLA_KEY_DIM), lambda n: (lastc - n, col))
    val = lambda col: pl.BlockSpec((rows, GLA_VAL_DIM), lambda n: (lastc - n, col))
    vec = pl.BlockSpec((1, GLA_DV), lambda n: (0, 0))
    outs, bufs = _call(
        body, name="gla_bwd", jobs=jobs,
        out_shape=[jax.ShapeDtypeStruct((s, GLA_KEY_DIM), BF16), jax.ShapeDtypeStruct((s, GLA_KEY_DIM), BF16),
                   jax.ShapeDtypeStruct((s, GLA_VAL_DIM), BF16), jax.ShapeDtypeStruct((s, GLA_VAL_DIM), BF16),
                   jax.ShapeDtypeStruct((s, GLA_KEY_DIM), F32), jax.ShapeDtypeStruct((1, GLA_DV), F32)],
        grid=(n_steps,),
        in_specs=[key(0), key(1), val(1), val(R_BLOCK // GLA_HEADS), key(0), val(0), vec, val(0),
                  pl.BlockSpec((GLA_HEADS, GLA_STEP_CHUNKS, GLA_DV, GLA_DK), lambda n: (0, lastc - n, 0, 0))],
        out_specs=[key(0), key(0), val(0), val(0), key(0), vec],
        scratch_shapes=[pltpu.VMEM((GLA_HEADS, GLA_DV, GLA_DK), F32)],
        args=[proj, proj, proj, proj, la, o, hn, dog, states], sem=("arbitrary",))
    return outs if jobs is None else (outs, bufs)


R_BLOCK = (2 * GLA_KEY_DIM + GLA_VAL_DIM) // GLA_DV


CONV_TC = 128
HEAD_ROWS = 16
SQRT_HALF = 0.7071067811865476
INV_SQRT_2PI = 0.3989422804014327


def _conv_gate(g_ref, cw_ref, cb_ref):
    g0 = g_ref[...].astype(F32)
    t = lax.broadcasted_iota(jnp.int32, g0.shape, 0)
    g1 = jnp.where(t >= 1, pltpu.roll(g0, 1, 0), 0.0)
    g2 = jnp.where(t >= 2, pltpu.roll(g0, 2, 0), 0.0)
    gc = cw_ref[0:1, :] * g2 + cw_ref[1:2, :] * g1 + cw_ref[2:3, :] * g0 + cb_ref[...]
    return g0, g1, g2, gc, t


def convglu_fwd(up, conv_w, conv_b, *, name, jobs=None):
    s = up.shape[0]
    nc = D_FF // CONV_TC

    def act(gc, u):
        return (0.5 * gc * (1.0 + lax.erf(gc * SQRT_HALF)) * u.astype(F32)).astype(BF16)

    def body(u_ref, g_ref, cw_ref, cb_ref, o_ref):
        g0 = g_ref[...].astype(F32)
        gc = (cw_ref[0:1, :] * pltpu.roll(g0, 2, 0) + cw_ref[1:2, :] * pltpu.roll(g0, 1, 0)
              + cw_ref[2:3, :] * g0 + cb_ref[...])
        o_ref[...] = act(gc, u_ref[...])
        head = pl.ds(0, HEAD_ROWS)
        _, _, _, gc_head, _ = _conv_gate(g_ref.at[head], cw_ref, cb_ref)
        o_ref[head, :] = act(gc_head, u_ref[head, :])

    (out,), bufs = _call(
        body, name=name, jobs=jobs,
        out_shape=[jax.ShapeDtypeStruct((s, D_FF), BF16)],
        grid=(nc,),
        in_specs=[pl.BlockSpec((s, CONV_TC), lambda c: (0, c)),
                  pl.BlockSpec((s, CONV_TC), lambda c: (0, nc + c)),
                  pl.BlockSpec((3, CONV_TC), lambda c: (0, c)),
                  pl.BlockSpec((1, CONV_TC), lambda c: (0, c))],
        out_specs=[pl.BlockSpec((s, CONV_TC), lambda c: (0, c))],
        args=[up, up, conv_w, conv_b], sem=("parallel",))
    return out if jobs is None else (out, bufs)


def convglu_bwd(up, conv_w, conv_b, dact, *, name, jobs=None):
    s = up.shape[0]
    nc = D_FF // CONV_TC

    def body(u_ref, g_ref, cw_ref, cb_ref, da_ref, dup_ref, dcw_ref, dcb_ref):
        du_ref, dg_ref = dup_ref.at[0], dup_ref.at[1]
        g0, g1, g2, gc, t = _conv_gate(g_ref, cw_ref, cb_ref)
        cdf = 0.5 * (1.0 + lax.erf(gc * SQRT_HALF))
        da = da_ref[...].astype(F32)
        du_ref[...] = (da * gc * cdf).astype(BF16)
        dgc = da * u_ref[...].astype(F32) * (cdf + gc * jnp.exp(-0.5 * gc * gc) * INV_SQRT_2PI)
        dcb_ref[...] = jnp.sum(dgc, axis=0, keepdims=True)
        dcw_ref[0:1, :] = jnp.sum(dgc * g2, axis=0, keepdims=True)
        dcw_ref[1:2, :] = jnp.sum(dgc * g1, axis=0, keepdims=True)
        dcw_ref[2:3, :] = jnp.sum(dgc * g0, axis=0, keepdims=True)
        n1 = jnp.where(t < s - 1, pltpu.roll(dgc, s - 1, 0), 0.0)
        n2 = jnp.where(t < s - 2, pltpu.roll(dgc, s - 2, 0), 0.0)
        dg_ref[...] = (cw_ref[2:3, :] * dgc + cw_ref[1:2, :] * n1 + cw_ref[0:1, :] * n2).astype(BF16)

    col = pl.BlockSpec((s, CONV_TC), lambda c: (0, c))
    outs, bufs = _call(
        body, name=name, jobs=jobs,
        out_shape=[jax.ShapeDtypeStruct((2, s, D_FF), BF16),
                   jax.ShapeDtypeStruct((3, D_FF), F32), jax.ShapeDtypeStruct((1, D_FF), F32)],
        grid=(nc,),
        in_specs=[col, pl.BlockSpec((s, CONV_TC), lambda c: (0, nc + c)),
                  pl.BlockSpec((3, CONV_TC), lambda c: (0, c)),
                  pl.BlockSpec((1, CONV_TC), lambda c: (0, c)), col],
        out_specs=[pl.BlockSpec((2, s, CONV_TC), lambda c: (0, 0, c)), pl.BlockSpec((3, CONV_TC), lambda c: (0, c)),
                   pl.BlockSpec((1, CONV_TC), lambda c: (0, c))],
        args=[up, up, conv_w, conv_b, dact], sem=("parallel",))
    return outs if jobs is None else (outs, bufs)


SLOPE_TILE = (8, LANE)


def _slope_table():
    return jnp.broadcast_to(jnp.asarray(ALIBI_SLOPES, F32)[:, None, None], (ATT_HEADS,) + SLOPE_TILE)


def _pieces(s_len, d):
    npc = STREAMS // d
    lp = ATT_BLOCK // npc
    return npc, lp, (s_len // STREAMS) // lp


def _gather(ref, r, b, d, s_len):
    npc, lp, _ = _pieces(s_len, d)
    per = s_len // STREAMS
    parts = [ref[pl.ds((r + d * k) * per + b * lp, lp), :] for k in range(npc)]
    return parts[0] if npc == 1 else jnp.concatenate(parts, axis=0)


def _scatter(ref, r, b, d, s_len, val, add=False):
    npc, lp, _ = _pieces(s_len, d)
    per = s_len // STREAMS
    for k in range(npc):
        rows = pl.ds((r + d * k) * per + b * lp, lp)
        piece = val[k * lp:(k + 1) * lp]
        if add:
            ref[rows, :] += piece
        else:
            ref[rows, :] = piece


def _stream_bias(slope, d, s_len):
    npc, lp, _ = _pieces(s_len, d)
    qi = lax.broadcasted_iota(jnp.int32, (ATT_BLOCK, 2 * ATT_BLOCK), 0)
    c = lax.broadcasted_iota(jnp.int32, (ATT_BLOCK, 2 * ATT_BLOCK), 1)
    own = c // ATT_BLOCK
    cc = c - own * ATT_BLOCK
    dist = npc * ((qi % lp) - (cc % lp) + lp * (1 - own)) + (qi // lp - cc // lp)
    ok = (dist >= 0) & (dist <= ATT_BLOCK)
    return jnp.where(ok, (slope * (-float(d))) * dist.astype(F32), NEG)


def attn_fwd(q, kv, jobs=None):
    s_len = q.shape[0]
    scale = HEAD_DIM ** -0.5

    def body(sl_ref, q_ref, k_ref, v_ref, o_ref, lse_ref):
        g = pl.program_id(1)
        slope = sl_ref[0:1, 0:1]

        def branch(gi, d):
            _, _, nblk = _pieces(s_len, d)
            bias = _stream_bias(slope, d, s_len)
            for r in range(d):
                for b in range(nblk):
                    qb = _gather(q_ref, r, b, d, s_len)
                    kc, vc = _gather(k_ref, r, b, d, s_len), _gather(v_ref, r, b, d, s_len)
                    if b == 0:
                        kcat, vcat, bb = kc, vc, bias[:, ATT_BLOCK:]
                    else:
                        kcat = jnp.concatenate([_gather(k_ref, r, b - 1, d, s_len), kc], axis=0)
                        vcat = jnp.concatenate([_gather(v_ref, r, b - 1, d, s_len), vc], axis=0)
                        bb = bias
                    sc = _dot(qb, kcat, 1, 1) * scale + bb
                    m = jnp.max(sc, axis=-1, keepdims=True)
                    p = jnp.exp(sc - m)
                    l = jnp.sum(p, axis=-1, keepdims=True)
                    o_new = _dot(p, vcat, 1, 0) / l
                    lse_new = m + jnp.log(l)
                    if gi > 0:
                        lse_old = _gather(lse_ref, r, b, d, s_len)[:, 0:1]
                        top = jnp.maximum(lse_old, lse_new)
                        e_old, e_new = jnp.exp(lse_old - top), jnp.exp(lse_new - top)
                        den = e_old + e_new
                        o_new = (e_old * _gather(o_ref, r, b, d, s_len) + e_new * o_new) / den
                        lse_new = top + jnp.log(den)
                    _scatter(o_ref, r, b, d, s_len, o_new)
                    _scatter(lse_ref, r, b, d, s_len, jnp.broadcast_to(lse_new, (ATT_BLOCK, HEAD_DIM)))

        for gi, d in enumerate(DILATIONS):
            @pl.when(g == gi)
            def _():
                branch(gi, d)

    blk = lambda col: pl.BlockSpec((s_len, HEAD_DIM), lambda h, g: (0, col(h, g)))
    head = lambda h, g: h
    outs, bufs = _call(
        body, name="attn_fwd", jobs=jobs,
        out_shape=[jax.ShapeDtypeStruct((s_len, ATT_HEADS * HEAD_DIM), F32)] * 2,
        grid=(ATT_HEADS, len(DILATIONS)),
        in_specs=[pl.BlockSpec((None,) + SLOPE_TILE, lambda h, g: (h, 0, 0)),
                  blk(lambda h, g: g * ATT_HEADS + h), blk(head), blk(lambda h, g: ATT_HEADS + h)],
        out_specs=[blk(head), blk(head)],
        args=[_slope_table(), q, kv, kv], sem=("parallel", "arbitrary"))
    return outs if jobs is None else (outs, bufs)


def attn_bwd(q, kv, o, lse, do, jobs=None):
    s_len = q.shape[0]
    scale = HEAD_DIM ** -0.5
    chunks = s_len // ATT_BLOCK

    def body(sl_ref, q_ref, k_ref, v_ref, o_ref, lse_ref, do_ref, dq_ref, dkv_ref, dlt):
        g = pl.program_id(1)
        slope = sl_ref[0:1, 0:1]
        dk_ref, dv_ref = dkv_ref.at[0], dkv_ref.at[1]

        @pl.when(g == 0)
        def _():
            dkv_ref[...] = jnp.zeros_like(dkv_ref)

            def deltas(c, carry):
                rows = pl.ds(pl.multiple_of(c * ATT_BLOCK, ATT_BLOCK), ATT_BLOCK)
                dlt[rows, :] = jnp.sum(do_ref[rows, :] * o_ref[rows, :], axis=-1, keepdims=True)
                return carry
            lax.fori_loop(0, chunks, deltas, 0)

        def branch(d):
            _, _, nblk = _pieces(s_len, d)
            bias = _stream_bias(slope, d, s_len)
            for r in range(d):
                for b in range(nblk):
                    qb = _gather(q_ref, r, b, d, s_len)
                    dob = _gather(do_ref, r, b, d, s_len)
                    kc, vc = _gather(k_ref, r, b, d, s_len), _gather(v_ref, r, b, d, s_len)
                    if b == 0:
                        kcat, vcat, bb = kc, vc, bias[:, ATT_BLOCK:]
                    else:
                        kcat = jnp.concatenate([_gather(k_ref, r, b - 1, d, s_len), kc], axis=0)
                        vcat = jnp.concatenate([_gather(v_ref, r, b - 1, d, s_len), vc], axis=0)
                        bb = bias
                    sc = _dot(qb, kcat, 1, 1) * scale + bb
                    p = jnp.exp(sc - _gather(lse_ref, r, b, d, s_len)[:, 0:1])
                    ds = p * (_dot(dob, vcat, 1, 1) - _gather(dlt, r, b, d, s_len))
                    _scatter(dq_ref, r, b, d, s_len, _dot(ds, kcat, 1, 0) * scale)
                    dk = _dot(ds, qb, 0, 0) * scale
                    dv = _dot(p, dob, 0, 0)
                    if b == 0:
                        _scatter(dk_ref, r, b, d, s_len, dk, add=True)
                        _scatter(dv_ref, r, b, d, s_len, dv, add=True)
                    else:
                        _scatter(dk_ref, r, b - 1, d, s_len, dk[:ATT_BLOCK], add=True)
                        _scatter(dv_ref, r, b - 1, d, s_len, dv[:ATT_BLOCK], add=True)
                        _scatter(dk_ref, r, b, d, s_len, dk[ATT_BLOCK:], add=True)
                        _scatter(dv_ref, r, b, d, s_len, dv[ATT_BLOCK:], add=True)

        for gi, d in enumerate(DILATIONS):
            @pl.when(g == gi)
            def _():
                branch(d)

    blk = lambda col: pl.BlockSpec((s_len, HEAD_DIM), lambda h, g: (0, col(h, g)))
    head = lambda h, g: h
    q_col = lambda h, g: g * ATT_HEADS + h
    outs, bufs = _call(
        body, name="attn_bwd", jobs=jobs,
        out_shape=[jax.ShapeDtypeStruct(q.shape, F32), jax.ShapeDtypeStruct((2, s_len, ATT_HEADS * HEAD_DIM), F32)],
        grid=(ATT_HEADS, len(DILATIONS)),
        in_specs=[pl.BlockSpec((None,) + SLOPE_TILE, lambda h, g: (h, 0, 0)),
                  blk(q_col), blk(head), blk(lambda h, g: ATT_HEADS + h), blk(head), blk(head), blk(head)],
        out_specs=[blk(q_col), pl.BlockSpec((2, s_len, HEAD_DIM), lambda h, g: (0, 0, h))],
        scratch_shapes=[pltpu.VMEM((s_len, 1), F32)],
        args=[_slope_table(), q, kv, kv, o, lse, do], sem=("parallel", "arbitrary"))
    return outs if jobs is None else (outs, bufs)


def _adam(w, g, m, v):
    m = ADAM_B1 * m + (1.0 - ADAM_B1) * g
    v = ADAM_B2 * v + (1.0 - ADAM_B2) * (g * g)
    m_hat = m / (1.0 - ADAM_B1 ** ADAM_STEP)
    v_hat = v / (1.0 - ADAM_B2 ** ADAM_STEP)
    delta = -ADAM_LR * (m_hat / (jnp.sqrt(v_hat) + ADAM_EPS) + ADAM_WD * w)
    return delta, m, v


def adam_sharded(recvs, w, m, v, *, name):
    layers = len(recvs)
    n_src, r, c = recvs[0].shape
    tr = _rows(r, c)

    def body(*refs):
        p_refs = refs[:layers]
        w_ref, m_ref, v_ref, g_ref, d_ref, mo_ref, vo_ref = refs[layers:]
        for layer, p_ref in enumerate(p_refs):
            @pl.when(pl.program_id(0) == layer)
            def _():
                g = p_ref[0].astype(F32)
                for src in range(1, n_src):
                    g = g + p_ref[src].astype(F32)
                delta, m_new, v_new = _adam(w_ref[...], g, m_ref[...], v_ref[...])
                g_ref[...] = g
                d_ref[...] = delta
                mo_ref[...] = m_new
                vo_ref[...] = v_new

    blk = pl.BlockSpec((None, tr, c), lambda l, i: (l, i, 0))
    out = jax.ShapeDtypeStruct((layers, r, c), F32)
    part = [pl.BlockSpec((n_src, tr, c), functools.partial(lambda l, i, layer: (0, jnp.where(l == layer, i, 0), 0),
                                                            layer=layer)) for layer in range(layers)]
    return pl.pallas_call(
        body,
        name=name,
        out_shape=[out] * 4,
        grid=(layers, r // tr),
        in_specs=part + [blk, blk, blk],
        out_specs=[blk] * 4,
        compiler_params=_params("parallel", "parallel"),
    )(*recvs, w, m, v)


def sum_partials(parts):
    n_src, r, c = parts.shape

    def body(p_ref, o_ref):
        g = p_ref[0]
        for src in range(1, n_src):
            g = g + p_ref[src]
        o_ref[...] = g

    return pl.pallas_call(
        body,
        name="sum_small_grads",
        out_shape=jax.ShapeDtypeStruct((r, c), F32),
    )(parts)


def adam_packed(w, g, m, v):
    def body(w_ref, g_ref, m_ref, v_ref, d_ref, mo_ref, vo_ref):
        delta, m_new, v_new = _adam(w_ref[...], g_ref[...], m_ref[...], v_ref[...])
        d_ref[...] = delta
        mo_ref[...] = m_new
        vo_ref[...] = v_new

    out = jax.ShapeDtypeStruct(w.shape, F32)
    return pl.pallas_call(body, name="adam_small", out_shape=[out] * 3)(w, g, m, v)


def all_gather(srcs, *, name):
    n = len(srcs)

    def body(*refs):
        src, dst = refs[:n], refs[n:2 * n]
        send_sems, recv_sems, local_sems = refs[2 * n:]
        x, y, c, me = _place()
        sibling = (x, y, 1 - c)
        chips = [(1 - x, y), (x, 1 - y), (1 - x, 1 - y)]

        def index(px, py, pc):
            return 4 * px + 2 * py + pc

        def copy(p, k, block, to, from_src=False):
            slot = dst[p].at[index(*block)]
            return pltpu.make_async_remote_copy(
                src_ref=src[p] if from_src else slot, dst_ref=slot,
                send_sem=send_sems.at[p, k], recv_sem=recv_sems.at[p, k],
                device_id=to, device_id_type=MESH)

        mine = [pltpu.make_async_copy(src[p], dst[p].at[me], local_sems.at[p]) for p in range(n)]
        for cp in mine:
            cp.start()
        first = []
        for p in range(n):
            first.append(copy(p, 0, (x, y, c), sibling, from_src=True))
            for jj, chip in enumerate(chips):
                first.append(copy(p, 1 + jj, (x, y, c), (*chip, c), from_src=True))
        for cp in first:
            cp.start()
        passed = []
        for jj, chip in enumerate(chips):
            for p in range(n):
                copy(p, 1 + jj, (*chip, c), (x, y, c)).wait_recv()
                fwd = copy(p, 4 + jj, (*chip, c), sibling)
                fwd.start()
                passed.append(fwd)
        for p in range(n):
            copy(p, 0, sibling, (x, y, c)).wait_recv()
            for jj, chip in enumerate(chips):
                copy(p, 4 + jj, (*chip, 1 - c), (x, y, c)).wait_recv()
        for cp in first + passed:
            cp.wait_send()
        for cp in mine:
            cp.wait()

    return pl.pallas_call(
        body,
        name=name,
        out_shape=[jax.ShapeDtypeStruct((N_DEV,) + a.shape, a.dtype) for a in srcs],
        in_specs=[ANY] * n,
        out_specs=[ANY] * n,
        scratch_shapes=[pltpu.SemaphoreType.DMA((n, 7)), pltpu.SemaphoreType.DMA((n, 7)),
                        pltpu.SemaphoreType.DMA((n,))],
    )(*srcs)


def exchange_only(*, name, jobs):
    def body(o_ref):
        o_ref[...] = jnp.zeros_like(o_ref)

    _, bufs = _call(body, name=name, jobs=jobs, out_shape=[jax.ShapeDtypeStruct((8, LANE), F32)], grid=(1,),
                    in_specs=[], out_specs=[pl.BlockSpec((8, LANE), lambda i: (0, 0))], args=[], sem=("arbitrary",))
    return None, bufs


def _pack_rows(parts, rows):
    flat = jnp.concatenate([p.reshape(-1) for p in parts])
    return jnp.pad(flat, (0, rows * LANE - flat.shape[0])).reshape(rows, LANE)


def _unpack_rows(packed, shapes):
    flat = packed.reshape(-1)
    out, at = [], 0
    for sh in shapes:
        size = 1
        for dim in sh:
            size *= dim
        out.append(flat[at:at + size].reshape(sh))
        at += size
    return out


CONV_W_PAD = 768
SMALL_W_ROWS = 56


def _pack_small_weights(w_a2, b_a2, hn, conv_w):
    cw = jnp.pad(conv_w.reshape(6, -1), ((0, 0), (0, CONV_W_PAD - conv_w.shape[-1]))).reshape(-1, LANE)
    rows = jnp.concatenate([w_a2[0], b_a2, jnp.pad(hn, ((0, 0), (0, LANE - hn.shape[-1]))), cw], axis=0)
    return jnp.pad(rows, ((0, SMALL_W_ROWS - rows.shape[0]), (0, 0)))


def _unpack_small_weights(gathered):
    w_a2 = gathered[:, 0:GATE_RANK, :].transpose(1, 0, 2).reshape(GATE_RANK, GLA_KEY_DIM)
    b_a2 = gathered[:, GATE_RANK, :].reshape(1, GLA_KEY_DIM)
    hn = gathered[:, GATE_RANK + 1, :GLA_DV // N_DEV].reshape(1, GLA_DV)
    per = D_FF // N_DEV
    cw = gathered[:, GATE_RANK + 2:GATE_RANK + 2 + 6 * CONV_W_PAD // LANE, :].reshape(N_DEV, 6, CONV_W_PAD)[:, :, :per]
    cw = cw.reshape(N_DEV, 2, 3, per).transpose(1, 2, 0, 3).reshape(2, 3, D_FF)
    return w_a2, b_a2, hn, cw


SCHEDULE = {
    "gla_in": [("g1", "gout", None), ("g1", "up0", (0, 1024))],
    "gla_fwd": [("g2", "gout", None), ("g2", "up0", (0, 1024)), ("g1", "up0", (1024, 2048))],
    "gla_out": [("g2", "up0", (1024, 2048)), ("g1", "dn0", (0, 352))],
    "ffn_up0": [("g2", "dn0", (0, 352)), ("g1", "dn0", (352, 704)), ("g1", "kv", None), ("g1", "q", (0, 768))],
    "convglu_fwd0": [("g2", "dn0", (352, 704))],
    "ffn_down0": [("g2", "kv", None), ("g2", "q", (0, 768)), ("g1", "q", (768, 2048)), ("g1", "dout", None)],
    "kv_proj": [("g2", "q", (768, 2048)), ("g2", "dout", None), ("g1", "up1", (0, 704))],
    "q_proj": [("g2", "up1", (0, 704)), ("g1", "up1", (704, 1664))],
    "attn_fwd": [("g2", "up1", (704, 1664)), ("g1", "up1", (1664, 2048)), ("g1", "dn1", None)],
    "dsa_out": [("g2", "up1", (1664, 2048)), ("g2", "dn1", None)],
    "ffn_down_dx1": [("sc", "dn1", (0, 352))],
    "convglu_bwd1": [("sc", "dn1", (352, 704))],
    "ffn_up_dx1": [("sc", "up1", (0, 1024))],
    "attn_bwd": [("sc", "up1", (1024, 2048)), ("sc", "dout", None)],
    "q_proj_dx": [("sc", "q", (0, 1024))],
    "kv_proj_dw": [("sc", "q", (1024, 1792))],
    "kv_proj_dx": [("sc", "q", (1792, 2048)), ("sc", "kv", (0, 768))],
    "ffn_down_dw0": [("sc", "kv", (768, 2048))],
    "ffn_down_dx0": [("sc", "dn0", (0, 384))],
    "convglu_bwd0": [("sc", "dn0", (384, 704))],
    "ffn_up_dx0": [("sc", "up0", (0, 1024))],
    "gla_out_dw": [("sc", "up0", (1024, 1216))],
    "gla_out_dx": [("sc", "up0", (1216, 1408))],
    "gla_bwd": [("sc", "up0", (1408, 2048))],
    "gla_in_dw": [("sc", "gout", None)],
    "gla_in_dx": [("sc", "in", (0, 1536))],
    "grads_tail": [("sc", "in", (1536, 2048)), ("all", "small", None)],
}
ROW_SHARDED = ("gout", "dout", "dn0", "dn1")


class Plan:
    def __init__(self, weights, srcs=None):
        self.w = dict(weights)
        self.srcs = srcs
        self.grads = {}
        self.recv = {}
        self._names = None

    def weight(self, name):
        buf = self.w[name]
        if name in ROW_SHARDED:
            return buf.reshape(1, buf.shape[0] * buf.shape[1], buf.shape[2])
        return buf

    def jobs(self, call):
        ops = SCHEDULE.get(call)
        if self.srcs is None or not ops:
            return None
        jobs, handles = Jobs(), {}
        backward = ops[0][0] in ("sc", "all")
        for op, name, rows in ops:
            assert (op in ("sc", "all")) == backward
            store = self.recv if backward else self.w
            if name not in handles:
                if name in store:
                    handles[name] = jobs.thru(store[name])
                elif op == "sc":
                    handles[name] = jobs.new(self.grads[name].shape, BF16)
                elif op == "all":
                    handles[name] = jobs.new((N_DEV,) + self.grads[name].shape, self.grads[name].dtype)
                else:
                    handles[name] = jobs.new((N_DEV,) + self.srcs[name].shape, BF16)
            if op == "g1":
                jobs.gather_ici(self.srcs[name], handles[name], rows)
            elif op == "g2":
                jobs.gather_d2d(handles[name], rows)
            else:
                jobs.scatter(self.grads[name], handles[name], rows, same=op == "all")
        self._names = [(name, self.recv if backward else self.w) for name in handles]
        return jobs

    def run(self, call, fn, *args, **kwargs):
        jobs = self.jobs(call)
        if jobs is None:
            return fn(*args, **kwargs)
        out, bufs = fn(*args, jobs=jobs, **kwargs)
        for (name, store), buf in zip(self._names, bufs):
            store[name] = buf
        return out


def _ffn_fwd(plan, h, norm_g, conv_w, conv_b, tag):
    (n,) = rms_fwd(h, [norm_g], name=f"ffn_norm_fwd{tag}")
    up = plan.run(f"ffn_up{tag}", mm_nn, n, plan.weight(f"up{tag}"), out_dtype=BF16, name=f"ffn_up{tag}")
    act = plan.run(f"convglu_fwd{tag}", convglu_fwd, up, conv_w, conv_b, name=f"convglu_fwd{tag}")
    h_out = plan.run(f"ffn_down{tag}", mm_nn, act, plan.weight(f"dn{tag}"), out_dtype=F32, res=h,
                     name=f"ffn_down{tag}")
    return h_out, (n, up, act)


def _by_rows(dw):
    return dw.reshape(N_DEV, dw.shape[1] // N_DEV, dw.shape[2])


def _ffn_bwd(plan, dh_out, h, saved, norm_g, conv_w, conv_b, tag):
    n, up, act = saved
    plan.grads[f"dn{tag}"] = _by_rows(plan.run(f"ffn_down_dw{tag}", mm_tn, act, dh_out, 1, name=f"ffn_down_dw{tag}"))
    dact = plan.run(f"ffn_down_dx{tag}", mm_nt, dh_out, plan.weight(f"dn{tag}"), out_dtype=BF16,
                    name=f"ffn_down_dx{tag}")
    dup, dconv_w, dconv_b = plan.run(f"convglu_bwd{tag}", convglu_bwd, up, conv_w, conv_b, dact,
                                     name=f"convglu_bwd{tag}")
    plan.grads[f"up{tag}"] = mm_tn(n, dup, N_DEV, name=f"ffn_up_dw{tag}")
    dh, (dnorm,) = plan.run(f"ffn_up_dx{tag}", mm_nt, dup, plan.weight(f"up{tag}"), out_dtype=F32,
                            name=f"ffn_up_dx{tag}", norm=(h, dh_out, [norm_g], []))
    return dh, dnorm, dconv_w, dconv_b


def local_step(x, target, wts, plan):
    row = lambda v: v.reshape(1, -1)
    attn_norm, ffn_norm = wts["attn_norm"], wts["ffn_norm"]
    conv_w, conv_b = wts["ffn_conv_w"], wts["ffn_conv_b"]

    (n1,) = rms_fwd(x, [row(attn_norm[0])], name="attn_norm_fwd0")
    proj = plan.run("gla_in", mm_nn, n1, wts["gla_w_in"], out_dtype=F32, name="gla_in")
    la = gate_fwd(proj, wts["gla_w_a2"], wts["gla_b_a2"])
    o_gla, states, og = plan.run("gla_fwd", gla_fwd, proj, la, wts["gla_head_norm"])
    h1 = plan.run("gla_out", mm_nn, og, plan.weight("gout"), out_dtype=F32, res=x, name="gla_out")
    h2, ffn0 = _ffn_fwd(plan, h1, row(ffn_norm[0]), conv_w[0], row(conv_b[0]), "0")

    h2s = to_streams(h2, name="h2_to_streams")
    kvn, n3 = rms_fwd(h2s, [row(wts["kv_norm"]), row(attn_norm[1])], name="kv_attn_norm_fwd")
    kv = plan.run("kv_proj", mm_nn, kvn, plan.weight("kv"), out_dtype=BF16, name="kv_proj")
    q = plan.run("q_proj", mm_nn, n3, plan.weight("q"), out_dtype=BF16, name="q_proj")
    o_att, lse = plan.run("attn_fwd", attn_fwd, q, kv)
    h3 = from_streams(plan.run("dsa_out", mm_nn, o_att, plan.weight("dout"), out_dtype=F32, res=h2s, name="dsa_out"),
                      name="h3_from_streams")
    h4, ffn1 = _ffn_fwd(plan, h3, row(ffn_norm[1]), conv_w[1], row(conv_b[1]), "1")

    loss_tile, dh4, d_final = loss_head(h4, row(wts["final_norm"]), target)

    dh3, d_ffn1, dcw1, dcb1 = _ffn_bwd(plan, dh4, h3, ffn1, row(ffn_norm[1]), conv_w[1], row(conv_b[1]), "1")
    dh3s = to_streams(dh3, name="dh3_to_streams")
    plan.grads["dout"] = _by_rows(mm_tn(o_att, dh3s, 1, name="dsa_out_dw"))
    do_att = mm_nt(dh3s, plan.weight("dout"), out_dtype=F32, name="dsa_out_dx")
    dq, dkv = plan.run("attn_bwd", attn_bwd, q, kv, o_att, lse, do_att)
    plan.grads["q"] = mm_tn(n3, dq, N_DEV, name="q_proj_dw")
    dh2_part, (d_attn1,) = plan.run("q_proj_dx", mm_nt, dq, plan.weight("q"), out_dtype=F32, name="q_proj_dx",
                                    norm=(h2s, dh3s, [row(attn_norm[1])], []))
    plan.grads["kv"] = plan.run("kv_proj_dw", mm_tn, kvn, dkv, N_DEV, name="kv_proj_dw")
    dh2s, (d_kvnorm,) = plan.run("kv_proj_dx", mm_nt, dkv, plan.weight("kv"), out_dtype=F32, name="kv_proj_dx",
                                 norm=(h2s, dh2_part, [row(wts["kv_norm"])], []))
    dh2 = from_streams(dh2s, name="dh2_from_streams")
    dh1, d_ffn0, dcw0, dcb0 = _ffn_bwd(plan, dh2, h1, ffn0, row(ffn_norm[0]), conv_w[0], row(conv_b[0]), "0")
    plan.grads["gout"] = _by_rows(plan.run("gla_out_dw", mm_tn, og, dh1, 1, name="gla_out_dw"))
    dog = plan.run("gla_out_dx", mm_nt, dh1, plan.weight("gout"), out_dtype=F32, name="gla_out_dx")
    dq_g, dk_g, dv_g, dr, dla, d_hn = plan.run("gla_bwd", gla_bwd, proj, la, states, o_gla, wts["gla_head_norm"], dog)
    da, dw_a2p, db_a2 = gate_bwd(proj, wts["gla_w_a2"], wts["gla_b_a2"], dla)
    dproj = jnp.concatenate([dq_g, dk_g, dv_g, dr, da], axis=1)
    assert dproj.shape[1] == GLA_IN_PAD
    dw_in = plan.run("gla_in_dw", mm_tn, n1, dproj, 1, name="gla_in_dw")
    plan.grads["in"] = dw_in[0, :, :GLA_IN_DIM].reshape(D_MODEL, N_DEV, GLA_IN_DIM // N_DEV).transpose(1, 0, 2)
    grad_x, (d_attn0,) = plan.run("gla_in_dx", mm_nt, dproj, wts["gla_w_in"], out_dtype=F32, name="gla_in_dx",
                                  norm=(x, dh1, [row(attn_norm[0])], []))

    small = dict(
        attn_norm=jnp.concatenate([d_attn0, d_attn1], axis=0),
        ffn_norm=jnp.concatenate([d_ffn0, d_ffn1], axis=0),
        kv_norm=d_kvnorm.reshape(-1),
        final_norm=d_final.reshape(-1),
        ffn_conv_b=jnp.concatenate([dcb0, dcb1], axis=0),
        gla_w_a2=dw_a2p[:GATE_RANK],
        gla_b_a2=db_a2,
        gla_head_norm=d_hn,
        ffn_conv_w=jnp.stack([dcw0, dcw1]),
    )
    return loss_tile, grad_x, small


SMALL_ORDER = ("attn_norm", "ffn_norm", "kv_norm", "final_norm", "ffn_conv_b",
               "gla_w_a2", "gla_b_a2", "gla_head_norm", "ffn_conv_w")
SMALL_FULL = dict(attn_norm=(2, D_MODEL), ffn_norm=(2, D_MODEL), kv_norm=(D_MODEL,), final_norm=(D_MODEL,),
                  ffn_conv_b=(2, D_FF), gla_w_a2=(GATE_RANK, GLA_KEY_DIM), gla_b_a2=(1, GLA_KEY_DIM),
                  gla_head_norm=(1, GLA_DV), ffn_conv_w=(2, 3, D_FF))
SMALL_SHARDED = ("gla_w_a2", "gla_b_a2", "gla_head_norm", "ffn_conv_w")
SMALL_GRAD_ROWS = 592
SMALL_ADAM_ROWS = 240


def kernel(x, attn_norm, gla_w_in, gla_w_a2, gla_b_a2, gla_head_norm, gla_w_out, kv_norm, w_kv, dsa_w_q, dsa_w_out, ffn_norm, ffn_w_up, ffn_conv_w, ffn_conv_b, ffn_w_down, final_norm, loss_target, m_attn_norm, m_gla_w_in, m_gla_w_a2, m_gla_b_a2, m_gla_head_norm, m_gla_w_out, m_kv_norm, m_w_kv, m_dsa_w_q, m_dsa_w_out, m_ffn_norm, m_ffn_w_up, m_ffn_conv_w, m_ffn_conv_b, m_ffn_w_down, m_final_norm, v_attn_norm, v_gla_w_in, v_gla_w_a2, v_gla_b_a2, v_gla_head_norm, v_gla_w_out, v_kv_norm, v_w_kv, v_dsa_w_q, v_dsa_w_out, v_ffn_norm, v_ffn_w_up, v_ffn_conv_w, v_ffn_conv_b, v_ffn_w_down, v_final_norm):
    me = 4 * lax.axis_index("x") + 2 * lax.axis_index("y") + lax.axis_index("c")
    bf = lambda a: a.astype(BF16)

    g_in, g_small = all_gather([bf(gla_w_in[0]), _pack_small_weights(gla_w_a2, gla_b_a2, gla_head_norm, ffn_conv_w)],
                               name="gather_first")
    w_a2_full, b_a2_full, hn_full, conv_w_full = _unpack_small_weights(g_small)
    w_in_full = jnp.pad(g_in.transpose(1, 0, 2).reshape(D_MODEL, GLA_IN_DIM), ((0, 0), (0, GLA_IN_PAD - GLA_IN_DIM)))
    wts = dict(
        attn_norm=attn_norm, ffn_norm=ffn_norm, kv_norm=kv_norm, final_norm=final_norm, ffn_conv_b=ffn_conv_b,
        gla_w_in=w_in_full[None],
        gla_w_a2=jnp.pad(bf(w_a2_full), ((0, LANE - GATE_RANK), (0, 0))),
        gla_b_a2=b_a2_full, gla_head_norm=hn_full, ffn_conv_w=conv_w_full,
    )
    plan = Plan({}, srcs=dict(gout=bf(gla_w_out[0]), kv=bf(w_kv), q=bf(dsa_w_q[0]), dout=bf(dsa_w_out[0]),
                              up0=bf(ffn_w_up[0]), up1=bf(ffn_w_up[1]), dn0=bf(ffn_w_down[0]), dn1=bf(ffn_w_down[1])))

    loss_tile, grad_x, small = local_step(x[0], loss_target[0], wts, plan)
    loss = lax.psum(loss_tile[0, 0], ("x", "y", "c"))

    plan.grads["small"] = _pack_rows([small[nm] for nm in SMALL_ORDER], SMALL_GRAD_ROWS)
    plan.run("grads_tail", exchange_only, name="grads_tail")
    shard3 = lambda a: a.reshape((-1,) + a.shape[-2:])
    big_params = dict(gla_w_in=(("in",), gla_w_in, m_gla_w_in, v_gla_w_in),
                      gla_w_out=(("gout",), gla_w_out, m_gla_w_out, v_gla_w_out),
                      w_kv=(("kv",), w_kv, m_w_kv, v_w_kv),
                      dsa_w_q=(("q",), dsa_w_q, m_dsa_w_q, v_dsa_w_q),
                      dsa_w_out=(("dout",), dsa_w_out, m_dsa_w_out, v_dsa_w_out),
                      ffn_w_up=(("up0", "up1"), ffn_w_up, m_ffn_w_up, v_ffn_w_up),
                      ffn_w_down=(("dn0", "dn1"), ffn_w_down, m_ffn_w_down, v_ffn_w_down))
    res = {}
    for nm, (parts, w, m, v) in big_params.items():
        outs = adam_sharded([plan.recv[p] for p in parts], shard3(w), shard3(m), shard3(v), name=f"adam_{nm}")
        res[nm] = [o.reshape(w.shape) for o in outs]

    full = dict(zip(SMALL_ORDER, _unpack_rows(sum_partials(plan.recv["small"]),
                                              [SMALL_FULL[nm] for nm in SMALL_ORDER])))
    local_w = dict(attn_norm=attn_norm, ffn_norm=ffn_norm, kv_norm=kv_norm, final_norm=final_norm,
                   ffn_conv_b=ffn_conv_b, gla_w_a2=gla_w_a2, gla_b_a2=gla_b_a2, gla_head_norm=gla_head_norm,
                   ffn_conv_w=ffn_conv_w)
    local_m = dict(attn_norm=m_attn_norm, ffn_norm=m_ffn_norm, kv_norm=m_kv_norm, final_norm=m_final_norm,
                   ffn_conv_b=m_ffn_conv_b, gla_w_a2=m_gla_w_a2, gla_b_a2=m_gla_b_a2, gla_head_norm=m_gla_head_norm,
                   ffn_conv_w=m_ffn_conv_w)
    local_v = dict(attn_norm=v_attn_norm, ffn_norm=v_ffn_norm, kv_norm=v_kv_norm, final_norm=v_final_norm,
                   ffn_conv_b=v_ffn_conv_b, gla_w_a2=v_gla_w_a2, gla_b_a2=v_gla_b_a2, gla_head_norm=v_gla_head_norm,
                   ffn_conv_w=v_ffn_conv_w)
    local_g = {}
    for nm in SMALL_ORDER:
        gfull = full[nm]
        if nm in SMALL_SHARDED:
            per = gfull.shape[-1] // N_DEV
            gfull = lax.dynamic_slice_in_dim(gfull, me * per, per, axis=gfull.ndim - 1)
        local_g[nm] = gfull.reshape(local_w[nm].shape)
    shapes = [local_w[nm].shape for nm in SMALL_ORDER]
    pk = lambda dd: _pack_rows([dd[nm] for nm in SMALL_ORDER], SMALL_ADAM_ROWS)
    d_p, m_p, v_p = adam_packed(pk(local_w), pk(local_g), pk(local_m), pk(local_v))
    for nm, dl, mn, vn in zip(SMALL_ORDER, _unpack_rows(d_p, shapes), _unpack_rows(m_p, shapes),
                              _unpack_rows(v_p, shapes)):
        res[nm] = [local_g[nm], dl, mn, vn]

    order = ("attn_norm", "gla_w_in", "gla_w_a2", "gla_b_a2", "gla_head_norm", "gla_w_out", "kv_norm", "w_kv",
             "dsa_w_q", "dsa_w_out", "ffn_norm", "ffn_w_up", "ffn_conv_w", "ffn_conv_b", "ffn_w_down", "final_norm")
    outs = [loss, grad_x[None]]
    for kind in range(4):
        outs.extend(res[nm][kind] for nm in order)
    return tuple(outs)
```
